```python
import jax, jax.numpy as jnp
from jax import lax
import numpy as np

D_MODEL = 1024
BATCH = 16
SEQ = 256
DEPTH = 2
DEC_BATCH = 4
DEC_SEQ = 2048
PAST_LEN = 512

GRID_W = 64
N_EVEN = (DEPTH + 1) // 2
N_ODD = DEPTH // 2
A_GROUPS = 4
A_GROUP_DIM = 128
A_WIDTH = A_GROUPS * A_GROUP_DIM
B_HEADS = 8
B_KV_HEADS = 2
B_HEAD_DIM = 64
B_Q_WIDTH = B_HEADS * B_HEAD_DIM
B_KV_WIDTH = B_KV_HEADS * B_HEAD_DIM
WINDOW = 128
EVEN_IN = A_WIDTH + B_Q_WIDTH + 2 * B_KV_WIDTH
EVEN_OUT = A_WIDTH + B_Q_WIDTH
C_HEADS = 8
C_KV_HEADS = 2
C_HEAD_DIM = 128
C_Q_WIDTH = C_HEADS * C_HEAD_DIM
C_KV_WIDTH = C_KV_HEADS * C_HEAD_DIM
ODD_IN = C_Q_WIDTH + 2 * C_KV_WIDTH
ODD_OUT = C_Q_WIDTH
ATTN_BLOCK = 128
ROPE_THETA = 10000.0
N_EXPERTS = 32
TOP_K = 4
D_FF = 1024
SWIGLU_LIMIT = 7.0
SWIGLU_ALPHA = 1.702
MOE_BLOCK = 256
EPS = 1e-6

kernel_name = "hybrid_diffusion_prefix_trunk_step"


def rmsnorm(x, g):
    xf = x.astype(jnp.float32)
    y = xf * lax.rsqrt(jnp.mean(xf * xf, axis=-1, keepdims=True) + EPS)
    return (y * g.astype(jnp.float32)).astype(x.dtype)


def axial_rope_tables(n_tokens, head_dim, dtype):
    rows = n_tokens // GRID_W
    row = jnp.repeat(jnp.arange(rows), GRID_W).astype(jnp.float32)
    col = jnp.tile(jnp.arange(GRID_W), rows).astype(jnp.float32)
    quarter = head_dim // 4
    inv = ROPE_THETA ** (-jnp.arange(quarter, dtype=jnp.float32) / quarter)
    ang = jnp.stack([row[:, None] * inv, col[:, None] * inv], axis=1)
    return jnp.cos(ang).astype(dtype), jnp.sin(ang).astype(dtype)


def apply_axial_rope(x, cos, sin):
    quarter = x.shape[-1] // 4
    xr = x.reshape(x.shape[:-1] + (2, 2, quarter))
    bshape = (cos.shape[0],) + (1,) * (x.ndim - 3) + (2, quarter)
    cs = cos.reshape(bshape)
    sn = sin.reshape(bshape)
    x1 = xr[..., 0, :]
    x2 = xr[..., 1, :]
    out = jnp.stack([x1 * cs - x2 * sn, x2 * cs + x1 * sn], axis=-2)
    return out.reshape(x.shape)


def softmax_with_sink(s, sink):
    sb = sink.astype(jnp.float32)[..., None, None]
    m = jnp.maximum(jnp.max(s, axis=-1, keepdims=True), sb)
    e = jnp.exp(s - m)
    return e / (jnp.sum(e, axis=-1, keepdims=True) + jnp.exp(sb - m))


def blocked_attention(q, k, v, sink=None):
    b, lq, kvh, g, dh = q.shape
    nb = lq // ATTN_BLOCK
    qb = jnp.moveaxis(q.reshape(b, nb, ATTN_BLOCK, kvh, g, dh), 1, 0)
    scale = dh ** -0.5

    def one_block(qblk):
        s = jnp.einsum('bqkgd,bskd->bkgqs', qblk, k).astype(jnp.float32) * scale
        p = jax.nn.softmax(s, axis=-1) if sink is None else softmax_with_sink(s, sink)
        return jnp.einsum('bkgqs,bskd->bqkgd', p.astype(v.dtype), v)

    out = lax.map(one_block, qb)
    return jnp.moveaxis(out, 0, 1).reshape(b, lq, kvh, g, dh)


def banded_attention(q, k, v, ctx_k, ctx_v, sink):
    b, l, kvh, g, dh = q.shape
    nb = l // ATTN_BLOCK
    scale = dh ** -0.5

    def band(t):
        tp = jnp.pad(t, ((0, 0), (ATTN_BLOCK, ATTN_BLOCK), (0, 0), (0, 0)))
        tp = tp.reshape(b, nb + 2, ATTN_BLOCK, kvh, dh)
        return jnp.concatenate([tp[:, :-2], tp[:, 1:-1], tp[:, 2:]], axis=2)

    kb = band(k)
    vb = band(v)
    qb = q.reshape(b, nb, ATTN_BLOCK, kvh, g, dh)
    s_loc = jnp.einsum('bnqkgd,bnskd->bnkgqs', qb, kb).astype(jnp.float32) * scale
    qpos = jnp.arange(nb)[:, None] * ATTN_BLOCK + jnp.arange(ATTN_BLOCK)[None, :]
    kpos = jnp.arange(nb)[:, None] * ATTN_BLOCK - ATTN_BLOCK + jnp.arange(3 * ATTN_BLOCK)[None, :]
    kp = kpos[:, None, :]
    mask = (jnp.abs(kp - qpos[:, :, None]) <= WINDOW) & (kp >= 0) & (kp < l)
    s_loc = jnp.where(mask[None, :, None, None], s_loc, -jnp.inf)
    s_ctx = jnp.einsum('bnqkgd,bskd->bnkgqs', qb, ctx_k).astype(jnp.float32) * scale
    p = softmax_with_sink(jnp.concatenate([s_loc, s_ctx], axis=-1), sink)
    nloc = 3 * ATTN_BLOCK
    out = (jnp.einsum('bnkgqs,bnskd->bnqkgd', p[..., :nloc].astype(v.dtype), vb)
           + jnp.einsum('bnkgqs,bskd->bnqkgd', p[..., nloc:].astype(v.dtype), ctx_v))
    return out.reshape(b, l, kvh, g, dh)


def fourier_mix(a):
    b, l, _ = a.shape
    ag = a.reshape(b, l, A_GROUPS, A_GROUP_DIM).astype(jnp.float32)
    f = jnp.fft.fft2(ag, axes=(1, 3), norm='ortho').real
    return f.astype(a.dtype).reshape(b, l, A_WIDTH)


def split_even(proj):
    b, l, _ = proj.shape
    a = proj[..., :A_WIDTH]
    q = proj[..., A_WIDTH:A_WIDTH + B_Q_WIDTH].reshape(b, l, B_KV_HEADS, B_HEADS // B_KV_HEADS, B_HEAD_DIM)
    o = A_WIDTH + B_Q_WIDTH
    k = proj[..., o:o + B_KV_WIDTH].reshape(b, l, B_KV_HEADS, B_HEAD_DIM)
    v = proj[..., o + B_KV_WIDTH:].reshape(b, l, B_KV_HEADS, B_HEAD_DIM)
    return a, q, k, v


def split_odd(proj, q_norm, k_norm):
    b, l, _ = proj.shape
    q = proj[..., :C_Q_WIDTH].reshape(b, l, C_KV_HEADS, C_HEADS // C_KV_HEADS, C_HEAD_DIM)
    k = proj[..., C_Q_WIDTH:C_Q_WIDTH + C_KV_WIDTH].reshape(b, l, C_KV_HEADS, C_HEAD_DIM)
    v = proj[..., C_Q_WIDTH + C_KV_WIDTH:].reshape(b, l, C_KV_HEADS, C_HEAD_DIM)
    return rmsnorm(q, q_norm), rmsnorm(k, k_norm), v


def moe_ffn(h, router_w, router_b, w_gu, b_gu, w_down, b_down):
    shp = h.shape
    xt = h.reshape(-1, shp[-1])
    t = xt.shape[0]
    logits = (xt @ router_w).astype(jnp.float32) + router_b.astype(jnp.float32)
    top_val, top_idx = lax.top_k(logits, TOP_K)
    gates = jax.nn.softmax(top_val, axis=-1)
    tk = t * TOP_K
    flat_e = top_idx.reshape(-1)
    order = jnp.argsort(flat_e)
    sorted_e = flat_e[order]
    counts = jnp.bincount(flat_e, length=N_EXPERTS)
    padded = ((counts + MOE_BLOCK - 1) // MOE_BLOCK) * MOE_BLOCK
    pad_end = jnp.cumsum(padded)
    pad_start = pad_end - padded
    start = jnp.cumsum(counts) - counts
    dest = pad_start[sorted_e] + jnp.arange(tk) - start[sorted_e]
    n_blocks = -(-(tk + N_EXPERTS * (MOE_BLOCK - 1)) // MOE_BLOCK)
    slot_tok = jnp.full((n_blocks * MOE_BLOCK,), t, jnp.int32).at[dest].set((order // TOP_K).astype(jnp.int32))
    block_e = jnp.minimum(jnp.searchsorted(pad_end, jnp.arange(n_blocks) * MOE_BLOCK, side='right'), N_EXPERTS - 1)
    x_pad = jnp.concatenate([xt, jnp.zeros((1, xt.shape[1]), xt.dtype)], axis=0)
    xb = x_pad[slot_tok].reshape(n_blocks, MOE_BLOCK, xt.shape[1])

    def expert_block(args):
        xblk, e = args
        gu = xblk @ w_gu[e] + b_gu[e]
        gate = jnp.minimum(gu[..., :D_FF], SWIGLU_LIMIT)
        up = jnp.clip(gu[..., D_FF:], -SWIGLU_LIMIT, SWIGLU_LIMIT)
        act = (up + 1.0) * (gate * jax.nn.sigmoid(SWIGLU_ALPHA * gate))
        return act @ w_down[e] + b_down[e]

    yb = lax.map(expert_block, (xb, block_e)).reshape(-1, xt.shape[1])
    y_assign = jnp.zeros((tk, xt.shape[1]), yb.dtype).at[order].set(yb[dest])
    y = jnp.einsum('tkd,tk->td', y_assign.reshape(t, TOP_K, -1), gates.astype(yb.dtype))
    return y.reshape(shp)


def setup_inputs(seed: int = 0) -> dict:
    key = jax.random.key(seed)
    ks = jax.random.split(key, 32)
    f32 = jnp.float32
    nrm = lambda k, shape, s: jax.random.normal(k, shape, f32) * s
    D = D_MODEL
    return {
        'x_prompt': nrm(ks[0], (BATCH, SEQ, D), 1.0),
        'x_sample': nrm(ks[1], (DEC_BATCH, DEC_SEQ, D), 1.0),
        'cache_b_k': nrm(ks[2], (DEC_BATCH, N_EVEN, PAST_LEN, B_KV_HEADS, B_HEAD_DIM), 1.0),
        'cache_b_v': nrm(ks[3], (DEC_BATCH, N_EVEN, PAST_LEN, B_KV_HEADS, B_HEAD_DIM), 1.0),
        'cache_c_k': nrm(ks[4], (DEC_BATCH, N_ODD, PAST_LEN, C_KV_HEADS, C_HEAD_DIM), 1.0),
        'cache_c_v': nrm(ks[5], (DEC_BATCH, N_ODD, PAST_LEN, C_KV_HEADS, C_HEAD_DIM), 1.0),
        'c': nrm(ks[6], (DEC_BATCH, D), 1.0),
        'c_ctx': nrm(ks[7], (D,), 1.0),
        'mod_w': nrm(ks[8], (DEPTH, D, 6 * D), D ** -0.5),
        'mod_b': nrm(ks[9], (DEPTH, 6 * D), 0.01),
        'norm_mix': 1.0 + nrm(ks[10], (DEPTH, D), 0.01),
        'norm_ffn': 1.0 + nrm(ks[11], (DEPTH, D), 0.01),
        'even_w_in': nrm(ks[12], (N_EVEN, D, EVEN_IN), D ** -0.5),
        'even_w_out': nrm(ks[13], (N_EVEN, EVEN_OUT, D), EVEN_OUT ** -0.5),
        'even_sink': nrm(ks[14], (N_EVEN, B_HEADS), 1.0),
        'odd_w_in': nrm(ks[15], (N_ODD, D, ODD_IN), D ** -0.5),
        'odd_w_out': nrm(ks[16], (N_ODD, ODD_OUT, D), ODD_OUT ** -0.5),
        'odd_q_norm': 1.0 + nrm(ks[17], (N_ODD, C_HEAD_DIM), 0.01),
        'odd_k_norm': 1.0 + nrm(ks[18], (N_ODD, C_HEAD_DIM), 0.01),
        'router_w': nrm(ks[19], (DEPTH, D, N_EXPERTS), D ** -0.5),
        'router_b': nrm(ks[20], (DEPTH, N_EXPERTS), 0.01),
        'moe_w_gu': nrm(ks[21], (DEPTH, N_EXPERTS, D, 2 * D_FF), D ** -0.5),
        'moe_b_gu': nrm(ks[22], (DEPTH, N_EXPERTS, 2 * D_FF), 0.01),
        'moe_w_down': nrm(ks[23], (DEPTH, N_EXPERTS, D_FF, D), D_FF ** -0.5),
        'moe_b_down': nrm(ks[24], (DEPTH, N_EXPERTS, D), 0.01),
        'final_norm': 1.0 + nrm(ks[25], (D,), 0.01),
    }


def reference(x_prompt, x_sample, cache_b_k, cache_b_v, cache_c_k, cache_c_v, c, c_ctx,
              mod_w, mod_b, norm_mix, norm_ffn, even_w_in, even_w_out, even_sink,
              odd_w_in, odd_w_out, odd_q_norm, odd_k_norm, router_w, router_b,
              moe_w_gu, moe_b_gu, moe_w_down, moe_b_down, final_norm):
    xp = x_prompt
    xs = x_sample
    bp, lp, _ = xp.shape
    bs, ls, _ = xs.shape
    cos_b, sin_b = axial_rope_tables(ls, B_HEAD_DIM, xs.dtype)
    cos_c, sin_c = axial_rope_tables(ls, C_HEAD_DIM, xs.dtype)
    new_b_k, new_b_v, new_c_k, new_c_v = [], [], [], []
    for layer in range(DEPTH):
        j = layer // 2
        mod_p = jnp.split(jax.nn.silu(c_ctx) @ mod_w[layer] + mod_b[layer], 6, axis=-1)
        mod_s = jnp.split((jax.nn.silu(c) @ mod_w[layer] + mod_b[layer])[:, None, :], 6, axis=-1)
        hp = rmsnorm(xp, norm_mix[layer]) * (1.0 + mod_p[1]) + mod_p[0]
        hs = rmsnorm(xs, norm_mix[layer]) * (1.0 + mod_s[1]) + mod_s[0]
        if layer % 2 == 0:
            a_p, q_p, k_p, v_p = split_even(hp @ even_w_in[j])
            sink = even_sink[j].reshape(B_KV_HEADS, B_HEADS // B_KV_HEADS)
            b_p = blocked_attention(q_p, k_p, v_p, sink).reshape(bp, lp, B_Q_WIDTH)
            mix_p = jnp.concatenate([fourier_mix(a_p), b_p], axis=-1) @ even_w_out[j]
            new_b_k.append(k_p)
            new_b_v.append(v_p)
            a_s, q_s, k_s, v_s = split_even(hs @ even_w_in[j])
            q_s = apply_axial_rope(q_s, cos_b, sin_b)
            k_s = apply_axial_rope(k_s, cos_b, sin_b)
            b_s = banded_attention(q_s, k_s, v_s, cache_b_k[:, j], cache_b_v[:, j], sink).reshape(bs, ls, B_Q_WIDTH)
            mix_s = jnp.concatenate([fourier_mix(a_s), b_s], axis=-1) @ even_w_out[j]
        else:
            q_p, k_p, v_p = split_odd(hp @ odd_w_in[j], odd_q_norm[j], odd_k_norm[j])
            mix_p = blocked_attention(q_p, k_p, v_p).reshape(bp, lp, C_Q_WIDTH) @ odd_w_out[j]
            new_c_k.append(k_p)
            new_c_v.append(v_p)
            q_s, k_s, v_s = split_odd(hs @ odd_w_in[j], odd_q_norm[j], odd_k_norm[j])
            q_s = apply_axial_rope(q_s, cos_c, sin_c)
            k_s = apply_axial_rope(k_s, cos_c, sin_c)
            k_all = jnp.concatenate([k_s, cache_c_k[:, j]], axis=1)
            v_all = jnp.concatenate([v_s, cache_c_v[:, j]], axis=1)
            mix_s = blocked_attention(q_s, k_all, v_all).reshape(bs, ls, C_Q_WIDTH) @ odd_w_out[j]
        xp = xp + mod_p[2] * mix_p
        xs = xs + mod_s[2] * mix_s
        hp = rmsnorm(xp, norm_ffn[layer]) * (1.0 + mod_p[4]) + mod_p[3]
        hs = rmsnorm(xs, norm_ffn[layer]) * (1.0 + mod_s[4]) + mod_s[3]
        xp = xp + mod_p[5] * moe_ffn(hp, router_w[layer], router_b[layer], moe_w_gu[layer], moe_b_gu[layer], moe_w_down[layer], moe_b_down[layer])
        xs = xs + mod_s[5] * moe_ffn(hs, router_w[layer], router_b[layer], moe_w_gu[layer], moe_b_gu[layer], moe_w_down[layer], moe_b_down[layer])
    y_prompt = rmsnorm(xp, final_norm)
    y_sample = rmsnorm(xs, final_norm)
    state_b_k = jnp.stack(new_b_k, axis=1)
    state_b_v = jnp.stack(new_b_v, axis=1)
    state_c_k = jnp.stack(new_c_k, axis=1)
    state_c_v = jnp.stack(new_c_v, axis=1)
    return (y_prompt, y_sample, state_b_k, state_b_v, state_c_k, state_c_v)
```

```python
import functools

import numpy as np
import jax
import jax.numpy as jnp
from jax import lax
from jax.experimental import pallas as pl
from jax.experimental.pallas import tpu as pltpu

F32 = jnp.float32
BF16 = jnp.bfloat16
I32 = jnp.int32

GRID_W = 64
A_GROUPS = 4
A_GROUP_DIM = 128
A_WIDTH = A_GROUPS * A_GROUP_DIM
B_HEADS = 8
B_KV_HEADS = 2
B_HEAD_DIM = 64
B_Q_WIDTH = B_HEADS * B_HEAD_DIM
B_KV_WIDTH = B_KV_HEADS * B_HEAD_DIM
WINDOW = 128
C_HEADS = 8
C_KV_HEADS = 2
C_HEAD_DIM = 128
C_Q_WIDTH = C_HEADS * C_HEAD_DIM
C_KV_WIDTH = C_KV_HEADS * C_HEAD_DIM
ROPE_THETA = 10000.0
N_EXPERTS = 32
TOP_K = 4
SWIGLU_LIMIT = 7.0
SWIGLU_ALPHA = 1.702
EPS = 1e-6

LANES = 128
TOKEN_TILE = 256
EXPERT_TILE = 256
ATTN_Q_TILE = 128
VMEM_LIMIT = 56 * 1024 * 1024
MASKED = -1e30
N_MOD = 6
MOD_ROWS = 8


def _cparams(*sem):
    return pltpu.CompilerParams(dimension_semantics=tuple(sem), vmem_limit_bytes=VMEM_LIMIT)


def _dot(a, b):
    return jnp.dot(a, b, preferred_element_type=F32)


def _dot_nt(a, b):
    return lax.dot_general(a, b, (((1,), (1,)), ((), ())), preferred_element_type=F32)


def _rms(x, g):
    return x * lax.rsqrt(jnp.mean(x * x, axis=-1, keepdims=True) + EPS) * g


def _dft_tables(n):
    j = np.arange(n, dtype=np.int64)
    ang = 2.0 * np.pi * ((j[:, None] * j[None, :]) % n).astype(np.float64) / n
    s = 1.0 / np.sqrt(n)
    return np.cos(ang) * s, np.sin(ang) * s


def _rope_tables(n_prompt_rows, n_latent, head_dim):
    quarter = head_dim // 4
    pos = np.arange(n_latent)
    row = (pos // GRID_W).astype(np.float32)
    col = (pos % GRID_W).astype(np.float32)
    inv = (np.float32(ROPE_THETA) ** (-np.arange(quarter, dtype=np.float32) / np.float32(quarter))).astype(np.float32)
    ang_row = (row[:, None] * inv[None, :]).astype(np.float32)
    ang_col = (col[:, None] * inv[None, :]).astype(np.float32)
    cos_h = np.concatenate([np.cos(ang_row)] * 2 + [np.cos(ang_col)] * 2, axis=1)
    sin_h = np.concatenate([-np.sin(ang_row), np.sin(ang_row), -np.sin(ang_col), np.sin(ang_col)], axis=1)
    reps = LANES // head_dim
    cos_l = np.tile(cos_h, (1, reps)).astype(np.float32)
    sin_l = np.tile(sin_h, (1, reps)).astype(np.float32)
    cos = np.concatenate([np.ones((n_prompt_rows, LANES), np.float32), cos_l], axis=0)
    sin = np.concatenate([np.zeros((n_prompt_rows, LANES), np.float32), sin_l], axis=0)
    return cos, sin


def _rope(x, cos, sin, quarter):
    lane = lax.broadcasted_iota(I32, (x.shape[0], LANES), 1)
    first = ((lane // quarter) % 2) == 0
    outs = []
    for c in range(x.shape[1] // LANES):
        xc = x[:, c * LANES:(c + 1) * LANES]
        partner = jnp.where(first, pltpu.roll(xc, LANES - quarter, 1), pltpu.roll(xc, quarter, 1))
        outs.append(xc * cos + partner * sin)
    return outs[0] if len(outs) == 1 else jnp.concatenate(outs, axis=1)


def _head_rms(x, g):
    outs = []
    for c in range(x.shape[1] // LANES):
        outs.append(_rms(x[:, c * LANES:(c + 1) * LANES], g))
    return outs[0] if len(outs) == 1 else jnp.concatenate(outs, axis=1)


def _mod_kernel(c_ref, w_ref, b_ref, o_ref):
    c = c_ref[...]
    s = c * (1.0 / (1.0 + jnp.exp(-c)))
    o_ref[0] = _dot(s.astype(BF16), w_ref[0].astype(BF16)) + b_ref[0]


def _modulation(cond, mod_w, mod_b):
    depth, d, n = mod_w.shape
    tn = 1536
    return pl.pallas_call(
        _mod_kernel,
        out_shape=jax.ShapeDtypeStruct((depth, MOD_ROWS, n), F32),
        grid=(depth, n // tn),
        in_specs=[
            pl.BlockSpec((MOD_ROWS, d), lambda l, j: (0, 0)),
            pl.BlockSpec((1, d, tn), lambda l, j: (l, 0, j)),
            pl.BlockSpec((1, 1, tn), lambda l, j: (l, 0, j)),
        ],
        out_specs=pl.BlockSpec((1, MOD_ROWS, tn), lambda l, j: (l, 0, j)),
        compiler_params=_cparams("arbitrary", "arbitrary"),
        name="modulation",
    )(cond, mod_w, mod_b.reshape(depth, 1, n))


class _Geom:
    def __init__(self, bp, lp, bs, ls):
        self.bp, self.lp, self.bs, self.ls = bp, lp, bs, ls
        self.tp = bp * lp
        self.t = bp * lp + bs * ls
        assert lp == TOKEN_TILE and ls % TOKEN_TILE == 0 and self.tp % ls == 0
        self.n_ptiles = self.tp // TOKEN_TILE
        self.tiles_per_lat = ls // TOKEN_TILE
        self.n_tiles = self.t // TOKEN_TILE

    def group(self, i):
        return jnp.where(i < self.n_ptiles, 0, 1 + (i - self.n_ptiles) // self.tiles_per_lat)

    def pos_block(self, i):
        return jnp.where(i < self.n_ptiles, 0, 1 + (i - self.n_ptiles) % self.tiles_per_lat)


def _mod_spec(geom, layer, which, d):
    def imap(i):
        return ((layer * MOD_ROWS + geom.group(i)) * N_MOD + which, 0, 0)
    return pl.BlockSpec((None, 1, d), imap)


def _proj_even_kernel(x_ref, g_ref, sh_ref, sc_ref, w_ref, cos_ref, sin_ref, dft_ref,
                      tc_ref, ts_ref, q_ref, k_ref, v_ref):
    h = _rms(x_ref[...], g_ref[...]) * (1.0 + sc_ref[...]) + sh_ref[...]
    p = _dot(h.astype(BF16), w_ref[...])
    cos = cos_ref[...]
    sin = sin_ref[...]
    dft = dft_ref[...]
    tcs, tss = [], []
    for g in range(A_GROUPS):
        t = _dot(p[:, g * A_GROUP_DIM:(g + 1) * A_GROUP_DIM].astype(BF16), dft)
        tcs.append(t[:, :A_GROUP_DIM])
        tss.append(t[:, A_GROUP_DIM:])
    tc_ref[...] = jnp.concatenate(tcs, axis=1).astype(BF16)
    ts_ref[...] = jnp.concatenate(tss, axis=1).astype(BF16)
    o = A_WIDTH
    q_ref[...] = _rope(p[:, o:o + B_Q_WIDTH], cos, sin, B_HEAD_DIM // 4).astype(BF16)
    o += B_Q_WIDTH
    k_ref[...] = _rope(p[:, o:o + B_KV_WIDTH], cos, sin, B_HEAD_DIM // 4)
    o += B_KV_WIDTH
    v_ref[...] = p[:, o:o + B_KV_WIDTH]


def _proj_odd_kernel(x_ref, g_ref, sh_ref, sc_ref, w_ref, cos_ref, sin_ref, qn_ref, kn_ref,
                     q_ref, k_ref, v_ref):
    h = _rms(x_ref[...], g_ref[...]) * (1.0 + sc_ref[...]) + sh_ref[...]
    p = _dot(h.astype(BF16), w_ref[...])
    cos = cos_ref[...]
    sin = sin_ref[...]
    q = _head_rms(p[:, :C_Q_WIDTH], qn_ref[...])
    k = _head_rms(p[:, C_Q_WIDTH:C_Q_WIDTH + C_KV_WIDTH], kn_ref[...])
    q_ref[...] = _rope(q, cos, sin, C_HEAD_DIM // 4).astype(BF16)
    k_ref[...] = _rope(k, cos, sin, C_HEAD_DIM // 4)
    v_ref[...] = p[:, C_Q_WIDTH + C_KV_WIDTH:]


def _in_projection(geom, layer, x, norm_g, mods, w, cos, sin, extras, kernel, out_widths, out_dtypes, name):
    t, d = x.shape
    tm = TOKEN_TILE
    n_out = w.shape[1]
    row = lambda i: (i, 0)
    const2 = lambda i: (0, 0)
    in_specs = [
        pl.BlockSpec((tm, d), row),
        pl.BlockSpec((1, d), const2),
        _mod_spec(geom, layer, 0, d),
        _mod_spec(geom, layer, 1, d),
        pl.BlockSpec((d, n_out), const2),
        pl.BlockSpec((tm, LANES), lambda i: (geom.pos_block(i), 0)),
        pl.BlockSpec((tm, LANES), lambda i: (geom.pos_block(i), 0)),
    ] + [pl.BlockSpec(e.shape, const2) for e in extras]
    return pl.pallas_call(
        kernel,
        out_shape=[jax.ShapeDtypeStruct((t, wd), dt) for wd, dt in zip(out_widths, out_dtypes)],
        grid=(geom.n_tiles,),
        in_specs=in_specs,
        out_specs=[pl.BlockSpec((tm, wd), row) for wd in out_widths],
        compiler_params=_cparams("arbitrary"),
        name=name,
    )(x, norm_g, mods, mods, w, cos, sin, *extras)


def _fourier_kernel(cl_ref, sl_ref, tc_ref, ts_ref, *rest):
    o_ref = rest[-1]
    o_ref[...] = (_dot(cl_ref[...], tc_ref[...]) - _dot(sl_ref[...], ts_ref[...])).astype(o_ref.dtype)


def _fourier_tokens(tc, ts, cl, sl, n_seq, seq_len, row0, prev):
    t, width = tc.shape
    tr = min(seq_len, 512)
    n_r = seq_len // tr
    assert row0 % seq_len == 0
    seq0 = row0 // seq_len
    out0 = row0 // tr
    in_specs = [
        pl.BlockSpec((tr, seq_len), lambda s, r: (r, 0)),
        pl.BlockSpec((tr, seq_len), lambda s, r: (r, 0)),
        pl.BlockSpec((seq_len, width), lambda s, r: (seq0 + s, 0)),
        pl.BlockSpec((seq_len, width), lambda s, r: (seq0 + s, 0)),
    ]
    args = [cl, sl, tc, ts]
    aliases = {}
    if prev is not None:
        in_specs.append(pl.BlockSpec(memory_space=pl.ANY))
        args.append(prev)
        aliases = {4: 0}
    return pl.pallas_call(
        _fourier_kernel,
        out_shape=jax.ShapeDtypeStruct((t, width), BF16),
        grid=(n_seq, n_r),
        in_specs=in_specs,
        out_specs=pl.BlockSpec((tr, width), lambda s, r: (out0 + s * n_r + r, 0)),
        input_output_aliases=aliases,
        compiler_params=_cparams("arbitrary", "arbitrary"),
        name="fourier_tokens",
    )(*args)


def _softmax_pv(q, parts, sink, scale):
    scores = []
    m = sink
    for k, _, mask in parts:
        s = _dot_nt(q, k) * scale
        if mask is not None:
            s = jnp.where(mask, s, MASKED)
        scores.append(s)
        mx = jnp.max(s, axis=-1, keepdims=True)
        m = mx if m is None else jnp.maximum(m, mx)
    den = None if sink is None else jnp.exp(sink - m)
    acc = None
    for (_, v, _), s in zip(parts, scores):
        e = jnp.exp(s - m)
        es = jnp.sum(e, axis=-1, keepdims=True)
        den = es if den is None else den + es
        o = _dot(e.astype(BF16), v)
        acc = o if acc is None else acc + o
    return acc / den


def _stack_heads(q, kv, groups, dh):
    return jnp.concatenate([q[:, (kv * groups + g) * dh:(kv * groups + g + 1) * dh] for g in range(groups)], axis=0)


def _sink_column(sink_ref, kv, groups, rows):
    return jnp.concatenate([jnp.full((rows, 1), sink_ref[kv * groups + g], F32) for g in range(groups)], axis=0)


def _attn_kernel(*refs, kv_heads, groups, dh, has_sink, has_ctx, window, q_tile, seq_len):
    refs = list(refs)
    sink_ref = refs.pop(0) if has_sink else None
    q_ref, k_ref, v_ref = refs[:3]
    ck_ref, cv_ref = (refs[3], refs[4]) if has_ctx else (None, None)
    o_ref = refs[-1]
    scale = dh ** -0.5
    q = q_ref[...]
    rows = q.shape[0]
    if window is None:
        k_loc = k_ref[...].astype(BF16)
        v_loc = v_ref[...].astype(BF16)
        mask = None
    else:
        n = pl.program_id(1)
        band = q_tile + 2 * window
        start = jnp.clip(n * q_tile - window, 0, seq_len - band)
        start = pl.multiple_of(start, LANES)
        k_loc = k_ref[pl.ds(start, band), :].astype(BF16)
        v_loc = v_ref[pl.ds(start, band), :].astype(BF16)
        qpos = n * q_tile + lax.broadcasted_iota(I32, (groups * rows, band), 0) % rows
        kpos = start + lax.broadcasted_iota(I32, (groups * rows, band), 1)
        mask = jnp.abs(kpos - qpos) <= window
    if has_ctx:
        k_ctx = ck_ref[...].astype(BF16)
        v_ctx = cv_ref[...].astype(BF16)
    outs = []
    for kv in range(kv_heads):
        sl = slice(kv * dh, (kv + 1) * dh)
        parts = [(k_loc[:, sl], v_loc[:, sl], mask)]
        if has_ctx:
            parts.append((k_ctx[:, sl], v_ctx[:, sl], None))
        sink = _sink_column(sink_ref, kv, groups, rows) if has_sink else None
        o = _softmax_pv(_stack_heads(q, kv, groups, dh), parts, sink, scale)
        outs.extend(o[g * rows:(g + 1) * rows] for g in range(groups))
    o_ref[...] = jnp.concatenate(outs, axis=1).astype(o_ref.dtype)


def _attention(q, k, v, ctx, sink, prev, *, n_seq, seq_len, row0, q_tile, kv_heads, groups, dh, window):
    t, qw = q.shape
    kw = k.shape[1]
    n_q = seq_len // q_tile
    assert row0 % seq_len == 0 and row0 % q_tile == 0
    seq0 = row0 // seq_len
    q0 = row0 // q_tile
    in_specs, args = [], []
    if sink is not None:
        in_specs.append(pl.BlockSpec(memory_space=pltpu.SMEM))
        args.append(sink)
    in_specs += [
        pl.BlockSpec((q_tile, qw), lambda s, n: (q0 + s * n_q + n, 0)),
        pl.BlockSpec((seq_len, kw), lambda s, n: (seq0 + s, 0)),
        pl.BlockSpec((seq_len, kw), lambda s, n: (seq0 + s, 0)),
    ]
    args += [q, k, v]
    if ctx is not None:
        p = ctx[0].shape[1]
        in_specs += [pl.BlockSpec((None, p, kw), lambda s, n: (s, 0, 0))] * 2
        args += list(ctx)
    aliases = {}
    if prev is not None:
        in_specs.append(pl.BlockSpec(memory_space=pl.ANY))
        aliases = {len(args): 0}
        args.append(prev)
    kern = functools.partial(
        _attn_kernel, kv_heads=kv_heads, groups=groups, dh=dh, has_sink=sink is not None,
        has_ctx=ctx is not None, window=window, q_tile=q_tile, seq_len=seq_len)
    return pl.pallas_call(
        kern,
        out_shape=jax.ShapeDtypeStruct((t, qw), BF16),
        grid=(n_seq, n_q),
        in_specs=in_specs,
        out_specs=pl.BlockSpec((q_tile, qw), lambda s, n: (q0 + s * n_q + n, 0)),
        input_output_aliases=aliases,
        compiler_params=_cparams("arbitrary", "arbitrary"),
        name="attention",
    )(*args)


def _out_route_kernel(*refs, n_mix):
    mix_refs = refs[:n_mix]
    w_refs = refs[n_mix:2 * n_mix]
    (x_ref, gate_ref, g2_ref, sh2_ref, sc2_ref, rw_ref, rb_ref,
     xo_ref, h_ref, idx_ref, rank_ref, gates_ref, cnt_ref, carry_ref) = refs[2 * n_mix:]
    i = pl.program_id(0)

    @pl.when(i == 0)
    def _():
        carry_ref[...] = jnp.zeros_like(carry_ref)

    acc = None
    for m_ref, w_ref in zip(mix_refs, w_refs):
        part = _dot(m_ref[...], w_ref[...])
        acc = part if acc is None else acc + part
    xn = x_ref[...] + gate_ref[...] * acc
    xo_ref[...] = xn
    h = _rms(xn, g2_ref[...]) * (1.0 + sc2_ref[...]) + sh2_ref[...]
    h_ref[...] = h

    logits = _dot(h.astype(BF16), rw_ref[...]) + rb_ref[...]
    tm, ne = logits.shape
    lane = lax.broadcasted_iota(I32, (tm, ne), 1).astype(F32)
    lane4 = lax.broadcasted_iota(I32, (tm, TOP_K), 1)
    work = logits
    sels, vals, ids = [], [], []
    for _ in range(TOP_K):
        mx = jnp.max(work, axis=-1, keepdims=True)
        first = jnp.min(jnp.where(work == mx, lane, float(ne)), axis=-1, keepdims=True)
        sel = lane == first
        work = jnp.where(sel, -jnp.inf, work)
        sels.append(sel)
        vals.append(mx)
        ids.append(first)
    exps = [jnp.exp(v - vals[0]) for v in vals]
    den = exps[0] + exps[1] + exps[2] + exps[3]

    onehot = jnp.zeros((tm, ne), F32)
    for sel in sels:
        onehot = onehot + sel.astype(F32)
    r_i = lax.broadcasted_iota(I32, (tm, tm), 0)
    c_i = lax.broadcasted_iota(I32, (tm, tm), 1)
    before = jnp.where(c_i < r_i, 1.0, 0.0).astype(BF16)
    base = carry_ref[...] + _dot(before, onehot.astype(BF16))

    idx_o = jnp.zeros((tm, TOP_K), I32)
    rank_o = jnp.zeros((tm, TOP_K), I32)
    gate_o = jnp.zeros((tm, TOP_K), F32)
    for k in range(TOP_K):
        rank_k = jnp.sum(jnp.where(sels[k], base, 0.0), axis=-1, keepdims=True)
        idx_o = jnp.where(lane4 == k, ids[k].astype(I32), idx_o)
        rank_o = jnp.where(lane4 == k, rank_k.astype(I32), rank_o)
        gate_o = jnp.where(lane4 == k, exps[k] / den, gate_o)
    idx_ref[...] = idx_o
    rank_ref[...] = rank_o
    gates_ref[...] = gate_o
    carry_ref[...] = carry_ref[...] + jnp.sum(onehot, axis=0, keepdims=True)
    cnt_ref[...] = carry_ref[...]


def _out_route(geom, layer, mixes, w_parts, x, mods, norm_g, router_w, router_b):
    t, d = x.shape
    tm = TOKEN_TILE
    row = lambda i: (i, 0)
    const2 = lambda i: (0, 0)
    in_specs = [pl.BlockSpec((tm, m.shape[1]), row) for m in mixes]
    in_specs += [pl.BlockSpec(w.shape, const2) for w in w_parts]
    in_specs += [
        pl.BlockSpec((tm, d), row),
        _mod_spec(geom, layer, 2, d),
        pl.BlockSpec((1, d), const2),
        _mod_spec(geom, layer, 3, d),
        _mod_spec(geom, layer, 4, d),
        pl.BlockSpec(router_w.shape, const2),
        pl.BlockSpec((1, N_EXPERTS), const2),
    ]
    out_shape = [
        jax.ShapeDtypeStruct((t, d), F32),
        jax.ShapeDtypeStruct((t, d), F32),
        jax.ShapeDtypeStruct((t, TOP_K), I32),
        jax.ShapeDtypeStruct((t, TOP_K), I32),
        jax.ShapeDtypeStruct((t, TOP_K), F32),
        jax.ShapeDtypeStruct((1, N_EXPERTS), F32),
    ]
    out_specs = [
        pl.BlockSpec((tm, d), row),
        pl.BlockSpec((tm, d), row),
        pl.BlockSpec((tm, TOP_K), row),
        pl.BlockSpec((tm, TOP_K), row),
        pl.BlockSpec((tm, TOP_K), row),
        pl.BlockSpec((1, N_EXPERTS), const2),
    ]
    return pl.pallas_call(
        functools.partial(_out_route_kernel, n_mix=len(mixes)),
        out_shape=out_shape,
        grid=(geom.n_tiles,),
        in_specs=in_specs,
        out_specs=out_specs,
        scratch_shapes=[pltpu.VMEM((1, N_EXPERTS), F32)],
        compiler_params=_cparams("arbitrary"),
        name="out_route",
    )(*mixes, *w_parts, x, mods, norm_g, mods, mods, router_w, router_b)


def _slot(start_ref, idx_ref, rank_ref, a):
    return start_ref[idx_ref[a]] + rank_ref[a]


def _dispatch_kernel(start_ref, idx_ref, rank_ref, h_ref, xs_ref, sem):
    tm = h_ref.shape[0]

    def issue(t, carry):
        for k in range(TOP_K):
            d = _slot(start_ref, idx_ref, rank_ref, t * TOP_K + k)
            pltpu.make_async_copy(h_ref.at[pl.ds(t, 1)], xs_ref.at[pl.ds(d, 1)], sem).start()
        return carry

    lax.fori_loop(0, tm, issue, 0)
    for _ in range(TOP_K):
        pltpu.make_async_copy(h_ref, xs_ref.at[pl.ds(0, tm)], sem).wait()


def _dispatch(h, idx_flat, rank_flat, slot_start, n_slots):
    t, d = h.shape
    tm = TOKEN_TILE
    return pl.pallas_call(
        _dispatch_kernel,
        out_shape=jax.ShapeDtypeStruct((n_slots, d), F32),
        grid_spec=pltpu.PrefetchScalarGridSpec(
            num_scalar_prefetch=1,
            grid=(t // tm,),
            in_specs=[
                pl.BlockSpec((tm * TOP_K,), lambda i, s: (i,), memory_space=pltpu.SMEM),
                pl.BlockSpec((tm * TOP_K,), lambda i, s: (i,), memory_space=pltpu.SMEM),
                pl.BlockSpec((tm, d), lambda i, s: (i, 0)),
            ],
            out_specs=pl.BlockSpec(memory_space=pl.ANY),
            scratch_shapes=[pltpu.SemaphoreType.DMA],
        ),
        compiler_params=pltpu.CompilerParams(
            dimension_semantics=("arbitrary",), vmem_limit_bytes=VMEM_LIMIT, has_side_effects=True),
        name="moe_dispatch",
    )(slot_start, idx_flat, rank_flat, h)


def _ffn_kernel(be_ref, valid_ref, nact_ref, x_ref, wgu_ref, bgu_ref, wd_ref, bd_ref, y_ref, wgu_bf, wd_bf):
    j = pl.program_id(0)
    d_ff = wd_ref.shape[1]

    @pl.when(j < nact_ref[0])
    def _():
        changed = jnp.logical_or(j == 0, be_ref[j] != be_ref[jnp.maximum(j - 1, 0)])

        @pl.when(changed)
        def _():
            chunk = 128
            def cast_gu(c, carry):
                r = pl.multiple_of(c * chunk, chunk)
                wgu_bf[pl.ds(r, chunk), :] = wgu_ref[0, pl.ds(r, chunk), :].astype(BF16)
                return carry
            lax.fori_loop(0, wgu_ref.shape[1] // chunk, cast_gu, 0)
            def cast_d(c, carry):
                r = pl.multiple_of(c * chunk, chunk)
                wd_bf[pl.ds(r, chunk), :] = wd_ref[0, pl.ds(r, chunk), :].astype(BF16)
                return carry
            lax.fori_loop(0, d_ff // chunk, cast_d, 0)

        rows = lax.broadcasted_iota(I32, (x_ref.shape[0], 1), 0)
        x = jnp.where(rows < valid_ref[j], x_ref[...], 0.0).astype(BF16)
        gu = _dot(x, wgu_bf[...]) + bgu_ref[0]
        gate = jnp.minimum(gu[:, :d_ff], SWIGLU_LIMIT)
        up = jnp.clip(gu[:, d_ff:], -SWIGLU_LIMIT, SWIGLU_LIMIT)
        act = (up + 1.0) * (gate * (1.0 / (1.0 + jnp.exp(-SWIGLU_ALPHA * gate))))
        y_ref[...] = _dot(act.astype(BF16), wd_bf[...]) + bd_ref[0]


def _expert_ffn(xs, block_expert, block_valid, n_active, w_gu, b_gu, w_down, b_down):
    n_slots, d = xs.shape
    tb = EXPERT_TILE
    ne, _, two_f = w_gu.shape
    d_ff = two_f // 2
    n_blocks = n_slots // tb
    row = lambda j, be, va, na: (jnp.minimum(j, na[0] - 1), 0)
    exp3 = lambda j, be, va, na: (be[j], 0, 0)
    return pl.pallas_call(
        _ffn_kernel,
        out_shape=jax.ShapeDtypeStruct((n_slots, d), F32),
        grid_spec=pltpu.PrefetchScalarGridSpec(
            num_scalar_prefetch=3,
            grid=(n_blocks,),
            in_specs=[
                pl.BlockSpec((tb, d), row),
                pl.BlockSpec((1, d, two_f), exp3),
                pl.BlockSpec((1, 1, two_f), exp3),
                pl.BlockSpec((1, d_ff, d), exp3),
                pl.BlockSpec((1, 1, d), exp3),
            ],
            out_specs=pl.BlockSpec((tb, d), row),
            scratch_shapes=[pltpu.VMEM((d, two_f), BF16), pltpu.VMEM((d_ff, d), BF16)],
        ),
        compiler_params=_cparams("arbitrary"),
        name="expert_ffn",
    )(block_expert, block_valid, n_active, xs, w_gu, b_gu.reshape(ne, 1, two_f), w_down, b_down.reshape(ne, 1, d))


def _combine_kernel(start_ref, idx_ref, rank_ref, x_ref, gates_ref, mg_ref, *rest, final):
    if final:
        fn_ref, ys_ref, o_ref, buf, sem = rest
    else:
        ys_ref, o_ref, buf, sem = rest
    tm = x_ref.shape[0]

    def issue(t, carry):
        for k in range(TOP_K):
            d = _slot(start_ref, idx_ref, rank_ref, t * TOP_K + k)
            pltpu.make_async_copy(ys_ref.at[pl.ds(d, 1)], buf.at[k, pl.ds(t, 1)], sem).start()
        return carry

    lax.fori_loop(0, tm, issue, 0)
    for k in range(TOP_K):
        pltpu.make_async_copy(ys_ref.at[pl.ds(0, tm)], buf.at[k], sem).wait()
    g = gates_ref[...]
    y = g[:, 0:1] * buf[0]
    for k in range(1, TOP_K):
        y = y + g[:, k:k + 1] * buf[k]
    xn = x_ref[...] + mg_ref[...] * y
    if final:
        xn = _rms(xn, fn_ref[...])
    o_ref[...] = xn


def _combine(geom, layer, ys, idx_flat, rank_flat, slot_start, x, gates, mods, final_g):
    t, d = x.shape
    tm = TOKEN_TILE
    final = final_g is not None

    def mod_imap(i, s):
        return ((layer * MOD_ROWS + geom.group(i)) * N_MOD + 5, 0, 0)

    in_specs = [
        pl.BlockSpec((tm * TOP_K,), lambda i, s: (i,), memory_space=pltpu.SMEM),
        pl.BlockSpec((tm * TOP_K,), lambda i, s: (i,), memory_space=pltpu.SMEM),
        pl.BlockSpec((tm, d), lambda i, s: (i, 0)),
        pl.BlockSpec((tm, TOP_K), lambda i, s: (i, 0)),
        pl.BlockSpec((None, 1, d), mod_imap),
    ]
    args = [slot_start, idx_flat, rank_flat, x, gates, mods]
    if final:
        in_specs.append(pl.BlockSpec((1, d), lambda i, s: (0, 0)))
        args.append(final_g)
    in_specs.append(pl.BlockSpec(memory_space=pl.ANY))
    args.append(ys)
    return pl.pallas_call(
        functools.partial(_combine_kernel, final=final),
        out_shape=jax.ShapeDtypeStruct((t, d), F32),
        grid_spec=pltpu.PrefetchScalarGridSpec(
            num_scalar_prefetch=1,
            grid=(t // tm,),
            in_specs=in_specs,
            out_specs=pl.BlockSpec((tm, d), lambda i, s: (i, 0)),
            scratch_shapes=[pltpu.VMEM((TOP_K, tm, d), F32), pltpu.SemaphoreType.DMA],
        ),
        compiler_params=_cparams("arbitrary"),
        name="moe_combine",
    )(*args)


def _moe(geom, layer, h, idx, rank, gates, counts, x, mods, w_gu, b_gu, w_down, b_down, final_g):
    t = h.shape[0]
    tb = EXPERT_TILE
    n_blocks = (t * TOP_K) // tb + N_EXPERTS
    cnt = counts[0].astype(I32)
    n_tiles_e = (cnt + tb - 1) // tb
    tile_end = jnp.cumsum(n_tiles_e)
    tile_start = tile_end - n_tiles_e
    slot_start = (tile_start * tb).astype(I32)
    n_active = tile_end[-1:].astype(I32)
    j = jnp.arange(n_blocks, dtype=I32)
    j_eff = jnp.minimum(j, n_active[0] - 1)
    be = jnp.minimum(jnp.searchsorted(tile_end, j_eff, side="right"), N_EXPERTS - 1).astype(I32)
    valid = jnp.clip(cnt[be] - (j_eff - tile_start[be]) * tb, 0, tb).astype(I32)
    idx_flat = idx.reshape(-1)
    rank_flat = rank.reshape(-1)
    xs = _dispatch(h, idx_flat, rank_flat, slot_start, n_blocks * tb)
    ys = _expert_ffn(xs, be, valid, n_active, w_gu, b_gu, w_down, b_down)
    return _combine(geom, layer, ys, idx_flat, rank_flat, slot_start, x, gates, mods, final_g)


def kernel(x_prompt, x_sample, cache_b_k, cache_b_v, cache_c_k, cache_c_v, c, c_ctx,
           mod_w, mod_b, norm_mix, norm_ffn, even_w_in, even_w_out, even_sink,
           odd_w_in, odd_w_out, odd_q_norm, odd_k_norm, router_w, router_b,
           moe_w_gu, moe_b_gu, moe_w_down, moe_b_down, final_norm):
    bp, lp, d = x_prompt.shape
    bs, ls, _ = x_sample.shape
    past = cache_b_k.shape[2]
    depth = mod_w.shape[0]
    geom = _Geom(bp, lp, bs, ls)
    tp = geom.tp

    x = jnp.concatenate([x_prompt.reshape(tp, d), x_sample.reshape(bs * ls, d)], axis=0)
    cond = jnp.concatenate([c_ctx[None, :], c, jnp.zeros((MOD_ROWS - 1 - bs, d), F32)], axis=0)
    mods = _modulation(cond, mod_w, mod_b).reshape(depth * MOD_ROWS * N_MOD, 1, d)

    cn, sn = _dft_tables(A_GROUP_DIM)
    dft_chan = jnp.asarray(np.concatenate([cn, sn], axis=1), BF16)
    dft_p = [jnp.asarray(m, BF16) for m in _dft_tables(lp)]
    dft_s = [jnp.asarray(m, BF16) for m in _dft_tables(ls)]
    rope_b = [jnp.asarray(m) for m in _rope_tables(TOKEN_TILE, ls, B_HEAD_DIM)]
    rope_c = [jnp.asarray(m) for m in _rope_tables(TOKEN_TILE, ls, C_HEAD_DIM)]

    states = {"bk": [], "bv": [], "ck": [], "cv": []}
    for layer in range(depth):
        j = layer // 2
        g_mix = norm_mix[layer][None, :]
        g_ffn = norm_ffn[layer][None, :]
        if layer % 2 == 0:
            tc, ts, q, k, v = _in_projection(
                geom, layer, x, g_mix, mods, even_w_in[j].astype(BF16), rope_b[0], rope_b[1], [dft_chan],
                _proj_even_kernel, (A_WIDTH, A_WIDTH, B_Q_WIDTH, B_KV_WIDTH, B_KV_WIDTH),
                (BF16, BF16, BF16, F32, F32), "proj_even")
            states["bk"].append(k[:tp].reshape(bp, lp, B_KV_HEADS, B_HEAD_DIM))
            states["bv"].append(v[:tp].reshape(bp, lp, B_KV_HEADS, B_HEAD_DIM))
            four = _fourier_tokens(tc, ts, dft_p[0], dft_p[1], bp, lp, 0, None)
            four = _fourier_tokens(tc, ts, dft_s[0], dft_s[1], bs, ls, tp, four)
            sink = even_sink[j]
            common = dict(kv_heads=B_KV_HEADS, groups=B_HEADS // B_KV_HEADS, dh=B_HEAD_DIM)
            att = _attention(q, k, v, None, sink, None, n_seq=bp, seq_len=lp, row0=0, q_tile=lp,
                             window=None, **common)
            ctx = (cache_b_k[:, j].reshape(bs, past, B_KV_WIDTH), cache_b_v[:, j].reshape(bs, past, B_KV_WIDTH))
            att = _attention(q, k, v, ctx, sink, att, n_seq=bs, seq_len=ls, row0=tp, q_tile=ATTN_Q_TILE,
                             window=WINDOW, **common)
            w_out = even_w_out[j].astype(BF16)
            mixes = [four, att]
            w_parts = [w_out[:A_WIDTH], w_out[A_WIDTH:]]
        else:
            q, k, v = _in_projection(
                geom, layer, x, g_mix, mods, odd_w_in[j].astype(BF16), rope_c[0], rope_c[1],
                [odd_q_norm[j][None, :], odd_k_norm[j][None, :]],
                _proj_odd_kernel, (C_Q_WIDTH, C_KV_WIDTH, C_KV_WIDTH), (BF16, F32, F32), "proj_odd")
            states["ck"].append(k[:tp].reshape(bp, lp, C_KV_HEADS, C_HEAD_DIM))
            states["cv"].append(v[:tp].reshape(bp, lp, C_KV_HEADS, C_HEAD_DIM))
            common = dict(kv_heads=C_KV_HEADS, groups=C_HEADS // C_KV_HEADS, dh=C_HEAD_DIM, window=None)
            att = _attention(q, k, v, None, None, None, n_seq=bp, seq_len=lp, row0=0, q_tile=lp, **common)
            ctx = (cache_c_k[:, j].reshape(bs, past, C_KV_WIDTH), cache_c_v[:, j].reshape(bs, past, C_KV_WIDTH))
            att = _attention(q, k, v, ctx, None, att, n_seq=bs, seq_len=ls, row0=tp, q_tile=ATTN_Q_TILE, **common)
            mixes = [att]
            w_parts = [odd_w_out[j].astype(BF16)]
        x, h, idx, rank, gates, counts = _out_route(
            geom, layer, mixes, w_parts, x, mods, g_ffn, router_w[layer].astype(BF16), router_b[layer][None, :])
        final_g = final_norm[None, :] if layer == depth - 1 else None
        x = _moe(geom, layer, h, idx, rank, gates, counts, x, mods,
                 moe_w_gu[layer], moe_b_gu[layer], moe_w_down[layer], moe_b_down[layer], final_g)

    y_prompt = x[:tp].reshape(bp, lp, d)
    y_sample = x[tp:].reshape(bs, ls, d)
    return (y_prompt, y_sample,
            jnp.stack(states["bk"], axis=1), jnp.stack(states["bv"], axis=1),
            jnp.stack(states["ck"], axis=1), jnp.stack(states["cv"], axis=1))
```

```python
import functools

import numpy as np
import jax
import jax.numpy as jnp
from jax import lax
from jax.experimental import pallas as pl
from jax.experimental.pallas import tpu as pltpu

F32 = jnp.float32
BF16 = jnp.bfloat16
I32 = jnp.int32

GRID_W = 64
A_GROUPS = 4
A_GROUP_DIM = 128
A_WIDTH = A_GROUPS * A_GROUP_DIM
B_HEADS = 8
B_KV_HEADS = 2
B_HEAD_DIM = 64
B_Q_WIDTH = B_HEADS * B_HEAD_DIM
B_KV_WIDTH = B_KV_HEADS * B_HEAD_DIM
WINDOW = 128
C_HEADS = 8
C_KV_HEADS = 2
C_HEAD_DIM = 128
C_Q_WIDTH = C_HEADS * C_HEAD_DIM
C_KV_WIDTH = C_KV_HEADS * C_HEAD_DIM
ROPE_THETA = 10000.0
N_EXPERTS = 32
TOP_K = 4
SWIGLU_LIMIT = 7.0
SWIGLU_ALPHA = 1.702
EPS = 1e-6

LANES = 128
TOKEN_TILE = 256
EXPERT_TILE = 256
ATTN_Q_TILE = 128
VMEM_LIMIT = 56 * 1024 * 1024
MASKED = -1e30
N_MOD = 6
MOD_ROWS = 8


def _cparams(*sem):
    return pltpu.CompilerParams(dimension_semantics=tuple(sem), vmem_limit_bytes=VMEM_LIMIT)


def _dot(a, b):
    return jnp.dot(a, b, preferred_element_type=F32)


def _dot_nt(a, b):
    return lax.dot_general(a, b, (((1,), (1,)), ((), ())), preferred_element_type=F32)


def _rms(x, g):
    return x * lax.rsqrt(jnp.mean(x * x, axis=-1, keepdims=True) + EPS) * g


def _dft_tables(n):
    j = np.arange(n, dtype=np.int64)
    ang = 2.0 * np.pi * ((j[:, None] * j[None, :]) % n).astype(np.float64) / n
    s = 1.0 / np.sqrt(n)
    return np.cos(ang) * s, np.sin(ang) * s


def _rope_tables(n_prompt_rows, n_latent, head_dim):
    quarter = head_dim // 4
    pos = np.arange(n_latent)
    row = (pos // GRID_W).astype(np.float32)
    col = (pos % GRID_W).astype(np.float32)
    inv = (np.float32(ROPE_THETA) ** (-np.arange(quarter, dtype=np.float32) / np.float32(quarter))).astype(np.float32)
    ang_row = (row[:, None] * inv[None, :]).astype(np.float32)
    ang_col = (col[:, None] * inv[None, :]).astype(np.float32)
    cos_h = np.concatenate([np.cos(ang_row)] * 2 + [np.cos(ang_col)] * 2, axis=1)
    sin_h = np.concatenate([-np.sin(ang_row), np.sin(ang_row), -np.sin(ang_col), np.sin(ang_col)], axis=1)
    reps = LANES // head_dim
    cos_l = np.tile(cos_h, (1, reps)).astype(np.float32)
    sin_l = np.tile(sin_h, (1, reps)).astype(np.float32)
    cos = np.concatenate([np.ones((n_prompt_rows, LANES), np.float32), cos_l], axis=0)
    sin = np.concatenate([np.zeros((n_prompt_rows, LANES), np.float32), sin_l], axis=0)
    return cos, sin


def _rope(x, cos, sin, quarter):
    lane = lax.broadcasted_iota(I32, (x.shape[0], LANES), 1)
    first = ((lane // quarter) % 2) == 0
    outs = []
    for c in range(x.shape[1] // LANES):
        xc = x[:, c * LANES:(c + 1) * LANES]
        partner = jnp.where(first, pltpu.roll(xc, LANES - quarter, 1), pltpu.roll(xc, quarter, 1))
        outs.append(xc * cos + partner * sin)
    return outs[0] if len(outs) == 1 else jnp.concatenate(outs, axis=1)


def _head_rms(x, g):
    outs = []
    for c in range(x.shape[1] // LANES):
        outs.append(_rms(x[:, c * LANES:(c + 1) * LANES], g))
    return outs[0] if len(outs) == 1 else jnp.concatenate(outs, axis=1)


def _mod_kernel(c_ref, w_ref, b_ref, o_ref):
    c = c_ref[...]
    s = c * (1.0 / (1.0 + jnp.exp(-c)))
    o_ref[0] = _dot(s.astype(BF16), w_ref[0].astype(BF16)) + b_ref[0]


def _modulation(cond, mod_w, mod_b):
    depth, d, n = mod_w.shape
    tn = 1536
    return pl.pallas_call(
        _mod_kernel,
        out_shape=jax.ShapeDtypeStruct((depth, MOD_ROWS, n), F32),
        grid=(depth, n // tn),
        in_specs=[
            pl.BlockSpec((MOD_ROWS, d), lambda l, j: (0, 0)),
            pl.BlockSpec((1, d, tn), lambda l, j: (l, 0, j)),
            pl.BlockSpec((1, 1, tn), lambda l, j: (l, 0, j)),
        ],
        out_specs=pl.BlockSpec((1, MOD_ROWS, tn), lambda l, j: (l, 0, j)),
        compiler_params=_cparams("arbitrary", "arbitrary"),
        name="modulation",
    )(cond, mod_w, mod_b.reshape(depth, 1, n))


class _Geom:
    def __init__(self, bp, lp, bs, ls):
        self.bp, self.lp, self.bs, self.ls = bp, lp, bs, ls
        self.tp = bp * lp
        self.t = bp * lp + bs * ls
        assert lp == TOKEN_TILE and ls % TOKEN_TILE == 0 and self.tp % ls == 0
        self.n_ptiles = self.tp // TOKEN_TILE
        self.tiles_per_lat = ls // TOKEN_TILE
        self.n_tiles = self.t // TOKEN_TILE

    def group(self, i):
        return jnp.where(i < self.n_ptiles, 0, 1 + (i - self.n_ptiles) // self.tiles_per_lat)

    def pos_block(self, i):
        return jnp.where(i < self.n_ptiles, 0, 1 + (i - self.n_ptiles) % self.tiles_per_lat)


def _mod_spec(geom, layer, which, d):
    def imap(i):
        return ((layer * MOD_ROWS + geom.group(i)) * N_MOD + which, 0, 0)
    return pl.BlockSpec((None, 1, d), imap)


def _proj_even_kernel(x_ref, g_ref, sh_ref, sc_ref, w_ref, cos_ref, sin_ref, dft_ref,
                      tc_ref, ts_ref, q_ref, k_ref, v_ref):
    h = _rms(x_ref[...], g_ref[...]) * (1.0 + sc_ref[...]) + sh_ref[...]
    p = _dot(h.astype(BF16), w_ref[...])
    cos = cos_ref[...]
    sin = sin_ref[...]
    dft = dft_ref[...]
    tcs, tss = [], []
    for g in range(A_GROUPS):
        t = _dot(p[:, g * A_GROUP_DIM:(g + 1) * A_GROUP_DIM].astype(BF16), dft)
        tcs.append(t[:, :A_GROUP_DIM])
        tss.append(t[:, A_GROUP_DIM:])
    tc_ref[...] = jnp.concatenate(tcs, axis=1).astype(BF16)
    ts_ref[...] = jnp.concatenate(tss, axis=1).astype(BF16)
    o = A_WIDTH
    q_ref[...] = _rope(p[:, o:o + B_Q_WIDTH], cos, sin, B_HEAD_DIM // 4).astype(BF16)
    o += B_Q_WIDTH
    k_ref[...] = _rope(p[:, o:o + B_KV_WIDTH], cos, sin, B_HEAD_DIM // 4)
    o += B_KV_WIDTH
    v_ref[...] = p[:, o:o + B_KV_WIDTH]


def _proj_odd_kernel(x_ref, g_ref, sh_ref, sc_ref, w_ref, cos_ref, sin_ref, qn_ref, kn_ref,
                     q_ref, k_ref, v_ref):
    h = _rms(x_ref[...], g_ref[...]) * (1.0 + sc_ref[...]) + sh_ref[...]
    p = _dot(h.astype(BF16), w_ref[...])
    cos = cos_ref[...]
    sin = sin_ref[...]
    q = _head_rms(p[:, :C_Q_WIDTH], qn_ref[...])
    k = _head_rms(p[:, C_Q_WIDTH:C_Q_WIDTH + C_KV_WIDTH], kn_ref[...])
    q_ref[...] = _rope(q, cos, sin, C_HEAD_DIM // 4).astype(BF16)
    k_ref[...] = _rope(k, cos, sin, C_HEAD_DIM // 4)
    v_ref[...] = p[:, C_Q_WIDTH + C_KV_WIDTH:]


def _in_projection(geom, layer, x, norm_g, mods, w, cos, sin, extras, kernel, out_widths, out_dtypes, name):
    t, d = x.shape
    tm = TOKEN_TILE
    n_out = w.shape[1]
    row = lambda i: (i, 0)
    const2 = lambda i: (0, 0)
    in_specs = [
        pl.BlockSpec((tm, d), row),
        pl.BlockSpec((1, d), const2),
        _mod_spec(geom, layer, 0, d),
        _mod_spec(geom, layer, 1, d),
        pl.BlockSpec((d, n_out), const2),
        pl.BlockSpec((tm, LANES), lambda i: (geom.pos_block(i), 0)),
        pl.BlockSpec((tm, LANES), lambda i: (geom.pos_block(i), 0)),
    ] + [pl.BlockSpec(e.shape, const2) for e in extras]
    return pl.pallas_call(
        kernel,
        out_shape=[jax.ShapeDtypeStruct((t, wd), dt) for wd, dt in zip(out_widths, out_dtypes)],
        grid=(geom.n_tiles,),
        in_specs=in_specs,
        out_specs=[pl.BlockSpec((tm, wd), row) for wd in out_widths],
        compiler_params=_cparams("arbitrary"),
        name=name,
    )(x, norm_g, mods, mods, w, cos, sin, *extras)


def _fourier_kernel(cl_ref, sl_ref, tc_ref, ts_ref, *rest):
    o_ref = rest[-1]
    o_ref[...] = (_dot(cl_ref[...], tc_ref[...]) - _dot(sl_ref[...], ts_ref[...])).astype(o_ref.dtype)


def _fourier_tokens(tc, ts, cl, sl, n_seq, seq_len, row0, prev):
    t, width = tc.shape
    tr = min(seq_len, 512)
    n_r = seq_len // tr
    assert row0 % seq_len == 0
    seq0 = row0 // seq_len
    out0 = row0 // tr
    in_specs = [
        pl.BlockSpec((tr, seq_len), lambda s, r: (r, 0)),
        pl.BlockSpec((tr, seq_len), lambda s, r: (r, 0)),
        pl.BlockSpec((seq_len, width), lambda s, r: (seq0 + s, 0)),
        pl.BlockSpec((seq_len, width), lambda s, r: (seq0 + s, 0)),
    ]
    args = [cl, sl, tc, ts]
    aliases = {}
    if prev is not None:
        in_specs.append(pl.BlockSpec(memory_space=pl.ANY))
        args.append(prev)
        aliases = {4: 0}
    return pl.pallas_call(
        _fourier_kernel,
        out_shape=jax.ShapeDtypeStruct((t, width), BF16),
        grid=(n_seq, n_r),
        in_specs=in_specs,
        out_specs=pl.BlockSpec((tr, width), lambda s, r: (out0 + s * n_r + r, 0)),
        input_output_aliases=aliases,
        compiler_params=_cparams("arbitrary", "arbitrary"),
        name="fourier_tokens",
    )(*args)


def _softmax_pv(q, parts, sink, scale):
    scores = []
    m = sink
    for k, _, mask in parts:
        s = _dot_nt(q, k) * scale
        if mask is not None:
            s = jnp.where(mask, s, MASKED)
        scores.append(s)
        mx = jnp.max(s, axis=-1, keepdims=True)
        m = mx if m is None else jnp.maximum(m, mx)
    den = None if sink is None else jnp.exp(sink - m)
    acc = None
    for (_, v, _), s in zip(parts, scores):
        e = jnp.exp(s - m)
        es = jnp.sum(e, axis=-1, keepdims=True)
        den = es if den is None else den + es
        o = _dot(e.astype(BF16), v)
        acc = o if acc is None else acc + o
    return acc / den


def _stack_heads(q, kv, groups, dh):
    return jnp.concatenate([q[:, (kv * groups + g) * dh:(kv * groups + g + 1) * dh] for g in range(groups)], axis=0)


def _sink_column(sink_ref, kv, groups, rows):
    return jnp.concatenate([jnp.full((rows, 1), sink_ref[kv * groups + g], F32) for g in range(groups)], axis=0)


def _attn_kernel(*refs, kv_heads, groups, dh, has_sink, has_ctx, window, q_tile, seq_len):
    refs = list(refs)
    sink_ref = refs.pop(0) if has_sink else None
    q_ref, k_ref, v_ref = refs[:3]
    ck_ref, cv_ref = (refs[3], refs[4]) if has_ctx else (None, None)
    o_ref = refs[-1]
    scale = dh ** -0.5
    q = q_ref[...]
    rows = q.shape[0]
    if window is None:
        k_loc = k_ref[...].astype(BF16)
        v_loc = v_ref[...].astype(BF16)
        mask = None
    else:
        n = pl.program_id(1)
        band = q_tile + 2 * window
        start = jnp.clip(n * q_tile - window, 0, seq_len - band)
        start = pl.multiple_of(start, LANES)
        k_loc = k_ref[pl.ds(start, band), :].astype(BF16)
        v_loc = v_ref[pl.ds(start, band), :].astype(BF16)
        qpos = n * q_tile + lax.broadcasted_iota(I32, (groups * rows, band), 0) % rows
        kpos = start + lax.broadcasted_iota(I32, (groups * rows, band), 1)
        mask = jnp.abs(kpos - qpos) <= window
    if has_ctx:
        k_ctx = ck_ref[...].astype(BF16)
        v_ctx = cv_ref[...].astype(BF16)
    outs = []
    for kv in range(kv_heads):
        sl = slice(kv * dh, (kv + 1) * dh)
        parts = [(k_loc[:, sl], v_loc[:, sl], mask)]
        if has_ctx:
            parts.append((k_ctx[:, sl], v_ctx[:, sl], None))
        sink = _sink_column(sink_ref, kv, groups, rows) if has_sink else None
        o = _softmax_pv(_stack_heads(q, kv, groups, dh), parts, sink, scale)
        outs.extend(o[g * rows:(g + 1) * rows] for g in range(groups))
    o_ref[...] = jnp.concatenate(outs, axis=1).astype(o_ref.dtype)


def _attention(q, k, v, ctx, sink, prev, *, n_seq, seq_len, row0, q_tile, kv_heads, groups, dh, window):
    t, qw = q.shape
    kw = k.shape[1]
    n_q = seq_len // q_tile
    assert row0 % seq_len == 0 and row0 % q_tile == 0
    seq0 = row0 // seq_len
    q0 = row0 // q_tile
    in_specs, args = [], []
    if sink is not None:
        in_specs.append(pl.BlockSpec(memory_space=pltpu.SMEM))
        args.append(sink)
    in_specs += [
        pl.BlockSpec((q_tile, qw), lambda s, n: (q0 + s * n_q + n, 0)),
        pl.BlockSpec((seq_len, kw), lambda s, n: (seq0 + s, 0)),
        pl.BlockSpec((seq_len, kw), lambda s, n: (seq0 + s, 0)),
    ]
    args += [q, k, v]
    if ctx is not None:
        p = ctx[0].shape[1]
        in_specs += [pl.BlockSpec((None, p, kw), lambda s, n: (s, 0, 0))] * 2
        args += list(ctx)
    aliases = {}
    if prev is not None:
        in_specs.append(pl.BlockSpec(memory_space=pl.ANY))
        aliases = {len(args): 0}
        args.append(prev)
    kern = functools.partial(
        _attn_kernel, kv_heads=kv_heads, groups=groups, dh=dh, has_sink=sink is not None,
        has_ctx=ctx is not None, window=window, q_tile=q_tile, seq_len=seq_len)
    return pl.pallas_call(
        kern,
        out_shape=jax.ShapeDtypeStruct((t, qw), BF16),
        grid=(n_seq, n_q),
        in_specs=in_specs,
        out_specs=pl.BlockSpec((q_tile, qw), lambda s, n: (q0 + s * n_q + n, 0)),
        input_output_aliases=aliases,
        compiler_params=_cparams("arbitrary", "arbitrary"),
        name="attention",
    )(*args)


def _out_route_kernel(*refs, n_mix):
    mix_refs = refs[:n_mix]
    w_refs = refs[n_mix:2 * n_mix]
    (x_ref, gate_ref, g2_ref, sh2_ref, sc2_ref, rw_ref, rb_ref,
     xo_ref, h_ref, idx_ref, rank_ref, gates_ref, cnt_ref, carry_ref) = refs[2 * n_mix:]
    i = pl.program_id(0)

    @pl.when(i == 0)
    def _():
        carry_ref[...] = jnp.zeros_like(carry_ref)

    acc = None
    for m_ref, w_ref in zip(mix_refs, w_refs):
        part = _dot(m_ref[...], w_ref[...])
        acc = part if acc is None else acc + part
    xn = x_ref[...] + gate_ref[...] * acc
    xo_ref[...] = xn
    h = _rms(xn, g2_ref[...]) * (1.0 + sc2_ref[...]) + sh2_ref[...]
    h_ref[...] = h

    logits = _dot(h.astype(BF16), rw_ref[...]) + rb_ref[...]
    tm, ne = logits.shape
    lane = lax.broadcasted_iota(I32, (tm, ne), 1).astype(F32)
    lane4 = lax.broadcasted_iota(I32, (tm, TOP_K), 1)
    work = logits
    sels, vals, ids = [], [], []
    for _ in range(TOP_K):
        mx = jnp.max(work, axis=-1, keepdims=True)
        first = jnp.min(jnp.where(work == mx, lane, float(ne)), axis=-1, keepdims=True)
        sel = lane == first
        work = jnp.where(sel, -jnp.inf, work)
        sels.append(sel)
        vals.append(mx)
        ids.append(first)
    exps = [jnp.exp(v - vals[0]) for v in vals]
    den = exps[0] + exps[1] + exps[2] + exps[3]

    onehot = jnp.zeros((tm, ne), F32)
    for sel in sels:
        onehot = onehot + sel.astype(F32)
    r_i = lax.broadcasted_iota(I32, (tm, tm), 0)
    c_i = lax.broadcasted_iota(I32, (tm, tm), 1)
    before = jnp.where(c_i < r_i, 1.0, 0.0).astype(BF16)
    base = carry_ref[...] + _dot(before, onehot.astype(BF16))

    idx_o = jnp.zeros((tm, TOP_K), I32)
    rank_o = jnp.zeros((tm, TOP_K), I32)
    gate_o = jnp.zeros((tm, TOP_K), F32)
    for k in range(TOP_K):
        rank_k = jnp.sum(jnp.where(sels[k], base, 0.0), axis=-1, keepdims=True)
        idx_o = jnp.where(lane4 == k, ids[k].astype(I32), idx_o)
        rank_o = jnp.where(lane4 == k, rank_k.astype(I32), rank_o)
        gate_o = jnp.where(lane4 == k, exps[k] / den, gate_o)
    idx_ref[...] = idx_o
    rank_ref[...] = rank_o
    gates_ref[...] = gate_o
    carry_ref[...] = carry_ref[...] + jnp.sum(onehot, axis=0, keepdims=True)
    cnt_ref[...] = carry_ref[...]


def _out_route(geom, layer, mixes, w_parts, x, mods, norm_g, router_w, router_b):
    t, d = x.shape
    tm = TOKEN_TILE
    row = lambda i: (i, 0)
    const2 = lambda i: (0, 0)
    in_specs = [pl.BlockSpec((tm, m.shape[1]), row) for m in mixes]
    in_specs += [pl.BlockSpec(w.shape, const2) for w in w_parts]
    in_specs += [
        pl.BlockSpec((tm, d), row),
        _mod_spec(geom, layer, 2, d),
        pl.BlockSpec((1, d), const2),
        _mod_spec(geom, layer, 3, d),
        _mod_spec(geom, layer, 4, d),
        pl.BlockSpec(router_w.shape, const2),
        pl.BlockSpec((1, N_EXPERTS), const2),
    ]
    out_shape = [
        jax.ShapeDtypeStruct((t, d), F32),
        jax.ShapeDtypeStruct((t, d), F32),
        jax.ShapeDtypeStruct((t, TOP_K), I32),
        jax.ShapeDtypeStruct((t, TOP_K), I32),
        jax.ShapeDtypeStruct((t, TOP_K), F32),
        jax.ShapeDtypeStruct((1, N_EXPERTS), F32),
    ]
    out_specs = [
        pl.BlockSpec((tm, d), row),
        pl.BlockSpec((tm, d), row),
        pl.BlockSpec((tm, TOP_K), row),
        pl.BlockSpec((tm, TOP_K), row),
        pl.BlockSpec((tm, TOP_K), row),
        pl.BlockSpec((1, N_EXPERTS), const2),
    ]
    return pl.pallas_call(
        functools.partial(_out_route_kernel, n_mix=len(mixes)),
        out_shape=out_shape,
        grid=(geom.n_tiles,),
        in_specs=in_specs,
        out_specs=out_specs,
        scratch_shapes=[pltpu.VMEM((1, N_EXPERTS), F32)],
        compiler_params=_cparams("arbitrary"),
        name="out_route",
    )(*mixes, *w_parts, x, mods, norm_g, mods, mods, router_w, router_b)


def _slot(start_ref, idx_ref, rank_ref, a):
    return start_ref[idx_ref[a]] + rank_ref[a]


def _dispatch_kernel(start_ref, idx_ref, rank_ref, h_ref, xs_ref, sem):
    tm = h_ref.shape[0]

    def issue(t, carry):
        for k in range(TOP_K):
            d = _slot(start_ref, idx_ref, rank_ref, t * TOP_K + k)
            pltpu.make_async_copy(h_ref.at[pl.ds(t, 1)], xs_ref.at[pl.ds(d, 1)], sem).start()
        return carry

    lax.fori_loop(0, tm, issue, 0)
    for _ in range(TOP_K):
        pltpu.make_async_copy(h_ref, xs_ref.at[pl.ds(0, tm)], sem).wait()


def _dispatch(h, idx_flat, rank_flat, slot_start, n_slots):
    t, d = h.shape
    tm = TOKEN_TILE
    return pl.pallas_call(
        _dispatch_kernel,
        out_shape=jax.ShapeDtypeStruct((n_slots, d), F32),
        grid_spec=pltpu.PrefetchScalarGridSpec(
            num_scalar_prefetch=1,
            grid=(t // tm,),
            in_specs=[
                pl.BlockSpec((tm * TOP_K,), lambda i, s: (i,), memory_space=pltpu.SMEM),
                pl.BlockSpec((tm * TOP_K,), lambda i, s: (i,), memory_space=pltpu.SMEM),
                pl.BlockSpec((tm, d), lambda i, s: (i, 0)),
            ],
            out_specs=pl.BlockSpec(memory_space=pl.ANY),
            scratch_shapes=[pltpu.SemaphoreType.DMA],
        ),
        compiler_params=pltpu.CompilerParams(
            dimension_semantics=("arbitrary",), vmem_limit_bytes=VMEM_LIMIT, has_side_effects=True),
        name="moe_dispatch",
    )(slot_start, idx_flat, rank_flat, h)


def _ffn_kernel(be_ref, valid_ref, nact_ref, x_ref, wgu_ref, bgu_ref, wd_ref, bd_ref, y_ref, wgu_bf, wd_bf):
    j = pl.program_id(0)
    d_ff = wd_ref.shape[1]

    @pl.when(j < nact_ref[0])
    def _():
        changed = jnp.logical_or(j == 0, be_ref[j] != be_ref[jnp.maximum(j - 1, 0)])

        @pl.when(changed)
        def _():
            chunk = 128
            def cast_gu(c, carry):
                r = pl.multiple_of(c * chunk, chunk)
                wgu_bf[pl.ds(r, chunk), :] = wgu_ref[0, pl.ds(r, chunk), :].astype(BF16)
                return carry
            lax.fori_loop(0, wgu_ref.shape[1] // chunk, cast_gu, 0)
            def cast_d(c, carry):
                r = pl.multiple_of(c * chunk, chunk)
                wd_bf[pl.ds(r, chunk), :] = wd_ref[0, pl.ds(r, chunk), :].astype(BF16)
                return carry
            lax.fori_loop(0, d_ff // chunk, cast_d, 0)

        rows = lax.broadcasted_iota(I32, (x_ref.shape[0], 1), 0)
        x = jnp.where(rows < valid_ref[j], x_ref[...], 0.0).astype(BF16)
        gu = _dot(x, wgu_bf[...]) + bgu_ref[0]
        gate = jnp.minimum(gu[:, :d_ff], SWIGLU_LIMIT)
        up = jnp.clip(gu[:, d_ff:], -SWIGLU_LIMIT, SWIGLU_LIMIT)
        act = (up + 1.0) * (gate * (1.0 / (1.0 + jnp.exp(-SWIGLU_ALPHA * gate))))
        y_ref[...] = _dot(act.astype(BF16), wd_bf[...]) + bd_ref[0]


def _expert_ffn(layer, xs, block_expert, block_valid, n_active, w_gu, b_gu, w_down, b_down):
    n_slots, d = xs.shape
    tb = EXPERT_TILE
    depth, ne, _, two_f = w_gu.shape
    d_ff = two_f // 2
    n_blocks = n_slots // tb
    row = lambda j, be, va, na: (jnp.minimum(j, na[0] - 1), 0)
    exp4 = lambda j, be, va, na: (layer, be[j], 0, 0)
    return pl.pallas_call(
        _ffn_kernel,
        out_shape=jax.ShapeDtypeStruct((n_slots, d), F32),
        grid_spec=pltpu.PrefetchScalarGridSpec(
            num_scalar_prefetch=3,
            grid=(n_blocks,),
            in_specs=[
                pl.BlockSpec((tb, d), row),
                pl.BlockSpec((None, 1, d, two_f), exp4),
                pl.BlockSpec((None, 1, 1, two_f), exp4),
                pl.BlockSpec((None, 1, d_ff, d), exp4),
                pl.BlockSpec((None, 1, 1, d), exp4),
            ],
            out_specs=pl.BlockSpec((tb, d), row),
            scratch_shapes=[pltpu.VMEM((d, two_f), BF16), pltpu.VMEM((d_ff, d), BF16)],
        ),
        compiler_params=_cparams("arbitrary"),
        name="expert_ffn",
    )(block_expert, block_valid, n_active, xs, w_gu, b_gu.reshape(depth, ne, 1, two_f),
      w_down, b_down.reshape(depth, ne, 1, d))


def _combine_kernel(start_ref, idx_ref, rank_ref, x_ref, gates_ref, mg_ref, *rest, final):
    if final:
        fn_ref, ys_ref, o_ref, buf, sem = rest
    else:
        ys_ref, o_ref, buf, sem = rest
    tm = x_ref.shape[0]

    def issue(t, carry):
        for k in range(TOP_K):
            d = _slot(start_ref, idx_ref, rank_ref, t * TOP_K + k)
            pltpu.make_async_copy(ys_ref.at[pl.ds(d, 1)], buf.at[k, pl.ds(t, 1)], sem).start()
        return carry

    lax.fori_loop(0, tm, issue, 0)
    for k in range(TOP_K):
        pltpu.make_async_copy(ys_ref.at[pl.ds(0, tm)], buf.at[k], sem).wait()
    g = gates_ref[...]
    y = g[:, 0:1] * buf[0]
    for k in range(1, TOP_K):
        y = y + g[:, k:k + 1] * buf[k]
    xn = x_ref[...] + mg_ref[...] * y
    if final:
        xn = _rms(xn, fn_ref[...])
    o_ref[...] = xn


def _combine(geom, layer, ys, idx_flat, rank_flat, slot_start, x, gates, mods, final_g):
    t, d = x.shape
    tm = TOKEN_TILE
    final = final_g is not None

    def mod_imap(i, s):
        return ((layer * MOD_ROWS + geom.group(i)) * N_MOD + 5, 0, 0)

    in_specs = [
        pl.BlockSpec((tm * TOP_K,), lambda i, s: (i,), memory_space=pltpu.SMEM),
        pl.BlockSpec((tm * TOP_K,), lambda i, s: (i,), memory_space=pltpu.SMEM),
        pl.BlockSpec((tm, d), lambda i, s: (i, 0)),
        pl.BlockSpec((tm, TOP_K), lambda i, s: (i, 0)),
        pl.BlockSpec((None, 1, d), mod_imap),
    ]
    args = [slot_start, idx_flat, rank_flat, x, gates, mods]
    if final:
        in_specs.append(pl.BlockSpec((1, d), lambda i, s: (0, 0)))
        args.append(final_g)
    in_specs.append(pl.BlockSpec(memory_space=pl.ANY))
    args.append(ys)
    return pl.pallas_call(
        functools.partial(_combine_kernel, final=final),
        out_shape=jax.ShapeDtypeStruct((t, d), F32),
        grid_spec=pltpu.PrefetchScalarGridSpec(
            num_scalar_prefetch=1,
            grid=(t // tm,),
            in_specs=in_specs,
            out_specs=pl.BlockSpec((tm, d), lambda i, s: (i, 0)),
            scratch_shapes=[pltpu.VMEM((TOP_K, tm, d), F32), pltpu.SemaphoreType.DMA],
        ),
        compiler_params=_cparams("arbitrary"),
        name="moe_combine",
    )(*args)


def _moe(geom, layer, h, idx, rank, gates, counts, x, mods, w_gu, b_gu, w_down, b_down, final_g):
    t = h.shape[0]
    tb = EXPERT_TILE
    n_blocks = (t * TOP_K) // tb + N_EXPERTS
    cnt = counts[0].astype(I32)
    n_tiles_e = (cnt + tb - 1) // tb
    tile_end = jnp.cumsum(n_tiles_e)
    tile_start = tile_end - n_tiles_e
    slot_start = (tile_start * tb).astype(I32)
    n_active = tile_end[-1:].astype(I32)
    j = jnp.arange(n_blocks, dtype=I32)
    j_eff = jnp.minimum(j, n_active[0] - 1)
    owner = tile_end[None, :] <= j_eff[:, None]
    be = jnp.minimum(jnp.sum(owner.astype(I32), axis=1), N_EXPERTS - 1)
    mine = (jnp.arange(N_EXPERTS, dtype=I32)[None, :] == be[:, None]).astype(I32)
    cnt_b = jnp.sum(mine * cnt[None, :], axis=1)
    start_b = jnp.sum(mine * tile_start[None, :], axis=1)
    valid = jnp.clip(cnt_b - (j_eff - start_b) * tb, 0, tb).astype(I32)
    idx_flat = idx.reshape(-1)
    rank_flat = rank.reshape(-1)
    xs = _dispatch(h, idx_flat, rank_flat, slot_start, n_blocks * tb)
    ys = _expert_ffn(layer, xs, be, valid, n_active, w_gu, b_gu, w_down, b_down)
    return _combine(geom, layer, ys, idx_flat, rank_flat, slot_start, x, gates, mods, final_g)


def kernel(x_prompt, x_sample, cache_b_k, cache_b_v, cache_c_k, cache_c_v, c, c_ctx,
           mod_w, mod_b, norm_mix, norm_ffn, even_w_in, even_w_out, even_sink,
           odd_w_in, odd_w_out, odd_q_norm, odd_k_norm, router_w, router_b,
           moe_w_gu, moe_b_gu, moe_w_down, moe_b_down, final_norm):
    bp, lp, d = x_prompt.shape
    bs, ls, _ = x_sample.shape
    past = cache_b_k.shape[2]
    depth = mod_w.shape[0]
    geom = _Geom(bp, lp, bs, ls)
    tp = geom.tp

    x = jnp.concatenate([x_prompt.reshape(tp, d), x_sample.reshape(bs * ls, d)], axis=0)
    cond = jnp.concatenate([c_ctx[None, :], c, jnp.zeros((MOD_ROWS - 1 - bs, d), F32)], axis=0)
    mods = _modulation(cond, mod_w, mod_b).reshape(depth * MOD_ROWS * N_MOD, 1, d)

    cn, sn = _dft_tables(A_GROUP_DIM)
    dft_chan = jnp.asarray(np.concatenate([cn, sn], axis=1), BF16)
    dft_p = [jnp.asarray(m, BF16) for m in _dft_tables(lp)]
    dft_s = [jnp.asarray(m, BF16) for m in _dft_tables(ls)]
    rope_b = [jnp.asarray(m) for m in _rope_tables(TOKEN_TILE, ls, B_HEAD_DIM)]
    rope_c = [jnp.asarray(m) for m in _rope_tables(TOKEN_TILE, ls, C_HEAD_DIM)]

    states = {"bk": [], "bv": [], "ck": [], "cv": []}
    for layer in range(depth):
        j = layer // 2
        g_mix = norm_mix[layer][None, :]
        g_ffn = norm_ffn[layer][None, :]
        if layer % 2 == 0:
            tc, ts, q, k, v = _in_projection(
                geom, layer, x, g_mix, mods, even_w_in[j].astype(BF16), rope_b[0], rope_b[1], [dft_chan],
                _proj_even_kernel, (A_WIDTH, A_WIDTH, B_Q_WIDTH, B_KV_WIDTH, B_KV_WIDTH),
                (BF16, BF16, BF16, F32, F32), "proj_even")
            states["bk"].append(k[:tp].reshape(bp, lp, B_KV_HEADS, B_HEAD_DIM))
            states["bv"].append(v[:tp].reshape(bp, lp, B_KV_HEADS, B_HEAD_DIM))
            four = _fourier_tokens(tc, ts, dft_p[0], dft_p[1], bp, lp, 0, None)
            four = _fourier_tokens(tc, ts, dft_s[0], dft_s[1], bs, ls, tp, four)
            sink = even_sink[j]
            common = dict(kv_heads=B_KV_HEADS, groups=B_HEADS // B_KV_HEADS, dh=B_HEAD_DIM)
            att = _attention(q, k, v, None, sink, None, n_seq=bp, seq_len=lp, row0=0, q_tile=lp,
                             window=None, **common)
            ctx = (cache_b_k[:, j].reshape(bs, past, B_KV_WIDTH), cache_b_v[:, j].reshape(bs, past, B_KV_WIDTH))
            att = _attention(q, k, v, ctx, sink, att, n_seq=bs, seq_len=ls, row0=tp, q_tile=ATTN_Q_TILE,
                             window=WINDOW, **common)
            w_out = even_w_out[j].astype(BF16)
            mixes = [four, att]
            w_parts = [w_out[:A_WIDTH], w_out[A_WIDTH:]]
        else:
            q, k, v = _in_projection(
                geom, layer, x, g_mix, mods, odd_w_in[j].astype(BF16), rope_c[0], rope_c[1],
                [odd_q_norm[j][None, :], odd_k_norm[j][None, :]],
                _proj_odd_kernel, (C_Q_WIDTH, C_KV_WIDTH, C_KV_WIDTH), (BF16, F32, F32), "proj_odd")
            states["ck"].append(k[:tp].reshape(bp, lp, C_KV_HEADS, C_HEAD_DIM))
            states["cv"].append(v[:tp].reshape(bp, lp, C_KV_HEADS, C_HEAD_DIM))
            common = dict(kv_heads=C_KV_HEADS, groups=C_HEADS // C_KV_HEADS, dh=C_HEAD_DIM, window=None)
            att = _attention(q, k, v, None, None, None, n_seq=bp, seq_len=lp, row0=0, q_tile=lp, **common)
            ctx = (cache_c_k[:, j].reshape(bs, past, C_KV_WIDTH), cache_c_v[:, j].reshape(bs, past, C_KV_WIDTH))
            att = _attention(q, k, v, ctx, None, att, n_seq=bs, seq_len=ls, row0=tp, q_tile=ATTN_Q_TILE, **common)
            mixes = [att]
            w_parts = [odd_w_out[j].astype(BF16)]
        x, h, idx, rank, gates, counts = _out_route(
            geom, layer, mixes, w_parts, x, mods, g_ffn, router_w[layer].astype(BF16), router_b[layer][None, :])
        final_g = final_norm[None, :] if layer == depth - 1 else None
        x = _moe(geom, layer, h, idx, rank, gates, counts, x, mods,
                 moe_w_gu, moe_b_gu, moe_w_down, moe_b_down, final_g)

    y_prompt = x[:tp].reshape(bp, lp, d)
    y_sample = x[tp:].reshape(bs, ls, d)
    return (y_prompt, y_sample,
            jnp.stack(states["bk"], axis=1), jnp.stack(states["bv"], axis=1),
            jnp.stack(states["ck"], axis=1), jnp.stack(states["cv"], axis=1))
```

```python
import functools

import numpy as np
import jax
import jax.numpy as jnp
from jax import lax
from jax.experimental import pallas as pl
from jax.experimental.pallas import tpu as pltpu

F32 = jnp.float32
BF16 = jnp.bfloat16
I32 = jnp.int32

GRID_W = 64
A_GROUPS = 4
A_GROUP_DIM = 128
A_WIDTH = A_GROUPS * A_GROUP_DIM
B_HEADS = 8
B_KV_HEADS = 2
B_HEAD_DIM = 64
B_Q_WIDTH = B_HEADS * B_HEAD_DIM
B_KV_WIDTH = B_KV_HEADS * B_HEAD_DIM
WINDOW = 128
C_HEADS = 8
C_KV_HEADS = 2
C_HEAD_DIM = 128
C_Q_WIDTH = C_HEADS * C_HEAD_DIM
C_KV_WIDTH = C_KV_HEADS * C_HEAD_DIM
ROPE_THETA = 10000.0
N_EXPERTS = 32
TOP_K = 4
SWIGLU_LIMIT = 7.0
SWIGLU_ALPHA = 1.702
EPS = 1e-6

LANES = 128
TOKEN_TILE = 256
EXPERT_TILE = 256
ATTN_Q_TILE = 128
SEG_ALIGN = 8
SORTED_ROWS = -(-(TOKEN_TILE * TOP_K + N_EXPERTS * (SEG_ALIGN - 1)) // LANES) * LANES
VMEM_LIMIT = 56 * 1024 * 1024
MASKED = -1e30
N_MOD = 6
MOD_ROWS = 8


def _cparams(*sem):
    return pltpu.CompilerParams(dimension_semantics=tuple(sem), vmem_limit_bytes=VMEM_LIMIT)


def _dot(a, b):
    return jnp.dot(a, b, preferred_element_type=F32)


def _dot_nt(a, b):
    return lax.dot_general(a, b, (((1,), (1,)), ((), ())), preferred_element_type=F32)


def _rms(x, g):
    return x * lax.rsqrt(jnp.mean(x * x, axis=-1, keepdims=True) + EPS) * g


def _dft_tables(n):
    j = np.arange(n, dtype=np.int64)
    ang = 2.0 * np.pi * ((j[:, None] * j[None, :]) % n).astype(np.float64) / n
    s = 1.0 / np.sqrt(n)
    return np.cos(ang) * s, np.sin(ang) * s


def _rope_tables(n_prompt_rows, n_latent, head_dim):
    quarter = head_dim // 4
    pos = np.arange(n_latent)
    row = (pos // GRID_W).astype(np.float32)
    col = (pos % GRID_W).astype(np.float32)
    inv = (np.float32(ROPE_THETA) ** (-np.arange(quarter, dtype=np.float32) / np.float32(quarter))).astype(np.float32)
    ang_row = (row[:, None] * inv[None, :]).astype(np.float32)
    ang_col = (col[:, None] * inv[None, :]).astype(np.float32)
    cos_h = np.concatenate([np.cos(ang_row)] * 2 + [np.cos(ang_col)] * 2, axis=1)
    sin_h = np.concatenate([-np.sin(ang_row), np.sin(ang_row), -np.sin(ang_col), np.sin(ang_col)], axis=1)
    reps = LANES // head_dim
    cos_l = np.tile(cos_h, (1, reps)).astype(np.float32)
    sin_l = np.tile(sin_h, (1, reps)).astype(np.float32)
    cos = np.concatenate([np.ones((n_prompt_rows, LANES), np.float32), cos_l], axis=0)
    sin = np.concatenate([np.zeros((n_prompt_rows, LANES), np.float32), sin_l], axis=0)
    return cos, sin


def _rope(x, cos, sin, quarter):
    lane = lax.broadcasted_iota(I32, (x.shape[0], LANES), 1)
    first = ((lane // quarter) % 2) == 0
    outs = []
    for c in range(x.shape[1] // LANES):
        xc = x[:, c * LANES:(c + 1) * LANES]
        partner = jnp.where(first, pltpu.roll(xc, LANES - quarter, 1), pltpu.roll(xc, quarter, 1))
        outs.append(xc * cos + partner * sin)
    return outs[0] if len(outs) == 1 else jnp.concatenate(outs, axis=1)


def _head_rms(x, g):
    outs = []
    for c in range(x.shape[1] // LANES):
        outs.append(_rms(x[:, c * LANES:(c + 1) * LANES], g))
    return outs[0] if len(outs) == 1 else jnp.concatenate(outs, axis=1)


def _mod_kernel(c_ref, w_ref, b_ref, o_ref):
    c = c_ref[...]
    s = c * (1.0 / (1.0 + jnp.exp(-c)))
    o_ref[0] = _dot(s.astype(BF16), w_ref[0].astype(BF16)) + b_ref[0]


def _modulation(cond, mod_w, mod_b):
    depth, d, n = mod_w.shape
    tn = 1536
    return pl.pallas_call(
        _mod_kernel,
        out_shape=jax.ShapeDtypeStruct((depth, MOD_ROWS, n), F32),
        grid=(depth, n // tn),
        in_specs=[
            pl.BlockSpec((MOD_ROWS, d), lambda l, j: (0, 0)),
            pl.BlockSpec((1, d, tn), lambda l, j: (l, 0, j)),
            pl.BlockSpec((1, 1, tn), lambda l, j: (l, 0, j)),
        ],
        out_specs=pl.BlockSpec((1, MOD_ROWS, tn), lambda l, j: (l, 0, j)),
        compiler_params=_cparams("arbitrary", "arbitrary"),
        name="modulation",
    )(cond, mod_w, mod_b.reshape(depth, 1, n))


class _Geom:
    def __init__(self, bp, lp, bs, ls):
        self.bp, self.lp, self.bs, self.ls = bp, lp, bs, ls
        self.tp = bp * lp
        self.t = bp * lp + bs * ls
        assert lp == TOKEN_TILE and ls % TOKEN_TILE == 0 and self.tp % ls == 0
        self.n_ptiles = self.tp // TOKEN_TILE
        self.tiles_per_lat = ls // TOKEN_TILE
        self.n_tiles = self.t // TOKEN_TILE

    def group(self, i):
        return jnp.where(i < self.n_ptiles, 0, 1 + (i - self.n_ptiles) // self.tiles_per_lat)

    def pos_block(self, i):
        return jnp.where(i < self.n_ptiles, 0, 1 + (i - self.n_ptiles) % self.tiles_per_lat)


def _mod_spec(geom, layer, which, d):
    def imap(i):
        return ((layer * MOD_ROWS + geom.group(i)) * N_MOD + which, 0, 0)
    return pl.BlockSpec((None, 1, d), imap)


def _proj_even_kernel(x_ref, g_ref, sh_ref, sc_ref, w_ref, cos_ref, sin_ref, dft_ref,
                      tc_ref, ts_ref, q_ref, k_ref, v_ref):
    h = _rms(x_ref[...], g_ref[...]) * (1.0 + sc_ref[...]) + sh_ref[...]
    p = _dot(h.astype(BF16), w_ref[...])
    cos = cos_ref[...]
    sin = sin_ref[...]
    dft = dft_ref[...]
    tcs, tss = [], []
    for g in range(A_GROUPS):
        t = _dot(p[:, g * A_GROUP_DIM:(g + 1) * A_GROUP_DIM].astype(BF16), dft)
        tcs.append(t[:, :A_GROUP_DIM])
        tss.append(t[:, A_GROUP_DIM:])
    tc_ref[...] = jnp.concatenate(tcs, axis=1).astype(BF16)
    ts_ref[...] = jnp.concatenate(tss, axis=1).astype(BF16)
    o = A_WIDTH
    q_ref[...] = _rope(p[:, o:o + B_Q_WIDTH], cos, sin, B_HEAD_DIM // 4).astype(BF16)
    o += B_Q_WIDTH
    k_ref[...] = _rope(p[:, o:o + B_KV_WIDTH], cos, sin, B_HEAD_DIM // 4)
    o += B_KV_WIDTH
    v_ref[...] = p[:, o:o + B_KV_WIDTH]


def _proj_odd_kernel(x_ref, g_ref, sh_ref, sc_ref, w_ref, cos_ref, sin_ref, qn_ref, kn_ref,
                     q_ref, k_ref, v_ref):
    h = _rms(x_ref[...], g_ref[...]) * (1.0 + sc_ref[...]) + sh_ref[...]
    p = _dot(h.astype(BF16), w_ref[...])
    cos = cos_ref[...]
    sin = sin_ref[...]
    q = _head_rms(p[:, :C_Q_WIDTH], qn_ref[...])
    k = _head_rms(p[:, C_Q_WIDTH:C_Q_WIDTH + C_KV_WIDTH], kn_ref[...])
    q_ref[...] = _rope(q, cos, sin, C_HEAD_DIM // 4).astype(BF16)
    k_ref[...] = _rope(k, cos, sin, C_HEAD_DIM // 4)
    v_ref[...] = p[:, C_Q_WIDTH + C_KV_WIDTH:]


def _in_projection(geom, layer, x, norm_g, mods, w, cos, sin, extras, kernel, out_widths, out_dtypes, name):
    t, d = x.shape
    tm = TOKEN_TILE
    n_out = w.shape[1]
    row = lambda i: (i, 0)
    const2 = lambda i: (0, 0)
    in_specs = [
        pl.BlockSpec((tm, d), row),
        pl.BlockSpec((1, d), const2),
        _mod_spec(geom, layer, 0, d),
        _mod_spec(geom, layer, 1, d),
        pl.BlockSpec((d, n_out), const2),
        pl.BlockSpec((tm, LANES), lambda i: (geom.pos_block(i), 0)),
        pl.BlockSpec((tm, LANES), lambda i: (geom.pos_block(i), 0)),
    ] + [pl.BlockSpec(e.shape, const2) for e in extras]
    return pl.pallas_call(
        kernel,
        out_shape=[jax.ShapeDtypeStruct((t, wd), dt) for wd, dt in zip(out_widths, out_dtypes)],
        grid=(geom.n_tiles,),
        in_specs=in_specs,
        out_specs=[pl.BlockSpec((tm, wd), row) for wd in out_widths],
        compiler_params=_cparams("arbitrary"),
        name=name,
    )(x, norm_g, mods, mods, w, cos, sin, *extras)


def _fourier_kernel(cl_ref, sl_ref, tc_ref, ts_ref, *rest):
    o_ref = rest[-1]
    o_ref[...] = (_dot(cl_ref[...], tc_ref[...]) - _dot(sl_ref[...], ts_ref[...])).astype(o_ref.dtype)


def _fourier_tokens(tc, ts, cl, sl, n_seq, seq_len, row0, prev):
    t, width = tc.shape
    tr = min(seq_len, 512)
    n_r = seq_len // tr
    assert row0 % seq_len == 0
    seq0 = row0 // seq_len
    out0 = row0 // tr
    in_specs = [
        pl.BlockSpec((tr, seq_len), lambda s, r: (r, 0)),
        pl.BlockSpec((tr, seq_len), lambda s, r: (r, 0)),
        pl.BlockSpec((seq_len, width), lambda s, r: (seq0 + s, 0)),
        pl.BlockSpec((seq_len, width), lambda s, r: (seq0 + s, 0)),
    ]
    args = [cl, sl, tc, ts]
    aliases = {}
    if prev is not None:
        in_specs.append(pl.BlockSpec(memory_space=pl.ANY))
        args.append(prev)
        aliases = {4: 0}
    return pl.pallas_call(
        _fourier_kernel,
        out_shape=jax.ShapeDtypeStruct((t, width), BF16),
        grid=(n_seq, n_r),
        in_specs=in_specs,
        out_specs=pl.BlockSpec((tr, width), lambda s, r: (out0 + s * n_r + r, 0)),
        input_output_aliases=aliases,
        compiler_params=_cparams("arbitrary", "arbitrary"),
        name="fourier_tokens",
    )(*args)


def _softmax_pv(q, parts, sink, scale):
    scores = []
    m = sink
    for k, _, mask in parts:
        s = _dot_nt(q, k) * scale
        if mask is not None:
            s = jnp.where(mask, s, MASKED)
        scores.append(s)
        mx = jnp.max(s, axis=-1, keepdims=True)
        m = mx if m is None else jnp.maximum(m, mx)
    den = None if sink is None else jnp.exp(sink - m)
    acc = None
    for (_, v, _), s in zip(parts, scores):
        e = jnp.exp(s - m)
        es = jnp.sum(e, axis=-1, keepdims=True)
        den = es if den is None else den + es
        o = _dot(e.astype(BF16), v)
        acc = o if acc is None else acc + o
    return acc / den


def _stack_heads(q, kv, groups, dh):
    return jnp.concatenate([q[:, (kv * groups + g) * dh:(kv * groups + g + 1) * dh] for g in range(groups)], axis=0)


def _sink_column(sink_ref, kv, groups, rows):
    return jnp.concatenate([jnp.full((rows, 1), sink_ref[kv * groups + g], F32) for g in range(groups)], axis=0)


def _attn_kernel(*refs, kv_heads, groups, dh, has_sink, has_ctx, window, q_tile, seq_len):
    refs = list(refs)
    sink_ref = refs.pop(0) if has_sink else None
    q_ref, k_ref, v_ref = refs[:3]
    ck_ref, cv_ref = (refs[3], refs[4]) if has_ctx else (None, None)
    o_ref = refs[-1]
    scale = dh ** -0.5
    q = q_ref[...]
    rows = q.shape[0]
    if window is None:
        k_loc = k_ref[...].astype(BF16)
        v_loc = v_ref[...].astype(BF16)
        mask = None
    else:
        n = pl.program_id(1)
        band = q_tile + 2 * window
        start = jnp.clip(n * q_tile - window, 0, seq_len - band)
        start = pl.multiple_of(start, LANES)
        k_loc = k_ref[pl.ds(start, band), :].astype(BF16)
        v_loc = v_ref[pl.ds(start, band), :].astype(BF16)
        qpos = n * q_tile + lax.broadcasted_iota(I32, (groups * rows, band), 0) % rows
        kpos = start + lax.broadcasted_iota(I32, (groups * rows, band), 1)
        mask = jnp.abs(kpos - qpos) <= window
    if has_ctx:
        k_ctx = ck_ref[...].astype(BF16)
        v_ctx = cv_ref[...].astype(BF16)
    outs = []
    for kv in range(kv_heads):
        sl = slice(kv * dh, (kv + 1) * dh)
        parts = [(k_loc[:, sl], v_loc[:, sl], mask)]
        if has_ctx:
            parts.append((k_ctx[:, sl], v_ctx[:, sl], None))
        sink = _sink_column(sink_ref, kv, groups, rows) if has_sink else None
        o = _softmax_pv(_stack_heads(q, kv, groups, dh), parts, sink, scale)
        outs.extend(o[g * rows:(g + 1) * rows] for g in range(groups))
    o_ref[...] = jnp.concatenate(outs, axis=1).astype(o_ref.dtype)


def _attention(q, k, v, ctx, sink, prev, *, n_seq, seq_len, row0, q_tile, kv_heads, groups, dh, window):
    t, qw = q.shape
    kw = k.shape[1]
    n_q = seq_len // q_tile
    assert row0 % seq_len == 0 and row0 % q_tile == 0
    seq0 = row0 // seq_len
    q0 = row0 // q_tile
    in_specs, args = [], []
    if sink is not None:
        in_specs.append(pl.BlockSpec(memory_space=pltpu.SMEM))
        args.append(sink)
    in_specs += [
        pl.BlockSpec((q_tile, qw), lambda s, n: (q0 + s * n_q + n, 0)),
        pl.BlockSpec((seq_len, kw), lambda s, n: (seq0 + s, 0)),
        pl.BlockSpec((seq_len, kw), lambda s, n: (seq0 + s, 0)),
    ]
    args += [q, k, v]
    if ctx is not None:
        p = ctx[0].shape[1]
        in_specs += [pl.BlockSpec((None, p, kw), lambda s, n: (s, 0, 0))] * 2
        args += list(ctx)
    aliases = {}
    if prev is not None:
        in_specs.append(pl.BlockSpec(memory_space=pl.ANY))
        aliases = {len(args): 0}
        args.append(prev)
    kern = functools.partial(
        _attn_kernel, kv_heads=kv_heads, groups=groups, dh=dh, has_sink=sink is not None,
        has_ctx=ctx is not None, window=window, q_tile=q_tile, seq_len=seq_len)
    return pl.pallas_call(
        kern,
        out_shape=jax.ShapeDtypeStruct((t, qw), BF16),
        grid=(n_seq, n_q),
        in_specs=in_specs,
        out_specs=pl.BlockSpec((q_tile, qw), lambda s, n: (q0 + s * n_q + n, 0)),
        input_output_aliases=aliases,
        compiler_params=_cparams("arbitrary", "arbitrary"),
        name="attention",
    )(*args)


def _out_route_kernel(*refs, n_mix):
    mix_refs = refs[:n_mix]
    w_refs = refs[n_mix:2 * n_mix]
    (x_ref, gate_ref, g2_ref, sh2_ref, sc2_ref, rw_ref, rb_ref,
     xo_ref, h_ref, pos_ref, gates_ref, rows_ref, off_ref) = refs[2 * n_mix:]

    acc = None
    for m_ref, w_ref in zip(mix_refs, w_refs):
        part = _dot(m_ref[...], w_ref[...])
        acc = part if acc is None else acc + part
    xn = x_ref[...] + gate_ref[...] * acc
    xo_ref[...] = xn
    h = _rms(xn, g2_ref[...]) * (1.0 + sc2_ref[...]) + sh2_ref[...]
    hb = h.astype(BF16)
    h_ref[...] = hb

    logits = _dot(hb, rw_ref[...]) + rb_ref[...]
    tm, ne = logits.shape
    lane = lax.broadcasted_iota(I32, (tm, ne), 1).astype(F32)
    lane4 = lax.broadcasted_iota(I32, (tm, TOP_K), 1)
    work = logits
    sels, vals = [], []
    for _ in range(TOP_K):
        mx = jnp.max(work, axis=-1, keepdims=True)
        first = jnp.min(jnp.where(work == mx, lane, float(ne)), axis=-1, keepdims=True)
        sel = lane == first
        work = jnp.where(sel, -jnp.inf, work)
        sels.append(sel)
        vals.append(mx)
    exps = [jnp.exp(v - vals[0]) for v in vals]
    den = exps[0] + exps[1] + exps[2] + exps[3]

    onehot = jnp.zeros((tm, ne), F32)
    for sel in sels:
        onehot = onehot + sel.astype(F32)
    r_i = lax.broadcasted_iota(I32, (tm, tm), 0)
    c_i = lax.broadcasted_iota(I32, (tm, tm), 1)
    before = jnp.where(c_i < r_i, 1.0, 0.0).astype(BF16)
    earlier = _dot(before, onehot.astype(BF16))

    cnt = jnp.sum(onehot, axis=0, keepdims=True)
    seg8 = jnp.floor((cnt + (SEG_ALIGN - 1.0)) * (1.0 / SEG_ALIGN))
    e_r = lax.broadcasted_iota(I32, (ne, ne), 0)
    e_c = lax.broadcasted_iota(I32, (ne, ne), 1)
    upper = jnp.where(e_r < e_c, 1.0, 0.0).astype(BF16)
    off8 = _dot(jnp.broadcast_to(seg8, (SEG_ALIGN, ne)).astype(BF16), upper)[0:1]
    seg_off = off8 * SEG_ALIGN
    base = seg_off + earlier

    pos_o = jnp.zeros((tm, TOP_K), I32)
    gate_o = jnp.zeros((tm, TOP_K), F32)
    for k in range(TOP_K):
        pos_k = jnp.sum(jnp.where(sels[k], base, 0.0), axis=-1, keepdims=True)
        pos_o = jnp.where(lane4 == k, pos_k.astype(I32), pos_o)
        gate_o = jnp.where(lane4 == k, exps[k] / den, gate_o)
    pos_ref[...] = pos_o
    gates_ref[...] = gate_o
    rows_ref[0] = (seg8 * SEG_ALIGN).astype(I32)
    off_ref[0] = seg_off.astype(I32)


def _out_route(geom, layer, mixes, w_parts, x, mods, norm_g, router_w, router_b):
    t, d = x.shape
    tm = TOKEN_TILE
    row = lambda i: (i, 0)
    const2 = lambda i: (0, 0)
    in_specs = [pl.BlockSpec((tm, m.shape[1]), row) for m in mixes]
    in_specs += [pl.BlockSpec(w.shape, const2) for w in w_parts]
    in_specs += [
        pl.BlockSpec((tm, d), row),
        _mod_spec(geom, layer, 2, d),
        pl.BlockSpec((1, d), const2),
        _mod_spec(geom, layer, 3, d),
        _mod_spec(geom, layer, 4, d),
        pl.BlockSpec(router_w.shape, const2),
        pl.BlockSpec((1, N_EXPERTS), const2),
    ]
    seg3 = lambda i: (i, 0, 0)
    out_shape = [
        jax.ShapeDtypeStruct((t, d), F32),
        jax.ShapeDtypeStruct((t, d), BF16),
        jax.ShapeDtypeStruct((t, TOP_K), I32),
        jax.ShapeDtypeStruct((t, TOP_K), F32),
        jax.ShapeDtypeStruct((geom.n_tiles, 1, N_EXPERTS), I32),
        jax.ShapeDtypeStruct((geom.n_tiles, 1, N_EXPERTS), I32),
    ]
    out_specs = [
        pl.BlockSpec((tm, d), row),
        pl.BlockSpec((tm, d), row),
        pl.BlockSpec((tm, TOP_K), row),
        pl.BlockSpec((tm, TOP_K), row),
        pl.BlockSpec((1, 1, N_EXPERTS), seg3),
        pl.BlockSpec((1, 1, N_EXPERTS), seg3),
    ]
    return pl.pallas_call(
        functools.partial(_out_route_kernel, n_mix=len(mixes)),
        out_shape=out_shape,
        grid=(geom.n_tiles,),
        in_specs=in_specs,
        out_specs=out_specs,
        compiler_params=_cparams("arbitrary"),
        name="out_route",
    )(*mixes, *w_parts, x, mods, norm_g, mods, mods, router_w, router_b)


def _segment_copies(i, rows_ref, off_ref, dst_ref, make_copy):
    def per_expert(e, total):
        a = i * N_EXPERTS + e
        n_chunks = rows_ref[a] // SEG_ALIGN
        local0 = off_ref[a]
        slot0 = dst_ref[a]

        def chunk(c, carry):
            local = pl.multiple_of(local0 + c * SEG_ALIGN, SEG_ALIGN)
            slot = pl.multiple_of(slot0 + c * SEG_ALIGN, SEG_ALIGN)
            make_copy(local, slot).start()
            return carry

        lax.fori_loop(0, n_chunks, chunk, 0)
        return total + n_chunks

    return lax.fori_loop(0, N_EXPERTS, per_expert, 0)


def _wait_copies(n, make_copy):
    def wait_one(c, carry):
        make_copy(0, 0).wait()
        return carry
    lax.fori_loop(0, n, wait_one, 0)


def _dispatch_kernel(rows_ref, off_ref, dst_ref, h_ref, pos_ref, xs_ref, sorted_ref, sem):
    i = pl.program_id(0)
    tm = h_ref.shape[0]
    n_sorted = sorted_ref.shape[0]
    pos = pos_ref[...]
    lane = lax.broadcasted_iota(I32, (tm, n_sorted), 1)
    hit = jnp.zeros((tm, n_sorted), F32)
    for k in range(TOP_K):
        hit = jnp.where(pos[:, k:k + 1] == lane, 1.0, hit)
    sorted_ref[...] = lax.dot_general(hit.astype(BF16), h_ref[...], (((0,), (0,)), ((), ())),
                                      preferred_element_type=F32)

    def make_copy(local, slot):
        return pltpu.make_async_copy(sorted_ref.at[pl.ds(local, SEG_ALIGN)], xs_ref.at[pl.ds(slot, SEG_ALIGN)], sem)

    n = _segment_copies(i, rows_ref, off_ref, dst_ref, make_copy)
    _wait_copies(n, make_copy)


def _dispatch(h, pos, seg_rows, seg_off, seg_dst, n_slots):
    t, d = h.shape
    tm = TOKEN_TILE
    return pl.pallas_call(
        _dispatch_kernel,
        out_shape=jax.ShapeDtypeStruct((n_slots, d), F32),
        grid_spec=pltpu.PrefetchScalarGridSpec(
            num_scalar_prefetch=3,
            grid=(t // tm,),
            in_specs=[
                pl.BlockSpec((tm, d), lambda i, *_: (i, 0)),
                pl.BlockSpec((tm, TOP_K), lambda i, *_: (i, 0)),
            ],
            out_specs=pl.BlockSpec(memory_space=pl.ANY),
            scratch_shapes=[pltpu.VMEM((SORTED_ROWS, d), F32), pltpu.SemaphoreType.DMA],
        ),
        compiler_params=_cparams("arbitrary"),
        name="moe_dispatch",
    )(seg_rows, seg_off, seg_dst, h, pos)


def _ffn_kernel(be_ref, valid_ref, nact_ref, x_ref, wgu_ref, bgu_ref, wd_ref, bd_ref, y_ref, wgu_bf, wd_bf):
    j = pl.program_id(0)
    d_ff = wd_ref.shape[1]

    @pl.when(j < nact_ref[0])
    def _():
        changed = jnp.logical_or(j == 0, be_ref[j] != be_ref[jnp.maximum(j - 1, 0)])

        @pl.when(changed)
        def _():
            chunk = 128
            def cast_gu(c, carry):
                r = pl.multiple_of(c * chunk, chunk)
                wgu_bf[pl.ds(r, chunk), :] = wgu_ref[0, pl.ds(r, chunk), :].astype(BF16)
                return carry
            lax.fori_loop(0, wgu_ref.shape[1] // chunk, cast_gu, 0)
            def cast_d(c, carry):
                r = pl.multiple_of(c * chunk, chunk)
                wd_bf[pl.ds(r, chunk), :] = wd_ref[0, pl.ds(r, chunk), :].astype(BF16)
                return carry
            lax.fori_loop(0, d_ff // chunk, cast_d, 0)

        rows = lax.broadcasted_iota(I32, (x_ref.shape[0], 1), 0)
        x = jnp.where(rows < valid_ref[j], x_ref[...], 0.0).astype(BF16)
        gu = _dot(x, wgu_bf[...]) + bgu_ref[0]
        gate = jnp.minimum(gu[:, :d_ff], SWIGLU_LIMIT)
        up = jnp.clip(gu[:, d_ff:], -SWIGLU_LIMIT, SWIGLU_LIMIT)
        act = (up + 1.0) * (gate * (1.0 / (1.0 + jnp.exp(-SWIGLU_ALPHA * gate))))
        y_ref[...] = _dot(act.astype(BF16), wd_bf[...]) + bd_ref[0]


def _expert_ffn(layer, xs, block_expert, block_valid, n_active, w_gu, b_gu, w_down, b_down):
    n_slots, d = xs.shape
    tb = EXPERT_TILE
    depth, ne, _, two_f = w_gu.shape
    d_ff = two_f // 2
    n_blocks = n_slots // tb
    row = lambda j, be, va, na: (jnp.minimum(j, na[0] - 1), 0)
    exp4 = lambda j, be, va, na: (layer, be[j], 0, 0)
    return pl.pallas_call(
        _ffn_kernel,
        out_shape=jax.ShapeDtypeStruct((n_slots, d), F32),
        grid_spec=pltpu.PrefetchScalarGridSpec(
            num_scalar_prefetch=3,
            grid=(n_blocks,),
            in_specs=[
                pl.BlockSpec((tb, d), row),
                pl.BlockSpec((None, 1, d, two_f), exp4),
                pl.BlockSpec((None, 1, 1, two_f), exp4),
                pl.BlockSpec((None, 1, d_ff, d), exp4),
                pl.BlockSpec((None, 1, 1, d), exp4),
            ],
            out_specs=pl.BlockSpec((tb, d), row),
            scratch_shapes=[pltpu.VMEM((d, two_f), BF16), pltpu.VMEM((d_ff, d), BF16)],
        ),
        compiler_params=_cparams("arbitrary"),
        name="expert_ffn",
    )(block_expert, block_valid, n_active, xs, w_gu, b_gu.reshape(depth, ne, 1, two_f),
      w_down, b_down.reshape(depth, ne, 1, d))


def _combine_kernel(rows_ref, off_ref, dst_ref, x_ref, pos_ref, gates_ref, mg_ref, *rest, final):
    if final:
        fn_ref, ys_ref, o_ref, buf, sem = rest
    else:
        ys_ref, o_ref, buf, sem = rest
    i = pl.program_id(0)
    tm = x_ref.shape[0]
    n_sorted = buf.shape[0]

    @pl.when(i == 0)
    def _():
        buf[...] = jnp.zeros_like(buf)

    def make_copy(local, slot):
        return pltpu.make_async_copy(ys_ref.at[pl.ds(slot, SEG_ALIGN)], buf.at[pl.ds(local, SEG_ALIGN)], sem)

    n = _segment_copies(i, rows_ref, off_ref, dst_ref, make_copy)
    pos = pos_ref[...]
    g = gates_ref[...]
    lane = lax.broadcasted_iota(I32, (tm, n_sorted), 1)
    weight = jnp.zeros((tm, n_sorted), F32)
    for k in range(TOP_K):
        weight = jnp.where(pos[:, k:k + 1] == lane, g[:, k:k + 1], weight)
    _wait_copies(n, make_copy)
    y = _dot(weight.astype(BF16), buf[...].astype(BF16))
    xn = x_ref[...] + mg_ref[...] * y
    if final:
        xn = _rms(xn, fn_ref[...])
    o_ref[...] = xn


def _combine(geom, layer, ys, pos, seg_rows, seg_off, seg_dst, x, gates, mods, final_g):
    t, d = x.shape
    tm = TOKEN_TILE
    final = final_g is not None

    def mod_imap(i, *_):
        return ((layer * MOD_ROWS + geom.group(i)) * N_MOD + 5, 0, 0)

    row = lambda i, *_: (i, 0)
    in_specs = [
        pl.BlockSpec((tm, d), row),
        pl.BlockSpec((tm, TOP_K), row),
        pl.BlockSpec((tm, TOP_K), row),
        pl.BlockSpec((None, 1, d), mod_imap),
    ]
    args = [seg_rows, seg_off, seg_dst, x, pos, gates, mods]
    if final:
        in_specs.append(pl.BlockSpec((1, d), lambda i, *_: (0, 0)))
        args.append(final_g)
    in_specs.append(pl.BlockSpec(memory_space=pl.ANY))
    args.append(ys)
    return pl.pallas_call(
        functools.partial(_combine_kernel, final=final),
        out_shape=jax.ShapeDtypeStruct((t, d), F32),
        grid_spec=pltpu.PrefetchScalarGridSpec(
            num_scalar_prefetch=3,
            grid=(t // tm,),
            in_specs=in_specs,
            out_specs=pl.BlockSpec((tm, d), row),
            scratch_shapes=[pltpu.VMEM((SORTED_ROWS, d), F32), pltpu.SemaphoreType.DMA],
        ),
        compiler_params=_cparams("arbitrary"),
        name="moe_combine",
    )(*args)


def _moe(geom, layer, h, pos, gates, seg_rows, seg_off, x, mods, w_gu, b_gu, w_down, b_down, final_g):
    t = h.shape[0]
    tb = EXPERT_TILE
    n_tok_tiles = seg_rows.shape[0]
    max_rows = t * TOP_K + n_tok_tiles * N_EXPERTS * (SEG_ALIGN - 1)
    n_blocks = -(-max_rows // tb) + N_EXPERTS
    rows = seg_rows[:, 0, :]
    cnt = jnp.sum(rows, axis=0)
    n_tiles_e = (cnt + tb - 1) // tb
    tile_end = jnp.cumsum(n_tiles_e)
    tile_start = tile_end - n_tiles_e
    seg_dst = (tile_start * tb)[None, :] + jnp.cumsum(rows, axis=0) - rows
    n_active = tile_end[-1:].astype(I32)
    j = jnp.arange(n_blocks, dtype=I32)
    j_eff = jnp.minimum(j, n_active[0] - 1)
    owner = tile_end[None, :] <= j_eff[:, None]
    be = jnp.minimum(jnp.sum(owner.astype(I32), axis=1), N_EXPERTS - 1)
    mine = (jnp.arange(N_EXPERTS, dtype=I32)[None, :] == be[:, None]).astype(I32)
    cnt_b = jnp.sum(mine * cnt[None, :], axis=1)
    start_b = jnp.sum(mine * tile_start[None, :], axis=1)
    valid = jnp.clip(cnt_b - (j_eff - start_b) * tb, 0, tb).astype(I32)
    rows_flat = rows.reshape(-1).astype(I32)
    off_flat = seg_off.reshape(-1).astype(I32)
    dst_flat = seg_dst.reshape(-1).astype(I32)
    xs = _dispatch(h, pos, rows_flat, off_flat, dst_flat, n_blocks * tb)
    ys = _expert_ffn(layer, xs, be, valid, n_active, w_gu, b_gu, w_down, b_down)
    return _combine(geom, layer, ys, pos, rows_flat, off_flat, dst_flat, x, gates, mods, final_g)


def kernel(x_prompt, x_sample, cache_b_k, cache_b_v, cache_c_k, cache_c_v, c, c_ctx,
           mod_w, mod_b, norm_mix, norm_ffn, even_w_in, even_w_out, even_sink,
           odd_w_in, odd_w_out, odd_q_norm, odd_k_norm, router_w, router_b,
           moe_w_gu, moe_b_gu, moe_w_down, moe_b_down, final_norm):
    bp, lp, d = x_prompt.shape
    bs, ls, _ = x_sample.shape
    past = cache_b_k.shape[2]
    depth = mod_w.shape[0]
    geom = _Geom(bp, lp, bs, ls)
    tp = geom.tp

    x = jnp.concatenate([x_prompt.reshape(tp, d), x_sample.reshape(bs * ls, d)], axis=0)
    cond = jnp.concatenate([c_ctx[None, :], c, jnp.zeros((MOD_ROWS - 1 - bs, d), F32)], axis=0)
    mods = _modulation(cond, mod_w, mod_b).reshape(depth * MOD_ROWS * N_MOD, 1, d)

    cn, sn = _dft_tables(A_GROUP_DIM)
    dft_chan = jnp.asarray(np.concatenate([cn, sn], axis=1), BF16)
    dft_p = [jnp.asarray(m, BF16) for m in _dft_tables(lp)]
    dft_s = [jnp.asarray(m, BF16) for m in _dft_tables(ls)]
    rope_b = [jnp.asarray(m) for m in _rope_tables(TOKEN_TILE, ls, B_HEAD_DIM)]
    rope_c = [jnp.asarray(m) for m in _rope_tables(TOKEN_TILE, ls, C_HEAD_DIM)]

    states = {"bk": [], "bv": [], "ck": [], "cv": []}
    for layer in range(depth):
        j = layer // 2
        g_mix = norm_mix[layer][None, :]
        g_ffn = norm_ffn[layer][None, :]
        if layer % 2 == 0:
            tc, ts, q, k, v = _in_projection(
                geom, layer, x, g_mix, mods, even_w_in[j].astype(BF16), rope_b[0], rope_b[1], [dft_chan],
                _proj_even_kernel, (A_WIDTH, A_WIDTH, B_Q_WIDTH, B_KV_WIDTH, B_KV_WIDTH),
                (BF16, BF16, BF16, F32, F32), "proj_even")
            states["bk"].append(k[:tp].reshape(bp, lp, B_KV_HEADS, B_HEAD_DIM))
            states["bv"].append(v[:tp].reshape(bp, lp, B_KV_HEADS, B_HEAD_DIM))
            four = _fourier_tokens(tc, ts, dft_p[0], dft_p[1], bp, lp, 0, None)
            four = _fourier_tokens(tc, ts, dft_s[0], dft_s[1], bs, ls, tp, four)
            sink = even_sink[j]
            common = dict(kv_heads=B_KV_HEADS, groups=B_HEADS // B_KV_HEADS, dh=B_HEAD_DIM)
            att = _attention(q, k, v, None, sink, None, n_seq=bp, seq_len=lp, row0=0, q_tile=lp,
                             window=None, **common)
            ctx = (cache_b_k[:, j].reshape(bs, past, B_KV_WIDTH), cache_b_v[:, j].reshape(bs, past, B_KV_WIDTH))
            att = _attention(q, k, v, ctx, sink, att, n_seq=bs, seq_len=ls, row0=tp, q_tile=ATTN_Q_TILE,
                             window=WINDOW, **common)
            w_out = even_w_out[j].astype(BF16)
            mixes = [four, att]
            w_parts = [w_out[:A_WIDTH], w_out[A_WIDTH:]]
        else:
            q, k, v = _in_projection(
                geom, layer, x, g_mix, mods, odd_w_in[j].astype(BF16), rope_c[0], rope_c[1],
                [odd_q_norm[j][None, :], odd_k_norm[j][None, :]],
                _proj_odd_kernel, (C_Q_WIDTH, C_KV_WIDTH, C_KV_WIDTH), (BF16, F32, F32), "proj_odd")
            states["ck"].append(k[:tp].reshape(bp, lp, C_KV_HEADS, C_HEAD_DIM))
            states["cv"].append(v[:tp].reshape(bp, lp, C_KV_HEADS, C_HEAD_DIM))
            common = dict(kv_heads=C_KV_HEADS, groups=C_HEADS // C_KV_HEADS, dh=C_HEAD_DIM, window=None)
            att = _attention(q, k, v, None, None, None, n_seq=bp, seq_len=lp, row0=0, q_tile=lp, **common)
            ctx = (cache_c_k[:, j].reshape(bs, past, C_KV_WIDTH), cache_c_v[:, j].reshape(bs, past, C_KV_WIDTH))
            att = _attention(q, k, v, ctx, None, att, n_seq=bs, seq_len=ls, row0=tp, q_tile=ATTN_Q_TILE, **common)
            mixes = [att]
            w_parts = [odd_w_out[j].astype(BF16)]
        x, h, pos, gates, seg_rows, seg_off = _out_route(
            geom, layer, mixes, w_parts, x, mods, g_ffn, router_w[layer].astype(BF16), router_b[layer][None, :])
        final_g = final_norm[None, :] if layer == depth - 1 else None
        x = _moe(geom, layer, h, pos, gates, seg_rows, seg_off, x, mods,
                 moe_w_gu, moe_b_gu, moe_w_down, moe_b_down, final_g)

    y_prompt = x[:tp].reshape(bp, lp, d)
    y_sample = x[tp:].reshape(bs, ls, d)
    return (y_prompt, y_sample,
            jnp.stack(states["bk"], axis=1), jnp.stack(states["bv"], axis=1),
            jnp.stack(states["ck"], axis=1), jnp.stack(states["cv"], axis=1))
```

```python
import functools

import numpy as np
import jax
import jax.numpy as jnp
from jax import lax
from jax.experimental import pallas as pl
from jax.experimental.pallas import tpu as pltpu

F32 = jnp.float32
BF16 = jnp.bfloat16
I32 = jnp.int32

GRID_W = 64
A_GROUPS = 4
A_GROUP_DIM = 128
A_WIDTH = A_GROUPS * A_GROUP_DIM
B_HEADS = 8
B_KV_HEADS = 2
B_HEAD_DIM = 64
B_Q_WIDTH = B_HEADS * B_HEAD_DIM
B_KV_WIDTH = B_KV_HEADS * B_HEAD_DIM
WINDOW = 128
C_HEADS = 8
C_KV_HEADS = 2
C_HEAD_DIM = 128
C_Q_WIDTH = C_HEADS * C_HEAD_DIM
C_KV_WIDTH = C_KV_HEADS * C_HEAD_DIM
ROPE_THETA = 10000.0
N_EXPERTS = 32
TOP_K = 4
SWIGLU_LIMIT = 7.0
SWIGLU_ALPHA = 1.702
EPS = 1e-6

LANES = 128
TOKEN_TILE = 256
EXPERT_TILE = 256
ATTN_Q_TILE = 128
SEG_ALIGN = 8
SORTED_ROWS = -(-(TOKEN_TILE * TOP_K + N_EXPERTS * (SEG_ALIGN - 1)) // LANES) * LANES
VMEM_LIMIT = 56 * 1024 * 1024
MASKED = -1e30
N_MOD = 6
MOD_ROWS = 8


def _cparams(*sem):
    return pltpu.CompilerParams(dimension_semantics=tuple(sem), vmem_limit_bytes=VMEM_LIMIT)


def _dot(a, b):
    return jnp.dot(a, b, preferred_element_type=F32)


def _dot_nt(a, b):
    return lax.dot_general(a, b, (((1,), (1,)), ((), ())), preferred_element_type=F32)


def _rms(x, g):
    return x * lax.rsqrt(jnp.mean(x * x, axis=-1, keepdims=True) + EPS) * g


def _dft_tables(n):
    j = np.arange(n, dtype=np.int64)
    ang = 2.0 * np.pi * ((j[:, None] * j[None, :]) % n).astype(np.float64) / n
    s = 1.0 / np.sqrt(n)
    return np.cos(ang) * s, np.sin(ang) * s


def _rope_tables(n_prompt_rows, n_latent, head_dim):
    quarter = head_dim // 4
    pos = np.arange(n_latent)
    row = (pos // GRID_W).astype(np.float32)
    col = (pos % GRID_W).astype(np.float32)
    inv = (np.float32(ROPE_THETA) ** (-np.arange(quarter, dtype=np.float32) / np.float32(quarter))).astype(np.float32)
    ang_row = (row[:, None] * inv[None, :]).astype(np.float32)
    ang_col = (col[:, None] * inv[None, :]).astype(np.float32)
    cos_h = np.concatenate([np.cos(ang_row)] * 2 + [np.cos(ang_col)] * 2, axis=1)
    sin_h = np.concatenate([-np.sin(ang_row), np.sin(ang_row), -np.sin(ang_col), np.sin(ang_col)], axis=1)
    reps = LANES // head_dim
    cos_l = np.tile(cos_h, (1, reps)).astype(np.float32)
    sin_l = np.tile(sin_h, (1, reps)).astype(np.float32)
    cos = np.concatenate([np.ones((n_prompt_rows, LANES), np.float32), cos_l], axis=0)
    sin = np.concatenate([np.zeros((n_prompt_rows, LANES), np.float32), sin_l], axis=0)
    return cos, sin


def _rope(x, cos, sin, quarter):
    lane = lax.broadcasted_iota(I32, (x.shape[0], LANES), 1)
    first = ((lane // quarter) % 2) == 0
    outs = []
    for c in range(x.shape[1] // LANES):
        xc = x[:, c * LANES:(c + 1) * LANES]
        partner = jnp.where(first, pltpu.roll(xc, LANES - quarter, 1), pltpu.roll(xc, quarter, 1))
        outs.append(xc * cos + partner * sin)
    return outs[0] if len(outs) == 1 else jnp.concatenate(outs, axis=1)


def _head_rms(x, g):
    outs = []
    for c in range(x.shape[1] // LANES):
        outs.append(_rms(x[:, c * LANES:(c + 1) * LANES], g))
    return outs[0] if len(outs) == 1 else jnp.concatenate(outs, axis=1)


def _mod_kernel(c_ref, w_ref, b_ref, o_ref):
    c = c_ref[...]
    s = c * (1.0 / (1.0 + jnp.exp(-c)))
    o_ref[0] = _dot(s.astype(BF16), w_ref[0].astype(BF16)) + b_ref[0]


def _modulation(cond, mod_w, mod_b):
    depth, d, n = mod_w.shape
    tn = 1536
    return pl.pallas_call(
        _mod_kernel,
        out_shape=jax.ShapeDtypeStruct((depth, MOD_ROWS, n), F32),
        grid=(depth, n // tn),
        in_specs=[
            pl.BlockSpec((MOD_ROWS, d), lambda l, j: (0, 0)),
            pl.BlockSpec((1, d, tn), lambda l, j: (l, 0, j)),
            pl.BlockSpec((1, 1, tn), lambda l, j: (l, 0, j)),
        ],
        out_specs=pl.BlockSpec((1, MOD_ROWS, tn), lambda l, j: (l, 0, j)),
        compiler_params=_cparams("arbitrary", "arbitrary"),
        name="modulation",
    )(cond, mod_w, mod_b.reshape(depth, 1, n))


class _Geom:
    def __init__(self, bp, lp, bs, ls):
        self.bp, self.lp, self.bs, self.ls = bp, lp, bs, ls
        self.tp = bp * lp
        self.t = bp * lp + bs * ls
        assert lp == TOKEN_TILE and ls % TOKEN_TILE == 0 and self.tp % ls == 0
        self.n_ptiles = self.tp // TOKEN_TILE
        self.tiles_per_lat = ls // TOKEN_TILE
        self.n_tiles = self.t // TOKEN_TILE

    def group(self, i):
        return jnp.where(i < self.n_ptiles, 0, 1 + (i - self.n_ptiles) // self.tiles_per_lat)

    def pos_block(self, i):
        return jnp.where(i < self.n_ptiles, 0, 1 + (i - self.n_ptiles) % self.tiles_per_lat)


def _mod_spec(geom, layer, which, d):
    def imap(i):
        return ((layer * MOD_ROWS + geom.group(i)) * N_MOD + which, 0, 0)
    return pl.BlockSpec((None, 1, d), imap)


def _proj_even_kernel(x_ref, g_ref, sh_ref, sc_ref, w_ref, cos_ref, sin_ref, dft_ref,
                      tc_ref, ts_ref, q_ref, k_ref, v_ref):
    h = _rms(x_ref[...], g_ref[...]) * (1.0 + sc_ref[...]) + sh_ref[...]
    p = _dot(h.astype(BF16), w_ref[...])
    cos = cos_ref[...]
    sin = sin_ref[...]
    dft = dft_ref[...]
    tcs, tss = [], []
    for g in range(A_GROUPS):
        t = _dot(p[:, g * A_GROUP_DIM:(g + 1) * A_GROUP_DIM].astype(BF16), dft)
        tcs.append(t[:, :A_GROUP_DIM])
        tss.append(t[:, A_GROUP_DIM:])
    tc_ref[...] = jnp.concatenate(tcs, axis=1).astype(BF16)
    ts_ref[...] = jnp.concatenate(tss, axis=1).astype(BF16)
    o = A_WIDTH
    q_ref[...] = _rope(p[:, o:o + B_Q_WIDTH], cos, sin, B_HEAD_DIM // 4).astype(BF16)
    o += B_Q_WIDTH
    k_ref[...] = _rope(p[:, o:o + B_KV_WIDTH], cos, sin, B_HEAD_DIM // 4)
    o += B_KV_WIDTH
    v_ref[...] = p[:, o:o + B_KV_WIDTH]


def _proj_odd_kernel(x_ref, g_ref, sh_ref, sc_ref, w_ref, cos_ref, sin_ref, qn_ref, kn_ref,
                     q_ref, k_ref, v_ref):
    h = _rms(x_ref[...], g_ref[...]) * (1.0 + sc_ref[...]) + sh_ref[...]
    p = _dot(h.astype(BF16), w_ref[...])
    cos = cos_ref[...]
    sin = sin_ref[...]
    q = _head_rms(p[:, :C_Q_WIDTH], qn_ref[...])
    k = _head_rms(p[:, C_Q_WIDTH:C_Q_WIDTH + C_KV_WIDTH], kn_ref[...])
    q_ref[...] = _rope(q, cos, sin, C_HEAD_DIM // 4).astype(BF16)
    k_ref[...] = _rope(k, cos, sin, C_HEAD_DIM // 4)
    v_ref[...] = p[:, C_Q_WIDTH + C_KV_WIDTH:]


def _in_projection(geom, layer, x, norm_g, mods, w, cos, sin, extras, kernel, out_widths, out_dtypes, name):
    t, d = x.shape
    tm = TOKEN_TILE
    n_out = w.shape[1]
    row = lambda i: (i, 0)
    const2 = lambda i: (0, 0)
    in_specs = [
        pl.BlockSpec((tm, d), row),
        pl.BlockSpec((1, d), const2),
        _mod_spec(geom, layer, 0, d),
        _mod_spec(geom, layer, 1, d),
        pl.BlockSpec((d, n_out), const2),
        pl.BlockSpec((tm, LANES), lambda i: (geom.pos_block(i), 0)),
        pl.BlockSpec((tm, LANES), lambda i: (geom.pos_block(i), 0)),
    ] + [pl.BlockSpec(e.shape, const2) for e in extras]
    return pl.pallas_call(
        kernel,
        out_shape=[jax.ShapeDtypeStruct((t, wd), dt) for wd, dt in zip(out_widths, out_dtypes)],
        grid=(geom.n_tiles,),
        in_specs=in_specs,
        out_specs=[pl.BlockSpec((tm, wd), row) for wd in out_widths],
        compiler_params=_cparams("arbitrary"),
        name=name,
    )(x, norm_g, mods, mods, w, cos, sin, *extras)


def _fourier_kernel(cl_ref, sl_ref, tc_ref, ts_ref, *rest):
    o_ref = rest[-1]
    o_ref[...] = (_dot(cl_ref[...], tc_ref[...]) - _dot(sl_ref[...], ts_ref[...])).astype(o_ref.dtype)


def _fourier_tokens(tc, ts, cl, sl, n_seq, seq_len, row0, prev):
    t, width = tc.shape
    tr = min(seq_len, 512)
    n_r = seq_len // tr
    assert row0 % seq_len == 0
    seq0 = row0 // seq_len
    out0 = row0 // tr
    in_specs = [
        pl.BlockSpec((tr, seq_len), lambda s, r: (r, 0)),
        pl.BlockSpec((tr, seq_len), lambda s, r: (r, 0)),
        pl.BlockSpec((seq_len, width), lambda s, r: (seq0 + s, 0)),
        pl.BlockSpec((seq_len, width), lambda s, r: (seq0 + s, 0)),
    ]
    args = [cl, sl, tc, ts]
    aliases = {}
    if prev is not None:
        in_specs.append(pl.BlockSpec(memory_space=pl.ANY))
        args.append(prev)
        aliases = {4: 0}
    return pl.pallas_call(
        _fourier_kernel,
        out_shape=jax.ShapeDtypeStruct((t, width), BF16),
        grid=(n_seq, n_r),
        in_specs=in_specs,
        out_specs=pl.BlockSpec((tr, width), lambda s, r: (out0 + s * n_r + r, 0)),
        input_output_aliases=aliases,
        compiler_params=_cparams("arbitrary", "arbitrary"),
        name="fourier_tokens",
    )(*args)


def _softmax_pv(q, parts, sink, scale):
    scores = []
    m = sink
    for k, _, mask in parts:
        s = _dot_nt(q, k) * scale
        if mask is not None:
            s = jnp.where(mask, s, MASKED)
        scores.append(s)
        mx = jnp.max(s, axis=-1, keepdims=True)
        m = mx if m is None else jnp.maximum(m, mx)
    den = None if sink is None else jnp.exp(sink - m)
    acc = None
    for (_, v, _), s in zip(parts, scores):
        e = jnp.exp(s - m)
        es = jnp.sum(e, axis=-1, keepdims=True)
        den = es if den is None else den + es
        o = _dot(e.astype(BF16), v)
        acc = o if acc is None else acc + o
    return acc / den


def _stack_heads(q, kv, groups, dh):
    return jnp.concatenate([q[:, (kv * groups + g) * dh:(kv * groups + g + 1) * dh] for g in range(groups)], axis=0)


def _sink_column(sink_ref, kv, groups, rows):
    return jnp.concatenate([jnp.full((rows, 1), sink_ref[kv * groups + g], F32) for g in range(groups)], axis=0)


def _attn_kernel(*refs, kv_heads, groups, dh, has_sink, has_ctx, window, q_tile, seq_len):
    refs = list(refs)
    sink_ref = refs.pop(0) if has_sink else None
    q_ref, k_ref, v_ref = refs[:3]
    ck_ref, cv_ref = (refs[3], refs[4]) if has_ctx else (None, None)
    o_ref = refs[-1]
    scale = dh ** -0.5
    q = q_ref[...]
    rows = q.shape[0]
    if window is None:
        k_loc = k_ref[...].astype(BF16)
        v_loc = v_ref[...].astype(BF16)
        mask = None
    else:
        n = pl.program_id(1)
        band = q_tile + 2 * window
        start = jnp.clip(n * q_tile - window, 0, seq_len - band)
        start = pl.multiple_of(start, LANES)
        k_loc = k_ref[pl.ds(start, band), :].astype(BF16)
        v_loc = v_ref[pl.ds(start, band), :].astype(BF16)
        qpos = n * q_tile + lax.broadcasted_iota(I32, (groups * rows, band), 0) % rows
        kpos = start + lax.broadcasted_iota(I32, (groups * rows, band), 1)
        mask = jnp.abs(kpos - qpos) <= window
    if has_ctx:
        k_ctx = ck_ref[...].astype(BF16)
        v_ctx = cv_ref[...].astype(BF16)
    outs = []
    for kv in range(kv_heads):
        sl = slice(kv * dh, (kv + 1) * dh)
        parts = [(k_loc[:, sl], v_loc[:, sl], mask)]
        if has_ctx:
            parts.append((k_ctx[:, sl], v_ctx[:, sl], None))
        sink = _sink_column(sink_ref, kv, groups, rows) if has_sink else None
        o = _softmax_pv(_stack_heads(q, kv, groups, dh), parts, sink, scale)
        outs.extend(o[g * rows:(g + 1) * rows] for g in range(groups))
    o_ref[...] = jnp.concatenate(outs, axis=1).astype(o_ref.dtype)


def _attention(q, k, v, ctx, sink, prev, *, n_seq, seq_len, row0, q_tile, kv_heads, groups, dh, window):
    t, qw = q.shape
    kw = k.shape[1]
    n_q = seq_len // q_tile
    assert row0 % seq_len == 0 and row0 % q_tile == 0
    seq0 = row0 // seq_len
    q0 = row0 // q_tile
    in_specs, args = [], []
    if sink is not None:
        in_specs.append(pl.BlockSpec(memory_space=pltpu.SMEM))
        args.append(sink)
    in_specs += [
        pl.BlockSpec((q_tile, qw), lambda s, n: (q0 + s * n_q + n, 0)),
        pl.BlockSpec((seq_len, kw), lambda s, n: (seq0 + s, 0)),
        pl.BlockSpec((seq_len, kw), lambda s, n: (seq0 + s, 0)),
    ]
    args += [q, k, v]
    if ctx is not None:
        p = ctx[0].shape[1]
        in_specs += [pl.BlockSpec((None, p, kw), lambda s, n: (s, 0, 0))] * 2
        args += list(ctx)
    aliases = {}
    if prev is not None:
        in_specs.append(pl.BlockSpec(memory_space=pl.ANY))
        aliases = {len(args): 0}
        args.append(prev)
    kern = functools.partial(
        _attn_kernel, kv_heads=kv_heads, groups=groups, dh=dh, has_sink=sink is not None,
        has_ctx=ctx is not None, window=window, q_tile=q_tile, seq_len=seq_len)
    return pl.pallas_call(
        kern,
        out_shape=jax.ShapeDtypeStruct((t, qw), BF16),
        grid=(n_seq, n_q),
        in_specs=in_specs,
        out_specs=pl.BlockSpec((q_tile, qw), lambda s, n: (q0 + s * n_q + n, 0)),
        input_output_aliases=aliases,
        compiler_params=_cparams("arbitrary", "arbitrary"),
        name="attention",
    )(*args)


def _out_route_kernel(*refs, n_mix):
    mix_refs = refs[:n_mix]
    w_refs = refs[n_mix:2 * n_mix]
    (x_ref, gate_ref, g2_ref, sh2_ref, sc2_ref, rw_ref, rb_ref,
     xo_ref, h_ref, pos_ref, gates_ref, rows_ref, off_ref) = refs[2 * n_mix:]

    acc = None
    for m_ref, w_ref in zip(mix_refs, w_refs):
        part = _dot(m_ref[...], w_ref[...])
        acc = part if acc is None else acc + part
    xn = x_ref[...] + gate_ref[...] * acc
    xo_ref[...] = xn
    h = _rms(xn, g2_ref[...]) * (1.0 + sc2_ref[...]) + sh2_ref[...]
    hb = h.astype(BF16)
    h_ref[...] = hb

    logits = _dot(hb, rw_ref[...]) + rb_ref[...]
    tm, ne = logits.shape
    lane = lax.broadcasted_iota(I32, (tm, ne), 1).astype(F32)
    lane4 = lax.broadcasted_iota(I32, (tm, TOP_K), 1)
    work = logits
    sels, vals = [], []
    for _ in range(TOP_K):
        mx = jnp.max(work, axis=-1, keepdims=True)
        first = jnp.min(jnp.where(work == mx, lane, float(ne)), axis=-1, keepdims=True)
        sel = lane == first
        work = jnp.where(sel, -jnp.inf, work)
        sels.append(sel)
        vals.append(mx)
    exps = [jnp.exp(v - vals[0]) for v in vals]
    den = exps[0] + exps[1] + exps[2] + exps[3]

    onehot = jnp.zeros((tm, ne), F32)
    for sel in sels:
        onehot = onehot + sel.astype(F32)
    r_i = lax.broadcasted_iota(I32, (tm, tm), 0)
    c_i = lax.broadcasted_iota(I32, (tm, tm), 1)
    before = jnp.where(c_i < r_i, 1.0, 0.0).astype(BF16)
    earlier = _dot(before, onehot.astype(BF16))

    cnt = jnp.sum(onehot, axis=0, keepdims=True)
    seg8 = jnp.floor((cnt + (SEG_ALIGN - 1.0)) * (1.0 / SEG_ALIGN))
    e_r = lax.broadcasted_iota(I32, (ne, ne), 0)
    e_c = lax.broadcasted_iota(I32, (ne, ne), 1)
    upper = jnp.where(e_r < e_c, 1.0, 0.0).astype(BF16)
    off8 = _dot(jnp.broadcast_to(seg8, (SEG_ALIGN, ne)).astype(BF16), upper)[0:1]
    seg_off = off8 * SEG_ALIGN
    base = seg_off + earlier

    pos_o = jnp.zeros((tm, TOP_K), I32)
    gate_o = jnp.zeros((tm, TOP_K), F32)
    for k in range(TOP_K):
        pos_k = jnp.sum(jnp.where(sels[k], base, 0.0), axis=-1, keepdims=True)
        pos_o = jnp.where(lane4 == k, pos_k.astype(I32), pos_o)
        gate_o = jnp.where(lane4 == k, exps[k] / den, gate_o)
    pos_ref[...] = pos_o
    gates_ref[...] = gate_o
    rows_ref[0] = (seg8 * SEG_ALIGN).astype(I32)
    off_ref[0] = seg_off.astype(I32)


def _out_route(geom, layer, mixes, w_parts, x, mods, norm_g, router_w, router_b):
    t, d = x.shape
    tm = TOKEN_TILE
    row = lambda i: (i, 0)
    const2 = lambda i: (0, 0)
    in_specs = [pl.BlockSpec((tm, m.shape[1]), row) for m in mixes]
    in_specs += [pl.BlockSpec(w.shape, const2) for w in w_parts]
    in_specs += [
        pl.BlockSpec((tm, d), row),
        _mod_spec(geom, layer, 2, d),
        pl.BlockSpec((1, d), const2),
        _mod_spec(geom, layer, 3, d),
        _mod_spec(geom, layer, 4, d),
        pl.BlockSpec(router_w.shape, const2),
        pl.BlockSpec((1, N_EXPERTS), const2),
    ]
    seg3 = lambda i: (i, 0, 0)
    out_shape = [
        jax.ShapeDtypeStruct((t, d), F32),
        jax.ShapeDtypeStruct((t, d), BF16),
        jax.ShapeDtypeStruct((t, TOP_K), I32),
        jax.ShapeDtypeStruct((t, TOP_K), F32),
        jax.ShapeDtypeStruct((geom.n_tiles, 1, N_EXPERTS), I32),
        jax.ShapeDtypeStruct((geom.n_tiles, 1, N_EXPERTS), I32),
    ]
    out_specs = [
        pl.BlockSpec((tm, d), row),
        pl.BlockSpec((tm, d), row),
        pl.BlockSpec((tm, TOP_K), row),
        pl.BlockSpec((tm, TOP_K), row),
        pl.BlockSpec((1, 1, N_EXPERTS), seg3),
        pl.BlockSpec((1, 1, N_EXPERTS), seg3),
    ]
    return pl.pallas_call(
        functools.partial(_out_route_kernel, n_mix=len(mixes)),
        out_shape=out_shape,
        grid=(geom.n_tiles,),
        in_specs=in_specs,
        out_specs=out_specs,
        compiler_params=_cparams("arbitrary"),
        name="out_route",
    )(*mixes, *w_parts, x, mods, norm_g, mods, mods, router_w, router_b)


def _segment_copies(i, rows_ref, off_ref, dst_ref, make_copy):
    def per_expert(e, total):
        a = i * N_EXPERTS + e
        n_chunks = rows_ref[a] // SEG_ALIGN
        local0 = off_ref[a]
        slot0 = dst_ref[a]

        def chunk(c, carry):
            local = pl.multiple_of(local0 + c * SEG_ALIGN, SEG_ALIGN)
            slot = pl.multiple_of(slot0 + c * SEG_ALIGN, SEG_ALIGN)
            make_copy(local, slot).start()
            return carry

        lax.fori_loop(0, n_chunks, chunk, 0)
        return total + n_chunks

    return lax.fori_loop(0, N_EXPERTS, per_expert, 0)


def _wait_copies(n, make_copy):
    def wait_one(c, carry):
        make_copy(0, 0).wait()
        return carry
    lax.fori_loop(0, n, wait_one, 0)


def _dispatch_kernel(rows_ref, off_ref, dst_ref, h_ref, pos_ref, xs_ref, sorted_ref, sem):
    i = pl.program_id(0)
    tm = h_ref.shape[0]
    n_sorted = sorted_ref.shape[0]
    pos = pos_ref[...]
    lane = lax.broadcasted_iota(I32, (tm, n_sorted), 1)
    hit = jnp.zeros((tm, n_sorted), F32)
    for k in range(TOP_K):
        hit = jnp.where(pos[:, k:k + 1] == lane, 1.0, hit)
    sorted_ref[...] = lax.dot_general(hit.astype(BF16), h_ref[...], (((0,), (0,)), ((), ())),
                                      preferred_element_type=F32)

    def make_copy(local, slot):
        return pltpu.make_async_copy(sorted_ref.at[pl.ds(local, SEG_ALIGN)], xs_ref.at[pl.ds(slot, SEG_ALIGN)], sem)

    n = _segment_copies(i, rows_ref, off_ref, dst_ref, make_copy)
    _wait_copies(n, make_copy)


def _dispatch(h, pos, seg_rows, seg_off, seg_dst, n_slots):
    t, d = h.shape
    tm = TOKEN_TILE
    return pl.pallas_call(
        _dispatch_kernel,
        out_shape=jax.ShapeDtypeStruct((n_slots, d), F32),
        grid_spec=pltpu.PrefetchScalarGridSpec(
            num_scalar_prefetch=3,
            grid=(t // tm,),
            in_specs=[
                pl.BlockSpec((tm, d), lambda i, *_: (i, 0)),
                pl.BlockSpec((tm, TOP_K), lambda i, *_: (i, 0)),
            ],
            out_specs=pl.BlockSpec(memory_space=pl.ANY),
            scratch_shapes=[pltpu.VMEM((SORTED_ROWS, d), F32), pltpu.SemaphoreType.DMA],
        ),
        compiler_params=_cparams("arbitrary"),
        name="moe_dispatch",
    )(seg_rows, seg_off, seg_dst, h, pos)


def _ffn_kernel(rows_ref, start_ref, wgu_ref, bgu_ref, wd_ref, bd_ref, xs_ref, ys_ref,
                wgu_bf, wd_bf, xbuf, ybuf, sem_in, sem_out):
    e = pl.program_id(0)
    d_ff = wd_ref.shape[1]
    tb = xbuf.shape[1]
    n_rows = rows_ref[e]
    n_tiles = (n_rows + tb - 1) // tb
    base = start_ref[e]

    chunk = 128
    def cast_gu(c, carry):
        r = pl.multiple_of(c * chunk, chunk)
        wgu_bf[pl.ds(r, chunk), :] = wgu_ref[0, pl.ds(r, chunk), :].astype(BF16)
        return carry
    lax.fori_loop(0, wgu_ref.shape[1] // chunk, cast_gu, 0)
    def cast_d(c, carry):
        r = pl.multiple_of(c * chunk, chunk)
        wd_bf[pl.ds(r, chunk), :] = wd_ref[0, pl.ds(r, chunk), :].astype(BF16)
        return carry
    lax.fori_loop(0, d_ff // chunk, cast_d, 0)

    def x_copy(s, slot):
        r = pl.multiple_of(base + s * tb, tb)
        return pltpu.make_async_copy(xs_ref.at[pl.ds(r, tb)], xbuf.at[slot], sem_in.at[slot])

    def y_copy(s, slot):
        r = pl.multiple_of(base + s * tb, tb)
        return pltpu.make_async_copy(ybuf.at[slot], ys_ref.at[pl.ds(r, tb)], sem_out.at[slot])

    @pl.when(n_tiles > 0)
    def _():
        x_copy(0, 0).start()

    def tile(s, carry):
        slot = s % 2
        x_copy(s, slot).wait()

        @pl.when(s + 1 < n_tiles)
        def _():
            x_copy(s + 1, 1 - slot).start()

        @pl.when(s >= 2)
        def _():
            y_copy(s - 2, slot).wait()

        rows = s * tb + lax.broadcasted_iota(I32, (tb, 1), 0)
        x = jnp.where(rows < n_rows, xbuf[slot], 0.0).astype(BF16)
        gu = _dot(x, wgu_bf[...]) + bgu_ref[0]
        gate = jnp.minimum(gu[:, :d_ff], SWIGLU_LIMIT)
        up = jnp.clip(gu[:, d_ff:], -SWIGLU_LIMIT, SWIGLU_LIMIT)
        act = (up + 1.0) * (gate * (1.0 / (1.0 + jnp.exp(-SWIGLU_ALPHA * gate))))
        ybuf[slot] = _dot(act.astype(BF16), wd_bf[...]) + bd_ref[0]
        y_copy(s, slot).start()
        return carry

    lax.fori_loop(0, n_tiles, tile, 0)

    @pl.when(n_tiles >= 2)
    def _():
        y_copy(n_tiles - 2, n_tiles % 2).wait()

    @pl.when(n_tiles >= 1)
    def _():
        y_copy(n_tiles - 1, (n_tiles - 1) % 2).wait()


def _expert_ffn(layer, xs, expert_rows, expert_start, w_gu, b_gu, w_down, b_down):
    n_slots, d = xs.shape
    tb = EXPERT_TILE
    depth, ne, _, two_f = w_gu.shape
    d_ff = two_f // 2
    exp4 = lambda e, *_: (layer, e, 0, 0)
    return pl.pallas_call(
        _ffn_kernel,
        out_shape=jax.ShapeDtypeStruct((n_slots, d), F32),
        grid_spec=pltpu.PrefetchScalarGridSpec(
            num_scalar_prefetch=2,
            grid=(ne,),
            in_specs=[
                pl.BlockSpec((None, 1, d, two_f), exp4),
                pl.BlockSpec((None, 1, 1, two_f), exp4),
                pl.BlockSpec((None, 1, d_ff, d), exp4),
                pl.BlockSpec((None, 1, 1, d), exp4),
                pl.BlockSpec(memory_space=pl.ANY),
            ],
            out_specs=pl.BlockSpec(memory_space=pl.ANY),
            scratch_shapes=[
                pltpu.VMEM((d, two_f), BF16), pltpu.VMEM((d_ff, d), BF16),
                pltpu.VMEM((2, tb, d), F32), pltpu.VMEM((2, tb, d), F32),
                pltpu.SemaphoreType.DMA((2,)), pltpu.SemaphoreType.DMA((2,)),
            ],
        ),
        compiler_params=_cparams("arbitrary"),
        name="expert_ffn",
    )(expert_rows, expert_start, w_gu, b_gu.reshape(depth, ne, 1, two_f),
      w_down, b_down.reshape(depth, ne, 1, d), xs)


def _combine_kernel(rows_ref, off_ref, dst_ref, x_ref, pos_ref, gates_ref, mg_ref, *rest, final):
    if final:
        fn_ref, ys_ref, o_ref, buf, sem = rest
    else:
        ys_ref, o_ref, buf, sem = rest
    i = pl.program_id(0)
    tm = x_ref.shape[0]
    n_sorted = buf.shape[0]

    @pl.when(i == 0)
    def _():
        buf[...] = jnp.zeros_like(buf)

    def make_copy(local, slot):
        return pltpu.make_async_copy(ys_ref.at[pl.ds(slot, SEG_ALIGN)], buf.at[pl.ds(local, SEG_ALIGN)], sem)

    n = _segment_copies(i, rows_ref, off_ref, dst_ref, make_copy)
    pos = pos_ref[...]
    g = gates_ref[...]
    lane = lax.broadcasted_iota(I32, (tm, n_sorted), 1)
    weight = jnp.zeros((tm, n_sorted), F32)
    for k in range(TOP_K):
        weight = jnp.where(pos[:, k:k + 1] == lane, g[:, k:k + 1], weight)
    _wait_copies(n, make_copy)
    y = _dot(weight.astype(BF16), buf[...].astype(BF16))
    xn = x_ref[...] + mg_ref[...] * y
    if final:
        xn = _rms(xn, fn_ref[...])
    o_ref[...] = xn


def _combine(geom, layer, ys, pos, seg_rows, seg_off, seg_dst, x, gates, mods, final_g):
    t, d = x.shape
    tm = TOKEN_TILE
    final = final_g is not None

    def mod_imap(i, *_):
        return ((layer * MOD_ROWS + geom.group(i)) * N_MOD + 5, 0, 0)

    row = lambda i, *_: (i, 0)
    in_specs = [
        pl.BlockSpec((tm, d), row),
        pl.BlockSpec((tm, TOP_K), row),
        pl.BlockSpec((tm, TOP_K), row),
        pl.BlockSpec((None, 1, d), mod_imap),
    ]
    args = [seg_rows, seg_off, seg_dst, x, pos, gates, mods]
    if final:
        in_specs.append(pl.BlockSpec((1, d), lambda i, *_: (0, 0)))
        args.append(final_g)
    in_specs.append(pl.BlockSpec(memory_space=pl.ANY))
    args.append(ys)
    return pl.pallas_call(
        functools.partial(_combine_kernel, final=final),
        out_shape=jax.ShapeDtypeStruct((t, d), F32),
        grid_spec=pltpu.PrefetchScalarGridSpec(
            num_scalar_prefetch=3,
            grid=(t // tm,),
            in_specs=in_specs,
            out_specs=pl.BlockSpec((tm, d), row),
            scratch_shapes=[pltpu.VMEM((SORTED_ROWS, d), F32), pltpu.SemaphoreType.DMA],
        ),
        compiler_params=_cparams("arbitrary"),
        name="moe_combine",
    )(*args)


def _moe(geom, layer, h, pos, gates, seg_rows, seg_off, x, mods, w_gu, b_gu, w_down, b_down, final_g):
    t = h.shape[0]
    tb = EXPERT_TILE
    n_tok_tiles = seg_rows.shape[0]
    max_rows = t * TOP_K + n_tok_tiles * N_EXPERTS * (SEG_ALIGN - 1)
    n_blocks = -(-max_rows // tb) + N_EXPERTS
    rows = seg_rows[:, 0, :]
    cnt = jnp.sum(rows, axis=0)
    n_tiles_e = (cnt + tb - 1) // tb
    tile_end = jnp.cumsum(n_tiles_e)
    tile_start = tile_end - n_tiles_e
    expert_start = (tile_start * tb).astype(I32)
    seg_dst = expert_start[None, :] + jnp.cumsum(rows, axis=0) - rows
    rows_flat = rows.reshape(-1).astype(I32)
    off_flat = seg_off.reshape(-1).astype(I32)
    dst_flat = seg_dst.reshape(-1).astype(I32)
    xs = _dispatch(h, pos, rows_flat, off_flat, dst_flat, n_blocks * tb)
    ys = _expert_ffn(layer, xs, cnt.astype(I32), expert_start, w_gu, b_gu, w_down, b_down)
    return _combine(geom, layer, ys, pos, rows_flat, off_flat, dst_flat, x, gates, mods, final_g)


def kernel(x_prompt, x_sample, cache_b_k, cache_b_v, cache_c_k, cache_c_v, c, c_ctx,
           mod_w, mod_b, norm_mix, norm_ffn, even_w_in, even_w_out, even_sink,
           odd_w_in, odd_w_out, odd_q_norm, odd_k_norm, router_w, router_b,
           moe_w_gu, moe_b_gu, moe_w_down, moe_b_down, final_norm):
    bp, lp, d = x_prompt.shape
    bs, ls, _ = x_sample.shape
    past = cache_b_k.shape[2]
    depth = mod_w.shape[0]
    geom = _Geom(bp, lp, bs, ls)
    tp = geom.tp

    x = jnp.concatenate([x_prompt.reshape(tp, d), x_sample.reshape(bs * ls, d)], axis=0)
    cond = jnp.concatenate([c_ctx[None, :], c, jnp.zeros((MOD_ROWS - 1 - bs, d), F32)], axis=0)
    mods = _modulation(cond, mod_w, mod_b).reshape(depth * MOD_ROWS * N_MOD, 1, d)

    cn, sn = _dft_tables(A_GROUP_DIM)
    dft_chan = jnp.asarray(np.concatenate([cn, sn], axis=1), BF16)
    dft_p = [jnp.asarray(m, BF16) for m in _dft_tables(lp)]
    dft_s = [jnp.asarray(m, BF16) for m in _dft_tables(ls)]
    rope_b = [jnp.asarray(m) for m in _rope_tables(TOKEN_TILE, ls, B_HEAD_DIM)]
    rope_c = [jnp.asarray(m) for m in _rope_tables(TOKEN_TILE, ls, C_HEAD_DIM)]

    states = {"bk": [], "bv": [], "ck": [], "cv": []}
    for layer in range(depth):
        j = layer // 2
        g_mix = norm_mix[layer][None, :]
        g_ffn = norm_ffn[layer][None, :]
        if layer % 2 == 0:
            tc, ts, q, k, v = _in_projection(
                geom, layer, x, g_mix, mods, even_w_in[j].astype(BF16), rope_b[0], rope_b[1], [dft_chan],
                _proj_even_kernel, (A_WIDTH, A_WIDTH, B_Q_WIDTH, B_KV_WIDTH, B_KV_WIDTH),
                (BF16, BF16, BF16, F32, F32), "proj_even")
            states["bk"].append(k[:tp].reshape(bp, lp, B_KV_HEADS, B_HEAD_DIM))
            states["bv"].append(v[:tp].reshape(bp, lp, B_KV_HEADS, B_HEAD_DIM))
            four = _fourier_tokens(tc, ts, dft_p[0], dft_p[1], bp, lp, 0, None)
            four = _fourier_tokens(tc, ts, dft_s[0], dft_s[1], bs, ls, tp, four)
            sink = even_sink[j]
            common = dict(kv_heads=B_KV_HEADS, groups=B_HEADS // B_KV_HEADS, dh=B_HEAD_DIM)
            att = _attention(q, k, v, None, sink, None, n_seq=bp, seq_len=lp, row0=0, q_tile=lp,
                             window=None, **common)
            ctx = (cache_b_k[:, j].reshape(bs, past, B_KV_WIDTH), cache_b_v[:, j].reshape(bs, past, B_KV_WIDTH))
            att = _attention(q, k, v, ctx, sink, att, n_seq=bs, seq_len=ls, row0=tp, q_tile=ATTN_Q_TILE,
                             window=WINDOW, **common)
            w_out = even_w_out[j].astype(BF16)
            mixes = [four, att]
            w_parts = [w_out[:A_WIDTH], w_out[A_WIDTH:]]
        else:
            q, k, v = _in_projection(
                geom, layer, x, g_mix, mods, odd_w_in[j].astype(BF16), rope_c[0], rope_c[1],
                [odd_q_norm[j][None, :], odd_k_norm[j][None, :]],
                _proj_odd_kernel, (C_Q_WIDTH, C_KV_WIDTH, C_KV_WIDTH), (BF16, F32, F32), "proj_odd")
            states["ck"].append(k[:tp].reshape(bp, lp, C_KV_HEADS, C_HEAD_DIM))
            states["cv"].append(v[:tp].reshape(bp, lp, C_KV_HEADS, C_HEAD_DIM))
            common = dict(kv_heads=C_KV_HEADS, groups=C_HEADS // C_KV_HEADS, dh=C_HEAD_DIM, window=None)
            att = _attention(q, k, v, None, None, None, n_seq=bp, seq_len=lp, row0=0, q_tile=lp, **common)
            ctx = (cache_c_k[:, j].reshape(bs, past, C_KV_WIDTH), cache_c_v[:, j].reshape(bs, past, C_KV_WIDTH))
            att = _attention(q, k, v, ctx, None, att, n_seq=bs, seq_len=ls, row0=tp, q_tile=ATTN_Q_TILE, **common)
            mixes = [att]
            w_parts = [odd_w_out[j].astype(BF16)]
        x, h, pos, gates, seg_rows, seg_off = _out_route(
            geom, layer, mixes, w_parts, x, mods, g_ffn, router_w[layer].astype(BF16), router_b[layer][None, :])
        final_g = final_norm[None, :] if layer == depth - 1 else None
        x = _moe(geom, layer, h, pos, gates, seg_rows, seg_off, x, mods,
                 moe_w_gu, moe_b_gu, moe_w_down, moe_b_down, final_g)

    y_prompt = x[:tp].reshape(bp, lp, d)
    y_sample = x[tp:].reshape(bs, ls, d)
    return (y_prompt, y_sample,
            jnp.stack(states["bk"], axis=1), jnp.stack(states["bv"], axis=1),
            jnp.stack(states["ck"], axis=1), jnp.stack(states["cv"], axis=1))
```

```python
import functools

import numpy as np
import jax
import jax.numpy as jnp
from jax import lax
from jax.experimental import pallas as pl
from jax.experimental.pallas import tpu as pltpu

F32 = jnp.float32
BF16 = jnp.bfloat16
I32 = jnp.int32

GRID_W = 64
A_GROUPS = 4
A_GROUP_DIM = 128
A_WIDTH = A_GROUPS * A_GROUP_DIM
B_HEADS = 8
B_KV_HEADS = 2
B_HEAD_DIM = 64
B_Q_WIDTH = B_HEADS * B_HEAD_DIM
B_KV_WIDTH = B_KV_HEADS * B_HEAD_DIM
WINDOW = 128
C_HEADS = 8
C_KV_HEADS = 2
C_HEAD_DIM = 128
C_Q_WIDTH = C_HEADS * C_HEAD_DIM
C_KV_WIDTH = C_KV_HEADS * C_HEAD_DIM
ROPE_THETA = 10000.0
N_EXPERTS = 32
TOP_K = 4
SWIGLU_LIMIT = 7.0
SWIGLU_ALPHA = 1.702
EPS = 1e-6

LANES = 128
TOKEN_TILE = 256
EXPERT_TILE = 256
FFN_TILE = 512
ATTN_Q_TILE = 128
SEG_ALIGN = 8
SORTED_ROWS = -(-(TOKEN_TILE * TOP_K + N_EXPERTS * (SEG_ALIGN - 1)) // LANES) * LANES
VMEM_LIMIT = 56 * 1024 * 1024
MASKED = -1e30
N_MOD = 6
MOD_ROWS = 8


def _cparams(*sem):
    return pltpu.CompilerParams(dimension_semantics=tuple(sem), vmem_limit_bytes=VMEM_LIMIT)


def _dot(a, b):
    return jnp.dot(a, b, preferred_element_type=F32)


def _dot_nt(a, b):
    return lax.dot_general(a, b, (((1,), (1,)), ((), ())), preferred_element_type=F32)


def _rms(x, g):
    return x * lax.rsqrt(jnp.mean(x * x, axis=-1, keepdims=True) + EPS) * g


def _dft_tables(n):
    j = np.arange(n, dtype=np.int64)
    ang = 2.0 * np.pi * ((j[:, None] * j[None, :]) % n).astype(np.float64) / n
    s = 1.0 / np.sqrt(n)
    return np.cos(ang) * s, np.sin(ang) * s


def _rope_tables(n_prompt_rows, n_latent, head_dim):
    quarter = head_dim // 4
    pos = np.arange(n_latent)
    row = (pos // GRID_W).astype(np.float32)
    col = (pos % GRID_W).astype(np.float32)
    inv = (np.float32(ROPE_THETA) ** (-np.arange(quarter, dtype=np.float32) / np.float32(quarter))).astype(np.float32)
    ang_row = (row[:, None] * inv[None, :]).astype(np.float32)
    ang_col = (col[:, None] * inv[None, :]).astype(np.float32)
    cos_h = np.concatenate([np.cos(ang_row)] * 2 + [np.cos(ang_col)] * 2, axis=1)
    sin_h = np.concatenate([-np.sin(ang_row), np.sin(ang_row), -np.sin(ang_col), np.sin(ang_col)], axis=1)
    reps = LANES // head_dim
    cos_l = np.tile(cos_h, (1, reps)).astype(np.float32)
    sin_l = np.tile(sin_h, (1, reps)).astype(np.float32)
    cos = np.concatenate([np.ones((n_prompt_rows, LANES), np.float32), cos_l], axis=0)
    sin = np.concatenate([np.zeros((n_prompt_rows, LANES), np.float32), sin_l], axis=0)
    return cos, sin


def _rope(x, cos, sin, quarter):
    lane = lax.broadcasted_iota(I32, (x.shape[0], LANES), 1)
    first = ((lane // quarter) % 2) == 0
    outs = []
    for c in range(x.shape[1] // LANES):
        xc = x[:, c * LANES:(c + 1) * LANES]
        partner = jnp.where(first, pltpu.roll(xc, LANES - quarter, 1), pltpu.roll(xc, quarter, 1))
        outs.append(xc * cos + partner * sin)
    return outs[0] if len(outs) == 1 else jnp.concatenate(outs, axis=1)


def _head_rms(x, g):
    outs = []
    for c in range(x.shape[1] // LANES):
        outs.append(_rms(x[:, c * LANES:(c + 1) * LANES], g))
    return outs[0] if len(outs) == 1 else jnp.concatenate(outs, axis=1)


def _mod_kernel(c_ref, w_ref, b_ref, o_ref):
    c = c_ref[...]
    s = c * (1.0 / (1.0 + jnp.exp(-c)))
    o_ref[0] = _dot(s.astype(BF16), w_ref[0].astype(BF16)) + b_ref[0]


def _modulation(cond, mod_w, mod_b):
    depth, d, n = mod_w.shape
    tn = 1536
    return pl.pallas_call(
        _mod_kernel,
        out_shape=jax.ShapeDtypeStruct((depth, MOD_ROWS, n), F32),
        grid=(depth, n // tn),
        in_specs=[
            pl.BlockSpec((MOD_ROWS, d), lambda l, j: (0, 0)),
            pl.BlockSpec((1, d, tn), lambda l, j: (l, 0, j)),
            pl.BlockSpec((1, 1, tn), lambda l, j: (l, 0, j)),
        ],
        out_specs=pl.BlockSpec((1, MOD_ROWS, tn), lambda l, j: (l, 0, j)),
        compiler_params=_cparams("arbitrary", "arbitrary"),
        name="modulation",
    )(cond, mod_w, mod_b.reshape(depth, 1, n))


class _Geom:
    def __init__(self, bp, lp, bs, ls):
        self.bp, self.lp, self.bs, self.ls = bp, lp, bs, ls
        self.tp = bp * lp
        self.t = bp * lp + bs * ls
        assert lp == TOKEN_TILE and ls % TOKEN_TILE == 0 and self.tp % ls == 0
        self.n_ptiles = self.tp // TOKEN_TILE
        self.tiles_per_lat = ls // TOKEN_TILE
        self.n_tiles = self.t // TOKEN_TILE

    def group(self, i):
        return jnp.where(i < self.n_ptiles, 0, 1 + (i - self.n_ptiles) // self.tiles_per_lat)

    def pos_block(self, i):
        return jnp.where(i < self.n_ptiles, 0, 1 + (i - self.n_ptiles) % self.tiles_per_lat)


def _mod_spec(geom, layer, which, d):
    def imap(i):
        return ((layer * MOD_ROWS + geom.group(i)) * N_MOD + which, 0, 0)
    return pl.BlockSpec((None, 1, d), imap)


def _proj_even_kernel(x_ref, g_ref, sh_ref, sc_ref, w_ref, cos_ref, sin_ref, dft_ref,
                      tc_ref, ts_ref, q_ref, k_ref, v_ref):
    h = _rms(x_ref[...], g_ref[...]) * (1.0 + sc_ref[...]) + sh_ref[...]
    p = _dot(h.astype(BF16), w_ref[...])
    cos = cos_ref[...]
    sin = sin_ref[...]
    dft = dft_ref[...]
    tcs, tss = [], []
    for g in range(A_GROUPS):
        t = _dot(p[:, g * A_GROUP_DIM:(g + 1) * A_GROUP_DIM].astype(BF16), dft)
        tcs.append(t[:, :A_GROUP_DIM])
        tss.append(t[:, A_GROUP_DIM:])
    tc_ref[...] = jnp.concatenate(tcs, axis=1).astype(BF16)
    ts_ref[...] = jnp.concatenate(tss, axis=1).astype(BF16)
    o = A_WIDTH
    q_ref[...] = _rope(p[:, o:o + B_Q_WIDTH], cos, sin, B_HEAD_DIM // 4).astype(BF16)
    o += B_Q_WIDTH
    k_ref[...] = _rope(p[:, o:o + B_KV_WIDTH], cos, sin, B_HEAD_DIM // 4)
    o += B_KV_WIDTH
    v_ref[...] = p[:, o:o + B_KV_WIDTH]


def _proj_odd_kernel(x_ref, g_ref, sh_ref, sc_ref, w_ref, cos_ref, sin_ref, qn_ref, kn_ref,
                     q_ref, k_ref, v_ref):
    h = _rms(x_ref[...], g_ref[...]) * (1.0 + sc_ref[...]) + sh_ref[...]
    p = _dot(h.astype(BF16), w_ref[...])
    cos = cos_ref[...]
    sin = sin_ref[...]
    q = _head_rms(p[:, :C_Q_WIDTH], qn_ref[...])
    k = _head_rms(p[:, C_Q_WIDTH:C_Q_WIDTH + C_KV_WIDTH], kn_ref[...])
    q_ref[...] = _rope(q, cos, sin, C_HEAD_DIM // 4).astype(BF16)
    k_ref[...] = _rope(k, cos, sin, C_HEAD_DIM // 4)
    v_ref[...] = p[:, C_Q_WIDTH + C_KV_WIDTH:]


def _in_projection(geom, layer, x, norm_g, mods, w, cos, sin, extras, kernel, out_widths, out_dtypes, name):
    t, d = x.shape
    tm = TOKEN_TILE
    n_out = w.shape[1]
    row = lambda i: (i, 0)
    const2 = lambda i: (0, 0)
    in_specs = [
        pl.BlockSpec((tm, d), row),
        pl.BlockSpec((1, d), const2),
        _mod_spec(geom, layer, 0, d),
        _mod_spec(geom, layer, 1, d),
        pl.BlockSpec((d, n_out), const2),
        pl.BlockSpec((tm, LANES), lambda i: (geom.pos_block(i), 0)),
        pl.BlockSpec((tm, LANES), lambda i: (geom.pos_block(i), 0)),
    ] + [pl.BlockSpec(e.shape, const2) for e in extras]
    return pl.pallas_call(
        kernel,
        out_shape=[jax.ShapeDtypeStruct((t, wd), dt) for wd, dt in zip(out_widths, out_dtypes)],
        grid=(geom.n_tiles,),
        in_specs=in_specs,
        out_specs=[pl.BlockSpec((tm, wd), row) for wd in out_widths],
        compiler_params=_cparams("arbitrary"),
        name=name,
    )(x, norm_g, mods, mods, w, cos, sin, *extras)


def _fourier_kernel(cl_ref, sl_ref, tc_ref, ts_ref, *rest):
    o_ref = rest[-1]
    o_ref[...] = (_dot(cl_ref[...], tc_ref[...]) - _dot(sl_ref[...], ts_ref[...])).astype(o_ref.dtype)


def _fourier_tokens(tc, ts, cl, sl, n_seq, seq_len, row0, prev):
    t, width = tc.shape
    tr = min(seq_len, 512)
    n_r = seq_len // tr
    assert row0 % seq_len == 0
    seq0 = row0 // seq_len
    out0 = row0 // tr
    in_specs = [
        pl.BlockSpec((tr, seq_len), lambda s, r: (r, 0)),
        pl.BlockSpec((tr, seq_len), lambda s, r: (r, 0)),
        pl.BlockSpec((seq_len, width), lambda s, r: (seq0 + s, 0)),
        pl.BlockSpec((seq_len, width), lambda s, r: (seq0 + s, 0)),
    ]
    args = [cl, sl, tc, ts]
    aliases = {}
    if prev is not None:
        in_specs.append(pl.BlockSpec(memory_space=pl.ANY))
        args.append(prev)
        aliases = {4: 0}
    return pl.pallas_call(
        _fourier_kernel,
        out_shape=jax.ShapeDtypeStruct((t, width), BF16),
        grid=(n_seq, n_r),
        in_specs=in_specs,
        out_specs=pl.BlockSpec((tr, width), lambda s, r: (out0 + s * n_r + r, 0)),
        input_output_aliases=aliases,
        compiler_params=_cparams("arbitrary", "arbitrary"),
        name="fourier_tokens",
    )(*args)


def _softmax_pv(q, parts, sink, scale):
    scores = []
    m = sink
    for k, _, mask in parts:
        s = _dot_nt(q, k) * scale
        if mask is not None:
            s = jnp.where(mask, s, MASKED)
        scores.append(s)
        mx = jnp.max(s, axis=-1, keepdims=True)
        m = mx if m is None else jnp.maximum(m, mx)
    den = None if sink is None else jnp.exp(sink - m)
    acc = None
    for (_, v, _), s in zip(parts, scores):
        e = jnp.exp(s - m)
        es = jnp.sum(e, axis=-1, keepdims=True)
        den = es if den is None else den + es
        o = _dot(e.astype(BF16), v)
        acc = o if acc is None else acc + o
    return acc / den


def _stack_heads(q, kv, groups, dh):
    return jnp.concatenate([q[:, (kv * groups + g) * dh:(kv * groups + g + 1) * dh] for g in range(groups)], axis=0)


def _sink_column(sink_ref, kv, groups, rows):
    return jnp.concatenate([jnp.full((rows, 1), sink_ref[kv * groups + g], F32) for g in range(groups)], axis=0)


def _attn_kernel(*refs, kv_heads, groups, dh, has_sink, has_ctx, window, q_tile, seq_len):
    refs = list(refs)
    sink_ref = refs.pop(0) if has_sink else None
    q_ref, k_ref, v_ref = refs[:3]
    ck_ref, cv_ref = (refs[3], refs[4]) if has_ctx else (None, None)
    o_ref = refs[-1]
    scale = dh ** -0.5
    q = q_ref[...]
    rows = q.shape[0]
    if window is None:
        k_loc = k_ref[...].astype(BF16)
        v_loc = v_ref[...].astype(BF16)
        mask = None
    else:
        n = pl.program_id(1)
        band = q_tile + 2 * window
        start = jnp.clip(n * q_tile - window, 0, seq_len - band)
        start = pl.multiple_of(start, LANES)
        k_loc = k_ref[pl.ds(start, band), :].astype(BF16)
        v_loc = v_ref[pl.ds(start, band), :].astype(BF16)
        qpos = n * q_tile + lax.broadcasted_iota(I32, (groups * rows, band), 0) % rows
        kpos = start + lax.broadcasted_iota(I32, (groups * rows, band), 1)
        mask = jnp.abs(kpos - qpos) <= window
    if has_ctx:
        k_ctx = ck_ref[...].astype(BF16)
        v_ctx = cv_ref[...].astype(BF16)
    outs = []
    for kv in range(kv_heads):
        sl = slice(kv * dh, (kv + 1) * dh)
        parts = [(k_loc[:, sl], v_loc[:, sl], mask)]
        if has_ctx:
            parts.append((k_ctx[:, sl], v_ctx[:, sl], None))
        sink = _sink_column(sink_ref, kv, groups, rows) if has_sink else None
        o = _softmax_pv(_stack_heads(q, kv, groups, dh), parts, sink, scale)
        outs.extend(o[g * rows:(g + 1) * rows] for g in range(groups))
    o_ref[...] = jnp.concatenate(outs, axis=1).astype(o_ref.dtype)


def _attention(q, k, v, ctx, sink, prev, *, n_seq, seq_len, row0, q_tile, kv_heads, groups, dh, window):
    t, qw = q.shape
    kw = k.shape[1]
    n_q = seq_len // q_tile
    assert row0 % seq_len == 0 and row0 % q_tile == 0
    seq0 = row0 // seq_len
    q0 = row0 // q_tile
    in_specs, args = [], []
    if sink is not None:
        in_specs.append(pl.BlockSpec(memory_space=pltpu.SMEM))
        args.append(sink)
    in_specs += [
        pl.BlockSpec((q_tile, qw), lambda s, n: (q0 + s * n_q + n, 0)),
        pl.BlockSpec((seq_len, kw), lambda s, n: (seq0 + s, 0)),
        pl.BlockSpec((seq_len, kw), lambda s, n: (seq0 + s, 0)),
    ]
    args += [q, k, v]
    if ctx is not None:
        p = ctx[0].shape[1]
        in_specs += [pl.BlockSpec((None, p, kw), lambda s, n: (s, 0, 0))] * 2
        args += list(ctx)
    aliases = {}
    if prev is not None:
        in_specs.append(pl.BlockSpec(memory_space=pl.ANY))
        aliases = {len(args): 0}
        args.append(prev)
    kern = functools.partial(
        _attn_kernel, kv_heads=kv_heads, groups=groups, dh=dh, has_sink=sink is not None,
        has_ctx=ctx is not None, window=window, q_tile=q_tile, seq_len=seq_len)
    return pl.pallas_call(
        kern,
        out_shape=jax.ShapeDtypeStruct((t, qw), BF16),
        grid=(n_seq, n_q),
        in_specs=in_specs,
        out_specs=pl.BlockSpec((q_tile, qw), lambda s, n: (q0 + s * n_q + n, 0)),
        input_output_aliases=aliases,
        compiler_params=_cparams("arbitrary", "arbitrary"),
        name="attention",
    )(*args)


def _out_route_kernel(*refs, n_mix):
    mix_refs = refs[:n_mix]
    w_refs = refs[n_mix:2 * n_mix]
    (x_ref, gate_ref, g2_ref, sh2_ref, sc2_ref, rw_ref, rb_ref,
     xo_ref, h_ref, pos_ref, gates_ref, rows_ref, off_ref) = refs[2 * n_mix:]

    acc = None
    for m_ref, w_ref in zip(mix_refs, w_refs):
        part = _dot(m_ref[...], w_ref[...])
        acc = part if acc is None else acc + part
    xn = x_ref[...] + gate_ref[...] * acc
    xo_ref[...] = xn
    h = _rms(xn, g2_ref[...]) * (1.0 + sc2_ref[...]) + sh2_ref[...]
    hb = h.astype(BF16)
    h_ref[...] = hb

    logits = _dot(hb, rw_ref[...]) + rb_ref[...]
    tm, ne = logits.shape
    lane = lax.broadcasted_iota(I32, (tm, ne), 1).astype(F32)
    lane4 = lax.broadcasted_iota(I32, (tm, TOP_K), 1)
    work = logits
    sels, vals = [], []
    for _ in range(TOP_K):
        mx = jnp.max(work, axis=-1, keepdims=True)
        first = jnp.min(jnp.where(work == mx, lane, float(ne)), axis=-1, keepdims=True)
        sel = lane == first
        work = jnp.where(sel, -jnp.inf, work)
        sels.append(sel)
        vals.append(mx)
    exps = [jnp.exp(v - vals[0]) for v in vals]
    den = exps[0] + exps[1] + exps[2] + exps[3]

    onehot = jnp.zeros((tm, ne), F32)
    for sel in sels:
        onehot = onehot + sel.astype(F32)
    r_i = lax.broadcasted_iota(I32, (tm, tm), 0)
    c_i = lax.broadcasted_iota(I32, (tm, tm), 1)
    before = jnp.where(c_i < r_i, 1.0, 0.0).astype(BF16)
    earlier = _dot(before, onehot.astype(BF16))

    cnt = jnp.sum(onehot, axis=0, keepdims=True)
    seg8 = jnp.floor((cnt + (SEG_ALIGN - 1.0)) * (1.0 / SEG_ALIGN))
    e_r = lax.broadcasted_iota(I32, (ne, ne), 0)
    e_c = lax.broadcasted_iota(I32, (ne, ne), 1)
    upper = jnp.where(e_r < e_c, 1.0, 0.0).astype(BF16)
    off8 = _dot(jnp.broadcast_to(seg8, (SEG_ALIGN, ne)).astype(BF16), upper)[0:1]
    seg_off = off8 * SEG_ALIGN
    base = seg_off + earlier

    pos_o = jnp.zeros((tm, TOP_K), I32)
    gate_o = jnp.zeros((tm, TOP_K), F32)
    for k in range(TOP_K):
        pos_k = jnp.sum(jnp.where(sels[k], base, 0.0), axis=-1, keepdims=True)
        pos_o = jnp.where(lane4 == k, pos_k.astype(I32), pos_o)
        gate_o = jnp.where(lane4 == k, exps[k] / den, gate_o)
    pos_ref[...] = pos_o
    gates_ref[...] = gate_o
    rows_ref[0] = (seg8 * SEG_ALIGN).astype(I32)
    off_ref[0] = seg_off.astype(I32)


def _out_route(geom, layer, mixes, w_parts, x, mods, norm_g, router_w, router_b):
    t, d = x.shape
    tm = TOKEN_TILE
    row = lambda i: (i, 0)
    const2 = lambda i: (0, 0)
    in_specs = [pl.BlockSpec((tm, m.shape[1]), row) for m in mixes]
    in_specs += [pl.BlockSpec(w.shape, const2) for w in w_parts]
    in_specs += [
        pl.BlockSpec((tm, d), row),
        _mod_spec(geom, layer, 2, d),
        pl.BlockSpec((1, d), const2),
        _mod_spec(geom, layer, 3, d),
        _mod_spec(geom, layer, 4, d),
        pl.BlockSpec(router_w.shape, const2),
        pl.BlockSpec((1, N_EXPERTS), const2),
    ]
    seg3 = lambda i: (i, 0, 0)
    out_shape = [
        jax.ShapeDtypeStruct((t, d), F32),
        jax.ShapeDtypeStruct((t, d), BF16),
        jax.ShapeDtypeStruct((t, TOP_K), I32),
        jax.ShapeDtypeStruct((t, TOP_K), F32),
        jax.ShapeDtypeStruct((geom.n_tiles, 1, N_EXPERTS), I32),
        jax.ShapeDtypeStruct((geom.n_tiles, 1, N_EXPERTS), I32),
    ]
    out_specs = [
        pl.BlockSpec((tm, d), row),
        pl.BlockSpec((tm, d), row),
        pl.BlockSpec((tm, TOP_K), row),
        pl.BlockSpec((tm, TOP_K), row),
        pl.BlockSpec((1, 1, N_EXPERTS), seg3),
        pl.BlockSpec((1, 1, N_EXPERTS), seg3),
    ]
    return pl.pallas_call(
        functools.partial(_out_route_kernel, n_mix=len(mixes)),
        out_shape=out_shape,
        grid=(geom.n_tiles,),
        in_specs=in_specs,
        out_specs=out_specs,
        compiler_params=_cparams("arbitrary"),
        name="out_route",
    )(*mixes, *w_parts, x, mods, norm_g, mods, mods, router_w, router_b)


def _segment_copies(i, rows_ref, off_ref, dst_ref, make_copy):
    def per_expert(e, total):
        a = i * N_EXPERTS + e
        n_chunks = rows_ref[a] // SEG_ALIGN
        local0 = off_ref[a]
        slot0 = dst_ref[a]

        def chunk(c, carry):
            local = pl.multiple_of(local0 + c * SEG_ALIGN, SEG_ALIGN)
            slot = pl.multiple_of(slot0 + c * SEG_ALIGN, SEG_ALIGN)
            make_copy(local, slot).start()
            return carry

        lax.fori_loop(0, n_chunks, chunk, 0)
        return total + n_chunks

    return lax.fori_loop(0, N_EXPERTS, per_expert, 0)


def _wait_copies(n, make_copy):
    def wait_one(c, carry):
        make_copy(0, 0).wait()
        return carry
    lax.fori_loop(0, n, wait_one, 0)


def _dispatch_kernel(rows_ref, off_ref, dst_ref, h_ref, pos_ref, xs_ref, sorted_ref, sem):
    i = pl.program_id(0)
    tm = h_ref.shape[0]
    n_sorted = sorted_ref.shape[0]
    pos = pos_ref[...]
    lane = lax.broadcasted_iota(I32, (tm, n_sorted), 1)
    hit = jnp.zeros((tm, n_sorted), F32)
    for k in range(TOP_K):
        hit = jnp.where(pos[:, k:k + 1] == lane, 1.0, hit)
    sorted_ref[...] = lax.dot_general(hit.astype(BF16), h_ref[...], (((0,), (0,)), ((), ())),
                                      preferred_element_type=F32)

    def make_copy(local, slot):
        return pltpu.make_async_copy(sorted_ref.at[pl.ds(local, SEG_ALIGN)], xs_ref.at[pl.ds(slot, SEG_ALIGN)], sem)

    n = _segment_copies(i, rows_ref, off_ref, dst_ref, make_copy)
    _wait_copies(n, make_copy)


def _dispatch(h, pos, seg_rows, seg_off, seg_dst, n_slots):
    t, d = h.shape
    tm = TOKEN_TILE
    return pl.pallas_call(
        _dispatch_kernel,
        out_shape=jax.ShapeDtypeStruct((n_slots, d), F32),
        grid_spec=pltpu.PrefetchScalarGridSpec(
            num_scalar_prefetch=3,
            grid=(t // tm,),
            in_specs=[
                pl.BlockSpec((tm, d), lambda i, *_: (i, 0)),
                pl.BlockSpec((tm, TOP_K), lambda i, *_: (i, 0)),
            ],
            out_specs=pl.BlockSpec(memory_space=pl.ANY),
            scratch_shapes=[pltpu.VMEM((SORTED_ROWS, d), F32), pltpu.SemaphoreType.DMA],
        ),
        compiler_params=_cparams("arbitrary"),
        name="moe_dispatch",
    )(seg_rows, seg_off, seg_dst, h, pos)


def _ffn_kernel(rows_ref, start_ref, wgu_ref, bgu_ref, wd_ref, bd_ref, xs_ref, ys_ref,
                wgu_bf, wd_bf, xbuf, ybuf, sem_in, sem_out):
    e = pl.program_id(0)
    d_ff = wd_ref.shape[1]
    tb = xbuf.shape[1]
    n_rows = rows_ref[e]
    n_tiles = (n_rows + tb - 1) // tb
    base = start_ref[e]

    chunk = 128
    def cast_gu(c, carry):
        r = pl.multiple_of(c * chunk, chunk)
        wgu_bf[pl.ds(r, chunk), :] = wgu_ref[0, pl.ds(r, chunk), :].astype(BF16)
        return carry
    lax.fori_loop(0, wgu_ref.shape[1] // chunk, cast_gu, 0)
    def cast_d(c, carry):
        r = pl.multiple_of(c * chunk, chunk)
        wd_bf[pl.ds(r, chunk), :] = wd_ref[0, pl.ds(r, chunk), :].astype(BF16)
        return carry
    lax.fori_loop(0, d_ff // chunk, cast_d, 0)

    def x_copy(s, slot):
        r = pl.multiple_of(base + s * tb, EXPERT_TILE)
        return pltpu.make_async_copy(xs_ref.at[pl.ds(r, tb)], xbuf.at[slot], sem_in.at[slot])

    def y_copy(s, slot):
        r = pl.multiple_of(base + s * tb, EXPERT_TILE)
        return pltpu.make_async_copy(ybuf.at[slot], ys_ref.at[pl.ds(r, tb)], sem_out.at[slot])

    @pl.when(n_tiles > 0)
    def _():
        x_copy(0, 0).start()

    def tile(s, carry):
        slot = s % 2
        x_copy(s, slot).wait()

        @pl.when(s + 1 < n_tiles)
        def _():
            x_copy(s + 1, 1 - slot).start()

        @pl.when(s >= 2)
        def _():
            y_copy(s - 2, slot).wait()

        rows = s * tb + lax.broadcasted_iota(I32, (tb, 1), 0)
        x = jnp.where(rows < n_rows, xbuf[slot], 0.0).astype(BF16)
        gu = _dot(x, wgu_bf[...]) + bgu_ref[0]
        gate = jnp.minimum(gu[:, :d_ff], SWIGLU_LIMIT)
        up = jnp.clip(gu[:, d_ff:], -SWIGLU_LIMIT, SWIGLU_LIMIT)
        act = (up + 1.0) * (gate * (1.0 / (1.0 + jnp.exp(-SWIGLU_ALPHA * gate))))
        ybuf[slot] = _dot(act.astype(BF16), wd_bf[...]) + bd_ref[0]
        y_copy(s, slot).start()
        return carry

    lax.fori_loop(0, n_tiles, tile, 0)

    @pl.when(n_tiles >= 2)
    def _():
        y_copy(n_tiles - 2, n_tiles % 2).wait()

    @pl.when(n_tiles >= 1)
    def _():
        y_copy(n_tiles - 1, (n_tiles - 1) % 2).wait()


def _expert_ffn(layer, xs, expert_rows, expert_start, w_gu, b_gu, w_down, b_down):
    n_slots, d = xs.shape
    tb = FFN_TILE
    depth, ne, _, two_f = w_gu.shape
    d_ff = two_f // 2
    exp4 = lambda e, *_: (layer, e, 0, 0)
    return pl.pallas_call(
        _ffn_kernel,
        out_shape=jax.ShapeDtypeStruct((n_slots, d), F32),
        grid_spec=pltpu.PrefetchScalarGridSpec(
            num_scalar_prefetch=2,
            grid=(ne,),
            in_specs=[
                pl.BlockSpec((None, 1, d, two_f), exp4),
                pl.BlockSpec((None, 1, 1, two_f), exp4),
                pl.BlockSpec((None, 1, d_ff, d), exp4),
                pl.BlockSpec((None, 1, 1, d), exp4),
                pl.BlockSpec(memory_space=pl.ANY),
            ],
            out_specs=pl.BlockSpec(memory_space=pl.ANY),
            scratch_shapes=[
                pltpu.VMEM((d, two_f), BF16), pltpu.VMEM((d_ff, d), BF16),
                pltpu.VMEM((2, tb, d), F32), pltpu.VMEM((2, tb, d), F32),
                pltpu.SemaphoreType.DMA((2,)), pltpu.SemaphoreType.DMA((2,)),
            ],
        ),
        compiler_params=_cparams("arbitrary"),
        name="expert_ffn",
    )(expert_rows, expert_start, w_gu, b_gu.reshape(depth, ne, 1, two_f),
      w_down, b_down.reshape(depth, ne, 1, d), xs)


def _combine_kernel(rows_ref, off_ref, dst_ref, x_ref, pos_ref, gates_ref, mg_ref, *rest, final):
    if final:
        fn_ref, ys_ref, o_ref, buf, sem = rest
    else:
        ys_ref, o_ref, buf, sem = rest
    i = pl.program_id(0)
    tm = x_ref.shape[0]
    n_sorted = buf.shape[0]

    @pl.when(i == 0)
    def _():
        buf[...] = jnp.zeros_like(buf)

    def make_copy(local, slot):
        return pltpu.make_async_copy(ys_ref.at[pl.ds(slot, SEG_ALIGN)], buf.at[pl.ds(local, SEG_ALIGN)], sem)

    n = _segment_copies(i, rows_ref, off_ref, dst_ref, make_copy)
    pos = pos_ref[...]
    g = gates_ref[...]
    lane = lax.broadcasted_iota(I32, (tm, n_sorted), 1)
    weight = jnp.zeros((tm, n_sorted), F32)
    for k in range(TOP_K):
        weight = jnp.where(pos[:, k:k + 1] == lane, g[:, k:k + 1], weight)
    _wait_copies(n, make_copy)
    y = _dot(weight.astype(BF16), buf[...].astype(BF16))
    xn = x_ref[...] + mg_ref[...] * y
    if final:
        xn = _rms(xn, fn_ref[...])
    o_ref[...] = xn


def _combine(geom, layer, ys, pos, seg_rows, seg_off, seg_dst, x, gates, mods, final_g):
    t, d = x.shape
    tm = TOKEN_TILE
    final = final_g is not None

    def mod_imap(i, *_):
        return ((layer * MOD_ROWS + geom.group(i)) * N_MOD + 5, 0, 0)

    row = lambda i, *_: (i, 0)
    in_specs = [
        pl.BlockSpec((tm, d), row),
        pl.BlockSpec((tm, TOP_K), row),
        pl.BlockSpec((tm, TOP_K), row),
        pl.BlockSpec((None, 1, d), mod_imap),
    ]
    args = [seg_rows, seg_off, seg_dst, x, pos, gates, mods]
    if final:
        in_specs.append(pl.BlockSpec((1, d), lambda i, *_: (0, 0)))
        args.append(final_g)
    in_specs.append(pl.BlockSpec(memory_space=pl.ANY))
    args.append(ys)
    return pl.pallas_call(
        functools.partial(_combine_kernel, final=final),
        out_shape=jax.ShapeDtypeStruct((t, d), F32),
        grid_spec=pltpu.PrefetchScalarGridSpec(
            num_scalar_prefetch=3,
            grid=(t // tm,),
            in_specs=in_specs,
            out_specs=pl.BlockSpec((tm, d), row),
            scratch_shapes=[pltpu.VMEM((SORTED_ROWS, d), F32), pltpu.SemaphoreType.DMA],
        ),
        compiler_params=_cparams("arbitrary"),
        name="moe_combine",
    )(*args)


def _moe(geom, layer, h, pos, gates, seg_rows, seg_off, x, mods, w_gu, b_gu, w_down, b_down, final_g):
    t = h.shape[0]
    tb = EXPERT_TILE
    n_tok_tiles = seg_rows.shape[0]
    max_rows = t * TOP_K + n_tok_tiles * N_EXPERTS * (SEG_ALIGN - 1)
    n_blocks = -(-max_rows // tb) + N_EXPERTS
    rows = seg_rows[:, 0, :]
    cnt = jnp.sum(rows, axis=0)
    n_tiles_e = (cnt + tb - 1) // tb
    tile_end = jnp.cumsum(n_tiles_e)
    tile_start = tile_end - n_tiles_e
    expert_start = (tile_start * tb).astype(I32)
    seg_dst = expert_start[None, :] + jnp.cumsum(rows, axis=0) - rows
    rows_flat = rows.reshape(-1).astype(I32)
    off_flat = seg_off.reshape(-1).astype(I32)
    dst_flat = seg_dst.reshape(-1).astype(I32)
    xs = _dispatch(h, pos, rows_flat, off_flat, dst_flat, n_blocks * tb + FFN_TILE - tb)
    ys = _expert_ffn(layer, xs, cnt.astype(I32), expert_start, w_gu, b_gu, w_down, b_down)
    return _combine(geom, layer, ys, pos, rows_flat, off_flat, dst_flat, x, gates, mods, final_g)


def kernel(x_prompt, x_sample, cache_b_k, cache_b_v, cache_c_k, cache_c_v, c, c_ctx,
           mod_w, mod_b, norm_mix, norm_ffn, even_w_in, even_w_out, even_sink,
           odd_w_in, odd_w_out, odd_q_norm, odd_k_norm, router_w, router_b,
           moe_w_gu, moe_b_gu, moe_w_down, moe_b_down, final_norm):
    bp, lp, d = x_prompt.shape
    bs, ls, _ = x_sample.shape
    past = cache_b_k.shape[2]
    depth = mod_w.shape[0]
    geom = _Geom(bp, lp, bs, ls)
    tp = geom.tp

    x = jnp.concatenate([x_prompt.reshape(tp, d), x_sample.reshape(bs * ls, d)], axis=0)
    cond = jnp.concatenate([c_ctx[None, :], c, jnp.zeros((MOD_ROWS - 1 - bs, d), F32)], axis=0)
    mods = _modulation(cond, mod_w, mod_b).reshape(depth * MOD_ROWS * N_MOD, 1, d)

    cn, sn = _dft_tables(A_GROUP_DIM)
    dft_chan = jnp.asarray(np.concatenate([cn, sn], axis=1), BF16)
    dft_p = [jnp.asarray(m, BF16) for m in _dft_tables(lp)]
    dft_s = [jnp.asarray(m, BF16) for m in _dft_tables(ls)]
    rope_b = [jnp.asarray(m) for m in _rope_tables(TOKEN_TILE, ls, B_HEAD_DIM)]
    rope_c = [jnp.asarray(m) for m in _rope_tables(TOKEN_TILE, ls, C_HEAD_DIM)]

    states = {"bk": [], "bv": [], "ck": [], "cv": []}
    for layer in range(depth):
        j = layer // 2
        g_mix = norm_mix[layer][None, :]
        g_ffn = norm_ffn[layer][None, :]
        if layer % 2 == 0:
            tc, ts, q, k, v = _in_projection(
                geom, layer, x, g_mix, mods, even_w_in[j].astype(BF16), rope_b[0], rope_b[1], [dft_chan],
                _proj_even_kernel, (A_WIDTH, A_WIDTH, B_Q_WIDTH, B_KV_WIDTH, B_KV_WIDTH),
                (BF16, BF16, BF16, F32, F32), "proj_even")
            states["bk"].append(k[:tp].reshape(bp, lp, B_KV_HEADS, B_HEAD_DIM))
            states["bv"].append(v[:tp].reshape(bp, lp, B_KV_HEADS, B_HEAD_DIM))
            four = _fourier_tokens(tc, ts, dft_p[0], dft_p[1], bp, lp, 0, None)
            four = _fourier_tokens(tc, ts, dft_s[0], dft_s[1], bs, ls, tp, four)
            sink = even_sink[j]
            common = dict(kv_heads=B_KV_HEADS, groups=B_HEADS // B_KV_HEADS, dh=B_HEAD_DIM)
            att = _attention(q, k, v, None, sink, None, n_seq=bp, seq_len=lp, row0=0, q_tile=lp,
                             window=None, **common)
            ctx = (cache_b_k[:, j].reshape(bs, past, B_KV_WIDTH), cache_b_v[:, j].reshape(bs, past, B_KV_WIDTH))
            att = _attention(q, k, v, ctx, sink, att, n_seq=bs, seq_len=ls, row0=tp, q_tile=ATTN_Q_TILE,
                             window=WINDOW, **common)
            w_out = even_w_out[j].astype(BF16)
            mixes = [four, att]
            w_parts = [w_out[:A_WIDTH], w_out[A_WIDTH:]]
        else:
            q, k, v = _in_projection(
                geom, layer, x, g_mix, mods, odd_w_in[j].astype(BF16), rope_c[0], rope_c[1],
                [odd_q_norm[j][None, :], odd_k_norm[j][None, :]],
                _proj_odd_kernel, (C_Q_WIDTH, C_KV_WIDTH, C_KV_WIDTH), (BF16, F32, F32), "proj_odd")
            states["ck"].append(k[:tp].reshape(bp, lp, C_KV_HEADS, C_HEAD_DIM))
            states["cv"].append(v[:tp].reshape(bp, lp, C_KV_HEADS, C_HEAD_DIM))
            common = dict(kv_heads=C_KV_HEADS, groups=C_HEADS // C_KV_HEADS, dh=C_HEAD_DIM, window=None)
            att = _attention(q, k, v, None, None, None, n_seq=bp, seq_len=lp, row0=0, q_tile=lp, **common)
            ctx = (cache_c_k[:, j].reshape(bs, past, C_KV_WIDTH), cache_c_v[:, j].reshape(bs, past, C_KV_WIDTH))
            att = _attention(q, k, v, ctx, None, att, n_seq=bs, seq_len=ls, row0=tp, q_tile=ATTN_Q_TILE, **common)
            mixes = [att]
            w_parts = [odd_w_out[j].astype(BF16)]
        x, h, pos, gates, seg_rows, seg_off = _out_route(
            geom, layer, mixes, w_parts, x, mods, g_ffn, router_w[layer].astype(BF16), router_b[layer][None, :])
        final_g = final_norm[None, :] if layer == depth - 1 else None
        x = _moe(geom, layer, h, pos, gates, seg_rows, seg_off, x, mods,
                 moe_w_gu, moe_b_gu, moe_w_down, moe_b_down, final_g)

    y_prompt = x[:tp].reshape(bp, lp, d)
    y_sample = x[tp:].reshape(bs, ls, d)
    return (y_prompt, y_sample,
            jnp.stack(states["bk"], axis=1), jnp.stack(states["bv"], axis=1),
            jnp.stack(states["ck"], axis=1), jnp.stack(states["cv"], axis=1))
```

```python
import functools

import numpy as np
import jax
import jax.numpy as jnp
from jax import lax
from jax.experimental import pallas as pl
from jax.experimental.pallas import tpu as pltpu

F32 = jnp.float32
BF16 = jnp.bfloat16
I32 = jnp.int32
U32 = jnp.uint32

GRID_W = 64
A_GROUPS = 4
A_GROUP_DIM = 128
A_WIDTH = A_GROUPS * A_GROUP_DIM
B_HEADS = 8
B_KV_HEADS = 2
B_HEAD_DIM = 64
B_Q_WIDTH = B_HEADS * B_HEAD_DIM
B_KV_WIDTH = B_KV_HEADS * B_HEAD_DIM
WINDOW = 128
C_HEADS = 8
C_KV_HEADS = 2
C_HEAD_DIM = 128
C_Q_WIDTH = C_HEADS * C_HEAD_DIM
C_KV_WIDTH = C_KV_HEADS * C_HEAD_DIM
ROPE_THETA = 10000.0
N_EXPERTS = 32
TOP_K = 4
SWIGLU_LIMIT = 7.0
SWIGLU_ALPHA = 1.702
EPS = 1e-6

LANES = 128
TOKEN_TILE = 256
EXPERT_TILE = 256
FFN_TILE = 512
ATTN_Q_TILE = 128
SEG_ALIGN = 8
SORTED_ROWS = -(-(TOKEN_TILE * TOP_K + N_EXPERTS * (SEG_ALIGN - 1)) // LANES) * LANES
VMEM_LIMIT = 56 * 1024 * 1024
MASKED = -1e30
N_MOD = 6
MOD_ROWS = 8


def _cparams(*sem):
    return pltpu.CompilerParams(dimension_semantics=tuple(sem), vmem_limit_bytes=VMEM_LIMIT)


def _dot(a, b):
    return jnp.dot(a, b, preferred_element_type=F32)


def _dot_nt(a, b):
    return lax.dot_general(a, b, (((1,), (1,)), ((), ())), preferred_element_type=F32)


def _rms(x, g):
    return x * lax.rsqrt(jnp.mean(x * x, axis=-1, keepdims=True) + EPS) * g


def _dft_tables(n):
    j = np.arange(n, dtype=np.int64)
    ang = 2.0 * np.pi * ((j[:, None] * j[None, :]) % n).astype(np.float64) / n
    s = 1.0 / np.sqrt(n)
    return np.cos(ang) * s, np.sin(ang) * s


def _rope_tables(n_prompt_rows, n_latent, head_dim):
    quarter = head_dim // 4
    pos = np.arange(n_latent)
    row = (pos // GRID_W).astype(np.float32)
    col = (pos % GRID_W).astype(np.float32)
    inv = (np.float32(ROPE_THETA) ** (-np.arange(quarter, dtype=np.float32) / np.float32(quarter))).astype(np.float32)
    ang_row = (row[:, None] * inv[None, :]).astype(np.float32)
    ang_col = (col[:, None] * inv[None, :]).astype(np.float32)
    cos_h = np.concatenate([np.cos(ang_row)] * 2 + [np.cos(ang_col)] * 2, axis=1)
    sin_h = np.concatenate([-np.sin(ang_row), np.sin(ang_row), -np.sin(ang_col), np.sin(ang_col)], axis=1)
    reps = LANES // head_dim
    cos_l = np.tile(cos_h, (1, reps)).astype(np.float32)
    sin_l = np.tile(sin_h, (1, reps)).astype(np.float32)
    cos = np.concatenate([np.ones((n_prompt_rows, LANES), np.float32), cos_l], axis=0)
    sin = np.concatenate([np.zeros((n_prompt_rows, LANES), np.float32), sin_l], axis=0)
    return cos, sin


def _rope(x, cos, sin, quarter):
    lane = lax.broadcasted_iota(I32, (x.shape[0], LANES), 1)
    first = ((lane // quarter) % 2) == 0
    outs = []
    for c in range(x.shape[1] // LANES):
        xc = x[:, c * LANES:(c + 1) * LANES]
        partner = jnp.where(first, pltpu.roll(xc, LANES - quarter, 1), pltpu.roll(xc, quarter, 1))
        outs.append(xc * cos + partner * sin)
    return outs[0] if len(outs) == 1 else jnp.concatenate(outs, axis=1)


def _head_rms(x, g):
    outs = []
    for c in range(x.shape[1] // LANES):
        outs.append(_rms(x[:, c * LANES:(c + 1) * LANES], g))
    return outs[0] if len(outs) == 1 else jnp.concatenate(outs, axis=1)


def _mod_kernel(c_ref, w_ref, b_ref, o_ref):
    c = c_ref[...]
    s = c * (1.0 / (1.0 + jnp.exp(-c)))
    o_ref[0] = _dot(s.astype(BF16), w_ref[0].astype(BF16)) + b_ref[0]


def _modulation(cond, mod_w, mod_b):
    depth, d, n = mod_w.shape
    tn = 1536
    return pl.pallas_call(
        _mod_kernel,
        out_shape=jax.ShapeDtypeStruct((depth, MOD_ROWS, n), F32),
        grid=(depth, n // tn),
        in_specs=[
            pl.BlockSpec((MOD_ROWS, d), lambda l, j: (0, 0)),
            pl.BlockSpec((1, d, tn), lambda l, j: (l, 0, j)),
            pl.BlockSpec((1, 1, tn), lambda l, j: (l, 0, j)),
        ],
        out_specs=pl.BlockSpec((1, MOD_ROWS, tn), lambda l, j: (l, 0, j)),
        compiler_params=_cparams("arbitrary", "arbitrary"),
        name="modulation",
    )(cond, mod_w, mod_b.reshape(depth, 1, n))


class _Geom:
    def __init__(self, bp, lp, bs, ls):
        self.bp, self.lp, self.bs, self.ls = bp, lp, bs, ls
        self.tp = bp * lp
        self.t = bp * lp + bs * ls
        assert lp == TOKEN_TILE and ls % TOKEN_TILE == 0 and self.tp % ls == 0
        self.n_ptiles = self.tp // TOKEN_TILE
        self.tiles_per_lat = ls // TOKEN_TILE
        self.n_tiles = self.t // TOKEN_TILE

    def group(self, i):
        return jnp.where(i < self.n_ptiles, 0, 1 + (i - self.n_ptiles) // self.tiles_per_lat)

    def pos_block(self, i):
        return jnp.where(i < self.n_ptiles, 0, 1 + (i - self.n_ptiles) % self.tiles_per_lat)


def _mod_spec(geom, layer, which, d):
    def imap(i):
        return ((layer * MOD_ROWS + geom.group(i)) * N_MOD + which, 0, 0)
    return pl.BlockSpec((None, 1, d), imap)


def _proj_even_kernel(x_ref, g_ref, sh_ref, sc_ref, w_ref, cos_ref, sin_ref, dft_ref,
                      tc_ref, ts_ref, q_ref, k_ref, v_ref):
    h = _rms(x_ref[...], g_ref[...]) * (1.0 + sc_ref[...]) + sh_ref[...]
    p = _dot(h.astype(BF16), w_ref[...])
    cos = cos_ref[...]
    sin = sin_ref[...]
    dft = dft_ref[...]
    tcs, tss = [], []
    for g in range(A_GROUPS):
        t = _dot(p[:, g * A_GROUP_DIM:(g + 1) * A_GROUP_DIM].astype(BF16), dft)
        tcs.append(t[:, :A_GROUP_DIM])
        tss.append(t[:, A_GROUP_DIM:])
    tc_ref[...] = jnp.concatenate(tcs, axis=1).astype(BF16)
    ts_ref[...] = jnp.concatenate(tss, axis=1).astype(BF16)
    o = A_WIDTH
    q_ref[...] = _rope(p[:, o:o + B_Q_WIDTH], cos, sin, B_HEAD_DIM // 4).astype(BF16)
    o += B_Q_WIDTH
    k_ref[...] = _rope(p[:, o:o + B_KV_WIDTH], cos, sin, B_HEAD_DIM // 4)
    o += B_KV_WIDTH
    v_ref[...] = p[:, o:o + B_KV_WIDTH]


def _proj_odd_kernel(x_ref, g_ref, sh_ref, sc_ref, w_ref, cos_ref, sin_ref, qn_ref, kn_ref,
                     q_ref, k_ref, v_ref):
    h = _rms(x_ref[...], g_ref[...]) * (1.0 + sc_ref[...]) + sh_ref[...]
    p = _dot(h.astype(BF16), w_ref[...])
    cos = cos_ref[...]
    sin = sin_ref[...]
    q = _head_rms(p[:, :C_Q_WIDTH], qn_ref[...])
    k = _head_rms(p[:, C_Q_WIDTH:C_Q_WIDTH + C_KV_WIDTH], kn_ref[...])
    q_ref[...] = _rope(q, cos, sin, C_HEAD_DIM // 4).astype(BF16)
    k_ref[...] = _rope(k, cos, sin, C_HEAD_DIM // 4)
    v_ref[...] = p[:, C_Q_WIDTH + C_KV_WIDTH:]


def _in_projection(geom, layer, x, norm_g, mods, w, cos, sin, extras, kernel, out_widths, out_dtypes, name):
    t, d = x.shape
    tm = TOKEN_TILE
    n_out = w.shape[1]
    row = lambda i: (i, 0)
    const2 = lambda i: (0, 0)
    in_specs = [
        pl.BlockSpec((tm, d), row),
        pl.BlockSpec((1, d), const2),
        _mod_spec(geom, layer, 0, d),
        _mod_spec(geom, layer, 1, d),
        pl.BlockSpec((d, n_out), const2),
        pl.BlockSpec((tm, LANES), lambda i: (geom.pos_block(i), 0)),
        pl.BlockSpec((tm, LANES), lambda i: (geom.pos_block(i), 0)),
    ] + [pl.BlockSpec(e.shape, const2) for e in extras]
    return pl.pallas_call(
        kernel,
        out_shape=[jax.ShapeDtypeStruct((t, wd), dt) for wd, dt in zip(out_widths, out_dtypes)],
        grid=(geom.n_tiles,),
        in_specs=in_specs,
        out_specs=[pl.BlockSpec((tm, wd), row) for wd in out_widths],
        compiler_params=_cparams("arbitrary"),
        name=name,
    )(x, norm_g, mods, mods, w, cos, sin, *extras)


def _fourier_kernel(cl_ref, sl_ref, tc_ref, ts_ref, *rest):
    o_ref = rest[-1]
    o_ref[...] = (_dot(cl_ref[...], tc_ref[...]) - _dot(sl_ref[...], ts_ref[...])).astype(o_ref.dtype)


def _fourier_tokens(tc, ts, cl, sl, n_seq, seq_len, row0, prev):
    t, width = tc.shape
    tr = min(seq_len, 512)
    n_r = seq_len // tr
    assert row0 % seq_len == 0
    seq0 = row0 // seq_len
    out0 = row0 // tr
    in_specs = [
        pl.BlockSpec((tr, seq_len), lambda s, r: (r, 0)),
        pl.BlockSpec((tr, seq_len), lambda s, r: (r, 0)),
        pl.BlockSpec((seq_len, width), lambda s, r: (seq0 + s, 0)),
        pl.BlockSpec((seq_len, width), lambda s, r: (seq0 + s, 0)),
    ]
    args = [cl, sl, tc, ts]
    aliases = {}
    if prev is not None:
        in_specs.append(pl.BlockSpec(memory_space=pl.ANY))
        args.append(prev)
        aliases = {4: 0}
    return pl.pallas_call(
        _fourier_kernel,
        out_shape=jax.ShapeDtypeStruct((t, width), BF16),
        grid=(n_seq, n_r),
        in_specs=in_specs,
        out_specs=pl.BlockSpec((tr, width), lambda s, r: (out0 + s * n_r + r, 0)),
        input_output_aliases=aliases,
        compiler_params=_cparams("arbitrary", "arbitrary"),
        name="fourier_tokens",
    )(*args)


def _softmax_pv(q, parts, sink, scale):
    scores = []
    m = sink
    for k, _, mask in parts:
        s = _dot_nt(q, k) * scale
        if mask is not None:
            s = jnp.where(mask, s, MASKED)
        scores.append(s)
        mx = jnp.max(s, axis=-1, keepdims=True)
        m = mx if m is None else jnp.maximum(m, mx)
    den = None if sink is None else jnp.exp(sink - m)
    acc = None
    for (_, v, _), s in zip(parts, scores):
        e = jnp.exp(s - m)
        es = jnp.sum(e, axis=-1, keepdims=True)
        den = es if den is None else den + es
        o = _dot(e.astype(BF16), v)
        acc = o if acc is None else acc + o
    return acc / den


def _stack_heads(q, kv, groups, dh):
    return jnp.concatenate([q[:, (kv * groups + g) * dh:(kv * groups + g + 1) * dh] for g in range(groups)], axis=0)


def _sink_column(sink_ref, kv, groups, rows):
    return jnp.concatenate([jnp.full((rows, 1), sink_ref[kv * groups + g], F32) for g in range(groups)], axis=0)


def _attn_kernel(*refs, kv_heads, groups, dh, has_sink, has_ctx, window, q_tile, seq_len):
    refs = list(refs)
    sink_ref = refs.pop(0) if has_sink else None
    q_ref, k_ref, v_ref = refs[:3]
    ck_ref, cv_ref = (refs[3], refs[4]) if has_ctx else (None, None)
    o_ref = refs[-1]
    scale = dh ** -0.5
    q = q_ref[...]
    rows = q.shape[0]
    if window is None:
        k_loc = k_ref[...].astype(BF16)
        v_loc = v_ref[...].astype(BF16)
        mask = None
    else:
        n = pl.program_id(1)
        band = q_tile + 2 * window
        start = jnp.clip(n * q_tile - window, 0, seq_len - band)
        start = pl.multiple_of(start, LANES)
        k_loc = k_ref[pl.ds(start, band), :].astype(BF16)
        v_loc = v_ref[pl.ds(start, band), :].astype(BF16)
        qpos = n * q_tile + lax.broadcasted_iota(I32, (groups * rows, band), 0) % rows
        kpos = start + lax.broadcasted_iota(I32, (groups * rows, band), 1)
        mask = jnp.abs(kpos - qpos) <= window
    if has_ctx:
        k_ctx = ck_ref[...].astype(BF16)
        v_ctx = cv_ref[...].astype(BF16)
    outs = []
    for kv in range(kv_heads):
        sl = slice(kv * dh, (kv + 1) * dh)
        parts = [(k_loc[:, sl], v_loc[:, sl], mask)]
        if has_ctx:
            parts.append((k_ctx[:, sl], v_ctx[:, sl], None))
        sink = _sink_column(sink_ref, kv, groups, rows) if has_sink else None
        o = _softmax_pv(_stack_heads(q, kv, groups, dh), parts, sink, scale)
        outs.extend(o[g * rows:(g + 1) * rows] for g in range(groups))
    o_ref[...] = jnp.concatenate(outs, axis=1).astype(o_ref.dtype)


def _attention(q, k, v, ctx, sink, prev, *, n_seq, seq_len, row0, q_tile, kv_heads, groups, dh, window):
    t, qw = q.shape
    kw = k.shape[1]
    n_q = seq_len // q_tile
    assert row0 % seq_len == 0 and row0 % q_tile == 0
    seq0 = row0 // seq_len
    q0 = row0 // q_tile
    in_specs, args = [], []
    if sink is not None:
        in_specs.append(pl.BlockSpec(memory_space=pltpu.SMEM))
        args.append(sink)
    in_specs += [
        pl.BlockSpec((q_tile, qw), lambda s, n: (q0 + s * n_q + n, 0)),
        pl.BlockSpec((seq_len, kw), lambda s, n: (seq0 + s, 0)),
        pl.BlockSpec((seq_len, kw), lambda s, n: (seq0 + s, 0)),
    ]
    args += [q, k, v]
    if ctx is not None:
        p = ctx[0].shape[1]
        in_specs += [pl.BlockSpec((None, p, kw), lambda s, n: (s, 0, 0))] * 2
        args += list(ctx)
    aliases = {}
    if prev is not None:
        in_specs.append(pl.BlockSpec(memory_space=pl.ANY))
        aliases = {len(args): 0}
        args.append(prev)
    kern = functools.partial(
        _attn_kernel, kv_heads=kv_heads, groups=groups, dh=dh, has_sink=sink is not None,
        has_ctx=ctx is not None, window=window, q_tile=q_tile, seq_len=seq_len)
    return pl.pallas_call(
        kern,
        out_shape=jax.ShapeDtypeStruct((t, qw), BF16),
        grid=(n_seq, n_q),
        in_specs=in_specs,
        out_specs=pl.BlockSpec((q_tile, qw), lambda s, n: (q0 + s * n_q + n, 0)),
        input_output_aliases=aliases,
        compiler_params=_cparams("arbitrary", "arbitrary"),
        name="attention",
    )(*args)


def _out_route_kernel(*refs, n_mix):
    mix_refs = refs[:n_mix]
    w_refs = refs[n_mix:2 * n_mix]
    (x_ref, gate_ref, g2_ref, sh2_ref, sc2_ref, rw_ref, rb_ref,
     xo_ref, h_ref, pos_ref, gates_ref, rows_ref, off_ref) = refs[2 * n_mix:]

    acc = None
    for m_ref, w_ref in zip(mix_refs, w_refs):
        part = _dot(m_ref[...], w_ref[...])
        acc = part if acc is None else acc + part
    xn = x_ref[...] + gate_ref[...] * acc
    xo_ref[...] = xn
    h = _rms(xn, g2_ref[...]) * (1.0 + sc2_ref[...]) + sh2_ref[...]
    hb = h.astype(BF16)
    h_ref[...] = hb

    logits = _dot(hb, rw_ref[...]) + rb_ref[...]
    tm, ne = logits.shape
    lane = lax.broadcasted_iota(I32, (tm, ne), 1).astype(F32)
    lane4 = lax.broadcasted_iota(I32, (tm, TOP_K), 1)
    work = logits
    sels, vals = [], []
    for _ in range(TOP_K):
        mx = jnp.max(work, axis=-1, keepdims=True)
        first = jnp.min(jnp.where(work == mx, lane, float(ne)), axis=-1, keepdims=True)
        sel = lane == first
        work = jnp.where(sel, -jnp.inf, work)
        sels.append(sel)
        vals.append(mx)
    exps = [jnp.exp(v - vals[0]) for v in vals]
    den = exps[0] + exps[1] + exps[2] + exps[3]

    onehot = jnp.zeros((tm, ne), F32)
    for sel in sels:
        onehot = onehot + sel.astype(F32)
    r_i = lax.broadcasted_iota(I32, (tm, tm), 0)
    c_i = lax.broadcasted_iota(I32, (tm, tm), 1)
    before = jnp.where(c_i < r_i, 1.0, 0.0).astype(BF16)
    earlier = _dot(before, onehot.astype(BF16))

    cnt = jnp.sum(onehot, axis=0, keepdims=True)
    seg8 = jnp.floor((cnt + (SEG_ALIGN - 1.0)) * (1.0 / SEG_ALIGN))
    e_r = lax.broadcasted_iota(I32, (ne, ne), 0)
    e_c = lax.broadcasted_iota(I32, (ne, ne), 1)
    upper = jnp.where(e_r < e_c, 1.0, 0.0).astype(BF16)
    off8 = _dot(jnp.broadcast_to(seg8, (SEG_ALIGN, ne)).astype(BF16), upper)[0:1]
    seg_off = off8 * SEG_ALIGN
    base = seg_off + earlier

    pos_o = jnp.zeros((tm, TOP_K), I32)
    gate_o = jnp.zeros((tm, TOP_K), F32)
    for k in range(TOP_K):
        pos_k = jnp.sum(jnp.where(sels[k], base, 0.0), axis=-1, keepdims=True)
        pos_o = jnp.where(lane4 == k, pos_k.astype(I32), pos_o)
        gate_o = jnp.where(lane4 == k, exps[k] / den, gate_o)
    pos_ref[...] = pos_o
    gates_ref[...] = gate_o
    rows_ref[0] = (seg8 * SEG_ALIGN).astype(I32)
    off_ref[0] = seg_off.astype(I32)


def _out_route(geom, layer, mixes, w_parts, x, mods, norm_g, router_w, router_b):
    t, d = x.shape
    tm = TOKEN_TILE
    row = lambda i: (i, 0)
    const2 = lambda i: (0, 0)
    in_specs = [pl.BlockSpec((tm, m.shape[1]), row) for m in mixes]
    in_specs += [pl.BlockSpec(w.shape, const2) for w in w_parts]
    in_specs += [
        pl.BlockSpec((tm, d), row),
        _mod_spec(geom, layer, 2, d),
        pl.BlockSpec((1, d), const2),
        _mod_spec(geom, layer, 3, d),
        _mod_spec(geom, layer, 4, d),
        pl.BlockSpec(router_w.shape, const2),
        pl.BlockSpec((1, N_EXPERTS), const2),
    ]
    seg3 = lambda i: (i, 0, 0)
    out_shape = [
        jax.ShapeDtypeStruct((t, d), F32),
        jax.ShapeDtypeStruct((t, d), BF16),
        jax.ShapeDtypeStruct((t, TOP_K), I32),
        jax.ShapeDtypeStruct((t, TOP_K), F32),
        jax.ShapeDtypeStruct((geom.n_tiles, 1, N_EXPERTS), I32),
        jax.ShapeDtypeStruct((geom.n_tiles, 1, N_EXPERTS), I32),
    ]
    out_specs = [
        pl.BlockSpec((tm, d), row),
        pl.BlockSpec((tm, d), row),
        pl.BlockSpec((tm, TOP_K), row),
        pl.BlockSpec((tm, TOP_K), row),
        pl.BlockSpec((1, 1, N_EXPERTS), seg3),
        pl.BlockSpec((1, 1, N_EXPERTS), seg3),
    ]
    return pl.pallas_call(
        functools.partial(_out_route_kernel, n_mix=len(mixes)),
        out_shape=out_shape,
        grid=(geom.n_tiles,),
        in_specs=in_specs,
        out_specs=out_specs,
        compiler_params=_cparams("arbitrary"),
        name="out_route",
    )(*mixes, *w_parts, x, mods, norm_g, mods, mods, router_w, router_b)


def _pack_pairs(v):
    n = v.shape[1] // 2
    bits = lax.bitcast_convert_type(v, U32)
    return (bits[:, :n] & jnp.uint32(0xFFFF0000)) | (bits[:, n:] >> 16)


def _unpack_pairs(p):
    hi = lax.bitcast_convert_type(p & jnp.uint32(0xFFFF0000), F32)
    lo = lax.bitcast_convert_type(p << 16, F32)
    return jnp.concatenate([hi, lo], axis=1).astype(BF16)


def _segment_copies(i, rows_ref, off_ref, dst_ref, make_copy):
    def per_expert(e, total):
        a = i * N_EXPERTS + e
        n_chunks = rows_ref[a] // SEG_ALIGN
        local0 = off_ref[a]
        slot0 = dst_ref[a]

        def chunk(c, carry):
            local = pl.multiple_of(local0 + c * SEG_ALIGN, SEG_ALIGN)
            slot = pl.multiple_of(slot0 + c * SEG_ALIGN, SEG_ALIGN)
            make_copy(local, slot).start()
            return carry

        lax.fori_loop(0, n_chunks, chunk, 0)
        return total + n_chunks

    return lax.fori_loop(0, N_EXPERTS, per_expert, 0)


def _wait_copies(n, make_copy):
    def wait_one(c, carry):
        make_copy(0, 0).wait()
        return carry
    lax.fori_loop(0, n, wait_one, 0)


def _dispatch_kernel(rows_ref, off_ref, dst_ref, h_ref, pos_ref, xs_ref, sorted_ref, sem):
    i = pl.program_id(0)
    tm = h_ref.shape[0]
    n_sorted = sorted_ref.shape[0]
    pos = pos_ref[...]
    lane = lax.broadcasted_iota(I32, (tm, n_sorted), 1)
    hit = jnp.zeros((tm, n_sorted), F32)
    for k in range(TOP_K):
        hit = jnp.where(pos[:, k:k + 1] == lane, 1.0, hit)
    sorted_ref[...] = _pack_pairs(lax.dot_general(hit.astype(BF16), h_ref[...], (((0,), (0,)), ((), ())),
                                                  preferred_element_type=F32))

    def make_copy(local, slot):
        return pltpu.make_async_copy(sorted_ref.at[pl.ds(local, SEG_ALIGN)], xs_ref.at[pl.ds(slot, SEG_ALIGN)], sem)

    n = _segment_copies(i, rows_ref, off_ref, dst_ref, make_copy)
    _wait_copies(n, make_copy)


def _dispatch(h, pos, seg_rows, seg_off, seg_dst, n_slots):
    t, d = h.shape
    tm = TOKEN_TILE
    return pl.pallas_call(
        _dispatch_kernel,
        out_shape=jax.ShapeDtypeStruct((n_slots, d // 2), U32),
        grid_spec=pltpu.PrefetchScalarGridSpec(
            num_scalar_prefetch=3,
            grid=(t // tm,),
            in_specs=[
                pl.BlockSpec((tm, d), lambda i, *_: (i, 0)),
                pl.BlockSpec((tm, TOP_K), lambda i, *_: (i, 0)),
            ],
            out_specs=pl.BlockSpec(memory_space=pl.ANY),
            scratch_shapes=[pltpu.VMEM((SORTED_ROWS, d // 2), U32), pltpu.SemaphoreType.DMA],
        ),
        compiler_params=_cparams("arbitrary"),
        name="moe_dispatch",
    )(seg_rows, seg_off, seg_dst, h, pos)


def _ffn_kernel(rows_ref, start_ref, wgu_ref, bgu_ref, wd_ref, bd_ref, xs_ref, ys_ref,
                wgu_bf, wd_bf, xbuf, ybuf, sem_in, sem_out):
    e = pl.program_id(0)
    d_ff = wd_ref.shape[1]
    tb = xbuf.shape[1]
    n_rows = rows_ref[e]
    n_tiles = (n_rows + tb - 1) // tb
    base = start_ref[e]

    chunk = 128
    def cast_gu(c, carry):
        r = pl.multiple_of(c * chunk, chunk)
        wgu_bf[pl.ds(r, chunk), :] = wgu_ref[0, pl.ds(r, chunk), :].astype(BF16)
        return carry
    lax.fori_loop(0, wgu_ref.shape[1] // chunk, cast_gu, 0)
    def cast_d(c, carry):
        r = pl.multiple_of(c * chunk, chunk)
        wd_bf[pl.ds(r, chunk), :] = wd_ref[0, pl.ds(r, chunk), :].astype(BF16)
        return carry
    lax.fori_loop(0, d_ff // chunk, cast_d, 0)

    def x_copy(s, slot):
        r = pl.multiple_of(base + s * tb, EXPERT_TILE)
        return pltpu.make_async_copy(xs_ref.at[pl.ds(r, tb)], xbuf.at[slot], sem_in.at[slot])

    def y_copy(s, slot):
        r = pl.multiple_of(base + s * tb, EXPERT_TILE)
        return pltpu.make_async_copy(ybuf.at[slot], ys_ref.at[pl.ds(r, tb)], sem_out.at[slot])

    @pl.when(n_tiles > 0)
    def _():
        x_copy(0, 0).start()

    def tile(s, carry):
        slot = s % 2
        x_copy(s, slot).wait()

        @pl.when(s + 1 < n_tiles)
        def _():
            x_copy(s + 1, 1 - slot).start()

        @pl.when(s >= 2)
        def _():
            y_copy(s - 2, slot).wait()

        rows = s * tb + lax.broadcasted_iota(I32, (tb, 1), 0)
        x = jnp.where(rows < n_rows, _unpack_pairs(xbuf[slot]), 0.0).astype(BF16)
        gu = _dot(x, wgu_bf[...]) + bgu_ref[0]
        gate = jnp.minimum(gu[:, :d_ff], SWIGLU_LIMIT)
        up = jnp.clip(gu[:, d_ff:], -SWIGLU_LIMIT, SWIGLU_LIMIT)
        act = (up + 1.0) * (gate * (1.0 / (1.0 + jnp.exp(-SWIGLU_ALPHA * gate))))
        y = _dot(act.astype(BF16), wd_bf[...]) + bd_ref[0]
        ybuf[slot] = _pack_pairs(y.astype(BF16).astype(F32))
        y_copy(s, slot).start()
        return carry

    lax.fori_loop(0, n_tiles, tile, 0)

    @pl.when(n_tiles >= 2)
    def _():
        y_copy(n_tiles - 2, n_tiles % 2).wait()

    @pl.when(n_tiles >= 1)
    def _():
        y_copy(n_tiles - 1, (n_tiles - 1) % 2).wait()


def _expert_ffn(layer, xs, expert_rows, expert_start, w_gu, b_gu, w_down, b_down):
    n_slots, packed_w = xs.shape
    tb = FFN_TILE
    depth, ne, d, two_f = w_gu.shape
    d_ff = two_f // 2
    exp4 = lambda e, *_: (layer, e, 0, 0)
    return pl.pallas_call(
        _ffn_kernel,
        out_shape=jax.ShapeDtypeStruct((n_slots, packed_w), U32),
        grid_spec=pltpu.PrefetchScalarGridSpec(
            num_scalar_prefetch=2,
            grid=(ne,),
            in_specs=[
                pl.BlockSpec((None, 1, d, two_f), exp4),
                pl.BlockSpec((None, 1, 1, two_f), exp4),
                pl.BlockSpec((None, 1, d_ff, d), exp4),
                pl.BlockSpec((None, 1, 1, d), exp4),
                pl.BlockSpec(memory_space=pl.ANY),
            ],
            out_specs=pl.BlockSpec(memory_space=pl.ANY),
            scratch_shapes=[
                pltpu.VMEM((d, two_f), BF16), pltpu.VMEM((d_ff, d), BF16),
                pltpu.VMEM((2, tb, packed_w), U32), pltpu.VMEM((2, tb, packed_w), U32),
                pltpu.SemaphoreType.DMA((2,)), pltpu.SemaphoreType.DMA((2,)),
            ],
        ),
        compiler_params=_cparams("arbitrary"),
        name="expert_ffn",
    )(expert_rows, expert_start, w_gu, b_gu.reshape(depth, ne, 1, two_f),
      w_down, b_down.reshape(depth, ne, 1, d), xs)


def _combine_kernel(rows_ref, off_ref, dst_ref, x_ref, pos_ref, gates_ref, mg_ref, *rest, final):
    if final:
        fn_ref, ys_ref, o_ref, buf, sem = rest
    else:
        ys_ref, o_ref, buf, sem = rest
    i = pl.program_id(0)
    tm = x_ref.shape[0]
    n_sorted = buf.shape[0]

    @pl.when(i == 0)
    def _():
        buf[...] = jnp.zeros_like(buf)

    def make_copy(local, slot):
        return pltpu.make_async_copy(ys_ref.at[pl.ds(slot, SEG_ALIGN)], buf.at[pl.ds(local, SEG_ALIGN)], sem)

    n = _segment_copies(i, rows_ref, off_ref, dst_ref, make_copy)
    pos = pos_ref[...]
    g = gates_ref[...]
    lane = lax.broadcasted_iota(I32, (tm, n_sorted), 1)
    weight = jnp.zeros((tm, n_sorted), F32)
    for k in range(TOP_K):
        weight = jnp.where(pos[:, k:k + 1] == lane, g[:, k:k + 1], weight)
    _wait_copies(n, make_copy)
    y = _dot(weight.astype(BF16), _unpack_pairs(buf[...]))
    xn = x_ref[...] + mg_ref[...] * y
    if final:
        xn = _rms(xn, fn_ref[...])
    o_ref[...] = xn


def _combine(geom, layer, ys, pos, seg_rows, seg_off, seg_dst, x, gates, mods, final_g):
    t, d = x.shape
    tm = TOKEN_TILE
    final = final_g is not None

    def mod_imap(i, *_):
        return ((layer * MOD_ROWS + geom.group(i)) * N_MOD + 5, 0, 0)

    row = lambda i, *_: (i, 0)
    in_specs = [
        pl.BlockSpec((tm, d), row),
        pl.BlockSpec((tm, TOP_K), row),
        pl.BlockSpec((tm, TOP_K), row),
        pl.BlockSpec((None, 1, d), mod_imap),
    ]
    args = [seg_rows, seg_off, seg_dst, x, pos, gates, mods]
    if final:
        in_specs.append(pl.BlockSpec((1, d), lambda i, *_: (0, 0)))
        args.append(final_g)
    in_specs.append(pl.BlockSpec(memory_space=pl.ANY))
    args.append(ys)
    return pl.pallas_call(
        functools.partial(_combine_kernel, final=final),
        out_shape=jax.ShapeDtypeStruct((t, d), F32),
        grid_spec=pltpu.PrefetchScalarGridSpec(
            num_scalar_prefetch=3,
            grid=(t // tm,),
            in_specs=in_specs,
            out_specs=pl.BlockSpec((tm, d), row),
            scratch_shapes=[pltpu.VMEM((SORTED_ROWS, d // 2), U32), pltpu.SemaphoreType.DMA],
        ),
        compiler_params=_cparams("arbitrary"),
        name="moe_combine",
    )(*args)


def _moe(geom, layer, h, pos, gates, seg_rows, seg_off, x, mods, w_gu, b_gu, w_down, b_down, final_g):
    t = h.shape[0]
    tb = EXPERT_TILE
    n_tok_tiles = seg_rows.shape[0]
    max_rows = t * TOP_K + n_tok_tiles * N_EXPERTS * (SEG_ALIGN - 1)
    n_blocks = -(-max_rows // tb) + N_EXPERTS
    rows = seg_rows[:, 0, :]
    cnt = jnp.sum(rows, axis=0)
    n_tiles_e = (cnt + tb - 1) // tb
    tile_end = jnp.cumsum(n_tiles_e)
    tile_start = tile_end - n_tiles_e
    expert_start = (tile_start * tb).astype(I32)
    seg_dst = expert_start[None, :] + jnp.cumsum(rows, axis=0) - rows
    rows_flat = rows.reshape(-1).astype(I32)
    off_flat = seg_off.reshape(-1).astype(I32)
    dst_flat = seg_dst.reshape(-1).astype(I32)
    xs = _dispatch(h, pos, rows_flat, off_flat, dst_flat, n_blocks * tb + FFN_TILE - tb)
    ys = _expert_ffn(layer, xs, cnt.astype(I32), expert_start, w_gu, b_gu, w_down, b_down)
    return _combine(geom, layer, ys, pos, rows_flat, off_flat, dst_flat, x, gates, mods, final_g)


def kernel(x_prompt, x_sample, cache_b_k, cache_b_v, cache_c_k, cache_c_v, c, c_ctx,
           mod_w, mod_b, norm_mix, norm_ffn, even_w_in, even_w_out, even_sink,
           odd_w_in, odd_w_out, odd_q_norm, odd_k_norm, router_w, router_b,
           moe_w_gu, moe_b_gu, moe_w_down, moe_b_down, final_norm):
    bp, lp, d = x_prompt.shape
    bs, ls, _ = x_sample.shape
    past = cache_b_k.shape[2]
    depth = mod_w.shape[0]
    geom = _Geom(bp, lp, bs, ls)
    tp = geom.tp

    x = jnp.concatenate([x_prompt.reshape(tp, d), x_sample.reshape(bs * ls, d)], axis=0)
    cond = jnp.concatenate([c_ctx[None, :], c, jnp.zeros((MOD_ROWS - 1 - bs, d), F32)], axis=0)
    mods = _modulation(cond, mod_w, mod_b).reshape(depth * MOD_ROWS * N_MOD, 1, d)

    cn, sn = _dft_tables(A_GROUP_DIM)
    dft_chan = jnp.asarray(np.concatenate([cn, sn], axis=1), BF16)
    dft_p = [jnp.asarray(m, BF16) for m in _dft_tables(lp)]
    dft_s = [jnp.asarray(m, BF16) for m in _dft_tables(ls)]
    rope_b = [jnp.asarray(m) for m in _rope_tables(TOKEN_TILE, ls, B_HEAD_DIM)]
    rope_c = [jnp.asarray(m) for m in _rope_tables(TOKEN_TILE, ls, C_HEAD_DIM)]

    states = {"bk": [], "bv": [], "ck": [], "cv": []}
    for layer in range(depth):
        j = layer // 2
        g_mix = norm_mix[layer][None, :]
        g_ffn = norm_ffn[layer][None, :]
        if layer % 2 == 0:
            tc, ts, q, k, v = _in_projection(
                geom, layer, x, g_mix, mods, even_w_in[j].astype(BF16), rope_b[0], rope_b[1], [dft_chan],
                _proj_even_kernel, (A_WIDTH, A_WIDTH, B_Q_WIDTH, B_KV_WIDTH, B_KV_WIDTH),
                (BF16, BF16, BF16, F32, F32), "proj_even")
            states["bk"].append(k[:tp].reshape(bp, lp, B_KV_HEADS, B_HEAD_DIM))
            states["bv"].append(v[:tp].reshape(bp, lp, B_KV_HEADS, B_HEAD_DIM))
            four = _fourier_tokens(tc, ts, dft_p[0], dft_p[1], bp, lp, 0, None)
            four = _fourier_tokens(tc, ts, dft_s[0], dft_s[1], bs, ls, tp, four)
            sink = even_sink[j]
            common = dict(kv_heads=B_KV_HEADS, groups=B_HEADS // B_KV_HEADS, dh=B_HEAD_DIM)
            att = _attention(q, k, v, None, sink, None, n_seq=bp, seq_len=lp, row0=0, q_tile=lp,
                             window=None, **common)
            ctx = (cache_b_k[:, j].reshape(bs, past, B_KV_WIDTH), cache_b_v[:, j].reshape(bs, past, B_KV_WIDTH))
            att = _attention(q, k, v, ctx, sink, att, n_seq=bs, seq_len=ls, row0=tp, q_tile=ATTN_Q_TILE,
                             window=WINDOW, **common)
            w_out = even_w_out[j].astype(BF16)
            mixes = [four, att]
            w_parts = [w_out[:A_WIDTH], w_out[A_WIDTH:]]
        else:
            q, k, v = _in_projection(
                geom, layer, x, g_mix, mods, odd_w_in[j].astype(BF16), rope_c[0], rope_c[1],
                [odd_q_norm[j][None, :], odd_k_norm[j][None, :]],
                _proj_odd_kernel, (C_Q_WIDTH, C_KV_WIDTH, C_KV_WIDTH), (BF16, F32, F32), "proj_odd")
            states["ck"].append(k[:tp].reshape(bp, lp, C_KV_HEADS, C_HEAD_DIM))
            states["cv"].append(v[:tp].reshape(bp, lp, C_KV_HEADS, C_HEAD_DIM))
            common = dict(kv_heads=C_KV_HEADS, groups=C_HEADS // C_KV_HEADS, dh=C_HEAD_DIM, window=None)
            att = _attention(q, k, v, None, None, None, n_seq=bp, seq_len=lp, row0=0, q_tile=lp, **common)
            ctx = (cache_c_k[:, j].reshape(bs, past, C_KV_WIDTH), cache_c_v[:, j].reshape(bs, past, C_KV_WIDTH))
            att = _attention(q, k, v, ctx, None, att, n_seq=bs, seq_len=ls, row0=tp, q_tile=ATTN_Q_TILE, **common)
            mixes = [att]
            w_parts = [odd_w_out[j].astype(BF16)]
        x, h, pos, gates, seg_rows, seg_off = _out_route(
            geom, layer, mixes, w_parts, x, mods, g_ffn, router_w[layer].astype(BF16), router_b[layer][None, :])
        final_g = final_norm[None, :] if layer == depth - 1 else None
        x = _moe(geom, layer, h, pos, gates, seg_rows, seg_off, x, mods,
                 moe_w_gu, moe_b_gu, moe_w_down, moe_b_down, final_g)

    y_prompt = x[:tp].reshape(bp, lp, d)
    y_sample = x[tp:].reshape(bs, ls, d)
    return (y_prompt, y_sample,
            jnp.stack(states["bk"], axis=1), jnp.stack(states["bv"], axis=1),
            jnp.stack(states["ck"], axis=1), jnp.stack(states["cv"], axis=1))
```

```python
import functools

import numpy as np
import jax
import jax.numpy as jnp
from jax import lax
from jax.experimental import pallas as pl
from jax.experimental.pallas import tpu as pltpu

F32 = jnp.float32
BF16 = jnp.bfloat16
I32 = jnp.int32
U32 = jnp.uint32

GRID_W = 64
A_GROUPS = 4
A_GROUP_DIM = 128
A_WIDTH = A_GROUPS * A_GROUP_DIM
B_HEADS = 8
B_KV_HEADS = 2
B_HEAD_DIM = 64
B_Q_WIDTH = B_HEADS * B_HEAD_DIM
B_KV_WIDTH = B_KV_HEADS * B_HEAD_DIM
WINDOW = 128
C_HEADS = 8
C_KV_HEADS = 2
C_HEAD_DIM = 128
C_Q_WIDTH = C_HEADS * C_HEAD_DIM
C_KV_WIDTH = C_KV_HEADS * C_HEAD_DIM
ROPE_THETA = 10000.0
N_EXPERTS = 32
TOP_K = 4
SWIGLU_LIMIT = 7.0
SWIGLU_ALPHA = 1.702
EPS = 1e-6

LANES = 128
TOKEN_TILE = 256
EXPERT_TILE = 256
FFN_TILE = 256
ATTN_Q_TILE = 128
SEG_ALIGN = 8
SEG_CHUNK = 32
WAIT_CHUNK = 256
SORTED_ROWS = -(-(TOKEN_TILE * TOP_K + N_EXPERTS * (SEG_ALIGN - 1)) // LANES) * LANES
VMEM_LIMIT = 56 * 1024 * 1024
MASKED = -1e30
N_MOD = 6
MOD_ROWS = 8


def _cparams(*sem):
    return pltpu.CompilerParams(dimension_semantics=tuple(sem), vmem_limit_bytes=VMEM_LIMIT)


def _dot(a, b):
    return jnp.dot(a, b, preferred_element_type=F32)


def _dot_nt(a, b):
    return lax.dot_general(a, b, (((1,), (1,)), ((), ())), preferred_element_type=F32)


def _rms(x, g):
    return x * lax.rsqrt(jnp.mean(x * x, axis=-1, keepdims=True) + EPS) * g


def _dft_tables(n):
    j = np.arange(n, dtype=np.int64)
    ang = 2.0 * np.pi * ((j[:, None] * j[None, :]) % n).astype(np.float64) / n
    s = 1.0 / np.sqrt(n)
    return np.cos(ang) * s, np.sin(ang) * s


def _rope_tables(n_prompt_rows, n_latent, head_dim):
    quarter = head_dim // 4
    pos = np.arange(n_latent)
    row = (pos // GRID_W).astype(np.float32)
    col = (pos % GRID_W).astype(np.float32)
    inv = (np.float32(ROPE_THETA) ** (-np.arange(quarter, dtype=np.float32) / np.float32(quarter))).astype(np.float32)
    ang_row = (row[:, None] * inv[None, :]).astype(np.float32)
    ang_col = (col[:, None] * inv[None, :]).astype(np.float32)
    cos_h = np.concatenate([np.cos(ang_row)] * 2 + [np.cos(ang_col)] * 2, axis=1)
    sin_h = np.concatenate([-np.sin(ang_row), np.sin(ang_row), -np.sin(ang_col), np.sin(ang_col)], axis=1)
    reps = LANES // head_dim
    cos_l = np.tile(cos_h, (1, reps)).astype(np.float32)
    sin_l = np.tile(sin_h, (1, reps)).astype(np.float32)
    cos = np.concatenate([np.ones((n_prompt_rows, LANES), np.float32), cos_l], axis=0)
    sin = np.concatenate([np.zeros((n_prompt_rows, LANES), np.float32), sin_l], axis=0)
    return cos, sin


def _rope(x, cos, sin, quarter):
    lane = lax.broadcasted_iota(I32, (x.shape[0], LANES), 1)
    first = ((lane // quarter) % 2) == 0
    outs = []
    for c in range(x.shape[1] // LANES):
        xc = x[:, c * LANES:(c + 1) * LANES]
        partner = jnp.where(first, pltpu.roll(xc, LANES - quarter, 1), pltpu.roll(xc, quarter, 1))
        outs.append(xc * cos + partner * sin)
    return outs[0] if len(outs) == 1 else jnp.concatenate(outs, axis=1)


def _head_rms(x, g):
    outs = []
    for c in range(x.shape[1] // LANES):
        outs.append(_rms(x[:, c * LANES:(c + 1) * LANES], g))
    return outs[0] if len(outs) == 1 else jnp.concatenate(outs, axis=1)


def _mod_kernel(c_ref, w_ref, b_ref, o_ref):
    c = c_ref[...]
    s = c * (1.0 / (1.0 + jnp.exp(-c)))
    o_ref[0] = _dot(s.astype(BF16), w_ref[0].astype(BF16)) + b_ref[0]


def _modulation(cond, mod_w, mod_b):
    depth, d, n = mod_w.shape
    tn = 1536
    return pl.pallas_call(
        _mod_kernel,
        out_shape=jax.ShapeDtypeStruct((depth, MOD_ROWS, n), F32),
        grid=(depth, n // tn),
        in_specs=[
            pl.BlockSpec((MOD_ROWS, d), lambda l, j: (0, 0)),
            pl.BlockSpec((1, d, tn), lambda l, j: (l, 0, j)),
            pl.BlockSpec((1, 1, tn), lambda l, j: (l, 0, j)),
        ],
        out_specs=pl.BlockSpec((1, MOD_ROWS, tn), lambda l, j: (l, 0, j)),
        compiler_params=_cparams("arbitrary", "arbitrary"),
        name="modulation",
    )(cond, mod_w, mod_b.reshape(depth, 1, n))


class _Geom:
    def __init__(self, bp, lp, bs, ls):
        self.bp, self.lp, self.bs, self.ls = bp, lp, bs, ls
        self.tp = bp * lp
        self.t = bp * lp + bs * ls
        assert lp == TOKEN_TILE and ls % TOKEN_TILE == 0 and self.tp % ls == 0
        self.n_ptiles = self.tp // TOKEN_TILE
        self.tiles_per_lat = ls // TOKEN_TILE
        self.n_tiles = self.t // TOKEN_TILE

    def group(self, i):
        return jnp.where(i < self.n_ptiles, 0, 1 + (i - self.n_ptiles) // self.tiles_per_lat)

    def pos_block(self, i):
        return jnp.where(i < self.n_ptiles, 0, 1 + (i - self.n_ptiles) % self.tiles_per_lat)


def _mod_spec(geom, layer, which, d):
    def imap(i):
        return ((layer * MOD_ROWS + geom.group(i)) * N_MOD + which, 0, 0)
    return pl.BlockSpec((None, 1, d), imap)


def _stream_specs(geom, x, d):
    tm = TOKEN_TILE
    if isinstance(x, tuple):
        return ([pl.BlockSpec((tm, d), lambda i, *_: (jnp.minimum(i, geom.n_ptiles - 1), 0)),
                 pl.BlockSpec((tm, d), lambda i, *_: (jnp.maximum(i - geom.n_ptiles, 0), 0))], list(x))
    return [pl.BlockSpec((tm, d), lambda i, *_: (i, 0))], [x]


def _stream_tile(x_refs, n_ptiles):
    if len(x_refs) == 1:
        return x_refs[0][...]
    return jnp.where(pl.program_id(0) < n_ptiles, x_refs[0][...], x_refs[1][...])


def _proj_even_kernel(*refs, n_x, n_ptiles):
    x_refs = refs[:n_x]
    g_ref, sh_ref, sc_ref, w_ref, cos_ref, sin_ref, dft_ref, tc_ref, ts_ref, q_ref, k_ref, v_ref = refs[n_x:]
    h = _rms(_stream_tile(x_refs, n_ptiles), g_ref[...]) * (1.0 + sc_ref[...]) + sh_ref[...]
    p = _dot(h.astype(BF16), w_ref[...])
    cos = cos_ref[...]
    sin = sin_ref[...]
    dft = dft_ref[...]
    tcs, tss = [], []
    for g in range(A_GROUPS):
        t = _dot(p[:, g * A_GROUP_DIM:(g + 1) * A_GROUP_DIM].astype(BF16), dft)
        tcs.append(t[:, :A_GROUP_DIM])
        tss.append(t[:, A_GROUP_DIM:])
    tc_ref[...] = jnp.concatenate(tcs, axis=1).astype(BF16)
    ts_ref[...] = jnp.concatenate(tss, axis=1).astype(BF16)
    o = A_WIDTH
    q_ref[...] = _rope(p[:, o:o + B_Q_WIDTH], cos, sin, B_HEAD_DIM // 4).astype(BF16)
    o += B_Q_WIDTH
    k_ref[...] = _rope(p[:, o:o + B_KV_WIDTH], cos, sin, B_HEAD_DIM // 4)
    o += B_KV_WIDTH
    v_ref[...] = p[:, o:o + B_KV_WIDTH]


def _proj_odd_kernel(*refs, n_x, n_ptiles):
    x_refs = refs[:n_x]
    g_ref, sh_ref, sc_ref, w_ref, cos_ref, sin_ref, qn_ref, kn_ref, q_ref, k_ref, v_ref = refs[n_x:]
    h = _rms(_stream_tile(x_refs, n_ptiles), g_ref[...]) * (1.0 + sc_ref[...]) + sh_ref[...]
    p = _dot(h.astype(BF16), w_ref[...])
    cos = cos_ref[...]
    sin = sin_ref[...]
    q = _head_rms(p[:, :C_Q_WIDTH], qn_ref[...])
    k = _head_rms(p[:, C_Q_WIDTH:C_Q_WIDTH + C_KV_WIDTH], kn_ref[...])
    q_ref[...] = _rope(q, cos, sin, C_HEAD_DIM // 4).astype(BF16)
    k_ref[...] = _rope(k, cos, sin, C_HEAD_DIM // 4)
    v_ref[...] = p[:, C_Q_WIDTH + C_KV_WIDTH:]


def _in_projection(geom, layer, x, norm_g, mods, w, cos, sin, extras, kernel, out_widths, out_dtypes, name):
    t, d = geom.t, w.shape[0]
    tm = TOKEN_TILE
    n_out = w.shape[1]
    row = lambda i: (i, 0)
    const2 = lambda i: (0, 0)
    x_specs, x_args = _stream_specs(geom, x, d)
    in_specs = x_specs + [
        pl.BlockSpec((1, d), const2),
        _mod_spec(geom, layer, 0, d),
        _mod_spec(geom, layer, 1, d),
        pl.BlockSpec((d, n_out), const2),
        pl.BlockSpec((tm, LANES), lambda i: (geom.pos_block(i), 0)),
        pl.BlockSpec((tm, LANES), lambda i: (geom.pos_block(i), 0)),
    ] + [pl.BlockSpec(e.shape, const2) for e in extras]
    return pl.pallas_call(
        functools.partial(kernel, n_x=len(x_args), n_ptiles=geom.n_ptiles),
        out_shape=[jax.ShapeDtypeStruct((t, wd), dt) for wd, dt in zip(out_widths, out_dtypes)],
        grid=(geom.n_tiles,),
        in_specs=in_specs,
        out_specs=[pl.BlockSpec((tm, wd), row) for wd in out_widths],
        compiler_params=_cparams("arbitrary"),
        name=name,
    )(*x_args, norm_g, mods, mods, w, cos, sin, *extras)


def _fourier_kernel(cl_ref, sl_ref, tc_ref, ts_ref, *rest):
    o_ref = rest[-1]
    o_ref[...] = (_dot(cl_ref[...], tc_ref[...]) - _dot(sl_ref[...], ts_ref[...])).astype(o_ref.dtype)


def _fourier_tokens(tc, ts, cl, sl, n_seq, seq_len, row0, prev):
    t, width = tc.shape
    tr = min(seq_len, 512)
    n_r = seq_len // tr
    assert row0 % seq_len == 0
    seq0 = row0 // seq_len
    out0 = row0 // tr
    in_specs = [
        pl.BlockSpec((tr, seq_len), lambda s, r: (r, 0)),
        pl.BlockSpec((tr, seq_len), lambda s, r: (r, 0)),
        pl.BlockSpec((seq_len, width), lambda s, r: (seq0 + s, 0)),
        pl.BlockSpec((seq_len, width), lambda s, r: (seq0 + s, 0)),
    ]
    args = [cl, sl, tc, ts]
    aliases = {}
    if prev is not None:
        in_specs.append(pl.BlockSpec(memory_space=pl.ANY))
        args.append(prev)
        aliases = {4: 0}
    return pl.pallas_call(
        _fourier_kernel,
        out_shape=jax.ShapeDtypeStruct((t, width), BF16),
        grid=(n_seq, n_r),
        in_specs=in_specs,
        out_specs=pl.BlockSpec((tr, width), lambda s, r: (out0 + s * n_r + r, 0)),
        input_output_aliases=aliases,
        compiler_params=_cparams("arbitrary", "arbitrary"),
        name="fourier_tokens",
    )(*args)


def _softmax_pv(q, parts, sink, scale):
    scores = []
    m = sink
    for k, _, mask in parts:
        s = _dot_nt(q, k) * scale
        if mask is not None:
            s = jnp.where(mask, s, MASKED)
        scores.append(s)
        mx = jnp.max(s, axis=-1, keepdims=True)
        m = mx if m is None else jnp.maximum(m, mx)
    den = None if sink is None else jnp.exp(sink - m)
    acc = None
    for (_, v, _), s in zip(parts, scores):
        e = jnp.exp(s - m)
        es = jnp.sum(e, axis=-1, keepdims=True)
        den = es if den is None else den + es
        o = _dot(e.astype(BF16), v)
        acc = o if acc is None else acc + o
    return acc / den


def _stack_heads(q, kv, groups, dh):
    return jnp.concatenate([q[:, (kv * groups + g) * dh:(kv * groups + g + 1) * dh] for g in range(groups)], axis=0)


def _sink_column(sink_ref, kv, groups, rows):
    return jnp.concatenate([jnp.full((rows, 1), sink_ref[kv * groups + g], F32) for g in range(groups)], axis=0)


def _attn_kernel(*refs, kv_heads, groups, dh, has_sink, has_ctx, window, q_tile, seq_len):
    refs = list(refs)
    sink_ref = refs.pop(0) if has_sink else None
    q_ref, k_ref, v_ref = refs[:3]
    ck_ref, cv_ref = (refs[3], refs[4]) if has_ctx else (None, None)
    o_ref = refs[-1]
    scale = dh ** -0.5
    q = q_ref[...]
    rows = q.shape[0]
    if window is None:
        k_loc = k_ref[...].astype(BF16)
        v_loc = v_ref[...].astype(BF16)
        mask = None
    else:
        n = pl.program_id(1)
        band = q_tile + 2 * window
        start = jnp.clip(n * q_tile - window, 0, seq_len - band)
        start = pl.multiple_of(start, LANES)
        k_loc = k_ref[pl.ds(start, band), :].astype(BF16)
        v_loc = v_ref[pl.ds(start, band), :].astype(BF16)
        qpos = n * q_tile + lax.broadcasted_iota(I32, (groups * rows, band), 0) % rows
        kpos = start + lax.broadcasted_iota(I32, (groups * rows, band), 1)
        mask = jnp.abs(kpos - qpos) <= window
    if has_ctx:
        k_ctx = ck_ref[...].astype(BF16)
        v_ctx = cv_ref[...].astype(BF16)
    outs = []
    for kv in range(kv_heads):
        sl = slice(kv * dh, (kv + 1) * dh)
        parts = [(k_loc[:, sl], v_loc[:, sl], mask)]
        if has_ctx:
            parts.append((k_ctx[:, sl], v_ctx[:, sl], None))
        sink = _sink_column(sink_ref, kv, groups, rows) if has_sink else None
        o = _softmax_pv(_stack_heads(q, kv, groups, dh), parts, sink, scale)
        outs.extend(o[g * rows:(g + 1) * rows] for g in range(groups))
    o_ref[...] = jnp.concatenate(outs, axis=1).astype(o_ref.dtype)


def _attention(q, k, v, ctx, sink, prev, *, n_seq, seq_len, row0, q_tile, kv_heads, groups, dh, window):
    t, qw = q.shape
    kw = k.shape[1]
    n_q = seq_len // q_tile
    assert row0 % seq_len == 0 and row0 % q_tile == 0
    seq0 = row0 // seq_len
    q0 = row0 // q_tile
    in_specs, args = [], []
    if sink is not None:
        in_specs.append(pl.BlockSpec(memory_space=pltpu.SMEM))
        args.append(sink)
    in_specs += [
        pl.BlockSpec((q_tile, qw), lambda s, n: (q0 + s * n_q + n, 0)),
        pl.BlockSpec((seq_len, kw), lambda s, n: (seq0 + s, 0)),
        pl.BlockSpec((seq_len, kw), lambda s, n: (seq0 + s, 0)),
    ]
    args += [q, k, v]
    if ctx is not None:
        p = ctx[0].shape[1]
        in_specs += [pl.BlockSpec((None, p, kw), lambda s, n: (s, 0, 0))] * 2
        args += list(ctx)
    aliases = {}
    if prev is not None:
        in_specs.append(pl.BlockSpec(memory_space=pl.ANY))
        aliases = {len(args): 0}
        args.append(prev)
    kern = functools.partial(
        _attn_kernel, kv_heads=kv_heads, groups=groups, dh=dh, has_sink=sink is not None,
        has_ctx=ctx is not None, window=window, q_tile=q_tile, seq_len=seq_len)
    return pl.pallas_call(
        kern,
        out_shape=jax.ShapeDtypeStruct((t, qw), BF16),
        grid=(n_seq, n_q),
        in_specs=in_specs,
        out_specs=pl.BlockSpec((q_tile, qw), lambda s, n: (q0 + s * n_q + n, 0)),
        input_output_aliases=aliases,
        compiler_params=_cparams("arbitrary", "arbitrary"),
        name="attention",
    )(*args)


def _out_route_kernel(*refs, n_mix, n_x, n_ptiles):
    mix_refs = refs[:n_mix]
    w_refs = refs[n_mix:2 * n_mix]
    x_refs = refs[2 * n_mix:2 * n_mix + n_x]
    (gate_ref, g2_ref, sh2_ref, sc2_ref, rw_ref, rb_ref,
     xo_ref, h_ref, pos_ref, gates_ref, rows_ref, off_ref) = refs[2 * n_mix + n_x:]

    acc = None
    for m_ref, w_ref in zip(mix_refs, w_refs):
        part = _dot(m_ref[...], w_ref[...])
        acc = part if acc is None else acc + part
    xn = _stream_tile(x_refs, n_ptiles) + gate_ref[...] * acc
    xo_ref[...] = xn
    h = _rms(xn, g2_ref[...]) * (1.0 + sc2_ref[...]) + sh2_ref[...]
    hb = h.astype(BF16)
    h_ref[...] = hb

    logits = _dot(hb, rw_ref[...]) + rb_ref[...]
    tm, ne = logits.shape
    lane = lax.broadcasted_iota(I32, (tm, ne), 1).astype(F32)
    lane4 = lax.broadcasted_iota(I32, (tm, TOP_K), 1)
    work = logits
    sels, vals = [], []
    for _ in range(TOP_K):
        mx = jnp.max(work, axis=-1, keepdims=True)
        first = jnp.min(jnp.where(work == mx, lane, float(ne)), axis=-1, keepdims=True)
        sel = lane == first
        work = jnp.where(sel, -jnp.inf, work)
        sels.append(sel)
        vals.append(mx)
    exps = [jnp.exp(v - vals[0]) for v in vals]
    den = exps[0] + exps[1] + exps[2] + exps[3]

    onehot = jnp.zeros((tm, ne), F32)
    for sel in sels:
        onehot = onehot + sel.astype(F32)
    r_i = lax.broadcasted_iota(I32, (tm, tm), 0)
    c_i = lax.broadcasted_iota(I32, (tm, tm), 1)
    before = jnp.where(c_i < r_i, 1.0, 0.0).astype(BF16)
    earlier = _dot(before, onehot.astype(BF16))

    cnt = jnp.sum(onehot, axis=0, keepdims=True)
    seg8 = jnp.floor((cnt + (SEG_ALIGN - 1.0)) * (1.0 / SEG_ALIGN))
    e_r = lax.broadcasted_iota(I32, (ne, ne), 0)
    e_c = lax.broadcasted_iota(I32, (ne, ne), 1)
    upper = jnp.where(e_r < e_c, 1.0, 0.0).astype(BF16)
    off8 = _dot(jnp.broadcast_to(seg8, (SEG_ALIGN, ne)).astype(BF16), upper)[0:1]
    seg_off = off8 * SEG_ALIGN
    base = seg_off + earlier

    pos_o = jnp.zeros((tm, TOP_K), I32)
    gate_o = jnp.zeros((tm, TOP_K), F32)
    for k in range(TOP_K):
        pos_k = jnp.sum(jnp.where(sels[k], base, 0.0), axis=-1, keepdims=True)
        pos_o = jnp.where(lane4 == k, pos_k.astype(I32), pos_o)
        gate_o = jnp.where(lane4 == k, exps[k] / den, gate_o)
    pos_ref[...] = pos_o
    gates_ref[...] = gate_o
    rows_ref[0] = (seg8 * SEG_ALIGN).astype(I32)
    off_ref[0] = seg_off.astype(I32)


def _out_route(geom, layer, mixes, w_parts, x, mods, norm_g, router_w, router_b):
    t, d = geom.t, w_parts[0].shape[1]
    tm = TOKEN_TILE
    row = lambda i: (i, 0)
    const2 = lambda i: (0, 0)
    x_specs, x_args = _stream_specs(geom, x, d)
    in_specs = [pl.BlockSpec((tm, m.shape[1]), row) for m in mixes]
    in_specs += [pl.BlockSpec(w.shape, const2) for w in w_parts]
    in_specs += x_specs
    in_specs += [
        _mod_spec(geom, layer, 2, d),
        pl.BlockSpec((1, d), const2),
        _mod_spec(geom, layer, 3, d),
        _mod_spec(geom, layer, 4, d),
        pl.BlockSpec(router_w.shape, const2),
        pl.BlockSpec((1, N_EXPERTS), const2),
    ]
    seg3 = lambda i: (i, 0, 0)
    out_shape = [
        jax.ShapeDtypeStruct((t, d), F32),
        jax.ShapeDtypeStruct((t, d), BF16),
        jax.ShapeDtypeStruct((t, TOP_K), I32),
        jax.ShapeDtypeStruct((t, TOP_K), F32),
        jax.ShapeDtypeStruct((geom.n_tiles, 1, N_EXPERTS), I32),
        jax.ShapeDtypeStruct((geom.n_tiles, 1, N_EXPERTS), I32),
    ]
    out_specs = [
        pl.BlockSpec((tm, d), row),
        pl.BlockSpec((tm, d), row),
        pl.BlockSpec((tm, TOP_K), row),
        pl.BlockSpec((tm, TOP_K), row),
        pl.BlockSpec((1, 1, N_EXPERTS), seg3),
        pl.BlockSpec((1, 1, N_EXPERTS), seg3),
    ]
    return pl.pallas_call(
        functools.partial(_out_route_kernel, n_mix=len(mixes), n_x=len(x_args), n_ptiles=geom.n_ptiles),
        out_shape=out_shape,
        grid=(geom.n_tiles,),
        in_specs=in_specs,
        out_specs=out_specs,
        compiler_params=_cparams("arbitrary"),
        name="out_route",
    )(*mixes, *w_parts, *x_args, mods, norm_g, mods, mods, router_w, router_b)


def _pack_pairs(v):
    n = v.shape[1] // 2
    bits = lax.bitcast_convert_type(v, U32)
    return (bits[:, :n] & jnp.uint32(0xFFFF0000)) | (bits[:, n:] >> 16)


def _unpack_pairs(p):
    hi = lax.bitcast_convert_type(p & jnp.uint32(0xFFFF0000), F32)
    lo = lax.bitcast_convert_type(p << 16, F32)
    return jnp.concatenate([hi, lo], axis=1).astype(BF16)


def _segment_copies(i, rows_ref, off_ref, dst_ref, make_copy):
    def per_expert(e, total):
        a = i * N_EXPERTS + e
        n_rows = rows_ref[a]
        local0 = off_ref[a]
        slot0 = dst_ref[a]

        def big(c, carry):
            r = c * SEG_CHUNK
            make_copy(pl.multiple_of(local0 + r, SEG_ALIGN), pl.multiple_of(slot0 + r, SEG_ALIGN), SEG_CHUNK).start()
            return carry

        lax.fori_loop(0, n_rows // SEG_CHUNK, big, 0)
        size = SEG_CHUNK // 2
        while size >= SEG_ALIGN:
            done = n_rows - n_rows % (2 * size)

            @pl.when(n_rows % (2 * size) >= size)
            def _(done=done, size=size):
                make_copy(pl.multiple_of(local0 + done, SEG_ALIGN), pl.multiple_of(slot0 + done, SEG_ALIGN),
                          size).start()

            size //= 2
        return total + n_rows

    return lax.fori_loop(0, N_EXPERTS, per_expert, 0)


def _wait_copies(n_rows, make_copy):
    def wait_big(c, carry):
        make_copy(0, 0, WAIT_CHUNK).wait()
        return carry
    lax.fori_loop(0, n_rows // WAIT_CHUNK, wait_big, 0)
    size = WAIT_CHUNK // 2
    while size >= SEG_ALIGN:
        @pl.when(n_rows % (2 * size) >= size)
        def _(size=size):
            make_copy(0, 0, size).wait()

        size //= 2


def _dispatch_kernel(rows_ref, off_ref, dst_ref, h_ref, pos_ref, xs_ref, sorted_ref, sem):
    i = pl.program_id(0)
    tm = h_ref.shape[0]
    n_sorted = sorted_ref.shape[0]
    pos = pos_ref[...]
    lane = lax.broadcasted_iota(I32, (tm, n_sorted), 1)
    hit = jnp.zeros((tm, n_sorted), F32)
    for k in range(TOP_K):
        hit = jnp.where(pos[:, k:k + 1] == lane, 1.0, hit)
    sorted_ref[...] = _pack_pairs(lax.dot_general(hit.astype(BF16), h_ref[...], (((0,), (0,)), ((), ())),
                                                  preferred_element_type=F32))

    def make_copy(local, slot, rows):
        return pltpu.make_async_copy(sorted_ref.at[pl.ds(local, rows)], xs_ref.at[pl.ds(slot, rows)], sem)

    n = _segment_copies(i, rows_ref, off_ref, dst_ref, make_copy)
    _wait_copies(n, make_copy)


def _dispatch(h, pos, seg_rows, seg_off, seg_dst, n_slots):
    t, d = h.shape
    tm = TOKEN_TILE
    return pl.pallas_call(
        _dispatch_kernel,
        out_shape=jax.ShapeDtypeStruct((n_slots, d // 2), U32),
        grid_spec=pltpu.PrefetchScalarGridSpec(
            num_scalar_prefetch=3,
            grid=(t // tm,),
            in_specs=[
                pl.BlockSpec((tm, d), lambda i, *_: (i, 0)),
                pl.BlockSpec((tm, TOP_K), lambda i, *_: (i, 0)),
            ],
            out_specs=pl.BlockSpec(memory_space=pl.ANY),
            scratch_shapes=[pltpu.VMEM((SORTED_ROWS, d // 2), U32), pltpu.SemaphoreType.DMA],
        ),
        compiler_params=_cparams("arbitrary"),
        name="moe_dispatch",
    )(seg_rows, seg_off, seg_dst, h, pos)


def _ffn_kernel(rows_ref, start_ref, wgu_ref, bgu_ref, wd_ref, bd_ref, xs_ref, ys_ref,
                wgu_bf, wd_bf, xbuf, ybuf, sem_in, sem_out):
    e = pl.program_id(0)
    d_ff = wd_ref.shape[1]
    tb = xbuf.shape[1]
    n_rows = rows_ref[e]
    n_tiles = (n_rows + tb - 1) // tb
    base = start_ref[e]

    chunk = 128
    def cast_gu(c, carry):
        r = pl.multiple_of(c * chunk, chunk)
        wgu_bf[pl.ds(r, chunk), :] = wgu_ref[0, pl.ds(r, chunk), :].astype(BF16)
        return carry
    lax.fori_loop(0, wgu_ref.shape[1] // chunk, cast_gu, 0)
    def cast_d(c, carry):
        r = pl.multiple_of(c * chunk, chunk)
        wd_bf[pl.ds(r, chunk), :] = wd_ref[0, pl.ds(r, chunk), :].astype(BF16)
        return carry
    lax.fori_loop(0, d_ff // chunk, cast_d, 0)

    def x_copy(s, slot):
        r = pl.multiple_of(base + s * tb, EXPERT_TILE)
        return pltpu.make_async_copy(xs_ref.at[pl.ds(r, tb)], xbuf.at[slot], sem_in.at[slot])

    def y_copy(s, slot):
        r = pl.multiple_of(base + s * tb, EXPERT_TILE)
        return pltpu.make_async_copy(ybuf.at[slot], ys_ref.at[pl.ds(r, tb)], sem_out.at[slot])

    @pl.when(n_tiles > 0)
    def _():
        x_copy(0, 0).start()

    def tile(s, carry):
        slot = s % 2
        x_copy(s, slot).wait()

        @pl.when(s + 1 < n_tiles)
        def _():
            x_copy(s + 1, 1 - slot).start()

        @pl.when(s >= 2)
        def _():
            y_copy(s - 2, slot).wait()

        rows = s * tb + lax.broadcasted_iota(I32, (tb, 1), 0)
        x = jnp.where(rows < n_rows, _unpack_pairs(xbuf[slot]), 0.0).astype(BF16)
        gu = _dot(x, wgu_bf[...]) + bgu_ref[0]
        gate = jnp.minimum(gu[:, :d_ff], SWIGLU_LIMIT)
        up = jnp.clip(gu[:, d_ff:], -SWIGLU_LIMIT, SWIGLU_LIMIT)
        act = (up + 1.0) * (gate * (1.0 / (1.0 + jnp.exp(-SWIGLU_ALPHA * gate))))
        y = _dot(act.astype(BF16), wd_bf[...]) + bd_ref[0]
        ybuf[slot] = _pack_pairs(y.astype(BF16).astype(F32))
        y_copy(s, slot).start()
        return carry

    lax.fori_loop(0, n_tiles, tile, 0)

    @pl.when(n_tiles >= 2)
    def _():
        y_copy(n_tiles - 2, n_tiles % 2).wait()

    @pl.when(n_tiles >= 1)
    def _():
        y_copy(n_tiles - 1, (n_tiles - 1) % 2).wait()


def _expert_ffn(layer, xs, expert_rows, expert_start, w_gu, b_gu, w_down, b_down):
    n_slots, packed_w = xs.shape
    tb = FFN_TILE
    depth, ne, d, two_f = w_gu.shape
    d_ff = two_f // 2
    exp4 = lambda e, *_: (layer, e, 0, 0)
    return pl.pallas_call(
        _ffn_kernel,
        out_shape=jax.ShapeDtypeStruct((n_slots, packed_w), U32),
        grid_spec=pltpu.PrefetchScalarGridSpec(
            num_scalar_prefetch=2,
            grid=(ne,),
            in_specs=[
                pl.BlockSpec((None, 1, d, two_f), exp4),
                pl.BlockSpec((None, 1, 1, two_f), exp4),
                pl.BlockSpec((None, 1, d_ff, d), exp4),
                pl.BlockSpec((None, 1, 1, d), exp4),
                pl.BlockSpec(memory_space=pl.ANY),
            ],
            out_specs=pl.BlockSpec(memory_space=pl.ANY),
            scratch_shapes=[
                pltpu.VMEM((d, two_f), BF16), pltpu.VMEM((d_ff, d), BF16),
                pltpu.VMEM((2, tb, packed_w), U32), pltpu.VMEM((2, tb, packed_w), U32),
                pltpu.SemaphoreType.DMA((2,)), pltpu.SemaphoreType.DMA((2,)),
            ],
        ),
        compiler_params=_cparams("arbitrary"),
        name="expert_ffn",
    )(expert_rows, expert_start, w_gu, b_gu.reshape(depth, ne, 1, two_f),
      w_down, b_down.reshape(depth, ne, 1, d), xs)


def _combine_kernel(rows_ref, off_ref, dst_ref, x_ref, pos_ref, gates_ref, mg_ref, *rest, final, n_ptiles):
    if final:
        fn_ref, ys_ref, op_ref, os_ref, buf, sem = rest
    else:
        ys_ref, o_ref, buf, sem = rest
    i = pl.program_id(0)
    tm = x_ref.shape[0]
    n_sorted = buf.shape[0]

    @pl.when(i == 0)
    def _():
        buf[...] = jnp.zeros_like(buf)

    def make_copy(local, slot, rows):
        return pltpu.make_async_copy(ys_ref.at[pl.ds(slot, rows)], buf.at[pl.ds(local, rows)], sem)

    n = _segment_copies(i, rows_ref, off_ref, dst_ref, make_copy)
    pos = pos_ref[...]
    g = gates_ref[...]
    lane = lax.broadcasted_iota(I32, (tm, n_sorted), 1)
    weight = jnp.zeros((tm, n_sorted), F32)
    for k in range(TOP_K):
        weight = jnp.where(pos[:, k:k + 1] == lane, g[:, k:k + 1], weight)
    _wait_copies(n, make_copy)
    y = _dot(weight.astype(BF16), _unpack_pairs(buf[...]))
    xn = x_ref[...] + mg_ref[...] * y
    if final:
        xn = _rms(xn, fn_ref[...])

        @pl.when(i < n_ptiles)
        def _():
            op_ref[...] = xn

        @pl.when(i >= n_ptiles)
        def _():
            os_ref[...] = xn
    else:
        o_ref[...] = xn


def _combine(geom, layer, ys, pos, seg_rows, seg_off, seg_dst, x, gates, mods, final_g):
    t, d = x.shape
    tm = TOKEN_TILE
    final = final_g is not None

    def mod_imap(i, *_):
        return ((layer * MOD_ROWS + geom.group(i)) * N_MOD + 5, 0, 0)

    row = lambda i, *_: (i, 0)
    in_specs = [
        pl.BlockSpec((tm, d), row),
        pl.BlockSpec((tm, TOP_K), row),
        pl.BlockSpec((tm, TOP_K), row),
        pl.BlockSpec((None, 1, d), mod_imap),
    ]
    args = [seg_rows, seg_off, seg_dst, x, pos, gates, mods]
    if final:
        in_specs.append(pl.BlockSpec((1, d), lambda i, *_: (0, 0)))
        args.append(final_g)
    in_specs.append(pl.BlockSpec(memory_space=pl.ANY))
    args.append(ys)
    if final:
        n_pt = geom.n_ptiles
        out_shape = [jax.ShapeDtypeStruct((geom.tp, d), F32), jax.ShapeDtypeStruct((t - geom.tp, d), F32)]
        out_specs = [pl.BlockSpec((tm, d), lambda i, *_: (jnp.minimum(i, n_pt - 1), 0)),
                     pl.BlockSpec((tm, d), lambda i, *_: (jnp.maximum(i - n_pt, 0), 0))]
    else:
        out_shape = jax.ShapeDtypeStruct((t, d), F32)
        out_specs = pl.BlockSpec((tm, d), row)
    return pl.pallas_call(
        functools.partial(_combine_kernel, final=final, n_ptiles=geom.n_ptiles),
        out_shape=out_shape,
        grid_spec=pltpu.PrefetchScalarGridSpec(
            num_scalar_prefetch=3,
            grid=(t // tm,),
            in_specs=in_specs,
            out_specs=out_specs,
            scratch_shapes=[pltpu.VMEM((SORTED_ROWS, d // 2), U32), pltpu.SemaphoreType.DMA],
        ),
        compiler_params=_cparams("arbitrary"),
        name="moe_combine",
    )(*args)


def _moe(geom, layer, h, pos, gates, seg_rows, seg_off, x, mods, w_gu, b_gu, w_down, b_down, final_g):
    t = h.shape[0]
    tb = EXPERT_TILE
    n_tok_tiles = seg_rows.shape[0]
    max_rows = t * TOP_K + n_tok_tiles * N_EXPERTS * (SEG_ALIGN - 1)
    n_blocks = -(-max_rows // tb) + N_EXPERTS
    rows = seg_rows[:, 0, :]
    cnt = jnp.sum(rows, axis=0)
    n_tiles_e = (cnt + tb - 1) // tb
    tile_end = jnp.cumsum(n_tiles_e)
    tile_start = tile_end - n_tiles_e
    expert_start = (tile_start * tb).astype(I32)
    seg_dst = expert_start[None, :] + jnp.cumsum(rows, axis=0) - rows
    rows_flat = rows.reshape(-1).astype(I32)
    off_flat = seg_off.reshape(-1).astype(I32)
    dst_flat = seg_dst.reshape(-1).astype(I32)
    xs = _dispatch(h, pos, rows_flat, off_flat, dst_flat, n_blocks * tb + FFN_TILE - tb)
    ys = _expert_ffn(layer, xs, cnt.astype(I32), expert_start, w_gu, b_gu, w_down, b_down)
    return _combine(geom, layer, ys, pos, rows_flat, off_flat, dst_flat, x, gates, mods, final_g)


def kernel(x_prompt, x_sample, cache_b_k, cache_b_v, cache_c_k, cache_c_v, c, c_ctx,
           mod_w, mod_b, norm_mix, norm_ffn, even_w_in, even_w_out, even_sink,
           odd_w_in, odd_w_out, odd_q_norm, odd_k_norm, router_w, router_b,
           moe_w_gu, moe_b_gu, moe_w_down, moe_b_down, final_norm):
    bp, lp, d = x_prompt.shape
    bs, ls, _ = x_sample.shape
    past = cache_b_k.shape[2]
    depth = mod_w.shape[0]
    geom = _Geom(bp, lp, bs, ls)
    tp = geom.tp

    x = (x_prompt.reshape(tp, d), x_sample.reshape(bs * ls, d))
    cond = jnp.concatenate([c_ctx[None, :], c, jnp.zeros((MOD_ROWS - 1 - bs, d), F32)], axis=0)
    mods = _modulation(cond, mod_w, mod_b).reshape(depth * MOD_ROWS * N_MOD, 1, d)

    cn, sn = _dft_tables(A_GROUP_DIM)
    dft_chan = jnp.asarray(np.concatenate([cn, sn], axis=1), BF16)
    dft_p = [jnp.asarray(m, BF16) for m in _dft_tables(lp)]
    dft_s = [jnp.asarray(m, BF16) for m in _dft_tables(ls)]
    rope_b = [jnp.asarray(m) for m in _rope_tables(TOKEN_TILE, ls, B_HEAD_DIM)]
    rope_c = [jnp.asarray(m) for m in _rope_tables(TOKEN_TILE, ls, C_HEAD_DIM)]

    states = {"bk": [], "bv": [], "ck": [], "cv": []}
    for layer in range(depth):
        j = layer // 2
        g_mix = norm_mix[layer][None, :]
        g_ffn = norm_ffn[layer][None, :]
        if layer % 2 == 0:
            tc, ts, q, k, v = _in_projection(
                geom, layer, x, g_mix, mods, even_w_in[j].astype(BF16), rope_b[0], rope_b[1], [dft_chan],
                _proj_even_kernel, (A_WIDTH, A_WIDTH, B_Q_WIDTH, B_KV_WIDTH, B_KV_WIDTH),
                (BF16, BF16, BF16, F32, F32), "proj_even")
            states["bk"].append(k[:tp].reshape(bp, lp, B_KV_HEADS, B_HEAD_DIM))
            states["bv"].append(v[:tp].reshape(bp, lp, B_KV_HEADS, B_HEAD_DIM))
            four = _fourier_tokens(tc, ts, dft_p[0], dft_p[1], bp, lp, 0, None)
            four = _fourier_tokens(tc, ts, dft_s[0], dft_s[1], bs, ls, tp, four)
            sink = even_sink[j]
            common = dict(kv_heads=B_KV_HEADS, groups=B_HEADS // B_KV_HEADS, dh=B_HEAD_DIM)
            att = _attention(q, k, v, None, sink, None, n_seq=bp, seq_len=lp, row0=0, q_tile=lp,
                             window=None, **common)
            ctx = (cache_b_k[:, j].reshape(bs, past, B_KV_WIDTH), cache_b_v[:, j].reshape(bs, past, B_KV_WIDTH))
            att = _attention(q, k, v, ctx, sink, att, n_seq=bs, seq_len=ls, row0=tp, q_tile=ATTN_Q_TILE,
                             window=WINDOW, **common)
            w_out = even_w_out[j].astype(BF16)
            mixes = [four, att]
            w_parts = [w_out[:A_WIDTH], w_out[A_WIDTH:]]
        else:
            q, k, v = _in_projection(
                geom, layer, x, g_mix, mods, odd_w_in[j].astype(BF16), rope_c[0], rope_c[1],
                [odd_q_norm[j][None, :], odd_k_norm[j][None, :]],
                _proj_odd_kernel, (C_Q_WIDTH, C_KV_WIDTH, C_KV_WIDTH), (BF16, F32, F32), "proj_odd")
            states["ck"].append(k[:tp].reshape(bp, lp, C_KV_HEADS, C_HEAD_DIM))
            states["cv"].append(v[:tp].reshape(bp, lp, C_KV_HEADS, C_HEAD_DIM))
            common = dict(kv_heads=C_KV_HEADS, groups=C_HEADS // C_KV_HEADS, dh=C_HEAD_DIM, window=None)
            att = _attention(q, k, v, None, None, None, n_seq=bp, seq_len=lp, row0=0, q_tile=lp, **common)
            ctx = (cache_c_k[:, j].reshape(bs, past, C_KV_WIDTH), cache_c_v[:, j].reshape(bs, past, C_KV_WIDTH))
            att = _attention(q, k, v, ctx, None, att, n_seq=bs, seq_len=ls, row0=tp, q_tile=ATTN_Q_TILE, **common)
            mixes = [att]
            w_parts = [odd_w_out[j].astype(BF16)]
        x, h, pos, gates, seg_rows, seg_off = _out_route(
            geom, layer, mixes, w_parts, x, mods, g_ffn, router_w[layer].astype(BF16), router_b[layer][None, :])
        final_g = final_norm[None, :] if layer == depth - 1 else None
        x = _moe(geom, layer, h, pos, gates, seg_rows, seg_off, x, mods,
                 moe_w_gu, moe_b_gu, moe_w_down, moe_b_down, final_g)

    y_prompt = x[0].reshape(bp, lp, d)
    y_sample = x[1].reshape(bs, ls, d)
    return (y_prompt, y_sample,
            jnp.stack(states["bk"], axis=1), jnp.stack(states["bv"], axis=1),
            jnp.stack(states["ck"], axis=1), jnp.stack(states["cv"], axis=1))
```

```python
import functools

import numpy as np
import jax
import jax.numpy as jnp
from jax import lax
from jax.experimental import pallas as pl
from jax.experimental.pallas import tpu as pltpu

F32 = jnp.float32
BF16 = jnp.bfloat16
I32 = jnp.int32
U32 = jnp.uint32

GRID_W = 64
A_GROUPS = 4
A_GROUP_DIM = 128
A_WIDTH = A_GROUPS * A_GROUP_DIM
B_HEADS = 8
B_KV_HEADS = 2
B_HEAD_DIM = 64
B_Q_WIDTH = B_HEADS * B_HEAD_DIM
B_KV_WIDTH = B_KV_HEADS * B_HEAD_DIM
WINDOW = 128
C_HEADS = 8
C_KV_HEADS = 2
C_HEAD_DIM = 128
C_Q_WIDTH = C_HEADS * C_HEAD_DIM
C_KV_WIDTH = C_KV_HEADS * C_HEAD_DIM
ROPE_THETA = 10000.0
N_EXPERTS = 32
TOP_K = 4
SWIGLU_LIMIT = 7.0
SWIGLU_ALPHA = 1.702
EPS = 1e-6

LANES = 128
TOKEN_TILE = 256
EXPERT_TILE = 256
FFN_TILE = 256
ATTN_Q_TILE = 128
ATTN_KEY_CHUNK = 512
SEG_ALIGN = 8
SEG_CHUNK = 32
WAIT_CHUNK = 256
SORTED_ROWS = -(-(TOKEN_TILE * TOP_K + N_EXPERTS * (SEG_ALIGN - 1)) // LANES) * LANES
VMEM_LIMIT = 56 * 1024 * 1024
MASKED = -1e30
N_MOD = 6
MOD_ROWS = 8


def _cparams(*sem):
    return pltpu.CompilerParams(dimension_semantics=tuple(sem), vmem_limit_bytes=VMEM_LIMIT)


def _dot(a, b):
    return jnp.dot(a, b, preferred_element_type=F32)


def _dot_nt(a, b):
    return lax.dot_general(a, b, (((1,), (1,)), ((), ())), preferred_element_type=F32)


def _rms(x, g):
    return x * lax.rsqrt(jnp.mean(x * x, axis=-1, keepdims=True) + EPS) * g


def _dft_tables(n):
    j = np.arange(n, dtype=np.int64)
    ang = 2.0 * np.pi * ((j[:, None] * j[None, :]) % n).astype(np.float64) / n
    s = 1.0 / np.sqrt(n)
    return np.cos(ang) * s, np.sin(ang) * s


def _rope_tables(n_prompt_rows, n_latent, head_dim):
    quarter = head_dim // 4
    pos = np.arange(n_latent)
    row = (pos // GRID_W).astype(np.float32)
    col = (pos % GRID_W).astype(np.float32)
    inv = (np.float32(ROPE_THETA) ** (-np.arange(quarter, dtype=np.float32) / np.float32(quarter))).astype(np.float32)
    ang_row = (row[:, None] * inv[None, :]).astype(np.float32)
    ang_col = (col[:, None] * inv[None, :]).astype(np.float32)
    cos_h = np.concatenate([np.cos(ang_row)] * 2 + [np.cos(ang_col)] * 2, axis=1)
    sin_h = np.concatenate([-np.sin(ang_row), np.sin(ang_row), -np.sin(ang_col), np.sin(ang_col)], axis=1)
    reps = LANES // head_dim
    cos_l = np.tile(cos_h, (1, reps)).astype(np.float32)
    sin_l = np.tile(sin_h, (1, reps)).astype(np.float32)
    cos = np.concatenate([np.ones((n_prompt_rows, LANES), np.float32), cos_l], axis=0)
    sin = np.concatenate([np.zeros((n_prompt_rows, LANES), np.float32), sin_l], axis=0)
    return cos, sin


def _rope(x, cos, sin, quarter):
    lane = lax.broadcasted_iota(I32, (x.shape[0], LANES), 1)
    first = ((lane // quarter) % 2) == 0
    outs = []
    for c in range(x.shape[1] // LANES):
        xc = x[:, c * LANES:(c + 1) * LANES]
        partner = jnp.where(first, pltpu.roll(xc, LANES - quarter, 1), pltpu.roll(xc, quarter, 1))
        outs.append(xc * cos + partner * sin)
    return outs[0] if len(outs) == 1 else jnp.concatenate(outs, axis=1)


def _head_rms(x, g):
    outs = []
    for c in range(x.shape[1] // LANES):
        outs.append(_rms(x[:, c * LANES:(c + 1) * LANES], g))
    return outs[0] if len(outs) == 1 else jnp.concatenate(outs, axis=1)


def _mod_kernel(c_ref, w_ref, b_ref, o_ref):
    c = c_ref[...]
    s = c * (1.0 / (1.0 + jnp.exp(-c)))
    o_ref[0] = _dot(s.astype(BF16), w_ref[0].astype(BF16)) + b_ref[0]


def _modulation(cond, mod_w, mod_b):
    depth, d, n = mod_w.shape
    tn = 1536
    return pl.pallas_call(
        _mod_kernel,
        out_shape=jax.ShapeDtypeStruct((depth, MOD_ROWS, n), F32),
        grid=(depth, n // tn),
        in_specs=[
            pl.BlockSpec((MOD_ROWS, d), lambda l, j: (0, 0)),
            pl.BlockSpec((1, d, tn), lambda l, j: (l, 0, j)),
            pl.BlockSpec((1, 1, tn), lambda l, j: (l, 0, j)),
        ],
        out_specs=pl.BlockSpec((1, MOD_ROWS, tn), lambda l, j: (l, 0, j)),
        compiler_params=_cparams("arbitrary", "arbitrary"),
        name="modulation",
    )(cond, mod_w, mod_b.reshape(depth, 1, n))


class _Geom:
    def __init__(self, bp, lp, bs, ls):
        self.bp, self.lp, self.bs, self.ls = bp, lp, bs, ls
        self.tp = bp * lp
        self.t = bp * lp + bs * ls
        assert lp == TOKEN_TILE and ls % TOKEN_TILE == 0 and self.tp % ls == 0
        self.n_ptiles = self.tp // TOKEN_TILE
        self.tiles_per_lat = ls // TOKEN_TILE
        self.n_tiles = self.t // TOKEN_TILE

    def group(self, i):
        return jnp.where(i < self.n_ptiles, 0, 1 + (i - self.n_ptiles) // self.tiles_per_lat)

    def pos_block(self, i):
        return jnp.where(i < self.n_ptiles, 0, 1 + (i - self.n_ptiles) % self.tiles_per_lat)


def _mod_spec(geom, layer, which, d):
    def imap(i):
        return ((layer * MOD_ROWS + geom.group(i)) * N_MOD + which, 0, 0)
    return pl.BlockSpec((None, 1, d), imap)


def _stream_specs(geom, x, d):
    tm = TOKEN_TILE
    if isinstance(x, tuple):
        return ([pl.BlockSpec((tm, d), lambda i, *_: (jnp.minimum(i, geom.n_ptiles - 1), 0)),
                 pl.BlockSpec((tm, d), lambda i, *_: (jnp.maximum(i - geom.n_ptiles, 0), 0))], list(x))
    return [pl.BlockSpec((tm, d), lambda i, *_: (i, 0))], [x]


def _stream_tile(x_refs, n_ptiles):
    if len(x_refs) == 1:
        return x_refs[0][...]
    return jnp.where(pl.program_id(0) < n_ptiles, x_refs[0][...], x_refs[1][...])


def _store_kv(k, v, kb_ref, vb_ref, ks_ref, vs_ref, n_ptiles):
    kb_ref[...] = k.astype(BF16)
    vb_ref[...] = v.astype(BF16)

    @pl.when(pl.program_id(0) < n_ptiles)
    def _():
        ks_ref[...] = k
        vs_ref[...] = v


def _proj_even_kernel(*refs, n_x, n_ptiles):
    x_refs = refs[:n_x]
    (g_ref, sh_ref, sc_ref, w_ref, cos_ref, sin_ref, dft_ref,
     tc_ref, ts_ref, q_ref, kb_ref, vb_ref, ks_ref, vs_ref) = refs[n_x:]
    h = _rms(_stream_tile(x_refs, n_ptiles), g_ref[...]) * (1.0 + sc_ref[...]) + sh_ref[...]
    p = _dot(h.astype(BF16), w_ref[...])
    cos = cos_ref[...]
    sin = sin_ref[...]
    dft = dft_ref[...]
    tcs, tss = [], []
    for g in range(A_GROUPS):
        t = _dot(p[:, g * A_GROUP_DIM:(g + 1) * A_GROUP_DIM].astype(BF16), dft)
        tcs.append(t[:, :A_GROUP_DIM])
        tss.append(t[:, A_GROUP_DIM:])
    tc_ref[...] = jnp.concatenate(tcs, axis=1).astype(BF16)
    ts_ref[...] = jnp.concatenate(tss, axis=1).astype(BF16)
    o = A_WIDTH
    q = _rope(p[:, o:o + B_Q_WIDTH], cos, sin, B_HEAD_DIM // 4)
    q_ref[...] = (q * B_HEAD_DIM ** -0.5).astype(BF16)
    o += B_Q_WIDTH
    k = _rope(p[:, o:o + B_KV_WIDTH], cos, sin, B_HEAD_DIM // 4)
    o += B_KV_WIDTH
    _store_kv(k, p[:, o:o + B_KV_WIDTH], kb_ref, vb_ref, ks_ref, vs_ref, n_ptiles)


def _proj_odd_kernel(*refs, n_x, n_ptiles):
    x_refs = refs[:n_x]
    (g_ref, sh_ref, sc_ref, w_ref, cos_ref, sin_ref, qn_ref, kn_ref,
     q_ref, kb_ref, vb_ref, ks_ref, vs_ref) = refs[n_x:]
    h = _rms(_stream_tile(x_refs, n_ptiles), g_ref[...]) * (1.0 + sc_ref[...]) + sh_ref[...]
    p = _dot(h.astype(BF16), w_ref[...])
    cos = cos_ref[...]
    sin = sin_ref[...]
    q = _head_rms(p[:, :C_Q_WIDTH], qn_ref[...])
    k = _head_rms(p[:, C_Q_WIDTH:C_Q_WIDTH + C_KV_WIDTH], kn_ref[...])
    q_ref[...] = (_rope(q, cos, sin, C_HEAD_DIM // 4) * C_HEAD_DIM ** -0.5).astype(BF16)
    k = _rope(k, cos, sin, C_HEAD_DIM // 4)
    _store_kv(k, p[:, C_Q_WIDTH + C_KV_WIDTH:], kb_ref, vb_ref, ks_ref, vs_ref, n_ptiles)


def _in_projection(geom, layer, x, norm_g, mods, w, cos, sin, extras, kernel, out_widths, kv_width, name):
    t, d = geom.t, w.shape[0]
    tm = TOKEN_TILE
    n_out = w.shape[1]
    row = lambda i: (i, 0)
    const2 = lambda i: (0, 0)
    x_specs, x_args = _stream_specs(geom, x, d)
    in_specs = x_specs + [
        pl.BlockSpec((1, d), const2),
        _mod_spec(geom, layer, 0, d),
        _mod_spec(geom, layer, 1, d),
        pl.BlockSpec((d, n_out), const2),
        pl.BlockSpec((tm, LANES), lambda i: (geom.pos_block(i), 0)),
        pl.BlockSpec((tm, LANES), lambda i: (geom.pos_block(i), 0)),
    ] + [pl.BlockSpec(e.shape, const2) for e in extras]
    return pl.pallas_call(
        functools.partial(kernel, n_x=len(x_args), n_ptiles=geom.n_ptiles),
        out_shape=([jax.ShapeDtypeStruct((t, wd), BF16) for wd in out_widths]
                   + [jax.ShapeDtypeStruct((geom.tp, kv_width), F32)] * 2),
        grid=(geom.n_tiles,),
        in_specs=in_specs,
        out_specs=([pl.BlockSpec((tm, wd), row) for wd in out_widths]
                   + [pl.BlockSpec((tm, kv_width), lambda i: (jnp.minimum(i, geom.n_ptiles - 1), 0))] * 2),
        compiler_params=_cparams("arbitrary"),
        name=name,
    )(*x_args, norm_g, mods, mods, w, cos, sin, *extras)


def _fourier_kernel(cl_ref, sl_ref, tc_ref, ts_ref, *rest):
    o_ref = rest[-1]
    o_ref[...] = (_dot(cl_ref[...], tc_ref[...]) - _dot(sl_ref[...], ts_ref[...])).astype(o_ref.dtype)


def _fourier_tokens(tc, ts, cl, sl, n_seq, seq_len, row0, prev):
    t, width = tc.shape
    tr = min(seq_len, 512)
    n_r = seq_len // tr
    assert row0 % seq_len == 0
    seq0 = row0 // seq_len
    out0 = row0 // tr
    in_specs = [
        pl.BlockSpec((tr, seq_len), lambda s, r: (r, 0)),
        pl.BlockSpec((tr, seq_len), lambda s, r: (r, 0)),
        pl.BlockSpec((seq_len, width), lambda s, r: (seq0 + s, 0)),
        pl.BlockSpec((seq_len, width), lambda s, r: (seq0 + s, 0)),
    ]
    args = [cl, sl, tc, ts]
    aliases = {}
    if prev is not None:
        in_specs.append(pl.BlockSpec(memory_space=pl.ANY))
        args.append(prev)
        aliases = {4: 0}
    return pl.pallas_call(
        _fourier_kernel,
        out_shape=jax.ShapeDtypeStruct((t, width), BF16),
        grid=(n_seq, n_r),
        in_specs=in_specs,
        out_specs=pl.BlockSpec((tr, width), lambda s, r: (out0 + s * n_r + r, 0)),
        input_output_aliases=aliases,
        compiler_params=_cparams("arbitrary", "arbitrary"),
        name="fourier_tokens",
    )(*args)


def _attend(q, chunks, sink, o0, dh, den_col):
    m = sink
    acc = None
    den = None
    for k, v, mask in chunks:
        s = _dot_nt(q, k)
        if mask is not None:
            s = jnp.where(mask, s, MASKED)
        m_new = jnp.max(s, axis=-1, keepdims=True)
        if m is not None:
            m_new = jnp.maximum(m, m_new)
        p = jnp.exp(s - m_new)
        pv = _dot(p.astype(BF16), v)
        if acc is None:
            acc = pv
            if den_col is None:
                den = jnp.sum(p, axis=-1, keepdims=True)
        else:
            alpha = jnp.exp(m - m_new)
            acc = alpha * acc + pv
            if den_col is None:
                den = alpha * den + jnp.sum(p, axis=-1, keepdims=True)
        m = m_new
    if den_col is not None:
        den = acc[:, den_col:den_col + 1]
    if sink is not None:
        den = den + jnp.exp(sink - m)
    return acc[:, o0:o0 + dh] / den


def _attend_two_pass(q, chunks, sink):
    scores = []
    m = sink
    for k, _, mask in chunks:
        s = _dot_nt(q, k)
        if mask is not None:
            s = jnp.where(mask, s, MASKED)
        scores.append(s)
        mx = jnp.max(s, axis=-1, keepdims=True)
        m = mx if m is None else jnp.maximum(m, mx)
    den = None if sink is None else jnp.exp(sink - m)
    acc = None
    for (_, v, _), s in zip(chunks, scores):
        e = jnp.exp(s - m)
        es = jnp.sum(e, axis=-1, keepdims=True)
        den = es if den is None else den + es
        o = _dot(e.astype(BF16), v)
        acc = o if acc is None else acc + o
    return acc / den


def _stack_heads(q, kv, groups, dh):
    return jnp.concatenate([q[:, (kv * groups + g) * dh:(kv * groups + g + 1) * dh] for g in range(groups)], axis=0)


def _sink_column(sink_ref, kv, groups, rows):
    return jnp.concatenate([jnp.full((rows, 1), sink_ref[kv * groups + g], F32) for g in range(groups)], axis=0)


def _head_values(v, kv, dh, with_ones):
    if not with_ones:
        return v[:, kv * dh:(kv + 1) * dh], 0, None
    assert dh == LANES
    lane = lax.broadcasted_iota(I32, (v.shape[0], LANES), 1)
    ones = jnp.where(lane == 0, 1.0, 0.0).astype(BF16)
    return jnp.concatenate([v[:, kv * dh:(kv + 1) * dh], ones], axis=1), 0, dh


def _attn_kernel(*refs, kv_heads, groups, dh, has_sink, has_ctx, window, q_tile, seq_len, chunk):
    refs = list(refs)
    sink_ref = refs.pop(0) if has_sink else None
    q_ref, k_ref, v_ref = refs[:3]
    ck_ref, cv_ref = (refs[3], refs[4]) if has_ctx else (None, None)
    o_ref = refs[-1]
    q = q_ref[...]
    rows = q.shape[0]
    if window is None:
        spans = [(c * chunk, chunk) for c in range(seq_len // chunk)]
        mask = None
    else:
        n = pl.program_id(1)
        band = q_tile + 2 * window
        start = pl.multiple_of(jnp.clip(n * q_tile - window, 0, seq_len - band), LANES)
        spans = [(start, band)]
        qpos = n * q_tile + lax.broadcasted_iota(I32, (groups * rows, band), 0) % rows
        kpos = start + lax.broadcasted_iota(I32, (groups * rows, band), 1)
        mask = jnp.abs(kpos - qpos) <= window
    sources = [(k_ref[pl.ds(s0, n_s), :], v_ref[pl.ds(s0, n_s), :], mask) for s0, n_s in spans]
    if has_ctx:
        sources.append((ck_ref[...], cv_ref[...], None))
    outs = []
    for kv in range(kv_heads):
        online = dh == LANES and len(sources) > 1
        chunks = []
        for k, v, msk in sources:
            vh, o0, den_col = _head_values(v, kv, dh, with_ones=online)
            chunks.append((k[:, kv * dh:(kv + 1) * dh], vh, msk))
        sink = _sink_column(sink_ref, kv, groups, rows) if has_sink else None
        qh = _stack_heads(q, kv, groups, dh)
        o = _attend(qh, chunks, sink, o0, dh, den_col) if online else _attend_two_pass(qh, chunks, sink)
        outs.extend(o[g * rows:(g + 1) * rows] for g in range(groups))
    o_ref[...] = jnp.concatenate(outs, axis=1).astype(o_ref.dtype)


def _attention(q, k, v, ctx, sink, prev, *, n_seq, seq_len, row0, q_tile, kv_heads, groups, dh, window):
    t, qw = q.shape
    kw = k.shape[1]
    n_q = seq_len // q_tile
    assert row0 % seq_len == 0 and row0 % q_tile == 0
    seq0 = row0 // seq_len
    q0 = row0 // q_tile
    in_specs, args = [], []
    if sink is not None:
        in_specs.append(pl.BlockSpec(memory_space=pltpu.SMEM))
        args.append(sink)
    in_specs += [
        pl.BlockSpec((q_tile, qw), lambda s, n: (q0 + s * n_q + n, 0)),
        pl.BlockSpec((seq_len, kw), lambda s, n: (seq0 + s, 0)),
        pl.BlockSpec((seq_len, kw), lambda s, n: (seq0 + s, 0)),
    ]
    args += [q, k, v]
    if ctx is not None:
        p = ctx[0].shape[1]
        in_specs += [pl.BlockSpec((None, p, kw), lambda s, n: (s, 0, 0))] * 2
        args += list(ctx)
    aliases = {}
    if prev is not None:
        in_specs.append(pl.BlockSpec(memory_space=pl.ANY))
        aliases = {len(args): 0}
        args.append(prev)
    kern = functools.partial(
        _attn_kernel, kv_heads=kv_heads, groups=groups, dh=dh, has_sink=sink is not None,
        has_ctx=ctx is not None, window=window, q_tile=q_tile, seq_len=seq_len, chunk=min(seq_len, ATTN_KEY_CHUNK))
    return pl.pallas_call(
        kern,
        out_shape=jax.ShapeDtypeStruct((t, qw), BF16),
        grid=(n_seq, n_q),
        in_specs=in_specs,
        out_specs=pl.BlockSpec((q_tile, qw), lambda s, n: (q0 + s * n_q + n, 0)),
        input_output_aliases=aliases,
        compiler_params=_cparams("arbitrary", "arbitrary"),
        name="attention",
    )(*args)


def _out_route_kernel(*refs, n_mix, n_x, n_ptiles):
    mix_refs = refs[:n_mix]
    w_refs = refs[n_mix:2 * n_mix]
    x_refs = refs[2 * n_mix:2 * n_mix + n_x]
    (gate_ref, g2_ref, sh2_ref, sc2_ref, rw_ref, rb_ref,
     xo_ref, h_ref, pos_ref, gates_ref, rows_ref, off_ref) = refs[2 * n_mix + n_x:]

    acc = None
    for m_ref, w_ref in zip(mix_refs, w_refs):
        part = _dot(m_ref[...], w_ref[...])
        acc = part if acc is None else acc + part
    xn = _stream_tile(x_refs, n_ptiles) + gate_ref[...] * acc
    xo_ref[...] = xn
    h = _rms(xn, g2_ref[...]) * (1.0 + sc2_ref[...]) + sh2_ref[...]
    hb = h.astype(BF16)
    h_ref[...] = hb

    logits = _dot(hb, rw_ref[...]) + rb_ref[...]
    tm, ne = logits.shape
    lane = lax.broadcasted_iota(I32, (tm, ne), 1).astype(F32)
    lane4 = lax.broadcasted_iota(I32, (tm, TOP_K), 1)
    work = logits
    sels, vals = [], []
    for _ in range(TOP_K):
        mx = jnp.max(work, axis=-1, keepdims=True)
        first = jnp.min(jnp.where(work == mx, lane, float(ne)), axis=-1, keepdims=True)
        sel = lane == first
        work = jnp.where(sel, -jnp.inf, work)
        sels.append(sel)
        vals.append(mx)
    exps = [jnp.exp(v - vals[0]) for v in vals]
    den = exps[0] + exps[1] + exps[2] + exps[3]

    onehot = jnp.zeros((tm, ne), F32)
    for sel in sels:
        onehot = onehot + sel.astype(F32)
    r_i = lax.broadcasted_iota(I32, (tm, tm), 0)
    c_i = lax.broadcasted_iota(I32, (tm, tm), 1)
    before = jnp.where(c_i < r_i, 1.0, 0.0).astype(BF16)
    earlier = _dot(before, onehot.astype(BF16))

    cnt = jnp.sum(onehot, axis=0, keepdims=True)
    seg8 = jnp.floor((cnt + (SEG_ALIGN - 1.0)) * (1.0 / SEG_ALIGN))
    e_r = lax.broadcasted_iota(I32, (ne, ne), 0)
    e_c = lax.broadcasted_iota(I32, (ne, ne), 1)
    upper = jnp.where(e_r < e_c, 1.0, 0.0).astype(BF16)
    off8 = _dot(jnp.broadcast_to(seg8, (SEG_ALIGN, ne)).astype(BF16), upper)[0:1]
    seg_off = off8 * SEG_ALIGN
    base = seg_off + earlier

    pos_o = jnp.zeros((tm, TOP_K), I32)
    gate_o = jnp.zeros((tm, TOP_K), F32)
    for k in range(TOP_K):
        pos_k = jnp.sum(jnp.where(sels[k], base, 0.0), axis=-1, keepdims=True)
        pos_o = jnp.where(lane4 == k, pos_k.astype(I32), pos_o)
        gate_o = jnp.where(lane4 == k, exps[k] / den, gate_o)
    pos_ref[...] = pos_o
    gates_ref[...] = gate_o
    rows_ref[0] = (seg8 * SEG_ALIGN).astype(I32)
    off_ref[0] = seg_off.astype(I32)


def _out_route(geom, layer, mixes, w_parts, x, mods, norm_g, router_w, router_b):
    t, d = geom.t, w_parts[0].shape[1]
    tm = TOKEN_TILE
    row = lambda i: (i, 0)
    const2 = lambda i: (0, 0)
    x_specs, x_args = _stream_specs(geom, x, d)
    in_specs = [pl.BlockSpec((tm, m.shape[1]), row) for m in mixes]
    in_specs += [pl.BlockSpec(w.shape, const2) for w in w_parts]
    in_specs += x_specs
    in_specs += [
        _mod_spec(geom, layer, 2, d),
        pl.BlockSpec((1, d), const2),
        _mod_spec(geom, layer, 3, d),
        _mod_spec(geom, layer, 4, d),
        pl.BlockSpec(router_w.shape, const2),
        pl.BlockSpec((1, N_EXPERTS), const2),
    ]
    seg3 = lambda i: (i, 0, 0)
    out_shape = [
        jax.ShapeDtypeStruct((t, d), F32),
        jax.ShapeDtypeStruct((t, d), BF16),
        jax.ShapeDtypeStruct((t, TOP_K), I32),
        jax.ShapeDtypeStruct((t, TOP_K), F32),
        jax.ShapeDtypeStruct((geom.n_tiles, 1, N_EXPERTS), I32),
        jax.ShapeDtypeStruct((geom.n_tiles, 1, N_EXPERTS), I32),
    ]
    out_specs = [
        pl.BlockSpec((tm, d), row),
        pl.BlockSpec((tm, d), row),
        pl.BlockSpec((tm, TOP_K), row),
        pl.BlockSpec((tm, TOP_K), row),
        pl.BlockSpec((1, 1, N_EXPERTS), seg3),
        pl.BlockSpec((1, 1, N_EXPERTS), seg3),
    ]
    return pl.pallas_call(
        functools.partial(_out_route_kernel, n_mix=len(mixes), n_x=len(x_args), n_ptiles=geom.n_ptiles),
        out_shape=out_shape,
        grid=(geom.n_tiles,),
        in_specs=in_specs,
        out_specs=out_specs,
        compiler_params=_cparams("arbitrary"),
        name="out_route",
    )(*mixes, *w_parts, *x_args, mods, norm_g, mods, mods, router_w, router_b)


def _pack_pairs(v):
    n = v.shape[1] // 2
    bits = lax.bitcast_convert_type(v, U32)
    return (bits[:, :n] & jnp.uint32(0xFFFF0000)) | (bits[:, n:] >> 16)


def _unpack_pairs(p):
    hi = lax.bitcast_convert_type(p & jnp.uint32(0xFFFF0000), F32)
    lo = lax.bitcast_convert_type(p << 16, F32)
    return jnp.concatenate([hi, lo], axis=1).astype(BF16)


def _segment_copies(i, rows_ref, off_ref, dst_ref, make_copy):
    def per_expert(e, total):
        a = i * N_EXPERTS + e
        n_rows = rows_ref[a]
        local0 = off_ref[a]
        slot0 = dst_ref[a]

        def big(c, carry):
            r = c * SEG_CHUNK
            make_copy(pl.multiple_of(local0 + r, SEG_ALIGN), pl.multiple_of(slot0 + r, SEG_ALIGN), SEG_CHUNK).start()
            return carry

        lax.fori_loop(0, n_rows // SEG_CHUNK, big, 0)
        size = SEG_CHUNK // 2
        while size >= SEG_ALIGN:
            done = n_rows - n_rows % (2 * size)

            @pl.when(n_rows % (2 * size) >= size)
            def _(done=done, size=size):
                make_copy(pl.multiple_of(local0 + done, SEG_ALIGN), pl.multiple_of(slot0 + done, SEG_ALIGN),
                          size).start()

            size //= 2
        return total + n_rows

    return lax.fori_loop(0, N_EXPERTS, per_expert, 0)


def _wait_copies(n_rows, make_copy):
    def wait_big(c, carry):
        make_copy(0, 0, WAIT_CHUNK).wait()
        return carry
    lax.fori_loop(0, n_rows // WAIT_CHUNK, wait_big, 0)
    size = WAIT_CHUNK // 2
    while size >= SEG_ALIGN:
        @pl.when(n_rows % (2 * size) >= size)
        def _(size=size):
            make_copy(0, 0, size).wait()

        size //= 2


def _dispatch_kernel(rows_ref, off_ref, dst_ref, h_ref, pos_ref, xs_ref, sorted_ref, sem):
    i = pl.program_id(0)
    tm = h_ref.shape[0]
    n_sorted = sorted_ref.shape[0]
    pos = pos_ref[...]
    lane = lax.broadcasted_iota(I32, (tm, n_sorted), 1)
    hit = jnp.zeros((tm, n_sorted), F32)
    for k in range(TOP_K):
        hit = jnp.where(pos[:, k:k + 1] == lane, 1.0, hit)
    sorted_ref[...] = _pack_pairs(lax.dot_general(hit.astype(BF16), h_ref[...], (((0,), (0,)), ((), ())),
                                                  preferred_element_type=F32))

    def make_copy(local, slot, rows):
        return pltpu.make_async_copy(sorted_ref.at[pl.ds(local, rows)], xs_ref.at[pl.ds(slot, rows)], sem)

    n = _segment_copies(i, rows_ref, off_ref, dst_ref, make_copy)
    _wait_copies(n, make_copy)


def _dispatch(h, pos, seg_rows, seg_off, seg_dst, n_slots):
    t, d = h.shape
    tm = TOKEN_TILE
    return pl.pallas_call(
        _dispatch_kernel,
        out_shape=jax.ShapeDtypeStruct((n_slots, d // 2), U32),
        grid_spec=pltpu.PrefetchScalarGridSpec(
            num_scalar_prefetch=3,
            grid=(t // tm,),
            in_specs=[
                pl.BlockSpec((tm, d), lambda i, *_: (i, 0)),
                pl.BlockSpec((tm, TOP_K), lambda i, *_: (i, 0)),
            ],
            out_specs=pl.BlockSpec(memory_space=pl.ANY),
            scratch_shapes=[pltpu.VMEM((SORTED_ROWS, d // 2), U32), pltpu.SemaphoreType.DMA],
        ),
        compiler_params=_cparams("arbitrary"),
        name="moe_dispatch",
    )(seg_rows, seg_off, seg_dst, h, pos)


def _ffn_kernel(rows_ref, start_ref, wgu_ref, bgu_ref, wd_ref, bd_ref, xs_ref, ys_ref,
                wgu_bf, wd_bf, xbuf, ybuf, sem_in, sem_out):
    e = pl.program_id(0)
    d_ff = wd_ref.shape[1]
    tb = xbuf.shape[1]
    n_rows = rows_ref[e]
    n_tiles = (n_rows + tb - 1) // tb
    base = start_ref[e]

    chunk = 128
    def cast_gu(c, carry):
        r = pl.multiple_of(c * chunk, chunk)
        wgu_bf[pl.ds(r, chunk), :] = wgu_ref[0, pl.ds(r, chunk), :].astype(BF16)
        return carry
    lax.fori_loop(0, wgu_ref.shape[1] // chunk, cast_gu, 0)
    def cast_d(c, carry):
        r = pl.multiple_of(c * chunk, chunk)
        wd_bf[pl.ds(r, chunk), :] = wd_ref[0, pl.ds(r, chunk), :].astype(BF16)
        return carry
    lax.fori_loop(0, d_ff // chunk, cast_d, 0)

    def x_copy(s, slot):
        r = pl.multiple_of(base + s * tb, EXPERT_TILE)
        return pltpu.make_async_copy(xs_ref.at[pl.ds(r, tb)], xbuf.at[slot], sem_in.at[slot])

    def y_copy(s, slot):
        r = pl.multiple_of(base + s * tb, EXPERT_TILE)
        return pltpu.make_async_copy(ybuf.at[slot], ys_ref.at[pl.ds(r, tb)], sem_out.at[slot])

    @pl.when(n_tiles > 0)
    def _():
        x_copy(0, 0).start()

    def tile(s, carry):
        slot = s % 2
        x_copy(s, slot).wait()

        @pl.when(s + 1 < n_tiles)
        def _():
            x_copy(s + 1, 1 - slot).start()

        @pl.when(s >= 2)
        def _():
            y_copy(s - 2, slot).wait()

        rows = s * tb + lax.broadcasted_iota(I32, (tb, 1), 0)
        x = jnp.where(rows < n_rows, _unpack_pairs(xbuf[slot]), 0.0).astype(BF16)
        gu = _dot(x, wgu_bf[...]) + bgu_ref[0]
        gate = jnp.minimum(gu[:, :d_ff], SWIGLU_LIMIT)
        up = jnp.clip(gu[:, d_ff:], -SWIGLU_LIMIT, SWIGLU_LIMIT)
        act = (up + 1.0) * (gate * (1.0 / (1.0 + jnp.exp(-SWIGLU_ALPHA * gate))))
        y = _dot(act.astype(BF16), wd_bf[...]) + bd_ref[0]
        ybuf[slot] = _pack_pairs(y.astype(BF16).astype(F32))
        y_copy(s, slot).start()
        return carry

    lax.fori_loop(0, n_tiles, tile, 0)

    @pl.when(n_tiles >= 2)
    def _():
        y_copy(n_tiles - 2, n_tiles % 2).wait()

    @pl.when(n_tiles >= 1)
    def _():
        y_copy(n_tiles - 1, (n_tiles - 1) % 2).wait()


def _expert_ffn(layer, xs, expert_rows, expert_start, w_gu, b_gu, w_down, b_down):
    n_slots, packed_w = xs.shape
    tb = FFN_TILE
    depth, ne, d, two_f = w_gu.shape
    d_ff = two_f // 2
    exp4 = lambda e, *_: (layer, e, 0, 0)
    return pl.pallas_call(
        _ffn_kernel,
        out_shape=jax.ShapeDtypeStruct((n_slots, packed_w), U32),
        grid_spec=pltpu.PrefetchScalarGridSpec(
            num_scalar_prefetch=2,
            grid=(ne,),
            in_specs=[
                pl.BlockSpec((None, 1, d, two_f), exp4),
                pl.BlockSpec((None, 1, 1, two_f), exp4),
                pl.BlockSpec((None, 1, d_ff, d), exp4),
                pl.BlockSpec((None, 1, 1, d), exp4),
                pl.BlockSpec(memory_space=pl.ANY),
            ],
            out_specs=pl.BlockSpec(memory_space=pl.ANY),
            scratch_shapes=[
                pltpu.VMEM((d, two_f), BF16), pltpu.VMEM((d_ff, d), BF16),
                pltpu.VMEM((2, tb, packed_w), U32), pltpu.VMEM((2, tb, packed_w), U32),
                pltpu.SemaphoreType.DMA((2,)), pltpu.SemaphoreType.DMA((2,)),
            ],
        ),
        compiler_params=_cparams("arbitrary"),
        name="expert_ffn",
    )(expert_rows, expert_start, w_gu, b_gu.reshape(depth, ne, 1, two_f),
      w_down, b_down.reshape(depth, ne, 1, d), xs)


def _combine_kernel(rows_ref, off_ref, dst_ref, x_ref, pos_ref, gates_ref, mg_ref, *rest, final, n_ptiles):
    if final:
        fn_ref, ys_ref, op_ref, os_ref, buf, sem = rest
    else:
        ys_ref, o_ref, buf, sem = rest
    i = pl.program_id(0)
    tm = x_ref.shape[0]
    n_sorted = buf.shape[0]

    @pl.when(i == 0)
    def _():
        buf[...] = jnp.zeros_like(buf)

    def make_copy(local, slot, rows):
        return pltpu.make_async_copy(ys_ref.at[pl.ds(slot, rows)], buf.at[pl.ds(local, rows)], sem)

    n = _segment_copies(i, rows_ref, off_ref, dst_ref, make_copy)
    pos = pos_ref[...]
    g = gates_ref[...]
    lane = lax.broadcasted_iota(I32, (tm, n_sorted), 1)
    weight = jnp.zeros((tm, n_sorted), F32)
    for k in range(TOP_K):
        weight = jnp.where(pos[:, k:k + 1] == lane, g[:, k:k + 1], weight)
    _wait_copies(n, make_copy)
    y = _dot(weight.astype(BF16), _unpack_pairs(buf[...]))
    xn = x_ref[...] + mg_ref[...] * y
    if final:
        xn = _rms(xn, fn_ref[...])

        @pl.when(i < n_ptiles)
        def _():
            op_ref[...] = xn

        @pl.when(i >= n_ptiles)
        def _():
            os_ref[...] = xn
    else:
        o_ref[...] = xn


def _combine(geom, layer, ys, pos, seg_rows, seg_off, seg_dst, x, gates, mods, final_g):
    t, d = x.shape
    tm = TOKEN_TILE
    final = final_g is not None

    def mod_imap(i, *_):
        return ((layer * MOD_ROWS + geom.group(i)) * N_MOD + 5, 0, 0)

    row = lambda i, *_: (i, 0)
    in_specs = [
        pl.BlockSpec((tm, d), row),
        pl.BlockSpec((tm, TOP_K), row),
        pl.BlockSpec((tm, TOP_K), row),
        pl.BlockSpec((None, 1, d), mod_imap),
    ]
    args = [seg_rows, seg_off, seg_dst, x, pos, gates, mods]
    if final:
        in_specs.append(pl.BlockSpec((1, d), lambda i, *_: (0, 0)))
        args.append(final_g)
    in_specs.append(pl.BlockSpec(memory_space=pl.ANY))
    args.append(ys)
    if final:
        n_pt = geom.n_ptiles
        out_shape = [jax.ShapeDtypeStruct((geom.tp, d), F32), jax.ShapeDtypeStruct((t - geom.tp, d), F32)]
        out_specs = [pl.BlockSpec((tm, d), lambda i, *_: (jnp.minimum(i, n_pt - 1), 0)),
                     pl.BlockSpec((tm, d), lambda i, *_: (jnp.maximum(i - n_pt, 0), 0))]
    else:
        out_shape = jax.ShapeDtypeStruct((t, d), F32)
        out_specs = pl.BlockSpec((tm, d), row)
    return pl.pallas_call(
        functools.partial(_combine_kernel, final=final, n_ptiles=geom.n_ptiles),
        out_shape=out_shape,
        grid_spec=pltpu.PrefetchScalarGridSpec(
            num_scalar_prefetch=3,
            grid=(t // tm,),
            in_specs=in_specs,
            out_specs=out_specs,
            scratch_shapes=[pltpu.VMEM((SORTED_ROWS, d // 2), U32), pltpu.SemaphoreType.DMA],
        ),
        compiler_params=_cparams("arbitrary"),
        name="moe_combine",
    )(*args)


def _moe(geom, layer, h, pos, gates, seg_rows, seg_off, x, mods, w_gu, b_gu, w_down, b_down, final_g):
    t = h.shape[0]
    tb = EXPERT_TILE
    n_tok_tiles = seg_rows.shape[0]
    max_rows = t * TOP_K + n_tok_tiles * N_EXPERTS * (SEG_ALIGN - 1)
    n_blocks = -(-max_rows // tb) + N_EXPERTS
    rows = seg_rows[:, 0, :]
    cnt = jnp.sum(rows, axis=0)
    n_tiles_e = (cnt + tb - 1) // tb
    tile_end = jnp.cumsum(n_tiles_e)
    tile_start = tile_end - n_tiles_e
    expert_start = (tile_start * tb).astype(I32)
    seg_dst = expert_start[None, :] + jnp.cumsum(rows, axis=0) - rows
    rows_flat = rows.reshape(-1).astype(I32)
    off_flat = seg_off.reshape(-1).astype(I32)
    dst_flat = seg_dst.reshape(-1).astype(I32)
    xs = _dispatch(h, pos, rows_flat, off_flat, dst_flat, n_blocks * tb + FFN_TILE - tb)
    ys = _expert_ffn(layer, xs, cnt.astype(I32), expert_start, w_gu, b_gu, w_down, b_down)
    return _combine(geom, layer, ys, pos, rows_flat, off_flat, dst_flat, x, gates, mods, final_g)


def kernel(x_prompt, x_sample, cache_b_k, cache_b_v, cache_c_k, cache_c_v, c, c_ctx,
           mod_w, mod_b, norm_mix, norm_ffn, even_w_in, even_w_out, even_sink,
           odd_w_in, odd_w_out, odd_q_norm, odd_k_norm, router_w, router_b,
           moe_w_gu, moe_b_gu, moe_w_down, moe_b_down, final_norm):
    bp, lp, d = x_prompt.shape
    bs, ls, _ = x_sample.shape
    past = cache_b_k.shape[2]
    depth = mod_w.shape[0]
    geom = _Geom(bp, lp, bs, ls)
    tp = geom.tp

    x = (x_prompt.reshape(tp, d), x_sample.reshape(bs * ls, d))
    cond = jnp.concatenate([c_ctx[None, :], c, jnp.zeros((MOD_ROWS - 1 - bs, d), F32)], axis=0)
    mods = _modulation(cond, mod_w, mod_b).reshape(depth * MOD_ROWS * N_MOD, 1, d)

    cn, sn = _dft_tables(A_GROUP_DIM)
    dft_chan = jnp.asarray(np.concatenate([cn, sn], axis=1), BF16)
    dft_p = [jnp.asarray(m, BF16) for m in _dft_tables(lp)]
    dft_s = [jnp.asarray(m, BF16) for m in _dft_tables(ls)]
    rope_b = [jnp.asarray(m) for m in _rope_tables(TOKEN_TILE, ls, B_HEAD_DIM)]
    rope_c = [jnp.asarray(m) for m in _rope_tables(TOKEN_TILE, ls, C_HEAD_DIM)]

    states = {"bk": [], "bv": [], "ck": [], "cv": []}
    for layer in range(depth):
        j = layer // 2
        g_mix = norm_mix[layer][None, :]
        g_ffn = norm_ffn[layer][None, :]
        if layer % 2 == 0:
            tc, ts, q, k, v, k_state, v_state = _in_projection(
                geom, layer, x, g_mix, mods, even_w_in[j].astype(BF16), rope_b[0], rope_b[1], [dft_chan],
                _proj_even_kernel, (A_WIDTH, A_WIDTH, B_Q_WIDTH, B_KV_WIDTH, B_KV_WIDTH), B_KV_WIDTH, "proj_even")
            states["bk"].append(k_state.reshape(bp, lp, B_KV_HEADS, B_HEAD_DIM))
            states["bv"].append(v_state.reshape(bp, lp, B_KV_HEADS, B_HEAD_DIM))
            four = _fourier_tokens(tc, ts, dft_p[0], dft_p[1], bp, lp, 0, None)
            four = _fourier_tokens(tc, ts, dft_s[0], dft_s[1], bs, ls, tp, four)
            sink = even_sink[j]
            common = dict(kv_heads=B_KV_HEADS, groups=B_HEADS // B_KV_HEADS, dh=B_HEAD_DIM)
            att = _attention(q, k, v, None, sink, None, n_seq=bp, seq_len=lp, row0=0, q_tile=lp,
                             window=None, **common)
            ctx = (cache_b_k[:, j].reshape(bs, past, B_KV_WIDTH).astype(BF16),
                   cache_b_v[:, j].reshape(bs, past, B_KV_WIDTH).astype(BF16))
            att = _attention(q, k, v, ctx, sink, att, n_seq=bs, seq_len=ls, row0=tp, q_tile=ATTN_Q_TILE,
                             window=WINDOW, **common)
            w_out = even_w_out[j].astype(BF16)
            mixes = [four, att]
            w_parts = [w_out[:A_WIDTH], w_out[A_WIDTH:]]
        else:
            q, k, v, k_state, v_state = _in_projection(
                geom, layer, x, g_mix, mods, odd_w_in[j].astype(BF16), rope_c[0], rope_c[1],
                [odd_q_norm[j][None, :], odd_k_norm[j][None, :]],
                _proj_odd_kernel, (C_Q_WIDTH, C_KV_WIDTH, C_KV_WIDTH), C_KV_WIDTH, "proj_odd")
            states["ck"].append(k_state.reshape(bp, lp, C_KV_HEADS, C_HEAD_DIM))
            states["cv"].append(v_state.reshape(bp, lp, C_KV_HEADS, C_HEAD_DIM))
            common = dict(kv_heads=C_KV_HEADS, groups=C_HEADS // C_KV_HEADS, dh=C_HEAD_DIM, window=None)
            att = _attention(q, k, v, None, None, None, n_seq=bp, seq_len=lp, row0=0, q_tile=lp, **common)
            ctx = (cache_c_k[:, j].reshape(bs, past, C_KV_WIDTH).astype(BF16),
                   cache_c_v[:, j].reshape(bs, past, C_KV_WIDTH).astype(BF16))
            att = _attention(q, k, v, ctx, None, att, n_seq=bs, seq_len=ls, row0=tp, q_tile=ATTN_Q_TILE, **common)
            mixes = [att]
            w_parts = [odd_w_out[j].astype(BF16)]
        x, h, pos, gates, seg_rows, seg_off = _out_route(
            geom, layer, mixes, w_parts, x, mods, g_ffn, router_w[layer].astype(BF16), router_b[layer][None, :])
        final_g = final_norm[None, :] if layer == depth - 1 else None
        x = _moe(geom, layer, h, pos, gates, seg_rows, seg_off, x, mods,
                 moe_w_gu, moe_b_gu, moe_w_down, moe_b_down, final_g)

    y_prompt = x[0].reshape(bp, lp, d)
    y_sample = x[1].reshape(bs, ls, d)
    return (y_prompt, y_sample,
            jnp.stack(states["bk"], axis=1), jnp.stack(states["bv"], axis=1),
            jnp.stack(states["ck"], axis=1), jnp.stack(states["cv"], axis=1))
```

```python
import functools

import numpy as np
import jax
import jax.numpy as jnp
from jax import lax
from jax.experimental import pallas as pl
from jax.experimental.pallas import tpu as pltpu

F32 = jnp.float32
BF16 = jnp.bfloat16
I32 = jnp.int32
U32 = jnp.uint32

GRID_W = 64
A_GROUPS = 4
A_GROUP_DIM = 128
A_WIDTH = A_GROUPS * A_GROUP_DIM
B_HEADS = 8
B_KV_HEADS = 2
B_HEAD_DIM = 64
B_Q_WIDTH = B_HEADS * B_HEAD_DIM
B_KV_WIDTH = B_KV_HEADS * B_HEAD_DIM
WINDOW = 128
C_HEADS = 8
C_KV_HEADS = 2
C_HEAD_DIM = 128
C_Q_WIDTH = C_HEADS * C_HEAD_DIM
C_KV_WIDTH = C_KV_HEADS * C_HEAD_DIM
ROPE_THETA = 10000.0
N_EXPERTS = 32
TOP_K = 4
SWIGLU_LIMIT = 7.0
SWIGLU_ALPHA = 1.702
EPS = 1e-6

LANES = 128
TOKEN_TILE = 256
EXPERT_TILE = 256
FFN_TILE = 256
ATTN_Q_TILE = 128
ATTN_KEY_CHUNK = 512
SEG_ALIGN = 8
SEG_CHUNK = 32
WAIT_CHUNK = 256
SORTED_ROWS = -(-(TOKEN_TILE * TOP_K + N_EXPERTS * (SEG_ALIGN - 1)) // LANES) * LANES
VMEM_LIMIT = 56 * 1024 * 1024
MASKED = -1e30
N_MOD = 6
MOD_ROWS = 8


def _cparams(*sem):
    return pltpu.CompilerParams(dimension_semantics=tuple(sem), vmem_limit_bytes=VMEM_LIMIT)


def _dot(a, b):
    return jnp.dot(a, b, preferred_element_type=F32)


def _dot_nt(a, b):
    return lax.dot_general(a, b, (((1,), (1,)), ((), ())), preferred_element_type=F32)


def _rms(x, g):
    return x * lax.rsqrt(jnp.mean(x * x, axis=-1, keepdims=True) + EPS) * g


def _dft_tables(n):
    j = np.arange(n, dtype=np.int64)
    ang = 2.0 * np.pi * ((j[:, None] * j[None, :]) % n).astype(np.float64) / n
    s = 1.0 / np.sqrt(n)
    return np.cos(ang) * s, np.sin(ang) * s


def _rope_tables(n_prompt_rows, n_latent, head_dim):
    quarter = head_dim // 4
    pos = np.arange(n_latent)
    row = (pos // GRID_W).astype(np.float32)
    col = (pos % GRID_W).astype(np.float32)
    inv = (np.float32(ROPE_THETA) ** (-np.arange(quarter, dtype=np.float32) / np.float32(quarter))).astype(np.float32)
    ang_row = (row[:, None] * inv[None, :]).astype(np.float32)
    ang_col = (col[:, None] * inv[None, :]).astype(np.float32)
    cos_h = np.concatenate([np.cos(ang_row)] * 2 + [np.cos(ang_col)] * 2, axis=1)
    sin_h = np.concatenate([-np.sin(ang_row), np.sin(ang_row), -np.sin(ang_col), np.sin(ang_col)], axis=1)
    reps = LANES // head_dim
    cos_l = np.tile(cos_h, (1, reps)).astype(np.float32)
    sin_l = np.tile(sin_h, (1, reps)).astype(np.float32)
    cos = np.concatenate([np.ones((n_prompt_rows, LANES), np.float32), cos_l], axis=0)
    sin = np.concatenate([np.zeros((n_prompt_rows, LANES), np.float32), sin_l], axis=0)
    return cos, sin


def _rope(x, cos, sin, quarter):
    lane = lax.broadcasted_iota(I32, (x.shape[0], LANES), 1)
    first = ((lane // quarter) % 2) == 0
    outs = []
    for c in range(x.shape[1] // LANES):
        xc = x[:, c * LANES:(c + 1) * LANES]
        partner = jnp.where(first, pltpu.roll(xc, LANES - quarter, 1), pltpu.roll(xc, quarter, 1))
        outs.append(xc * cos + partner * sin)
    return outs[0] if len(outs) == 1 else jnp.concatenate(outs, axis=1)


def _head_rms(x, g):
    outs = []
    for c in range(x.shape[1] // LANES):
        outs.append(_rms(x[:, c * LANES:(c + 1) * LANES], g))
    return outs[0] if len(outs) == 1 else jnp.concatenate(outs, axis=1)


def _mod_kernel(c_ref, w_ref, b_ref, o_ref):
    c = c_ref[...]
    s = c * (1.0 / (1.0 + jnp.exp(-c)))
    o_ref[0] = _dot(s.astype(BF16), w_ref[0].astype(BF16)) + b_ref[0]


def _modulation(cond, mod_w, mod_b):
    depth, d, n = mod_w.shape
    tn = 1536
    return pl.pallas_call(
        _mod_kernel,
        out_shape=jax.ShapeDtypeStruct((depth, MOD_ROWS, n), F32),
        grid=(depth, n // tn),
        in_specs=[
            pl.BlockSpec((MOD_ROWS, d), lambda l, j: (0, 0)),
            pl.BlockSpec((1, d, tn), lambda l, j: (l, 0, j)),
            pl.BlockSpec((1, 1, tn), lambda l, j: (l, 0, j)),
        ],
        out_specs=pl.BlockSpec((1, MOD_ROWS, tn), lambda l, j: (l, 0, j)),
        compiler_params=_cparams("arbitrary", "arbitrary"),
        name="modulation",
    )(cond, mod_w, mod_b.reshape(depth, 1, n))


class _Geom:
    def __init__(self, bp, lp, bs, ls):
        self.bp, self.lp, self.bs, self.ls = bp, lp, bs, ls
        self.tp = bp * lp
        self.t = bp * lp + bs * ls
        assert lp == TOKEN_TILE and ls % TOKEN_TILE == 0 and self.tp % ls == 0
        self.n_ptiles = self.tp // TOKEN_TILE
        self.tiles_per_lat = ls // TOKEN_TILE
        self.n_tiles = self.t // TOKEN_TILE

    def group(self, i):
        return jnp.where(i < self.n_ptiles, 0, 1 + (i - self.n_ptiles) // self.tiles_per_lat)

    def pos_block(self, i):
        return jnp.where(i < self.n_ptiles, 0, 1 + (i - self.n_ptiles) % self.tiles_per_lat)


def _mod_spec(geom, layer, which, d):
    def imap(i):
        return ((layer * MOD_ROWS + geom.group(i)) * N_MOD + which, 0, 0)
    return pl.BlockSpec((None, 1, d), imap)


def _stream_specs(geom, x, d):
    tm = TOKEN_TILE
    if isinstance(x, tuple):
        return ([pl.BlockSpec((tm, d), lambda i, *_: (jnp.minimum(i, geom.n_ptiles - 1), 0)),
                 pl.BlockSpec((tm, d), lambda i, *_: (jnp.maximum(i - geom.n_ptiles, 0), 0))], list(x))
    return [pl.BlockSpec((tm, d), lambda i, *_: (i, 0))], [x]


def _stream_tile(x_refs, n_ptiles):
    if len(x_refs) == 1:
        return x_refs[0][...]
    return jnp.where(pl.program_id(0) < n_ptiles, x_refs[0][...], x_refs[1][...])


def _store_kv(k, v, kb_ref, vb_ref, ks_ref, vs_ref, n_ptiles):
    kb_ref[...] = k.astype(BF16)
    vb_ref[...] = v.astype(BF16)

    @pl.when(pl.program_id(0) < n_ptiles)
    def _():
        ks_ref[...] = k
        vs_ref[...] = v


def _proj_even_kernel(*refs, n_x, n_ptiles):
    x_refs = refs[:n_x]
    (g_ref, sh_ref, sc_ref, w_ref, cos_ref, sin_ref, dft_ref,
     tc_ref, ts_ref, q_ref, kb_ref, vb_ref, ks_ref, vs_ref) = refs[n_x:]
    h = _rms(_stream_tile(x_refs, n_ptiles), g_ref[...]) * (1.0 + sc_ref[...]) + sh_ref[...]
    p = _dot(h.astype(BF16), w_ref[...])
    cos = cos_ref[...]
    sin = sin_ref[...]
    dft = dft_ref[...]
    tcs, tss = [], []
    for g in range(A_GROUPS):
        t = _dot(p[:, g * A_GROUP_DIM:(g + 1) * A_GROUP_DIM].astype(BF16), dft)
        tcs.append(t[:, :A_GROUP_DIM])
        tss.append(t[:, A_GROUP_DIM:])
    tc_ref[...] = jnp.concatenate(tcs, axis=1).astype(BF16)
    ts_ref[...] = jnp.concatenate(tss, axis=1).astype(BF16)
    o = A_WIDTH
    q = _rope(p[:, o:o + B_Q_WIDTH], cos, sin, B_HEAD_DIM // 4)
    q_ref[...] = (q * B_HEAD_DIM ** -0.5).astype(BF16)
    o += B_Q_WIDTH
    k = _rope(p[:, o:o + B_KV_WIDTH], cos, sin, B_HEAD_DIM // 4)
    o += B_KV_WIDTH
    _store_kv(k, p[:, o:o + B_KV_WIDTH], kb_ref, vb_ref, ks_ref, vs_ref, n_ptiles)


def _proj_odd_kernel(*refs, n_x, n_ptiles):
    x_refs = refs[:n_x]
    (g_ref, sh_ref, sc_ref, w_ref, cos_ref, sin_ref, qn_ref, kn_ref,
     q_ref, kb_ref, vb_ref, ks_ref, vs_ref) = refs[n_x:]
    h = _rms(_stream_tile(x_refs, n_ptiles), g_ref[...]) * (1.0 + sc_ref[...]) + sh_ref[...]
    p = _dot(h.astype(BF16), w_ref[...])
    cos = cos_ref[...]
    sin = sin_ref[...]
    q = _head_rms(p[:, :C_Q_WIDTH], qn_ref[...])
    k = _head_rms(p[:, C_Q_WIDTH:C_Q_WIDTH + C_KV_WIDTH], kn_ref[...])
    q_ref[...] = (_rope(q, cos, sin, C_HEAD_DIM // 4) * C_HEAD_DIM ** -0.5).astype(BF16)
    k = _rope(k, cos, sin, C_HEAD_DIM // 4)
    _store_kv(k, p[:, C_Q_WIDTH + C_KV_WIDTH:], kb_ref, vb_ref, ks_ref, vs_ref, n_ptiles)


def _in_projection(geom, layer, x, norm_g, mods, w, cos, sin, extras, kernel, out_widths, kv_width, name):
    t, d = geom.t, w.shape[0]
    tm = TOKEN_TILE
    n_out = w.shape[1]
    row = lambda i: (i, 0)
    const2 = lambda i: (0, 0)
    x_specs, x_args = _stream_specs(geom, x, d)
    in_specs = x_specs + [
        pl.BlockSpec((1, d), const2),
        _mod_spec(geom, layer, 0, d),
        _mod_spec(geom, layer, 1, d),
        pl.BlockSpec((d, n_out), const2),
        pl.BlockSpec((tm, LANES), lambda i: (geom.pos_block(i), 0)),
        pl.BlockSpec((tm, LANES), lambda i: (geom.pos_block(i), 0)),
    ] + [pl.BlockSpec(e.shape, const2) for e in extras]
    return pl.pallas_call(
        functools.partial(kernel, n_x=len(x_args), n_ptiles=geom.n_ptiles),
        out_shape=([jax.ShapeDtypeStruct((t, wd), BF16) for wd in out_widths]
                   + [jax.ShapeDtypeStruct((geom.tp, kv_width), F32)] * 2),
        grid=(geom.n_tiles,),
        in_specs=in_specs,
        out_specs=([pl.BlockSpec((tm, wd), row) for wd in out_widths]
                   + [pl.BlockSpec((tm, kv_width), lambda i: (jnp.minimum(i, geom.n_ptiles - 1), 0))] * 2),
        compiler_params=_cparams("arbitrary"),
        name=name,
    )(*x_args, norm_g, mods, mods, w, cos, sin, *extras)


def _fourier_kernel(cl_ref, sl_ref, tc_ref, ts_ref, *rest):
    o_ref = rest[-1]
    o_ref[...] = (_dot(cl_ref[...], tc_ref[...]) - _dot(sl_ref[...], ts_ref[...])).astype(o_ref.dtype)


def _fourier_tokens(tc, ts, cl, sl, n_seq, seq_len, row0, prev):
    t, width = tc.shape
    tr = min(seq_len, 512)
    n_r = seq_len // tr
    assert row0 % seq_len == 0
    seq0 = row0 // seq_len
    out0 = row0 // tr
    in_specs = [
        pl.BlockSpec((tr, seq_len), lambda s, r: (r, 0)),
        pl.BlockSpec((tr, seq_len), lambda s, r: (r, 0)),
        pl.BlockSpec((seq_len, width), lambda s, r: (seq0 + s, 0)),
        pl.BlockSpec((seq_len, width), lambda s, r: (seq0 + s, 0)),
    ]
    args = [cl, sl, tc, ts]
    aliases = {}
    if prev is not None:
        in_specs.append(pl.BlockSpec(memory_space=pl.ANY))
        args.append(prev)
        aliases = {4: 0}
    return pl.pallas_call(
        _fourier_kernel,
        out_shape=jax.ShapeDtypeStruct((t, width), BF16),
        grid=(n_seq, n_r),
        in_specs=in_specs,
        out_specs=pl.BlockSpec((tr, width), lambda s, r: (out0 + s * n_r + r, 0)),
        input_output_aliases=aliases,
        compiler_params=_cparams("arbitrary", "arbitrary"),
        name="fourier_tokens",
    )(*args)


def _attend(q, chunks, sink, o0, dh, den_col):
    m = sink
    acc = None
    den = None
    for k, v, mask in chunks:
        s = _dot_nt(q, k)
        if mask is not None:
            s = jnp.where(mask, s, MASKED)
        m_new = jnp.max(s, axis=-1, keepdims=True)
        if m is not None:
            m_new = jnp.maximum(m, m_new)
        p = jnp.exp(s - m_new)
        pv = _dot(p.astype(BF16), v)
        if acc is None:
            acc = pv
            if den_col is None:
                den = jnp.sum(p, axis=-1, keepdims=True)
        else:
            alpha = jnp.exp(m - m_new)
            acc = alpha * acc + pv
            if den_col is None:
                den = alpha * den + jnp.sum(p, axis=-1, keepdims=True)
        m = m_new
    if den_col is not None:
        den = acc[:, den_col:den_col + 1]
    if sink is not None:
        den = den + jnp.exp(sink - m)
    return acc[:, o0:o0 + dh] / den


def _attend_two_pass(q, chunks, sink):
    scores = []
    m = sink
    for k, _, mask in chunks:
        s = _dot_nt(q, k)
        if mask is not None:
            s = jnp.where(mask, s, MASKED)
        scores.append(s)
        mx = jnp.max(s, axis=-1, keepdims=True)
        m = mx if m is None else jnp.maximum(m, mx)
    den = None if sink is None else jnp.exp(sink - m)
    acc = None
    for (_, v, _), s in zip(chunks, scores):
        e = jnp.exp(s - m)
        es = jnp.sum(e, axis=-1, keepdims=True)
        den = es if den is None else den + es
        o = _dot(e.astype(BF16), v)
        acc = o if acc is None else acc + o
    return acc / den


def _stack_heads(q, kv, groups, dh):
    return jnp.concatenate([q[:, (kv * groups + g) * dh:(kv * groups + g + 1) * dh] for g in range(groups)], axis=0)


def _sink_column(sink_ref, kv, groups, rows):
    return jnp.concatenate([jnp.full((rows, 1), sink_ref[kv * groups + g], F32) for g in range(groups)], axis=0)


def _head_values(v, kv, dh, with_ones):
    if not with_ones:
        return v[:, kv * dh:(kv + 1) * dh], 0, None
    assert dh == LANES
    lane = lax.broadcasted_iota(I32, (v.shape[0], LANES), 1)
    ones = jnp.where(lane == 0, 1.0, 0.0).astype(BF16)
    return jnp.concatenate([v[:, kv * dh:(kv + 1) * dh], ones], axis=1), 0, dh


def _attn_kernel(*refs, kv_heads, groups, dh, has_sink, has_ctx, window, q_tile, seq_len, chunk):
    refs = list(refs)
    sink_ref = refs.pop(0) if has_sink else None
    q_ref, k_ref, v_ref = refs[:3]
    ck_ref, cv_ref = (refs[3], refs[4]) if has_ctx else (None, None)
    o_ref = refs[-1]
    q = q_ref[...]
    rows = q.shape[0]
    if window is None:
        spans = [(c * chunk, chunk) for c in range(seq_len // chunk)]
        mask = None
    else:
        n = pl.program_id(1)
        band = q_tile + 2 * window
        start = pl.multiple_of(jnp.clip(n * q_tile - window, 0, seq_len - band), LANES)
        spans = [(start, band)]
        qpos = n * q_tile + lax.broadcasted_iota(I32, (groups * rows, band), 0) % rows
        kpos = start + lax.broadcasted_iota(I32, (groups * rows, band), 1)
        mask = jnp.abs(kpos - qpos) <= window
    sources = [(k_ref[pl.ds(s0, n_s), :], v_ref[pl.ds(s0, n_s), :], mask) for s0, n_s in spans]
    if has_ctx:
        sources.append((ck_ref[...], cv_ref[...], None))
    outs = []
    for kv in range(kv_heads):
        online = dh == LANES and len(sources) > 1
        chunks = []
        for k, v, msk in sources:
            vh, o0, den_col = _head_values(v, kv, dh, with_ones=online)
            chunks.append((k[:, kv * dh:(kv + 1) * dh], vh, msk))
        sink = _sink_column(sink_ref, kv, groups, rows) if has_sink else None
        qh = _stack_heads(q, kv, groups, dh)
        o = _attend(qh, chunks, sink, o0, dh, den_col) if online else _attend_two_pass(qh, chunks, sink)
        outs.extend(o[g * rows:(g + 1) * rows] for g in range(groups))
    o_ref[...] = jnp.concatenate(outs, axis=1).astype(o_ref.dtype)


def _attention(q, k, v, ctx, sink, prev, *, n_seq, seq_len, row0, q_tile, kv_heads, groups, dh, window):
    t, qw = q.shape
    kw = k.shape[1]
    n_q = seq_len // q_tile
    assert row0 % seq_len == 0 and row0 % q_tile == 0
    seq0 = row0 // seq_len
    q0 = row0 // q_tile
    in_specs, args = [], []
    if sink is not None:
        in_specs.append(pl.BlockSpec(memory_space=pltpu.SMEM))
        args.append(sink)
    in_specs += [
        pl.BlockSpec((q_tile, qw), lambda s, n: (q0 + s * n_q + n, 0)),
        pl.BlockSpec((seq_len, kw), lambda s, n: (seq0 + s, 0)),
        pl.BlockSpec((seq_len, kw), lambda s, n: (seq0 + s, 0)),
    ]
    args += [q, k, v]
    if ctx is not None:
        p = ctx[0].shape[1]
        in_specs += [pl.BlockSpec((None, p, kw), lambda s, n: (s, 0, 0))] * 2
        args += list(ctx)
    aliases = {}
    if prev is not None:
        in_specs.append(pl.BlockSpec(memory_space=pl.ANY))
        aliases = {len(args): 0}
        args.append(prev)
    kern = functools.partial(
        _attn_kernel, kv_heads=kv_heads, groups=groups, dh=dh, has_sink=sink is not None,
        has_ctx=ctx is not None, window=window, q_tile=q_tile, seq_len=seq_len, chunk=min(seq_len, ATTN_KEY_CHUNK))
    return pl.pallas_call(
        kern,
        out_shape=jax.ShapeDtypeStruct((t, qw), BF16),
        grid=(n_seq, n_q),
        in_specs=in_specs,
        out_specs=pl.BlockSpec((q_tile, qw), lambda s, n: (q0 + s * n_q + n, 0)),
        input_output_aliases=aliases,
        compiler_params=_cparams("arbitrary", "arbitrary"),
        name="attention",
    )(*args)


def _out_route_kernel(*refs, n_mix, n_x, n_ptiles):
    mix_refs = refs[:n_mix]
    w_refs = refs[n_mix:2 * n_mix]
    x_refs = refs[2 * n_mix:2 * n_mix + n_x]
    (gate_ref, g2_ref, sh2_ref, sc2_ref, rw_ref, rb_ref,
     xo_ref, h_ref, pos_ref, gates_ref, rows_ref, off_ref) = refs[2 * n_mix + n_x:]

    acc = None
    for m_ref, w_ref in zip(mix_refs, w_refs):
        part = _dot(m_ref[...], w_ref[...])
        acc = part if acc is None else acc + part
    xn = _stream_tile(x_refs, n_ptiles) + gate_ref[...] * acc
    xo_ref[...] = xn
    h = _rms(xn, g2_ref[...]) * (1.0 + sc2_ref[...]) + sh2_ref[...]
    hb = h.astype(BF16)
    h_ref[...] = hb

    logits = _dot(hb, rw_ref[...]) + rb_ref[...]
    tm, ne = logits.shape
    lane = lax.broadcasted_iota(I32, (tm, ne), 1).astype(F32)
    lane4 = lax.broadcasted_iota(I32, (tm, TOP_K), 1)
    work = logits
    sels, vals = [], []
    for _ in range(TOP_K):
        mx = jnp.max(work, axis=-1, keepdims=True)
        first = jnp.min(jnp.where(work == mx, lane, float(ne)), axis=-1, keepdims=True)
        sel = lane == first
        work = jnp.where(sel, -jnp.inf, work)
        sels.append(sel)
        vals.append(mx)
    exps = [jnp.exp(v - vals[0]) for v in vals]
    den = exps[0] + exps[1] + exps[2] + exps[3]

    onehot = jnp.zeros((tm, ne), F32)
    for sel in sels:
        onehot = onehot + sel.astype(F32)
    r_i = lax.broadcasted_iota(I32, (tm, tm), 0)
    c_i = lax.broadcasted_iota(I32, (tm, tm), 1)
    before = jnp.where(c_i < r_i, 1.0, 0.0).astype(BF16)
    earlier = _dot(before, onehot.astype(BF16))

    cnt = jnp.sum(onehot, axis=0, keepdims=True)
    seg8 = jnp.floor((cnt + (SEG_ALIGN - 1.0)) * (1.0 / SEG_ALIGN))
    e_r = lax.broadcasted_iota(I32, (ne, ne), 0)
    e_c = lax.broadcasted_iota(I32, (ne, ne), 1)
    upper = jnp.where(e_r < e_c, 1.0, 0.0).astype(BF16)
    off8 = _dot(jnp.broadcast_to(seg8, (SEG_ALIGN, ne)).astype(BF16), upper)[0:1]
    seg_off = off8 * SEG_ALIGN
    base = seg_off + earlier

    pos_o = jnp.zeros((tm, TOP_K), I32)
    gate_o = jnp.zeros((tm, TOP_K), F32)
    for k in range(TOP_K):
        pos_k = jnp.sum(jnp.where(sels[k], base, 0.0), axis=-1, keepdims=True)
        pos_o = jnp.where(lane4 == k, pos_k.astype(I32), pos_o)
        gate_o = jnp.where(lane4 == k, exps[k] / den, gate_o)
    pos_ref[...] = pos_o
    gates_ref[...] = gate_o
    rows_ref[0] = (seg8 * SEG_ALIGN).astype(I32)
    off_ref[0] = seg_off.astype(I32)


def _out_route(geom, layer, mixes, w_parts, x, mods, norm_g, router_w, router_b):
    t, d = geom.t, w_parts[0].shape[1]
    tm = TOKEN_TILE
    row = lambda i: (i, 0)
    const2 = lambda i: (0, 0)
    x_specs, x_args = _stream_specs(geom, x, d)
    in_specs = [pl.BlockSpec((tm, m.shape[1]), row) for m in mixes]
    in_specs += [pl.BlockSpec(w.shape, const2) for w in w_parts]
    in_specs += x_specs
    in_specs += [
        _mod_spec(geom, layer, 2, d),
        pl.BlockSpec((1, d), const2),
        _mod_spec(geom, layer, 3, d),
        _mod_spec(geom, layer, 4, d),
        pl.BlockSpec(router_w.shape, const2),
        pl.BlockSpec((1, N_EXPERTS), const2),
    ]
    seg3 = lambda i: (i, 0, 0)
    out_shape = [
        jax.ShapeDtypeStruct((t, d), F32),
        jax.ShapeDtypeStruct((t, d), BF16),
        jax.ShapeDtypeStruct((t, TOP_K), I32),
        jax.ShapeDtypeStruct((t, TOP_K), F32),
        jax.ShapeDtypeStruct((geom.n_tiles, 1, N_EXPERTS), I32),
        jax.ShapeDtypeStruct((geom.n_tiles, 1, N_EXPERTS), I32),
    ]
    out_specs = [
        pl.BlockSpec((tm, d), row),
        pl.BlockSpec((tm, d), row),
        pl.BlockSpec((tm, TOP_K), row),
        pl.BlockSpec((tm, TOP_K), row),
        pl.BlockSpec((1, 1, N_EXPERTS), seg3),
        pl.BlockSpec((1, 1, N_EXPERTS), seg3),
    ]
    return pl.pallas_call(
        functools.partial(_out_route_kernel, n_mix=len(mixes), n_x=len(x_args), n_ptiles=geom.n_ptiles),
        out_shape=out_shape,
        grid=(geom.n_tiles,),
        in_specs=in_specs,
        out_specs=out_specs,
        compiler_params=_cparams("arbitrary"),
        name="out_route",
    )(*mixes, *w_parts, *x_args, mods, norm_g, mods, mods, router_w, router_b)


def _pack_pairs(v):
    n = v.shape[1] // 2
    bits = lax.bitcast_convert_type(v, U32)
    return (bits[:, :n] & jnp.uint32(0xFFFF0000)) | (bits[:, n:] >> 16)


def _unpack_pairs(p):
    hi = lax.bitcast_convert_type(p & jnp.uint32(0xFFFF0000), F32)
    lo = lax.bitcast_convert_type(p << 16, F32)
    return jnp.concatenate([hi, lo], axis=1).astype(BF16)


def _segment_copies(i, rows_ref, off_ref, dst_ref, make_copy):
    def per_expert(e, total):
        a = i * N_EXPERTS + e
        n_rows = rows_ref[a]
        local0 = off_ref[a]
        slot0 = dst_ref[a]

        def big(c, carry):
            r = c * SEG_CHUNK
            make_copy(pl.multiple_of(local0 + r, SEG_ALIGN), pl.multiple_of(slot0 + r, SEG_ALIGN), SEG_CHUNK).start()
            return carry

        lax.fori_loop(0, n_rows // SEG_CHUNK, big, 0)
        size = SEG_CHUNK // 2
        while size >= SEG_ALIGN:
            done = n_rows - n_rows % (2 * size)

            @pl.when(n_rows % (2 * size) >= size)
            def _(done=done, size=size):
                make_copy(pl.multiple_of(local0 + done, SEG_ALIGN), pl.multiple_of(slot0 + done, SEG_ALIGN),
                          size).start()

            size //= 2
        return total + n_rows

    return lax.fori_loop(0, N_EXPERTS, per_expert, 0)


def _wait_copies(n_rows, make_copy):
    def wait_big(c, carry):
        make_copy(0, 0, WAIT_CHUNK).wait()
        return carry
    lax.fori_loop(0, n_rows // WAIT_CHUNK, wait_big, 0)
    size = WAIT_CHUNK // 2
    while size >= SEG_ALIGN:
        @pl.when(n_rows % (2 * size) >= size)
        def _(size=size):
            make_copy(0, 0, size).wait()

        size //= 2


def _dispatch_kernel(rows_ref, off_ref, dst_ref, tot_ref, h_ref, pos_ref, xs_ref, sorted_ref, sem):
    i = pl.program_id(0)
    buf = i % 2
    tm = h_ref.shape[0]
    n_sorted = sorted_ref.shape[1]
    pos = pos_ref[...]
    lane = lax.broadcasted_iota(I32, (tm, n_sorted), 1)
    hit = jnp.zeros((tm, n_sorted), F32)
    for k in range(TOP_K):
        hit = jnp.where(pos[:, k:k + 1] == lane, 1.0, hit)
    sorted_ref[buf] = _pack_pairs(lax.dot_general(hit.astype(BF16), h_ref[...], (((0,), (0,)), ((), ())),
                                                  preferred_element_type=F32))

    def copies_from(b):
        def make_copy(local, slot, rows):
            return pltpu.make_async_copy(sorted_ref.at[b, pl.ds(local, rows)], xs_ref.at[pl.ds(slot, rows)],
                                         sem.at[b])
        return make_copy

    _segment_copies(i, rows_ref, off_ref, dst_ref, copies_from(buf))

    @pl.when(i > 0)
    def _():
        _wait_copies(tot_ref[jnp.maximum(i - 1, 0)], copies_from(1 - buf))

    @pl.when(i == pl.num_programs(0) - 1)
    def _():
        _wait_copies(tot_ref[i], copies_from(buf))


def _dispatch(h, pos, seg_rows, seg_off, seg_dst, tile_rows, n_slots):
    t, d = h.shape
    tm = TOKEN_TILE
    return pl.pallas_call(
        _dispatch_kernel,
        out_shape=jax.ShapeDtypeStruct((n_slots, d // 2), U32),
        grid_spec=pltpu.PrefetchScalarGridSpec(
            num_scalar_prefetch=4,
            grid=(t // tm,),
            in_specs=[
                pl.BlockSpec((tm, d), lambda i, *_: (i, 0)),
                pl.BlockSpec((tm, TOP_K), lambda i, *_: (i, 0)),
            ],
            out_specs=pl.BlockSpec(memory_space=pl.ANY),
            scratch_shapes=[pltpu.VMEM((2, SORTED_ROWS, d // 2), U32), pltpu.SemaphoreType.DMA((2,))],
        ),
        compiler_params=_cparams("arbitrary"),
        name="moe_dispatch",
    )(seg_rows, seg_off, seg_dst, tile_rows, h, pos)


def _ffn_kernel(rows_ref, start_ref, wgu_ref, bgu_ref, wd_ref, bd_ref, xs_ref, ys_ref,
                wgu_bf, wd_bf, xbuf, ybuf, sem_in, sem_out):
    e = pl.program_id(0)
    d_ff = wd_ref.shape[1]
    tb = xbuf.shape[1]
    n_rows = rows_ref[e]
    n_tiles = (n_rows + tb - 1) // tb
    base = start_ref[e]

    chunk = 128
    def cast_gu(c, carry):
        r = pl.multiple_of(c * chunk, chunk)
        wgu_bf[pl.ds(r, chunk), :] = wgu_ref[0, pl.ds(r, chunk), :].astype(BF16)
        return carry
    lax.fori_loop(0, wgu_ref.shape[1] // chunk, cast_gu, 0)
    def cast_d(c, carry):
        r = pl.multiple_of(c * chunk, chunk)
        wd_bf[pl.ds(r, chunk), :] = wd_ref[0, pl.ds(r, chunk), :].astype(BF16)
        return carry
    lax.fori_loop(0, d_ff // chunk, cast_d, 0)

    def x_copy(s, slot):
        r = pl.multiple_of(base + s * tb, EXPERT_TILE)
        return pltpu.make_async_copy(xs_ref.at[pl.ds(r, tb)], xbuf.at[slot], sem_in.at[slot])

    def y_copy(s, slot):
        r = pl.multiple_of(base + s * tb, EXPERT_TILE)
        return pltpu.make_async_copy(ybuf.at[slot], ys_ref.at[pl.ds(r, tb)], sem_out.at[slot])

    @pl.when(n_tiles > 0)
    def _():
        x_copy(0, 0).start()

    def tile(s, carry):
        slot = s % 2
        x_copy(s, slot).wait()

        @pl.when(s + 1 < n_tiles)
        def _():
            x_copy(s + 1, 1 - slot).start()

        @pl.when(s >= 2)
        def _():
            y_copy(s - 2, slot).wait()

        rows = s * tb + lax.broadcasted_iota(I32, (tb, 1), 0)
        x = jnp.where(rows < n_rows, _unpack_pairs(xbuf[slot]), 0.0).astype(BF16)
        gu = _dot(x, wgu_bf[...]) + bgu_ref[0]
        gate = jnp.minimum(gu[:, :d_ff], SWIGLU_LIMIT)
        up = jnp.clip(gu[:, d_ff:], -SWIGLU_LIMIT, SWIGLU_LIMIT)
        act = (up + 1.0) * (gate * (1.0 / (1.0 + jnp.exp(-SWIGLU_ALPHA * gate))))
        y = _dot(act.astype(BF16), wd_bf[...]) + bd_ref[0]
        ybuf[slot] = _pack_pairs(y.astype(BF16).astype(F32))
        y_copy(s, slot).start()
        return carry

    lax.fori_loop(0, n_tiles, tile, 0)

    @pl.when(n_tiles >= 2)
    def _():
        y_copy(n_tiles - 2, n_tiles % 2).wait()

    @pl.when(n_tiles >= 1)
    def _():
        y_copy(n_tiles - 1, (n_tiles - 1) % 2).wait()


def _expert_ffn(layer, xs, expert_rows, expert_start, w_gu, b_gu, w_down, b_down):
    n_slots, packed_w = xs.shape
    tb = FFN_TILE
    depth, ne, d, two_f = w_gu.shape
    d_ff = two_f // 2
    exp4 = lambda e, *_: (layer, e, 0, 0)
    return pl.pallas_call(
        _ffn_kernel,
        out_shape=jax.ShapeDtypeStruct((n_slots, packed_w), U32),
        grid_spec=pltpu.PrefetchScalarGridSpec(
            num_scalar_prefetch=2,
            grid=(ne,),
            in_specs=[
                pl.BlockSpec((None, 1, d, two_f), exp4),
                pl.BlockSpec((None, 1, 1, two_f), exp4),
                pl.BlockSpec((None, 1, d_ff, d), exp4),
                pl.BlockSpec((None, 1, 1, d), exp4),
                pl.BlockSpec(memory_space=pl.ANY),
            ],
            out_specs=pl.BlockSpec(memory_space=pl.ANY),
            scratch_shapes=[
                pltpu.VMEM((d, two_f), BF16), pltpu.VMEM((d_ff, d), BF16),
                pltpu.VMEM((2, tb, packed_w), U32), pltpu.VMEM((2, tb, packed_w), U32),
                pltpu.SemaphoreType.DMA((2,)), pltpu.SemaphoreType.DMA((2,)),
            ],
        ),
        compiler_params=_cparams("arbitrary"),
        name="expert_ffn",
    )(expert_rows, expert_start, w_gu, b_gu.reshape(depth, ne, 1, two_f),
      w_down, b_down.reshape(depth, ne, 1, d), xs)


def _combine_kernel(rows_ref, off_ref, dst_ref, tot_ref, x_ref, pos_ref, gates_ref, mg_ref, *rest,
                    final, n_ptiles):
    if final:
        fn_ref, ys_ref, op_ref, os_ref, buf, sem = rest
    else:
        ys_ref, o_ref, buf, sem = rest
    i = pl.program_id(0)
    cur = i % 2
    tm = x_ref.shape[0]
    n_sorted = buf.shape[1]

    def copies_into(b):
        def make_copy(local, slot, rows):
            return pltpu.make_async_copy(ys_ref.at[pl.ds(slot, rows)], buf.at[b, pl.ds(local, rows)], sem.at[b])
        return make_copy

    @pl.when(i == 0)
    def _():
        buf[...] = jnp.zeros_like(buf)
        _segment_copies(i, rows_ref, off_ref, dst_ref, copies_into(cur))

    @pl.when(i + 1 < pl.num_programs(0))
    def _():
        _segment_copies(i + 1, rows_ref, off_ref, dst_ref, copies_into(1 - cur))

    pos = pos_ref[...]
    g = gates_ref[...]
    lane = lax.broadcasted_iota(I32, (tm, n_sorted), 1)
    weight = jnp.zeros((tm, n_sorted), F32)
    for k in range(TOP_K):
        weight = jnp.where(pos[:, k:k + 1] == lane, g[:, k:k + 1], weight)
    _wait_copies(tot_ref[i], copies_into(cur))
    y = _dot(weight.astype(BF16), _unpack_pairs(buf[cur]))
    xn = x_ref[...] + mg_ref[...] * y
    if final:
        xn = _rms(xn, fn_ref[...])

        @pl.when(i < n_ptiles)
        def _():
            op_ref[...] = xn

        @pl.when(i >= n_ptiles)
        def _():
            os_ref[...] = xn
    else:
        o_ref[...] = xn


def _combine(geom, layer, ys, pos, seg_rows, seg_off, seg_dst, tile_rows, x, gates, mods, final_g):
    t, d = x.shape
    tm = TOKEN_TILE
    final = final_g is not None

    def mod_imap(i, *_):
        return ((layer * MOD_ROWS + geom.group(i)) * N_MOD + 5, 0, 0)

    row = lambda i, *_: (i, 0)
    in_specs = [
        pl.BlockSpec((tm, d), row),
        pl.BlockSpec((tm, TOP_K), row),
        pl.BlockSpec((tm, TOP_K), row),
        pl.BlockSpec((None, 1, d), mod_imap),
    ]
    args = [seg_rows, seg_off, seg_dst, tile_rows, x, pos, gates, mods]
    if final:
        in_specs.append(pl.BlockSpec((1, d), lambda i, *_: (0, 0)))
        args.append(final_g)
    in_specs.append(pl.BlockSpec(memory_space=pl.ANY))
    args.append(ys)
    if final:
        n_pt = geom.n_ptiles
        out_shape = [jax.ShapeDtypeStruct((geom.tp, d), F32), jax.ShapeDtypeStruct((t - geom.tp, d), F32)]
        out_specs = [pl.BlockSpec((tm, d), lambda i, *_: (jnp.minimum(i, n_pt - 1), 0)),
                     pl.BlockSpec((tm, d), lambda i, *_: (jnp.maximum(i - n_pt, 0), 0))]
    else:
        out_shape = jax.ShapeDtypeStruct((t, d), F32)
        out_specs = pl.BlockSpec((tm, d), row)
    return pl.pallas_call(
        functools.partial(_combine_kernel, final=final, n_ptiles=geom.n_ptiles),
        out_shape=out_shape,
        grid_spec=pltpu.PrefetchScalarGridSpec(
            num_scalar_prefetch=4,
            grid=(t // tm,),
            in_specs=in_specs,
            out_specs=out_specs,
            scratch_shapes=[pltpu.VMEM((2, SORTED_ROWS, d // 2), U32), pltpu.SemaphoreType.DMA((2,))],
        ),
        compiler_params=_cparams("arbitrary"),
        name="moe_combine",
    )(*args)


def _moe(geom, layer, h, pos, gates, seg_rows, seg_off, x, mods, w_gu, b_gu, w_down, b_down, final_g):
    t = h.shape[0]
    tb = EXPERT_TILE
    n_tok_tiles = seg_rows.shape[0]
    max_rows = t * TOP_K + n_tok_tiles * N_EXPERTS * (SEG_ALIGN - 1)
    n_blocks = -(-max_rows // tb) + N_EXPERTS
    rows = seg_rows[:, 0, :]
    cnt = jnp.sum(rows, axis=0)
    n_tiles_e = (cnt + tb - 1) // tb
    tile_end = jnp.cumsum(n_tiles_e)
    tile_start = tile_end - n_tiles_e
    expert_start = (tile_start * tb).astype(I32)
    seg_dst = expert_start[None, :] + jnp.cumsum(rows, axis=0) - rows
    rows_flat = rows.reshape(-1).astype(I32)
    off_flat = seg_off.reshape(-1).astype(I32)
    dst_flat = seg_dst.reshape(-1).astype(I32)
    tile_rows = jnp.sum(rows, axis=1).astype(I32)
    xs = _dispatch(h, pos, rows_flat, off_flat, dst_flat, tile_rows, n_blocks * tb + FFN_TILE - tb)
    ys = _expert_ffn(layer, xs, cnt.astype(I32), expert_start, w_gu, b_gu, w_down, b_down)
    return _combine(geom, layer, ys, pos, rows_flat, off_flat, dst_flat, tile_rows, x, gates, mods, final_g)


def kernel(x_prompt, x_sample, cache_b_k, cache_b_v, cache_c_k, cache_c_v, c, c_ctx,
           mod_w, mod_b, norm_mix, norm_ffn, even_w_in, even_w_out, even_sink,
           odd_w_in, odd_w_out, odd_q_norm, odd_k_norm, router_w, router_b,
           moe_w_gu, moe_b_gu, moe_w_down, moe_b_down, final_norm):
    bp, lp, d = x_prompt.shape
    bs, ls, _ = x_sample.shape
    past = cache_b_k.shape[2]
    depth = mod_w.shape[0]
    geom = _Geom(bp, lp, bs, ls)
    tp = geom.tp

    x = (x_prompt.reshape(tp, d), x_sample.reshape(bs * ls, d))
    cond = jnp.concatenate([c_ctx[None, :], c, jnp.zeros((MOD_ROWS - 1 - bs, d), F32)], axis=0)
    mods = _modulation(cond, mod_w, mod_b).reshape(depth * MOD_ROWS * N_MOD, 1, d)

    cn, sn = _dft_tables(A_GROUP_DIM)
    dft_chan = jnp.asarray(np.concatenate([cn, sn], axis=1), BF16)
    dft_p = [jnp.asarray(m, BF16) for m in _dft_tables(lp)]
    dft_s = [jnp.asarray(m, BF16) for m in _dft_tables(ls)]
    rope_b = [jnp.asarray(m) for m in _rope_tables(TOKEN_TILE, ls, B_HEAD_DIM)]
    rope_c = [jnp.asarray(m) for m in _rope_tables(TOKEN_TILE, ls, C_HEAD_DIM)]

    states = {"bk": [], "bv": [], "ck": [], "cv": []}
    for layer in range(depth):
        j = layer // 2
        g_mix = norm_mix[layer][None, :]
        g_ffn = norm_ffn[layer][None, :]
        if layer % 2 == 0:
            tc, ts, q, k, v, k_state, v_state = _in_projection(
                geom, layer, x, g_mix, mods, even_w_in[j].astype(BF16), rope_b[0], rope_b[1], [dft_chan],
                _proj_even_kernel, (A_WIDTH, A_WIDTH, B_Q_WIDTH, B_KV_WIDTH, B_KV_WIDTH), B_KV_WIDTH, "proj_even")
            states["bk"].append(k_state.reshape(bp, lp, B_KV_HEADS, B_HEAD_DIM))
            states["bv"].append(v_state.reshape(bp, lp, B_KV_HEADS, B_HEAD_DIM))
            four = _fourier_tokens(tc, ts, dft_p[0], dft_p[1], bp, lp, 0, None)
            four = _fourier_tokens(tc, ts, dft_s[0], dft_s[1], bs, ls, tp, four)
            sink = even_sink[j]
            common = dict(kv_heads=B_KV_HEADS, groups=B_HEADS // B_KV_HEADS, dh=B_HEAD_DIM)
            att = _attention(q, k, v, None, sink, None, n_seq=bp, seq_len=lp, row0=0, q_tile=lp,
                             window=None, **common)
            ctx = (cache_b_k[:, j].reshape(bs, past, B_KV_WIDTH).astype(BF16),
                   cache_b_v[:, j].reshape(bs, past, B_KV_WIDTH).astype(BF16))
            att = _attention(q, k, v, ctx, sink, att, n_seq=bs, seq_len=ls, row0=tp, q_tile=ATTN_Q_TILE,
                             window=WINDOW, **common)
            w_out = even_w_out[j].astype(BF16)
            mixes = [four, att]
            w_parts = [w_out[:A_WIDTH], w_out[A_WIDTH:]]
        else:
            q, k, v, k_state, v_state = _in_projection(
                geom, layer, x, g_mix, mods, odd_w_in[j].astype(BF16), rope_c[0], rope_c[1],
                [odd_q_norm[j][None, :], odd_k_norm[j][None, :]],
                _proj_odd_kernel, (C_Q_WIDTH, C_KV_WIDTH, C_KV_WIDTH), C_KV_WIDTH, "proj_odd")
            states["ck"].append(k_state.reshape(bp, lp, C_KV_HEADS, C_HEAD_DIM))
            states["cv"].append(v_state.reshape(bp, lp, C_KV_HEADS, C_HEAD_DIM))
            common = dict(kv_heads=C_KV_HEADS, groups=C_HEADS // C_KV_HEADS, dh=C_HEAD_DIM, window=None)
            att = _attention(q, k, v, None, None, None, n_seq=bp, seq_len=lp, row0=0, q_tile=lp, **common)
            ctx = (cache_c_k[:, j].reshape(bs, past, C_KV_WIDTH).astype(BF16),
                   cache_c_v[:, j].reshape(bs, past, C_KV_WIDTH).astype(BF16))
            att = _attention(q, k, v, ctx, None, att, n_seq=bs, seq_len=ls, row0=tp, q_tile=ATTN_Q_TILE, **common)
            mixes = [att]
            w_parts = [odd_w_out[j].astype(BF16)]
        x, h, pos, gates, seg_rows, seg_off = _out_route(
            geom, layer, mixes, w_parts, x, mods, g_ffn, router_w[layer].astype(BF16), router_b[layer][None, :])
        final_g = final_norm[None, :] if layer == depth - 1 else None
        x = _moe(geom, layer, h, pos, gates, seg_rows, seg_off, x, mods,
                 moe_w_gu, moe_b_gu, moe_w_down, moe_b_down, final_g)

    y_prompt = x[0].reshape(bp, lp, d)
    y_sample = x[1].reshape(bs, ls, d)
    return (y_prompt, y_sample,
            jnp.stack(states["bk"], axis=1), jnp.stack(states["bv"], axis=1),
            jnp.stack(states["ck"], axis=1), jnp.stack(states["cv"], axis=1))
```

```python
import functools

import numpy as np
import jax
import jax.numpy as jnp
from jax import lax
from jax.experimental import pallas as pl
from jax.experimental.pallas import tpu as pltpu

F32 = jnp.float32
BF16 = jnp.bfloat16
I32 = jnp.int32
U32 = jnp.uint32

GRID_W = 64
A_GROUPS = 4
A_GROUP_DIM = 128
A_WIDTH = A_GROUPS * A_GROUP_DIM
B_HEADS = 8
B_KV_HEADS = 2
B_HEAD_DIM = 64
B_Q_WIDTH = B_HEADS * B_HEAD_DIM
B_KV_WIDTH = B_KV_HEADS * B_HEAD_DIM
WINDOW = 128
C_HEADS = 8
C_KV_HEADS = 2
C_HEAD_DIM = 128
C_Q_WIDTH = C_HEADS * C_HEAD_DIM
C_KV_WIDTH = C_KV_HEADS * C_HEAD_DIM
ROPE_THETA = 10000.0
N_EXPERTS = 32
TOP_K = 4
SWIGLU_LIMIT = 7.0
SWIGLU_ALPHA = 1.702
EPS = 1e-6

LANES = 128
TOKEN_TILE = 256
EXPERT_TILE = 256
FFN_TILE = 256
ATTN_Q_TILE = 128
ATTN_KEY_CHUNK = 512
SEG_ALIGN = 8
SEG_CHUNK = 32
WAIT_CHUNK = 256
SORTED_ROWS = -(-(TOKEN_TILE * TOP_K + N_EXPERTS * (SEG_ALIGN - 1)) // LANES) * LANES
VMEM_LIMIT = 56 * 1024 * 1024
MASKED = -1e30
N_MOD = 6
MOD_ROWS = 8


def _cparams(*sem):
    return pltpu.CompilerParams(dimension_semantics=tuple(sem), vmem_limit_bytes=VMEM_LIMIT)


def _dot(a, b):
    return jnp.dot(a, b, preferred_element_type=F32)


def _dot_nt(a, b):
    return lax.dot_general(a, b, (((1,), (1,)), ((), ())), preferred_element_type=F32)


def _rms(x, g):
    return x * lax.rsqrt(jnp.mean(x * x, axis=-1, keepdims=True) + EPS) * g


def _dft_tables(n):
    j = np.arange(n, dtype=np.int64)
    ang = 2.0 * np.pi * ((j[:, None] * j[None, :]) % n).astype(np.float64) / n
    s = 1.0 / np.sqrt(n)
    return np.cos(ang) * s, np.sin(ang) * s


def _rope_tables(n_prompt_rows, n_latent, head_dim):
    quarter = head_dim // 4
    pos = np.arange(n_latent)
    row = (pos // GRID_W).astype(np.float32)
    col = (pos % GRID_W).astype(np.float32)
    inv = (np.float32(ROPE_THETA) ** (-np.arange(quarter, dtype=np.float32) / np.float32(quarter))).astype(np.float32)
    ang_row = (row[:, None] * inv[None, :]).astype(np.float32)
    ang_col = (col[:, None] * inv[None, :]).astype(np.float32)
    cos_h = np.concatenate([np.cos(ang_row)] * 2 + [np.cos(ang_col)] * 2, axis=1)
    sin_h = np.concatenate([-np.sin(ang_row), np.sin(ang_row), -np.sin(ang_col), np.sin(ang_col)], axis=1)
    reps = LANES // head_dim
    cos_l = np.tile(cos_h, (1, reps)).astype(np.float32)
    sin_l = np.tile(sin_h, (1, reps)).astype(np.float32)
    cos = np.concatenate([np.ones((n_prompt_rows, LANES), np.float32), cos_l], axis=0)
    sin = np.concatenate([np.zeros((n_prompt_rows, LANES), np.float32), sin_l], axis=0)
    return cos, sin


def _rope(x, cos, sin, quarter):
    lane = lax.broadcasted_iota(I32, (x.shape[0], LANES), 1)
    first = ((lane // quarter) % 2) == 0
    outs = []
    for c in range(x.shape[1] // LANES):
        xc = x[:, c * LANES:(c + 1) * LANES]
        partner = jnp.where(first, pltpu.roll(xc, LANES - quarter, 1), pltpu.roll(xc, quarter, 1))
        outs.append(xc * cos + partner * sin)
    return outs[0] if len(outs) == 1 else jnp.concatenate(outs, axis=1)


def _head_rms(x, g):
    outs = []
    for c in range(x.shape[1] // LANES):
        outs.append(_rms(x[:, c * LANES:(c + 1) * LANES], g))
    return outs[0] if len(outs) == 1 else jnp.concatenate(outs, axis=1)


def _mod_kernel(c_ref, w_ref, b_ref, o_ref):
    c = c_ref[...]
    s = c * (1.0 / (1.0 + jnp.exp(-c)))
    o_ref[0] = _dot(s.astype(BF16), w_ref[0].astype(BF16)) + b_ref[0]


def _modulation(cond, mod_w, mod_b):
    depth, d, n = mod_w.shape
    tn = 1536
    return pl.pallas_call(
        _mod_kernel,
        out_shape=jax.ShapeDtypeStruct((depth, MOD_ROWS, n), F32),
        grid=(depth, n // tn),
        in_specs=[
            pl.BlockSpec((MOD_ROWS, d), lambda l, j: (0, 0)),
            pl.BlockSpec((1, d, tn), lambda l, j: (l, 0, j)),
            pl.BlockSpec((1, 1, tn), lambda l, j: (l, 0, j)),
        ],
        out_specs=pl.BlockSpec((1, MOD_ROWS, tn), lambda l, j: (l, 0, j)),
        compiler_params=_cparams("arbitrary", "arbitrary"),
        name="modulation",
    )(cond, mod_w, mod_b.reshape(depth, 1, n))


class _Geom:
    def __init__(self, bp, lp, bs, ls):
        self.bp, self.lp, self.bs, self.ls = bp, lp, bs, ls
        self.tp = bp * lp
        self.t = bp * lp + bs * ls
        assert lp == TOKEN_TILE and ls % TOKEN_TILE == 0 and self.tp % ls == 0
        self.n_ptiles = self.tp // TOKEN_TILE
        self.tiles_per_lat = ls // TOKEN_TILE
        self.n_tiles = self.t // TOKEN_TILE

    def group(self, i):
        return jnp.where(i < self.n_ptiles, 0, 1 + (i - self.n_ptiles) // self.tiles_per_lat)

    def pos_block(self, i):
        return jnp.where(i < self.n_ptiles, 0, 1 + (i - self.n_ptiles) % self.tiles_per_lat)


def _mod_spec(geom, layer, which, d):
    def imap(i):
        return ((layer * MOD_ROWS + geom.group(i)) * N_MOD + which, 0, 0)
    return pl.BlockSpec((None, 1, d), imap)


def _stream_specs(geom, x, d):
    tm = TOKEN_TILE
    if isinstance(x, tuple):
        return ([pl.BlockSpec((tm, d), lambda i, *_: (jnp.minimum(i, geom.n_ptiles - 1), 0)),
                 pl.BlockSpec((tm, d), lambda i, *_: (jnp.maximum(i - geom.n_ptiles, 0), 0))], list(x))
    return [pl.BlockSpec((tm, d), lambda i, *_: (i, 0))], [x]


def _stream_tile(x_refs, n_ptiles):
    if len(x_refs) == 1:
        return x_refs[0][...]
    return jnp.where(pl.program_id(0) < n_ptiles, x_refs[0][...], x_refs[1][...])


def _store_kv(k, v, kb_ref, vb_ref, ks_ref, vs_ref, n_ptiles):
    kb_ref[...] = k.astype(BF16)
    vb_ref[...] = v.astype(BF16)

    @pl.when(pl.program_id(0) < n_ptiles)
    def _():
        ks_ref[...] = k
        vs_ref[...] = v


def _proj_even_kernel(*refs, n_x, n_ptiles):
    x_refs = refs[:n_x]
    (g_ref, sh_ref, sc_ref, w_ref, cos_ref, sin_ref, dft_ref,
     tc_ref, ts_ref, q_ref, kb_ref, vb_ref, ks_ref, vs_ref) = refs[n_x:]
    h = _rms(_stream_tile(x_refs, n_ptiles), g_ref[...]) * (1.0 + sc_ref[...]) + sh_ref[...]
    p = _dot(h.astype(BF16), w_ref[...])
    cos = cos_ref[...]
    sin = sin_ref[...]
    dft = dft_ref[...]
    tcs, tss = [], []
    for g in range(A_GROUPS):
        t = _dot(p[:, g * A_GROUP_DIM:(g + 1) * A_GROUP_DIM].astype(BF16), dft)
        tcs.append(t[:, :A_GROUP_DIM])
        tss.append(t[:, A_GROUP_DIM:])
    tc_ref[...] = jnp.concatenate(tcs, axis=1).astype(BF16)
    ts_ref[...] = jnp.concatenate(tss, axis=1).astype(BF16)
    o = A_WIDTH
    q = _rope(p[:, o:o + B_Q_WIDTH], cos, sin, B_HEAD_DIM // 4)
    q_ref[...] = (q * B_HEAD_DIM ** -0.5).astype(BF16)
    o += B_Q_WIDTH
    k = _rope(p[:, o:o + B_KV_WIDTH], cos, sin, B_HEAD_DIM // 4)
    o += B_KV_WIDTH
    _store_kv(k, p[:, o:o + B_KV_WIDTH], kb_ref, vb_ref, ks_ref, vs_ref, n_ptiles)


def _proj_odd_kernel(*refs, n_x, n_ptiles):
    x_refs = refs[:n_x]
    (g_ref, sh_ref, sc_ref, w_ref, cos_ref, sin_ref, qn_ref, kn_ref,
     q_ref, kb_ref, vb_ref, ks_ref, vs_ref) = refs[n_x:]
    h = _rms(_stream_tile(x_refs, n_ptiles), g_ref[...]) * (1.0 + sc_ref[...]) + sh_ref[...]
    p = _dot(h.astype(BF16), w_ref[...])
    cos = cos_ref[...]
    sin = sin_ref[...]
    q = _head_rms(p[:, :C_Q_WIDTH], qn_ref[...])
    k = _head_rms(p[:, C_Q_WIDTH:C_Q_WIDTH + C_KV_WIDTH], kn_ref[...])
    q_ref[...] = (_rope(q, cos, sin, C_HEAD_DIM // 4) * C_HEAD_DIM ** -0.5).astype(BF16)
    k = _rope(k, cos, sin, C_HEAD_DIM // 4)
    _store_kv(k, p[:, C_Q_WIDTH + C_KV_WIDTH:], kb_ref, vb_ref, ks_ref, vs_ref, n_ptiles)


def _in_projection(geom, layer, x, norm_g, mods, w, cos, sin, extras, kernel, out_widths, kv_width, name):
    t, d = geom.t, w.shape[0]
    tm = TOKEN_TILE
    n_out = w.shape[1]
    row = lambda i: (i, 0)
    const2 = lambda i: (0, 0)
    x_specs, x_args = _stream_specs(geom, x, d)
    in_specs = x_specs + [
        pl.BlockSpec((1, d), const2),
        _mod_spec(geom, layer, 0, d),
        _mod_spec(geom, layer, 1, d),
        pl.BlockSpec((d, n_out), const2),
        pl.BlockSpec((tm, LANES), lambda i: (geom.pos_block(i), 0)),
        pl.BlockSpec((tm, LANES), lambda i: (geom.pos_block(i), 0)),
    ] + [pl.BlockSpec(e.shape, const2) for e in extras]
    return pl.pallas_call(
        functools.partial(kernel, n_x=len(x_args), n_ptiles=geom.n_ptiles),
        out_shape=([jax.ShapeDtypeStruct((t, wd), BF16) for wd in out_widths]
                   + [jax.ShapeDtypeStruct((geom.tp, kv_width), F32)] * 2),
        grid=(geom.n_tiles,),
        in_specs=in_specs,
        out_specs=([pl.BlockSpec((tm, wd), row) for wd in out_widths]
                   + [pl.BlockSpec((tm, kv_width), lambda i: (jnp.minimum(i, geom.n_ptiles - 1), 0))] * 2),
        compiler_params=_cparams("arbitrary"),
        name=name,
    )(*x_args, norm_g, mods, mods, w, cos, sin, *extras)


def _fourier_kernel(cl_ref, sl_ref, tc_ref, ts_ref, *rest):
    o_ref = rest[-1]
    o_ref[...] = (_dot(cl_ref[...], tc_ref[...]) - _dot(sl_ref[...], ts_ref[...])).astype(o_ref.dtype)


def _fourier_tokens(tc, ts, cl, sl, n_seq, seq_len, row0, prev):
    t, width = tc.shape
    tr = min(seq_len, 512)
    n_r = seq_len // tr
    assert row0 % seq_len == 0
    seq0 = row0 // seq_len
    out0 = row0 // tr
    in_specs = [
        pl.BlockSpec((tr, seq_len), lambda s, r: (r, 0)),
        pl.BlockSpec((tr, seq_len), lambda s, r: (r, 0)),
        pl.BlockSpec((seq_len, width), lambda s, r: (seq0 + s, 0)),
        pl.BlockSpec((seq_len, width), lambda s, r: (seq0 + s, 0)),
    ]
    args = [cl, sl, tc, ts]
    aliases = {}
    if prev is not None:
        in_specs.append(pl.BlockSpec(memory_space=pl.ANY))
        args.append(prev)
        aliases = {4: 0}
    return pl.pallas_call(
        _fourier_kernel,
        out_shape=jax.ShapeDtypeStruct((t, width), BF16),
        grid=(n_seq, n_r),
        in_specs=in_specs,
        out_specs=pl.BlockSpec((tr, width), lambda s, r: (out0 + s * n_r + r, 0)),
        input_output_aliases=aliases,
        compiler_params=_cparams("arbitrary", "arbitrary"),
        name="fourier_tokens",
    )(*args)


def _attend(q, chunks, sink, o0, dh, den_col):
    m = sink
    acc = None
    den = None
    for k, v, mask in chunks:
        s = _dot_nt(q, k)
        if mask is not None:
            s = jnp.where(mask, s, MASKED)
        m_new = jnp.max(s, axis=-1, keepdims=True)
        if m is not None:
            m_new = jnp.maximum(m, m_new)
        p = jnp.exp(s - m_new)
        pv = _dot(p.astype(BF16), v)
        if acc is None:
            acc = pv
            if den_col is None:
                den = jnp.sum(p, axis=-1, keepdims=True)
        else:
            alpha = jnp.exp(m - m_new)
            acc = alpha * acc + pv
            if den_col is None:
                den = alpha * den + jnp.sum(p, axis=-1, keepdims=True)
        m = m_new
    if den_col is not None:
        den = acc[:, den_col:den_col + 1]
    if sink is not None:
        den = den + jnp.exp(sink - m)
    return acc[:, o0:o0 + dh] / den


def _attend_two_pass(q, chunks, sink):
    scores = []
    m = sink
    for k, _, mask in chunks:
        s = _dot_nt(q, k)
        if mask is not None:
            s = jnp.where(mask, s, MASKED)
        scores.append(s)
        mx = jnp.max(s, axis=-1, keepdims=True)
        m = mx if m is None else jnp.maximum(m, mx)
    den = None if sink is None else jnp.exp(sink - m)
    acc = None
    for (_, v, _), s in zip(chunks, scores):
        e = jnp.exp(s - m)
        es = jnp.sum(e, axis=-1, keepdims=True)
        den = es if den is None else den + es
        o = _dot(e.astype(BF16), v)
        acc = o if acc is None else acc + o
    return acc / den


def _stack_heads(q, kv, groups, dh):
    return jnp.concatenate([q[:, (kv * groups + g) * dh:(kv * groups + g + 1) * dh] for g in range(groups)], axis=0)


def _sink_column(sink_ref, kv, groups, rows):
    return jnp.concatenate([jnp.full((rows, 1), sink_ref[kv * groups + g], F32) for g in range(groups)], axis=0)


def _head_values(v, kv, dh, with_ones):
    if not with_ones:
        return v[:, kv * dh:(kv + 1) * dh], 0, None
    assert dh == LANES
    lane = lax.broadcasted_iota(I32, (v.shape[0], LANES), 1)
    ones = jnp.where(lane == 0, 1.0, 0.0).astype(BF16)
    return jnp.concatenate([v[:, kv * dh:(kv + 1) * dh], ones], axis=1), 0, dh


def _attn_kernel(*refs, kv_heads, groups, dh, has_sink, has_ctx, window, q_tile, seq_len, chunk):
    refs = list(refs)
    sink_ref = refs.pop(0) if has_sink else None
    q_ref, k_ref, v_ref = refs[:3]
    ck_ref, cv_ref = (refs[3], refs[4]) if has_ctx else (None, None)
    o_ref = refs[-1]
    q = q_ref[...]
    rows = q.shape[0]
    if window is None:
        spans = [(c * chunk, chunk) for c in range(seq_len // chunk)]
        mask = None
    else:
        n = pl.program_id(1)
        band = q_tile + 2 * window
        start = pl.multiple_of(jnp.clip(n * q_tile - window, 0, seq_len - band), LANES)
        spans = [(start, band)]
        qpos = n * q_tile + lax.broadcasted_iota(I32, (groups * rows, band), 0) % rows
        kpos = start + lax.broadcasted_iota(I32, (groups * rows, band), 1)
        mask = jnp.abs(kpos - qpos) <= window
    sources = [(k_ref[pl.ds(s0, n_s), :], v_ref[pl.ds(s0, n_s), :], mask) for s0, n_s in spans]
    if has_ctx:
        sources.append((ck_ref[...], cv_ref[...], None))
    outs = []
    for kv in range(kv_heads):
        online = dh == LANES and len(sources) > 1
        chunks = []
        for k, v, msk in sources:
            vh, o0, den_col = _head_values(v, kv, dh, with_ones=online)
            chunks.append((k[:, kv * dh:(kv + 1) * dh], vh, msk))
        sink = _sink_column(sink_ref, kv, groups, rows) if has_sink else None
        qh = _stack_heads(q, kv, groups, dh)
        o = _attend(qh, chunks, sink, o0, dh, den_col) if online else _attend_two_pass(qh, chunks, sink)
        outs.extend(o[g * rows:(g + 1) * rows] for g in range(groups))
    o_ref[...] = jnp.concatenate(outs, axis=1).astype(o_ref.dtype)


def _attention(q, k, v, ctx, sink, prev, *, n_seq, seq_len, row0, q_tile, kv_heads, groups, dh, window):
    t, qw = q.shape
    kw = k.shape[1]
    n_q = seq_len // q_tile
    assert row0 % seq_len == 0 and row0 % q_tile == 0
    seq0 = row0 // seq_len
    q0 = row0 // q_tile
    in_specs, args = [], []
    if sink is not None:
        in_specs.append(pl.BlockSpec(memory_space=pltpu.SMEM))
        args.append(sink)
    in_specs += [
        pl.BlockSpec((q_tile, qw), lambda s, n: (q0 + s * n_q + n, 0)),
        pl.BlockSpec((seq_len, kw), lambda s, n: (seq0 + s, 0)),
        pl.BlockSpec((seq_len, kw), lambda s, n: (seq0 + s, 0)),
    ]
    args += [q, k, v]
    if ctx is not None:
        p = ctx[0].shape[1]
        in_specs += [pl.BlockSpec((None, p, kw), lambda s, n: (s, 0, 0))] * 2
        args += list(ctx)
    aliases = {}
    if prev is not None:
        in_specs.append(pl.BlockSpec(memory_space=pl.ANY))
        aliases = {len(args): 0}
        args.append(prev)
    kern = functools.partial(
        _attn_kernel, kv_heads=kv_heads, groups=groups, dh=dh, has_sink=sink is not None,
        has_ctx=ctx is not None, window=window, q_tile=q_tile, seq_len=seq_len, chunk=min(seq_len, ATTN_KEY_CHUNK))
    return pl.pallas_call(
        kern,
        out_shape=jax.ShapeDtypeStruct((t, qw), BF16),
        grid=(n_seq, n_q),
        in_specs=in_specs,
        out_specs=pl.BlockSpec((q_tile, qw), lambda s, n: (q0 + s * n_q + n, 0)),
        input_output_aliases=aliases,
        compiler_params=_cparams("arbitrary", "arbitrary"),
        name="attention",
    )(*args)


def _out_route_kernel(*refs, n_mix, n_x, n_ptiles):
    mix_refs = refs[:n_mix]
    w_refs = refs[n_mix:2 * n_mix]
    x_refs = refs[2 * n_mix:2 * n_mix + n_x]
    (gate_ref, g2_ref, sh2_ref, sc2_ref, rw_ref, rb_ref,
     xo_ref, h_ref, pos_ref, gates_ref, rows_ref, off_ref) = refs[2 * n_mix + n_x:]

    acc = None
    for m_ref, w_ref in zip(mix_refs, w_refs):
        part = _dot(m_ref[...], w_ref[...])
        acc = part if acc is None else acc + part
    xn = _stream_tile(x_refs, n_ptiles) + gate_ref[...] * acc
    xo_ref[...] = xn
    h = _rms(xn, g2_ref[...]) * (1.0 + sc2_ref[...]) + sh2_ref[...]
    hb = h.astype(BF16)
    h_ref[...] = hb

    logits = _dot(hb, rw_ref[...]) + rb_ref[...]
    tm, ne = logits.shape
    lane = lax.broadcasted_iota(I32, (tm, ne), 1).astype(F32)
    lane4 = lax.broadcasted_iota(I32, (tm, TOP_K), 1)
    work = logits
    sels, vals = [], []
    for _ in range(TOP_K):
        mx = jnp.max(work, axis=-1, keepdims=True)
        first = jnp.min(jnp.where(work == mx, lane, float(ne)), axis=-1, keepdims=True)
        sel = lane == first
        work = jnp.where(sel, -jnp.inf, work)
        sels.append(sel)
        vals.append(mx)
    exps = [jnp.exp(v - vals[0]) for v in vals]
    den = exps[0] + exps[1] + exps[2] + exps[3]

    onehot = jnp.zeros((tm, ne), F32)
    for sel in sels:
        onehot = onehot + sel.astype(F32)
    r_i = lax.broadcasted_iota(I32, (tm, tm), 0)
    c_i = lax.broadcasted_iota(I32, (tm, tm), 1)
    before = jnp.where(c_i < r_i, 1.0, 0.0).astype(BF16)
    earlier = _dot(before, onehot.astype(BF16))

    cnt = jnp.sum(onehot, axis=0, keepdims=True)
    seg8 = jnp.floor((cnt + (SEG_ALIGN - 1.0)) * (1.0 / SEG_ALIGN))
    e_r = lax.broadcasted_iota(I32, (ne, ne), 0)
    e_c = lax.broadcasted_iota(I32, (ne, ne), 1)
    upper = jnp.where(e_r < e_c, 1.0, 0.0).astype(BF16)
    off8 = _dot(jnp.broadcast_to(seg8, (SEG_ALIGN, ne)).astype(BF16), upper)[0:1]
    seg_off = off8 * SEG_ALIGN
    base = seg_off + earlier

    pos_o = jnp.zeros((tm, TOP_K), I32)
    gate_o = jnp.zeros((tm, TOP_K), F32)
    for k in range(TOP_K):
        pos_k = jnp.sum(jnp.where(sels[k], base, 0.0), axis=-1, keepdims=True)
        pos_o = jnp.where(lane4 == k, pos_k.astype(I32), pos_o)
        gate_o = jnp.where(lane4 == k, exps[k] / den, gate_o)
    pos_ref[...] = pos_o
    gates_ref[...] = gate_o
    rows_ref[0] = (seg8 * SEG_ALIGN).astype(I32)
    off_ref[0] = seg_off.astype(I32)


def _out_route(geom, layer, mixes, w_parts, x, mods, norm_g, router_w, router_b):
    t, d = geom.t, w_parts[0].shape[1]
    tm = TOKEN_TILE
    row = lambda i: (i, 0)
    const2 = lambda i: (0, 0)
    x_specs, x_args = _stream_specs(geom, x, d)
    in_specs = [pl.BlockSpec((tm, m.shape[1]), row) for m in mixes]
    in_specs += [pl.BlockSpec(w.shape, const2) for w in w_parts]
    in_specs += x_specs
    in_specs += [
        _mod_spec(geom, layer, 2, d),
        pl.BlockSpec((1, d), const2),
        _mod_spec(geom, layer, 3, d),
        _mod_spec(geom, layer, 4, d),
        pl.BlockSpec(router_w.shape, const2),
        pl.BlockSpec((1, N_EXPERTS), const2),
    ]
    seg3 = lambda i: (i, 0, 0)
    out_shape = [
        jax.ShapeDtypeStruct((t, d), F32),
        jax.ShapeDtypeStruct((t, d), BF16),
        jax.ShapeDtypeStruct((t, TOP_K), I32),
        jax.ShapeDtypeStruct((t, TOP_K), F32),
        jax.ShapeDtypeStruct((geom.n_tiles, 1, N_EXPERTS), I32),
        jax.ShapeDtypeStruct((geom.n_tiles, 1, N_EXPERTS), I32),
    ]
    out_specs = [
        pl.BlockSpec((tm, d), row),
        pl.BlockSpec((tm, d), row),
        pl.BlockSpec((tm, TOP_K), row),
        pl.BlockSpec((tm, TOP_K), row),
        pl.BlockSpec((1, 1, N_EXPERTS), seg3),
        pl.BlockSpec((1, 1, N_EXPERTS), seg3),
    ]
    return pl.pallas_call(
        functools.partial(_out_route_kernel, n_mix=len(mixes), n_x=len(x_args), n_ptiles=geom.n_ptiles),
        out_shape=out_shape,
        grid=(geom.n_tiles,),
        in_specs=in_specs,
        out_specs=out_specs,
        compiler_params=_cparams("arbitrary"),
        name="out_route",
    )(*mixes, *w_parts, *x_args, mods, norm_g, mods, mods, router_w, router_b)


def _pack_pairs(v):
    n = v.shape[1] // 2
    bits = lax.bitcast_convert_type(v, U32)
    return (bits[:, :n] & jnp.uint32(0xFFFF0000)) | (bits[:, n:] >> 16)


def _unpack_pairs(p):
    hi = lax.bitcast_convert_type(p & jnp.uint32(0xFFFF0000), F32)
    lo = lax.bitcast_convert_type(p << 16, F32)
    return jnp.concatenate([hi, lo], axis=1).astype(BF16)


def _segment_copies(i, rows_ref, off_ref, dst_ref, make_copy):
    def per_expert(e, total):
        a = i * N_EXPERTS + e
        n_rows = rows_ref[a]
        local0 = off_ref[a]
        slot0 = dst_ref[a]

        def big(c, carry):
            r = c * SEG_CHUNK
            make_copy(pl.multiple_of(local0 + r, SEG_ALIGN), pl.multiple_of(slot0 + r, SEG_ALIGN), SEG_CHUNK).start()
            return carry

        lax.fori_loop(0, n_rows // SEG_CHUNK, big, 0)
        size = SEG_CHUNK // 2
        while size >= SEG_ALIGN:
            done = n_rows - n_rows % (2 * size)

            @pl.when(n_rows % (2 * size) >= size)
            def _(done=done, size=size):
                make_copy(pl.multiple_of(local0 + done, SEG_ALIGN), pl.multiple_of(slot0 + done, SEG_ALIGN),
                          size).start()

            size //= 2
        return total + n_rows

    return lax.fori_loop(0, N_EXPERTS, per_expert, 0)


def _wait_copies(n_rows, make_copy):
    def wait_big(c, carry):
        make_copy(0, 0, WAIT_CHUNK).wait()
        return carry
    lax.fori_loop(0, n_rows // WAIT_CHUNK, wait_big, 0)
    size = WAIT_CHUNK // 2
    while size >= SEG_ALIGN:
        @pl.when(n_rows % (2 * size) >= size)
        def _(size=size):
            make_copy(0, 0, size).wait()

        size //= 2


def _dispatch_kernel(rows_ref, off_ref, dst_ref, tot_ref, h_ref, pos_ref, xs_ref, sorted_ref, sem):
    i = pl.program_id(0)
    buf = i % 2
    tm = h_ref.shape[0]
    n_sorted = sorted_ref.shape[1]
    pos = pos_ref[...]
    lane = lax.broadcasted_iota(I32, (tm, n_sorted), 1)
    hit = jnp.zeros((tm, n_sorted), F32)
    for k in range(TOP_K):
        hit = jnp.where(pos[:, k:k + 1] == lane, 1.0, hit)
    sorted_ref[buf] = _pack_pairs(lax.dot_general(hit.astype(BF16), h_ref[...], (((0,), (0,)), ((), ())),
                                                  preferred_element_type=F32))

    def copies_from(b):
        def make_copy(local, slot, rows):
            return pltpu.make_async_copy(sorted_ref.at[b, pl.ds(local, rows)], xs_ref.at[pl.ds(slot, rows)],
                                         sem.at[b])
        return make_copy

    _segment_copies(i, rows_ref, off_ref, dst_ref, copies_from(buf))

    @pl.when(i > 0)
    def _():
        _wait_copies(tot_ref[jnp.maximum(i - 1, 0)], copies_from(1 - buf))

    @pl.when(i == pl.num_programs(0) - 1)
    def _():
        _wait_copies(tot_ref[i], copies_from(buf))


def _dispatch(h, pos, seg_rows, seg_off, seg_dst, tile_rows, n_slots):
    t, d = h.shape
    tm = TOKEN_TILE
    return pl.pallas_call(
        _dispatch_kernel,
        out_shape=jax.ShapeDtypeStruct((n_slots, d // 2), U32),
        grid_spec=pltpu.PrefetchScalarGridSpec(
            num_scalar_prefetch=4,
            grid=(t // tm,),
            in_specs=[
                pl.BlockSpec((tm, d), lambda i, *_: (i, 0)),
                pl.BlockSpec((tm, TOP_K), lambda i, *_: (i, 0)),
            ],
            out_specs=pl.BlockSpec(memory_space=pl.ANY),
            scratch_shapes=[pltpu.VMEM((2, SORTED_ROWS, d // 2), U32), pltpu.SemaphoreType.DMA((2,))],
        ),
        compiler_params=_cparams("arbitrary"),
        name="moe_dispatch",
    )(seg_rows, seg_off, seg_dst, tile_rows, h, pos)


def _ffn_kernel(rows_ref, start_ref, wgu_ref, bgu_ref, wd_ref, bd_ref, xs_ref, ys_ref,
                wgu_bf, wd_bf, xbuf, ybuf, sem_in, sem_out):
    e = pl.program_id(0)
    d_ff = wd_ref.shape[1]
    tb = xbuf.shape[1]
    n_rows = rows_ref[e]
    n_tiles = (n_rows + tb - 1) // tb
    base = start_ref[e]

    chunk = 128
    def cast_gu(c, carry):
        r = pl.multiple_of(c * chunk, chunk)
        wgu_bf[pl.ds(r, chunk), :] = wgu_ref[0, pl.ds(r, chunk), :].astype(BF16)
        return carry
    lax.fori_loop(0, wgu_ref.shape[1] // chunk, cast_gu, 0)
    def cast_d(c, carry):
        r = pl.multiple_of(c * chunk, chunk)
        wd_bf[pl.ds(r, chunk), :] = wd_ref[0, pl.ds(r, chunk), :].astype(BF16)
        return carry
    lax.fori_loop(0, d_ff // chunk, cast_d, 0)

    def x_copy(s, slot):
        r = pl.multiple_of(base + s * tb, EXPERT_TILE)
        return pltpu.make_async_copy(xs_ref.at[pl.ds(r, tb)], xbuf.at[slot], sem_in.at[slot])

    def y_copy(s, slot):
        r = pl.multiple_of(base + s * tb, EXPERT_TILE)
        return pltpu.make_async_copy(ybuf.at[slot], ys_ref.at[pl.ds(r, tb)], sem_out.at[slot])

    @pl.when(n_tiles > 0)
    def _():
        x_copy(0, 0).start()

    def tile(s, carry):
        slot = s % 2
        x_copy(s, slot).wait()

        @pl.when(s + 1 < n_tiles)
        def _():
            x_copy(s + 1, 1 - slot).start()

        @pl.when(s >= 2)
        def _():
            y_copy(s - 2, slot).wait()

        rows = s * tb + lax.broadcasted_iota(I32, (tb, 1), 0)
        x = _unpack_pairs(jnp.where(rows < n_rows, xbuf[slot], jnp.uint32(0)))
        gu = _dot(x, wgu_bf[...]) + bgu_ref[0]
        gate = jnp.minimum(gu[:, :d_ff], SWIGLU_LIMIT)
        up = jnp.clip(gu[:, d_ff:], -SWIGLU_LIMIT, SWIGLU_LIMIT)
        act = (up + 1.0) * (gate * (1.0 / (1.0 + jnp.exp(-SWIGLU_ALPHA * gate))))
        y = _dot(act.astype(BF16), wd_bf[...]) + bd_ref[0]
        ybuf[slot] = _pack_pairs(y.astype(BF16).astype(F32))
        y_copy(s, slot).start()
        return carry

    lax.fori_loop(0, n_tiles, tile, 0)

    @pl.when(n_tiles >= 2)
    def _():
        y_copy(n_tiles - 2, n_tiles % 2).wait()

    @pl.when(n_tiles >= 1)
    def _():
        y_copy(n_tiles - 1, (n_tiles - 1) % 2).wait()


def _expert_ffn(layer, xs, expert_rows, expert_start, w_gu, b_gu, w_down, b_down):
    n_slots, packed_w = xs.shape
    tb = FFN_TILE
    depth, ne, d, two_f = w_gu.shape
    d_ff = two_f // 2
    exp4 = lambda e, *_: (layer, e, 0, 0)
    return pl.pallas_call(
        _ffn_kernel,
        out_shape=jax.ShapeDtypeStruct((n_slots, packed_w), U32),
        grid_spec=pltpu.PrefetchScalarGridSpec(
            num_scalar_prefetch=2,
            grid=(ne,),
            in_specs=[
                pl.BlockSpec((None, 1, d, two_f), exp4),
                pl.BlockSpec((None, 1, 1, two_f), exp4),
                pl.BlockSpec((None, 1, d_ff, d), exp4),
                pl.BlockSpec((None, 1, 1, d), exp4),
                pl.BlockSpec(memory_space=pl.ANY),
            ],
            out_specs=pl.BlockSpec(memory_space=pl.ANY),
            scratch_shapes=[
                pltpu.VMEM((d, two_f), BF16), pltpu.VMEM((d_ff, d), BF16),
                pltpu.VMEM((2, tb, packed_w), U32), pltpu.VMEM((2, tb, packed_w), U32),
                pltpu.SemaphoreType.DMA((2,)), pltpu.SemaphoreType.DMA((2,)),
            ],
        ),
        compiler_params=_cparams("arbitrary"),
        name="expert_ffn",
    )(expert_rows, expert_start, w_gu, b_gu.reshape(depth, ne, 1, two_f),
      w_down, b_down.reshape(depth, ne, 1, d), xs)


def _combine_kernel(rows_ref, off_ref, dst_ref, tot_ref, x_ref, pos_ref, gates_ref, mg_ref, *rest,
                    final, n_ptiles):
    if final:
        fn_ref, ys_ref, op_ref, os_ref, buf, sem = rest
    else:
        ys_ref, o_ref, buf, sem = rest
    i = pl.program_id(0)
    cur = i % 2
    tm = x_ref.shape[0]
    n_sorted = buf.shape[1]

    def copies_into(b):
        def make_copy(local, slot, rows):
            return pltpu.make_async_copy(ys_ref.at[pl.ds(slot, rows)], buf.at[b, pl.ds(local, rows)], sem.at[b])
        return make_copy

    @pl.when(i == 0)
    def _():
        buf[...] = jnp.zeros_like(buf)
        _segment_copies(i, rows_ref, off_ref, dst_ref, copies_into(cur))

    @pl.when(i + 1 < pl.num_programs(0))
    def _():
        _segment_copies(i + 1, rows_ref, off_ref, dst_ref, copies_into(1 - cur))

    pos = pos_ref[...]
    g = gates_ref[...]
    lane = lax.broadcasted_iota(I32, (tm, n_sorted), 1)
    weight = jnp.zeros((tm, n_sorted), F32)
    for k in range(TOP_K):
        weight = jnp.where(pos[:, k:k + 1] == lane, g[:, k:k + 1], weight)
    _wait_copies(tot_ref[i], copies_into(cur))
    y = _dot(weight.astype(BF16), _unpack_pairs(buf[cur]))
    xn = x_ref[...] + mg_ref[...] * y
    if final:
        xn = _rms(xn, fn_ref[...])

        @pl.when(i < n_ptiles)
        def _():
            op_ref[...] = xn

        @pl.when(i >= n_ptiles)
        def _():
            os_ref[...] = xn
    else:
        o_ref[...] = xn


def _combine(geom, layer, ys, pos, seg_rows, seg_off, seg_dst, tile_rows, x, gates, mods, final_g):
    t, d = x.shape
    tm = TOKEN_TILE
    final = final_g is not None

    def mod_imap(i, *_):
        return ((layer * MOD_ROWS + geom.group(i)) * N_MOD + 5, 0, 0)

    row = lambda i, *_: (i, 0)
    in_specs = [
        pl.BlockSpec((tm, d), row),
        pl.BlockSpec((tm, TOP_K), row),
        pl.BlockSpec((tm, TOP_K), row),
        pl.BlockSpec((None, 1, d), mod_imap),
    ]
    args = [seg_rows, seg_off, seg_dst, tile_rows, x, pos, gates, mods]
    if final:
        in_specs.append(pl.BlockSpec((1, d), lambda i, *_: (0, 0)))
        args.append(final_g)
    in_specs.append(pl.BlockSpec(memory_space=pl.ANY))
    args.append(ys)
    if final:
        n_pt = geom.n_ptiles
        out_shape = [jax.ShapeDtypeStruct((geom.tp, d), F32), jax.ShapeDtypeStruct((t - geom.tp, d), F32)]
        out_specs = [pl.BlockSpec((tm, d), lambda i, *_: (jnp.minimum(i, n_pt - 1), 0)),
                     pl.BlockSpec((tm, d), lambda i, *_: (jnp.maximum(i - n_pt, 0), 0))]
    else:
        out_shape = jax.ShapeDtypeStruct((t, d), F32)
        out_specs = pl.BlockSpec((tm, d), row)
    return pl.pallas_call(
        functools.partial(_combine_kernel, final=final, n_ptiles=geom.n_ptiles),
        out_shape=out_shape,
        grid_spec=pltpu.PrefetchScalarGridSpec(
            num_scalar_prefetch=4,
            grid=(t // tm,),
            in_specs=in_specs,
            out_specs=out_specs,
            scratch_shapes=[pltpu.VMEM((2, SORTED_ROWS, d // 2), U32), pltpu.SemaphoreType.DMA((2,))],
        ),
        compiler_params=_cparams("arbitrary"),
        name="moe_combine",
    )(*args)


def _moe(geom, layer, h, pos, gates, seg_rows, seg_off, x, mods, w_gu, b_gu, w_down, b_down, final_g):
    t = h.shape[0]
    tb = EXPERT_TILE
    n_tok_tiles = seg_rows.shape[0]
    max_rows = t * TOP_K + n_tok_tiles * N_EXPERTS * (SEG_ALIGN - 1)
    n_blocks = -(-max_rows // tb) + N_EXPERTS
    rows = seg_rows[:, 0, :]
    cnt = jnp.sum(rows, axis=0)
    n_tiles_e = (cnt + tb - 1) // tb
    tile_end = jnp.cumsum(n_tiles_e)
    tile_start = tile_end - n_tiles_e
    expert_start = (tile_start * tb).astype(I32)
    seg_dst = expert_start[None, :] + jnp.cumsum(rows, axis=0) - rows
    rows_flat = rows.reshape(-1).astype(I32)
    off_flat = seg_off.reshape(-1).astype(I32)
    dst_flat = seg_dst.reshape(-1).astype(I32)
    tile_rows = jnp.sum(rows, axis=1).astype(I32)
    xs = _dispatch(h, pos, rows_flat, off_flat, dst_flat, tile_rows, n_blocks * tb + FFN_TILE - tb)
    ys = _expert_ffn(layer, xs, cnt.astype(I32), expert_start, w_gu, b_gu, w_down, b_down)
    return _combine(geom, layer, ys, pos, rows_flat, off_flat, dst_flat, tile_rows, x, gates, mods, final_g)


def kernel(x_prompt, x_sample, cache_b_k, cache_b_v, cache_c_k, cache_c_v, c, c_ctx,
           mod_w, mod_b, norm_mix, norm_ffn, even_w_in, even_w_out, even_sink,
           odd_w_in, odd_w_out, odd_q_norm, odd_k_norm, router_w, router_b,
           moe_w_gu, moe_b_gu, moe_w_down, moe_b_down, final_norm):
    bp, lp, d = x_prompt.shape
    bs, ls, _ = x_sample.shape
    past = cache_b_k.shape[2]
    depth = mod_w.shape[0]
    geom = _Geom(bp, lp, bs, ls)
    tp = geom.tp

    x = (x_prompt.reshape(tp, d), x_sample.reshape(bs * ls, d))
    cond = jnp.concatenate([c_ctx[None, :], c, jnp.zeros((MOD_ROWS - 1 - bs, d), F32)], axis=0)
    mods = _modulation(cond, mod_w, mod_b).reshape(depth * MOD_ROWS * N_MOD, 1, d)

    cn, sn = _dft_tables(A_GROUP_DIM)
    dft_chan = jnp.asarray(np.concatenate([cn, sn], axis=1), BF16)
    dft_p = [jnp.asarray(m, BF16) for m in _dft_tables(lp)]
    dft_s = [jnp.asarray(m, BF16) for m in _dft_tables(ls)]
    rope_b = [jnp.asarray(m) for m in _rope_tables(TOKEN_TILE, ls, B_HEAD_DIM)]
    rope_c = [jnp.asarray(m) for m in _rope_tables(TOKEN_TILE, ls, C_HEAD_DIM)]

    states = {"bk": [], "bv": [], "ck": [], "cv": []}
    for layer in range(depth):
        j = layer // 2
        g_mix = norm_mix[layer][None, :]
        g_ffn = norm_ffn[layer][None, :]
        if layer % 2 == 0:
            tc, ts, q, k, v, k_state, v_state = _in_projection(
                geom, layer, x, g_mix, mods, even_w_in[j].astype(BF16), rope_b[0], rope_b[1], [dft_chan],
                _proj_even_kernel, (A_WIDTH, A_WIDTH, B_Q_WIDTH, B_KV_WIDTH, B_KV_WIDTH), B_KV_WIDTH, "proj_even")
            states["bk"].append(k_state.reshape(bp, lp, B_KV_HEADS, B_HEAD_DIM))
            states["bv"].append(v_state.reshape(bp, lp, B_KV_HEADS, B_HEAD_DIM))
            four = _fourier_tokens(tc, ts, dft_p[0], dft_p[1], bp, lp, 0, None)
            four = _fourier_tokens(tc, ts, dft_s[0], dft_s[1], bs, ls, tp, four)
            sink = even_sink[j]
            common = dict(kv_heads=B_KV_HEADS, groups=B_HEADS // B_KV_HEADS, dh=B_HEAD_DIM)
            att = _attention(q, k, v, None, sink, None, n_seq=bp, seq_len=lp, row0=0, q_tile=lp,
                             window=None, **common)
            ctx = (cache_b_k[:, j].reshape(bs, past, B_KV_WIDTH).astype(BF16),
                   cache_b_v[:, j].reshape(bs, past, B_KV_WIDTH).astype(BF16))
            att = _attention(q, k, v, ctx, sink, att, n_seq=bs, seq_len=ls, row0=tp, q_tile=ATTN_Q_TILE,
                             window=WINDOW, **common)
            w_out = even_w_out[j].astype(BF16)
            mixes = [four, att]
            w_parts = [w_out[:A_WIDTH], w_out[A_WIDTH:]]
        else:
            q, k, v, k_state, v_state = _in_projection(
                geom, layer, x, g_mix, mods, odd_w_in[j].astype(BF16), rope_c[0], rope_c[1],
                [odd_q_norm[j][None, :], odd_k_norm[j][None, :]],
                _proj_odd_kernel, (C_Q_WIDTH, C_KV_WIDTH, C_KV_WIDTH), C_KV_WIDTH, "proj_odd")
            states["ck"].append(k_state.reshape(bp, lp, C_KV_HEADS, C_HEAD_DIM))
            states["cv"].append(v_state.reshape(bp, lp, C_KV_HEADS, C_HEAD_DIM))
            common = dict(kv_heads=C_KV_HEADS, groups=C_HEADS // C_KV_HEADS, dh=C_HEAD_DIM, window=None)
            att = _attention(q, k, v, None, None, None, n_seq=bp, seq_len=lp, row0=0, q_tile=lp, **common)
            ctx = (cache_c_k[:, j].reshape(bs, past, C_KV_WIDTH).astype(BF16),
                   cache_c_v[:, j].reshape(bs, past, C_KV_WIDTH).astype(BF16))
            att = _attention(q, k, v, ctx, None, att, n_seq=bs, seq_len=ls, row0=tp, q_tile=ATTN_Q_TILE, **common)
            mixes = [att]
            w_parts = [odd_w_out[j].astype(BF16)]
        x, h, pos, gates, seg_rows, seg_off = _out_route(
            geom, layer, mixes, w_parts, x, mods, g_ffn, router_w[layer].astype(BF16), router_b[layer][None, :])
        final_g = final_norm[None, :] if layer == depth - 1 else None
        x = _moe(geom, layer, h, pos, gates, seg_rows, seg_off, x, mods,
                 moe_w_gu, moe_b_gu, moe_w_down, moe_b_down, final_g)

    y_prompt = x[0].reshape(bp, lp, d)
    y_sample = x[1].reshape(bs, ls, d)
    return (y_prompt, y_sample,
            jnp.stack(states["bk"], axis=1), jnp.stack(states["bv"], axis=1),
            jnp.stack(states["ck"], axis=1), jnp.stack(states["cv"], axis=1))
```

```python
import functools

import numpy as np
import jax
import jax.numpy as jnp
from jax import lax
from jax.experimental import pallas as pl
from jax.experimental.pallas import tpu as pltpu

F32 = jnp.float32
BF16 = jnp.bfloat16
I32 = jnp.int32
U32 = jnp.uint32

GRID_W = 64
A_GROUPS = 4
A_GROUP_DIM = 128
A_WIDTH = A_GROUPS * A_GROUP_DIM
B_HEADS = 8
B_KV_HEADS = 2
B_HEAD_DIM = 64
B_Q_WIDTH = B_HEADS * B_HEAD_DIM
B_KV_WIDTH = B_KV_HEADS * B_HEAD_DIM
WINDOW = 128
C_HEADS = 8
C_KV_HEADS = 2
C_HEAD_DIM = 128
C_Q_WIDTH = C_HEADS * C_HEAD_DIM
C_KV_WIDTH = C_KV_HEADS * C_HEAD_DIM
ROPE_THETA = 10000.0
N_EXPERTS = 32
TOP_K = 4
SWIGLU_LIMIT = 7.0
SWIGLU_ALPHA = 1.702
EPS = 1e-6

LANES = 128
TOKEN_TILE = 256
EXPERT_TILE = 256
FFN_TILE = 256
ATTN_Q_TILE = 128
ATTN_KEY_CHUNK = 512
SEG_ALIGN = 8
COPY_SIZES = (32, 16, 8)
WAIT_CHUNK = 256
SORTED_ROWS = -(-(TOKEN_TILE * TOP_K + N_EXPERTS * (SEG_ALIGN - 1)) // LANES) * LANES
PLAN_WIDTHS = (SORTED_ROWS // COPY_SIZES[0],) + (N_EXPERTS,) * (len(COPY_SIZES) - 1)
VMEM_LIMIT = 56 * 1024 * 1024
MASKED = -1e30
N_MOD = 6
MOD_ROWS = 8


def _cparams(*sem):
    return pltpu.CompilerParams(dimension_semantics=tuple(sem), vmem_limit_bytes=VMEM_LIMIT)


def _dot(a, b):
    return jnp.dot(a, b, preferred_element_type=F32)


def _dot_nt(a, b):
    return lax.dot_general(a, b, (((1,), (1,)), ((), ())), preferred_element_type=F32)


def _rms(x, g):
    return x * lax.rsqrt(jnp.mean(x * x, axis=-1, keepdims=True) + EPS) * g


def _dft_tables(n):
    j = np.arange(n, dtype=np.int64)
    ang = 2.0 * np.pi * ((j[:, None] * j[None, :]) % n).astype(np.float64) / n
    s = 1.0 / np.sqrt(n)
    return np.cos(ang) * s, np.sin(ang) * s


def _rope_tables(n_prompt_rows, n_latent, head_dim):
    quarter = head_dim // 4
    pos = np.arange(n_latent)
    row = (pos // GRID_W).astype(np.float32)
    col = (pos % GRID_W).astype(np.float32)
    inv = (np.float32(ROPE_THETA) ** (-np.arange(quarter, dtype=np.float32) / np.float32(quarter))).astype(np.float32)
    ang_row = (row[:, None] * inv[None, :]).astype(np.float32)
    ang_col = (col[:, None] * inv[None, :]).astype(np.float32)
    cos_h = np.concatenate([np.cos(ang_row)] * 2 + [np.cos(ang_col)] * 2, axis=1)
    sin_h = np.concatenate([-np.sin(ang_row), np.sin(ang_row), -np.sin(ang_col), np.sin(ang_col)], axis=1)
    reps = LANES // head_dim
    cos_l = np.tile(cos_h, (1, reps)).astype(np.float32)
    sin_l = np.tile(sin_h, (1, reps)).astype(np.float32)
    cos = np.concatenate([np.ones((n_prompt_rows, LANES), np.float32), cos_l], axis=0)
    sin = np.concatenate([np.zeros((n_prompt_rows, LANES), np.float32), sin_l], axis=0)
    return cos, sin


def _rope(x, cos, sin, quarter):
    lane = lax.broadcasted_iota(I32, (x.shape[0], LANES), 1)
    first = ((lane // quarter) % 2) == 0
    outs = []
    for c in range(x.shape[1] // LANES):
        xc = x[:, c * LANES:(c + 1) * LANES]
        partner = jnp.where(first, pltpu.roll(xc, LANES - quarter, 1), pltpu.roll(xc, quarter, 1))
        outs.append(xc * cos + partner * sin)
    return outs[0] if len(outs) == 1 else jnp.concatenate(outs, axis=1)


def _head_rms(x, g):
    outs = []
    for c in range(x.shape[1] // LANES):
        outs.append(_rms(x[:, c * LANES:(c + 1) * LANES], g))
    return outs[0] if len(outs) == 1 else jnp.concatenate(outs, axis=1)


def _mod_kernel(c_ref, w_ref, b_ref, o_ref):
    c = c_ref[...]
    s = c * (1.0 / (1.0 + jnp.exp(-c)))
    o_ref[0] = _dot(s.astype(BF16), w_ref[0].astype(BF16)) + b_ref[0]


def _modulation(cond, mod_w, mod_b):
    depth, d, n = mod_w.shape
    tn = 1536
    return pl.pallas_call(
        _mod_kernel,
        out_shape=jax.ShapeDtypeStruct((depth, MOD_ROWS, n), F32),
        grid=(depth, n // tn),
        in_specs=[
            pl.BlockSpec((MOD_ROWS, d), lambda l, j: (0, 0)),
            pl.BlockSpec((1, d, tn), lambda l, j: (l, 0, j)),
            pl.BlockSpec((1, 1, tn), lambda l, j: (l, 0, j)),
        ],
        out_specs=pl.BlockSpec((1, MOD_ROWS, tn), lambda l, j: (l, 0, j)),
        compiler_params=_cparams("arbitrary", "arbitrary"),
        name="modulation",
    )(cond, mod_w, mod_b.reshape(depth, 1, n))


class _Geom:
    def __init__(self, bp, lp, bs, ls):
        self.bp, self.lp, self.bs, self.ls = bp, lp, bs, ls
        self.tp = bp * lp
        self.t = bp * lp + bs * ls
        assert lp == TOKEN_TILE and ls % TOKEN_TILE == 0 and self.tp % ls == 0
        self.n_ptiles = self.tp // TOKEN_TILE
        self.tiles_per_lat = ls // TOKEN_TILE
        self.n_tiles = self.t // TOKEN_TILE

    def group(self, i):
        return jnp.where(i < self.n_ptiles, 0, 1 + (i - self.n_ptiles) // self.tiles_per_lat)

    def pos_block(self, i):
        return jnp.where(i < self.n_ptiles, 0, 1 + (i - self.n_ptiles) % self.tiles_per_lat)


def _mod_spec(geom, layer, which, d):
    def imap(i):
        return ((layer * MOD_ROWS + geom.group(i)) * N_MOD + which, 0, 0)
    return pl.BlockSpec((None, 1, d), imap)


def _stream_specs(geom, x, d):
    tm = TOKEN_TILE
    if isinstance(x, tuple):
        return ([pl.BlockSpec((tm, d), lambda i, *_: (jnp.minimum(i, geom.n_ptiles - 1), 0)),
                 pl.BlockSpec((tm, d), lambda i, *_: (jnp.maximum(i - geom.n_ptiles, 0), 0))], list(x))
    return [pl.BlockSpec((tm, d), lambda i, *_: (i, 0))], [x]


def _stream_tile(x_refs, n_ptiles):
    if len(x_refs) == 1:
        return x_refs[0][...]
    return jnp.where(pl.program_id(0) < n_ptiles, x_refs[0][...], x_refs[1][...])


def _store_kv(k, v, kb_ref, vb_ref, ks_ref, vs_ref, n_ptiles):
    kb_ref[...] = k.astype(BF16)
    vb_ref[...] = v.astype(BF16)

    @pl.when(pl.program_id(0) < n_ptiles)
    def _():
        ks_ref[...] = k
        vs_ref[...] = v


def _proj_even_kernel(*refs, n_x, n_ptiles):
    x_refs = refs[:n_x]
    (g_ref, sh_ref, sc_ref, w_ref, cos_ref, sin_ref, dft_ref,
     tc_ref, ts_ref, q_ref, kb_ref, vb_ref, ks_ref, vs_ref) = refs[n_x:]
    h = _rms(_stream_tile(x_refs, n_ptiles), g_ref[...]) * (1.0 + sc_ref[...]) + sh_ref[...]
    p = _dot(h.astype(BF16), w_ref[...])
    cos = cos_ref[...]
    sin = sin_ref[...]
    dft = dft_ref[...]
    tcs, tss = [], []
    for g in range(A_GROUPS):
        t = _dot(p[:, g * A_GROUP_DIM:(g + 1) * A_GROUP_DIM].astype(BF16), dft)
        tcs.append(t[:, :A_GROUP_DIM])
        tss.append(t[:, A_GROUP_DIM:])
    tc_ref[...] = jnp.concatenate(tcs, axis=1).astype(BF16)
    ts_ref[...] = jnp.concatenate(tss, axis=1).astype(BF16)
    o = A_WIDTH
    q = _rope(p[:, o:o + B_Q_WIDTH], cos, sin, B_HEAD_DIM // 4)
    q_ref[...] = (q * B_HEAD_DIM ** -0.5).astype(BF16)
    o += B_Q_WIDTH
    k = _rope(p[:, o:o + B_KV_WIDTH], cos, sin, B_HEAD_DIM // 4)
    o += B_KV_WIDTH
    _store_kv(k, p[:, o:o + B_KV_WIDTH], kb_ref, vb_ref, ks_ref, vs_ref, n_ptiles)


def _proj_odd_kernel(*refs, n_x, n_ptiles):
    x_refs = refs[:n_x]
    (g_ref, sh_ref, sc_ref, w_ref, cos_ref, sin_ref, qn_ref, kn_ref,
     q_ref, kb_ref, vb_ref, ks_ref, vs_ref) = refs[n_x:]
    h = _rms(_stream_tile(x_refs, n_ptiles), g_ref[...]) * (1.0 + sc_ref[...]) + sh_ref[...]
    p = _dot(h.astype(BF16), w_ref[...])
    cos = cos_ref[...]
    sin = sin_ref[...]
    q = _head_rms(p[:, :C_Q_WIDTH], qn_ref[...])
    k = _head_rms(p[:, C_Q_WIDTH:C_Q_WIDTH + C_KV_WIDTH], kn_ref[...])
    q_ref[...] = (_rope(q, cos, sin, C_HEAD_DIM // 4) * C_HEAD_DIM ** -0.5).astype(BF16)
    k = _rope(k, cos, sin, C_HEAD_DIM // 4)
    _store_kv(k, p[:, C_Q_WIDTH + C_KV_WIDTH:], kb_ref, vb_ref, ks_ref, vs_ref, n_ptiles)


def _in_projection(geom, layer, x, norm_g, mods, w, cos, sin, extras, kernel, out_widths, kv_width, name):
    t, d = geom.t, w.shape[0]
    tm = TOKEN_TILE
    n_out = w.shape[1]
    row = lambda i: (i, 0)
    const2 = lambda i: (0, 0)
    x_specs, x_args = _stream_specs(geom, x, d)
    in_specs = x_specs + [
        pl.BlockSpec((1, d), const2),
        _mod_spec(geom, layer, 0, d),
        _mod_spec(geom, layer, 1, d),
        pl.BlockSpec((d, n_out), const2),
        pl.BlockSpec((tm, LANES), lambda i: (geom.pos_block(i), 0)),
        pl.BlockSpec((tm, LANES), lambda i: (geom.pos_block(i), 0)),
    ] + [pl.BlockSpec(e.shape, const2) for e in extras]
    return pl.pallas_call(
        functools.partial(kernel, n_x=len(x_args), n_ptiles=geom.n_ptiles),
        out_shape=([jax.ShapeDtypeStruct((t, wd), BF16) for wd in out_widths]
                   + [jax.ShapeDtypeStruct((geom.tp, kv_width), F32)] * 2),
        grid=(geom.n_tiles,),
        in_specs=in_specs,
        out_specs=([pl.BlockSpec((tm, wd), row) for wd in out_widths]
                   + [pl.BlockSpec((tm, kv_width), lambda i: (jnp.minimum(i, geom.n_ptiles - 1), 0))] * 2),
        compiler_params=_cparams("arbitrary"),
        name=name,
    )(*x_args, norm_g, mods, mods, w, cos, sin, *extras)


def _fourier_kernel(cl_ref, sl_ref, tc_ref, ts_ref, *rest):
    o_ref = rest[-1]
    o_ref[...] = (_dot(cl_ref[...], tc_ref[...]) - _dot(sl_ref[...], ts_ref[...])).astype(o_ref.dtype)


def _fourier_tokens(tc, ts, cl, sl, n_seq, seq_len, row0, prev):
    t, width = tc.shape
    tr = min(seq_len, 512)
    n_r = seq_len // tr
    assert row0 % seq_len == 0
    seq0 = row0 // seq_len
    out0 = row0 // tr
    in_specs = [
        pl.BlockSpec((tr, seq_len), lambda s, r: (r, 0)),
        pl.BlockSpec((tr, seq_len), lambda s, r: (r, 0)),
        pl.BlockSpec((seq_len, width), lambda s, r: (seq0 + s, 0)),
        pl.BlockSpec((seq_len, width), lambda s, r: (seq0 + s, 0)),
    ]
    args = [cl, sl, tc, ts]
    aliases = {}
    if prev is not None:
        in_specs.append(pl.BlockSpec(memory_space=pl.ANY))
        args.append(prev)
        aliases = {4: 0}
    return pl.pallas_call(
        _fourier_kernel,
        out_shape=jax.ShapeDtypeStruct((t, width), BF16),
        grid=(n_seq, n_r),
        in_specs=in_specs,
        out_specs=pl.BlockSpec((tr, width), lambda s, r: (out0 + s * n_r + r, 0)),
        input_output_aliases=aliases,
        compiler_params=_cparams("arbitrary", "arbitrary"),
        name="fourier_tokens",
    )(*args)


def _attend(q, chunks, sink, o0, dh, den_col):
    m = sink
    acc = None
    den = None
    for k, v, mask in chunks:
        s = _dot_nt(q, k)
        if mask is not None:
            s = jnp.where(mask, s, MASKED)
        m_new = jnp.max(s, axis=-1, keepdims=True)
        if m is not None:
            m_new = jnp.maximum(m, m_new)
        p = jnp.exp(s - m_new)
        pv = _dot(p.astype(BF16), v)
        if acc is None:
            acc = pv
            if den_col is None:
                den = jnp.sum(p, axis=-1, keepdims=True)
        else:
            alpha = jnp.exp(m - m_new)
            acc = alpha * acc + pv
            if den_col is None:
                den = alpha * den + jnp.sum(p, axis=-1, keepdims=True)
        m = m_new
    if den_col is not None:
        den = acc[:, den_col:den_col + 1]
    if sink is not None:
        den = den + jnp.exp(sink - m)
    return acc[:, o0:o0 + dh] / den


def _attend_two_pass(q, chunks, sink):
    scores = []
    m = sink
    for k, _, mask in chunks:
        s = _dot_nt(q, k)
        if mask is not None:
            s = jnp.where(mask, s, MASKED)
        scores.append(s)
        mx = jnp.max(s, axis=-1, keepdims=True)
        m = mx if m is None else jnp.maximum(m, mx)
    den = None if sink is None else jnp.exp(sink - m)
    acc = None
    for (_, v, _), s in zip(chunks, scores):
        e = jnp.exp(s - m)
        es = jnp.sum(e, axis=-1, keepdims=True)
        den = es if den is None else den + es
        o = _dot(e.astype(BF16), v)
        acc = o if acc is None else acc + o
    return acc / den


def _stack_heads(q, kv, groups, dh):
    return jnp.concatenate([q[:, (kv * groups + g) * dh:(kv * groups + g + 1) * dh] for g in range(groups)], axis=0)


def _sink_column(sink_ref, kv, groups, rows):
    return jnp.concatenate([jnp.full((rows, 1), sink_ref[kv * groups + g], F32) for g in range(groups)], axis=0)


def _head_values(v, kv, dh, with_ones):
    if not with_ones:
        return v[:, kv * dh:(kv + 1) * dh], 0, None
    assert dh == LANES
    lane = lax.broadcasted_iota(I32, (v.shape[0], LANES), 1)
    ones = jnp.where(lane == 0, 1.0, 0.0).astype(BF16)
    return jnp.concatenate([v[:, kv * dh:(kv + 1) * dh], ones], axis=1), 0, dh


def _attn_kernel(*refs, kv_heads, groups, dh, has_sink, has_ctx, window, q_tile, seq_len, chunk):
    refs = list(refs)
    sink_ref = refs.pop(0) if has_sink else None
    q_ref, k_ref, v_ref = refs[:3]
    ck_ref, cv_ref = (refs[3], refs[4]) if has_ctx else (None, None)
    o_ref = refs[-1]
    q = q_ref[...]
    rows = q.shape[0]
    if window is None:
        spans = [(c * chunk, chunk) for c in range(seq_len // chunk)]
        mask = None
    else:
        n = pl.program_id(1)
        band = q_tile + 2 * window
        start = pl.multiple_of(jnp.clip(n * q_tile - window, 0, seq_len - band), LANES)
        spans = [(start, band)]
        qpos = n * q_tile + lax.broadcasted_iota(I32, (groups * rows, band), 0) % rows
        kpos = start + lax.broadcasted_iota(I32, (groups * rows, band), 1)
        mask = jnp.abs(kpos - qpos) <= window
    sources = [(k_ref[pl.ds(s0, n_s), :], v_ref[pl.ds(s0, n_s), :], mask) for s0, n_s in spans]
    if has_ctx:
        sources.append((ck_ref[...], cv_ref[...], None))
    outs = []
    for kv in range(kv_heads):
        online = dh == LANES and len(sources) > 1
        chunks = []
        for k, v, msk in sources:
            vh, o0, den_col = _head_values(v, kv, dh, with_ones=online)
            chunks.append((k[:, kv * dh:(kv + 1) * dh], vh, msk))
        sink = _sink_column(sink_ref, kv, groups, rows) if has_sink else None
        qh = _stack_heads(q, kv, groups, dh)
        o = _attend(qh, chunks, sink, o0, dh, den_col) if online else _attend_two_pass(qh, chunks, sink)
        outs.extend(o[g * rows:(g + 1) * rows] for g in range(groups))
    o_ref[...] = jnp.concatenate(outs, axis=1).astype(o_ref.dtype)


def _attention(q, k, v, ctx, sink, prev, *, n_seq, seq_len, row0, q_tile, kv_heads, groups, dh, window):
    t, qw = q.shape
    kw = k.shape[1]
    n_q = seq_len // q_tile
    assert row0 % seq_len == 0 and row0 % q_tile == 0
    seq0 = row0 // seq_len
    q0 = row0 // q_tile
    in_specs, args = [], []
    if sink is not None:
        in_specs.append(pl.BlockSpec(memory_space=pltpu.SMEM))
        args.append(sink)
    in_specs += [
        pl.BlockSpec((q_tile, qw), lambda s, n: (q0 + s * n_q + n, 0)),
        pl.BlockSpec((seq_len, kw), lambda s, n: (seq0 + s, 0)),
        pl.BlockSpec((seq_len, kw), lambda s, n: (seq0 + s, 0)),
    ]
    args += [q, k, v]
    if ctx is not None:
        p = ctx[0].shape[1]
        in_specs += [pl.BlockSpec((None, p, kw), lambda s, n: (s, 0, 0))] * 2
        args += list(ctx)
    aliases = {}
    if prev is not None:
        in_specs.append(pl.BlockSpec(memory_space=pl.ANY))
        aliases = {len(args): 0}
        args.append(prev)
    kern = functools.partial(
        _attn_kernel, kv_heads=kv_heads, groups=groups, dh=dh, has_sink=sink is not None,
        has_ctx=ctx is not None, window=window, q_tile=q_tile, seq_len=seq_len, chunk=min(seq_len, ATTN_KEY_CHUNK))
    return pl.pallas_call(
        kern,
        out_shape=jax.ShapeDtypeStruct((t, qw), BF16),
        grid=(n_seq, n_q),
        in_specs=in_specs,
        out_specs=pl.BlockSpec((q_tile, qw), lambda s, n: (q0 + s * n_q + n, 0)),
        input_output_aliases=aliases,
        compiler_params=_cparams("arbitrary", "arbitrary"),
        name="attention",
    )(*args)


def _out_route_kernel(*refs, n_mix, n_x, n_ptiles):
    mix_refs = refs[:n_mix]
    w_refs = refs[n_mix:2 * n_mix]
    x_refs = refs[2 * n_mix:2 * n_mix + n_x]
    (gate_ref, g2_ref, sh2_ref, sc2_ref, rw_ref, rb_ref,
     xo_ref, h_ref, pos_ref, gates_ref, rows_ref, off_ref) = refs[2 * n_mix + n_x:]

    acc = None
    for m_ref, w_ref in zip(mix_refs, w_refs):
        part = _dot(m_ref[...], w_ref[...])
        acc = part if acc is None else acc + part
    xn = _stream_tile(x_refs, n_ptiles) + gate_ref[...] * acc
    xo_ref[...] = xn
    h = _rms(xn, g2_ref[...]) * (1.0 + sc2_ref[...]) + sh2_ref[...]
    hb = h.astype(BF16)
    h_ref[...] = hb

    logits = _dot(hb, rw_ref[...]) + rb_ref[...]
    tm, ne = logits.shape
    lane = lax.broadcasted_iota(I32, (tm, ne), 1).astype(F32)
    lane4 = lax.broadcasted_iota(I32, (tm, TOP_K), 1)
    work = logits
    sels, vals = [], []
    for _ in range(TOP_K):
        mx = jnp.max(work, axis=-1, keepdims=True)
        first = jnp.min(jnp.where(work == mx, lane, float(ne)), axis=-1, keepdims=True)
        sel = lane == first
        work = jnp.where(sel, -jnp.inf, work)
        sels.append(sel)
        vals.append(mx)
    exps = [jnp.exp(v - vals[0]) for v in vals]
    den = exps[0] + exps[1] + exps[2] + exps[3]

    onehot = jnp.zeros((tm, ne), F32)
    for sel in sels:
        onehot = onehot + sel.astype(F32)
    r_i = lax.broadcasted_iota(I32, (tm, tm), 0)
    c_i = lax.broadcasted_iota(I32, (tm, tm), 1)
    before = jnp.where(c_i < r_i, 1.0, 0.0).astype(BF16)
    earlier = _dot(before, onehot.astype(BF16))

    cnt = jnp.sum(onehot, axis=0, keepdims=True)
    seg8 = jnp.floor((cnt + (SEG_ALIGN - 1.0)) * (1.0 / SEG_ALIGN))
    e_r = lax.broadcasted_iota(I32, (ne, ne), 0)
    e_c = lax.broadcasted_iota(I32, (ne, ne), 1)
    upper = jnp.where(e_r < e_c, 1.0, 0.0).astype(BF16)
    off8 = _dot(jnp.broadcast_to(seg8, (SEG_ALIGN, ne)).astype(BF16), upper)[0:1]
    seg_off = off8 * SEG_ALIGN
    base = seg_off + earlier

    pos_o = jnp.zeros((tm, TOP_K), I32)
    gate_o = jnp.zeros((tm, TOP_K), F32)
    for k in range(TOP_K):
        pos_k = jnp.sum(jnp.where(sels[k], base, 0.0), axis=-1, keepdims=True)
        pos_o = jnp.where(lane4 == k, pos_k.astype(I32), pos_o)
        gate_o = jnp.where(lane4 == k, exps[k] / den, gate_o)
    pos_ref[...] = pos_o
    gates_ref[...] = gate_o
    rows_ref[0] = (seg8 * SEG_ALIGN).astype(I32)
    off_ref[0] = seg_off.astype(I32)


def _out_route(geom, layer, mixes, w_parts, x, mods, norm_g, router_w, router_b):
    t, d = geom.t, w_parts[0].shape[1]
    tm = TOKEN_TILE
    row = lambda i: (i, 0)
    const2 = lambda i: (0, 0)
    x_specs, x_args = _stream_specs(geom, x, d)
    in_specs = [pl.BlockSpec((tm, m.shape[1]), row) for m in mixes]
    in_specs += [pl.BlockSpec(w.shape, const2) for w in w_parts]
    in_specs += x_specs
    in_specs += [
        _mod_spec(geom, layer, 2, d),
        pl.BlockSpec((1, d), const2),
        _mod_spec(geom, layer, 3, d),
        _mod_spec(geom, layer, 4, d),
        pl.BlockSpec(router_w.shape, const2),
        pl.BlockSpec((1, N_EXPERTS), const2),
    ]
    seg3 = lambda i: (i, 0, 0)
    out_shape = [
        jax.ShapeDtypeStruct((t, d), F32),
        jax.ShapeDtypeStruct((t, d), BF16),
        jax.ShapeDtypeStruct((t, TOP_K), I32),
        jax.ShapeDtypeStruct((t, TOP_K), F32),
        jax.ShapeDtypeStruct((geom.n_tiles, 1, N_EXPERTS), I32),
        jax.ShapeDtypeStruct((geom.n_tiles, 1, N_EXPERTS), I32),
    ]
    out_specs = [
        pl.BlockSpec((tm, d), row),
        pl.BlockSpec((tm, d), row),
        pl.BlockSpec((tm, TOP_K), row),
        pl.BlockSpec((tm, TOP_K), row),
        pl.BlockSpec((1, 1, N_EXPERTS), seg3),
        pl.BlockSpec((1, 1, N_EXPERTS), seg3),
    ]
    return pl.pallas_call(
        functools.partial(_out_route_kernel, n_mix=len(mixes), n_x=len(x_args), n_ptiles=geom.n_ptiles),
        out_shape=out_shape,
        grid=(geom.n_tiles,),
        in_specs=in_specs,
        out_specs=out_specs,
        compiler_params=_cparams("arbitrary"),
        name="out_route",
    )(*mixes, *w_parts, *x_args, mods, norm_g, mods, mods, router_w, router_b)


def _pack_pairs(v):
    n = v.shape[1] // 2
    bits = lax.bitcast_convert_type(v, U32)
    return (bits[:, :n] & jnp.uint32(0xFFFF0000)) | (bits[:, n:] >> 16)


def _unpack_pairs(p):
    hi = lax.bitcast_convert_type(p & jnp.uint32(0xFFFF0000), F32)
    lo = lax.bitcast_convert_type(p << 16, F32)
    return jnp.concatenate([hi, lo], axis=1).astype(BF16)


def _planned_copies(i, plan_refs, make_copy):
    for size, width, (local_ref, slot_ref, count_ref) in zip(COPY_SIZES, PLAN_WIDTHS, plan_refs):
        def one(c, carry, size=size, width=width, local_ref=local_ref, slot_ref=slot_ref):
            a = i * width + c
            make_copy(pl.multiple_of(local_ref[a], SEG_ALIGN), pl.multiple_of(slot_ref[a], SEG_ALIGN), size).start()
            return carry

        lax.fori_loop(0, count_ref[i], one, 0)


def _copy_plan(rows, seg_off, seg_dst):
    plan = []
    experts = jnp.arange(N_EXPERTS, dtype=I32)
    for n, (size, width) in enumerate(zip(COPY_SIZES, PLAN_WIDTHS)):
        if n == 0:
            count, done = rows // size, jnp.zeros_like(rows)
        else:
            count, done = (rows % (2 * size)) // size, rows - rows % (2 * size)
        cum = jnp.cumsum(count, axis=1)
        j = jnp.arange(width, dtype=I32)
        owner = jnp.sum((cum[:, None, :] <= j[None, :, None]).astype(I32), axis=2)
        pick = (jnp.minimum(owner, N_EXPERTS - 1)[:, :, None] == experts[None, None, :]).astype(I32)
        take = lambda v: jnp.sum(pick * v[:, None, :], axis=2)
        within = (j[None, :] - take(cum - count)) * size
        plan += [(take(seg_off + done) + within).reshape(-1).astype(I32),
                 (take(seg_dst + done) + within).reshape(-1).astype(I32),
                 cum[:, -1].astype(I32)]
    return plan


def _wait_copies(n_rows, make_copy):
    def wait_big(c, carry):
        make_copy(0, 0, WAIT_CHUNK).wait()
        return carry
    lax.fori_loop(0, n_rows // WAIT_CHUNK, wait_big, 0)
    size = WAIT_CHUNK // 2
    while size >= SEG_ALIGN:
        @pl.when(n_rows % (2 * size) >= size)
        def _(size=size):
            make_copy(0, 0, size).wait()

        size //= 2


def _dispatch_kernel(*refs):
    plan_refs, (tot_ref, h_ref, pos_ref, xs_ref, sorted_ref, sem) = _split_plan(refs)
    i = pl.program_id(0)
    buf = i % 2
    tm = h_ref.shape[0]
    n_sorted = sorted_ref.shape[1]
    pos = pos_ref[...]
    lane = lax.broadcasted_iota(I32, (tm, n_sorted), 1)
    hit = jnp.zeros((tm, n_sorted), F32)
    for k in range(TOP_K):
        hit = jnp.where(pos[:, k:k + 1] == lane, 1.0, hit)
    sorted_ref[buf] = _pack_pairs(lax.dot_general(hit.astype(BF16), h_ref[...], (((0,), (0,)), ((), ())),
                                                  preferred_element_type=F32))

    def copies_from(b):
        def make_copy(local, slot, rows):
            return pltpu.make_async_copy(sorted_ref.at[b, pl.ds(local, rows)], xs_ref.at[pl.ds(slot, rows)],
                                         sem.at[b])
        return make_copy

    _planned_copies(i, plan_refs, copies_from(buf))

    @pl.when(i > 0)
    def _():
        _wait_copies(tot_ref[jnp.maximum(i - 1, 0)], copies_from(1 - buf))

    @pl.when(i == pl.num_programs(0) - 1)
    def _():
        _wait_copies(tot_ref[i], copies_from(buf))


def _split_plan(refs):
    n = 3 * len(COPY_SIZES)
    return [refs[k:k + 3] for k in range(0, n, 3)], refs[n:]


def _dispatch(h, pos, plan, tile_rows, n_slots):
    t, d = h.shape
    tm = TOKEN_TILE
    return pl.pallas_call(
        _dispatch_kernel,
        out_shape=jax.ShapeDtypeStruct((n_slots, d // 2), U32),
        grid_spec=pltpu.PrefetchScalarGridSpec(
            num_scalar_prefetch=len(plan) + 1,
            grid=(t // tm,),
            in_specs=[
                pl.BlockSpec((tm, d), lambda i, *_: (i, 0)),
                pl.BlockSpec((tm, TOP_K), lambda i, *_: (i, 0)),
            ],
            out_specs=pl.BlockSpec(memory_space=pl.ANY),
            scratch_shapes=[pltpu.VMEM((2, SORTED_ROWS, d // 2), U32), pltpu.SemaphoreType.DMA((2,))],
        ),
        compiler_params=_cparams("arbitrary"),
        name="moe_dispatch",
    )(*plan, tile_rows, h, pos)


def _ffn_kernel(rows_ref, start_ref, wgu_ref, bgu_ref, wd_ref, bd_ref, xs_ref, ys_ref,
                wgu_bf, wd_bf, xbuf, ybuf, sem_in, sem_out):
    e = pl.program_id(0)
    d_ff = wd_ref.shape[1]
    tb = xbuf.shape[1]
    n_rows = rows_ref[e]
    n_tiles = (n_rows + tb - 1) // tb
    base = start_ref[e]

    chunk = 128
    def cast_gu(c, carry):
        r = pl.multiple_of(c * chunk, chunk)
        wgu_bf[pl.ds(r, chunk), :] = wgu_ref[0, pl.ds(r, chunk), :].astype(BF16)
        return carry
    lax.fori_loop(0, wgu_ref.shape[1] // chunk, cast_gu, 0)
    def cast_d(c, carry):
        r = pl.multiple_of(c * chunk, chunk)
        wd_bf[pl.ds(r, chunk), :] = wd_ref[0, pl.ds(r, chunk), :].astype(BF16)
        return carry
    lax.fori_loop(0, d_ff // chunk, cast_d, 0)

    def x_copy(s, slot):
        r = pl.multiple_of(base + s * tb, EXPERT_TILE)
        return pltpu.make_async_copy(xs_ref.at[pl.ds(r, tb)], xbuf.at[slot], sem_in.at[slot])

    def y_copy(s, slot):
        r = pl.multiple_of(base + s * tb, EXPERT_TILE)
        return pltpu.make_async_copy(ybuf.at[slot], ys_ref.at[pl.ds(r, tb)], sem_out.at[slot])

    @pl.when(n_tiles > 0)
    def _():
        x_copy(0, 0).start()

    def tile(s, carry):
        slot = s % 2
        x_copy(s, slot).wait()

        @pl.when(s + 1 < n_tiles)
        def _():
            x_copy(s + 1, 1 - slot).start()

        @pl.when(s >= 2)
        def _():
            y_copy(s - 2, slot).wait()

        rows = s * tb + lax.broadcasted_iota(I32, (tb, 1), 0)
        x = _unpack_pairs(jnp.where(rows < n_rows, xbuf[slot], jnp.uint32(0)))
        gu = _dot(x, wgu_bf[...]) + bgu_ref[0]
        gate = jnp.minimum(gu[:, :d_ff], SWIGLU_LIMIT)
        up = jnp.clip(gu[:, d_ff:], -SWIGLU_LIMIT, SWIGLU_LIMIT)
        act = (up + 1.0) * (gate * (1.0 / (1.0 + jnp.exp(-SWIGLU_ALPHA * gate))))
        y = _dot(act.astype(BF16), wd_bf[...]) + bd_ref[0]
        ybuf[slot] = _pack_pairs(y.astype(BF16).astype(F32))
        y_copy(s, slot).start()
        return carry

    lax.fori_loop(0, n_tiles, tile, 0)

    @pl.when(n_tiles >= 2)
    def _():
        y_copy(n_tiles - 2, n_tiles % 2).wait()

    @pl.when(n_tiles >= 1)
    def _():
        y_copy(n_tiles - 1, (n_tiles - 1) % 2).wait()


def _expert_ffn(layer, xs, expert_rows, expert_start, w_gu, b_gu, w_down, b_down):
    n_slots, packed_w = xs.shape
    tb = FFN_TILE
    depth, ne, d, two_f = w_gu.shape
    d_ff = two_f // 2
    exp4 = lambda e, *_: (layer, e, 0, 0)
    return pl.pallas_call(
        _ffn_kernel,
        out_shape=jax.ShapeDtypeStruct((n_slots, packed_w), U32),
        grid_spec=pltpu.PrefetchScalarGridSpec(
            num_scalar_prefetch=2,
            grid=(ne,),
            in_specs=[
                pl.BlockSpec((None, 1, d, two_f), exp4),
                pl.BlockSpec((None, 1, 1, two_f), exp4),
                pl.BlockSpec((None, 1, d_ff, d), exp4),
                pl.BlockSpec((None, 1, 1, d), exp4),
                pl.BlockSpec(memory_space=pl.ANY),
            ],
            out_specs=pl.BlockSpec(memory_space=pl.ANY),
            scratch_shapes=[
                pltpu.VMEM((d, two_f), BF16), pltpu.VMEM((d_ff, d), BF16),
                pltpu.VMEM((2, tb, packed_w), U32), pltpu.VMEM((2, tb, packed_w), U32),
                pltpu.SemaphoreType.DMA((2,)), pltpu.SemaphoreType.DMA((2,)),
            ],
        ),
        compiler_params=_cparams("arbitrary"),
        name="expert_ffn",
    )(expert_rows, expert_start, w_gu, b_gu.reshape(depth, ne, 1, two_f),
      w_down, b_down.reshape(depth, ne, 1, d), xs)


def _combine_kernel(*refs, final, n_ptiles):
    plan_refs, (tot_ref, x_ref, pos_ref, gates_ref, mg_ref, *rest) = _split_plan(refs)
    if final:
        fn_ref, ys_ref, op_ref, os_ref, buf, sem = rest
    else:
        ys_ref, o_ref, buf, sem = rest
    i = pl.program_id(0)
    cur = i % 2
    tm = x_ref.shape[0]
    n_sorted = buf.shape[1]

    def copies_into(b):
        def make_copy(local, slot, rows):
            return pltpu.make_async_copy(ys_ref.at[pl.ds(slot, rows)], buf.at[b, pl.ds(local, rows)], sem.at[b])
        return make_copy

    @pl.when(i == 0)
    def _():
        buf[...] = jnp.zeros_like(buf)
        _planned_copies(i, plan_refs, copies_into(cur))

    @pl.when(i + 1 < pl.num_programs(0))
    def _():
        _planned_copies(i + 1, plan_refs, copies_into(1 - cur))

    pos = pos_ref[...]
    g = gates_ref[...]
    lane = lax.broadcasted_iota(I32, (tm, n_sorted), 1)
    weight = jnp.zeros((tm, n_sorted), F32)
    for k in range(TOP_K):
        weight = jnp.where(pos[:, k:k + 1] == lane, g[:, k:k + 1], weight)
    _wait_copies(tot_ref[i], copies_into(cur))
    y = _dot(weight.astype(BF16), _unpack_pairs(buf[cur]))
    xn = x_ref[...] + mg_ref[...] * y
    if final:
        xn = _rms(xn, fn_ref[...])

        @pl.when(i < n_ptiles)
        def _():
            op_ref[...] = xn

        @pl.when(i >= n_ptiles)
        def _():
            os_ref[...] = xn
    else:
        o_ref[...] = xn


def _combine(geom, layer, ys, pos, plan, tile_rows, x, gates, mods, final_g):
    t, d = x.shape
    tm = TOKEN_TILE
    final = final_g is not None

    def mod_imap(i, *_):
        return ((layer * MOD_ROWS + geom.group(i)) * N_MOD + 5, 0, 0)

    row = lambda i, *_: (i, 0)
    in_specs = [
        pl.BlockSpec((tm, d), row),
        pl.BlockSpec((tm, TOP_K), row),
        pl.BlockSpec((tm, TOP_K), row),
        pl.BlockSpec((None, 1, d), mod_imap),
    ]
    args = [*plan, tile_rows, x, pos, gates, mods]
    if final:
        in_specs.append(pl.BlockSpec((1, d), lambda i, *_: (0, 0)))
        args.append(final_g)
    in_specs.append(pl.BlockSpec(memory_space=pl.ANY))
    args.append(ys)
    if final:
        n_pt = geom.n_ptiles
        out_shape = [jax.ShapeDtypeStruct((geom.tp, d), F32), jax.ShapeDtypeStruct((t - geom.tp, d), F32)]
        out_specs = [pl.BlockSpec((tm, d), lambda i, *_: (jnp.minimum(i, n_pt - 1), 0)),
                     pl.BlockSpec((tm, d), lambda i, *_: (jnp.maximum(i - n_pt, 0), 0))]
    else:
        out_shape = jax.ShapeDtypeStruct((t, d), F32)
        out_specs = pl.BlockSpec((tm, d), row)
    return pl.pallas_call(
        functools.partial(_combine_kernel, final=final, n_ptiles=geom.n_ptiles),
        out_shape=out_shape,
        grid_spec=pltpu.PrefetchScalarGridSpec(
            num_scalar_prefetch=len(plan) + 1,
            grid=(t // tm,),
            in_specs=in_specs,
            out_specs=out_specs,
            scratch_shapes=[pltpu.VMEM((2, SORTED_ROWS, d // 2), U32), pltpu.SemaphoreType.DMA((2,))],
        ),
        compiler_params=_cparams("arbitrary"),
        name="moe_combine",
    )(*args)


def _moe(geom, layer, h, pos, gates, seg_rows, seg_off, x, mods, w_gu, b_gu, w_down, b_down, final_g):
    t = h.shape[0]
    tb = EXPERT_TILE
    n_tok_tiles = seg_rows.shape[0]
    max_rows = t * TOP_K + n_tok_tiles * N_EXPERTS * (SEG_ALIGN - 1)
    n_blocks = -(-max_rows // tb) + N_EXPERTS
    rows = seg_rows[:, 0, :]
    cnt = jnp.sum(rows, axis=0)
    n_tiles_e = (cnt + tb - 1) // tb
    tile_end = jnp.cumsum(n_tiles_e)
    tile_start = tile_end - n_tiles_e
    expert_start = (tile_start * tb).astype(I32)
    seg_dst = expert_start[None, :] + jnp.cumsum(rows, axis=0) - rows
    plan = _copy_plan(rows, seg_off[:, 0, :], seg_dst)
    tile_rows = jnp.sum(rows, axis=1).astype(I32)
    xs = _dispatch(h, pos, plan, tile_rows, n_blocks * tb + FFN_TILE - tb)
    ys = _expert_ffn(layer, xs, cnt.astype(I32), expert_start, w_gu, b_gu, w_down, b_down)
    return _combine(geom, layer, ys, pos, plan, tile_rows, x, gates, mods, final_g)


def kernel(x_prompt, x_sample, cache_b_k, cache_b_v, cache_c_k, cache_c_v, c, c_ctx,
           mod_w, mod_b, norm_mix, norm_ffn, even_w_in, even_w_out, even_sink,
           odd_w_in, odd_w_out, odd_q_norm, odd_k_norm, router_w, router_b,
           moe_w_gu, moe_b_gu, moe_w_down, moe_b_down, final_norm):
    bp, lp, d = x_prompt.shape
    bs, ls, _ = x_sample.shape
    past = cache_b_k.shape[2]
    depth = mod_w.shape[0]
    geom = _Geom(bp, lp, bs, ls)
    tp = geom.tp

    x = (x_prompt.reshape(tp, d), x_sample.reshape(bs * ls, d))
    cond = jnp.concatenate([c_ctx[None, :], c, jnp.zeros((MOD_ROWS - 1 - bs, d), F32)], axis=0)
    mods = _modulation(cond, mod_w, mod_b).reshape(depth * MOD_ROWS * N_MOD, 1, d)

    cn, sn = _dft_tables(A_GROUP_DIM)
    dft_chan = jnp.asarray(np.concatenate([cn, sn], axis=1), BF16)
    dft_p = [jnp.asarray(m, BF16) for m in _dft_tables(lp)]
    dft_s = [jnp.asarray(m, BF16) for m in _dft_tables(ls)]
    rope_b = [jnp.asarray(m) for m in _rope_tables(TOKEN_TILE, ls, B_HEAD_DIM)]
    rope_c = [jnp.asarray(m) for m in _rope_tables(TOKEN_TILE, ls, C_HEAD_DIM)]

    states = {"bk": [], "bv": [], "ck": [], "cv": []}
    for layer in range(depth):
        j = layer // 2
        g_mix = norm_mix[layer][None, :]
        g_ffn = norm_ffn[layer][None, :]
        if layer % 2 == 0:
            tc, ts, q, k, v, k_state, v_state = _in_projection(
                geom, layer, x, g_mix, mods, even_w_in[j].astype(BF16), rope_b[0], rope_b[1], [dft_chan],
                _proj_even_kernel, (A_WIDTH, A_WIDTH, B_Q_WIDTH, B_KV_WIDTH, B_KV_WIDTH), B_KV_WIDTH, "proj_even")
            states["bk"].append(k_state.reshape(bp, lp, B_KV_HEADS, B_HEAD_DIM))
            states["bv"].append(v_state.reshape(bp, lp, B_KV_HEADS, B_HEAD_DIM))
            four = _fourier_tokens(tc, ts, dft_p[0], dft_p[1], bp, lp, 0, None)
            four = _fourier_tokens(tc, ts, dft_s[0], dft_s[1], bs, ls, tp, four)
            sink = even_sink[j]
            common = dict(kv_heads=B_KV_HEADS, groups=B_HEADS // B_KV_HEADS, dh=B_HEAD_DIM)
            att = _attention(q, k, v, None, sink, None, n_seq=bp, seq_len=lp, row0=0, q_tile=lp,
                             window=None, **common)
            ctx = (cache_b_k[:, j].reshape(bs, past, B_KV_WIDTH).astype(BF16),
                   cache_b_v[:, j].reshape(bs, past, B_KV_WIDTH).astype(BF16))
            att = _attention(q, k, v, ctx, sink, att, n_seq=bs, seq_len=ls, row0=tp, q_tile=ATTN_Q_TILE,
                             window=WINDOW, **common)
            w_out = even_w_out[j].astype(BF16)
            mixes = [four, att]
            w_parts = [w_out[:A_WIDTH], w_out[A_WIDTH:]]
        else:
            q, k, v, k_state, v_state = _in_projection(
                geom, layer, x, g_mix, mods, odd_w_in[j].astype(BF16), rope_c[0], rope_c[1],
                [odd_q_norm[j][None, :], odd_k_norm[j][None, :]],
                _proj_odd_kernel, (C_Q_WIDTH, C_KV_WIDTH, C_KV_WIDTH), C_KV_WIDTH, "proj_odd")
            states["ck"].append(k_state.reshape(bp, lp, C_KV_HEADS, C_HEAD_DIM))
            states["cv"].append(v_state.reshape(bp, lp, C_KV_HEADS, C_HEAD_DIM))
            common = dict(kv_heads=C_KV_HEADS, groups=C_HEADS // C_KV_HEADS, dh=C_HEAD_DIM, window=None)
            att = _attention(q, k, v, None, None, None, n_seq=bp, seq_len=lp, row0=0, q_tile=lp, **common)
            ctx = (cache_c_k[:, j].reshape(bs, past, C_KV_WIDTH).astype(BF16),
                   cache_c_v[:, j].reshape(bs, past, C_KV_WIDTH).astype(BF16))
            att = _attention(q, k, v, ctx, None, att, n_seq=bs, seq_len=ls, row0=tp, q_tile=ATTN_Q_TILE, **common)
            mixes = [att]
            w_parts = [odd_w_out[j].astype(BF16)]
        x, h, pos, gates, seg_rows, seg_off = _out_route(
            geom, layer, mixes, w_parts, x, mods, g_ffn, router_w[layer].astype(BF16), router_b[layer][None, :])
        final_g = final_norm[None, :] if layer == depth - 1 else None
        x = _moe(geom, layer, h, pos, gates, seg_rows, seg_off, x, mods,
                 moe_w_gu, moe_b_gu, moe_w_down, moe_b_down, final_g)

    y_prompt = x[0].reshape(bp, lp, d)
    y_sample = x[1].reshape(bs, ls, d)
    return (y_prompt, y_sample,
            jnp.stack(states["bk"], axis=1), jnp.stack(states["bv"], axis=1),
            jnp.stack(states["ck"], axis=1), jnp.stack(states["cv"], axis=1))
```

```python
import functools

import numpy as np
import jax
import jax.numpy as jnp
from jax import lax
from jax.experimental import pallas as pl
from jax.experimental.pallas import tpu as pltpu

F32 = jnp.float32
BF16 = jnp.bfloat16
I32 = jnp.int32
U32 = jnp.uint32

GRID_W = 64
A_GROUPS = 4
A_GROUP_DIM = 128
A_WIDTH = A_GROUPS * A_GROUP_DIM
B_HEADS = 8
B_KV_HEADS = 2
B_HEAD_DIM = 64
B_Q_WIDTH = B_HEADS * B_HEAD_DIM
B_KV_WIDTH = B_KV_HEADS * B_HEAD_DIM
WINDOW = 128
C_HEADS = 8
C_KV_HEADS = 2
C_HEAD_DIM = 128
C_Q_WIDTH = C_HEADS * C_HEAD_DIM
C_KV_WIDTH = C_KV_HEADS * C_HEAD_DIM
ROPE_THETA = 10000.0
N_EXPERTS = 32
TOP_K = 4
SWIGLU_LIMIT = 7.0
SWIGLU_ALPHA = 1.702
EPS = 1e-6

LANES = 128
TOKEN_TILE = 256
EXPERT_TILE = 128
FFN_TILE = 256
ATTN_Q_TILE = 128
ATTN_KEY_CHUNK = 512
SEG_ALIGN = 8
COPY_SIZES = (32, 16, 8)
WAIT_CHUNK = 256
SORTED_ROWS = -(-(TOKEN_TILE * TOP_K + N_EXPERTS * (SEG_ALIGN - 1)) // LANES) * LANES
PLAN_WIDTHS = (SORTED_ROWS // COPY_SIZES[0],) + (N_EXPERTS,) * (len(COPY_SIZES) - 1)
VMEM_LIMIT = 56 * 1024 * 1024
MASKED = -1e30
N_MOD = 6
MOD_ROWS = 8


def _cparams(*sem):
    return pltpu.CompilerParams(dimension_semantics=tuple(sem), vmem_limit_bytes=VMEM_LIMIT)


def _dot(a, b):
    return jnp.dot(a, b, preferred_element_type=F32)


def _dot_nt(a, b):
    return lax.dot_general(a, b, (((1,), (1,)), ((), ())), preferred_element_type=F32)


def _rms(x, g):
    return x * lax.rsqrt(jnp.mean(x * x, axis=-1, keepdims=True) + EPS) * g


def _dft_tables(n):
    j = np.arange(n, dtype=np.int64)
    ang = 2.0 * np.pi * ((j[:, None] * j[None, :]) % n).astype(np.float64) / n
    s = 1.0 / np.sqrt(n)
    return np.cos(ang) * s, np.sin(ang) * s


def _paired_head_order(kv_heads, groups, dh):
    assert kv_heads * dh == LANES
    return np.array([(kv * groups + g) * dh + d for g in range(groups) for kv in range(kv_heads) for d in range(dh)])


def _rope_tables(n_prompt_rows, n_latent, head_dim):
    quarter = head_dim // 4
    pos = np.arange(n_latent)
    row = (pos // GRID_W).astype(np.float32)
    col = (pos % GRID_W).astype(np.float32)
    inv = (np.float32(ROPE_THETA) ** (-np.arange(quarter, dtype=np.float32) / np.float32(quarter))).astype(np.float32)
    ang_row = (row[:, None] * inv[None, :]).astype(np.float32)
    ang_col = (col[:, None] * inv[None, :]).astype(np.float32)
    cos_h = np.concatenate([np.cos(ang_row)] * 2 + [np.cos(ang_col)] * 2, axis=1)
    sin_h = np.concatenate([-np.sin(ang_row), np.sin(ang_row), -np.sin(ang_col), np.sin(ang_col)], axis=1)
    reps = LANES // head_dim
    cos_l = np.tile(cos_h, (1, reps)).astype(np.float32)
    sin_l = np.tile(sin_h, (1, reps)).astype(np.float32)
    cos = np.concatenate([np.ones((n_prompt_rows, LANES), np.float32), cos_l], axis=0)
    sin = np.concatenate([np.zeros((n_prompt_rows, LANES), np.float32), sin_l], axis=0)
    return cos, sin


def _rope(x, cos, sin, quarter):
    lane = lax.broadcasted_iota(I32, (x.shape[0], LANES), 1)
    first = ((lane // quarter) % 2) == 0
    outs = []
    for c in range(x.shape[1] // LANES):
        xc = x[:, c * LANES:(c + 1) * LANES]
        partner = jnp.where(first, pltpu.roll(xc, LANES - quarter, 1), pltpu.roll(xc, quarter, 1))
        outs.append(xc * cos + partner * sin)
    return outs[0] if len(outs) == 1 else jnp.concatenate(outs, axis=1)


def _head_rms(x, g):
    outs = []
    for c in range(x.shape[1] // LANES):
        outs.append(_rms(x[:, c * LANES:(c + 1) * LANES], g))
    return outs[0] if len(outs) == 1 else jnp.concatenate(outs, axis=1)


def _mod_kernel(c_ref, w_ref, b_ref, o_ref):
    c = c_ref[...]
    s = c * (1.0 / (1.0 + jnp.exp(-c)))
    o_ref[0] = _dot(s.astype(BF16), w_ref[0].astype(BF16)) + b_ref[0]


def _modulation(cond, mod_w, mod_b):
    depth, d, n = mod_w.shape
    tn = 1536
    return pl.pallas_call(
        _mod_kernel,
        out_shape=jax.ShapeDtypeStruct((depth, MOD_ROWS, n), F32),
        grid=(depth, n // tn),
        in_specs=[
            pl.BlockSpec((MOD_ROWS, d), lambda l, j: (0, 0)),
            pl.BlockSpec((1, d, tn), lambda l, j: (l, 0, j)),
            pl.BlockSpec((1, 1, tn), lambda l, j: (l, 0, j)),
        ],
        out_specs=pl.BlockSpec((1, MOD_ROWS, tn), lambda l, j: (l, 0, j)),
        compiler_params=_cparams("arbitrary", "arbitrary"),
        name="modulation",
    )(cond, mod_w, mod_b.reshape(depth, 1, n))


class _Geom:
    def __init__(self, bp, lp, bs, ls):
        self.bp, self.lp, self.bs, self.ls = bp, lp, bs, ls
        self.tp = bp * lp
        self.t = bp * lp + bs * ls
        assert lp == TOKEN_TILE and ls % TOKEN_TILE == 0 and self.tp % ls == 0
        self.n_ptiles = self.tp // TOKEN_TILE
        self.tiles_per_lat = ls // TOKEN_TILE
        self.n_tiles = self.t // TOKEN_TILE

    def group(self, i):
        return jnp.where(i < self.n_ptiles, 0, 1 + (i - self.n_ptiles) // self.tiles_per_lat)

    def pos_block(self, i):
        return jnp.where(i < self.n_ptiles, 0, 1 + (i - self.n_ptiles) % self.tiles_per_lat)


def _mod_spec(geom, layer, which, d):
    def imap(i):
        return ((layer * MOD_ROWS + geom.group(i)) * N_MOD + which, 0, 0)
    return pl.BlockSpec((None, 1, d), imap)


def _stream_specs(geom, x, d):
    tm = TOKEN_TILE
    if isinstance(x, tuple):
        return ([pl.BlockSpec((tm, d), lambda i, *_: (jnp.minimum(i, geom.n_ptiles - 1), 0)),
                 pl.BlockSpec((tm, d), lambda i, *_: (jnp.maximum(i - geom.n_ptiles, 0), 0))], list(x))
    return [pl.BlockSpec((tm, d), lambda i, *_: (i, 0))], [x]


def _stream_tile(x_refs, n_ptiles):
    if len(x_refs) == 1:
        return x_refs[0][...]
    return jnp.where(pl.program_id(0) < n_ptiles, x_refs[0][...], x_refs[1][...])


def _store_kv(k, v, kb_ref, vb_ref, ks_ref, vs_ref, n_ptiles):
    kb_ref[...] = k.astype(BF16)
    vb_ref[...] = v.astype(BF16)

    @pl.when(pl.program_id(0) < n_ptiles)
    def _():
        ks_ref[...] = k
        vs_ref[...] = v


def _proj_even_kernel(*refs, n_x, n_ptiles):
    x_refs = refs[:n_x]
    (g_ref, sh_ref, sc_ref, w_ref, cos_ref, sin_ref, dft_ref,
     tc_ref, ts_ref, q_ref, kb_ref, vb_ref, ks_ref, vs_ref) = refs[n_x:]
    h = _rms(_stream_tile(x_refs, n_ptiles), g_ref[...]) * (1.0 + sc_ref[...]) + sh_ref[...]
    p = _dot(h.astype(BF16), w_ref[...])
    cos = cos_ref[...]
    sin = sin_ref[...]
    dft = dft_ref[...]
    tcs, tss = [], []
    for g in range(A_GROUPS):
        t = _dot(p[:, g * A_GROUP_DIM:(g + 1) * A_GROUP_DIM].astype(BF16), dft)
        tcs.append(t[:, :A_GROUP_DIM])
        tss.append(t[:, A_GROUP_DIM:])
    tc_ref[...] = jnp.concatenate(tcs, axis=1).astype(BF16)
    ts_ref[...] = jnp.concatenate(tss, axis=1).astype(BF16)
    o = A_WIDTH
    q = _rope(p[:, o:o + B_Q_WIDTH], cos, sin, B_HEAD_DIM // 4)
    q_ref[...] = (q * B_HEAD_DIM ** -0.5).astype(BF16)
    o += B_Q_WIDTH
    k = _rope(p[:, o:o + B_KV_WIDTH], cos, sin, B_HEAD_DIM // 4)
    o += B_KV_WIDTH
    _store_kv(k, p[:, o:o + B_KV_WIDTH], kb_ref, vb_ref, ks_ref, vs_ref, n_ptiles)


def _proj_odd_kernel(*refs, n_x, n_ptiles):
    x_refs = refs[:n_x]
    (g_ref, sh_ref, sc_ref, w_ref, cos_ref, sin_ref, qn_ref, kn_ref,
     q_ref, kb_ref, vb_ref, ks_ref, vs_ref) = refs[n_x:]
    h = _rms(_stream_tile(x_refs, n_ptiles), g_ref[...]) * (1.0 + sc_ref[...]) + sh_ref[...]
    p = _dot(h.astype(BF16), w_ref[...])
    cos = cos_ref[...]
    sin = sin_ref[...]
    q = _head_rms(p[:, :C_Q_WIDTH], qn_ref[...])
    k = _head_rms(p[:, C_Q_WIDTH:C_Q_WIDTH + C_KV_WIDTH], kn_ref[...])
    q_ref[...] = (_rope(q, cos, sin, C_HEAD_DIM // 4) * C_HEAD_DIM ** -0.5).astype(BF16)
    k = _rope(k, cos, sin, C_HEAD_DIM // 4)
    _store_kv(k, p[:, C_Q_WIDTH + C_KV_WIDTH:], kb_ref, vb_ref, ks_ref, vs_ref, n_ptiles)


def _in_projection(geom, layer, x, norm_g, mods, w, cos, sin, extras, kernel, out_widths, kv_width, name):
    t, d = geom.t, w.shape[0]
    tm = TOKEN_TILE
    n_out = w.shape[1]
    row = lambda i: (i, 0)
    const2 = lambda i: (0, 0)
    x_specs, x_args = _stream_specs(geom, x, d)
    in_specs = x_specs + [
        pl.BlockSpec((1, d), const2),
        _mod_spec(geom, layer, 0, d),
        _mod_spec(geom, layer, 1, d),
        pl.BlockSpec((d, n_out), const2),
        pl.BlockSpec((tm, LANES), lambda i: (geom.pos_block(i), 0)),
        pl.BlockSpec((tm, LANES), lambda i: (geom.pos_block(i), 0)),
    ] + [pl.BlockSpec(e.shape, const2) for e in extras]
    return pl.pallas_call(
        functools.partial(kernel, n_x=len(x_args), n_ptiles=geom.n_ptiles),
        out_shape=([jax.ShapeDtypeStruct((t, wd), BF16) for wd in out_widths]
                   + [jax.ShapeDtypeStruct((geom.tp, kv_width), F32)] * 2),
        grid=(geom.n_tiles,),
        in_specs=in_specs,
        out_specs=([pl.BlockSpec((tm, wd), row) for wd in out_widths]
                   + [pl.BlockSpec((tm, kv_width), lambda i: (jnp.minimum(i, geom.n_ptiles - 1), 0))] * 2),
        compiler_params=_cparams("arbitrary"),
        name=name,
    )(*x_args, norm_g, mods, mods, w, cos, sin, *extras)


def _fourier_kernel(cl_ref, sl_ref, tc_ref, ts_ref, *rest):
    o_ref = rest[-1]
    o_ref[...] = (_dot(cl_ref[...], tc_ref[...]) - _dot(sl_ref[...], ts_ref[...])).astype(o_ref.dtype)


def _fourier_tokens(tc, ts, cl, sl, n_seq, seq_len, row0, prev):
    t, width = tc.shape
    tr = min(seq_len, 512)
    n_r = seq_len // tr
    assert row0 % seq_len == 0
    seq0 = row0 // seq_len
    out0 = row0 // tr
    in_specs = [
        pl.BlockSpec((tr, seq_len), lambda s, r: (r, 0)),
        pl.BlockSpec((tr, seq_len), lambda s, r: (r, 0)),
        pl.BlockSpec((seq_len, width), lambda s, r: (seq0 + s, 0)),
        pl.BlockSpec((seq_len, width), lambda s, r: (seq0 + s, 0)),
    ]
    args = [cl, sl, tc, ts]
    aliases = {}
    if prev is not None:
        in_specs.append(pl.BlockSpec(memory_space=pl.ANY))
        args.append(prev)
        aliases = {4: 0}
    return pl.pallas_call(
        _fourier_kernel,
        out_shape=jax.ShapeDtypeStruct((t, width), BF16),
        grid=(n_seq, n_r),
        in_specs=in_specs,
        out_specs=pl.BlockSpec((tr, width), lambda s, r: (out0 + s * n_r + r, 0)),
        input_output_aliases=aliases,
        compiler_params=_cparams("arbitrary", "arbitrary"),
        name="fourier_tokens",
    )(*args)


def _attend(q, chunks, sink, o0, dh, den_col):
    m = sink
    acc = None
    den = None
    for k, v, mask in chunks:
        s = _dot_nt(q, k)
        if mask is not None:
            s = jnp.where(mask, s, MASKED)
        m_new = jnp.max(s, axis=-1, keepdims=True)
        if m is not None:
            m_new = jnp.maximum(m, m_new)
        p = jnp.exp(s - m_new)
        pv = _dot(p.astype(BF16), v)
        if acc is None:
            acc = pv
            if den_col is None:
                den = jnp.sum(p, axis=-1, keepdims=True)
        else:
            alpha = jnp.exp(m - m_new)
            acc = alpha * acc + pv
            if den_col is None:
                den = alpha * den + jnp.sum(p, axis=-1, keepdims=True)
        m = m_new
    if den_col is not None:
        den = acc[:, den_col:den_col + 1]
    if sink is not None:
        den = den + jnp.exp(sink - m)
    return acc[:, o0:o0 + dh] / den


def _attend_two_pass(q, chunks, sink):
    scores = []
    m = sink
    for k, _, mask in chunks:
        s = _dot_nt(q, k)
        if mask is not None:
            s = jnp.where(mask, s, MASKED)
        scores.append(s)
        mx = jnp.max(s, axis=-1, keepdims=True)
        m = mx if m is None else jnp.maximum(m, mx)
    den = None if sink is None else jnp.exp(sink - m)
    acc = None
    for (_, v, _), s in zip(chunks, scores):
        e = jnp.exp(s - m)
        es = jnp.sum(e, axis=-1, keepdims=True)
        den = es if den is None else den + es
        o = _dot(e.astype(BF16), v)
        acc = o if acc is None else acc + o
    return acc / den


def _stack_heads(q, kv, groups, dh):
    return jnp.concatenate([q[:, (kv * groups + g) * dh:(kv * groups + g + 1) * dh] for g in range(groups)], axis=0)


def _sink_column(sink_ref, kv, groups, rows):
    return jnp.concatenate([jnp.full((rows, 1), sink_ref[kv * groups + g], F32) for g in range(groups)], axis=0)


def _head_values(v, kv, dh, with_ones):
    if not with_ones:
        return v[:, kv * dh:(kv + 1) * dh], 0, None
    assert dh == LANES
    lane = lax.broadcasted_iota(I32, (v.shape[0], LANES), 1)
    ones = jnp.where(lane == 0, 1.0, 0.0).astype(BF16)
    return jnp.concatenate([v[:, kv * dh:(kv + 1) * dh], ones], axis=1), 0, dh


def _paired_heads_attention(q, sources, sink_ref, groups, dh):
    rows = q.shape[0]
    half_q = lax.broadcasted_iota(I32, (rows, LANES), 1) // dh
    per_head = []
    for kv in range(2):
        qh = jnp.concatenate(
            [jnp.where(half_q == kv, q[:, g * LANES:(g + 1) * LANES], jnp.zeros((), q.dtype)) for g in range(groups)],
            axis=0)
        chunks = []
        for k, v, msk in sources:
            half_v = lax.broadcasted_iota(I32, v.shape, 1) // dh
            chunks.append((k, jnp.where(half_v == kv, v, jnp.zeros((), v.dtype)), msk))
        sink = _sink_column(sink_ref, kv, groups, rows) if sink_ref is not None else None
        per_head.append(_attend_two_pass(qh, chunks, sink))
    return jnp.concatenate(
        [per_head[0][g * rows:(g + 1) * rows] + per_head[1][g * rows:(g + 1) * rows] for g in range(groups)], axis=1)


def _attn_kernel(*refs, kv_heads, groups, dh, has_sink, has_ctx, window, q_tile, seq_len, chunk):
    refs = list(refs)
    sink_ref = refs.pop(0) if has_sink else None
    q_ref, k_ref, v_ref = refs[:3]
    ck_ref, cv_ref = (refs[3], refs[4]) if has_ctx else (None, None)
    o_ref = refs[-1]
    q = q_ref[...]
    rows = q.shape[0]
    if window is None:
        spans = [(c * chunk, chunk) for c in range(seq_len // chunk)]
        mask = None
    else:
        n = pl.program_id(1)
        band = q_tile + 2 * window
        start = pl.multiple_of(jnp.clip(n * q_tile - window, 0, seq_len - band), LANES)
        spans = [(start, band)]
        qpos = n * q_tile + lax.broadcasted_iota(I32, (groups * rows, band), 0) % rows
        kpos = start + lax.broadcasted_iota(I32, (groups * rows, band), 1)
        mask = jnp.abs(kpos - qpos) <= window
    sources = [(k_ref[pl.ds(s0, n_s), :], v_ref[pl.ds(s0, n_s), :], mask) for s0, n_s in spans]
    if has_ctx:
        sources.append((ck_ref[...], cv_ref[...], None))
    if 2 * dh == LANES:
        o_ref[...] = _paired_heads_attention(q, sources, sink_ref, groups, dh).astype(o_ref.dtype)
        return
    outs = []
    for kv in range(kv_heads):
        online = dh == LANES and len(sources) > 1
        chunks = []
        for k, v, msk in sources:
            vh, o0, den_col = _head_values(v, kv, dh, with_ones=online)
            chunks.append((k[:, kv * dh:(kv + 1) * dh], vh, msk))
        sink = _sink_column(sink_ref, kv, groups, rows) if has_sink else None
        qh = _stack_heads(q, kv, groups, dh)
        o = _attend(qh, chunks, sink, o0, dh, den_col) if online else _attend_two_pass(qh, chunks, sink)
        outs.extend(o[g * rows:(g + 1) * rows] for g in range(groups))
    o_ref[...] = jnp.concatenate(outs, axis=1).astype(o_ref.dtype)


def _attention(q, k, v, ctx, sink, prev, *, n_seq, seq_len, row0, q_tile, kv_heads, groups, dh, window):
    t, qw = q.shape
    kw = k.shape[1]
    n_q = seq_len // q_tile
    assert row0 % seq_len == 0 and row0 % q_tile == 0
    seq0 = row0 // seq_len
    q0 = row0 // q_tile
    in_specs, args = [], []
    if sink is not None:
        in_specs.append(pl.BlockSpec(memory_space=pltpu.SMEM))
        args.append(sink)
    in_specs += [
        pl.BlockSpec((q_tile, qw), lambda s, n: (q0 + s * n_q + n, 0)),
        pl.BlockSpec((seq_len, kw), lambda s, n: (seq0 + s, 0)),
        pl.BlockSpec((seq_len, kw), lambda s, n: (seq0 + s, 0)),
    ]
    args += [q, k, v]
    if ctx is not None:
        p = ctx[0].shape[1]
        in_specs += [pl.BlockSpec((None, p, kw), lambda s, n: (s, 0, 0))] * 2
        args += list(ctx)
    aliases = {}
    if prev is not None:
        in_specs.append(pl.BlockSpec(memory_space=pl.ANY))
        aliases = {len(args): 0}
        args.append(prev)
    kern = functools.partial(
        _attn_kernel, kv_heads=kv_heads, groups=groups, dh=dh, has_sink=sink is not None,
        has_ctx=ctx is not None, window=window, q_tile=q_tile, seq_len=seq_len, chunk=min(seq_len, ATTN_KEY_CHUNK))
    return pl.pallas_call(
        kern,
        out_shape=jax.ShapeDtypeStruct((t, qw), BF16),
        grid=(n_seq, n_q),
        in_specs=in_specs,
        out_specs=pl.BlockSpec((q_tile, qw), lambda s, n: (q0 + s * n_q + n, 0)),
        input_output_aliases=aliases,
        compiler_params=_cparams("arbitrary", "arbitrary"),
        name="attention",
    )(*args)


def _out_route_kernel(*refs, n_mix, n_x, n_ptiles):
    mix_refs = refs[:n_mix]
    w_refs = refs[n_mix:2 * n_mix]
    x_refs = refs[2 * n_mix:2 * n_mix + n_x]
    (gate_ref, g2_ref, sh2_ref, sc2_ref, rw_ref, rb_ref,
     xo_ref, h_ref, pos_ref, gates_ref, rows_ref, off_ref) = refs[2 * n_mix + n_x:]

    acc = None
    for m_ref, w_ref in zip(mix_refs, w_refs):
        part = _dot(m_ref[...], w_ref[...])
        acc = part if acc is None else acc + part
    xn = _stream_tile(x_refs, n_ptiles) + gate_ref[...] * acc
    xo_ref[...] = xn
    h = _rms(xn, g2_ref[...]) * (1.0 + sc2_ref[...]) + sh2_ref[...]
    hb = h.astype(BF16)
    h_ref[...] = hb

    logits = _dot(hb, rw_ref[...]) + rb_ref[...]
    tm, ne = logits.shape
    lane = lax.broadcasted_iota(I32, (tm, ne), 1).astype(F32)
    lane4 = lax.broadcasted_iota(I32, (tm, TOP_K), 1)
    work = logits
    sels, vals = [], []
    for _ in range(TOP_K):
        mx = jnp.max(work, axis=-1, keepdims=True)
        first = jnp.min(jnp.where(work == mx, lane, float(ne)), axis=-1, keepdims=True)
        sel = lane == first
        work = jnp.where(sel, -jnp.inf, work)
        sels.append(sel)
        vals.append(mx)
    exps = [jnp.exp(v - vals[0]) for v in vals]
    den = exps[0] + exps[1] + exps[2] + exps[3]

    onehot = jnp.zeros((tm, ne), F32)
    for sel in sels:
        onehot = onehot + sel.astype(F32)
    r_i = lax.broadcasted_iota(I32, (tm, tm), 0)
    c_i = lax.broadcasted_iota(I32, (tm, tm), 1)
    before = jnp.where(c_i < r_i, 1.0, 0.0).astype(BF16)
    earlier = _dot(before, onehot.astype(BF16))

    cnt = jnp.sum(onehot, axis=0, keepdims=True)
    seg8 = jnp.floor((cnt + (SEG_ALIGN - 1.0)) * (1.0 / SEG_ALIGN))
    e_r = lax.broadcasted_iota(I32, (ne, ne), 0)
    e_c = lax.broadcasted_iota(I32, (ne, ne), 1)
    upper = jnp.where(e_r < e_c, 1.0, 0.0).astype(BF16)
    off8 = _dot(jnp.broadcast_to(seg8, (SEG_ALIGN, ne)).astype(BF16), upper)[0:1]
    seg_off = off8 * SEG_ALIGN
    base = seg_off + earlier

    pos_o = jnp.zeros((tm, TOP_K), I32)
    gate_o = jnp.zeros((tm, TOP_K), F32)
    for k in range(TOP_K):
        pos_k = jnp.sum(jnp.where(sels[k], base, 0.0), axis=-1, keepdims=True)
        pos_o = jnp.where(lane4 == k, pos_k.astype(I32), pos_o)
        gate_o = jnp.where(lane4 == k, exps[k] / den, gate_o)
    pos_ref[...] = pos_o
    gates_ref[...] = gate_o
    rows_ref[0] = (seg8 * SEG_ALIGN).astype(I32)
    off_ref[0] = seg_off.astype(I32)


def _out_route(geom, layer, mixes, w_parts, x, mods, norm_g, router_w, router_b):
    t, d = geom.t, w_parts[0].shape[1]
    tm = TOKEN_TILE
    row = lambda i: (i, 0)
    const2 = lambda i: (0, 0)
    x_specs, x_args = _stream_specs(geom, x, d)
    in_specs = [pl.BlockSpec((tm, m.shape[1]), row) for m in mixes]
    in_specs += [pl.BlockSpec(w.shape, const2) for w in w_parts]
    in_specs += x_specs
    in_specs += [
        _mod_spec(geom, layer, 2, d),
        pl.BlockSpec((1, d), const2),
        _mod_spec(geom, layer, 3, d),
        _mod_spec(geom, layer, 4, d),
        pl.BlockSpec(router_w.shape, const2),
        pl.BlockSpec((1, N_EXPERTS), const2),
    ]
    seg3 = lambda i: (i, 0, 0)
    out_shape = [
        jax.ShapeDtypeStruct((t, d), F32),
        jax.ShapeDtypeStruct((t, d), BF16),
        jax.ShapeDtypeStruct((t, TOP_K), I32),
        jax.ShapeDtypeStruct((t, TOP_K), F32),
        jax.ShapeDtypeStruct((geom.n_tiles, 1, N_EXPERTS), I32),
        jax.ShapeDtypeStruct((geom.n_tiles, 1, N_EXPERTS), I32),
    ]
    out_specs = [
        pl.BlockSpec((tm, d), row),
        pl.BlockSpec((tm, d), row),
        pl.BlockSpec((tm, TOP_K), row),
        pl.BlockSpec((tm, TOP_K), row),
        pl.BlockSpec((1, 1, N_EXPERTS), seg3),
        pl.BlockSpec((1, 1, N_EXPERTS), seg3),
    ]
    return pl.pallas_call(
        functools.partial(_out_route_kernel, n_mix=len(mixes), n_x=len(x_args), n_ptiles=geom.n_ptiles),
        out_shape=out_shape,
        grid=(geom.n_tiles,),
        in_specs=in_specs,
        out_specs=out_specs,
        compiler_params=_cparams("arbitrary"),
        name="out_route",
    )(*mixes, *w_parts, *x_args, mods, norm_g, mods, mods, router_w, router_b)


def _pack_pairs(v):
    n = v.shape[1] // 2
    bits = lax.bitcast_convert_type(v, U32)
    return (bits[:, :n] & jnp.uint32(0xFFFF0000)) | (bits[:, n:] >> 16)


def _unpack_pairs(p):
    hi = lax.bitcast_convert_type(p & jnp.uint32(0xFFFF0000), F32)
    lo = lax.bitcast_convert_type(p << 16, F32)
    return jnp.concatenate([hi, lo], axis=1).astype(BF16)


def _planned_copies(i, plan_refs, make_copy):
    for size, width, (local_ref, slot_ref, count_ref) in zip(COPY_SIZES, PLAN_WIDTHS, plan_refs):
        def one(c, carry, size=size, width=width, local_ref=local_ref, slot_ref=slot_ref):
            a = i * width + c
            make_copy(pl.multiple_of(local_ref[a], SEG_ALIGN), pl.multiple_of(slot_ref[a], SEG_ALIGN), size).start()
            return carry

        lax.fori_loop(0, count_ref[i], one, 0)


def _copy_plan(rows, seg_off, seg_dst):
    plan = []
    experts = jnp.arange(N_EXPERTS, dtype=I32)
    for n, (size, width) in enumerate(zip(COPY_SIZES, PLAN_WIDTHS)):
        if n == 0:
            count, done = rows // size, jnp.zeros_like(rows)
        else:
            count, done = (rows % (2 * size)) // size, rows - rows % (2 * size)
        cum = jnp.cumsum(count, axis=1)
        j = jnp.arange(width, dtype=I32)
        owner = jnp.sum((cum[:, None, :] <= j[None, :, None]).astype(I32), axis=2)
        pick = (jnp.minimum(owner, N_EXPERTS - 1)[:, :, None] == experts[None, None, :]).astype(I32)
        take = lambda v: jnp.sum(pick * v[:, None, :], axis=2)
        within = (j[None, :] - take(cum - count)) * size
        plan += [(take(seg_off + done) + within).reshape(-1).astype(I32),
                 (take(seg_dst + done) + within).reshape(-1).astype(I32),
                 cum[:, -1].astype(I32)]
    return plan


def _wait_copies(n_rows, make_copy):
    def wait_big(c, carry):
        make_copy(0, 0, WAIT_CHUNK).wait()
        return carry
    lax.fori_loop(0, n_rows // WAIT_CHUNK, wait_big, 0)
    size = WAIT_CHUNK // 2
    while size >= SEG_ALIGN:
        @pl.when(n_rows % (2 * size) >= size)
        def _(size=size):
            make_copy(0, 0, size).wait()

        size //= 2


def _dispatch_kernel(*refs):
    plan_refs, (tot_ref, h_ref, pos_ref, xs_ref, sorted_ref, sem) = _split_plan(refs)
    i = pl.program_id(0)
    buf = i % 2
    tm = h_ref.shape[0]
    n_sorted = sorted_ref.shape[1]
    pos = pos_ref[...]
    lane = lax.broadcasted_iota(I32, (tm, n_sorted), 1)
    hit = jnp.zeros((tm, n_sorted), F32)
    for k in range(TOP_K):
        hit = jnp.where(pos[:, k:k + 1] == lane, 1.0, hit)
    sorted_ref[buf] = _pack_pairs(lax.dot_general(hit.astype(BF16), h_ref[...], (((0,), (0,)), ((), ())),
                                                  preferred_element_type=F32))

    def copies_from(b):
        def make_copy(local, slot, rows):
            return pltpu.make_async_copy(sorted_ref.at[b, pl.ds(local, rows)], xs_ref.at[pl.ds(slot, rows)],
                                         sem.at[b])
        return make_copy

    _planned_copies(i, plan_refs, copies_from(buf))

    @pl.when(i > 0)
    def _():
        _wait_copies(tot_ref[jnp.maximum(i - 1, 0)], copies_from(1 - buf))

    @pl.when(i == pl.num_programs(0) - 1)
    def _():
        _wait_copies(tot_ref[i], copies_from(buf))


def _split_plan(refs):
    n = 3 * len(COPY_SIZES)
    return [refs[k:k + 3] for k in range(0, n, 3)], refs[n:]


def _dispatch(h, pos, plan, tile_rows, n_slots):
    t, d = h.shape
    tm = TOKEN_TILE
    return pl.pallas_call(
        _dispatch_kernel,
        out_shape=jax.ShapeDtypeStruct((n_slots, d // 2), U32),
        grid_spec=pltpu.PrefetchScalarGridSpec(
            num_scalar_prefetch=len(plan) + 1,
            grid=(t // tm,),
            in_specs=[
                pl.BlockSpec((tm, d), lambda i, *_: (i, 0)),
                pl.BlockSpec((tm, TOP_K), lambda i, *_: (i, 0)),
            ],
            out_specs=pl.BlockSpec(memory_space=pl.ANY),
            scratch_shapes=[pltpu.VMEM((2, SORTED_ROWS, d // 2), U32), pltpu.SemaphoreType.DMA((2,))],
        ),
        compiler_params=_cparams("arbitrary"),
        name="moe_dispatch",
    )(*plan, tile_rows, h, pos)


def _ffn_kernel(rows_ref, start_ref, wgu_ref, bgu_ref, wd_ref, bd_ref, xs_ref, ys_ref,
                wgu_bf, wd_bf, xbuf, ybuf, sem_in, sem_out):
    e = pl.program_id(0)
    d_ff = wd_ref.shape[1]
    tb = xbuf.shape[1]
    n_rows = rows_ref[e]
    n_tiles = (n_rows + tb - 1) // tb
    base = start_ref[e]

    chunk = 128
    def cast_gu(c, carry):
        r = pl.multiple_of(c * chunk, chunk)
        wgu_bf[pl.ds(r, chunk), :] = wgu_ref[0, pl.ds(r, chunk), :].astype(BF16)
        return carry
    lax.fori_loop(0, wgu_ref.shape[1] // chunk, cast_gu, 0)
    def cast_d(c, carry):
        r = pl.multiple_of(c * chunk, chunk)
        wd_bf[pl.ds(r, chunk), :] = wd_ref[0, pl.ds(r, chunk), :].astype(BF16)
        return carry
    lax.fori_loop(0, d_ff // chunk, cast_d, 0)

    half = tb // 2
    last_small = (n_rows - (n_tiles - 1) * tb) <= half

    def x_copy(s, slot):
        r = pl.multiple_of(base + s * tb, EXPERT_TILE)
        return pltpu.make_async_copy(xs_ref.at[pl.ds(r, tb)], xbuf.at[slot], sem_in.at[slot])

    def y_copy(s, slot, rows=tb):
        r = pl.multiple_of(base + s * tb, EXPERT_TILE)
        return pltpu.make_async_copy(ybuf.at[slot, pl.ds(0, rows)], ys_ref.at[pl.ds(r, rows)], sem_out.at[slot])

    def ffn_rows(s, slot, n):
        rows = s * tb + lax.broadcasted_iota(I32, (n, 1), 0)
        x = _unpack_pairs(jnp.where(rows < n_rows, xbuf[slot, pl.ds(0, n), :], jnp.uint32(0)))
        gu = _dot(x, wgu_bf[...]) + bgu_ref[0]
        gate = jnp.minimum(gu[:, :d_ff], SWIGLU_LIMIT)
        up = jnp.clip(gu[:, d_ff:], -SWIGLU_LIMIT, SWIGLU_LIMIT)
        act = (up + 1.0) * (gate * (1.0 / (1.0 + jnp.exp(-SWIGLU_ALPHA * gate))))
        y = _dot(act.astype(BF16), wd_bf[...]) + bd_ref[0]
        ybuf[slot, pl.ds(0, n), :] = _pack_pairs(y.astype(BF16).astype(F32))
        y_copy(s, slot, n).start()

    @pl.when(n_tiles > 0)
    def _():
        x_copy(0, 0).start()

    def tile(s, carry):
        slot = s % 2
        x_copy(s, slot).wait()

        @pl.when(s + 1 < n_tiles)
        def _():
            x_copy(s + 1, 1 - slot).start()

        @pl.when(s >= 2)
        def _():
            y_copy(s - 2, slot).wait()

        small = jnp.logical_and(s == n_tiles - 1, last_small)

        @pl.when(jnp.logical_not(small))
        def _():
            ffn_rows(s, slot, tb)

        @pl.when(small)
        def _():
            ffn_rows(s, slot, half)

        return carry

    lax.fori_loop(0, n_tiles, tile, 0)

    @pl.when(n_tiles >= 2)
    def _():
        y_copy(n_tiles - 2, n_tiles % 2).wait()

    @pl.when(jnp.logical_and(n_tiles >= 1, jnp.logical_not(last_small)))
    def _():
        y_copy(n_tiles - 1, (n_tiles - 1) % 2).wait()

    @pl.when(jnp.logical_and(n_tiles >= 1, last_small))
    def _():
        y_copy(n_tiles - 1, (n_tiles - 1) % 2, half).wait()


def _expert_ffn(layer, xs, expert_rows, expert_start, w_gu, b_gu, w_down, b_down):
    n_slots, packed_w = xs.shape
    tb = FFN_TILE
    depth, ne, d, two_f = w_gu.shape
    d_ff = two_f // 2
    exp4 = lambda e, *_: (layer, e, 0, 0)
    return pl.pallas_call(
        _ffn_kernel,
        out_shape=jax.ShapeDtypeStruct((n_slots, packed_w), U32),
        grid_spec=pltpu.PrefetchScalarGridSpec(
            num_scalar_prefetch=2,
            grid=(ne,),
            in_specs=[
                pl.BlockSpec((None, 1, d, two_f), exp4),
                pl.BlockSpec((None, 1, 1, two_f), exp4),
                pl.BlockSpec((None, 1, d_ff, d), exp4),
                pl.BlockSpec((None, 1, 1, d), exp4),
                pl.BlockSpec(memory_space=pl.ANY),
            ],
            out_specs=pl.BlockSpec(memory_space=pl.ANY),
            scratch_shapes=[
                pltpu.VMEM((d, two_f), BF16), pltpu.VMEM((d_ff, d), BF16),
                pltpu.VMEM((2, tb, packed_w), U32), pltpu.VMEM((2, tb, packed_w), U32),
                pltpu.SemaphoreType.DMA((2,)), pltpu.SemaphoreType.DMA((2,)),
            ],
        ),
        compiler_params=_cparams("arbitrary"),
        name="expert_ffn",
    )(expert_rows, expert_start, w_gu, b_gu.reshape(depth, ne, 1, two_f),
      w_down, b_down.reshape(depth, ne, 1, d), xs)


def _combine_kernel(*refs, final, n_ptiles):
    plan_refs, (tot_ref, x_ref, pos_ref, gates_ref, mg_ref, *rest) = _split_plan(refs)
    if final:
        fn_ref, ys_ref, op_ref, os_ref, buf, sem = rest
    else:
        ys_ref, o_ref, buf, sem = rest
    i = pl.program_id(0)
    cur = i % 2
    tm = x_ref.shape[0]
    n_sorted = buf.shape[1]

    def copies_into(b):
        def make_copy(local, slot, rows):
            return pltpu.make_async_copy(ys_ref.at[pl.ds(slot, rows)], buf.at[b, pl.ds(local, rows)], sem.at[b])
        return make_copy

    @pl.when(i == 0)
    def _():
        buf[...] = jnp.zeros_like(buf)
        _planned_copies(i, plan_refs, copies_into(cur))

    @pl.when(i + 1 < pl.num_programs(0))
    def _():
        _planned_copies(i + 1, plan_refs, copies_into(1 - cur))

    pos = pos_ref[...]
    g = gates_ref[...]
    lane = lax.broadcasted_iota(I32, (tm, n_sorted), 1)
    weight = jnp.zeros((tm, n_sorted), F32)
    for k in range(TOP_K):
        weight = jnp.where(pos[:, k:k + 1] == lane, g[:, k:k + 1], weight)
    _wait_copies(tot_ref[i], copies_into(cur))
    y = _dot(weight.astype(BF16), _unpack_pairs(buf[cur]))
    xn = x_ref[...] + mg_ref[...] * y
    if final:
        xn = _rms(xn, fn_ref[...])

        @pl.when(i < n_ptiles)
        def _():
            op_ref[...] = xn

        @pl.when(i >= n_ptiles)
        def _():
            os_ref[...] = xn
    else:
        o_ref[...] = xn


def _combine(geom, layer, ys, pos, plan, tile_rows, x, gates, mods, final_g):
    t, d = x.shape
    tm = TOKEN_TILE
    final = final_g is not None

    def mod_imap(i, *_):
        return ((layer * MOD_ROWS + geom.group(i)) * N_MOD + 5, 0, 0)

    row = lambda i, *_: (i, 0)
    in_specs = [
        pl.BlockSpec((tm, d), row),
        pl.BlockSpec((tm, TOP_K), row),
        pl.BlockSpec((tm, TOP_K), row),
        pl.BlockSpec((None, 1, d), mod_imap),
    ]
    args = [*plan, tile_rows, x, pos, gates, mods]
    if final:
        in_specs.append(pl.BlockSpec((1, d), lambda i, *_: (0, 0)))
        args.append(final_g)
    in_specs.append(pl.BlockSpec(memory_space=pl.ANY))
    args.append(ys)
    if final:
        n_pt = geom.n_ptiles
        out_shape = [jax.ShapeDtypeStruct((geom.tp, d), F32), jax.ShapeDtypeStruct((t - geom.tp, d), F32)]
        out_specs = [pl.BlockSpec((tm, d), lambda i, *_: (jnp.minimum(i, n_pt - 1), 0)),
                     pl.BlockSpec((tm, d), lambda i, *_: (jnp.maximum(i - n_pt, 0), 0))]
    else:
        out_shape = jax.ShapeDtypeStruct((t, d), F32)
        out_specs = pl.BlockSpec((tm, d), row)
    return pl.pallas_call(
        functools.partial(_combine_kernel, final=final, n_ptiles=geom.n_ptiles),
        out_shape=out_shape,
        grid_spec=pltpu.PrefetchScalarGridSpec(
            num_scalar_prefetch=len(plan) + 1,
            grid=(t // tm,),
            in_specs=in_specs,
            out_specs=out_specs,
            scratch_shapes=[pltpu.VMEM((2, SORTED_ROWS, d // 2), U32), pltpu.SemaphoreType.DMA((2,))],
        ),
        compiler_params=_cparams("arbitrary"),
        name="moe_combine",
    )(*args)


def _moe(geom, layer, h, pos, gates, seg_rows, seg_off, x, mods, w_gu, b_gu, w_down, b_down, final_g):
    t = h.shape[0]
    tb = EXPERT_TILE
    n_tok_tiles = seg_rows.shape[0]
    max_rows = t * TOP_K + n_tok_tiles * N_EXPERTS * (SEG_ALIGN - 1)
    n_blocks = -(-max_rows // tb) + N_EXPERTS
    rows = seg_rows[:, 0, :]
    cnt = jnp.sum(rows, axis=0)
    n_tiles_e = (cnt + tb - 1) // tb
    tile_end = jnp.cumsum(n_tiles_e)
    tile_start = tile_end - n_tiles_e
    expert_start = (tile_start * tb).astype(I32)
    seg_dst = expert_start[None, :] + jnp.cumsum(rows, axis=0) - rows
    plan = _copy_plan(rows, seg_off[:, 0, :], seg_dst)
    tile_rows = jnp.sum(rows, axis=1).astype(I32)
    xs = _dispatch(h, pos, plan, tile_rows, n_blocks * tb + FFN_TILE - tb)
    ys = _expert_ffn(layer, xs, cnt.astype(I32), expert_start, w_gu, b_gu, w_down, b_down)
    return _combine(geom, layer, ys, pos, plan, tile_rows, x, gates, mods, final_g)


def kernel(x_prompt, x_sample, cache_b_k, cache_b_v, cache_c_k, cache_c_v, c, c_ctx,
           mod_w, mod_b, norm_mix, norm_ffn, even_w_in, even_w_out, even_sink,
           odd_w_in, odd_w_out, odd_q_norm, odd_k_norm, router_w, router_b,
           moe_w_gu, moe_b_gu, moe_w_down, moe_b_down, final_norm):
    bp, lp, d = x_prompt.shape
    bs, ls, _ = x_sample.shape
    past = cache_b_k.shape[2]
    depth = mod_w.shape[0]
    geom = _Geom(bp, lp, bs, ls)
    tp = geom.tp

    x = (x_prompt.reshape(tp, d), x_sample.reshape(bs * ls, d))
    cond = jnp.concatenate([c_ctx[None, :], c, jnp.zeros((MOD_ROWS - 1 - bs, d), F32)], axis=0)
    mods = _modulation(cond, mod_w, mod_b).reshape(depth * MOD_ROWS * N_MOD, 1, d)

    cn, sn = _dft_tables(A_GROUP_DIM)
    dft_chan = jnp.asarray(np.concatenate([cn, sn], axis=1), BF16)
    dft_p = [jnp.asarray(m, BF16) for m in _dft_tables(lp)]
    dft_s = [jnp.asarray(m, BF16) for m in _dft_tables(ls)]
    rope_b = [jnp.asarray(m) for m in _rope_tables(TOKEN_TILE, ls, B_HEAD_DIM)]
    rope_c = [jnp.asarray(m) for m in _rope_tables(TOKEN_TILE, ls, C_HEAD_DIM)]

    states = {"bk": [], "bv": [], "ck": [], "cv": []}
    for layer in range(depth):
        j = layer // 2
        g_mix = norm_mix[layer][None, :]
        g_ffn = norm_ffn[layer][None, :]
        if layer % 2 == 0:
            pair = _paired_head_order(B_KV_HEADS, B_HEADS // B_KV_HEADS, B_HEAD_DIM)
            w_in = even_w_in[j]
            w_in = jnp.concatenate([w_in[:, :A_WIDTH], w_in[:, A_WIDTH:A_WIDTH + B_Q_WIDTH][:, pair],
                                    w_in[:, A_WIDTH + B_Q_WIDTH:]], axis=1).astype(BF16)
            tc, ts, q, k, v, k_state, v_state = _in_projection(
                geom, layer, x, g_mix, mods, w_in, rope_b[0], rope_b[1], [dft_chan],
                _proj_even_kernel, (A_WIDTH, A_WIDTH, B_Q_WIDTH, B_KV_WIDTH, B_KV_WIDTH), B_KV_WIDTH, "proj_even")
            states["bk"].append(k_state.reshape(bp, lp, B_KV_HEADS, B_HEAD_DIM))
            states["bv"].append(v_state.reshape(bp, lp, B_KV_HEADS, B_HEAD_DIM))
            four = _fourier_tokens(tc, ts, dft_p[0], dft_p[1], bp, lp, 0, None)
            four = _fourier_tokens(tc, ts, dft_s[0], dft_s[1], bs, ls, tp, four)
            sink = even_sink[j]
            common = dict(kv_heads=B_KV_HEADS, groups=B_HEADS // B_KV_HEADS, dh=B_HEAD_DIM)
            att = _attention(q, k, v, None, sink, None, n_seq=bp, seq_len=lp, row0=0, q_tile=lp,
                             window=None, **common)
            ctx = (cache_b_k[:, j].reshape(bs, past, B_KV_WIDTH).astype(BF16),
                   cache_b_v[:, j].reshape(bs, past, B_KV_WIDTH).astype(BF16))
            att = _attention(q, k, v, ctx, sink, att, n_seq=bs, seq_len=ls, row0=tp, q_tile=ATTN_Q_TILE,
                             window=WINDOW, **common)
            w_out = even_w_out[j].astype(BF16)
            mixes = [four, att]
            w_parts = [w_out[:A_WIDTH], w_out[A_WIDTH:][pair]]
        else:
            q, k, v, k_state, v_state = _in_projection(
                geom, layer, x, g_mix, mods, odd_w_in[j].astype(BF16), rope_c[0], rope_c[1],
                [odd_q_norm[j][None, :], odd_k_norm[j][None, :]],
                _proj_odd_kernel, (C_Q_WIDTH, C_KV_WIDTH, C_KV_WIDTH), C_KV_WIDTH, "proj_odd")
            states["ck"].append(k_state.reshape(bp, lp, C_KV_HEADS, C_HEAD_DIM))
            states["cv"].append(v_state.reshape(bp, lp, C_KV_HEADS, C_HEAD_DIM))
            common = dict(kv_heads=C_KV_HEADS, groups=C_HEADS // C_KV_HEADS, dh=C_HEAD_DIM, window=None)
            att = _attention(q, k, v, None, None, None, n_seq=bp, seq_len=lp, row0=0, q_tile=lp, **common)
            ctx = (cache_c_k[:, j].reshape(bs, past, C_KV_WIDTH).astype(BF16),
                   cache_c_v[:, j].reshape(bs, past, C_KV_WIDTH).astype(BF16))
            att = _attention(q, k, v, ctx, None, att, n_seq=bs, seq_len=ls, row0=tp, q_tile=ATTN_Q_TILE, **common)
            mixes = [att]
            w_parts = [odd_w_out[j].astype(BF16)]
        x, h, pos, gates, seg_rows, seg_off = _out_route(
            geom, layer, mixes, w_parts, x, mods, g_ffn, router_w[layer].astype(BF16), router_b[layer][None, :])
        final_g = final_norm[None, :] if layer == depth - 1 else None
        x = _moe(geom, layer, h, pos, gates, seg_rows, seg_off, x, mods,
                 moe_w_gu, moe_b_gu, moe_w_down, moe_b_down, final_g)

    y_prompt = x[0].reshape(bp, lp, d)
    y_sample = x[1].reshape(bs, ls, d)
    return (y_prompt, y_sample,
            jnp.stack(states["bk"], axis=1), jnp.stack(states["bv"], axis=1),
            jnp.stack(states["ck"], axis=1), jnp.stack(states["cv"], axis=1))
```

```python
import functools

import numpy as np
import jax
import jax.numpy as jnp
from jax import lax
from jax.experimental import pallas as pl
from jax.experimental.pallas import tpu as pltpu

F32 = jnp.float32
BF16 = jnp.bfloat16
I32 = jnp.int32
U32 = jnp.uint32

GRID_W = 64
A_GROUPS = 4
A_GROUP_DIM = 128
A_WIDTH = A_GROUPS * A_GROUP_DIM
B_HEADS = 8
B_KV_HEADS = 2
B_HEAD_DIM = 64
B_Q_WIDTH = B_HEADS * B_HEAD_DIM
B_KV_WIDTH = B_KV_HEADS * B_HEAD_DIM
WINDOW = 128
C_HEADS = 8
C_KV_HEADS = 2
C_HEAD_DIM = 128
C_Q_WIDTH = C_HEADS * C_HEAD_DIM
C_KV_WIDTH = C_KV_HEADS * C_HEAD_DIM
ROPE_THETA = 10000.0
N_EXPERTS = 32
TOP_K = 4
SWIGLU_LIMIT = 7.0
SWIGLU_ALPHA = 1.702
EPS = 1e-6

LANES = 128
TOKEN_TILE = 256
EXPERT_TILE = 128
FFN_TILE = 256
ATTN_Q_TILE = 128
ATTN_KEY_CHUNK = 512
SEG_ALIGN = 8
COPY_SIZES = (32, 16, 8)
WAIT_CHUNK = 256
SORTED_ROWS = -(-(TOKEN_TILE * TOP_K + N_EXPERTS * (SEG_ALIGN - 1)) // LANES) * LANES
PLAN_WIDTHS = (SORTED_ROWS // COPY_SIZES[0],) + (N_EXPERTS,) * (len(COPY_SIZES) - 1)
VMEM_LIMIT = 56 * 1024 * 1024
MASKED = -1e30
N_MOD = 6
MOD_ROWS = 8


def _cparams(*sem):
    return pltpu.CompilerParams(dimension_semantics=tuple(sem), vmem_limit_bytes=VMEM_LIMIT)


def _dot(a, b):
    return jnp.dot(a, b, preferred_element_type=F32)


def _dot_nt(a, b):
    return lax.dot_general(a, b, (((1,), (1,)), ((), ())), preferred_element_type=F32)


def _rms(x, g):
    return x * lax.rsqrt(jnp.mean(x * x, axis=-1, keepdims=True) + EPS) * g


def _dft_tables(n):
    j = np.arange(n, dtype=np.int64)
    ang = 2.0 * np.pi * ((j[:, None] * j[None, :]) % n).astype(np.float64) / n
    s = 1.0 / np.sqrt(n)
    return np.cos(ang) * s, np.sin(ang) * s


def _paired_head_order(kv_heads, groups, dh):
    assert kv_heads * dh == LANES
    return np.array([(kv * groups + g) * dh + d for g in range(groups) for kv in range(kv_heads) for d in range(dh)])


def _rope_tables(n_prompt_rows, n_latent, head_dim):
    quarter = head_dim // 4
    pos = np.arange(n_latent)
    row = (pos // GRID_W).astype(np.float32)
    col = (pos % GRID_W).astype(np.float32)
    inv = (np.float32(ROPE_THETA) ** (-np.arange(quarter, dtype=np.float32) / np.float32(quarter))).astype(np.float32)
    ang_row = (row[:, None] * inv[None, :]).astype(np.float32)
    ang_col = (col[:, None] * inv[None, :]).astype(np.float32)
    cos_h = np.concatenate([np.cos(ang_row)] * 2 + [np.cos(ang_col)] * 2, axis=1)
    sin_h = np.concatenate([-np.sin(ang_row), np.sin(ang_row), -np.sin(ang_col), np.sin(ang_col)], axis=1)
    reps = LANES // head_dim
    cos_l = np.tile(cos_h, (1, reps)).astype(np.float32)
    sin_l = np.tile(sin_h, (1, reps)).astype(np.float32)
    cos = np.concatenate([np.ones((n_prompt_rows, LANES), np.float32), cos_l], axis=0)
    sin = np.concatenate([np.zeros((n_prompt_rows, LANES), np.float32), sin_l], axis=0)
    return cos, sin


def _rope(x, cos, sin, quarter):
    lane = lax.broadcasted_iota(I32, (x.shape[0], LANES), 1)
    first = ((lane // quarter) % 2) == 0
    outs = []
    for c in range(x.shape[1] // LANES):
        xc = x[:, c * LANES:(c + 1) * LANES]
        partner = jnp.where(first, pltpu.roll(xc, LANES - quarter, 1), pltpu.roll(xc, quarter, 1))
        outs.append(xc * cos + partner * sin)
    return outs[0] if len(outs) == 1 else jnp.concatenate(outs, axis=1)


def _head_rms(x, g):
    outs = []
    for c in range(x.shape[1] // LANES):
        outs.append(_rms(x[:, c * LANES:(c + 1) * LANES], g))
    return outs[0] if len(outs) == 1 else jnp.concatenate(outs, axis=1)


def _mod_kernel(c_ref, w_ref, b_ref, o_ref):
    c = c_ref[...]
    s = c * (1.0 / (1.0 + jnp.exp(-c)))
    o_ref[0] = _dot(s.astype(BF16), w_ref[0].astype(BF16)) + b_ref[0]


def _modulation(cond, mod_w, mod_b):
    depth, d, n = mod_w.shape
    tn = 1536
    return pl.pallas_call(
        _mod_kernel,
        out_shape=jax.ShapeDtypeStruct((depth, MOD_ROWS, n), F32),
        grid=(depth, n // tn),
        in_specs=[
            pl.BlockSpec((MOD_ROWS, d), lambda l, j: (0, 0)),
            pl.BlockSpec((1, d, tn), lambda l, j: (l, 0, j)),
            pl.BlockSpec((1, 1, tn), lambda l, j: (l, 0, j)),
        ],
        out_specs=pl.BlockSpec((1, MOD_ROWS, tn), lambda l, j: (l, 0, j)),
        compiler_params=_cparams("arbitrary", "arbitrary"),
        name="modulation",
    )(cond, mod_w, mod_b.reshape(depth, 1, n))


class _Geom:
    def __init__(self, bp, lp, bs, ls):
        self.bp, self.lp, self.bs, self.ls = bp, lp, bs, ls
        self.tp = bp * lp
        self.t = bp * lp + bs * ls
        assert lp == TOKEN_TILE and ls % TOKEN_TILE == 0 and self.tp % ls == 0
        self.n_ptiles = self.tp // TOKEN_TILE
        self.tiles_per_lat = ls // TOKEN_TILE
        self.n_tiles = self.t // TOKEN_TILE

    def group(self, i):
        return jnp.where(i < self.n_ptiles, 0, 1 + (i - self.n_ptiles) // self.tiles_per_lat)

    def pos_block(self, i):
        return jnp.where(i < self.n_ptiles, 0, 1 + (i - self.n_ptiles) % self.tiles_per_lat)


def _mod_spec(geom, layer, which, d):
    def imap(i):
        return ((layer * MOD_ROWS + geom.group(i)) * N_MOD + which, 0, 0)
    return pl.BlockSpec((None, 1, d), imap)


def _stream_specs(geom, x, d):
    tm = TOKEN_TILE
    if isinstance(x, tuple):
        return ([pl.BlockSpec((tm, d), lambda i, *_: (jnp.minimum(i, geom.n_ptiles - 1), 0)),
                 pl.BlockSpec((tm, d), lambda i, *_: (jnp.maximum(i - geom.n_ptiles, 0), 0))], list(x))
    return [pl.BlockSpec((tm, d), lambda i, *_: (i, 0))], [x]


def _stream_tile(x_refs, n_ptiles):
    if len(x_refs) == 1:
        return x_refs[0][...]
    return jnp.where(pl.program_id(0) < n_ptiles, x_refs[0][...], x_refs[1][...])


def _store_kv(k, v, kb_ref, vb_ref, ks_ref, vs_ref, n_ptiles):
    kb_ref[...] = k.astype(BF16)
    vb_ref[...] = v.astype(BF16)

    @pl.when(pl.program_id(0) < n_ptiles)
    def _():
        ks_ref[...] = k
        vs_ref[...] = v


def _proj_even_kernel(*refs, n_x, n_ptiles):
    x_refs = refs[:n_x]
    (g_ref, sh_ref, sc_ref, w_ref, cos_ref, sin_ref, dft_ref,
     tc_ref, ts_ref, q_ref, kb_ref, vb_ref, ks_ref, vs_ref) = refs[n_x:]
    h = _rms(_stream_tile(x_refs, n_ptiles), g_ref[...]) * (1.0 + sc_ref[...]) + sh_ref[...]
    p = _dot(h.astype(BF16), w_ref[...])
    cos = cos_ref[...]
    sin = sin_ref[...]
    dft = dft_ref[...]
    tcs, tss = [], []
    for g in range(A_GROUPS):
        t = _dot(p[:, g * A_GROUP_DIM:(g + 1) * A_GROUP_DIM].astype(BF16), dft)
        tcs.append(t[:, :A_GROUP_DIM])
        tss.append(t[:, A_GROUP_DIM:])
    tc_ref[...] = jnp.concatenate(tcs, axis=1).astype(BF16)
    ts_ref[...] = jnp.concatenate(tss, axis=1).astype(BF16)
    o = A_WIDTH
    q = _rope(p[:, o:o + B_Q_WIDTH], cos, sin, B_HEAD_DIM // 4)
    q_ref[...] = (q * B_HEAD_DIM ** -0.5).astype(BF16)
    o += B_Q_WIDTH
    k = _rope(p[:, o:o + B_KV_WIDTH], cos, sin, B_HEAD_DIM // 4)
    o += B_KV_WIDTH
    _store_kv(k, p[:, o:o + B_KV_WIDTH], kb_ref, vb_ref, ks_ref, vs_ref, n_ptiles)


def _proj_odd_kernel(*refs, n_x, n_ptiles):
    x_refs = refs[:n_x]
    (g_ref, sh_ref, sc_ref, w_ref, cos_ref, sin_ref, qn_ref, kn_ref,
     q_ref, kb_ref, vb_ref, ks_ref, vs_ref) = refs[n_x:]
    h = _rms(_stream_tile(x_refs, n_ptiles), g_ref[...]) * (1.0 + sc_ref[...]) + sh_ref[...]
    p = _dot(h.astype(BF16), w_ref[...])
    cos = cos_ref[...]
    sin = sin_ref[...]
    q = _head_rms(p[:, :C_Q_WIDTH], qn_ref[...])
    k = _head_rms(p[:, C_Q_WIDTH:C_Q_WIDTH + C_KV_WIDTH], kn_ref[...])
    q_ref[...] = (_rope(q, cos, sin, C_HEAD_DIM // 4) * C_HEAD_DIM ** -0.5).astype(BF16)
    k = _rope(k, cos, sin, C_HEAD_DIM // 4)
    _store_kv(k, p[:, C_Q_WIDTH + C_KV_WIDTH:], kb_ref, vb_ref, ks_ref, vs_ref, n_ptiles)


def _in_projection(geom, layer, x, norm_g, mods, w, cos, sin, extras, kernel, out_widths, kv_width, name):
    t, d = geom.t, w.shape[0]
    tm = TOKEN_TILE
    n_out = w.shape[1]
    row = lambda i: (i, 0)
    const2 = lambda i: (0, 0)
    x_specs, x_args = _stream_specs(geom, x, d)
    in_specs = x_specs + [
        pl.BlockSpec((1, d), const2),
        _mod_spec(geom, layer, 0, d),
        _mod_spec(geom, layer, 1, d),
        pl.BlockSpec((d, n_out), const2),
        pl.BlockSpec((tm, LANES), lambda i: (geom.pos_block(i), 0)),
        pl.BlockSpec((tm, LANES), lambda i: (geom.pos_block(i), 0)),
    ] + [pl.BlockSpec(e.shape, const2) for e in extras]
    return pl.pallas_call(
        functools.partial(kernel, n_x=len(x_args), n_ptiles=geom.n_ptiles),
        out_shape=([jax.ShapeDtypeStruct((t, wd), BF16) for wd in out_widths]
                   + [jax.ShapeDtypeStruct((geom.tp, kv_width), F32)] * 2),
        grid=(geom.n_tiles,),
        in_specs=in_specs,
        out_specs=([pl.BlockSpec((tm, wd), row) for wd in out_widths]
                   + [pl.BlockSpec((tm, kv_width), lambda i: (jnp.minimum(i, geom.n_ptiles - 1), 0))] * 2),
        compiler_params=_cparams("arbitrary"),
        name=name,
    )(*x_args, norm_g, mods, mods, w, cos, sin, *extras)


def _fourier_kernel(cl_ref, sl_ref, tc_ref, ts_ref, *rest):
    o_ref = rest[-1]
    o_ref[...] = (_dot(cl_ref[...], tc_ref[...]) - _dot(sl_ref[...], ts_ref[...])).astype(o_ref.dtype)


def _fourier_tokens(tc, ts, cl, sl, n_seq, seq_len, row0, prev):
    t, width = tc.shape
    tr = min(seq_len, 512)
    n_r = seq_len // tr
    assert row0 % seq_len == 0
    seq0 = row0 // seq_len
    out0 = row0 // tr
    in_specs = [
        pl.BlockSpec((tr, seq_len), lambda s, r: (r, 0)),
        pl.BlockSpec((tr, seq_len), lambda s, r: (r, 0)),
        pl.BlockSpec((seq_len, width), lambda s, r: (seq0 + s, 0)),
        pl.BlockSpec((seq_len, width), lambda s, r: (seq0 + s, 0)),
    ]
    args = [cl, sl, tc, ts]
    aliases = {}
    if prev is not None:
        in_specs.append(pl.BlockSpec(memory_space=pl.ANY))
        args.append(prev)
        aliases = {4: 0}
    return pl.pallas_call(
        _fourier_kernel,
        out_shape=jax.ShapeDtypeStruct((t, width), BF16),
        grid=(n_seq, n_r),
        in_specs=in_specs,
        out_specs=pl.BlockSpec((tr, width), lambda s, r: (out0 + s * n_r + r, 0)),
        input_output_aliases=aliases,
        compiler_params=_cparams("arbitrary", "arbitrary"),
        name="fourier_tokens",
    )(*args)


def _attend(q, chunks, sink, o0, dh, den_col):
    m = sink
    acc = None
    den = None
    for k, v, mask in chunks:
        s = _dot_nt(q, k)
        if mask is not None:
            s = jnp.where(mask, s, MASKED)
        m_new = jnp.max(s, axis=-1, keepdims=True)
        if m is not None:
            m_new = jnp.maximum(m, m_new)
        p = jnp.exp(s - m_new)
        pv = _dot(p.astype(BF16), v)
        if acc is None:
            acc = pv
            if den_col is None:
                den = jnp.sum(p, axis=-1, keepdims=True)
        else:
            alpha = jnp.exp(m - m_new)
            acc = alpha * acc + pv
            if den_col is None:
                den = alpha * den + jnp.sum(p, axis=-1, keepdims=True)
        m = m_new
    if den_col is not None:
        den = acc[:, den_col:den_col + 1]
    if sink is not None:
        den = den + jnp.exp(sink - m)
    return acc[:, o0:o0 + dh] / den


def _attend_two_pass(q, chunks, sink):
    scores = []
    m = sink
    for k, _, mask in chunks:
        s = _dot_nt(q, k)
        if mask is not None:
            s = jnp.where(mask, s, MASKED)
        scores.append(s)
        mx = jnp.max(s, axis=-1, keepdims=True)
        m = mx if m is None else jnp.maximum(m, mx)
    den = None if sink is None else jnp.exp(sink - m)
    acc = None
    for (_, v, _), s in zip(chunks, scores):
        e = jnp.exp(s - m)
        es = jnp.sum(e, axis=-1, keepdims=True)
        den = es if den is None else den + es
        o = _dot(e.astype(BF16), v)
        acc = o if acc is None else acc + o
    return acc / den


def _stack_heads(q, kv, groups, dh):
    return jnp.concatenate([q[:, (kv * groups + g) * dh:(kv * groups + g + 1) * dh] for g in range(groups)], axis=0)


def _sink_column(sink_ref, kv, groups, rows):
    return jnp.concatenate([jnp.full((rows, 1), sink_ref[kv * groups + g], F32) for g in range(groups)], axis=0)


def _head_values(v, kv, dh, with_ones):
    if not with_ones:
        return v[:, kv * dh:(kv + 1) * dh], 0, None
    assert dh == LANES
    lane = lax.broadcasted_iota(I32, (v.shape[0], LANES), 1)
    ones = jnp.where(lane == 0, 1.0, 0.0).astype(BF16)
    return jnp.concatenate([v[:, kv * dh:(kv + 1) * dh], ones], axis=1), 0, dh


def _paired_heads_attention(q, sources, sink_ref, groups, dh):
    rows = q.shape[0]
    half_q = lax.broadcasted_iota(I32, (rows, LANES), 1) // dh
    per_head = []
    for kv in range(2):
        qh = jnp.concatenate(
            [jnp.where(half_q == kv, q[:, g * LANES:(g + 1) * LANES], jnp.zeros((), q.dtype)) for g in range(groups)],
            axis=0)
        chunks = []
        for k, v, msk in sources:
            half_v = lax.broadcasted_iota(I32, v.shape, 1) // dh
            chunks.append((k, jnp.where(half_v == kv, v, jnp.zeros((), v.dtype)), msk))
        sink = _sink_column(sink_ref, kv, groups, rows) if sink_ref is not None else None
        per_head.append(_attend_two_pass(qh, chunks, sink))
    return jnp.concatenate(
        [per_head[0][g * rows:(g + 1) * rows] + per_head[1][g * rows:(g + 1) * rows] for g in range(groups)], axis=1)


def _attn_kernel(*refs, kv_heads, groups, dh, has_sink, has_ctx, window, q_tile, seq_len, chunk):
    refs = list(refs)
    sink_ref = refs.pop(0) if has_sink else None
    q_ref, k_ref, v_ref = refs[:3]
    ck_ref, cv_ref = (refs[3], refs[4]) if has_ctx else (None, None)
    o_ref = refs[-1]
    q = q_ref[...]
    rows = q.shape[0]
    if window is None:
        spans = [(c * chunk, chunk) for c in range(seq_len // chunk)]
        mask = None
    else:
        n = pl.program_id(1)
        band = q_tile + 2 * window
        start = pl.multiple_of(jnp.clip(n * q_tile - window, 0, seq_len - band), LANES)
        spans = [(start, band)]
        qpos = n * q_tile + lax.broadcasted_iota(I32, (groups * rows, band), 0) % rows
        kpos = start + lax.broadcasted_iota(I32, (groups * rows, band), 1)
        mask = jnp.abs(kpos - qpos) <= window
    sources = [(k_ref[pl.ds(s0, n_s), :], v_ref[pl.ds(s0, n_s), :], mask) for s0, n_s in spans]
    if has_ctx:
        sources.append((ck_ref[...], cv_ref[...], None))
    if 2 * dh == LANES:
        o_ref[...] = _paired_heads_attention(q, sources, sink_ref, groups, dh).astype(o_ref.dtype)
        return
    outs = []
    for kv in range(kv_heads):
        online = dh == LANES and len(sources) > 1
        chunks = []
        for k, v, msk in sources:
            vh, o0, den_col = _head_values(v, kv, dh, with_ones=online)
            chunks.append((k[:, kv * dh:(kv + 1) * dh], vh, msk))
        sink = _sink_column(sink_ref, kv, groups, rows) if has_sink else None
        qh = _stack_heads(q, kv, groups, dh)
        o = _attend(qh, chunks, sink, o0, dh, den_col) if online else _attend_two_pass(qh, chunks, sink)
        outs.extend(o[g * rows:(g + 1) * rows] for g in range(groups))
    o_ref[...] = jnp.concatenate(outs, axis=1).astype(o_ref.dtype)


def _attention(q, k, v, ctx, sink, prev, *, n_seq, seq_len, row0, q_tile, kv_heads, groups, dh, window):
    t, qw = q.shape
    kw = k.shape[1]
    n_q = seq_len // q_tile
    assert row0 % seq_len == 0 and row0 % q_tile == 0
    seq0 = row0 // seq_len
    q0 = row0 // q_tile
    in_specs, args = [], []
    if sink is not None:
        in_specs.append(pl.BlockSpec(memory_space=pltpu.SMEM))
        args.append(sink)
    in_specs += [
        pl.BlockSpec((q_tile, qw), lambda s, n: (q0 + s * n_q + n, 0)),
        pl.BlockSpec((seq_len, kw), lambda s, n: (seq0 + s, 0)),
        pl.BlockSpec((seq_len, kw), lambda s, n: (seq0 + s, 0)),
    ]
    args += [q, k, v]
    if ctx is not None:
        p = ctx[0].shape[1]
        in_specs += [pl.BlockSpec((None, p, kw), lambda s, n: (s, 0, 0))] * 2
        args += list(ctx)
    aliases = {}
    if prev is not None:
        in_specs.append(pl.BlockSpec(memory_space=pl.ANY))
        aliases = {len(args): 0}
        args.append(prev)
    kern = functools.partial(
        _attn_kernel, kv_heads=kv_heads, groups=groups, dh=dh, has_sink=sink is not None,
        has_ctx=ctx is not None, window=window, q_tile=q_tile, seq_len=seq_len, chunk=min(seq_len, ATTN_KEY_CHUNK))
    return pl.pallas_call(
        kern,
        out_shape=jax.ShapeDtypeStruct((t, qw), BF16),
        grid=(n_seq, n_q),
        in_specs=in_specs,
        out_specs=pl.BlockSpec((q_tile, qw), lambda s, n: (q0 + s * n_q + n, 0)),
        input_output_aliases=aliases,
        compiler_params=_cparams("arbitrary", "arbitrary"),
        name="attention",
    )(*args)


def _out_route_kernel(*refs, n_mix, n_x, n_ptiles):
    mix_refs = refs[:n_mix]
    w_refs = refs[n_mix:2 * n_mix]
    x_refs = refs[2 * n_mix:2 * n_mix + n_x]
    (gate_ref, g2_ref, sh2_ref, sc2_ref, rw_ref, rb_ref,
     xo_ref, h_ref, pos_ref, gates_ref, rows_ref, off_ref) = refs[2 * n_mix + n_x:]

    acc = None
    for m_ref, w_ref in zip(mix_refs, w_refs):
        part = _dot(m_ref[...], w_ref[...])
        acc = part if acc is None else acc + part
    xn = _stream_tile(x_refs, n_ptiles) + gate_ref[...] * acc
    xo_ref[...] = xn
    h = _rms(xn, g2_ref[...]) * (1.0 + sc2_ref[...]) + sh2_ref[...]
    hb = h.astype(BF16)
    h_ref[...] = hb

    logits = _dot(hb, rw_ref[...]) + rb_ref[...]
    tm, ne = logits.shape
    lane = lax.broadcasted_iota(I32, (tm, ne), 1).astype(F32)
    lane4 = lax.broadcasted_iota(I32, (tm, TOP_K), 1)
    work = logits
    sels, vals = [], []
    for _ in range(TOP_K):
        mx = jnp.max(work, axis=-1, keepdims=True)
        first = jnp.min(jnp.where(work == mx, lane, float(ne)), axis=-1, keepdims=True)
        sel = lane == first
        work = jnp.where(sel, -jnp.inf, work)
        sels.append(sel)
        vals.append(mx)
    exps = [jnp.exp(v - vals[0]) for v in vals]
    den = exps[0] + exps[1] + exps[2] + exps[3]

    onehot = jnp.zeros((tm, ne), F32)
    for sel in sels:
        onehot = onehot + sel.astype(F32)
    r_i = lax.broadcasted_iota(I32, (tm, tm), 0)
    c_i = lax.broadcasted_iota(I32, (tm, tm), 1)
    before = jnp.where(c_i < r_i, 1.0, 0.0).astype(BF16)
    earlier = _dot(before, onehot.astype(BF16))

    cnt = jnp.sum(onehot, axis=0, keepdims=True)
    seg8 = jnp.floor((cnt + (SEG_ALIGN - 1.0)) * (1.0 / SEG_ALIGN))
    e_r = lax.broadcasted_iota(I32, (ne, ne), 0)
    e_c = lax.broadcasted_iota(I32, (ne, ne), 1)
    upper = jnp.where(e_r < e_c, 1.0, 0.0).astype(BF16)
    off8 = _dot(jnp.broadcast_to(seg8, (SEG_ALIGN, ne)).astype(BF16), upper)[0:1]
    seg_off = off8 * SEG_ALIGN
    base = seg_off + earlier

    pos_o = jnp.zeros((tm, TOP_K), I32)
    gate_o = jnp.zeros((tm, TOP_K), F32)
    for k in range(TOP_K):
        pos_k = jnp.sum(jnp.where(sels[k], base, 0.0), axis=-1, keepdims=True)
        pos_o = jnp.where(lane4 == k, pos_k.astype(I32), pos_o)
        gate_o = jnp.where(lane4 == k, exps[k] / den, gate_o)
    pos_ref[...] = pos_o
    gates_ref[...] = gate_o
    rows_ref[0] = (seg8 * SEG_ALIGN).astype(I32)
    off_ref[0] = seg_off.astype(I32)


def _out_route(geom, layer, mixes, w_parts, x, mods, norm_g, router_w, router_b):
    t, d = geom.t, w_parts[0].shape[1]
    tm = TOKEN_TILE
    row = lambda i: (i, 0)
    const2 = lambda i: (0, 0)
    x_specs, x_args = _stream_specs(geom, x, d)
    in_specs = [pl.BlockSpec((tm, m.shape[1]), row) for m in mixes]
    in_specs += [pl.BlockSpec(w.shape, const2) for w in w_parts]
    in_specs += x_specs
    in_specs += [
        _mod_spec(geom, layer, 2, d),
        pl.BlockSpec((1, d), const2),
        _mod_spec(geom, layer, 3, d),
        _mod_spec(geom, layer, 4, d),
        pl.BlockSpec(router_w.shape, const2),
        pl.BlockSpec((1, N_EXPERTS), const2),
    ]
    seg3 = lambda i: (i, 0, 0)
    out_shape = [
        jax.ShapeDtypeStruct((t, d), F32),
        jax.ShapeDtypeStruct((t, d), BF16),
        jax.ShapeDtypeStruct((t, TOP_K), I32),
        jax.ShapeDtypeStruct((t, TOP_K), F32),
        jax.ShapeDtypeStruct((geom.n_tiles, 1, N_EXPERTS), I32),
        jax.ShapeDtypeStruct((geom.n_tiles, 1, N_EXPERTS), I32),
    ]
    out_specs = [
        pl.BlockSpec((tm, d), row),
        pl.BlockSpec((tm, d), row),
        pl.BlockSpec((tm, TOP_K), row),
        pl.BlockSpec((tm, TOP_K), row),
        pl.BlockSpec((1, 1, N_EXPERTS), seg3),
        pl.BlockSpec((1, 1, N_EXPERTS), seg3),
    ]
    return pl.pallas_call(
        functools.partial(_out_route_kernel, n_mix=len(mixes), n_x=len(x_args), n_ptiles=geom.n_ptiles),
        out_shape=out_shape,
        grid=(geom.n_tiles,),
        in_specs=in_specs,
        out_specs=out_specs,
        compiler_params=_cparams("arbitrary"),
        name="out_route",
    )(*mixes, *w_parts, *x_args, mods, norm_g, mods, mods, router_w, router_b)


def _pack_pairs(v):
    n = v.shape[1] // 2
    bits = lax.bitcast_convert_type(v, U32)
    return (bits[:, :n] & jnp.uint32(0xFFFF0000)) | (bits[:, n:] >> 16)


def _unpack_pairs(p):
    hi = lax.bitcast_convert_type(p & jnp.uint32(0xFFFF0000), F32)
    lo = lax.bitcast_convert_type(p << 16, F32)
    return jnp.concatenate([hi, lo], axis=1).astype(BF16)


def _planned_copies(i, plan_refs, make_copy):
    for size, width, (local_ref, slot_ref, count_ref) in zip(COPY_SIZES, PLAN_WIDTHS, plan_refs):
        def one(c, carry, size=size, width=width, local_ref=local_ref, slot_ref=slot_ref):
            a = i * width + c
            make_copy(pl.multiple_of(local_ref[a], SEG_ALIGN), pl.multiple_of(slot_ref[a], SEG_ALIGN), size).start()
            return carry

        lax.fori_loop(0, count_ref[i], one, 0)


def _copy_plan(rows, seg_off, seg_dst):
    plan = []
    experts = jnp.arange(N_EXPERTS, dtype=I32)
    for n, (size, width) in enumerate(zip(COPY_SIZES, PLAN_WIDTHS)):
        if n == 0:
            count, done = rows // size, jnp.zeros_like(rows)
        else:
            count, done = (rows % (2 * size)) // size, rows - rows % (2 * size)
        cum = jnp.cumsum(count, axis=1)
        j = jnp.arange(width, dtype=I32)
        owner = jnp.sum((cum[:, None, :] <= j[None, :, None]).astype(I32), axis=2)
        pick = (jnp.minimum(owner, N_EXPERTS - 1)[:, :, None] == experts[None, None, :]).astype(I32)
        take = lambda v: jnp.sum(pick * v[:, None, :], axis=2)
        within = (j[None, :] - take(cum - count)) * size
        plan += [(take(seg_off + done) + within).reshape(-1).astype(I32),
                 (take(seg_dst + done) + within).reshape(-1).astype(I32),
                 cum[:, -1].astype(I32)]
    return plan


def _wait_copies(n_rows, make_copy):
    def wait_big(c, carry):
        make_copy(0, 0, WAIT_CHUNK).wait()
        return carry
    lax.fori_loop(0, n_rows // WAIT_CHUNK, wait_big, 0)
    size = WAIT_CHUNK // 2
    while size >= SEG_ALIGN:
        @pl.when(n_rows % (2 * size) >= size)
        def _(size=size):
            make_copy(0, 0, size).wait()

        size //= 2


def _dispatch_kernel(*refs):
    plan_refs, (tot_ref, h_ref, pos_ref, xs_ref, sorted_ref, sem) = _split_plan(refs)
    i = pl.program_id(0)
    buf = i % 2
    tm = h_ref.shape[0]
    n_sorted = sorted_ref.shape[1]
    pos = pos_ref[...]
    lane = lax.broadcasted_iota(I32, (tm, n_sorted), 1)
    hit = jnp.zeros((tm, n_sorted), F32)
    for k in range(TOP_K):
        hit = jnp.where(pos[:, k:k + 1] == lane, 1.0, hit)
    sorted_ref[buf] = _pack_pairs(lax.dot_general(hit.astype(BF16), h_ref[...], (((0,), (0,)), ((), ())),
                                                  preferred_element_type=F32))

    def copies_from(b):
        def make_copy(local, slot, rows):
            return pltpu.make_async_copy(sorted_ref.at[b, pl.ds(local, rows)], xs_ref.at[pl.ds(slot, rows)],
                                         sem.at[b])
        return make_copy

    _planned_copies(i, plan_refs, copies_from(buf))

    @pl.when(i > 0)
    def _():
        _wait_copies(tot_ref[jnp.maximum(i - 1, 0)], copies_from(1 - buf))

    @pl.when(i == pl.num_programs(0) - 1)
    def _():
        _wait_copies(tot_ref[i], copies_from(buf))


def _split_plan(refs):
    n = 3 * len(COPY_SIZES)
    return [refs[k:k + 3] for k in range(0, n, 3)], refs[n:]


def _dispatch(h, pos, plan, tile_rows, n_slots):
    t, d = h.shape
    tm = TOKEN_TILE
    return pl.pallas_call(
        _dispatch_kernel,
        out_shape=jax.ShapeDtypeStruct((n_slots, d // 2), U32),
        grid_spec=pltpu.PrefetchScalarGridSpec(
            num_scalar_prefetch=len(plan) + 1,
            grid=(t // tm,),
            in_specs=[
                pl.BlockSpec((tm, d), lambda i, *_: (i, 0)),
                pl.BlockSpec((tm, TOP_K), lambda i, *_: (i, 0)),
            ],
            out_specs=pl.BlockSpec(memory_space=pl.ANY),
            scratch_shapes=[pltpu.VMEM((2, SORTED_ROWS, d // 2), U32), pltpu.SemaphoreType.DMA((2,))],
        ),
        compiler_params=_cparams("arbitrary"),
        name="moe_dispatch",
    )(*plan, tile_rows, h, pos)


def _ffn_kernel(rows_ref, start_ref, wgu_ref, bgu_ref, wd_ref, bd_ref, xs_ref, ys_ref,
                wgu_bf, wd_bf, xbuf, ybuf, sem_in, sem_out):
    e = pl.program_id(0)
    d_ff = wd_ref.shape[1]
    tb = xbuf.shape[1]
    n_rows = rows_ref[e]
    n_tiles = (n_rows + tb - 1) // tb
    base = start_ref[e]

    chunk = 128
    def cast_gu(c, carry):
        r = pl.multiple_of(c * chunk, chunk)
        wgu_bf[pl.ds(r, chunk), :] = wgu_ref[0, pl.ds(r, chunk), :].astype(BF16)
        return carry
    lax.fori_loop(0, wgu_ref.shape[1] // chunk, cast_gu, 0)
    def cast_d(c, carry):
        r = pl.multiple_of(c * chunk, chunk)
        wd_bf[pl.ds(r, chunk), :] = wd_ref[0, pl.ds(r, chunk), :].astype(BF16)
        return carry
    lax.fori_loop(0, d_ff // chunk, cast_d, 0)

    half = tb // 2
    last_small = (n_rows - (n_tiles - 1) * tb) <= half
    n_rounds = (n_tiles + 1) // 2
    n_buf = xbuf.shape[0]

    def x_copy(s):
        r = pl.multiple_of(base + s * tb, EXPERT_TILE)
        return pltpu.make_async_copy(xs_ref.at[pl.ds(r, tb)], xbuf.at[s % n_buf], sem_in.at[s % n_buf])

    def y_copy(s, rows=tb):
        r = pl.multiple_of(base + s * tb, EXPERT_TILE)
        return pltpu.make_async_copy(ybuf.at[s % n_buf, pl.ds(0, rows)], ys_ref.at[pl.ds(r, rows)],
                                     sem_out.at[s % n_buf])

    def is_small(s):
        return jnp.logical_and(s == n_tiles - 1, last_small)

    def ffn_rows(s, n):
        rows = s * tb + lax.broadcasted_iota(I32, (n, 1), 0)
        x = _unpack_pairs(jnp.where(rows < n_rows, xbuf[s % n_buf, pl.ds(0, n), :], jnp.uint32(0)))
        gu = _dot(x, wgu_bf[...]) + bgu_ref[0]
        gate = jnp.minimum(gu[:, :d_ff], SWIGLU_LIMIT)
        up = jnp.clip(gu[:, d_ff:], -SWIGLU_LIMIT, SWIGLU_LIMIT)
        act = (up + 1.0) * (gate * (1.0 / (1.0 + jnp.exp(-SWIGLU_ALPHA * gate))))
        y = _dot(act.astype(BF16), wd_bf[...]) + bd_ref[0]
        ybuf[s % n_buf, pl.ds(0, n), :] = _pack_pairs(y.astype(BF16).astype(F32))

    def one_tile(s):
        @pl.when(jnp.logical_not(is_small(s)))
        def _():
            ffn_rows(s, tb)
            y_copy(s).start()

        @pl.when(is_small(s))
        def _():
            ffn_rows(s, half)
            y_copy(s, half).start()

    for k in range(2):
        @pl.when(k < n_tiles)
        def _(k=k):
            x_copy(k).start()

    def two_tiles(r, carry):
        a = 2 * r
        b = a + 1
        x_copy(a).wait()

        @pl.when(b < n_tiles)
        def _():
            x_copy(b).wait()

        for k in (2, 3):
            @pl.when(a + k < n_tiles)
            def _(k=k):
                x_copy(a + k).start()

        @pl.when(r >= 2)
        def _():
            y_copy(a - 4).wait()
            y_copy(a - 3).wait()

        both_full = jnp.logical_and(b < n_tiles, jnp.logical_not(is_small(b)))

        @pl.when(both_full)
        def _():
            ffn_rows(a, tb)
            ffn_rows(b, tb)
            y_copy(a).start()
            y_copy(b).start()

        @pl.when(jnp.logical_not(both_full))
        def _():
            one_tile(a)

            @pl.when(b < n_tiles)
            def _():
                one_tile(b)

        return carry

    lax.fori_loop(0, n_rounds, two_tiles, 0)

    for k in range(4):
        t = 2 * (n_rounds - 2) + k

        @pl.when(jnp.logical_and(t >= 0, t < n_tiles))
        def _(t=t):
            @pl.when(jnp.logical_not(is_small(t)))
            def _():
                y_copy(t).wait()

            @pl.when(is_small(t))
            def _():
                y_copy(t, half).wait()


def _expert_ffn(layer, xs, expert_rows, expert_start, w_gu, b_gu, w_down, b_down):
    n_slots, packed_w = xs.shape
    tb = FFN_TILE
    depth, ne, d, two_f = w_gu.shape
    d_ff = two_f // 2
    exp4 = lambda e, *_: (layer, e, 0, 0)
    return pl.pallas_call(
        _ffn_kernel,
        out_shape=jax.ShapeDtypeStruct((n_slots, packed_w), U32),
        grid_spec=pltpu.PrefetchScalarGridSpec(
            num_scalar_prefetch=2,
            grid=(ne,),
            in_specs=[
                pl.BlockSpec((None, 1, d, two_f), exp4),
                pl.BlockSpec((None, 1, 1, two_f), exp4),
                pl.BlockSpec((None, 1, d_ff, d), exp4),
                pl.BlockSpec((None, 1, 1, d), exp4),
                pl.BlockSpec(memory_space=pl.ANY),
            ],
            out_specs=pl.BlockSpec(memory_space=pl.ANY),
            scratch_shapes=[
                pltpu.VMEM((d, two_f), BF16), pltpu.VMEM((d_ff, d), BF16),
                pltpu.VMEM((4, tb, packed_w), U32), pltpu.VMEM((4, tb, packed_w), U32),
                pltpu.SemaphoreType.DMA((4,)), pltpu.SemaphoreType.DMA((4,)),
            ],
        ),
        compiler_params=_cparams("arbitrary"),
        name="expert_ffn",
    )(expert_rows, expert_start, w_gu, b_gu.reshape(depth, ne, 1, two_f),
      w_down, b_down.reshape(depth, ne, 1, d), xs)


def _combine_kernel(*refs, final, n_ptiles):
    plan_refs, (tot_ref, x_ref, pos_ref, gates_ref, mg_ref, *rest) = _split_plan(refs)
    if final:
        fn_ref, ys_ref, op_ref, os_ref, buf, sem = rest
    else:
        ys_ref, o_ref, buf, sem = rest
    i = pl.program_id(0)
    cur = i % 2
    tm = x_ref.shape[0]
    n_sorted = buf.shape[1]

    def copies_into(b):
        def make_copy(local, slot, rows):
            return pltpu.make_async_copy(ys_ref.at[pl.ds(slot, rows)], buf.at[b, pl.ds(local, rows)], sem.at[b])
        return make_copy

    @pl.when(i == 0)
    def _():
        buf[...] = jnp.zeros_like(buf)
        _planned_copies(i, plan_refs, copies_into(cur))

    @pl.when(i + 1 < pl.num_programs(0))
    def _():
        _planned_copies(i + 1, plan_refs, copies_into(1 - cur))

    pos = pos_ref[...]
    g = gates_ref[...]
    lane = lax.broadcasted_iota(I32, (tm, n_sorted), 1)
    weight = jnp.zeros((tm, n_sorted), F32)
    for k in range(TOP_K):
        weight = jnp.where(pos[:, k:k + 1] == lane, g[:, k:k + 1], weight)
    _wait_copies(tot_ref[i], copies_into(cur))
    y = _dot(weight.astype(BF16), _unpack_pairs(buf[cur]))
    xn = x_ref[...] + mg_ref[...] * y
    if final:
        xn = _rms(xn, fn_ref[...])

        @pl.when(i < n_ptiles)
        def _():
            op_ref[...] = xn

        @pl.when(i >= n_ptiles)
        def _():
            os_ref[...] = xn
    else:
        o_ref[...] = xn


def _combine(geom, layer, ys, pos, plan, tile_rows, x, gates, mods, final_g):
    t, d = x.shape
    tm = TOKEN_TILE
    final = final_g is not None

    def mod_imap(i, *_):
        return ((layer * MOD_ROWS + geom.group(i)) * N_MOD + 5, 0, 0)

    row = lambda i, *_: (i, 0)
    in_specs = [
        pl.BlockSpec((tm, d), row),
        pl.BlockSpec((tm, TOP_K), row),
        pl.BlockSpec((tm, TOP_K), row),
        pl.BlockSpec((None, 1, d), mod_imap),
    ]
    args = [*plan, tile_rows, x, pos, gates, mods]
    if final:
        in_specs.append(pl.BlockSpec((1, d), lambda i, *_: (0, 0)))
        args.append(final_g)
    in_specs.append(pl.BlockSpec(memory_space=pl.ANY))
    args.append(ys)
    if final:
        n_pt = geom.n_ptiles
        out_shape = [jax.ShapeDtypeStruct((geom.tp, d), F32), jax.ShapeDtypeStruct((t - geom.tp, d), F32)]
        out_specs = [pl.BlockSpec((tm, d), lambda i, *_: (jnp.minimum(i, n_pt - 1), 0)),
                     pl.BlockSpec((tm, d), lambda i, *_: (jnp.maximum(i - n_pt, 0), 0))]
    else:
        out_shape = jax.ShapeDtypeStruct((t, d), F32)
        out_specs = pl.BlockSpec((tm, d), row)
    return pl.pallas_call(
        functools.partial(_combine_kernel, final=final, n_ptiles=geom.n_ptiles),
        out_shape=out_shape,
        grid_spec=pltpu.PrefetchScalarGridSpec(
            num_scalar_prefetch=len(plan) + 1,
            grid=(t // tm,),
            in_specs=in_specs,
            out_specs=out_specs,
            scratch_shapes=[pltpu.VMEM((2, SORTED_ROWS, d // 2), U32), pltpu.SemaphoreType.DMA((2,))],
        ),
        compiler_params=_cparams("arbitrary"),
        name="moe_combine",
    )(*args)


def _moe(geom, layer, h, pos, gates, seg_rows, seg_off, x, mods, w_gu, b_gu, w_down, b_down, final_g):
    t = h.shape[0]
    tb = EXPERT_TILE
    n_tok_tiles = seg_rows.shape[0]
    max_rows = t * TOP_K + n_tok_tiles * N_EXPERTS * (SEG_ALIGN - 1)
    n_blocks = -(-max_rows // tb) + N_EXPERTS
    rows = seg_rows[:, 0, :]
    cnt = jnp.sum(rows, axis=0)
    n_tiles_e = (cnt + tb - 1) // tb
    tile_end = jnp.cumsum(n_tiles_e)
    tile_start = tile_end - n_tiles_e
    expert_start = (tile_start * tb).astype(I32)
    seg_dst = expert_start[None, :] + jnp.cumsum(rows, axis=0) - rows
    plan = _copy_plan(rows, seg_off[:, 0, :], seg_dst)
    tile_rows = jnp.sum(rows, axis=1).astype(I32)
    xs = _dispatch(h, pos, plan, tile_rows, n_blocks * tb + FFN_TILE - tb)
    ys = _expert_ffn(layer, xs, cnt.astype(I32), expert_start, w_gu, b_gu, w_down, b_down)
    return _combine(geom, layer, ys, pos, plan, tile_rows, x, gates, mods, final_g)


def kernel(x_prompt, x_sample, cache_b_k, cache_b_v, cache_c_k, cache_c_v, c, c_ctx,
           mod_w, mod_b, norm_mix, norm_ffn, even_w_in, even_w_out, even_sink,
           odd_w_in, odd_w_out, odd_q_norm, odd_k_norm, router_w, router_b,
           moe_w_gu, moe_b_gu, moe_w_down, moe_b_down, final_norm):
    bp, lp, d = x_prompt.shape
    bs, ls, _ = x_sample.shape
    past = cache_b_k.shape[2]
    depth = mod_w.shape[0]
    geom = _Geom(bp, lp, bs, ls)
    tp = geom.tp

    x = (x_prompt.reshape(tp, d), x_sample.reshape(bs * ls, d))
    cond = jnp.concatenate([c_ctx[None, :], c, jnp.zeros((MOD_ROWS - 1 - bs, d), F32)], axis=0)
    mods = _modulation(cond, mod_w, mod_b).reshape(depth * MOD_ROWS * N_MOD, 1, d)

    cn, sn = _dft_tables(A_GROUP_DIM)
    dft_chan = jnp.asarray(np.concatenate([cn, sn], axis=1), BF16)
    dft_p = [jnp.asarray(m, BF16) for m in _dft_tables(lp)]
    dft_s = [jnp.asarray(m, BF16) for m in _dft_tables(ls)]
    rope_b = [jnp.asarray(m) for m in _rope_tables(TOKEN_TILE, ls, B_HEAD_DIM)]
    rope_c = [jnp.asarray(m) for m in _rope_tables(TOKEN_TILE, ls, C_HEAD_DIM)]

    states = {"bk": [], "bv": [], "ck": [], "cv": []}
    for layer in range(depth):
        j = layer // 2
        g_mix = norm_mix[layer][None, :]
        g_ffn = norm_ffn[layer][None, :]
        if layer % 2 == 0:
            pair = _paired_head_order(B_KV_HEADS, B_HEADS // B_KV_HEADS, B_HEAD_DIM)
            w_in = even_w_in[j]
            w_in = jnp.concatenate([w_in[:, :A_WIDTH], w_in[:, A_WIDTH:A_WIDTH + B_Q_WIDTH][:, pair],
                                    w_in[:, A_WIDTH + B_Q_WIDTH:]], axis=1).astype(BF16)
            tc, ts, q, k, v, k_state, v_state = _in_projection(
                geom, layer, x, g_mix, mods, w_in, rope_b[0], rope_b[1], [dft_chan],
                _proj_even_kernel, (A_WIDTH, A_WIDTH, B_Q_WIDTH, B_KV_WIDTH, B_KV_WIDTH), B_KV_WIDTH, "proj_even")
            states["bk"].append(k_state.reshape(bp, lp, B_KV_HEADS, B_HEAD_DIM))
            states["bv"].append(v_state.reshape(bp, lp, B_KV_HEADS, B_HEAD_DIM))
            four = _fourier_tokens(tc, ts, dft_p[0], dft_p[1], bp, lp, 0, None)
            four = _fourier_tokens(tc, ts, dft_s[0], dft_s[1], bs, ls, tp, four)
            sink = even_sink[j]
            common = dict(kv_heads=B_KV_HEADS, groups=B_HEADS // B_KV_HEADS, dh=B_HEAD_DIM)
            att = _attention(q, k, v, None, sink, None, n_seq=bp, seq_len=lp, row0=0, q_tile=lp,
                             window=None, **common)
            ctx = (cache_b_k[:, j].reshape(bs, past, B_KV_WIDTH).astype(BF16),
                   cache_b_v[:, j].reshape(bs, past, B_KV_WIDTH).astype(BF16))
            att = _attention(q, k, v, ctx, sink, att, n_seq=bs, seq_len=ls, row0=tp, q_tile=ATTN_Q_TILE,
                             window=WINDOW, **common)
            w_out = even_w_out[j].astype(BF16)
            mixes = [four, att]
            w_parts = [w_out[:A_WIDTH], w_out[A_WIDTH:][pair]]
        else:
            q, k, v, k_state, v_state = _in_projection(
                geom, layer, x, g_mix, mods, odd_w_in[j].astype(BF16), rope_c[0], rope_c[1],
                [odd_q_norm[j][None, :], odd_k_norm[j][None, :]],
                _proj_odd_kernel, (C_Q_WIDTH, C_KV_WIDTH, C_KV_WIDTH), C_KV_WIDTH, "proj_odd")
            states["ck"].append(k_state.reshape(bp, lp, C_KV_HEADS, C_HEAD_DIM))
            states["cv"].append(v_state.reshape(bp, lp, C_KV_HEADS, C_HEAD_DIM))
            common = dict(kv_heads=C_KV_HEADS, groups=C_HEADS // C_KV_HEADS, dh=C_HEAD_DIM, window=None)
            att = _attention(q, k, v, None, None, None, n_seq=bp, seq_len=lp, row0=0, q_tile=lp, **common)
            ctx = (cache_c_k[:, j].reshape(bs, past, C_KV_WIDTH).astype(BF16),
                   cache_c_v[:, j].reshape(bs, past, C_KV_WIDTH).astype(BF16))
            att = _attention(q, k, v, ctx, None, att, n_seq=bs, seq_len=ls, row0=tp, q_tile=ATTN_Q_TILE, **common)
            mixes = [att]
            w_parts = [odd_w_out[j].astype(BF16)]
        x, h, pos, gates, seg_rows, seg_off = _out_route(
            geom, layer, mixes, w_parts, x, mods, g_ffn, router_w[layer].astype(BF16), router_b[layer][None, :])
        final_g = final_norm[None, :] if layer == depth - 1 else None
        x = _moe(geom, layer, h, pos, gates, seg_rows, seg_off, x, mods,
                 moe_w_gu, moe_b_gu, moe_w_down, moe_b_down, final_g)

    y_prompt = x[0].reshape(bp, lp, d)
    y_sample = x[1].reshape(bs, ls, d)
    return (y_prompt, y_sample,
            jnp.stack(states["bk"], axis=1), jnp.stack(states["bv"], axis=1),
            jnp.stack(states["ck"], axis=1), jnp.stack(states["cv"], axis=1))
```

```python
import functools

import numpy as np
import jax
import jax.numpy as jnp
from jax import lax
from jax.experimental import pallas as pl
from jax.experimental.pallas import tpu as pltpu

F32 = jnp.float32
BF16 = jnp.bfloat16
I32 = jnp.int32
U32 = jnp.uint32

GRID_W = 64
A_GROUPS = 4
A_GROUP_DIM = 128
A_WIDTH = A_GROUPS * A_GROUP_DIM
B_HEADS = 8
B_KV_HEADS = 2
B_HEAD_DIM = 64
B_Q_WIDTH = B_HEADS * B_HEAD_DIM
B_KV_WIDTH = B_KV_HEADS * B_HEAD_DIM
WINDOW = 128
C_HEADS = 8
C_KV_HEADS = 2
C_HEAD_DIM = 128
C_Q_WIDTH = C_HEADS * C_HEAD_DIM
C_KV_WIDTH = C_KV_HEADS * C_HEAD_DIM
ROPE_THETA = 10000.0
N_EXPERTS = 32
TOP_K = 4
SWIGLU_LIMIT = 7.0
SWIGLU_ALPHA = 1.702
EPS = 1e-6

LANES = 128
TOKEN_TILE = 256
EXPERT_TILE = 128
FFN_TILE = 256
ATTN_Q_TILE = 128
DENSE_Q_TILE = 256
ATTN_KEY_CHUNK = 1024
SEG_ALIGN = 8
COPY_SIZES = (32, 16, 8)
WAIT_CHUNK = 256
SORTED_ROWS = -(-(TOKEN_TILE * TOP_K + N_EXPERTS * (SEG_ALIGN - 1)) // LANES) * LANES
PLAN_WIDTHS = (SORTED_ROWS // COPY_SIZES[0],) + (N_EXPERTS,) * (len(COPY_SIZES) - 1)
VMEM_LIMIT = 56 * 1024 * 1024
MASKED = -1e30
N_MOD = 6
MOD_ROWS = 8


def _cparams(*sem):
    return pltpu.CompilerParams(dimension_semantics=tuple(sem), vmem_limit_bytes=VMEM_LIMIT)


def _dot(a, b):
    return jnp.dot(a, b, preferred_element_type=F32)


def _dot_nt(a, b):
    return lax.dot_general(a, b, (((1,), (1,)), ((), ())), preferred_element_type=F32)


def _rms(x, g):
    return x * lax.rsqrt(jnp.mean(x * x, axis=-1, keepdims=True) + EPS) * g


def _dft_tables(n):
    j = np.arange(n, dtype=np.int64)
    ang = 2.0 * np.pi * ((j[:, None] * j[None, :]) % n).astype(np.float64) / n
    s = 1.0 / np.sqrt(n)
    return np.cos(ang) * s, np.sin(ang) * s


def _paired_head_order(kv_heads, groups, dh):
    assert kv_heads * dh == LANES
    return np.array([(kv * groups + g) * dh + d for g in range(groups) for kv in range(kv_heads) for d in range(dh)])


def _rope_tables(n_prompt_rows, n_latent, head_dim):
    quarter = head_dim // 4
    pos = np.arange(n_latent)
    row = (pos // GRID_W).astype(np.float32)
    col = (pos % GRID_W).astype(np.float32)
    inv = (np.float32(ROPE_THETA) ** (-np.arange(quarter, dtype=np.float32) / np.float32(quarter))).astype(np.float32)
    ang_row = (row[:, None] * inv[None, :]).astype(np.float32)
    ang_col = (col[:, None] * inv[None, :]).astype(np.float32)
    cos_h = np.concatenate([np.cos(ang_row)] * 2 + [np.cos(ang_col)] * 2, axis=1)
    sin_h = np.concatenate([-np.sin(ang_row), np.sin(ang_row), -np.sin(ang_col), np.sin(ang_col)], axis=1)
    reps = LANES // head_dim
    cos_l = np.tile(cos_h, (1, reps)).astype(np.float32)
    sin_l = np.tile(sin_h, (1, reps)).astype(np.float32)
    cos = np.concatenate([np.ones((n_prompt_rows, LANES), np.float32), cos_l], axis=0)
    sin = np.concatenate([np.zeros((n_prompt_rows, LANES), np.float32), sin_l], axis=0)
    return cos, sin


def _rope(x, cos, sin, quarter):
    lane = lax.broadcasted_iota(I32, (x.shape[0], LANES), 1)
    first = ((lane // quarter) % 2) == 0
    outs = []
    for c in range(x.shape[1] // LANES):
        xc = x[:, c * LANES:(c + 1) * LANES]
        partner = jnp.where(first, pltpu.roll(xc, LANES - quarter, 1), pltpu.roll(xc, quarter, 1))
        outs.append(xc * cos + partner * sin)
    return outs[0] if len(outs) == 1 else jnp.concatenate(outs, axis=1)


def _head_rms(x, g):
    outs = []
    for c in range(x.shape[1] // LANES):
        outs.append(_rms(x[:, c * LANES:(c + 1) * LANES], g))
    return outs[0] if len(outs) == 1 else jnp.concatenate(outs, axis=1)


def _mod_kernel(c_ref, w_ref, b_ref, o_ref):
    c = c_ref[...]
    s = c * (1.0 / (1.0 + jnp.exp(-c)))
    o_ref[0] = _dot(s.astype(BF16), w_ref[0].astype(BF16)) + b_ref[0]


def _modulation(cond, mod_w, mod_b):
    depth, d, n = mod_w.shape
    tn = 1536
    return pl.pallas_call(
        _mod_kernel,
        out_shape=jax.ShapeDtypeStruct((depth, MOD_ROWS, n), F32),
        grid=(depth, n // tn),
        in_specs=[
            pl.BlockSpec((MOD_ROWS, d), lambda l, j: (0, 0)),
            pl.BlockSpec((1, d, tn), lambda l, j: (l, 0, j)),
            pl.BlockSpec((1, 1, tn), lambda l, j: (l, 0, j)),
        ],
        out_specs=pl.BlockSpec((1, MOD_ROWS, tn), lambda l, j: (l, 0, j)),
        compiler_params=_cparams("arbitrary", "arbitrary"),
        name="modulation",
    )(cond, mod_w, mod_b.reshape(depth, 1, n))


class _Geom:
    def __init__(self, bp, lp, bs, ls):
        self.bp, self.lp, self.bs, self.ls = bp, lp, bs, ls
        self.tp = bp * lp
        self.t = bp * lp + bs * ls
        assert lp == TOKEN_TILE and ls % TOKEN_TILE == 0 and self.tp % ls == 0
        self.n_ptiles = self.tp // TOKEN_TILE
        self.tiles_per_lat = ls // TOKEN_TILE
        self.n_tiles = self.t // TOKEN_TILE

    def group(self, i):
        return jnp.where(i < self.n_ptiles, 0, 1 + (i - self.n_ptiles) // self.tiles_per_lat)

    def pos_block(self, i):
        return jnp.where(i < self.n_ptiles, 0, 1 + (i - self.n_ptiles) % self.tiles_per_lat)


def _mod_spec(geom, layer, which, d):
    def imap(i):
        return ((layer * MOD_ROWS + geom.group(i)) * N_MOD + which, 0, 0)
    return pl.BlockSpec((None, 1, d), imap)


def _stream_specs(geom, x, d):
    tm = TOKEN_TILE
    if isinstance(x, tuple):
        return ([pl.BlockSpec((tm, d), lambda i, *_: (jnp.minimum(i, geom.n_ptiles - 1), 0)),
                 pl.BlockSpec((tm, d), lambda i, *_: (jnp.maximum(i - geom.n_ptiles, 0), 0))], list(x))
    return [pl.BlockSpec((tm, d), lambda i, *_: (i, 0))], [x]


def _stream_tile(x_refs, n_ptiles):
    if len(x_refs) == 1:
        return x_refs[0][...]
    return jnp.where(pl.program_id(0) < n_ptiles, x_refs[0][...], x_refs[1][...])


def _store_kv(k, v, kb_ref, vb_ref, ks_ref, vs_ref, n_ptiles):
    kb_ref[...] = k.astype(BF16)
    vb_ref[...] = v.astype(BF16)

    @pl.when(pl.program_id(0) < n_ptiles)
    def _():
        ks_ref[...] = k
        vs_ref[...] = v


def _proj_even_kernel(*refs, n_x, n_ptiles):
    x_refs = refs[:n_x]
    (g_ref, sh_ref, sc_ref, w_ref, cos_ref, sin_ref, dft_ref,
     tc_ref, ts_ref, q_ref, kb_ref, vb_ref, ks_ref, vs_ref) = refs[n_x:]
    h = _rms(_stream_tile(x_refs, n_ptiles), g_ref[...]) * (1.0 + sc_ref[...]) + sh_ref[...]
    p = _dot(h.astype(BF16), w_ref[...])
    cos = cos_ref[...]
    sin = sin_ref[...]
    dft = dft_ref[...]
    tcs, tss = [], []
    for g in range(A_GROUPS):
        t = _dot(p[:, g * A_GROUP_DIM:(g + 1) * A_GROUP_DIM].astype(BF16), dft)
        tcs.append(t[:, :A_GROUP_DIM])
        tss.append(t[:, A_GROUP_DIM:])
    tc_ref[...] = jnp.concatenate(tcs, axis=1).astype(BF16)
    ts_ref[...] = jnp.concatenate(tss, axis=1).astype(BF16)
    o = A_WIDTH
    q = _rope(p[:, o:o + B_Q_WIDTH], cos, sin, B_HEAD_DIM // 4)
    q_ref[...] = (q * B_HEAD_DIM ** -0.5).astype(BF16)
    o += B_Q_WIDTH
    k = _rope(p[:, o:o + B_KV_WIDTH], cos, sin, B_HEAD_DIM // 4)
    o += B_KV_WIDTH
    _store_kv(k, p[:, o:o + B_KV_WIDTH], kb_ref, vb_ref, ks_ref, vs_ref, n_ptiles)


def _proj_odd_kernel(*refs, n_x, n_ptiles):
    x_refs = refs[:n_x]
    (g_ref, sh_ref, sc_ref, w_ref, cos_ref, sin_ref, qn_ref, kn_ref,
     q_ref, kb_ref, vb_ref, ks_ref, vs_ref) = refs[n_x:]
    h = _rms(_stream_tile(x_refs, n_ptiles), g_ref[...]) * (1.0 + sc_ref[...]) + sh_ref[...]
    p = _dot(h.astype(BF16), w_ref[...])
    cos = cos_ref[...]
    sin = sin_ref[...]
    q = _head_rms(p[:, :C_Q_WIDTH], qn_ref[...])
    k = _head_rms(p[:, C_Q_WIDTH:C_Q_WIDTH + C_KV_WIDTH], kn_ref[...])
    q_ref[...] = (_rope(q, cos, sin, C_HEAD_DIM // 4) * C_HEAD_DIM ** -0.5).astype(BF16)
    k = _rope(k, cos, sin, C_HEAD_DIM // 4)
    _store_kv(k, p[:, C_Q_WIDTH + C_KV_WIDTH:], kb_ref, vb_ref, ks_ref, vs_ref, n_ptiles)


def _in_projection(geom, layer, x, norm_g, mods, w, cos, sin, extras, kernel, out_widths, kv_width, name):
    t, d = geom.t, w.shape[0]
    tm = TOKEN_TILE
    n_out = w.shape[1]
    row = lambda i: (i, 0)
    const2 = lambda i: (0, 0)
    x_specs, x_args = _stream_specs(geom, x, d)
    in_specs = x_specs + [
        pl.BlockSpec((1, d), const2),
        _mod_spec(geom, layer, 0, d),
        _mod_spec(geom, layer, 1, d),
        pl.BlockSpec((d, n_out), const2),
        pl.BlockSpec((tm, LANES), lambda i: (geom.pos_block(i), 0)),
        pl.BlockSpec((tm, LANES), lambda i: (geom.pos_block(i), 0)),
    ] + [pl.BlockSpec(e.shape, const2) for e in extras]
    return pl.pallas_call(
        functools.partial(kernel, n_x=len(x_args), n_ptiles=geom.n_ptiles),
        out_shape=([jax.ShapeDtypeStruct((t, wd), BF16) for wd in out_widths]
                   + [jax.ShapeDtypeStruct((geom.tp, kv_width), F32)] * 2),
        grid=(geom.n_tiles,),
        in_specs=in_specs,
        out_specs=([pl.BlockSpec((tm, wd), row) for wd in out_widths]
                   + [pl.BlockSpec((tm, kv_width), lambda i: (jnp.minimum(i, geom.n_ptiles - 1), 0))] * 2),
        compiler_params=_cparams("arbitrary"),
        name=name,
    )(*x_args, norm_g, mods, mods, w, cos, sin, *extras)


def _fourier_kernel(cl_ref, sl_ref, tc_ref, ts_ref, *rest):
    o_ref = rest[-1]
    o_ref[...] = (_dot(cl_ref[...], tc_ref[...]) - _dot(sl_ref[...], ts_ref[...])).astype(o_ref.dtype)


def _fourier_tokens(tc, ts, cl, sl, n_seq, seq_len, row0, prev):
    t, width = tc.shape
    tr = min(seq_len, 512)
    n_r = seq_len // tr
    assert row0 % seq_len == 0
    seq0 = row0 // seq_len
    out0 = row0 // tr
    in_specs = [
        pl.BlockSpec((tr, seq_len), lambda s, r: (r, 0)),
        pl.BlockSpec((tr, seq_len), lambda s, r: (r, 0)),
        pl.BlockSpec((seq_len, width), lambda s, r: (seq0 + s, 0)),
        pl.BlockSpec((seq_len, width), lambda s, r: (seq0 + s, 0)),
    ]
    args = [cl, sl, tc, ts]
    aliases = {}
    if prev is not None:
        in_specs.append(pl.BlockSpec(memory_space=pl.ANY))
        args.append(prev)
        aliases = {4: 0}
    return pl.pallas_call(
        _fourier_kernel,
        out_shape=jax.ShapeDtypeStruct((t, width), BF16),
        grid=(n_seq, n_r),
        in_specs=in_specs,
        out_specs=pl.BlockSpec((tr, width), lambda s, r: (out0 + s * n_r + r, 0)),
        input_output_aliases=aliases,
        compiler_params=_cparams("arbitrary", "arbitrary"),
        name="fourier_tokens",
    )(*args)


def _attend(q, chunks, sink, o0, dh, den_col):
    m = sink
    acc = None
    den = None
    for k, v, mask in chunks:
        s = _dot_nt(q, k)
        if mask is not None:
            s = jnp.where(mask, s, MASKED)
        m_new = jnp.max(s, axis=-1, keepdims=True)
        if m is not None:
            m_new = jnp.maximum(m, m_new)
        p = jnp.exp(s - m_new)
        pv = _dot(p.astype(BF16), v)
        if acc is None:
            acc = pv
            if den_col is None:
                den = jnp.sum(p, axis=-1, keepdims=True)
        else:
            alpha = jnp.exp(m - m_new)
            acc = alpha * acc + pv
            if den_col is None:
                den = alpha * den + jnp.sum(p, axis=-1, keepdims=True)
        m = m_new
    if den_col is not None:
        den = acc[:, den_col:den_col + 1]
    if sink is not None:
        den = den + jnp.exp(sink - m)
    return acc[:, o0:o0 + dh] / den


def _attend_two_pass(q, chunks, sink):
    scores = []
    m = sink
    for k, _, mask in chunks:
        s = _dot_nt(q, k)
        if mask is not None:
            s = jnp.where(mask, s, MASKED)
        scores.append(s)
        mx = jnp.max(s, axis=-1, keepdims=True)
        m = mx if m is None else jnp.maximum(m, mx)
    den = None if sink is None else jnp.exp(sink - m)
    acc = None
    for (_, v, _), s in zip(chunks, scores):
        e = jnp.exp(s - m)
        es = jnp.sum(e, axis=-1, keepdims=True)
        den = es if den is None else den + es
        o = _dot(e.astype(BF16), v)
        acc = o if acc is None else acc + o
    return acc / den


def _stack_heads(q, kv, groups, dh):
    return jnp.concatenate([q[:, (kv * groups + g) * dh:(kv * groups + g + 1) * dh] for g in range(groups)], axis=0)


def _sink_column(sink_ref, kv, groups, rows):
    return jnp.concatenate([jnp.full((rows, 1), sink_ref[kv * groups + g], F32) for g in range(groups)], axis=0)


def _head_values(v, kv, dh, with_ones):
    if not with_ones:
        return v[:, kv * dh:(kv + 1) * dh], 0, None
    assert dh == LANES
    lane = lax.broadcasted_iota(I32, (v.shape[0], LANES), 1)
    ones = jnp.where(lane == 0, 1.0, 0.0).astype(BF16)
    return jnp.concatenate([v[:, kv * dh:(kv + 1) * dh], ones], axis=1), 0, dh


def _paired_heads_attention(q, sources, sink_ref, groups, dh):
    rows = q.shape[0]
    half_q = lax.broadcasted_iota(I32, (rows, LANES), 1) // dh
    per_head = []
    for kv in range(2):
        qh = jnp.concatenate(
            [jnp.where(half_q == kv, q[:, g * LANES:(g + 1) * LANES], jnp.zeros((), q.dtype)) for g in range(groups)],
            axis=0)
        chunks = []
        for k, v, msk in sources:
            half_v = lax.broadcasted_iota(I32, v.shape, 1) // dh
            chunks.append((k, jnp.where(half_v == kv, v, jnp.zeros((), v.dtype)), msk))
        sink = _sink_column(sink_ref, kv, groups, rows) if sink_ref is not None else None
        per_head.append(_attend_two_pass(qh, chunks, sink))
    return jnp.concatenate(
        [per_head[0][g * rows:(g + 1) * rows] + per_head[1][g * rows:(g + 1) * rows] for g in range(groups)], axis=1)


def _attn_kernel(*refs, kv_heads, groups, dh, has_sink, has_ctx, window, q_tile, seq_len, chunk):
    refs = list(refs)
    sink_ref = refs.pop(0) if has_sink else None
    q_ref, k_ref, v_ref = refs[:3]
    ck_ref, cv_ref = (refs[3], refs[4]) if has_ctx else (None, None)
    o_ref = refs[-1]
    q = q_ref[...]
    rows = q.shape[0]
    if window is None:
        spans = [(c * chunk, chunk) for c in range(seq_len // chunk)]
        mask = None
    else:
        n = pl.program_id(1)
        band = q_tile + 2 * window
        start = pl.multiple_of(jnp.clip(n * q_tile - window, 0, seq_len - band), LANES)
        spans = [(start, band)]
        qpos = n * q_tile + lax.broadcasted_iota(I32, (groups * rows, band), 0) % rows
        kpos = start + lax.broadcasted_iota(I32, (groups * rows, band), 1)
        mask = jnp.abs(kpos - qpos) <= window
    sources = [(k_ref[pl.ds(s0, n_s), :], v_ref[pl.ds(s0, n_s), :], mask) for s0, n_s in spans]
    if has_ctx:
        sources.append((ck_ref[...], cv_ref[...], None))
    if 2 * dh == LANES:
        o_ref[...] = _paired_heads_attention(q, sources, sink_ref, groups, dh).astype(o_ref.dtype)
        return
    outs = []
    for kv in range(kv_heads):
        online = dh == LANES and len(sources) > 1
        chunks = []
        for k, v, msk in sources:
            vh, o0, den_col = _head_values(v, kv, dh, with_ones=online)
            chunks.append((k[:, kv * dh:(kv + 1) * dh], vh, msk))
        sink = _sink_column(sink_ref, kv, groups, rows) if has_sink else None
        qh = _stack_heads(q, kv, groups, dh)
        o = _attend(qh, chunks, sink, o0, dh, den_col) if online else _attend_two_pass(qh, chunks, sink)
        outs.extend(o[g * rows:(g + 1) * rows] for g in range(groups))
    o_ref[...] = jnp.concatenate(outs, axis=1).astype(o_ref.dtype)


def _attention(q, k, v, ctx, sink, prev, *, n_seq, seq_len, row0, q_tile, kv_heads, groups, dh, window):
    t, qw = q.shape
    kw = k.shape[1]
    n_q = seq_len // q_tile
    assert row0 % seq_len == 0 and row0 % q_tile == 0
    seq0 = row0 // seq_len
    q0 = row0 // q_tile
    in_specs, args = [], []
    if sink is not None:
        in_specs.append(pl.BlockSpec(memory_space=pltpu.SMEM))
        args.append(sink)
    in_specs += [
        pl.BlockSpec((q_tile, qw), lambda s, n: (q0 + s * n_q + n, 0)),
        pl.BlockSpec((seq_len, kw), lambda s, n: (seq0 + s, 0)),
        pl.BlockSpec((seq_len, kw), lambda s, n: (seq0 + s, 0)),
    ]
    args += [q, k, v]
    if ctx is not None:
        p = ctx[0].shape[1]
        in_specs += [pl.BlockSpec((None, p, kw), lambda s, n: (s, 0, 0))] * 2
        args += list(ctx)
    aliases = {}
    if prev is not None:
        in_specs.append(pl.BlockSpec(memory_space=pl.ANY))
        aliases = {len(args): 0}
        args.append(prev)
    kern = functools.partial(
        _attn_kernel, kv_heads=kv_heads, groups=groups, dh=dh, has_sink=sink is not None,
        has_ctx=ctx is not None, window=window, q_tile=q_tile, seq_len=seq_len, chunk=min(seq_len, ATTN_KEY_CHUNK))
    return pl.pallas_call(
        kern,
        out_shape=jax.ShapeDtypeStruct((t, qw), BF16),
        grid=(n_seq, n_q),
        in_specs=in_specs,
        out_specs=pl.BlockSpec((q_tile, qw), lambda s, n: (q0 + s * n_q + n, 0)),
        input_output_aliases=aliases,
        compiler_params=_cparams("arbitrary", "arbitrary"),
        name="attention",
    )(*args)


def _out_route_kernel(*refs, n_mix, n_x, n_ptiles):
    mix_refs = refs[:n_mix]
    w_refs = refs[n_mix:2 * n_mix]
    x_refs = refs[2 * n_mix:2 * n_mix + n_x]
    (gate_ref, g2_ref, sh2_ref, sc2_ref, rw_ref, rb_ref,
     xo_ref, h_ref, pos_ref, gates_ref, rows_ref, off_ref) = refs[2 * n_mix + n_x:]

    acc = None
    for m_ref, w_ref in zip(mix_refs, w_refs):
        part = _dot(m_ref[...], w_ref[...])
        acc = part if acc is None else acc + part
    xn = _stream_tile(x_refs, n_ptiles) + gate_ref[...] * acc
    xo_ref[...] = xn
    h = _rms(xn, g2_ref[...]) * (1.0 + sc2_ref[...]) + sh2_ref[...]
    hb = h.astype(BF16)
    h_ref[...] = hb

    logits = _dot(hb, rw_ref[...]) + rb_ref[...]
    tm, ne = logits.shape
    lane = lax.broadcasted_iota(I32, (tm, ne), 1).astype(F32)
    lane4 = lax.broadcasted_iota(I32, (tm, TOP_K), 1)
    work = logits
    sels, vals = [], []
    for _ in range(TOP_K):
        mx = jnp.max(work, axis=-1, keepdims=True)
        first = jnp.min(jnp.where(work == mx, lane, float(ne)), axis=-1, keepdims=True)
        sel = lane == first
        work = jnp.where(sel, -jnp.inf, work)
        sels.append(sel)
        vals.append(mx)
    exps = [jnp.exp(v - vals[0]) for v in vals]
    den = exps[0] + exps[1] + exps[2] + exps[3]

    onehot = jnp.zeros((tm, ne), F32)
    for sel in sels:
        onehot = onehot + sel.astype(F32)
    r_i = lax.broadcasted_iota(I32, (tm, tm), 0)
    c_i = lax.broadcasted_iota(I32, (tm, tm), 1)
    before = jnp.where(c_i < r_i, 1.0, 0.0).astype(BF16)
    earlier = _dot(before, onehot.astype(BF16))

    cnt = jnp.sum(onehot, axis=0, keepdims=True)
    seg8 = jnp.floor((cnt + (SEG_ALIGN - 1.0)) * (1.0 / SEG_ALIGN))
    e_r = lax.broadcasted_iota(I32, (ne, ne), 0)
    e_c = lax.broadcasted_iota(I32, (ne, ne), 1)
    upper = jnp.where(e_r < e_c, 1.0, 0.0).astype(BF16)
    off8 = _dot(jnp.broadcast_to(seg8, (SEG_ALIGN, ne)).astype(BF16), upper)[0:1]
    seg_off = off8 * SEG_ALIGN
    base = seg_off + earlier

    pos_o = jnp.zeros((tm, TOP_K), I32)
    gate_o = jnp.zeros((tm, TOP_K), F32)
    for k in range(TOP_K):
        pos_k = jnp.sum(jnp.where(sels[k], base, 0.0), axis=-1, keepdims=True)
        pos_o = jnp.where(lane4 == k, pos_k.astype(I32), pos_o)
        gate_o = jnp.where(lane4 == k, exps[k] / den, gate_o)
    pos_ref[...] = pos_o
    gates_ref[...] = gate_o
    rows_ref[0] = (seg8 * SEG_ALIGN).astype(I32)
    off_ref[0] = seg_off.astype(I32)


def _out_route(geom, layer, mixes, w_parts, x, mods, norm_g, router_w, router_b):
    t, d = geom.t, w_parts[0].shape[1]
    tm = TOKEN_TILE
    row = lambda i: (i, 0)
    const2 = lambda i: (0, 0)
    x_specs, x_args = _stream_specs(geom, x, d)
    in_specs = [pl.BlockSpec((tm, m.shape[1]), row) for m in mixes]
    in_specs += [pl.BlockSpec(w.shape, const2) for w in w_parts]
    in_specs += x_specs
    in_specs += [
        _mod_spec(geom, layer, 2, d),
        pl.BlockSpec((1, d), const2),
        _mod_spec(geom, layer, 3, d),
        _mod_spec(geom, layer, 4, d),
        pl.BlockSpec(router_w.shape, const2),
        pl.BlockSpec((1, N_EXPERTS), const2),
    ]
    seg3 = lambda i: (i, 0, 0)
    out_shape = [
        jax.ShapeDtypeStruct((t, d), F32),
        jax.ShapeDtypeStruct((t, d), BF16),
        jax.ShapeDtypeStruct((t, TOP_K), I32),
        jax.ShapeDtypeStruct((t, TOP_K), F32),
        jax.ShapeDtypeStruct((geom.n_tiles, 1, N_EXPERTS), I32),
        jax.ShapeDtypeStruct((geom.n_tiles, 1, N_EXPERTS), I32),
    ]
    out_specs = [
        pl.BlockSpec((tm, d), row),
        pl.BlockSpec((tm, d), row),
        pl.BlockSpec((tm, TOP_K), row),
        pl.BlockSpec((tm, TOP_K), row),
        pl.BlockSpec((1, 1, N_EXPERTS), seg3),
        pl.BlockSpec((1, 1, N_EXPERTS), seg3),
    ]
    return pl.pallas_call(
        functools.partial(_out_route_kernel, n_mix=len(mixes), n_x=len(x_args), n_ptiles=geom.n_ptiles),
        out_shape=out_shape,
        grid=(geom.n_tiles,),
        in_specs=in_specs,
        out_specs=out_specs,
        compiler_params=_cparams("arbitrary"),
        name="out_route",
    )(*mixes, *w_parts, *x_args, mods, norm_g, mods, mods, router_w, router_b)


def _pack_pairs(v):
    n = v.shape[1] // 2
    bits = lax.bitcast_convert_type(v, U32)
    return (bits[:, :n] & jnp.uint32(0xFFFF0000)) | (bits[:, n:] >> 16)


def _unpack_pairs(p):
    hi = lax.bitcast_convert_type(p & jnp.uint32(0xFFFF0000), F32)
    lo = lax.bitcast_convert_type(p << 16, F32)
    return jnp.concatenate([hi, lo], axis=1).astype(BF16)


def _planned_copies(i, plan_refs, make_copy):
    for size, width, (local_ref, slot_ref, count_ref) in zip(COPY_SIZES, PLAN_WIDTHS, plan_refs):
        def one(c, carry, size=size, width=width, local_ref=local_ref, slot_ref=slot_ref):
            a = i * width + c
            make_copy(pl.multiple_of(local_ref[a], SEG_ALIGN), pl.multiple_of(slot_ref[a], SEG_ALIGN), size).start()
            return carry

        lax.fori_loop(0, count_ref[i], one, 0)


def _copy_plan(rows, seg_off, seg_dst):
    plan = []
    experts = jnp.arange(N_EXPERTS, dtype=I32)
    for n, (size, width) in enumerate(zip(COPY_SIZES, PLAN_WIDTHS)):
        if n == 0:
            count, done = rows // size, jnp.zeros_like(rows)
        else:
            count, done = (rows % (2 * size)) // size, rows - rows % (2 * size)
        cum = jnp.cumsum(count, axis=1)
        j = jnp.arange(width, dtype=I32)
        owner = jnp.sum((cum[:, None, :] <= j[None, :, None]).astype(I32), axis=2)
        pick = (jnp.minimum(owner, N_EXPERTS - 1)[:, :, None] == experts[None, None, :]).astype(I32)
        take = lambda v: jnp.sum(pick * v[:, None, :], axis=2)
        within = (j[None, :] - take(cum - count)) * size
        plan += [(take(seg_off + done) + within).reshape(-1).astype(I32),
                 (take(seg_dst + done) + within).reshape(-1).astype(I32),
                 cum[:, -1].astype(I32)]
    return plan


def _wait_copies(n_rows, make_copy):
    def wait_big(c, carry):
        make_copy(0, 0, WAIT_CHUNK).wait()
        return carry
    lax.fori_loop(0, n_rows // WAIT_CHUNK, wait_big, 0)
    size = WAIT_CHUNK // 2
    while size >= SEG_ALIGN:
        @pl.when(n_rows % (2 * size) >= size)
        def _(size=size):
            make_copy(0, 0, size).wait()

        size //= 2


def _dispatch_kernel(*refs):
    plan_refs, (tot_ref, h_ref, pos_ref, xs_ref, sorted_ref, sem) = _split_plan(refs)
    i = pl.program_id(0)
    buf = i % 2
    tm = h_ref.shape[0]
    n_sorted = sorted_ref.shape[1]
    pos = pos_ref[...]
    lane = lax.broadcasted_iota(I32, (tm, n_sorted), 1)
    hit = jnp.zeros((tm, n_sorted), F32)
    for k in range(TOP_K):
        hit = jnp.where(pos[:, k:k + 1] == lane, 1.0, hit)
    sorted_ref[buf] = _pack_pairs(lax.dot_general(hit.astype(BF16), h_ref[...], (((0,), (0,)), ((), ())),
                                                  preferred_element_type=F32))

    def copies_from(b):
        def make_copy(local, slot, rows):
            return pltpu.make_async_copy(sorted_ref.at[b, pl.ds(local, rows)], xs_ref.at[pl.ds(slot, rows)],
                                         sem.at[b])
        return make_copy

    _planned_copies(i, plan_refs, copies_from(buf))

    @pl.when(i > 0)
    def _():
        _wait_copies(tot_ref[jnp.maximum(i - 1, 0)], copies_from(1 - buf))

    @pl.when(i == pl.num_programs(0) - 1)
    def _():
        _wait_copies(tot_ref[i], copies_from(buf))


def _split_plan(refs):
    n = 3 * len(COPY_SIZES)
    return [refs[k:k + 3] for k in range(0, n, 3)], refs[n:]


def _dispatch(h, pos, plan, tile_rows, n_slots):
    t, d = h.shape
    tm = TOKEN_TILE
    return pl.pallas_call(
        _dispatch_kernel,
        out_shape=jax.ShapeDtypeStruct((n_slots, d // 2), U32),
        grid_spec=pltpu.PrefetchScalarGridSpec(
            num_scalar_prefetch=len(plan) + 1,
            grid=(t // tm,),
            in_specs=[
                pl.BlockSpec((tm, d), lambda i, *_: (i, 0)),
                pl.BlockSpec((tm, TOP_K), lambda i, *_: (i, 0)),
            ],
            out_specs=pl.BlockSpec(memory_space=pl.ANY),
            scratch_shapes=[pltpu.VMEM((2, SORTED_ROWS, d // 2), U32), pltpu.SemaphoreType.DMA((2,))],
        ),
        compiler_params=_cparams("arbitrary"),
        name="moe_dispatch",
    )(*plan, tile_rows, h, pos)


def _ffn_kernel(rows_ref, start_ref, wgu_ref, bgu_ref, wd_ref, bd_ref, xs_ref, ys_ref,
                wgu_bf, wd_bf, xbuf, ybuf, sem_in, sem_out):
    e = pl.program_id(0)
    d_ff = wd_ref.shape[1]
    tb = xbuf.shape[1]
    n_rows = rows_ref[e]
    n_tiles = (n_rows + tb - 1) // tb
    base = start_ref[e]

    chunk = 128
    def cast_gu(c, carry):
        r = pl.multiple_of(c * chunk, chunk)
        wgu_bf[pl.ds(r, chunk), :] = wgu_ref[0, pl.ds(r, chunk), :].astype(BF16)
        return carry
    lax.fori_loop(0, wgu_ref.shape[1] // chunk, cast_gu, 0)
    def cast_d(c, carry):
        r = pl.multiple_of(c * chunk, chunk)
        wd_bf[pl.ds(r, chunk), :] = wd_ref[0, pl.ds(r, chunk), :].astype(BF16)
        return carry
    lax.fori_loop(0, d_ff // chunk, cast_d, 0)

    half = tb // 2
    last_small = (n_rows - (n_tiles - 1) * tb) <= half

    def x_copy(s, slot):
        r = pl.multiple_of(base + s * tb, EXPERT_TILE)
        return pltpu.make_async_copy(xs_ref.at[pl.ds(r, tb)], xbuf.at[slot], sem_in.at[slot])

    def y_copy(s, slot, rows=tb):
        r = pl.multiple_of(base + s * tb, EXPERT_TILE)
        return pltpu.make_async_copy(ybuf.at[slot, pl.ds(0, rows)], ys_ref.at[pl.ds(r, rows)], sem_out.at[slot])

    def ffn_rows(s, slot, n):
        rows = s * tb + lax.broadcasted_iota(I32, (n, 1), 0)
        x = _unpack_pairs(jnp.where(rows < n_rows, xbuf[slot, pl.ds(0, n), :], jnp.uint32(0)))
        gu = _dot(x, wgu_bf[...]) + bgu_ref[0]
        gate = jnp.minimum(gu[:, :d_ff], SWIGLU_LIMIT)
        up = jnp.clip(gu[:, d_ff:], -SWIGLU_LIMIT, SWIGLU_LIMIT)
        act = (up + 1.0) * (gate * (1.0 / (1.0 + jnp.exp(-SWIGLU_ALPHA * gate))))
        y = _dot(act.astype(BF16), wd_bf[...]) + bd_ref[0]
        ybuf[slot, pl.ds(0, n), :] = _pack_pairs(y.astype(BF16).astype(F32))
        y_copy(s, slot, n).start()

    @pl.when(n_tiles > 0)
    def _():
        x_copy(0, 0).start()

    def tile(s, carry):
        slot = s % 2
        x_copy(s, slot).wait()

        @pl.when(s + 1 < n_tiles)
        def _():
            x_copy(s + 1, 1 - slot).start()

        @pl.when(s >= 2)
        def _():
            y_copy(s - 2, slot).wait()

        small = jnp.logical_and(s == n_tiles - 1, last_small)

        @pl.when(jnp.logical_not(small))
        def _():
            ffn_rows(s, slot, tb)

        @pl.when(small)
        def _():
            ffn_rows(s, slot, half)

        return carry

    lax.fori_loop(0, n_tiles, tile, 0)

    @pl.when(n_tiles >= 2)
    def _():
        y_copy(n_tiles - 2, n_tiles % 2).wait()

    @pl.when(jnp.logical_and(n_tiles >= 1, jnp.logical_not(last_small)))
    def _():
        y_copy(n_tiles - 1, (n_tiles - 1) % 2).wait()

    @pl.when(jnp.logical_and(n_tiles >= 1, last_small))
    def _():
        y_copy(n_tiles - 1, (n_tiles - 1) % 2, half).wait()


def _expert_ffn(layer, xs, expert_rows, expert_start, w_gu, b_gu, w_down, b_down):
    n_slots, packed_w = xs.shape
    tb = FFN_TILE
    depth, ne, d, two_f = w_gu.shape
    d_ff = two_f // 2
    exp4 = lambda e, *_: (layer, e, 0, 0)
    return pl.pallas_call(
        _ffn_kernel,
        out_shape=jax.ShapeDtypeStruct((n_slots, packed_w), U32),
        grid_spec=pltpu.PrefetchScalarGridSpec(
            num_scalar_prefetch=2,
            grid=(ne,),
            in_specs=[
                pl.BlockSpec((None, 1, d, two_f), exp4),
                pl.BlockSpec((None, 1, 1, two_f), exp4),
                pl.BlockSpec((None, 1, d_ff, d), exp4),
                pl.BlockSpec((None, 1, 1, d), exp4),
                pl.BlockSpec(memory_space=pl.ANY),
            ],
            out_specs=pl.BlockSpec(memory_space=pl.ANY),
            scratch_shapes=[
                pltpu.VMEM((d, two_f), BF16), pltpu.VMEM((d_ff, d), BF16),
                pltpu.VMEM((2, tb, packed_w), U32), pltpu.VMEM((2, tb, packed_w), U32),
                pltpu.SemaphoreType.DMA((2,)), pltpu.SemaphoreType.DMA((2,)),
            ],
        ),
        compiler_params=_cparams("arbitrary"),
        name="expert_ffn",
    )(expert_rows, expert_start, w_gu, b_gu.reshape(depth, ne, 1, two_f),
      w_down, b_down.reshape(depth, ne, 1, d), xs)


def _combine_kernel(*refs, final, n_ptiles):
    plan_refs, (tot_ref, x_ref, pos_ref, gates_ref, mg_ref, *rest) = _split_plan(refs)
    if final:
        fn_ref, ys_ref, op_ref, os_ref, buf, sem = rest
    else:
        ys_ref, o_ref, buf, sem = rest
    i = pl.program_id(0)
    cur = i % 2
    tm = x_ref.shape[0]
    n_sorted = buf.shape[1]

    def copies_into(b):
        def make_copy(local, slot, rows):
            return pltpu.make_async_copy(ys_ref.at[pl.ds(slot, rows)], buf.at[b, pl.ds(local, rows)], sem.at[b])
        return make_copy

    @pl.when(i == 0)
    def _():
        buf[...] = jnp.zeros_like(buf)
        _planned_copies(i, plan_refs, copies_into(cur))

    @pl.when(i + 1 < pl.num_programs(0))
    def _():
        _planned_copies(i + 1, plan_refs, copies_into(1 - cur))

    pos = pos_ref[...]
    g = gates_ref[...]
    lane = lax.broadcasted_iota(I32, (tm, n_sorted), 1)
    weight = jnp.zeros((tm, n_sorted), F32)
    for k in range(TOP_K):
        weight = jnp.where(pos[:, k:k + 1] == lane, g[:, k:k + 1], weight)
    _wait_copies(tot_ref[i], copies_into(cur))
    y = _dot(weight.astype(BF16), _unpack_pairs(buf[cur]))
    xn = x_ref[...] + mg_ref[...] * y
    if final:
        xn = _rms(xn, fn_ref[...])

        @pl.when(i < n_ptiles)
        def _():
            op_ref[...] = xn

        @pl.when(i >= n_ptiles)
        def _():
            os_ref[...] = xn
    else:
        o_ref[...] = xn


def _combine(geom, layer, ys, pos, plan, tile_rows, x, gates, mods, final_g):
    t, d = x.shape
    tm = TOKEN_TILE
    final = final_g is not None

    def mod_imap(i, *_):
        return ((layer * MOD_ROWS + geom.group(i)) * N_MOD + 5, 0, 0)

    row = lambda i, *_: (i, 0)
    in_specs = [
        pl.BlockSpec((tm, d), row),
        pl.BlockSpec((tm, TOP_K), row),
        pl.BlockSpec((tm, TOP_K), row),
        pl.BlockSpec((None, 1, d), mod_imap),
    ]
    args = [*plan, tile_rows, x, pos, gates, mods]
    if final:
        in_specs.append(pl.BlockSpec((1, d), lambda i, *_: (0, 0)))
        args.append(final_g)
    in_specs.append(pl.BlockSpec(memory_space=pl.ANY))
    args.append(ys)
    if final:
        n_pt = geom.n_ptiles
        out_shape = [jax.ShapeDtypeStruct((geom.tp, d), F32), jax.ShapeDtypeStruct((t - geom.tp, d), F32)]
        out_specs = [pl.BlockSpec((tm, d), lambda i, *_: (jnp.minimum(i, n_pt - 1), 0)),
                     pl.BlockSpec((tm, d), lambda i, *_: (jnp.maximum(i - n_pt, 0), 0))]
    else:
        out_shape = jax.ShapeDtypeStruct((t, d), F32)
        out_specs = pl.BlockSpec((tm, d), row)
    return pl.pallas_call(
        functools.partial(_combine_kernel, final=final, n_ptiles=geom.n_ptiles),
        out_shape=out_shape,
        grid_spec=pltpu.PrefetchScalarGridSpec(
            num_scalar_prefetch=len(plan) + 1,
            grid=(t // tm,),
            in_specs=in_specs,
            out_specs=out_specs,
            scratch_shapes=[pltpu.VMEM((2, SORTED_ROWS, d // 2), U32), pltpu.SemaphoreType.DMA((2,))],
        ),
        compiler_params=_cparams("arbitrary"),
        name="moe_combine",
    )(*args)


def _moe(geom, layer, h, pos, gates, seg_rows, seg_off, x, mods, w_gu, b_gu, w_down, b_down, final_g):
    t = h.shape[0]
    tb = EXPERT_TILE
    n_tok_tiles = seg_rows.shape[0]
    max_rows = t * TOP_K + n_tok_tiles * N_EXPERTS * (SEG_ALIGN - 1)
    n_blocks = -(-max_rows // tb) + N_EXPERTS
    rows = seg_rows[:, 0, :]
    cnt = jnp.sum(rows, axis=0)
    n_tiles_e = (cnt + tb - 1) // tb
    tile_end = jnp.cumsum(n_tiles_e)
    tile_start = tile_end - n_tiles_e
    expert_start = (tile_start * tb).astype(I32)
    seg_dst = expert_start[None, :] + jnp.cumsum(rows, axis=0) - rows
    plan = _copy_plan(rows, seg_off[:, 0, :], seg_dst)
    tile_rows = jnp.sum(rows, axis=1).astype(I32)
    xs = _dispatch(h, pos, plan, tile_rows, n_blocks * tb + FFN_TILE - tb)
    ys = _expert_ffn(layer, xs, cnt.astype(I32), expert_start, w_gu, b_gu, w_down, b_down)
    return _combine(geom, layer, ys, pos, plan, tile_rows, x, gates, mods, final_g)


def kernel(x_prompt, x_sample, cache_b_k, cache_b_v, cache_c_k, cache_c_v, c, c_ctx,
           mod_w, mod_b, norm_mix, norm_ffn, even_w_in, even_w_out, even_sink,
           odd_w_in, odd_w_out, odd_q_norm, odd_k_norm, router_w, router_b,
           moe_w_gu, moe_b_gu, moe_w_down, moe_b_down, final_norm):
    bp, lp, d = x_prompt.shape
    bs, ls, _ = x_sample.shape
    past = cache_b_k.shape[2]
    depth = mod_w.shape[0]
    geom = _Geom(bp, lp, bs, ls)
    tp = geom.tp

    x = (x_prompt.reshape(tp, d), x_sample.reshape(bs * ls, d))
    cond = jnp.concatenate([c_ctx[None, :], c, jnp.zeros((MOD_ROWS - 1 - bs, d), F32)], axis=0)
    mods = _modulation(cond, mod_w, mod_b).reshape(depth * MOD_ROWS * N_MOD, 1, d)

    cn, sn = _dft_tables(A_GROUP_DIM)
    dft_chan = jnp.asarray(np.concatenate([cn, sn], axis=1), BF16)
    dft_p = [jnp.asarray(m, BF16) for m in _dft_tables(lp)]
    dft_s = [jnp.asarray(m, BF16) for m in _dft_tables(ls)]
    rope_b = [jnp.asarray(m) for m in _rope_tables(TOKEN_TILE, ls, B_HEAD_DIM)]
    rope_c = [jnp.asarray(m) for m in _rope_tables(TOKEN_TILE, ls, C_HEAD_DIM)]

    states = {"bk": [], "bv": [], "ck": [], "cv": []}
    for layer in range(depth):
        j = layer // 2
        g_mix = norm_mix[layer][None, :]
        g_ffn = norm_ffn[layer][None, :]
        if layer % 2 == 0:
            pair = _paired_head_order(B_KV_HEADS, B_HEADS // B_KV_HEADS, B_HEAD_DIM)
            w_in = even_w_in[j]
            w_in = jnp.concatenate([w_in[:, :A_WIDTH], w_in[:, A_WIDTH:A_WIDTH + B_Q_WIDTH][:, pair],
                                    w_in[:, A_WIDTH + B_Q_WIDTH:]], axis=1).astype(BF16)
            tc, ts, q, k, v, k_state, v_state = _in_projection(
                geom, layer, x, g_mix, mods, w_in, rope_b[0], rope_b[1], [dft_chan],
                _proj_even_kernel, (A_WIDTH, A_WIDTH, B_Q_WIDTH, B_KV_WIDTH, B_KV_WIDTH), B_KV_WIDTH, "proj_even")
            states["bk"].append(k_state.reshape(bp, lp, B_KV_HEADS, B_HEAD_DIM))
            states["bv"].append(v_state.reshape(bp, lp, B_KV_HEADS, B_HEAD_DIM))
            four = _fourier_tokens(tc, ts, dft_p[0], dft_p[1], bp, lp, 0, None)
            four = _fourier_tokens(tc, ts, dft_s[0], dft_s[1], bs, ls, tp, four)
            sink = even_sink[j]
            common = dict(kv_heads=B_KV_HEADS, groups=B_HEADS // B_KV_HEADS, dh=B_HEAD_DIM)
            att = _attention(q, k, v, None, sink, None, n_seq=bp, seq_len=lp, row0=0, q_tile=lp,
                             window=None, **common)
            ctx = (cache_b_k[:, j].reshape(bs, past, B_KV_WIDTH).astype(BF16),
                   cache_b_v[:, j].reshape(bs, past, B_KV_WIDTH).astype(BF16))
            att = _attention(q, k, v, ctx, sink, att, n_seq=bs, seq_len=ls, row0=tp, q_tile=ATTN_Q_TILE,
                             window=WINDOW, **common)
            w_out = even_w_out[j].astype(BF16)
            mixes = [four, att]
            w_parts = [w_out[:A_WIDTH], w_out[A_WIDTH:][pair]]
        else:
            q, k, v, k_state, v_state = _in_projection(
                geom, layer, x, g_mix, mods, odd_w_in[j].astype(BF16), rope_c[0], rope_c[1],
                [odd_q_norm[j][None, :], odd_k_norm[j][None, :]],
                _proj_odd_kernel, (C_Q_WIDTH, C_KV_WIDTH, C_KV_WIDTH), C_KV_WIDTH, "proj_odd")
            states["ck"].append(k_state.reshape(bp, lp, C_KV_HEADS, C_HEAD_DIM))
            states["cv"].append(v_state.reshape(bp, lp, C_KV_HEADS, C_HEAD_DIM))
            common = dict(kv_heads=C_KV_HEADS, groups=C_HEADS // C_KV_HEADS, dh=C_HEAD_DIM, window=None)
            att = _attention(q, k, v, None, None, None, n_seq=bp, seq_len=lp, row0=0, q_tile=lp, **common)
            ctx = (cache_c_k[:, j].reshape(bs, past, C_KV_WIDTH).astype(BF16),
                   cache_c_v[:, j].reshape(bs, past, C_KV_WIDTH).astype(BF16))
            att = _attention(q, k, v, ctx, None, att, n_seq=bs, seq_len=ls, row0=tp, q_tile=DENSE_Q_TILE, **common)
            mixes = [att]
            w_parts = [odd_w_out[j].astype(BF16)]
        x, h, pos, gates, seg_rows, seg_off = _out_route(
            geom, layer, mixes, w_parts, x, mods, g_ffn, router_w[layer].astype(BF16), router_b[layer][None, :])
        final_g = final_norm[None, :] if layer == depth - 1 else None
        x = _moe(geom, layer, h, pos, gates, seg_rows, seg_off, x, mods,
                 moe_w_gu, moe_b_gu, moe_w_down, moe_b_down, final_g)

    y_prompt = x[0].reshape(bp, lp, d)
    y_sample = x[1].reshape(bs, ls, d)
    return (y_prompt, y_sample,
            jnp.stack(states["bk"], axis=1), jnp.stack(states["bv"], axis=1),
            jnp.stack(states["ck"], axis=1), jnp.stack(states["cv"], axis=1))
```

```python
import functools

import numpy as np
import jax
import jax.numpy as jnp
from jax import lax
from jax.experimental import pallas as pl
from jax.experimental.pallas import tpu as pltpu

F32 = jnp.float32
BF16 = jnp.bfloat16
I32 = jnp.int32
U32 = jnp.uint32

GRID_W = 64
A_GROUPS = 4
A_GROUP_DIM = 128
A_WIDTH = A_GROUPS * A_GROUP_DIM
B_HEADS = 8
B_KV_HEADS = 2
B_HEAD_DIM = 64
B_Q_WIDTH = B_HEADS * B_HEAD_DIM
B_KV_WIDTH = B_KV_HEADS * B_HEAD_DIM
WINDOW = 128
C_HEADS = 8
C_KV_HEADS = 2
C_HEAD_DIM = 128
C_Q_WIDTH = C_HEADS * C_HEAD_DIM
C_KV_WIDTH = C_KV_HEADS * C_HEAD_DIM
ROPE_THETA = 10000.0
N_EXPERTS = 32
TOP_K = 4
SWIGLU_LIMIT = 7.0
SWIGLU_ALPHA = 1.702
EPS = 1e-6

LANES = 128
TOKEN_TILE = 256
EXPERT_TILE = 128
FFN_TILE = 256
ATTN_Q_TILE = 128
DENSE_Q_TILE = 256
ATTN_KEY_CHUNK = 1024
SEG_ALIGN = 8
COPY_SIZES = (32, 16, 8)
WAIT_CHUNK = 256
SORTED_ROWS = -(-(TOKEN_TILE * TOP_K + N_EXPERTS * (SEG_ALIGN - 1)) // LANES) * LANES
PLAN_WIDTHS = (SORTED_ROWS // COPY_SIZES[0],) + (N_EXPERTS,) * (len(COPY_SIZES) - 1)
VMEM_LIMIT = 56 * 1024 * 1024
MASKED = -1e30
N_MOD = 6
MOD_ROWS = 8


def _cparams(*sem):
    return pltpu.CompilerParams(dimension_semantics=tuple(sem), vmem_limit_bytes=VMEM_LIMIT)


def _dot(a, b):
    return jnp.dot(a, b, preferred_element_type=F32)


def _dot_nt(a, b):
    return lax.dot_general(a, b, (((1,), (1,)), ((), ())), preferred_element_type=F32)


def _rms(x, g):
    return x * lax.rsqrt(jnp.mean(x * x, axis=-1, keepdims=True) + EPS) * g


def _dft_tables(n):
    j = np.arange(n, dtype=np.int64)
    ang = 2.0 * np.pi * ((j[:, None] * j[None, :]) % n).astype(np.float64) / n
    s = 1.0 / np.sqrt(n)
    return np.cos(ang) * s, np.sin(ang) * s


def _paired_head_order(kv_heads, groups, dh):
    assert kv_heads * dh == LANES
    return np.array([(kv * groups + g) * dh + d for g in range(groups) for kv in range(kv_heads) for d in range(dh)])


def _rope_tables(n_prompt_rows, n_latent, head_dim):
    quarter = head_dim // 4
    pos = np.arange(n_latent)
    row = (pos // GRID_W).astype(np.float32)
    col = (pos % GRID_W).astype(np.float32)
    inv = (np.float32(ROPE_THETA) ** (-np.arange(quarter, dtype=np.float32) / np.float32(quarter))).astype(np.float32)
    ang_row = (row[:, None] * inv[None, :]).astype(np.float32)
    ang_col = (col[:, None] * inv[None, :]).astype(np.float32)
    cos_h = np.concatenate([np.cos(ang_row)] * 2 + [np.cos(ang_col)] * 2, axis=1)
    sin_h = np.concatenate([-np.sin(ang_row), np.sin(ang_row), -np.sin(ang_col), np.sin(ang_col)], axis=1)
    reps = LANES // head_dim
    cos_l = np.tile(cos_h, (1, reps)).astype(np.float32)
    sin_l = np.tile(sin_h, (1, reps)).astype(np.float32)
    cos = np.concatenate([np.ones((n_prompt_rows, LANES), np.float32), cos_l], axis=0)
    sin = np.concatenate([np.zeros((n_prompt_rows, LANES), np.float32), sin_l], axis=0)
    return cos, sin


def _rope(x, cos, sin, quarter):
    lane = lax.broadcasted_iota(I32, (x.shape[0], LANES), 1)
    first = ((lane // quarter) % 2) == 0
    outs = []
    for c in range(x.shape[1] // LANES):
        xc = x[:, c * LANES:(c + 1) * LANES]
        partner = jnp.where(first, pltpu.roll(xc, LANES - quarter, 1), pltpu.roll(xc, quarter, 1))
        outs.append(xc * cos + partner * sin)
    return outs[0] if len(outs) == 1 else jnp.concatenate(outs, axis=1)


def _head_rms(x, g):
    outs = []
    for c in range(x.shape[1] // LANES):
        outs.append(_rms(x[:, c * LANES:(c + 1) * LANES], g))
    return outs[0] if len(outs) == 1 else jnp.concatenate(outs, axis=1)


def _mod_kernel(c_ref, w_ref, b_ref, o_ref):
    c = c_ref[...]
    s = c * (1.0 / (1.0 + jnp.exp(-c)))
    o_ref[0] = _dot(s.astype(BF16), w_ref[0].astype(BF16)) + b_ref[0]


def _modulation(cond, mod_w, mod_b):
    depth, d, n = mod_w.shape
    tn = 1536
    return pl.pallas_call(
        _mod_kernel,
        out_shape=jax.ShapeDtypeStruct((depth, MOD_ROWS, n), F32),
        grid=(depth, n // tn),
        in_specs=[
            pl.BlockSpec((MOD_ROWS, d), lambda l, j: (0, 0)),
            pl.BlockSpec((1, d, tn), lambda l, j: (l, 0, j)),
            pl.BlockSpec((1, 1, tn), lambda l, j: (l, 0, j)),
        ],
        out_specs=pl.BlockSpec((1, MOD_ROWS, tn), lambda l, j: (l, 0, j)),
        compiler_params=_cparams("arbitrary", "arbitrary"),
        name="modulation",
    )(cond, mod_w, mod_b.reshape(depth, 1, n))


class _Geom:
    def __init__(self, bp, lp, bs, ls):
        self.bp, self.lp, self.bs, self.ls = bp, lp, bs, ls
        self.tp = bp * lp
        self.t = bp * lp + bs * ls
        assert lp == TOKEN_TILE and ls % TOKEN_TILE == 0 and self.tp % ls == 0
        self.n_ptiles = self.tp // TOKEN_TILE
        self.tiles_per_lat = ls // TOKEN_TILE
        self.n_tiles = self.t // TOKEN_TILE

    def group(self, i):
        return jnp.where(i < self.n_ptiles, 0, 1 + (i - self.n_ptiles) // self.tiles_per_lat)

    def pos_block(self, i):
        return jnp.where(i < self.n_ptiles, 0, 1 + (i - self.n_ptiles) % self.tiles_per_lat)


def _mod_spec(geom, layer, which, d):
    def imap(i):
        return ((layer * MOD_ROWS + geom.group(i)) * N_MOD + which, 0, 0)
    return pl.BlockSpec((None, 1, d), imap)


def _stream_specs(geom, x, d):
    tm = TOKEN_TILE
    if isinstance(x, tuple):
        return ([pl.BlockSpec((tm, d), lambda i, *_: (jnp.minimum(i, geom.n_ptiles - 1), 0)),
                 pl.BlockSpec((tm, d), lambda i, *_: (jnp.maximum(i - geom.n_ptiles, 0), 0))], list(x))
    return [pl.BlockSpec((tm, d), lambda i, *_: (i, 0))], [x]


def _stream_tile(x_refs, n_ptiles):
    if len(x_refs) == 1:
        return x_refs[0][...]
    return jnp.where(pl.program_id(0) < n_ptiles, x_refs[0][...], x_refs[1][...])


def _store_kv(k, v, kb_ref, vb_ref, ks_ref, vs_ref, n_ptiles):
    kb_ref[...] = k.astype(BF16)
    vb_ref[...] = v.astype(BF16)

    @pl.when(pl.program_id(0) < n_ptiles)
    def _():
        ks_ref[...] = k
        vs_ref[...] = v


def _proj_even_kernel(*refs, n_x, n_ptiles):
    x_refs = refs[:n_x]
    (g_ref, sh_ref, sc_ref, w_ref, cos_ref, sin_ref, dft_ref,
     tc_ref, ts_ref, q_ref, kb_ref, vb_ref, ks_ref, vs_ref) = refs[n_x:]
    h = _rms(_stream_tile(x_refs, n_ptiles), g_ref[...]) * (1.0 + sc_ref[...]) + sh_ref[...]
    p = _dot(h.astype(BF16), w_ref[...])
    cos = cos_ref[...]
    sin = sin_ref[...]
    dft = dft_ref[...]
    tcs, tss = [], []
    for g in range(A_GROUPS):
        t = _dot(p[:, g * A_GROUP_DIM:(g + 1) * A_GROUP_DIM].astype(BF16), dft)
        tcs.append(t[:, :A_GROUP_DIM])
        tss.append(t[:, A_GROUP_DIM:])
    tc_ref[...] = jnp.concatenate(tcs, axis=1).astype(BF16)
    ts_ref[...] = jnp.concatenate(tss, axis=1).astype(BF16)
    o = A_WIDTH
    q = _rope(p[:, o:o + B_Q_WIDTH], cos, sin, B_HEAD_DIM // 4)
    q_ref[...] = (q * B_HEAD_DIM ** -0.5).astype(BF16)
    o += B_Q_WIDTH
    k = _rope(p[:, o:o + B_KV_WIDTH], cos, sin, B_HEAD_DIM // 4)
    o += B_KV_WIDTH
    _store_kv(k, p[:, o:o + B_KV_WIDTH], kb_ref, vb_ref, ks_ref, vs_ref, n_ptiles)


def _proj_odd_kernel(*refs, n_x, n_ptiles):
    x_refs = refs[:n_x]
    (g_ref, sh_ref, sc_ref, w_ref, cos_ref, sin_ref, qn_ref, kn_ref,
     q_ref, kb_ref, vb_ref, ks_ref, vs_ref) = refs[n_x:]
    h = _rms(_stream_tile(x_refs, n_ptiles), g_ref[...]) * (1.0 + sc_ref[...]) + sh_ref[...]
    p = _dot(h.astype(BF16), w_ref[...])
    cos = cos_ref[...]
    sin = sin_ref[...]
    q = _head_rms(p[:, :C_Q_WIDTH], qn_ref[...])
    k = _head_rms(p[:, C_Q_WIDTH:C_Q_WIDTH + C_KV_WIDTH], kn_ref[...])
    q_ref[...] = (_rope(q, cos, sin, C_HEAD_DIM // 4) * C_HEAD_DIM ** -0.5).astype(BF16)
    k = _rope(k, cos, sin, C_HEAD_DIM // 4)
    _store_kv(k, p[:, C_Q_WIDTH + C_KV_WIDTH:], kb_ref, vb_ref, ks_ref, vs_ref, n_ptiles)


def _in_projection(geom, layer, x, norm_g, mods, w, cos, sin, extras, kernel, out_widths, kv_width, name):
    t, d = geom.t, w.shape[0]
    tm = TOKEN_TILE
    n_out = w.shape[1]
    row = lambda i: (i, 0)
    const2 = lambda i: (0, 0)
    x_specs, x_args = _stream_specs(geom, x, d)
    in_specs = x_specs + [
        pl.BlockSpec((1, d), const2),
        _mod_spec(geom, layer, 0, d),
        _mod_spec(geom, layer, 1, d),
        pl.BlockSpec((d, n_out), const2),
        pl.BlockSpec((tm, LANES), lambda i: (geom.pos_block(i), 0)),
        pl.BlockSpec((tm, LANES), lambda i: (geom.pos_block(i), 0)),
    ] + [pl.BlockSpec(e.shape, const2) for e in extras]
    return pl.pallas_call(
        functools.partial(kernel, n_x=len(x_args), n_ptiles=geom.n_ptiles),
        out_shape=([jax.ShapeDtypeStruct((t, wd), BF16) for wd in out_widths]
                   + [jax.ShapeDtypeStruct((geom.tp, kv_width), F32)] * 2),
        grid=(geom.n_tiles,),
        in_specs=in_specs,
        out_specs=([pl.BlockSpec((tm, wd), row) for wd in out_widths]
                   + [pl.BlockSpec((tm, kv_width), lambda i: (jnp.minimum(i, geom.n_ptiles - 1), 0))] * 2),
        compiler_params=_cparams("arbitrary"),
        name=name,
    )(*x_args, norm_g, mods, mods, w, cos, sin, *extras)


def _fourier_kernel(cl_ref, sl_ref, tc_ref, ts_ref, *rest):
    o_ref = rest[-1]
    o_ref[...] = (_dot(cl_ref[...], tc_ref[...]) - _dot(sl_ref[...], ts_ref[...])).astype(o_ref.dtype)


def _fourier_tokens(tc, ts, cl, sl, n_seq, seq_len, row0, prev):
    t, width = tc.shape
    tr = min(seq_len, 512)
    n_r = seq_len // tr
    assert row0 % seq_len == 0
    seq0 = row0 // seq_len
    out0 = row0 // tr
    in_specs = [
        pl.BlockSpec((tr, seq_len), lambda s, r: (r, 0)),
        pl.BlockSpec((tr, seq_len), lambda s, r: (r, 0)),
        pl.BlockSpec((seq_len, width), lambda s, r: (seq0 + s, 0)),
        pl.BlockSpec((seq_len, width), lambda s, r: (seq0 + s, 0)),
    ]
    args = [cl, sl, tc, ts]
    aliases = {}
    if prev is not None:
        in_specs.append(pl.BlockSpec(memory_space=pl.ANY))
        args.append(prev)
        aliases = {4: 0}
    return pl.pallas_call(
        _fourier_kernel,
        out_shape=jax.ShapeDtypeStruct((t, width), BF16),
        grid=(n_seq, n_r),
        in_specs=in_specs,
        out_specs=pl.BlockSpec((tr, width), lambda s, r: (out0 + s * n_r + r, 0)),
        input_output_aliases=aliases,
        compiler_params=_cparams("arbitrary", "arbitrary"),
        name="fourier_tokens",
    )(*args)


def _attend(q, chunks, sink, o0, dh, den_col):
    m = sink
    acc = None
    den = None
    for k, v, mask in chunks:
        s = _dot_nt(q, k)
        if mask is not None:
            s = jnp.where(mask, s, MASKED)
        m_new = jnp.max(s, axis=-1, keepdims=True)
        if m is not None:
            m_new = jnp.maximum(m, m_new)
        p = jnp.exp(s - m_new)
        pv = _dot(p.astype(BF16), v)
        if acc is None:
            acc = pv
            if den_col is None:
                den = jnp.sum(p, axis=-1, keepdims=True)
        else:
            alpha = jnp.exp(m - m_new)
            acc = alpha * acc + pv
            if den_col is None:
                den = alpha * den + jnp.sum(p, axis=-1, keepdims=True)
        m = m_new
    if den_col is not None:
        den = acc[:, den_col:den_col + 1]
    if sink is not None:
        den = den + jnp.exp(sink - m)
    return acc[:, o0:o0 + dh] / den


def _attend_two_pass(q, chunks, sink):
    scores = []
    m = sink
    for k, _, mask in chunks:
        s = _dot_nt(q, k)
        if mask is not None:
            s = jnp.where(mask, s, MASKED)
        scores.append(s)
        mx = jnp.max(s, axis=-1, keepdims=True)
        m = mx if m is None else jnp.maximum(m, mx)
    den = None if sink is None else jnp.exp(sink - m)
    acc = None
    for (_, v, _), s in zip(chunks, scores):
        e = jnp.exp(s - m)
        es = jnp.sum(e, axis=-1, keepdims=True)
        den = es if den is None else den + es
        o = _dot(e.astype(BF16), v)
        acc = o if acc is None else acc + o
    return acc / den


def _stack_heads(q, kv, groups, dh):
    return jnp.concatenate([q[:, (kv * groups + g) * dh:(kv * groups + g + 1) * dh] for g in range(groups)], axis=0)


def _sink_column(sink_ref, kv, groups, rows):
    return jnp.concatenate([jnp.full((rows, 1), sink_ref[kv * groups + g], F32) for g in range(groups)], axis=0)


def _head_values(v, kv, dh, with_ones):
    if not with_ones:
        return v[:, kv * dh:(kv + 1) * dh], 0, None
    assert dh == LANES
    lane = lax.broadcasted_iota(I32, (v.shape[0], LANES), 1)
    ones = jnp.where(lane == 0, 1.0, 0.0).astype(BF16)
    return jnp.concatenate([v[:, kv * dh:(kv + 1) * dh], ones], axis=1), 0, dh


def _paired_heads_attention(q, sources, sink_ref, groups, dh):
    rows = q.shape[0]
    half_q = lax.broadcasted_iota(I32, (rows, LANES), 1) // dh
    per_head = []
    for kv in range(2):
        qh = jnp.concatenate(
            [jnp.where(half_q == kv, q[:, g * LANES:(g + 1) * LANES], jnp.zeros((), q.dtype)) for g in range(groups)],
            axis=0)
        chunks = []
        for k, v, msk in sources:
            half_v = lax.broadcasted_iota(I32, v.shape, 1) // dh
            chunks.append((k, jnp.where(half_v == kv, v, jnp.zeros((), v.dtype)), msk))
        sink = _sink_column(sink_ref, kv, groups, rows) if sink_ref is not None else None
        per_head.append(_attend_two_pass(qh, chunks, sink))
    return jnp.concatenate(
        [per_head[0][g * rows:(g + 1) * rows] + per_head[1][g * rows:(g + 1) * rows] for g in range(groups)], axis=1)


def _attn_kernel(*refs, kv_heads, groups, dh, has_sink, has_ctx, window, q_tile, seq_len, chunk):
    refs = list(refs)
    sink_ref = refs.pop(0) if has_sink else None
    q_ref, k_ref, v_ref = refs[:3]
    ck_ref, cv_ref = (refs[3], refs[4]) if has_ctx else (None, None)
    o_ref = refs[-1]
    q = q_ref[...]
    rows = q.shape[0]
    if window is None:
        spans = [(c * chunk, chunk) for c in range(seq_len // chunk)]
        mask = None
    else:
        n = pl.program_id(1)
        band = q_tile + 2 * window
        start = pl.multiple_of(jnp.clip(n * q_tile - window, 0, seq_len - band), LANES)
        spans = [(start, band)]
        qpos = n * q_tile + lax.broadcasted_iota(I32, (groups * rows, band), 0) % rows
        kpos = start + lax.broadcasted_iota(I32, (groups * rows, band), 1)
        mask = jnp.abs(kpos - qpos) <= window
    sources = [(k_ref[pl.ds(s0, n_s), :], v_ref[pl.ds(s0, n_s), :], mask) for s0, n_s in spans]
    if has_ctx:
        sources.append((ck_ref[...], cv_ref[...], None))
    if 2 * dh == LANES:
        o_ref[...] = _paired_heads_attention(q, sources, sink_ref, groups, dh).astype(o_ref.dtype)
        return
    outs = []
    for kv in range(kv_heads):
        online = dh == LANES and len(sources) > 1
        chunks = []
        for k, v, msk in sources:
            vh, o0, den_col = _head_values(v, kv, dh, with_ones=online)
            chunks.append((k[:, kv * dh:(kv + 1) * dh], vh, msk))
        sink = _sink_column(sink_ref, kv, groups, rows) if has_sink else None
        qh = _stack_heads(q, kv, groups, dh)
        o = _attend(qh, chunks, sink, o0, dh, den_col) if online else _attend_two_pass(qh, chunks, sink)
        outs.extend(o[g * rows:(g + 1) * rows] for g in range(groups))
    o_ref[...] = jnp.concatenate(outs, axis=1).astype(o_ref.dtype)


def _attention(q, k, v, ctx, sink, prev, *, n_seq, seq_len, row0, q_tile, kv_heads, groups, dh, window):
    t, qw = q.shape
    kw = k.shape[1]
    n_q = seq_len // q_tile
    assert row0 % seq_len == 0 and row0 % q_tile == 0
    seq0 = row0 // seq_len
    q0 = row0 // q_tile
    in_specs, args = [], []
    if sink is not None:
        in_specs.append(pl.BlockSpec(memory_space=pltpu.SMEM))
        args.append(sink)
    in_specs += [
        pl.BlockSpec((q_tile, qw), lambda s, n: (q0 + s * n_q + n, 0)),
        pl.BlockSpec((seq_len, kw), lambda s, n: (seq0 + s, 0)),
        pl.BlockSpec((seq_len, kw), lambda s, n: (seq0 + s, 0)),
    ]
    args += [q, k, v]
    if ctx is not None:
        p = ctx[0].shape[1]
        in_specs += [pl.BlockSpec((None, p, kw), lambda s, n: (s, 0, 0))] * 2
        args += list(ctx)
    aliases = {}
    if prev is not None:
        in_specs.append(pl.BlockSpec(memory_space=pl.ANY))
        aliases = {len(args): 0}
        args.append(prev)
    kern = functools.partial(
        _attn_kernel, kv_heads=kv_heads, groups=groups, dh=dh, has_sink=sink is not None,
        has_ctx=ctx is not None, window=window, q_tile=q_tile, seq_len=seq_len, chunk=min(seq_len, ATTN_KEY_CHUNK))
    return pl.pallas_call(
        kern,
        out_shape=jax.ShapeDtypeStruct((t, qw), BF16),
        grid=(n_seq, n_q),
        in_specs=in_specs,
        out_specs=pl.BlockSpec((q_tile, qw), lambda s, n: (q0 + s * n_q + n, 0)),
        input_output_aliases=aliases,
        compiler_params=_cparams("arbitrary", "arbitrary"),
        name="attention",
    )(*args)


def _out_route_kernel(*refs, n_mix, n_x, n_ptiles):
    mix_refs = refs[:n_mix]
    w_refs = refs[n_mix:2 * n_mix]
    x_refs = refs[2 * n_mix:2 * n_mix + n_x]
    (gate_ref, g2_ref, sh2_ref, sc2_ref, rw_ref, rb_ref,
     xo_ref, h_ref, pos_ref, gates_ref, rows_ref, off_ref) = refs[2 * n_mix + n_x:]

    acc = None
    for m_ref, w_ref in zip(mix_refs, w_refs):
        part = _dot(m_ref[...], w_ref[...])
        acc = part if acc is None else acc + part
    xn = _stream_tile(x_refs, n_ptiles) + gate_ref[...] * acc
    xo_ref[...] = xn
    h = _rms(xn, g2_ref[...]) * (1.0 + sc2_ref[...]) + sh2_ref[...]
    hb = h.astype(BF16)
    h_ref[...] = hb

    logits = _dot_nt(rw_ref[...], hb) + rb_ref[...]
    ne, tm = logits.shape
    expert = lax.broadcasted_iota(I32, (ne, tm), 0).astype(F32)
    work = logits
    sels, vals = [], []
    for _ in range(TOP_K):
        mx = jnp.max(work, axis=0, keepdims=True)
        first = jnp.min(jnp.where(work == mx, expert, float(ne)), axis=0, keepdims=True)
        sel = expert == first
        work = jnp.where(sel, -jnp.inf, work)
        sels.append(sel)
        vals.append(mx)
    exps = [jnp.exp(v - vals[0]) for v in vals]
    den = exps[0] + exps[1] + exps[2] + exps[3]

    onehot = jnp.zeros((ne, tm), F32)
    for sel in sels:
        onehot = onehot + sel.astype(F32)
    r_i = lax.broadcasted_iota(I32, (tm, tm), 0)
    c_i = lax.broadcasted_iota(I32, (tm, tm), 1)
    before = jnp.where(r_i < c_i, 1.0, 0.0).astype(BF16)
    earlier = _dot(onehot.astype(BF16), before)

    cnt = jnp.sum(onehot, axis=1, keepdims=True)
    seg8 = jnp.floor((cnt + (SEG_ALIGN - 1.0)) * (1.0 / SEG_ALIGN))
    e_r = lax.broadcasted_iota(I32, (ne, ne), 0)
    e_c = lax.broadcasted_iota(I32, (ne, ne), 1)
    lower = jnp.where(e_c < e_r, 1.0, 0.0).astype(BF16)
    off8 = _dot(lower, jnp.broadcast_to(seg8, (ne, LANES)).astype(BF16))[:, 0:1]
    seg_off = off8 * SEG_ALIGN
    base = seg_off + earlier

    sub = lax.broadcasted_iota(I32, (2 * TOP_K, tm), 0)
    token_rows = jnp.zeros((2 * TOP_K, tm), F32)
    for k in range(TOP_K):
        pos_k = jnp.sum(jnp.where(sels[k], base, 0.0), axis=0, keepdims=True)
        token_rows = jnp.where(sub == k, pos_k, token_rows)
        token_rows = jnp.where(sub == TOP_K + k, exps[k] / den, token_rows)
    first_lane = lax.broadcasted_iota(I32, (ne, tm), 1) == 0
    block = jnp.concatenate([
        token_rows,
        jnp.where(first_lane, seg8 * SEG_ALIGN, 0.0),
        jnp.where(first_lane, seg_off, 0.0),
        jnp.zeros((LANES - 2 * TOP_K - 2 * ne, tm), F32)], axis=0)
    by_token = block.T
    pos_ref[...] = by_token[:, 0:TOP_K].astype(I32)
    gates_ref[...] = by_token[:, TOP_K:2 * TOP_K]
    rows_ref[0] = by_token[0:1, 2 * TOP_K:2 * TOP_K + ne].astype(I32)
    off_ref[0] = by_token[0:1, 2 * TOP_K + ne:2 * TOP_K + 2 * ne].astype(I32)


def _out_route(geom, layer, mixes, w_parts, x, mods, norm_g, router_w, router_b):
    t, d = geom.t, w_parts[0].shape[1]
    tm = TOKEN_TILE
    row = lambda i: (i, 0)
    const2 = lambda i: (0, 0)
    x_specs, x_args = _stream_specs(geom, x, d)
    in_specs = [pl.BlockSpec((tm, m.shape[1]), row) for m in mixes]
    in_specs += [pl.BlockSpec(w.shape, const2) for w in w_parts]
    in_specs += x_specs
    in_specs += [
        _mod_spec(geom, layer, 2, d),
        pl.BlockSpec((1, d), const2),
        _mod_spec(geom, layer, 3, d),
        _mod_spec(geom, layer, 4, d),
        pl.BlockSpec(router_w.shape, const2),
        pl.BlockSpec((N_EXPERTS, 1), const2),
    ]
    seg3 = lambda i: (i, 0, 0)
    out_shape = [
        jax.ShapeDtypeStruct((t, d), F32),
        jax.ShapeDtypeStruct((t, d), BF16),
        jax.ShapeDtypeStruct((t, TOP_K), I32),
        jax.ShapeDtypeStruct((t, TOP_K), F32),
        jax.ShapeDtypeStruct((geom.n_tiles, 1, N_EXPERTS), I32),
        jax.ShapeDtypeStruct((geom.n_tiles, 1, N_EXPERTS), I32),
    ]
    out_specs = [
        pl.BlockSpec((tm, d), row),
        pl.BlockSpec((tm, d), row),
        pl.BlockSpec((tm, TOP_K), row),
        pl.BlockSpec((tm, TOP_K), row),
        pl.BlockSpec((1, 1, N_EXPERTS), seg3),
        pl.BlockSpec((1, 1, N_EXPERTS), seg3),
    ]
    return pl.pallas_call(
        functools.partial(_out_route_kernel, n_mix=len(mixes), n_x=len(x_args), n_ptiles=geom.n_ptiles),
        out_shape=out_shape,
        grid=(geom.n_tiles,),
        in_specs=in_specs,
        out_specs=out_specs,
        compiler_params=_cparams("arbitrary"),
        name="out_route",
    )(*mixes, *w_parts, *x_args, mods, norm_g, mods, mods, router_w, router_b)


def _pack_pairs(v):
    n = v.shape[1] // 2
    bits = lax.bitcast_convert_type(v, U32)
    return (bits[:, :n] & jnp.uint32(0xFFFF0000)) | (bits[:, n:] >> 16)


def _unpack_pairs(p):
    hi = lax.bitcast_convert_type(p & jnp.uint32(0xFFFF0000), F32)
    lo = lax.bitcast_convert_type(p << 16, F32)
    return jnp.concatenate([hi, lo], axis=1).astype(BF16)


def _planned_copies(i, plan_refs, make_copy):
    for size, width, (local_ref, slot_ref, count_ref) in zip(COPY_SIZES, PLAN_WIDTHS, plan_refs):
        def one(c, carry, size=size, width=width, local_ref=local_ref, slot_ref=slot_ref):
            a = i * width + c
            make_copy(pl.multiple_of(local_ref[a], SEG_ALIGN), pl.multiple_of(slot_ref[a], SEG_ALIGN), size).start()
            return carry

        lax.fori_loop(0, count_ref[i], one, 0)


def _copy_plan(rows, seg_off, seg_dst):
    plan = []
    experts = jnp.arange(N_EXPERTS, dtype=I32)
    for n, (size, width) in enumerate(zip(COPY_SIZES, PLAN_WIDTHS)):
        if n == 0:
            count, done = rows // size, jnp.zeros_like(rows)
        else:
            count, done = (rows % (2 * size)) // size, rows - rows % (2 * size)
        cum = jnp.cumsum(count, axis=1)
        j = jnp.arange(width, dtype=I32)
        owner = jnp.sum((cum[:, None, :] <= j[None, :, None]).astype(I32), axis=2)
        pick = (jnp.minimum(owner, N_EXPERTS - 1)[:, :, None] == experts[None, None, :]).astype(I32)
        take = lambda v: jnp.sum(pick * v[:, None, :], axis=2)
        within = (j[None, :] - take(cum - count)) * size
        plan += [(take(seg_off + done) + within).reshape(-1).astype(I32),
                 (take(seg_dst + done) + within).reshape(-1).astype(I32),
                 cum[:, -1].astype(I32)]
    return plan


def _wait_copies(n_rows, make_copy):
    def wait_big(c, carry):
        make_copy(0, 0, WAIT_CHUNK).wait()
        return carry
    lax.fori_loop(0, n_rows // WAIT_CHUNK, wait_big, 0)
    size = WAIT_CHUNK // 2
    while size >= SEG_ALIGN:
        @pl.when(n_rows % (2 * size) >= size)
        def _(size=size):
            make_copy(0, 0, size).wait()

        size //= 2


def _dispatch_kernel(*refs):
    plan_refs, (tot_ref, h_ref, pos_ref, xs_ref, sorted_ref, sem) = _split_plan(refs)
    i = pl.program_id(0)
    buf = i % 2
    tm = h_ref.shape[0]
    n_sorted = sorted_ref.shape[1]
    pos = pos_ref[...]
    lane = lax.broadcasted_iota(I32, (tm, n_sorted), 1)
    hit = jnp.zeros((tm, n_sorted), F32)
    for k in range(TOP_K):
        hit = jnp.where(pos[:, k:k + 1] == lane, 1.0, hit)
    sorted_ref[buf] = _pack_pairs(lax.dot_general(hit.astype(BF16), h_ref[...], (((0,), (0,)), ((), ())),
                                                  preferred_element_type=F32))

    def copies_from(b):
        def make_copy(local, slot, rows):
            return pltpu.make_async_copy(sorted_ref.at[b, pl.ds(local, rows)], xs_ref.at[pl.ds(slot, rows)],
                                         sem.at[b])
        return make_copy

    _planned_copies(i, plan_refs, copies_from(buf))

    @pl.when(i > 0)
    def _():
        _wait_copies(tot_ref[jnp.maximum(i - 1, 0)], copies_from(1 - buf))

    @pl.when(i == pl.num_programs(0) - 1)
    def _():
        _wait_copies(tot_ref[i], copies_from(buf))


def _split_plan(refs):
    n = 3 * len(COPY_SIZES)
    return [refs[k:k + 3] for k in range(0, n, 3)], refs[n:]


def _dispatch(h, pos, plan, tile_rows, n_slots):
    t, d = h.shape
    tm = TOKEN_TILE
    return pl.pallas_call(
        _dispatch_kernel,
        out_shape=jax.ShapeDtypeStruct((n_slots, d // 2), U32),
        grid_spec=pltpu.PrefetchScalarGridSpec(
            num_scalar_prefetch=len(plan) + 1,
            grid=(t // tm,),
            in_specs=[
                pl.BlockSpec((tm, d), lambda i, *_: (i, 0)),
                pl.BlockSpec((tm, TOP_K), lambda i, *_: (i, 0)),
            ],
            out_specs=pl.BlockSpec(memory_space=pl.ANY),
            scratch_shapes=[pltpu.VMEM((2, SORTED_ROWS, d // 2), U32), pltpu.SemaphoreType.DMA((2,))],
        ),
        compiler_params=_cparams("arbitrary"),
        name="moe_dispatch",
    )(*plan, tile_rows, h, pos)


def _ffn_kernel(rows_ref, start_ref, wgu_ref, bgu_ref, wd_ref, bd_ref, xs_ref, ys_ref,
                wgu_bf, wd_bf, xbuf, ybuf, sem_in, sem_out):
    e = pl.program_id(0)
    d_ff = wd_ref.shape[1]
    tb = xbuf.shape[1]
    n_rows = rows_ref[e]
    n_tiles = (n_rows + tb - 1) // tb
    base = start_ref[e]

    chunk = 128
    def cast_gu(c, carry):
        r = pl.multiple_of(c * chunk, chunk)
        wgu_bf[pl.ds(r, chunk), :] = wgu_ref[0, pl.ds(r, chunk), :].astype(BF16)
        return carry
    lax.fori_loop(0, wgu_ref.shape[1] // chunk, cast_gu, 0)
    def cast_d(c, carry):
        r = pl.multiple_of(c * chunk, chunk)
        wd_bf[pl.ds(r, chunk), :] = wd_ref[0, pl.ds(r, chunk), :].astype(BF16)
        return carry
    lax.fori_loop(0, d_ff // chunk, cast_d, 0)

    half = tb // 2
    last_small = (n_rows - (n_tiles - 1) * tb) <= half

    def x_copy(s, slot):
        r = pl.multiple_of(base + s * tb, EXPERT_TILE)
        return pltpu.make_async_copy(xs_ref.at[pl.ds(r, tb)], xbuf.at[slot], sem_in.at[slot])

    def y_copy(s, slot, rows=tb):
        r = pl.multiple_of(base + s * tb, EXPERT_TILE)
        return pltpu.make_async_copy(ybuf.at[slot, pl.ds(0, rows)], ys_ref.at[pl.ds(r, rows)], sem_out.at[slot])

    def ffn_rows(s, slot, n):
        rows = s * tb + lax.broadcasted_iota(I32, (n, 1), 0)
        x = _unpack_pairs(jnp.where(rows < n_rows, xbuf[slot, pl.ds(0, n), :], jnp.uint32(0)))
        gu = _dot(x, wgu_bf[...]) + bgu_ref[0]
        gate = jnp.minimum(gu[:, :d_ff], SWIGLU_LIMIT)
        up = jnp.clip(gu[:, d_ff:], -SWIGLU_LIMIT, SWIGLU_LIMIT)
        act = (up + 1.0) * (gate * (1.0 / (1.0 + jnp.exp(-SWIGLU_ALPHA * gate))))
        y = _dot(act.astype(BF16), wd_bf[...]) + bd_ref[0]
        ybuf[slot, pl.ds(0, n), :] = _pack_pairs(y.astype(BF16).astype(F32))
        y_copy(s, slot, n).start()

    @pl.when(n_tiles > 0)
    def _():
        x_copy(0, 0).start()

    def tile(s, carry):
        slot = s % 2
        x_copy(s, slot).wait()

        @pl.when(s + 1 < n_tiles)
        def _():
            x_copy(s + 1, 1 - slot).start()

        @pl.when(s >= 2)
        def _():
            y_copy(s - 2, slot).wait()

        small = jnp.logical_and(s == n_tiles - 1, last_small)

        @pl.when(jnp.logical_not(small))
        def _():
            ffn_rows(s, slot, tb)

        @pl.when(small)
        def _():
            ffn_rows(s, slot, half)

        return carry

    lax.fori_loop(0, n_tiles, tile, 0)

    @pl.when(n_tiles >= 2)
    def _():
        y_copy(n_tiles - 2, n_tiles % 2).wait()

    @pl.when(jnp.logical_and(n_tiles >= 1, jnp.logical_not(last_small)))
    def _():
        y_copy(n_tiles - 1, (n_tiles - 1) % 2).wait()

    @pl.when(jnp.logical_and(n_tiles >= 1, last_small))
    def _():
        y_copy(n_tiles - 1, (n_tiles - 1) % 2, half).wait()


def _expert_ffn(layer, xs, expert_rows, expert_start, w_gu, b_gu, w_down, b_down):
    n_slots, packed_w = xs.shape
    tb = FFN_TILE
    depth, ne, d, two_f = w_gu.shape
    d_ff = two_f // 2
    exp4 = lambda e, *_: (layer, e, 0, 0)
    return pl.pallas_call(
        _ffn_kernel,
        out_shape=jax.ShapeDtypeStruct((n_slots, packed_w), U32),
        grid_spec=pltpu.PrefetchScalarGridSpec(
            num_scalar_prefetch=2,
            grid=(ne,),
            in_specs=[
                pl.BlockSpec((None, 1, d, two_f), exp4),
                pl.BlockSpec((None, 1, 1, two_f), exp4),
                pl.BlockSpec((None, 1, d_ff, d), exp4),
                pl.BlockSpec((None, 1, 1, d), exp4),
                pl.BlockSpec(memory_space=pl.ANY),
            ],
            out_specs=pl.BlockSpec(memory_space=pl.ANY),
            scratch_shapes=[
                pltpu.VMEM((d, two_f), BF16), pltpu.VMEM((d_ff, d), BF16),
                pltpu.VMEM((2, tb, packed_w), U32), pltpu.VMEM((2, tb, packed_w), U32),
                pltpu.SemaphoreType.DMA((2,)), pltpu.SemaphoreType.DMA((2,)),
            ],
        ),
        compiler_params=_cparams("arbitrary"),
        name="expert_ffn",
    )(expert_rows, expert_start, w_gu, b_gu.reshape(depth, ne, 1, two_f),
      w_down, b_down.reshape(depth, ne, 1, d), xs)


def _combine_kernel(*refs, final, n_ptiles):
    plan_refs, (tot_ref, x_ref, pos_ref, gates_ref, mg_ref, *rest) = _split_plan(refs)
    if final:
        fn_ref, ys_ref, op_ref, os_ref, buf, sem = rest
    else:
        ys_ref, o_ref, buf, sem = rest
    i = pl.program_id(0)
    cur = i % 2
    tm = x_ref.shape[0]
    n_sorted = buf.shape[1]

    def copies_into(b):
        def make_copy(local, slot, rows):
            return pltpu.make_async_copy(ys_ref.at[pl.ds(slot, rows)], buf.at[b, pl.ds(local, rows)], sem.at[b])
        return make_copy

    @pl.when(i == 0)
    def _():
        buf[...] = jnp.zeros_like(buf)
        _planned_copies(i, plan_refs, copies_into(cur))

    @pl.when(i + 1 < pl.num_programs(0))
    def _():
        _planned_copies(i + 1, plan_refs, copies_into(1 - cur))

    pos = pos_ref[...]
    g = gates_ref[...]
    lane = lax.broadcasted_iota(I32, (tm, n_sorted), 1)
    weight = jnp.zeros((tm, n_sorted), F32)
    for k in range(TOP_K):
        weight = jnp.where(pos[:, k:k + 1] == lane, g[:, k:k + 1], weight)
    _wait_copies(tot_ref[i], copies_into(cur))
    y = _dot(weight.astype(BF16), _unpack_pairs(buf[cur]))
    xn = x_ref[...] + mg_ref[...] * y
    if final:
        xn = _rms(xn, fn_ref[...])

        @pl.when(i < n_ptiles)
        def _():
            op_ref[...] = xn

        @pl.when(i >= n_ptiles)
        def _():
            os_ref[...] = xn
    else:
        o_ref[...] = xn


def _combine(geom, layer, ys, pos, plan, tile_rows, x, gates, mods, final_g):
    t, d = x.shape
    tm = TOKEN_TILE
    final = final_g is not None

    def mod_imap(i, *_):
        return ((layer * MOD_ROWS + geom.group(i)) * N_MOD + 5, 0, 0)

    row = lambda i, *_: (i, 0)
    in_specs = [
        pl.BlockSpec((tm, d), row),
        pl.BlockSpec((tm, TOP_K), row),
        pl.BlockSpec((tm, TOP_K), row),
        pl.BlockSpec((None, 1, d), mod_imap),
    ]
    args = [*plan, tile_rows, x, pos, gates, mods]
    if final:
        in_specs.append(pl.BlockSpec((1, d), lambda i, *_: (0, 0)))
        args.append(final_g)
    in_specs.append(pl.BlockSpec(memory_space=pl.ANY))
    args.append(ys)
    if final:
        n_pt = geom.n_ptiles
        out_shape = [jax.ShapeDtypeStruct((geom.tp, d), F32), jax.ShapeDtypeStruct((t - geom.tp, d), F32)]
        out_specs = [pl.BlockSpec((tm, d), lambda i, *_: (jnp.minimum(i, n_pt - 1), 0)),
                     pl.BlockSpec((tm, d), lambda i, *_: (jnp.maximum(i - n_pt, 0), 0))]
    else:
        out_shape = jax.ShapeDtypeStruct((t, d), F32)
        out_specs = pl.BlockSpec((tm, d), row)
    return pl.pallas_call(
        functools.partial(_combine_kernel, final=final, n_ptiles=geom.n_ptiles),
        out_shape=out_shape,
        grid_spec=pltpu.PrefetchScalarGridSpec(
            num_scalar_prefetch=len(plan) + 1,
            grid=(t // tm,),
            in_specs=in_specs,
            out_specs=out_specs,
            scratch_shapes=[pltpu.VMEM((2, SORTED_ROWS, d // 2), U32), pltpu.SemaphoreType.DMA((2,))],
        ),
        compiler_params=_cparams("arbitrary"),
        name="moe_combine",
    )(*args)


def _moe(geom, layer, h, pos, gates, seg_rows, seg_off, x, mods, w_gu, b_gu, w_down, b_down, final_g):
    t = h.shape[0]
    tb = EXPERT_TILE
    n_tok_tiles = seg_rows.shape[0]
    max_rows = t * TOP_K + n_tok_tiles * N_EXPERTS * (SEG_ALIGN - 1)
    n_blocks = -(-max_rows // tb) + N_EXPERTS
    rows = seg_rows[:, 0, :]
    cnt = jnp.sum(rows, axis=0)
    n_tiles_e = (cnt + tb - 1) // tb
    tile_end = jnp.cumsum(n_tiles_e)
    tile_start = tile_end - n_tiles_e
    expert_start = (tile_start * tb).astype(I32)
    seg_dst = expert_start[None, :] + jnp.cumsum(rows, axis=0) - rows
    plan = _copy_plan(rows, seg_off[:, 0, :], seg_dst)
    tile_rows = jnp.sum(rows, axis=1).astype(I32)
    xs = _dispatch(h, pos, plan, tile_rows, n_blocks * tb + FFN_TILE - tb)
    ys = _expert_ffn(layer, xs, cnt.astype(I32), expert_start, w_gu, b_gu, w_down, b_down)
    return _combine(geom, layer, ys, pos, plan, tile_rows, x, gates, mods, final_g)


def kernel(x_prompt, x_sample, cache_b_k, cache_b_v, cache_c_k, cache_c_v, c, c_ctx,
           mod_w, mod_b, norm_mix, norm_ffn, even_w_in, even_w_out, even_sink,
           odd_w_in, odd_w_out, odd_q_norm, odd_k_norm, router_w, router_b,
           moe_w_gu, moe_b_gu, moe_w_down, moe_b_down, final_norm):
    bp, lp, d = x_prompt.shape
    bs, ls, _ = x_sample.shape
    past = cache_b_k.shape[2]
    depth = mod_w.shape[0]
    geom = _Geom(bp, lp, bs, ls)
    tp = geom.tp

    x = (x_prompt.reshape(tp, d), x_sample.reshape(bs * ls, d))
    cond = jnp.concatenate([c_ctx[None, :], c, jnp.zeros((MOD_ROWS - 1 - bs, d), F32)], axis=0)
    mods = _modulation(cond, mod_w, mod_b).reshape(depth * MOD_ROWS * N_MOD, 1, d)

    cn, sn = _dft_tables(A_GROUP_DIM)
    dft_chan = jnp.asarray(np.concatenate([cn, sn], axis=1), BF16)
    dft_p = [jnp.asarray(m, BF16) for m in _dft_tables(lp)]
    dft_s = [jnp.asarray(m, BF16) for m in _dft_tables(ls)]
    rope_b = [jnp.asarray(m) for m in _rope_tables(TOKEN_TILE, ls, B_HEAD_DIM)]
    rope_c = [jnp.asarray(m) for m in _rope_tables(TOKEN_TILE, ls, C_HEAD_DIM)]

    states = {"bk": [], "bv": [], "ck": [], "cv": []}
    for layer in range(depth):
        j = layer // 2
        g_mix = norm_mix[layer][None, :]
        g_ffn = norm_ffn[layer][None, :]
        if layer % 2 == 0:
            pair = _paired_head_order(B_KV_HEADS, B_HEADS // B_KV_HEADS, B_HEAD_DIM)
            w_in = even_w_in[j]
            w_in = jnp.concatenate([w_in[:, :A_WIDTH], w_in[:, A_WIDTH:A_WIDTH + B_Q_WIDTH][:, pair],
                                    w_in[:, A_WIDTH + B_Q_WIDTH:]], axis=1).astype(BF16)
            tc, ts, q, k, v, k_state, v_state = _in_projection(
                geom, layer, x, g_mix, mods, w_in, rope_b[0], rope_b[1], [dft_chan],
                _proj_even_kernel, (A_WIDTH, A_WIDTH, B_Q_WIDTH, B_KV_WIDTH, B_KV_WIDTH), B_KV_WIDTH, "proj_even")
            states["bk"].append(k_state.reshape(bp, lp, B_KV_HEADS, B_HEAD_DIM))
            states["bv"].append(v_state.reshape(bp, lp, B_KV_HEADS, B_HEAD_DIM))
            four = _fourier_tokens(tc, ts, dft_p[0], dft_p[1], bp, lp, 0, None)
            four = _fourier_tokens(tc, ts, dft_s[0], dft_s[1], bs, ls, tp, four)
            sink = even_sink[j]
            common = dict(kv_heads=B_KV_HEADS, groups=B_HEADS // B_KV_HEADS, dh=B_HEAD_DIM)
            att = _attention(q, k, v, None, sink, None, n_seq=bp, seq_len=lp, row0=0, q_tile=lp,
                             window=None, **common)
            ctx = (cache_b_k[:, j].reshape(bs, past, B_KV_WIDTH).astype(BF16),
                   cache_b_v[:, j].reshape(bs, past, B_KV_WIDTH).astype(BF16))
            att = _attention(q, k, v, ctx, sink, att, n_seq=bs, seq_len=ls, row0=tp, q_tile=ATTN_Q_TILE,
                             window=WINDOW, **common)
            w_out = even_w_out[j].astype(BF16)
            mixes = [four, att]
            w_parts = [w_out[:A_WIDTH], w_out[A_WIDTH:][pair]]
        else:
            q, k, v, k_state, v_state = _in_projection(
                geom, layer, x, g_mix, mods, odd_w_in[j].astype(BF16), rope_c[0], rope_c[1],
                [odd_q_norm[j][None, :], odd_k_norm[j][None, :]],
                _proj_odd_kernel, (C_Q_WIDTH, C_KV_WIDTH, C_KV_WIDTH), C_KV_WIDTH, "proj_odd")
            states["ck"].append(k_state.reshape(bp, lp, C_KV_HEADS, C_HEAD_DIM))
            states["cv"].append(v_state.reshape(bp, lp, C_KV_HEADS, C_HEAD_DIM))
            common = dict(kv_heads=C_KV_HEADS, groups=C_HEADS // C_KV_HEADS, dh=C_HEAD_DIM, window=None)
            att = _attention(q, k, v, None, None, None, n_seq=bp, seq_len=lp, row0=0, q_tile=lp, **common)
            ctx = (cache_c_k[:, j].reshape(bs, past, C_KV_WIDTH).astype(BF16),
                   cache_c_v[:, j].reshape(bs, past, C_KV_WIDTH).astype(BF16))
            att = _attention(q, k, v, ctx, None, att, n_seq=bs, seq_len=ls, row0=tp, q_tile=DENSE_Q_TILE, **common)
            mixes = [att]
            w_parts = [odd_w_out[j].astype(BF16)]
        x, h, pos, gates, seg_rows, seg_off = _out_route(
            geom, layer, mixes, w_parts, x, mods, g_ffn, router_w[layer].T.astype(BF16), router_b[layer][:, None])
        final_g = final_norm[None, :] if layer == depth - 1 else None
        x = _moe(geom, layer, h, pos, gates, seg_rows, seg_off, x, mods,
                 moe_w_gu, moe_b_gu, moe_w_down, moe_b_down, final_g)

    y_prompt = x[0].reshape(bp, lp, d)
    y_sample = x[1].reshape(bs, ls, d)
    return (y_prompt, y_sample,
            jnp.stack(states["bk"], axis=1), jnp.stack(states["bv"], axis=1),
            jnp.stack(states["ck"], axis=1), jnp.stack(states["cv"], axis=1))
```

```python
import functools

import numpy as np
import jax
import jax.numpy as jnp
from jax import lax
from jax.experimental import pallas as pl
from jax.experimental.pallas import tpu as pltpu

F32 = jnp.float32
BF16 = jnp.bfloat16
I32 = jnp.int32
U32 = jnp.uint32

GRID_W = 64
A_GROUPS = 4
A_GROUP_DIM = 128
A_WIDTH = A_GROUPS * A_GROUP_DIM
B_HEADS = 8
B_KV_HEADS = 2
B_HEAD_DIM = 64
B_Q_WIDTH = B_HEADS * B_HEAD_DIM
B_KV_WIDTH = B_KV_HEADS * B_HEAD_DIM
WINDOW = 128
C_HEADS = 8
C_KV_HEADS = 2
C_HEAD_DIM = 128
C_Q_WIDTH = C_HEADS * C_HEAD_DIM
C_KV_WIDTH = C_KV_HEADS * C_HEAD_DIM
ROPE_THETA = 10000.0
N_EXPERTS = 32
TOP_K = 4
SWIGLU_LIMIT = 7.0
SWIGLU_ALPHA = 1.702
EPS = 1e-6

LANES = 128
TOKEN_TILE = 256
EXPERT_TILE = 128
FFN_TILE = 256
WEIGHT_CAST_BLOCK = 256
ATTN_Q_TILE = 128
DENSE_Q_TILE = 256
ATTN_KEY_CHUNK = 1024
SEG_ALIGN = 8
COPY_SIZES = (32, 16, 8)
WAIT_CHUNK = 256
SORTED_ROWS = -(-(TOKEN_TILE * TOP_K + N_EXPERTS * (SEG_ALIGN - 1)) // LANES) * LANES
PLAN_WIDTHS = (SORTED_ROWS // COPY_SIZES[0],) + (N_EXPERTS,) * (len(COPY_SIZES) - 1)
VMEM_LIMIT = 56 * 1024 * 1024
MASKED = -1e30
N_MOD = 6
MOD_ROWS = 8


def _cparams(*sem):
    return pltpu.CompilerParams(dimension_semantics=tuple(sem), vmem_limit_bytes=VMEM_LIMIT)


def _dot(a, b):
    return jnp.dot(a, b, preferred_element_type=F32)


def _dot_nt(a, b):
    return lax.dot_general(a, b, (((1,), (1,)), ((), ())), preferred_element_type=F32)


def _rms(x, g):
    return x * lax.rsqrt(jnp.mean(x * x, axis=-1, keepdims=True) + EPS) * g


def _dft_tables(n):
    j = np.arange(n, dtype=np.int64)
    ang = 2.0 * np.pi * ((j[:, None] * j[None, :]) % n).astype(np.float64) / n
    s = 1.0 / np.sqrt(n)
    return np.cos(ang) * s, np.sin(ang) * s


def _paired_head_order(kv_heads, groups, dh):
    assert kv_heads * dh == LANES
    return np.array([(kv * groups + g) * dh + d for g in range(groups) for kv in range(kv_heads) for d in range(dh)])


def _rope_tables(n_prompt_rows, n_latent, head_dim):
    quarter = head_dim // 4
    pos = np.arange(n_latent)
    row = (pos // GRID_W).astype(np.float32)
    col = (pos % GRID_W).astype(np.float32)
    inv = (np.float32(ROPE_THETA) ** (-np.arange(quarter, dtype=np.float32) / np.float32(quarter))).astype(np.float32)
    ang_row = (row[:, None] * inv[None, :]).astype(np.float32)
    ang_col = (col[:, None] * inv[None, :]).astype(np.float32)
    cos_h = np.concatenate([np.cos(ang_row)] * 2 + [np.cos(ang_col)] * 2, axis=1)
    sin_h = np.concatenate([-np.sin(ang_row), np.sin(ang_row), -np.sin(ang_col), np.sin(ang_col)], axis=1)
    reps = LANES // head_dim
    cos_l = np.tile(cos_h, (1, reps)).astype(np.float32)
    sin_l = np.tile(sin_h, (1, reps)).astype(np.float32)
    cos = np.concatenate([np.ones((n_prompt_rows, LANES), np.float32), cos_l], axis=0)
    sin = np.concatenate([np.zeros((n_prompt_rows, LANES), np.float32), sin_l], axis=0)
    return cos, sin


def _rope(x, cos, sin, quarter):
    lane = lax.broadcasted_iota(I32, (x.shape[0], LANES), 1)
    first = ((lane // quarter) % 2) == 0
    outs = []
    for c in range(x.shape[1] // LANES):
        xc = x[:, c * LANES:(c + 1) * LANES]
        partner = jnp.where(first, pltpu.roll(xc, LANES - quarter, 1), pltpu.roll(xc, quarter, 1))
        outs.append(xc * cos + partner * sin)
    return outs[0] if len(outs) == 1 else jnp.concatenate(outs, axis=1)


def _head_rms(x, g):
    outs = []
    for c in range(x.shape[1] // LANES):
        outs.append(_rms(x[:, c * LANES:(c + 1) * LANES], g))
    return outs[0] if len(outs) == 1 else jnp.concatenate(outs, axis=1)


def _mod_kernel(c_ref, w_ref, b_ref, o_ref):
    c = c_ref[...]
    s = c * (1.0 / (1.0 + jnp.exp(-c)))
    o_ref[0] = _dot(s.astype(BF16), w_ref[0].astype(BF16)) + b_ref[0]


def _modulation(cond, mod_w, mod_b):
    depth, d, n = mod_w.shape
    tn = 1536
    return pl.pallas_call(
        _mod_kernel,
        out_shape=jax.ShapeDtypeStruct((depth, MOD_ROWS, n), F32),
        grid=(depth, n // tn),
        in_specs=[
            pl.BlockSpec((MOD_ROWS, d), lambda l, j: (0, 0)),
            pl.BlockSpec((1, d, tn), lambda l, j: (l, 0, j)),
            pl.BlockSpec((1, 1, tn), lambda l, j: (l, 0, j)),
        ],
        out_specs=pl.BlockSpec((1, MOD_ROWS, tn), lambda l, j: (l, 0, j)),
        compiler_params=_cparams("arbitrary", "arbitrary"),
        name="modulation",
    )(cond, mod_w, mod_b.reshape(depth, 1, n))


class _Geom:
    def __init__(self, bp, lp, bs, ls):
        self.bp, self.lp, self.bs, self.ls = bp, lp, bs, ls
        self.tp = bp * lp
        self.t = bp * lp + bs * ls
        assert lp == TOKEN_TILE and ls % TOKEN_TILE == 0 and self.tp % ls == 0
        self.n_ptiles = self.tp // TOKEN_TILE
        self.tiles_per_lat = ls // TOKEN_TILE
        self.n_tiles = self.t // TOKEN_TILE

    def group(self, i):
        return jnp.where(i < self.n_ptiles, 0, 1 + (i - self.n_ptiles) // self.tiles_per_lat)

    def pos_block(self, i):
        return jnp.where(i < self.n_ptiles, 0, 1 + (i - self.n_ptiles) % self.tiles_per_lat)


def _mod_spec(geom, layer, which, d):
    def imap(i):
        return ((layer * MOD_ROWS + geom.group(i)) * N_MOD + which, 0, 0)
    return pl.BlockSpec((None, 1, d), imap)


def _stream_specs(geom, x, d):
    tm = TOKEN_TILE
    if isinstance(x, tuple):
        return ([pl.BlockSpec((tm, d), lambda i, *_: (jnp.minimum(i, geom.n_ptiles - 1), 0)),
                 pl.BlockSpec((tm, d), lambda i, *_: (jnp.maximum(i - geom.n_ptiles, 0), 0))], list(x))
    return [pl.BlockSpec((tm, d), lambda i, *_: (i, 0))], [x]


def _stream_tile(x_refs, n_ptiles):
    if len(x_refs) == 1:
        return x_refs[0][...]
    return jnp.where(pl.program_id(0) < n_ptiles, x_refs[0][...], x_refs[1][...])


def _store_kv(k, v, kb_ref, vb_ref, ks_ref, vs_ref, n_ptiles):
    kb_ref[...] = k.astype(BF16)
    vb_ref[...] = v.astype(BF16)

    @pl.when(pl.program_id(0) < n_ptiles)
    def _():
        ks_ref[...] = k
        vs_ref[...] = v


def _proj_even_kernel(*refs, n_x, n_ptiles):
    x_refs = refs[:n_x]
    (g_ref, sh_ref, sc_ref, w_ref, cos_ref, sin_ref, dft_ref,
     tc_ref, ts_ref, q_ref, kb_ref, vb_ref, ks_ref, vs_ref) = refs[n_x:]
    h = _rms(_stream_tile(x_refs, n_ptiles), g_ref[...]) * (1.0 + sc_ref[...]) + sh_ref[...]
    p = _dot(h.astype(BF16), w_ref[...])
    cos = cos_ref[...]
    sin = sin_ref[...]
    dft = dft_ref[...]
    tcs, tss = [], []
    for g in range(A_GROUPS):
        t = _dot(p[:, g * A_GROUP_DIM:(g + 1) * A_GROUP_DIM].astype(BF16), dft)
        tcs.append(t[:, :A_GROUP_DIM])
        tss.append(t[:, A_GROUP_DIM:])
    tc_ref[...] = jnp.concatenate(tcs, axis=1).astype(BF16)
    ts_ref[...] = jnp.concatenate(tss, axis=1).astype(BF16)
    o = A_WIDTH
    q = _rope(p[:, o:o + B_Q_WIDTH], cos, sin, B_HEAD_DIM // 4)
    q_ref[...] = (q * B_HEAD_DIM ** -0.5).astype(BF16)
    o += B_Q_WIDTH
    k = _rope(p[:, o:o + B_KV_WIDTH], cos, sin, B_HEAD_DIM // 4)
    o += B_KV_WIDTH
    _store_kv(k, p[:, o:o + B_KV_WIDTH], kb_ref, vb_ref, ks_ref, vs_ref, n_ptiles)


def _proj_odd_kernel(*refs, n_x, n_ptiles):
    x_refs = refs[:n_x]
    (g_ref, sh_ref, sc_ref, w_ref, cos_ref, sin_ref, qn_ref, kn_ref,
     q_ref, kb_ref, vb_ref, ks_ref, vs_ref) = refs[n_x:]
    h = _rms(_stream_tile(x_refs, n_ptiles), g_ref[...]) * (1.0 + sc_ref[...]) + sh_ref[...]
    p = _dot(h.astype(BF16), w_ref[...])
    cos = cos_ref[...]
    sin = sin_ref[...]
    q = _head_rms(p[:, :C_Q_WIDTH], qn_ref[...])
    k = _head_rms(p[:, C_Q_WIDTH:C_Q_WIDTH + C_KV_WIDTH], kn_ref[...])
    q_ref[...] = (_rope(q, cos, sin, C_HEAD_DIM // 4) * C_HEAD_DIM ** -0.5).astype(BF16)
    k = _rope(k, cos, sin, C_HEAD_DIM // 4)
    _store_kv(k, p[:, C_Q_WIDTH + C_KV_WIDTH:], kb_ref, vb_ref, ks_ref, vs_ref, n_ptiles)


def _in_projection(geom, layer, x, norm_g, mods, w, cos, sin, extras, kernel, out_widths, kv_width, name):
    t, d = geom.t, w.shape[0]
    tm = TOKEN_TILE
    n_out = w.shape[1]
    row = lambda i: (i, 0)
    const2 = lambda i: (0, 0)
    x_specs, x_args = _stream_specs(geom, x, d)
    in_specs = x_specs + [
        pl.BlockSpec((1, d), const2),
        _mod_spec(geom, layer, 0, d),
        _mod_spec(geom, layer, 1, d),
        pl.BlockSpec((d, n_out), const2),
        pl.BlockSpec((tm, LANES), lambda i: (geom.pos_block(i), 0)),
        pl.BlockSpec((tm, LANES), lambda i: (geom.pos_block(i), 0)),
    ] + [pl.BlockSpec(e.shape, const2) for e in extras]
    return pl.pallas_call(
        functools.partial(kernel, n_x=len(x_args), n_ptiles=geom.n_ptiles),
        out_shape=([jax.ShapeDtypeStruct((t, wd), BF16) for wd in out_widths]
                   + [jax.ShapeDtypeStruct((geom.tp, kv_width), F32)] * 2),
        grid=(geom.n_tiles,),
        in_specs=in_specs,
        out_specs=([pl.BlockSpec((tm, wd), row) for wd in out_widths]
                   + [pl.BlockSpec((tm, kv_width), lambda i: (jnp.minimum(i, geom.n_ptiles - 1), 0))] * 2),
        compiler_params=_cparams("arbitrary"),
        name=name,
    )(*x_args, norm_g, mods, mods, w, cos, sin, *extras)


def _fourier_kernel(cl_ref, sl_ref, tc_ref, ts_ref, *rest):
    o_ref = rest[-1]
    o_ref[...] = (_dot(cl_ref[...], tc_ref[...]) - _dot(sl_ref[...], ts_ref[...])).astype(o_ref.dtype)


def _fourier_tokens(tc, ts, cl, sl, n_seq, seq_len, row0, prev):
    t, width = tc.shape
    tr = min(seq_len, 512)
    n_r = seq_len // tr
    assert row0 % seq_len == 0
    seq0 = row0 // seq_len
    out0 = row0 // tr
    in_specs = [
        pl.BlockSpec((tr, seq_len), lambda s, r: (r, 0)),
        pl.BlockSpec((tr, seq_len), lambda s, r: (r, 0)),
        pl.BlockSpec((seq_len, width), lambda s, r: (seq0 + s, 0)),
        pl.BlockSpec((seq_len, width), lambda s, r: (seq0 + s, 0)),
    ]
    args = [cl, sl, tc, ts]
    aliases = {}
    if prev is not None:
        in_specs.append(pl.BlockSpec(memory_space=pl.ANY))
        args.append(prev)
        aliases = {4: 0}
    return pl.pallas_call(
        _fourier_kernel,
        out_shape=jax.ShapeDtypeStruct((t, width), BF16),
        grid=(n_seq, n_r),
        in_specs=in_specs,
        out_specs=pl.BlockSpec((tr, width), lambda s, r: (out0 + s * n_r + r, 0)),
        input_output_aliases=aliases,
        compiler_params=_cparams("arbitrary", "arbitrary"),
        name="fourier_tokens",
    )(*args)


def _attend(q, chunks, sink, o0, dh, den_col):
    m = sink
    acc = None
    den = None
    for k, v, mask in chunks:
        s = _dot_nt(q, k)
        if mask is not None:
            s = jnp.where(mask, s, MASKED)
        m_new = jnp.max(s, axis=-1, keepdims=True)
        if m is not None:
            m_new = jnp.maximum(m, m_new)
        p = jnp.exp(s - m_new)
        pv = _dot(p.astype(BF16), v)
        if acc is None:
            acc = pv
            if den_col is None:
                den = jnp.sum(p, axis=-1, keepdims=True)
        else:
            alpha = jnp.exp(m - m_new)
            acc = alpha * acc + pv
            if den_col is None:
                den = alpha * den + jnp.sum(p, axis=-1, keepdims=True)
        m = m_new
    if den_col is not None:
        den = acc[:, den_col:den_col + 1]
    if sink is not None:
        den = den + jnp.exp(sink - m)
    return acc[:, o0:o0 + dh] / den


def _attend_two_pass(q, chunks, sink):
    scores = []
    m = sink
    for k, _, mask in chunks:
        s = _dot_nt(q, k)
        if mask is not None:
            s = jnp.where(mask, s, MASKED)
        scores.append(s)
        mx = jnp.max(s, axis=-1, keepdims=True)
        m = mx if m is None else jnp.maximum(m, mx)
    den = None if sink is None else jnp.exp(sink - m)
    acc = None
    for (_, v, _), s in zip(chunks, scores):
        e = jnp.exp(s - m)
        es = jnp.sum(e, axis=-1, keepdims=True)
        den = es if den is None else den + es
        o = _dot(e.astype(BF16), v)
        acc = o if acc is None else acc + o
    return acc / den


def _stack_heads(q, kv, groups, dh):
    return jnp.concatenate([q[:, (kv * groups + g) * dh:(kv * groups + g + 1) * dh] for g in range(groups)], axis=0)


def _sink_column(sink_ref, kv, groups, rows):
    return jnp.concatenate([jnp.full((rows, 1), sink_ref[kv * groups + g], F32) for g in range(groups)], axis=0)


def _head_values(v, kv, dh, with_ones):
    if not with_ones:
        return v[:, kv * dh:(kv + 1) * dh], 0, None
    assert dh == LANES
    lane = lax.broadcasted_iota(I32, (v.shape[0], LANES), 1)
    ones = jnp.where(lane == 0, 1.0, 0.0).astype(BF16)
    return jnp.concatenate([v[:, kv * dh:(kv + 1) * dh], ones], axis=1), 0, dh


def _paired_heads_attention(q, sources, sink_ref, groups, dh):
    rows = q.shape[0]
    half_q = lax.broadcasted_iota(I32, (rows, LANES), 1) // dh
    per_head = []
    for kv in range(2):
        qh = jnp.concatenate(
            [jnp.where(half_q == kv, q[:, g * LANES:(g + 1) * LANES], jnp.zeros((), q.dtype)) for g in range(groups)],
            axis=0)
        chunks = []
        for k, v, msk in sources:
            half_v = lax.broadcasted_iota(I32, v.shape, 1) // dh
            chunks.append((k, jnp.where(half_v == kv, v, jnp.zeros((), v.dtype)), msk))
        sink = _sink_column(sink_ref, kv, groups, rows) if sink_ref is not None else None
        per_head.append(_attend_two_pass(qh, chunks, sink))
    return jnp.concatenate(
        [per_head[0][g * rows:(g + 1) * rows] + per_head[1][g * rows:(g + 1) * rows] for g in range(groups)], axis=1)


def _attn_kernel(*refs, kv_heads, groups, dh, has_sink, has_ctx, window, q_tile, seq_len, chunk):
    refs = list(refs)
    sink_ref = refs.pop(0) if has_sink else None
    q_ref, k_ref, v_ref = refs[:3]
    ck_ref, cv_ref = (refs[3], refs[4]) if has_ctx else (None, None)
    o_ref = refs[-1]
    q = q_ref[...]
    rows = q.shape[0]
    if window is None:
        spans = [(c * chunk, chunk) for c in range(seq_len // chunk)]
        mask = None
    else:
        n = pl.program_id(1)
        band = q_tile + 2 * window
        start = pl.multiple_of(jnp.clip(n * q_tile - window, 0, seq_len - band), LANES)
        spans = [(start, band)]
        qpos = n * q_tile + lax.broadcasted_iota(I32, (groups * rows, band), 0) % rows
        kpos = start + lax.broadcasted_iota(I32, (groups * rows, band), 1)
        mask = jnp.abs(kpos - qpos) <= window
    sources = [(k_ref[pl.ds(s0, n_s), :], v_ref[pl.ds(s0, n_s), :], mask) for s0, n_s in spans]
    if has_ctx:
        sources.append((ck_ref[...], cv_ref[...], None))
    if 2 * dh == LANES:
        o_ref[...] = _paired_heads_attention(q, sources, sink_ref, groups, dh).astype(o_ref.dtype)
        return
    outs = []
    for kv in range(kv_heads):
        online = dh == LANES and len(sources) > 1
        chunks = []
        for k, v, msk in sources:
            vh, o0, den_col = _head_values(v, kv, dh, with_ones=online)
            chunks.append((k[:, kv * dh:(kv + 1) * dh], vh, msk))
        sink = _sink_column(sink_ref, kv, groups, rows) if has_sink else None
        qh = _stack_heads(q, kv, groups, dh)
        o = _attend(qh, chunks, sink, o0, dh, den_col) if online else _attend_two_pass(qh, chunks, sink)
        outs.extend(o[g * rows:(g + 1) * rows] for g in range(groups))
    o_ref[...] = jnp.concatenate(outs, axis=1).astype(o_ref.dtype)


def _attention(q, k, v, ctx, sink, prev, *, n_seq, seq_len, row0, q_tile, kv_heads, groups, dh, window):
    t, qw = q.shape
    kw = k.shape[1]
    n_q = seq_len // q_tile
    assert row0 % seq_len == 0 and row0 % q_tile == 0
    seq0 = row0 // seq_len
    q0 = row0 // q_tile
    in_specs, args = [], []
    if sink is not None:
        in_specs.append(pl.BlockSpec(memory_space=pltpu.SMEM))
        args.append(sink)
    in_specs += [
        pl.BlockSpec((q_tile, qw), lambda s, n: (q0 + s * n_q + n, 0)),
        pl.BlockSpec((seq_len, kw), lambda s, n: (seq0 + s, 0)),
        pl.BlockSpec((seq_len, kw), lambda s, n: (seq0 + s, 0)),
    ]
    args += [q, k, v]
    if ctx is not None:
        p = ctx[0].shape[1]
        in_specs += [pl.BlockSpec((None, p, kw), lambda s, n: (s, 0, 0))] * 2
        args += list(ctx)
    aliases = {}
    if prev is not None:
        in_specs.append(pl.BlockSpec(memory_space=pl.ANY))
        aliases = {len(args): 0}
        args.append(prev)
    kern = functools.partial(
        _attn_kernel, kv_heads=kv_heads, groups=groups, dh=dh, has_sink=sink is not None,
        has_ctx=ctx is not None, window=window, q_tile=q_tile, seq_len=seq_len, chunk=min(seq_len, ATTN_KEY_CHUNK))
    return pl.pallas_call(
        kern,
        out_shape=jax.ShapeDtypeStruct((t, qw), BF16),
        grid=(n_seq, n_q),
        in_specs=in_specs,
        out_specs=pl.BlockSpec((q_tile, qw), lambda s, n: (q0 + s * n_q + n, 0)),
        input_output_aliases=aliases,
        compiler_params=_cparams("arbitrary", "arbitrary"),
        name="attention",
    )(*args)


def _out_route_kernel(*refs, n_mix, n_x, n_ptiles):
    mix_refs = refs[:n_mix]
    w_refs = refs[n_mix:2 * n_mix]
    x_refs = refs[2 * n_mix:2 * n_mix + n_x]
    (gate_ref, g2_ref, sh2_ref, sc2_ref, rw_ref, rb_ref,
     xo_ref, h_ref, pos_ref, gates_ref, rows_ref, off_ref) = refs[2 * n_mix + n_x:]

    acc = None
    for m_ref, w_ref in zip(mix_refs, w_refs):
        part = _dot(m_ref[...], w_ref[...])
        acc = part if acc is None else acc + part
    xn = _stream_tile(x_refs, n_ptiles) + gate_ref[...] * acc
    xo_ref[...] = xn
    h = _rms(xn, g2_ref[...]) * (1.0 + sc2_ref[...]) + sh2_ref[...]
    hb = h.astype(BF16)
    h_ref[...] = hb

    logits = _dot_nt(rw_ref[...], hb) + rb_ref[...]
    ne, tm = logits.shape
    expert = lax.broadcasted_iota(I32, (ne, tm), 0).astype(F32)
    work = logits
    sels, vals = [], []
    for _ in range(TOP_K):
        mx = jnp.max(work, axis=0, keepdims=True)
        first = jnp.min(jnp.where(work == mx, expert, float(ne)), axis=0, keepdims=True)
        sel = expert == first
        work = jnp.where(sel, -jnp.inf, work)
        sels.append(sel)
        vals.append(mx)
    exps = [jnp.exp(v - vals[0]) for v in vals]
    den = exps[0] + exps[1] + exps[2] + exps[3]

    onehot = jnp.zeros((ne, tm), F32)
    for sel in sels:
        onehot = onehot + sel.astype(F32)
    r_i = lax.broadcasted_iota(I32, (tm, tm), 0)
    c_i = lax.broadcasted_iota(I32, (tm, tm), 1)
    before = jnp.where(r_i < c_i, 1.0, 0.0).astype(BF16)
    earlier = _dot(onehot.astype(BF16), before)

    cnt = jnp.sum(onehot, axis=1, keepdims=True)
    seg8 = jnp.floor((cnt + (SEG_ALIGN - 1.0)) * (1.0 / SEG_ALIGN))
    e_r = lax.broadcasted_iota(I32, (ne, ne), 0)
    e_c = lax.broadcasted_iota(I32, (ne, ne), 1)
    lower = jnp.where(e_c < e_r, 1.0, 0.0).astype(BF16)
    off8 = _dot(lower, jnp.broadcast_to(seg8, (ne, LANES)).astype(BF16))[:, 0:1]
    seg_off = off8 * SEG_ALIGN
    base = seg_off + earlier

    sub = lax.broadcasted_iota(I32, (2 * TOP_K, tm), 0)
    token_rows = jnp.zeros((2 * TOP_K, tm), F32)
    for k in range(TOP_K):
        pos_k = jnp.sum(jnp.where(sels[k], base, 0.0), axis=0, keepdims=True)
        token_rows = jnp.where(sub == k, pos_k, token_rows)
        token_rows = jnp.where(sub == TOP_K + k, exps[k] / den, token_rows)
    first_lane = lax.broadcasted_iota(I32, (ne, tm), 1) == 0
    block = jnp.concatenate([
        token_rows,
        jnp.where(first_lane, seg8 * SEG_ALIGN, 0.0),
        jnp.where(first_lane, seg_off, 0.0),
        jnp.zeros((LANES - 2 * TOP_K - 2 * ne, tm), F32)], axis=0)
    by_token = block.T
    pos_ref[...] = by_token[:, 0:TOP_K].astype(I32)
    gates_ref[...] = by_token[:, TOP_K:2 * TOP_K]
    rows_ref[0] = by_token[0:1, 2 * TOP_K:2 * TOP_K + ne].astype(I32)
    off_ref[0] = by_token[0:1, 2 * TOP_K + ne:2 * TOP_K + 2 * ne].astype(I32)


def _out_route(geom, layer, mixes, w_parts, x, mods, norm_g, router_w, router_b):
    t, d = geom.t, w_parts[0].shape[1]
    tm = TOKEN_TILE
    row = lambda i: (i, 0)
    const2 = lambda i: (0, 0)
    x_specs, x_args = _stream_specs(geom, x, d)
    in_specs = [pl.BlockSpec((tm, m.shape[1]), row) for m in mixes]
    in_specs += [pl.BlockSpec(w.shape, const2) for w in w_parts]
    in_specs += x_specs
    in_specs += [
        _mod_spec(geom, layer, 2, d),
        pl.BlockSpec((1, d), const2),
        _mod_spec(geom, layer, 3, d),
        _mod_spec(geom, layer, 4, d),
        pl.BlockSpec(router_w.shape, const2),
        pl.BlockSpec((N_EXPERTS, 1), const2),
    ]
    seg3 = lambda i: (i, 0, 0)
    out_shape = [
        jax.ShapeDtypeStruct((t, d), F32),
        jax.ShapeDtypeStruct((t, d), BF16),
        jax.ShapeDtypeStruct((t, TOP_K), I32),
        jax.ShapeDtypeStruct((t, TOP_K), F32),
        jax.ShapeDtypeStruct((geom.n_tiles, 1, N_EXPERTS), I32),
        jax.ShapeDtypeStruct((geom.n_tiles, 1, N_EXPERTS), I32),
    ]
    out_specs = [
        pl.BlockSpec((tm, d), row),
        pl.BlockSpec((tm, d), row),
        pl.BlockSpec((tm, TOP_K), row),
        pl.BlockSpec((tm, TOP_K), row),
        pl.BlockSpec((1, 1, N_EXPERTS), seg3),
        pl.BlockSpec((1, 1, N_EXPERTS), seg3),
    ]
    return pl.pallas_call(
        functools.partial(_out_route_kernel, n_mix=len(mixes), n_x=len(x_args), n_ptiles=geom.n_ptiles),
        out_shape=out_shape,
        grid=(geom.n_tiles,),
        in_specs=in_specs,
        out_specs=out_specs,
        compiler_params=_cparams("arbitrary"),
        name="out_route",
    )(*mixes, *w_parts, *x_args, mods, norm_g, mods, mods, router_w, router_b)


def _pack_pairs(v):
    n = v.shape[1] // 2
    bits = lax.bitcast_convert_type(v, U32)
    return (bits[:, :n] & jnp.uint32(0xFFFF0000)) | (bits[:, n:] >> 16)


def _unpack_pairs(p):
    hi = lax.bitcast_convert_type(p & jnp.uint32(0xFFFF0000), F32)
    lo = lax.bitcast_convert_type(p << 16, F32)
    return jnp.concatenate([hi, lo], axis=1).astype(BF16)


def _planned_copies(i, plan_refs, make_copy):
    for size, width, (local_ref, slot_ref, count_ref) in zip(COPY_SIZES, PLAN_WIDTHS, plan_refs):
        def one(c, carry, size=size, width=width, local_ref=local_ref, slot_ref=slot_ref):
            a = i * width + c
            make_copy(pl.multiple_of(local_ref[a], SEG_ALIGN), pl.multiple_of(slot_ref[a], SEG_ALIGN), size).start()
            return carry

        lax.fori_loop(0, count_ref[i], one, 0)


def _copy_plan(rows, seg_off, seg_dst):
    plan = []
    experts = jnp.arange(N_EXPERTS, dtype=I32)
    for n, (size, width) in enumerate(zip(COPY_SIZES, PLAN_WIDTHS)):
        if n == 0:
            count, done = rows // size, jnp.zeros_like(rows)
        else:
            count, done = (rows % (2 * size)) // size, rows - rows % (2 * size)
        cum = jnp.cumsum(count, axis=1)
        j = jnp.arange(width, dtype=I32)
        owner = jnp.sum((cum[:, None, :] <= j[None, :, None]).astype(I32), axis=2)
        pick = (jnp.minimum(owner, N_EXPERTS - 1)[:, :, None] == experts[None, None, :]).astype(I32)
        take = lambda v: jnp.sum(pick * v[:, None, :], axis=2)
        within = (j[None, :] - take(cum - count)) * size
        plan += [(take(seg_off + done) + within).reshape(-1).astype(I32),
                 (take(seg_dst + done) + within).reshape(-1).astype(I32),
                 cum[:, -1].astype(I32)]
    return plan


def _wait_copies(n_rows, make_copy):
    def wait_big(c, carry):
        make_copy(0, 0, WAIT_CHUNK).wait()
        return carry
    lax.fori_loop(0, n_rows // WAIT_CHUNK, wait_big, 0)
    size = WAIT_CHUNK // 2
    while size >= SEG_ALIGN:
        @pl.when(n_rows % (2 * size) >= size)
        def _(size=size):
            make_copy(0, 0, size).wait()

        size //= 2


def _dispatch_kernel(*refs):
    plan_refs, (tot_ref, h_ref, pos_ref, xs_ref, sorted_ref, sem) = _split_plan(refs)
    i = pl.program_id(0)
    buf = i % 2
    tm = h_ref.shape[0]
    n_sorted = sorted_ref.shape[1]
    pos = pos_ref[...]
    lane = lax.broadcasted_iota(I32, (tm, n_sorted), 1)
    hit = jnp.zeros((tm, n_sorted), F32)
    for k in range(TOP_K):
        hit = jnp.where(pos[:, k:k + 1] == lane, 1.0, hit)
    sorted_ref[buf] = _pack_pairs(lax.dot_general(hit.astype(BF16), h_ref[...], (((0,), (0,)), ((), ())),
                                                  preferred_element_type=F32))

    def copies_from(b):
        def make_copy(local, slot, rows):
            return pltpu.make_async_copy(sorted_ref.at[b, pl.ds(local, rows)], xs_ref.at[pl.ds(slot, rows)],
                                         sem.at[b])
        return make_copy

    _planned_copies(i, plan_refs, copies_from(buf))

    @pl.when(i > 0)
    def _():
        _wait_copies(tot_ref[jnp.maximum(i - 1, 0)], copies_from(1 - buf))

    @pl.when(i == pl.num_programs(0) - 1)
    def _():
        _wait_copies(tot_ref[i], copies_from(buf))


def _split_plan(refs):
    n = 3 * len(COPY_SIZES)
    return [refs[k:k + 3] for k in range(0, n, 3)], refs[n:]


def _dispatch(h, pos, plan, tile_rows, n_slots):
    t, d = h.shape
    tm = TOKEN_TILE
    return pl.pallas_call(
        _dispatch_kernel,
        out_shape=jax.ShapeDtypeStruct((n_slots, d // 2), U32),
        grid_spec=pltpu.PrefetchScalarGridSpec(
            num_scalar_prefetch=len(plan) + 1,
            grid=(t // tm,),
            in_specs=[
                pl.BlockSpec((tm, d), lambda i, *_: (i, 0)),
                pl.BlockSpec((tm, TOP_K), lambda i, *_: (i, 0)),
            ],
            out_specs=pl.BlockSpec(memory_space=pl.ANY),
            scratch_shapes=[pltpu.VMEM((2, SORTED_ROWS, d // 2), U32), pltpu.SemaphoreType.DMA((2,))],
        ),
        compiler_params=_cparams("arbitrary"),
        name="moe_dispatch",
    )(*plan, tile_rows, h, pos)


def _ffn_kernel(rows_ref, start_ref, wgu_ref, bgu_ref, wd_ref, bd_ref, xs_ref, ys_ref,
                wgu_bf, wd_bf, xbuf, ybuf, sem_in, sem_out):
    e = pl.program_id(0)
    d_ff = wd_ref.shape[1]
    tb = xbuf.shape[1]
    n_rows = rows_ref[e]
    n_tiles = (n_rows + tb - 1) // tb
    base = start_ref[e]

    half = tb // 2
    last_small = (n_rows - (n_tiles - 1) * tb) <= half

    cast_in_first_tile = jnp.logical_or(n_tiles >= 2, jnp.logical_not(last_small))

    @pl.when(jnp.logical_not(cast_in_first_tile))
    def _():
        chunk = 128

        def cast_gu(c, carry):
            r = pl.multiple_of(c * chunk, chunk)
            wgu_bf[pl.ds(r, chunk), :] = wgu_ref[0, pl.ds(r, chunk), :].astype(BF16)
            return carry
        lax.fori_loop(0, wgu_ref.shape[1] // chunk, cast_gu, 0)

        def cast_d(c, carry):
            r = pl.multiple_of(c * chunk, chunk)
            wd_bf[pl.ds(r, chunk), :] = wd_ref[0, pl.ds(r, chunk), :].astype(BF16)
            return carry
        lax.fori_loop(0, d_ff // chunk, cast_d, 0)

    def x_copy(s, slot):
        r = pl.multiple_of(base + s * tb, EXPERT_TILE)
        return pltpu.make_async_copy(xs_ref.at[pl.ds(r, tb)], xbuf.at[slot], sem_in.at[slot])

    def y_copy(s, slot, rows=tb):
        r = pl.multiple_of(base + s * tb, EXPERT_TILE)
        return pltpu.make_async_copy(ybuf.at[slot, pl.ds(0, rows)], ys_ref.at[pl.ds(r, rows)], sem_out.at[slot])

    def matmul(lhs, w_ref, w_bf, cast_now):
        if not cast_now:
            return _dot(lhs, w_bf[...])
        outs = []
        for c in range(w_bf.shape[1] // WEIGHT_CAST_BLOCK):
            cols = slice(c * WEIGHT_CAST_BLOCK, (c + 1) * WEIGHT_CAST_BLOCK)
            w = w_ref[0, :, cols].astype(BF16)
            w_bf[:, cols] = w
            outs.append(_dot(lhs, w))
        return jnp.concatenate(outs, axis=1)

    def ffn_rows(s, slot, n, cast_now=False):
        rows = s * tb + lax.broadcasted_iota(I32, (n, 1), 0)
        x = _unpack_pairs(jnp.where(rows < n_rows, xbuf[slot, pl.ds(0, n), :], jnp.uint32(0)))
        gu = matmul(x, wgu_ref, wgu_bf, cast_now) + bgu_ref[0]
        gate = jnp.minimum(gu[:, :d_ff], SWIGLU_LIMIT)
        up = jnp.clip(gu[:, d_ff:], -SWIGLU_LIMIT, SWIGLU_LIMIT)
        act = (up + 1.0) * (gate * (1.0 / (1.0 + jnp.exp(-SWIGLU_ALPHA * gate))))
        y = matmul(act.astype(BF16), wd_ref, wd_bf, cast_now) + bd_ref[0]
        ybuf[slot, pl.ds(0, n), :] = _pack_pairs(y.astype(BF16).astype(F32))
        y_copy(s, slot, n).start()

    @pl.when(n_tiles > 0)
    def _():
        x_copy(0, 0).start()

    def tile(s, carry):
        slot = s % 2
        x_copy(s, slot).wait()

        @pl.when(s + 1 < n_tiles)
        def _():
            x_copy(s + 1, 1 - slot).start()

        @pl.when(s >= 2)
        def _():
            y_copy(s - 2, slot).wait()

        small = jnp.logical_and(s == n_tiles - 1, last_small)
        casting = jnp.logical_and(s == 0, cast_in_first_tile)

        @pl.when(casting)
        def _():
            ffn_rows(s, slot, tb, cast_now=True)

        @pl.when(jnp.logical_not(jnp.logical_or(small, casting)))
        def _():
            ffn_rows(s, slot, tb)

        @pl.when(small)
        def _():
            ffn_rows(s, slot, half)

        return carry

    lax.fori_loop(0, n_tiles, tile, 0)

    @pl.when(n_tiles >= 2)
    def _():
        y_copy(n_tiles - 2, n_tiles % 2).wait()

    @pl.when(jnp.logical_and(n_tiles >= 1, jnp.logical_not(last_small)))
    def _():
        y_copy(n_tiles - 1, (n_tiles - 1) % 2).wait()

    @pl.when(jnp.logical_and(n_tiles >= 1, last_small))
    def _():
        y_copy(n_tiles - 1, (n_tiles - 1) % 2, half).wait()


def _expert_ffn(layer, xs, expert_rows, expert_start, w_gu, b_gu, w_down, b_down):
    n_slots, packed_w = xs.shape
    tb = FFN_TILE
    depth, ne, d, two_f = w_gu.shape
    d_ff = two_f // 2
    exp4 = lambda e, *_: (layer, e, 0, 0)
    return pl.pallas_call(
        _ffn_kernel,
        out_shape=jax.ShapeDtypeStruct((n_slots, packed_w), U32),
        grid_spec=pltpu.PrefetchScalarGridSpec(
            num_scalar_prefetch=2,
            grid=(ne,),
            in_specs=[
                pl.BlockSpec((None, 1, d, two_f), exp4),
                pl.BlockSpec((None, 1, 1, two_f), exp4),
                pl.BlockSpec((None, 1, d_ff, d), exp4),
                pl.BlockSpec((None, 1, 1, d), exp4),
                pl.BlockSpec(memory_space=pl.ANY),
            ],
            out_specs=pl.BlockSpec(memory_space=pl.ANY),
            scratch_shapes=[
                pltpu.VMEM((d, two_f), BF16), pltpu.VMEM((d_ff, d), BF16),
                pltpu.VMEM((2, tb, packed_w), U32), pltpu.VMEM((2, tb, packed_w), U32),
                pltpu.SemaphoreType.DMA((2,)), pltpu.SemaphoreType.DMA((2,)),
            ],
        ),
        compiler_params=_cparams("arbitrary"),
        name="expert_ffn",
    )(expert_rows, expert_start, w_gu, b_gu.reshape(depth, ne, 1, two_f),
      w_down, b_down.reshape(depth, ne, 1, d), xs)


def _combine_kernel(*refs, final, n_ptiles):
    plan_refs, (tot_ref, x_ref, pos_ref, gates_ref, mg_ref, *rest) = _split_plan(refs)
    if final:
        fn_ref, ys_ref, op_ref, os_ref, buf, sem = rest
    else:
        ys_ref, o_ref, buf, sem = rest
    i = pl.program_id(0)
    cur = i % 2
    tm = x_ref.shape[0]
    n_sorted = buf.shape[1]

    def copies_into(b):
        def make_copy(local, slot, rows):
            return pltpu.make_async_copy(ys_ref.at[pl.ds(slot, rows)], buf.at[b, pl.ds(local, rows)], sem.at[b])
        return make_copy

    @pl.when(i == 0)
    def _():
        buf[...] = jnp.zeros_like(buf)
        _planned_copies(i, plan_refs, copies_into(cur))

    @pl.when(i + 1 < pl.num_programs(0))
    def _():
        _planned_copies(i + 1, plan_refs, copies_into(1 - cur))

    pos = pos_ref[...]
    g = gates_ref[...]
    lane = lax.broadcasted_iota(I32, (tm, n_sorted), 1)
    weight = jnp.zeros((tm, n_sorted), F32)
    for k in range(TOP_K):
        weight = jnp.where(pos[:, k:k + 1] == lane, g[:, k:k + 1], weight)
    _wait_copies(tot_ref[i], copies_into(cur))
    y = _dot(weight.astype(BF16), _unpack_pairs(buf[cur]))
    xn = x_ref[...] + mg_ref[...] * y
    if final:
        xn = _rms(xn, fn_ref[...])

        @pl.when(i < n_ptiles)
        def _():
            op_ref[...] = xn

        @pl.when(i >= n_ptiles)
        def _():
            os_ref[...] = xn
    else:
        o_ref[...] = xn


def _combine(geom, layer, ys, pos, plan, tile_rows, x, gates, mods, final_g):
    t, d = x.shape
    tm = TOKEN_TILE
    final = final_g is not None

    def mod_imap(i, *_):
        return ((layer * MOD_ROWS + geom.group(i)) * N_MOD + 5, 0, 0)

    row = lambda i, *_: (i, 0)
    in_specs = [
        pl.BlockSpec((tm, d), row),
        pl.BlockSpec((tm, TOP_K), row),
        pl.BlockSpec((tm, TOP_K), row),
        pl.BlockSpec((None, 1, d), mod_imap),
    ]
    args = [*plan, tile_rows, x, pos, gates, mods]
    if final:
        in_specs.append(pl.BlockSpec((1, d), lambda i, *_: (0, 0)))
        args.append(final_g)
    in_specs.append(pl.BlockSpec(memory_space=pl.ANY))
    args.append(ys)
    if final:
        n_pt = geom.n_ptiles
        out_shape = [jax.ShapeDtypeStruct((geom.tp, d), F32), jax.ShapeDtypeStruct((t - geom.tp, d), F32)]
        out_specs = [pl.BlockSpec((tm, d), lambda i, *_: (jnp.minimum(i, n_pt - 1), 0)),
                     pl.BlockSpec((tm, d), lambda i, *_: (jnp.maximum(i - n_pt, 0), 0))]
    else:
        out_shape = jax.ShapeDtypeStruct((t, d), F32)
        out_specs = pl.BlockSpec((tm, d), row)
    return pl.pallas_call(
        functools.partial(_combine_kernel, final=final, n_ptiles=geom.n_ptiles),
        out_shape=out_shape,
        grid_spec=pltpu.PrefetchScalarGridSpec(
            num_scalar_prefetch=len(plan) + 1,
            grid=(t // tm,),
            in_specs=in_specs,
            out_specs=out_specs,
            scratch_shapes=[pltpu.VMEM((2, SORTED_ROWS, d // 2), U32), pltpu.SemaphoreType.DMA((2,))],
        ),
        compiler_params=_cparams("arbitrary"),
        name="moe_combine",
    )(*args)


def _moe(geom, layer, h, pos, gates, seg_rows, seg_off, x, mods, w_gu, b_gu, w_down, b_down, final_g):
    t = h.shape[0]
    tb = EXPERT_TILE
    n_tok_tiles = seg_rows.shape[0]
    max_rows = t * TOP_K + n_tok_tiles * N_EXPERTS * (SEG_ALIGN - 1)
    n_blocks = -(-max_rows // tb) + N_EXPERTS
    rows = seg_rows[:, 0, :]
    cnt = jnp.sum(rows, axis=0)
    n_tiles_e = (cnt + tb - 1) // tb
    tile_end = jnp.cumsum(n_tiles_e)
    tile_start = tile_end - n_tiles_e
    expert_start = (tile_start * tb).astype(I32)
    seg_dst = expert_start[None, :] + jnp.cumsum(rows, axis=0) - rows
    plan = _copy_plan(rows, seg_off[:, 0, :], seg_dst)
    tile_rows = jnp.sum(rows, axis=1).astype(I32)
    xs = _dispatch(h, pos, plan, tile_rows, n_blocks * tb + FFN_TILE - tb)
    ys = _expert_ffn(layer, xs, cnt.astype(I32), expert_start, w_gu, b_gu, w_down, b_down)
    return _combine(geom, layer, ys, pos, plan, tile_rows, x, gates, mods, final_g)


def kernel(x_prompt, x_sample, cache_b_k, cache_b_v, cache_c_k, cache_c_v, c, c_ctx,
           mod_w, mod_b, norm_mix, norm_ffn, even_w_in, even_w_out, even_sink,
           odd_w_in, odd_w_out, odd_q_norm, odd_k_norm, router_w, router_b,
           moe_w_gu, moe_b_gu, moe_w_down, moe_b_down, final_norm):
    bp, lp, d = x_prompt.shape
    bs, ls, _ = x_sample.shape
    past = cache_b_k.shape[2]
    depth = mod_w.shape[0]
    geom = _Geom(bp, lp, bs, ls)
    tp = geom.tp

    x = (x_prompt.reshape(tp, d), x_sample.reshape(bs * ls, d))
    cond = jnp.concatenate([c_ctx[None, :], c, jnp.zeros((MOD_ROWS - 1 - bs, d), F32)], axis=0)
    mods = _modulation(cond, mod_w, mod_b).reshape(depth * MOD_ROWS * N_MOD, 1, d)

    cn, sn = _dft_tables(A_GROUP_DIM)
    dft_chan = jnp.asarray(np.concatenate([cn, sn], axis=1), BF16)
    dft_p = [jnp.asarray(m, BF16) for m in _dft_tables(lp)]
    dft_s = [jnp.asarray(m, BF16) for m in _dft_tables(ls)]
    rope_b = [jnp.asarray(m) for m in _rope_tables(TOKEN_TILE, ls, B_HEAD_DIM)]
    rope_c = [jnp.asarray(m) for m in _rope_tables(TOKEN_TILE, ls, C_HEAD_DIM)]

    states = {"bk": [], "bv": [], "ck": [], "cv": []}
    for layer in range(depth):
        j = layer // 2
        g_mix = norm_mix[layer][None, :]
        g_ffn = norm_ffn[layer][None, :]
        if layer % 2 == 0:
            pair = _paired_head_order(B_KV_HEADS, B_HEADS // B_KV_HEADS, B_HEAD_DIM)
            w_in = even_w_in[j]
            w_in = jnp.concatenate([w_in[:, :A_WIDTH], w_in[:, A_WIDTH:A_WIDTH + B_Q_WIDTH][:, pair],
                                    w_in[:, A_WIDTH + B_Q_WIDTH:]], axis=1).astype(BF16)
            tc, ts, q, k, v, k_state, v_state = _in_projection(
                geom, layer, x, g_mix, mods, w_in, rope_b[0], rope_b[1], [dft_chan],
                _proj_even_kernel, (A_WIDTH, A_WIDTH, B_Q_WIDTH, B_KV_WIDTH, B_KV_WIDTH), B_KV_WIDTH, "proj_even")
            states["bk"].append(k_state.reshape(bp, lp, B_KV_HEADS, B_HEAD_DIM))
            states["bv"].append(v_state.reshape(bp, lp, B_KV_HEADS, B_HEAD_DIM))
            four = _fourier_tokens(tc, ts, dft_p[0], dft_p[1], bp, lp, 0, None)
            four = _fourier_tokens(tc, ts, dft_s[0], dft_s[1], bs, ls, tp, four)
            sink = even_sink[j]
            common = dict(kv_heads=B_KV_HEADS, groups=B_HEADS // B_KV_HEADS, dh=B_HEAD_DIM)
            att = _attention(q, k, v, None, sink, None, n_seq=bp, seq_len=lp, row0=0, q_tile=lp,
                             window=None, **common)
            ctx = (cache_b_k[:, j].reshape(bs, past, B_KV_WIDTH).astype(BF16),
                   cache_b_v[:, j].reshape(bs, past, B_KV_WIDTH).astype(BF16))
            att = _attention(q, k, v, ctx, sink, att, n_seq=bs, seq_len=ls, row0=tp, q_tile=ATTN_Q_TILE,
                             window=WINDOW, **common)
            w_out = even_w_out[j].astype(BF16)
            mixes = [four, att]
            w_parts = [w_out[:A_WIDTH], w_out[A_WIDTH:][pair]]
        else:
            q, k, v, k_state, v_state = _in_projection(
                geom, layer, x, g_mix, mods, odd_w_in[j].astype(BF16), rope_c[0], rope_c[1],
                [odd_q_norm[j][None, :], odd_k_norm[j][None, :]],
                _proj_odd_kernel, (C_Q_WIDTH, C_KV_WIDTH, C_KV_WIDTH), C_KV_WIDTH, "proj_odd")
            states["ck"].append(k_state.reshape(bp, lp, C_KV_HEADS, C_HEAD_DIM))
            states["cv"].append(v_state.reshape(bp, lp, C_KV_HEADS, C_HEAD_DIM))
            common = dict(kv_heads=C_KV_HEADS, groups=C_HEADS // C_KV_HEADS, dh=C_HEAD_DIM, window=None)
            att = _attention(q, k, v, None, None, None, n_seq=bp, seq_len=lp, row0=0, q_tile=lp, **common)
            ctx = (cache_c_k[:, j].reshape(bs, past, C_KV_WIDTH).astype(BF16),
                   cache_c_v[:, j].reshape(bs, past, C_KV_WIDTH).astype(BF16))
            att = _attention(q, k, v, ctx, None, att, n_seq=bs, seq_len=ls, row0=tp, q_tile=DENSE_Q_TILE, **common)
            mixes = [att]
            w_parts = [odd_w_out[j].astype(BF16)]
        x, h, pos, gates, seg_rows, seg_off = _out_route(
            geom, layer, mixes, w_parts, x, mods, g_ffn, router_w[layer].T.astype(BF16), router_b[layer][:, None])
        final_g = final_norm[None, :] if layer == depth - 1 else None
        x = _moe(geom, layer, h, pos, gates, seg_rows, seg_off, x, mods,
                 moe_w_gu, moe_b_gu, moe_w_down, moe_b_down, final_g)

    y_prompt = x[0].reshape(bp, lp, d)
    y_sample = x[1].reshape(bs, ls, d)
    return (y_prompt, y_sample,
            jnp.stack(states["bk"], axis=1), jnp.stack(states["bv"], axis=1),
            jnp.stack(states["ck"], axis=1), jnp.stack(states["cv"], axis=1))
```

```python
import functools

import numpy as np
import jax
import jax.numpy as jnp
from jax import lax
from jax.experimental import pallas as pl
from jax.experimental.pallas import tpu as pltpu

F32 = jnp.float32
BF16 = jnp.bfloat16
I32 = jnp.int32
U32 = jnp.uint32

GRID_W = 64
A_GROUPS = 4
A_GROUP_DIM = 128
A_WIDTH = A_GROUPS * A_GROUP_DIM
B_HEADS = 8
B_KV_HEADS = 2
B_HEAD_DIM = 64
B_Q_WIDTH = B_HEADS * B_HEAD_DIM
B_KV_WIDTH = B_KV_HEADS * B_HEAD_DIM
WINDOW = 128
C_HEADS = 8
C_KV_HEADS = 2
C_HEAD_DIM = 128
C_Q_WIDTH = C_HEADS * C_HEAD_DIM
C_KV_WIDTH = C_KV_HEADS * C_HEAD_DIM
ROPE_THETA = 10000.0
N_EXPERTS = 32
TOP_K = 4
SWIGLU_LIMIT = 7.0
SWIGLU_ALPHA = 1.702
EPS = 1e-6

LANES = 128
SUBLANES = 8
TOKEN_TILE = 256
EXPERT_TILE = 128
FFN_TILE = 256
ATTN_Q_TILE = 128
DENSE_Q_TILE = 256
ATTN_KEY_CHUNK = 1024
SEG_ALIGN = SUBLANES
COPY_SIZES = (32, 16, 8)
WAIT_CHUNK = 256
SORTED_ROWS = -(-(TOKEN_TILE * TOP_K + N_EXPERTS * (SEG_ALIGN - 1)) // LANES) * LANES
PLAN_WIDTHS = (SORTED_ROWS // COPY_SIZES[0],) + (N_EXPERTS,) * (len(COPY_SIZES) - 1)
VMEM_LIMIT = 56 * 1024 * 1024
MASKED = -1e30
N_MOD = 6
MOD_ROWS = SUBLANES


def _cparams(*sem):
    return pltpu.CompilerParams(dimension_semantics=tuple(sem), vmem_limit_bytes=VMEM_LIMIT)


def _dot(a, b):
    return jnp.dot(a, b, preferred_element_type=F32)


def _dot_nt(a, b):
    return lax.dot_general(a, b, (((1,), (1,)), ((), ())), preferred_element_type=F32)


def _rms(x, g):
    return x * lax.rsqrt(jnp.mean(x * x, axis=-1, keepdims=True) + EPS) * g


def _dft_tables(n):
    j = np.arange(n, dtype=np.int64)
    ang = 2.0 * np.pi * ((j[:, None] * j[None, :]) % n).astype(np.float64) / n
    s = 1.0 / np.sqrt(n)
    return np.cos(ang) * s, np.sin(ang) * s


def _paired_head_order(kv_heads, groups, dh):
    assert kv_heads * dh == LANES
    return np.array([(kv * groups + g) * dh + d for g in range(groups) for kv in range(kv_heads) for d in range(dh)])


def _rope_tables(n_prompt_rows, n_latent, head_dim):
    quarter = head_dim // 4
    pos = np.arange(n_latent)
    row = (pos // GRID_W).astype(np.float32)
    col = (pos % GRID_W).astype(np.float32)
    inv = (np.float32(ROPE_THETA) ** (-np.arange(quarter, dtype=np.float32) / np.float32(quarter))).astype(np.float32)
    ang_row = (row[:, None] * inv[None, :]).astype(np.float32)
    ang_col = (col[:, None] * inv[None, :]).astype(np.float32)
    cos_h = np.concatenate([np.cos(ang_row)] * 2 + [np.cos(ang_col)] * 2, axis=1)
    sin_h = np.concatenate([-np.sin(ang_row), np.sin(ang_row), -np.sin(ang_col), np.sin(ang_col)], axis=1)
    reps = LANES // head_dim
    cos_l = np.tile(cos_h, (1, reps)).astype(np.float32)
    sin_l = np.tile(sin_h, (1, reps)).astype(np.float32)
    cos = np.concatenate([np.ones((n_prompt_rows, LANES), np.float32), cos_l], axis=0)
    sin = np.concatenate([np.zeros((n_prompt_rows, LANES), np.float32), sin_l], axis=0)
    return cos, sin


def _rope(x, cos, sin, quarter):
    lane = lax.broadcasted_iota(I32, (x.shape[0], LANES), 1)
    first = ((lane // quarter) % 2) == 0
    outs = []
    for c in range(x.shape[1] // LANES):
        xc = x[:, c * LANES:(c + 1) * LANES]
        partner = jnp.where(first, pltpu.roll(xc, LANES - quarter, 1), pltpu.roll(xc, quarter, 1))
        outs.append(xc * cos + partner * sin)
    return outs[0] if len(outs) == 1 else jnp.concatenate(outs, axis=1)


def _head_rms(x, g):
    outs = []
    for c in range(x.shape[1] // LANES):
        outs.append(_rms(x[:, c * LANES:(c + 1) * LANES], g))
    return outs[0] if len(outs) == 1 else jnp.concatenate(outs, axis=1)


def _mod_kernel(c_ref, w_ref, b_ref, o_ref):
    c = c_ref[...]
    s = c * (1.0 / (1.0 + jnp.exp(-c)))
    o_ref[0] = _dot(s.astype(BF16), w_ref[0].astype(BF16)) + b_ref[0]


def _modulation(cond, mod_w, mod_b):
    depth, d, n = mod_w.shape
    tn = 1536
    return pl.pallas_call(
        _mod_kernel,
        out_shape=jax.ShapeDtypeStruct((depth, MOD_ROWS, n), F32),
        grid=(depth, n // tn),
        in_specs=[
            pl.BlockSpec((MOD_ROWS, d), lambda l, j: (0, 0)),
            pl.BlockSpec((1, d, tn), lambda l, j: (l, 0, j)),
            pl.BlockSpec((1, 1, tn), lambda l, j: (l, 0, j)),
        ],
        out_specs=pl.BlockSpec((1, MOD_ROWS, tn), lambda l, j: (l, 0, j)),
        compiler_params=_cparams("arbitrary", "arbitrary"),
        name="modulation",
    )(cond, mod_w, mod_b.reshape(depth, 1, n))


class _Geom:
    def __init__(self, bp, lp, bs, ls):
        self.bp, self.lp, self.bs, self.ls = bp, lp, bs, ls
        self.tp = bp * lp
        self.t = bp * lp + bs * ls
        assert lp == TOKEN_TILE and ls % TOKEN_TILE == 0 and self.tp % ls == 0
        self.n_ptiles = self.tp // TOKEN_TILE
        self.tiles_per_lat = ls // TOKEN_TILE
        self.n_tiles = self.t // TOKEN_TILE

    def group(self, i):
        return jnp.where(i < self.n_ptiles, 0, 1 + (i - self.n_ptiles) // self.tiles_per_lat)

    def pos_block(self, i):
        return jnp.where(i < self.n_ptiles, 0, 1 + (i - self.n_ptiles) % self.tiles_per_lat)


def _mod_spec(geom, layer, which, d):
    def imap(i):
        return ((layer * MOD_ROWS + geom.group(i)) * N_MOD + which, 0, 0)
    return pl.BlockSpec((None, 1, d), imap)


def _stream_specs(geom, x, d):
    tm = TOKEN_TILE
    if isinstance(x, tuple):
        return ([pl.BlockSpec((tm, d), lambda i, *_: (jnp.minimum(i, geom.n_ptiles - 1), 0)),
                 pl.BlockSpec((tm, d), lambda i, *_: (jnp.maximum(i - geom.n_ptiles, 0), 0))], list(x))
    return [pl.BlockSpec((tm, d), lambda i, *_: (i, 0))], [x]


def _stream_tile(x_refs, n_ptiles):
    if len(x_refs) == 1:
        return x_refs[0][...]
    return jnp.where(pl.program_id(0) < n_ptiles, x_refs[0][...], x_refs[1][...])


def _store_kv(k, v, kb_ref, vb_ref, ks_ref, vs_ref, n_ptiles):
    kb_ref[...] = k.astype(BF16)
    vb_ref[...] = v.astype(BF16)

    @pl.when(pl.program_id(0) < n_ptiles)
    def _():
        ks_ref[...] = k
        vs_ref[...] = v


def _proj_even_kernel(*refs, n_x, n_ptiles):
    x_refs = refs[:n_x]
    (g_ref, sh_ref, sc_ref, w_ref, cos_ref, sin_ref, dft_ref,
     tc_ref, ts_ref, q_ref, kb_ref, vb_ref, ks_ref, vs_ref) = refs[n_x:]
    h = _rms(_stream_tile(x_refs, n_ptiles), g_ref[...]) * (1.0 + sc_ref[...]) + sh_ref[...]
    p = _dot(h.astype(BF16), w_ref[...])
    cos = cos_ref[...]
    sin = sin_ref[...]
    dft = dft_ref[...]
    tcs, tss = [], []
    for g in range(A_GROUPS):
        t = _dot(p[:, g * A_GROUP_DIM:(g + 1) * A_GROUP_DIM].astype(BF16), dft)
        tcs.append(t[:, :A_GROUP_DIM])
        tss.append(t[:, A_GROUP_DIM:])
    tc_ref[...] = jnp.concatenate(tcs, axis=1).astype(BF16)
    ts_ref[...] = jnp.concatenate(tss, axis=1).astype(BF16)
    o = A_WIDTH
    q = _rope(p[:, o:o + B_Q_WIDTH], cos, sin, B_HEAD_DIM // 4)
    q_ref[...] = (q * B_HEAD_DIM ** -0.5).astype(BF16)
    o += B_Q_WIDTH
    k = _rope(p[:, o:o + B_KV_WIDTH], cos, sin, B_HEAD_DIM // 4)
    o += B_KV_WIDTH
    _store_kv(k, p[:, o:o + B_KV_WIDTH], kb_ref, vb_ref, ks_ref, vs_ref, n_ptiles)


def _proj_odd_kernel(*refs, n_x, n_ptiles):
    x_refs = refs[:n_x]
    (g_ref, sh_ref, sc_ref, w_ref, cos_ref, sin_ref, qn_ref, kn_ref,
     q_ref, kb_ref, vb_ref, ks_ref, vs_ref) = refs[n_x:]
    h = _rms(_stream_tile(x_refs, n_ptiles), g_ref[...]) * (1.0 + sc_ref[...]) + sh_ref[...]
    p = _dot(h.astype(BF16), w_ref[...])
    cos = cos_ref[...]
    sin = sin_ref[...]
    q = _head_rms(p[:, :C_Q_WIDTH], qn_ref[...])
    k = _head_rms(p[:, C_Q_WIDTH:C_Q_WIDTH + C_KV_WIDTH], kn_ref[...])
    q_ref[...] = (_rope(q, cos, sin, C_HEAD_DIM // 4) * C_HEAD_DIM ** -0.5).astype(BF16)
    k = _rope(k, cos, sin, C_HEAD_DIM // 4)
    _store_kv(k, p[:, C_Q_WIDTH + C_KV_WIDTH:], kb_ref, vb_ref, ks_ref, vs_ref, n_ptiles)


def _in_projection(geom, layer, x, norm_g, mods, w, cos, sin, extras, kernel, out_widths, kv_width, name):
    t, d = geom.t, w.shape[0]
    tm = TOKEN_TILE
    n_out = w.shape[1]
    row = lambda i: (i, 0)
    const2 = lambda i: (0, 0)
    x_specs, x_args = _stream_specs(geom, x, d)
    in_specs = x_specs + [
        pl.BlockSpec((1, d), const2),
        _mod_spec(geom, layer, 0, d),
        _mod_spec(geom, layer, 1, d),
        pl.BlockSpec((d, n_out), const2),
        pl.BlockSpec((tm, LANES), lambda i: (geom.pos_block(i), 0)),
        pl.BlockSpec((tm, LANES), lambda i: (geom.pos_block(i), 0)),
    ] + [pl.BlockSpec(e.shape, const2) for e in extras]
    return pl.pallas_call(
        functools.partial(kernel, n_x=len(x_args), n_ptiles=geom.n_ptiles),
        out_shape=([jax.ShapeDtypeStruct((t, wd), BF16) for wd in out_widths]
                   + [jax.ShapeDtypeStruct((geom.tp, kv_width), F32)] * 2),
        grid=(geom.n_tiles,),
        in_specs=in_specs,
        out_specs=([pl.BlockSpec((tm, wd), row) for wd in out_widths]
                   + [pl.BlockSpec((tm, kv_width), lambda i: (jnp.minimum(i, geom.n_ptiles - 1), 0))] * 2),
        compiler_params=_cparams("arbitrary"),
        name=name,
    )(*x_args, norm_g, mods, mods, w, cos, sin, *extras)


def _fourier_kernel(cl_ref, sl_ref, tc_ref, ts_ref, *rest):
    o_ref = rest[-1]
    o_ref[...] = (_dot(cl_ref[...], tc_ref[...]) - _dot(sl_ref[...], ts_ref[...])).astype(o_ref.dtype)


def _fourier_tokens(tc, ts, cl, sl, n_seq, seq_len, row0, prev):
    t, width = tc.shape
    tr = min(seq_len, 512)
    n_r = seq_len // tr
    assert row0 % seq_len == 0
    seq0 = row0 // seq_len
    out0 = row0 // tr
    in_specs = [
        pl.BlockSpec((tr, seq_len), lambda s, r: (r, 0)),
        pl.BlockSpec((tr, seq_len), lambda s, r: (r, 0)),
        pl.BlockSpec((seq_len, width), lambda s, r: (seq0 + s, 0)),
        pl.BlockSpec((seq_len, width), lambda s, r: (seq0 + s, 0)),
    ]
    args = [cl, sl, tc, ts]
    aliases = {}
    if prev is not None:
        in_specs.append(pl.BlockSpec(memory_space=pl.ANY))
        args.append(prev)
        aliases = {4: 0}
    return pl.pallas_call(
        _fourier_kernel,
        out_shape=jax.ShapeDtypeStruct((t, width), BF16),
        grid=(n_seq, n_r),
        in_specs=in_specs,
        out_specs=pl.BlockSpec((tr, width), lambda s, r: (out0 + s * n_r + r, 0)),
        input_output_aliases=aliases,
        compiler_params=_cparams("arbitrary", "arbitrary"),
        name="fourier_tokens",
    )(*args)


def _attend(q, chunks, sink, o0, dh, den_col):
    m = sink
    acc = None
    den = None
    for k, v, mask in chunks:
        s = _dot_nt(q, k)
        if mask is not None:
            s = jnp.where(mask, s, MASKED)
        m_new = jnp.max(s, axis=-1, keepdims=True)
        if m is not None:
            m_new = jnp.maximum(m, m_new)
        p = jnp.exp(s - m_new)
        pv = _dot(p.astype(BF16), v)
        if acc is None:
            acc = pv
            if den_col is None:
                den = jnp.sum(p, axis=-1, keepdims=True)
        else:
            alpha = jnp.exp(m - m_new)
            acc = alpha * acc + pv
            if den_col is None:
                den = alpha * den + jnp.sum(p, axis=-1, keepdims=True)
        m = m_new
    if den_col is not None:
        den = acc[:, den_col:den_col + 1]
    if sink is not None:
        den = den + jnp.exp(sink - m)
    return acc[:, o0:o0 + dh] / den


def _attend_two_pass(q, chunks, sink):
    scores = []
    m = sink
    for k, _, mask in chunks:
        s = _dot_nt(q, k)
        if mask is not None:
            s = jnp.where(mask, s, MASKED)
        scores.append(s)
        mx = jnp.max(s, axis=-1, keepdims=True)
        m = mx if m is None else jnp.maximum(m, mx)
    den = None if sink is None else jnp.exp(sink - m)
    acc = None
    for (_, v, _), s in zip(chunks, scores):
        e = jnp.exp(s - m)
        es = jnp.sum(e, axis=-1, keepdims=True)
        den = es if den is None else den + es
        o = _dot(e.astype(BF16), v)
        acc = o if acc is None else acc + o
    return acc / den


def _stack_heads(q, kv, groups, dh):
    return jnp.concatenate([q[:, (kv * groups + g) * dh:(kv * groups + g + 1) * dh] for g in range(groups)], axis=0)


def _sink_column(sink_ref, kv, groups, rows):
    return jnp.concatenate([jnp.full((rows, 1), sink_ref[kv * groups + g], F32) for g in range(groups)], axis=0)


def _head_values(v, kv, dh, with_ones):
    if not with_ones:
        return v[:, kv * dh:(kv + 1) * dh], 0, None
    assert dh == LANES
    lane = lax.broadcasted_iota(I32, (v.shape[0], LANES), 1)
    ones = jnp.where(lane == 0, 1.0, 0.0).astype(BF16)
    return jnp.concatenate([v[:, kv * dh:(kv + 1) * dh], ones], axis=1), 0, dh


def _paired_heads_attention(q, sources, sink_ref, groups, dh):
    rows = q.shape[0]
    half_q = lax.broadcasted_iota(I32, (rows, LANES), 1) // dh
    per_head = []
    for kv in range(2):
        qh = jnp.concatenate(
            [jnp.where(half_q == kv, q[:, g * LANES:(g + 1) * LANES], jnp.zeros((), q.dtype)) for g in range(groups)],
            axis=0)
        chunks = []
        for k, v, msk in sources:
            half_v = lax.broadcasted_iota(I32, v.shape, 1) // dh
            chunks.append((k, jnp.where(half_v == kv, v, jnp.zeros((), v.dtype)), msk))
        sink = _sink_column(sink_ref, kv, groups, rows) if sink_ref is not None else None
        per_head.append(_attend_two_pass(qh, chunks, sink))
    return jnp.concatenate(
        [per_head[0][g * rows:(g + 1) * rows] + per_head[1][g * rows:(g + 1) * rows] for g in range(groups)], axis=1)


def _attn_kernel(*refs, kv_heads, groups, dh, has_sink, has_ctx, window, q_tile, seq_len, chunk):
    refs = list(refs)
    sink_ref = refs.pop(0) if has_sink else None
    q_ref, k_ref, v_ref = refs[:3]
    ck_ref, cv_ref = (refs[3], refs[4]) if has_ctx else (None, None)
    o_ref = refs[-1]
    q = q_ref[...]
    rows = q.shape[0]
    if window is None:
        spans = [(c * chunk, chunk) for c in range(seq_len // chunk)]
        mask = None
    else:
        n = pl.program_id(1)
        band = q_tile + 2 * window
        start = pl.multiple_of(jnp.clip(n * q_tile - window, 0, seq_len - band), LANES)
        spans = [(start, band)]
        qpos = n * q_tile + lax.broadcasted_iota(I32, (groups * rows, band), 0) % rows
        kpos = start + lax.broadcasted_iota(I32, (groups * rows, band), 1)
        mask = jnp.abs(kpos - qpos) <= window
    sources = [(k_ref[pl.ds(s0, n_s), :], v_ref[pl.ds(s0, n_s), :], mask) for s0, n_s in spans]
    if has_ctx:
        sources.append((ck_ref[...], cv_ref[...], None))
    if 2 * dh == LANES:
        o_ref[...] = _paired_heads_attention(q, sources, sink_ref, groups, dh).astype(o_ref.dtype)
        return
    outs = []
    for kv in range(kv_heads):
        online = dh == LANES and len(sources) > 1
        chunks = []
        for k, v, msk in sources:
            vh, o0, den_col = _head_values(v, kv, dh, with_ones=online)
            chunks.append((k[:, kv * dh:(kv + 1) * dh], vh, msk))
        sink = _sink_column(sink_ref, kv, groups, rows) if has_sink else None
        qh = _stack_heads(q, kv, groups, dh)
        o = _attend(qh, chunks, sink, o0, dh, den_col) if online else _attend_two_pass(qh, chunks, sink)
        outs.extend(o[g * rows:(g + 1) * rows] for g in range(groups))
    o_ref[...] = jnp.concatenate(outs, axis=1).astype(o_ref.dtype)


def _attention(q, k, v, ctx, sink, prev, *, n_seq, seq_len, row0, q_tile, kv_heads, groups, dh, window):
    t, qw = q.shape
    kw = k.shape[1]
    n_q = seq_len // q_tile
    assert row0 % seq_len == 0 and row0 % q_tile == 0
    seq0 = row0 // seq_len
    q0 = row0 // q_tile
    in_specs, args = [], []
    if sink is not None:
        in_specs.append(pl.BlockSpec(memory_space=pltpu.SMEM))
        args.append(sink)
    in_specs += [
        pl.BlockSpec((q_tile, qw), lambda s, n: (q0 + s * n_q + n, 0)),
        pl.BlockSpec((seq_len, kw), lambda s, n: (seq0 + s, 0)),
        pl.BlockSpec((seq_len, kw), lambda s, n: (seq0 + s, 0)),
    ]
    args += [q, k, v]
    if ctx is not None:
        p = ctx[0].shape[1]
        in_specs += [pl.BlockSpec((None, p, kw), lambda s, n: (s, 0, 0))] * 2
        args += list(ctx)
    aliases = {}
    if prev is not None:
        in_specs.append(pl.BlockSpec(memory_space=pl.ANY))
        aliases = {len(args): 0}
        args.append(prev)
    kern = functools.partial(
        _attn_kernel, kv_heads=kv_heads, groups=groups, dh=dh, has_sink=sink is not None,
        has_ctx=ctx is not None, window=window, q_tile=q_tile, seq_len=seq_len, chunk=min(seq_len, ATTN_KEY_CHUNK))
    return pl.pallas_call(
        kern,
        out_shape=jax.ShapeDtypeStruct((t, qw), BF16),
        grid=(n_seq, n_q),
        in_specs=in_specs,
        out_specs=pl.BlockSpec((q_tile, qw), lambda s, n: (q0 + s * n_q + n, 0)),
        input_output_aliases=aliases,
        compiler_params=_cparams("arbitrary", "arbitrary"),
        name="attention",
    )(*args)


def _out_route_kernel(*refs, n_mix, n_x, n_ptiles):
    mix_refs = refs[:n_mix]
    w_refs = refs[n_mix:2 * n_mix]
    x_refs = refs[2 * n_mix:2 * n_mix + n_x]
    (gate_ref, g2_ref, sh2_ref, sc2_ref, rw_ref, rb_ref,
     xo_ref, h_ref, pos_ref, post_ref, gates_ref, rows_ref, off_ref) = refs[2 * n_mix + n_x:]

    acc = None
    for m_ref, w_ref in zip(mix_refs, w_refs):
        part = _dot(m_ref[...], w_ref[...])
        acc = part if acc is None else acc + part
    xn = _stream_tile(x_refs, n_ptiles) + gate_ref[...] * acc
    xo_ref[...] = xn
    h = _rms(xn, g2_ref[...]) * (1.0 + sc2_ref[...]) + sh2_ref[...]
    hb = h.astype(BF16)
    h_ref[...] = hb

    logits = _dot_nt(rw_ref[...], hb) + rb_ref[...]
    ne, tm = logits.shape
    expert = lax.broadcasted_iota(I32, (ne, tm), 0).astype(F32)
    work = logits
    sels, vals = [], []
    for _ in range(TOP_K):
        mx = jnp.max(work, axis=0, keepdims=True)
        first = jnp.min(jnp.where(work == mx, expert, float(ne)), axis=0, keepdims=True)
        sel = expert == first
        work = jnp.where(sel, -jnp.inf, work)
        sels.append(sel)
        vals.append(mx)
    exps = [jnp.exp(v - vals[0]) for v in vals]
    den = exps[0] + exps[1] + exps[2] + exps[3]

    onehot = jnp.zeros((ne, tm), F32)
    for sel in sels:
        onehot = onehot + sel.astype(F32)
    r_i = lax.broadcasted_iota(I32, (tm, tm), 0)
    c_i = lax.broadcasted_iota(I32, (tm, tm), 1)
    before = jnp.where(r_i < c_i, 1.0, 0.0).astype(BF16)
    earlier = _dot(onehot.astype(BF16), before)

    cnt = jnp.sum(onehot, axis=1, keepdims=True)
    seg8 = jnp.floor((cnt + (SEG_ALIGN - 1.0)) * (1.0 / SEG_ALIGN))
    e_r = lax.broadcasted_iota(I32, (ne, ne), 0)
    e_c = lax.broadcasted_iota(I32, (ne, ne), 1)
    lower = jnp.where(e_c < e_r, 1.0, 0.0).astype(BF16)
    off8 = _dot(lower, jnp.broadcast_to(seg8, (ne, LANES)).astype(BF16))[:, 0:1]
    seg_off = off8 * SEG_ALIGN
    base = seg_off + earlier

    sub = lax.broadcasted_iota(I32, (2 * TOP_K, tm), 0)
    token_rows = jnp.zeros((2 * TOP_K, tm), F32)
    for k in range(TOP_K):
        pos_k = jnp.sum(jnp.where(sels[k], base, 0.0), axis=0, keepdims=True)
        token_rows = jnp.where(sub == k, pos_k, token_rows)
        token_rows = jnp.where(sub == TOP_K + k, exps[k] / den, token_rows)
    first_lane = lax.broadcasted_iota(I32, (ne, tm), 1) == 0
    block = jnp.concatenate([
        token_rows,
        jnp.where(first_lane, seg8 * SEG_ALIGN, 0.0),
        jnp.where(first_lane, seg_off, 0.0),
        jnp.zeros((LANES - 2 * TOP_K - 2 * ne, tm), F32)], axis=0)
    by_token = block.T
    post_ref[0] = token_rows.astype(I32)
    pos_ref[...] = by_token[:, 0:TOP_K].astype(I32)
    gates_ref[...] = by_token[:, TOP_K:2 * TOP_K]
    rows_ref[0] = by_token[0:1, 2 * TOP_K:2 * TOP_K + ne].astype(I32)
    off_ref[0] = by_token[0:1, 2 * TOP_K + ne:2 * TOP_K + 2 * ne].astype(I32)


def _out_route(geom, layer, mixes, w_parts, x, mods, norm_g, router_w, router_b):
    t, d = geom.t, w_parts[0].shape[1]
    tm = TOKEN_TILE
    row = lambda i: (i, 0)
    const2 = lambda i: (0, 0)
    x_specs, x_args = _stream_specs(geom, x, d)
    in_specs = [pl.BlockSpec((tm, m.shape[1]), row) for m in mixes]
    in_specs += [pl.BlockSpec(w.shape, const2) for w in w_parts]
    in_specs += x_specs
    in_specs += [
        _mod_spec(geom, layer, 2, d),
        pl.BlockSpec((1, d), const2),
        _mod_spec(geom, layer, 3, d),
        _mod_spec(geom, layer, 4, d),
        pl.BlockSpec(router_w.shape, const2),
        pl.BlockSpec((N_EXPERTS, 1), const2),
    ]
    seg3 = lambda i: (i, 0, 0)
    out_shape = [
        jax.ShapeDtypeStruct((t, d), F32),
        jax.ShapeDtypeStruct((t, d), BF16),
        jax.ShapeDtypeStruct((t, TOP_K), I32),
        jax.ShapeDtypeStruct((geom.n_tiles, 2 * TOP_K, tm), I32),
        jax.ShapeDtypeStruct((t, TOP_K), F32),
        jax.ShapeDtypeStruct((geom.n_tiles, 1, N_EXPERTS), I32),
        jax.ShapeDtypeStruct((geom.n_tiles, 1, N_EXPERTS), I32),
    ]
    out_specs = [
        pl.BlockSpec((tm, d), row),
        pl.BlockSpec((tm, d), row),
        pl.BlockSpec((tm, TOP_K), row),
        pl.BlockSpec((1, 2 * TOP_K, tm), seg3),
        pl.BlockSpec((tm, TOP_K), row),
        pl.BlockSpec((1, 1, N_EXPERTS), seg3),
        pl.BlockSpec((1, 1, N_EXPERTS), seg3),
    ]
    return pl.pallas_call(
        functools.partial(_out_route_kernel, n_mix=len(mixes), n_x=len(x_args), n_ptiles=geom.n_ptiles),
        out_shape=out_shape,
        grid=(geom.n_tiles,),
        in_specs=in_specs,
        out_specs=out_specs,
        compiler_params=_cparams("arbitrary"),
        name="out_route",
    )(*mixes, *w_parts, *x_args, mods, norm_g, mods, mods, router_w, router_b)


def _pack_pairs(v):
    n = v.shape[1] // 2
    bits = lax.bitcast_convert_type(v, U32)
    return (bits[:, :n] & jnp.uint32(0xFFFF0000)) | (bits[:, n:] >> 16)


def _unpack_pairs(p):
    hi = lax.bitcast_convert_type(p & jnp.uint32(0xFFFF0000), F32)
    lo = lax.bitcast_convert_type(p << 16, F32)
    return jnp.concatenate([hi, lo], axis=1).astype(BF16)


def _planned_copies(i, plan_refs, make_copy):
    for size, width, (local_ref, slot_ref, count_ref) in zip(COPY_SIZES, PLAN_WIDTHS, plan_refs):
        def one(c, carry, size=size, width=width, local_ref=local_ref, slot_ref=slot_ref):
            a = i * width + c
            make_copy(pl.multiple_of(local_ref[a], SEG_ALIGN), pl.multiple_of(slot_ref[a], SEG_ALIGN), size).start()
            return carry

        lax.fori_loop(0, count_ref[i], one, 0)


def _copy_plan(rows, seg_off, seg_dst):
    plan = []
    experts = jnp.arange(N_EXPERTS, dtype=I32)
    for n, (size, width) in enumerate(zip(COPY_SIZES, PLAN_WIDTHS)):
        if n == 0:
            count, done = rows // size, jnp.zeros_like(rows)
        else:
            count, done = (rows % (2 * size)) // size, rows - rows % (2 * size)
        cum = jnp.cumsum(count, axis=1)
        j = jnp.arange(width, dtype=I32)
        owner = jnp.sum((cum[:, None, :] <= j[None, :, None]).astype(I32), axis=2)
        pick = (jnp.minimum(owner, N_EXPERTS - 1)[:, :, None] == experts[None, None, :]).astype(I32)
        take = lambda v: jnp.sum(pick * v[:, None, :], axis=2)
        within = (j[None, :] - take(cum - count)) * size
        plan += [(take(seg_off + done) + within).reshape(-1).astype(I32),
                 (take(seg_dst + done) + within).reshape(-1).astype(I32),
                 cum[:, -1].astype(I32)]
    return plan


def _wait_copies(n_rows, make_copy):
    def wait_big(c, carry):
        make_copy(0, 0, WAIT_CHUNK).wait()
        return carry
    lax.fori_loop(0, n_rows // WAIT_CHUNK, wait_big, 0)
    size = WAIT_CHUNK // 2
    while size >= SEG_ALIGN:
        @pl.when(n_rows % (2 * size) >= size)
        def _(size=size):
            make_copy(0, 0, size).wait()

        size //= 2


def _dispatch_kernel(*refs):
    plan_refs, (tot_ref, h_ref, pos_ref, xs_ref, sorted_ref, sem) = _split_plan(refs)
    i = pl.program_id(0)
    buf = i % 2
    tm = h_ref.shape[0]
    n_sorted = sorted_ref.shape[1]
    pos = pos_ref[0]
    slot = lax.broadcasted_iota(I32, (n_sorted, tm), 0)
    hit = jnp.zeros((n_sorted, tm), F32)
    for k in range(TOP_K):
        hit = jnp.where(pos[k:k + 1, :] == slot, 1.0, hit)
    sorted_ref[buf] = _pack_pairs(_dot(hit.astype(BF16), h_ref[...]))

    def copies_from(b):
        def make_copy(local, slot, rows):
            return pltpu.make_async_copy(sorted_ref.at[b, pl.ds(local, rows)], xs_ref.at[pl.ds(slot, rows)],
                                         sem.at[b])
        return make_copy

    _planned_copies(i, plan_refs, copies_from(buf))

    @pl.when(i > 0)
    def _():
        _wait_copies(tot_ref[jnp.maximum(i - 1, 0)], copies_from(1 - buf))

    @pl.when(i == pl.num_programs(0) - 1)
    def _():
        _wait_copies(tot_ref[i], copies_from(buf))


def _split_plan(refs):
    n = 3 * len(COPY_SIZES)
    return [refs[k:k + 3] for k in range(0, n, 3)], refs[n:]


def _dispatch(h, pos, plan, tile_rows, n_slots):
    t, d = h.shape
    tm = TOKEN_TILE
    return pl.pallas_call(
        _dispatch_kernel,
        out_shape=jax.ShapeDtypeStruct((n_slots, d // 2), U32),
        grid_spec=pltpu.PrefetchScalarGridSpec(
            num_scalar_prefetch=len(plan) + 1,
            grid=(t // tm,),
            in_specs=[
                pl.BlockSpec((tm, d), lambda i, *_: (i, 0)),
                pl.BlockSpec((1, 2 * TOP_K, tm), lambda i, *_: (i, 0, 0)),
            ],
            out_specs=pl.BlockSpec(memory_space=pl.ANY),
            scratch_shapes=[pltpu.VMEM((2, SORTED_ROWS, d // 2), U32), pltpu.SemaphoreType.DMA((2,))],
        ),
        compiler_params=_cparams("arbitrary"),
        name="moe_dispatch",
    )(*plan, tile_rows, h, pos)


def _ffn_kernel(rows_ref, start_ref, wgu_ref, bgu_ref, wd_ref, bd_ref, xs_ref, ys_ref,
                wgu_bf, wd_bf, xbuf, ybuf, sem_in, sem_out):
    e = pl.program_id(0)
    d_ff = wd_ref.shape[1]
    tb = xbuf.shape[1]
    n_rows = rows_ref[e]
    n_tiles = (n_rows + tb - 1) // tb
    base = start_ref[e]

    chunk = 128
    def cast_gu(c, carry):
        r = pl.multiple_of(c * chunk, chunk)
        wgu_bf[pl.ds(r, chunk), :] = wgu_ref[0, pl.ds(r, chunk), :].astype(BF16)
        return carry
    lax.fori_loop(0, wgu_ref.shape[1] // chunk, cast_gu, 0)
    def cast_d(c, carry):
        r = pl.multiple_of(c * chunk, chunk)
        wd_bf[pl.ds(r, chunk), :] = wd_ref[0, pl.ds(r, chunk), :].astype(BF16)
        return carry
    lax.fori_loop(0, d_ff // chunk, cast_d, 0)

    half = tb // 2
    last_small = (n_rows - (n_tiles - 1) * tb) <= half

    def x_copy(s, slot):
        r = pl.multiple_of(base + s * tb, EXPERT_TILE)
        return pltpu.make_async_copy(xs_ref.at[pl.ds(r, tb)], xbuf.at[slot], sem_in.at[slot])

    def y_copy(s, slot, rows=tb):
        r = pl.multiple_of(base + s * tb, EXPERT_TILE)
        return pltpu.make_async_copy(ybuf.at[slot, pl.ds(0, rows)], ys_ref.at[pl.ds(r, rows)], sem_out.at[slot])

    def ffn_rows(s, slot, n):
        rows = s * tb + lax.broadcasted_iota(I32, (n, 1), 0)
        x = _unpack_pairs(jnp.where(rows < n_rows, xbuf[slot, pl.ds(0, n), :], jnp.uint32(0)))
        gu = _dot(x, wgu_bf[...]) + bgu_ref[0]
        gate = jnp.minimum(gu[:, :d_ff], SWIGLU_LIMIT)
        up = jnp.clip(gu[:, d_ff:], -SWIGLU_LIMIT, SWIGLU_LIMIT)
        act = (up + 1.0) * (gate * (1.0 / (1.0 + jnp.exp(-SWIGLU_ALPHA * gate))))
        y = _dot(act.astype(BF16), wd_bf[...]) + bd_ref[0]
        ybuf[slot, pl.ds(0, n), :] = _pack_pairs(y.astype(BF16).astype(F32))
        y_copy(s, slot, n).start()

    @pl.when(n_tiles > 0)
    def _():
        x_copy(0, 0).start()

    def tile(s, carry):
        slot = s % 2
        x_copy(s, slot).wait()

        @pl.when(s + 1 < n_tiles)
        def _():
            x_copy(s + 1, 1 - slot).start()

        @pl.when(s >= 2)
        def _():
            y_copy(s - 2, slot).wait()

        small = jnp.logical_and(s == n_tiles - 1, last_small)

        @pl.when(jnp.logical_not(small))
        def _():
            ffn_rows(s, slot, tb)

        @pl.when(small)
        def _():
            ffn_rows(s, slot, half)

        return carry

    lax.fori_loop(0, n_tiles, tile, 0)

    @pl.when(n_tiles >= 2)
    def _():
        y_copy(n_tiles - 2, n_tiles % 2).wait()

    @pl.when(jnp.logical_and(n_tiles >= 1, jnp.logical_not(last_small)))
    def _():
        y_copy(n_tiles - 1, (n_tiles - 1) % 2).wait()

    @pl.when(jnp.logical_and(n_tiles >= 1, last_small))
    def _():
        y_copy(n_tiles - 1, (n_tiles - 1) % 2, half).wait()


def _expert_ffn(layer, xs, expert_rows, expert_start, w_gu, b_gu, w_down, b_down):
    n_slots, packed_w = xs.shape
    tb = FFN_TILE
    depth, ne, d, two_f = w_gu.shape
    d_ff = two_f // 2
    exp4 = lambda e, *_: (layer, e, 0, 0)
    return pl.pallas_call(
        _ffn_kernel,
        out_shape=jax.ShapeDtypeStruct((n_slots, packed_w), U32),
        grid_spec=pltpu.PrefetchScalarGridSpec(
            num_scalar_prefetch=2,
            grid=(ne,),
            in_specs=[
                pl.BlockSpec((None, 1, d, two_f), exp4),
                pl.BlockSpec((None, 1, 1, two_f), exp4),
                pl.BlockSpec((None, 1, d_ff, d), exp4),
                pl.BlockSpec((None, 1, 1, d), exp4),
                pl.BlockSpec(memory_space=pl.ANY),
            ],
            out_specs=pl.BlockSpec(memory_space=pl.ANY),
            scratch_shapes=[
                pltpu.VMEM((d, two_f), BF16), pltpu.VMEM((d_ff, d), BF16),
                pltpu.VMEM((2, tb, packed_w), U32), pltpu.VMEM((2, tb, packed_w), U32),
                pltpu.SemaphoreType.DMA((2,)), pltpu.SemaphoreType.DMA((2,)),
            ],
        ),
        compiler_params=_cparams("arbitrary"),
        name="expert_ffn",
    )(expert_rows, expert_start, w_gu, b_gu.reshape(depth, ne, 1, two_f),
      w_down, b_down.reshape(depth, ne, 1, d), xs)


def _combine_kernel(*refs, final, n_ptiles):
    plan_refs, (tot_ref, x_ref, pos_ref, gates_ref, mg_ref, *rest) = _split_plan(refs)
    if final:
        fn_ref, ys_ref, op_ref, os_ref, buf, sem = rest
    else:
        ys_ref, o_ref, buf, sem = rest
    i = pl.program_id(0)
    cur = i % 2
    tm = x_ref.shape[0]
    n_sorted = buf.shape[1]

    def copies_into(b):
        def make_copy(local, slot, rows):
            return pltpu.make_async_copy(ys_ref.at[pl.ds(slot, rows)], buf.at[b, pl.ds(local, rows)], sem.at[b])
        return make_copy

    @pl.when(i == 0)
    def _():
        buf[...] = jnp.zeros_like(buf)
        _planned_copies(i, plan_refs, copies_into(cur))

    @pl.when(i + 1 < pl.num_programs(0))
    def _():
        _planned_copies(i + 1, plan_refs, copies_into(1 - cur))

    pos = pos_ref[...]
    g = gates_ref[...]
    lane = lax.broadcasted_iota(I32, (tm, n_sorted), 1)
    weight = jnp.zeros((tm, n_sorted), F32)
    for k in range(TOP_K):
        weight = jnp.where(pos[:, k:k + 1] == lane, g[:, k:k + 1], weight)
    _wait_copies(tot_ref[i], copies_into(cur))
    y = _dot(weight.astype(BF16), _unpack_pairs(buf[cur]))
    xn = x_ref[...] + mg_ref[...] * y
    if final:
        xn = _rms(xn, fn_ref[...])

        @pl.when(i < n_ptiles)
        def _():
            op_ref[...] = xn

        @pl.when(i >= n_ptiles)
        def _():
            os_ref[...] = xn
    else:
        o_ref[...] = xn


def _combine(geom, layer, ys, pos, plan, tile_rows, x, gates, mods, final_g):
    t, d = x.shape
    tm = TOKEN_TILE
    final = final_g is not None

    def mod_imap(i, *_):
        return ((layer * MOD_ROWS + geom.group(i)) * N_MOD + 5, 0, 0)

    row = lambda i, *_: (i, 0)
    in_specs = [
        pl.BlockSpec((tm, d), row),
        pl.BlockSpec((tm, TOP_K), row),
        pl.BlockSpec((tm, TOP_K), row),
        pl.BlockSpec((None, 1, d), mod_imap),
    ]
    args = [*plan, tile_rows, x, pos, gates, mods]
    if final:
        in_specs.append(pl.BlockSpec((1, d), lambda i, *_: (0, 0)))
        args.append(final_g)
    in_specs.append(pl.BlockSpec(memory_space=pl.ANY))
    args.append(ys)
    if final:
        n_pt = geom.n_ptiles
        out_shape = [jax.ShapeDtypeStruct((geom.tp, d), F32), jax.ShapeDtypeStruct((t - geom.tp, d), F32)]
        out_specs = [pl.BlockSpec((tm, d), lambda i, *_: (jnp.minimum(i, n_pt - 1), 0)),
                     pl.BlockSpec((tm, d), lambda i, *_: (jnp.maximum(i - n_pt, 0), 0))]
    else:
        out_shape = jax.ShapeDtypeStruct((t, d), F32)
        out_specs = pl.BlockSpec((tm, d), row)
    return pl.pallas_call(
        functools.partial(_combine_kernel, final=final, n_ptiles=geom.n_ptiles),
        out_shape=out_shape,
        grid_spec=pltpu.PrefetchScalarGridSpec(
            num_scalar_prefetch=len(plan) + 1,
            grid=(t // tm,),
            in_specs=in_specs,
            out_specs=out_specs,
            scratch_shapes=[pltpu.VMEM((2, SORTED_ROWS, d // 2), U32), pltpu.SemaphoreType.DMA((2,))],
        ),
        compiler_params=_cparams("arbitrary"),
        name="moe_combine",
    )(*args)


def _moe(geom, layer, h, pos, pos_t, gates, seg_rows, seg_off, x, mods, w_gu, b_gu, w_down, b_down, final_g):
    t = h.shape[0]
    tb = EXPERT_TILE
    n_tok_tiles = seg_rows.shape[0]
    max_rows = t * TOP_K + n_tok_tiles * N_EXPERTS * (SEG_ALIGN - 1)
    n_blocks = -(-max_rows // tb) + N_EXPERTS
    rows = seg_rows[:, 0, :]
    cnt = jnp.sum(rows, axis=0)
    n_tiles_e = (cnt + tb - 1) // tb
    tile_end = jnp.cumsum(n_tiles_e)
    tile_start = tile_end - n_tiles_e
    expert_start = (tile_start * tb).astype(I32)
    seg_dst = expert_start[None, :] + jnp.cumsum(rows, axis=0) - rows
    plan = _copy_plan(rows, seg_off[:, 0, :], seg_dst)
    tile_rows = jnp.sum(rows, axis=1).astype(I32)
    xs = _dispatch(h, pos_t, plan, tile_rows, n_blocks * tb + FFN_TILE - tb)
    ys = _expert_ffn(layer, xs, cnt.astype(I32), expert_start, w_gu, b_gu, w_down, b_down)
    return _combine(geom, layer, ys, pos, plan, tile_rows, x, gates, mods, final_g)


def kernel(x_prompt, x_sample, cache_b_k, cache_b_v, cache_c_k, cache_c_v, c, c_ctx,
           mod_w, mod_b, norm_mix, norm_ffn, even_w_in, even_w_out, even_sink,
           odd_w_in, odd_w_out, odd_q_norm, odd_k_norm, router_w, router_b,
           moe_w_gu, moe_b_gu, moe_w_down, moe_b_down, final_norm):
    bp, lp, d = x_prompt.shape
    bs, ls, _ = x_sample.shape
    past = cache_b_k.shape[2]
    depth = mod_w.shape[0]
    geom = _Geom(bp, lp, bs, ls)
    tp = geom.tp

    x = (x_prompt.reshape(tp, d), x_sample.reshape(bs * ls, d))
    cond = jnp.concatenate([c_ctx[None, :], c, jnp.zeros((MOD_ROWS - 1 - bs, d), F32)], axis=0)
    mods = _modulation(cond, mod_w, mod_b).reshape(depth * MOD_ROWS * N_MOD, 1, d)

    cn, sn = _dft_tables(A_GROUP_DIM)
    dft_chan = jnp.asarray(np.concatenate([cn, sn], axis=1), BF16)
    dft_p = [jnp.asarray(m, BF16) for m in _dft_tables(lp)]
    dft_s = [jnp.asarray(m, BF16) for m in _dft_tables(ls)]
    rope_b = [jnp.asarray(m) for m in _rope_tables(TOKEN_TILE, ls, B_HEAD_DIM)]
    rope_c = [jnp.asarray(m) for m in _rope_tables(TOKEN_TILE, ls, C_HEAD_DIM)]

    states = {"bk": [], "bv": [], "ck": [], "cv": []}
    for layer in range(depth):
        j = layer // 2
        g_mix = norm_mix[layer][None, :]
        g_ffn = norm_ffn[layer][None, :]
        if layer % 2 == 0:
            pair = _paired_head_order(B_KV_HEADS, B_HEADS // B_KV_HEADS, B_HEAD_DIM)
            w_in = even_w_in[j]
            w_in = jnp.concatenate([w_in[:, :A_WIDTH], w_in[:, A_WIDTH:A_WIDTH + B_Q_WIDTH][:, pair],
                                    w_in[:, A_WIDTH + B_Q_WIDTH:]], axis=1).astype(BF16)
            tc, ts, q, k, v, k_state, v_state = _in_projection(
                geom, layer, x, g_mix, mods, w_in, rope_b[0], rope_b[1], [dft_chan],
                _proj_even_kernel, (A_WIDTH, A_WIDTH, B_Q_WIDTH, B_KV_WIDTH, B_KV_WIDTH), B_KV_WIDTH, "proj_even")
            states["bk"].append(k_state.reshape(bp, lp, B_KV_HEADS, B_HEAD_DIM))
            states["bv"].append(v_state.reshape(bp, lp, B_KV_HEADS, B_HEAD_DIM))
            four = _fourier_tokens(tc, ts, dft_p[0], dft_p[1], bp, lp, 0, None)
            four = _fourier_tokens(tc, ts, dft_s[0], dft_s[1], bs, ls, tp, four)
            sink = even_sink[j]
            common = dict(kv_heads=B_KV_HEADS, groups=B_HEADS // B_KV_HEADS, dh=B_HEAD_DIM)
            att = _attention(q, k, v, None, sink, None, n_seq=bp, seq_len=lp, row0=0, q_tile=lp,
                             window=None, **common)
            ctx = (cache_b_k[:, j].reshape(bs, past, B_KV_WIDTH).astype(BF16),
                   cache_b_v[:, j].reshape(bs, past, B_KV_WIDTH).astype(BF16))
            att = _attention(q, k, v, ctx, sink, att, n_seq=bs, seq_len=ls, row0=tp, q_tile=ATTN_Q_TILE,
                             window=WINDOW, **common)
            w_out = even_w_out[j].astype(BF16)
            mixes = [four, att]
            w_parts = [w_out[:A_WIDTH], w_out[A_WIDTH:][pair]]
        else:
            q, k, v, k_state, v_state = _in_projection(
                geom, layer, x, g_mix, mods, odd_w_in[j].astype(BF16), rope_c[0], rope_c[1],
                [odd_q_norm[j][None, :], odd_k_norm[j][None, :]],
                _proj_odd_kernel, (C_Q_WIDTH, C_KV_WIDTH, C_KV_WIDTH), C_KV_WIDTH, "proj_odd")
            states["ck"].append(k_state.reshape(bp, lp, C_KV_HEADS, C_HEAD_DIM))
            states["cv"].append(v_state.reshape(bp, lp, C_KV_HEADS, C_HEAD_DIM))
            common = dict(kv_heads=C_KV_HEADS, groups=C_HEADS // C_KV_HEADS, dh=C_HEAD_DIM, window=None)
            att = _attention(q, k, v, None, None, None, n_seq=bp, seq_len=lp, row0=0, q_tile=lp, **common)
            ctx = (cache_c_k[:, j].reshape(bs, past, C_KV_WIDTH).astype(BF16),
                   cache_c_v[:, j].reshape(bs, past, C_KV_WIDTH).astype(BF16))
            att = _attention(q, k, v, ctx, None, att, n_seq=bs, seq_len=ls, row0=tp, q_tile=DENSE_Q_TILE, **common)
            mixes = [att]
            w_parts = [odd_w_out[j].astype(BF16)]
        x, h, pos, pos_t, gates, seg_rows, seg_off = _out_route(
            geom, layer, mixes, w_parts, x, mods, g_ffn, router_w[layer].T.astype(BF16), router_b[layer][:, None])
        final_g = final_norm[None, :] if layer == depth - 1 else None
        x = _moe(geom, layer, h, pos, pos_t, gates, seg_rows, seg_off, x, mods,
                 moe_w_gu, moe_b_gu, moe_w_down, moe_b_down, final_g)

    y_prompt = x[0].reshape(bp, lp, d)
    y_sample = x[1].reshape(bs, ls, d)
    return (y_prompt, y_sample,
            jnp.stack(states["bk"], axis=1), jnp.stack(states["bv"], axis=1),
            jnp.stack(states["ck"], axis=1), jnp.stack(states["cv"], axis=1))
```

```python
import functools

import numpy as np
import jax
import jax.numpy as jnp
from jax import lax
from jax.experimental import pallas as pl
from jax.experimental.pallas import tpu as pltpu

F32 = jnp.float32
BF16 = jnp.bfloat16
I32 = jnp.int32
U32 = jnp.uint32

GRID_W = 64
A_GROUPS = 4
A_GROUP_DIM = 128
A_WIDTH = A_GROUPS * A_GROUP_DIM
B_HEADS = 8
B_KV_HEADS = 2
B_HEAD_DIM = 64
B_Q_WIDTH = B_HEADS * B_HEAD_DIM
B_KV_WIDTH = B_KV_HEADS * B_HEAD_DIM
WINDOW = 128
C_HEADS = 8
C_KV_HEADS = 2
C_HEAD_DIM = 128
C_Q_WIDTH = C_HEADS * C_HEAD_DIM
C_KV_WIDTH = C_KV_HEADS * C_HEAD_DIM
ROPE_THETA = 10000.0
N_EXPERTS = 32
TOP_K = 4
SWIGLU_LIMIT = 7.0
SWIGLU_ALPHA = 1.702
EPS = 1e-6

LANES = 128
SUBLANES = 8
TOKEN_TILE = 256
ROUTE_TILES_PER_STEP = 4
EXPERT_TILE = 128
FFN_TILE = 256
ATTN_Q_TILE = 128
DENSE_Q_TILE = 256
ATTN_KEY_CHUNK = 1024
SEG_ALIGN = SUBLANES
COPY_SIZES = (32, 16, 8)
WAIT_CHUNK = 256
SORTED_ROWS = -(-(TOKEN_TILE * TOP_K + N_EXPERTS * (SEG_ALIGN - 1)) // LANES) * LANES
PLAN_WIDTHS = (SORTED_ROWS // COPY_SIZES[0],) + (N_EXPERTS,) * (len(COPY_SIZES) - 1)
VMEM_LIMIT = 56 * 1024 * 1024
MASKED = -1e30
N_MOD = 6
MOD_ROWS = SUBLANES


def _cparams(*sem):
    return pltpu.CompilerParams(dimension_semantics=tuple(sem), vmem_limit_bytes=VMEM_LIMIT)


def _dot(a, b):
    return jnp.dot(a, b, preferred_element_type=F32)


def _dot_nt(a, b):
    return lax.dot_general(a, b, (((1,), (1,)), ((), ())), preferred_element_type=F32)


def _rms(x, g):
    return x * lax.rsqrt(jnp.mean(x * x, axis=-1, keepdims=True) + EPS) * g


def _dft_tables(n):
    j = np.arange(n, dtype=np.int64)
    ang = 2.0 * np.pi * ((j[:, None] * j[None, :]) % n).astype(np.float64) / n
    s = 1.0 / np.sqrt(n)
    return np.cos(ang) * s, np.sin(ang) * s


def _paired_head_order(kv_heads, groups, dh):
    assert kv_heads * dh == LANES
    return np.array([(kv * groups + g) * dh + d for g in range(groups) for kv in range(kv_heads) for d in range(dh)])


def _rope_tables(n_prompt_rows, n_latent, head_dim):
    quarter = head_dim // 4
    pos = np.arange(n_latent)
    row = (pos // GRID_W).astype(np.float32)
    col = (pos % GRID_W).astype(np.float32)
    inv = (np.float32(ROPE_THETA) ** (-np.arange(quarter, dtype=np.float32) / np.float32(quarter))).astype(np.float32)
    ang_row = (row[:, None] * inv[None, :]).astype(np.float32)
    ang_col = (col[:, None] * inv[None, :]).astype(np.float32)
    cos_h = np.concatenate([np.cos(ang_row)] * 2 + [np.cos(ang_col)] * 2, axis=1)
    sin_h = np.concatenate([-np.sin(ang_row), np.sin(ang_row), -np.sin(ang_col), np.sin(ang_col)], axis=1)
    reps = LANES // head_dim
    cos_l = np.tile(cos_h, (1, reps)).astype(np.float32)
    sin_l = np.tile(sin_h, (1, reps)).astype(np.float32)
    cos = np.concatenate([np.ones((n_prompt_rows, LANES), np.float32), cos_l], axis=0)
    sin = np.concatenate([np.zeros((n_prompt_rows, LANES), np.float32), sin_l], axis=0)
    return cos, sin


def _rope(x, cos, sin, quarter):
    lane = lax.broadcasted_iota(I32, (x.shape[0], LANES), 1)
    first = ((lane // quarter) % 2) == 0
    outs = []
    for c in range(x.shape[1] // LANES):
        xc = x[:, c * LANES:(c + 1) * LANES]
        partner = jnp.where(first, pltpu.roll(xc, LANES - quarter, 1), pltpu.roll(xc, quarter, 1))
        outs.append(xc * cos + partner * sin)
    return outs[0] if len(outs) == 1 else jnp.concatenate(outs, axis=1)


def _head_rms(x, g):
    outs = []
    for c in range(x.shape[1] // LANES):
        outs.append(_rms(x[:, c * LANES:(c + 1) * LANES], g))
    return outs[0] if len(outs) == 1 else jnp.concatenate(outs, axis=1)


def _mod_kernel(c_ref, w_ref, b_ref, o_ref):
    c = c_ref[...]
    s = c * (1.0 / (1.0 + jnp.exp(-c)))
    o_ref[0] = _dot(s.astype(BF16), w_ref[0].astype(BF16)) + b_ref[0]


def _modulation(cond, mod_w, mod_b):
    depth, d, n = mod_w.shape
    tn = 1536
    return pl.pallas_call(
        _mod_kernel,
        out_shape=jax.ShapeDtypeStruct((depth, MOD_ROWS, n), F32),
        grid=(depth, n // tn),
        in_specs=[
            pl.BlockSpec((MOD_ROWS, d), lambda l, j: (0, 0)),
            pl.BlockSpec((1, d, tn), lambda l, j: (l, 0, j)),
            pl.BlockSpec((1, 1, tn), lambda l, j: (l, 0, j)),
        ],
        out_specs=pl.BlockSpec((1, MOD_ROWS, tn), lambda l, j: (l, 0, j)),
        compiler_params=_cparams("arbitrary", "arbitrary"),
        name="modulation",
    )(cond, mod_w, mod_b.reshape(depth, 1, n))


class _Geom:
    def __init__(self, bp, lp, bs, ls):
        self.bp, self.lp, self.bs, self.ls = bp, lp, bs, ls
        self.tp = bp * lp
        self.t = bp * lp + bs * ls
        assert lp == TOKEN_TILE and ls % TOKEN_TILE == 0 and self.tp % ls == 0
        self.n_ptiles = self.tp // TOKEN_TILE
        self.tiles_per_lat = ls // TOKEN_TILE
        self.n_tiles = self.t // TOKEN_TILE

    def group(self, i):
        return jnp.where(i < self.n_ptiles, 0, 1 + (i - self.n_ptiles) // self.tiles_per_lat)

    def pos_block(self, i):
        return jnp.where(i < self.n_ptiles, 0, 1 + (i - self.n_ptiles) % self.tiles_per_lat)


def _mod_spec(geom, layer, which, d, tiles=1):
    def imap(i):
        return ((layer * MOD_ROWS + geom.group(i * tiles)) * N_MOD + which, 0, 0)
    return pl.BlockSpec((None, 1, d), imap)


def _stream_specs(geom, x, d, tiles=1):
    tm = TOKEN_TILE * tiles
    n_p = geom.n_ptiles // tiles
    if isinstance(x, tuple):
        return ([pl.BlockSpec((tm, d), lambda i, *_: (jnp.minimum(i, n_p - 1), 0)),
                 pl.BlockSpec((tm, d), lambda i, *_: (jnp.maximum(i - n_p, 0), 0))], list(x))
    return [pl.BlockSpec((tm, d), lambda i, *_: (i, 0))], [x]


def _stream_tile(x_refs, n_ptiles):
    if len(x_refs) == 1:
        return x_refs[0][...]
    return jnp.where(pl.program_id(0) < n_ptiles, x_refs[0][...], x_refs[1][...])


def _store_kv(k, v, kb_ref, vb_ref, ks_ref, vs_ref, n_ptiles):
    kb_ref[...] = k.astype(BF16)
    vb_ref[...] = v.astype(BF16)

    @pl.when(pl.program_id(0) < n_ptiles)
    def _():
        ks_ref[...] = k
        vs_ref[...] = v


def _proj_even_kernel(*refs, n_x, n_ptiles):
    x_refs = refs[:n_x]
    (g_ref, sh_ref, sc_ref, w_ref, cos_ref, sin_ref, dft_ref,
     tc_ref, ts_ref, q_ref, kb_ref, vb_ref, ks_ref, vs_ref) = refs[n_x:]
    h = _rms(_stream_tile(x_refs, n_ptiles), g_ref[...]) * (1.0 + sc_ref[...]) + sh_ref[...]
    p = _dot(h.astype(BF16), w_ref[...])
    cos = cos_ref[...]
    sin = sin_ref[...]
    dft = dft_ref[...]
    tcs, tss = [], []
    for g in range(A_GROUPS):
        t = _dot(p[:, g * A_GROUP_DIM:(g + 1) * A_GROUP_DIM].astype(BF16), dft)
        tcs.append(t[:, :A_GROUP_DIM])
        tss.append(t[:, A_GROUP_DIM:])
    tc_ref[...] = jnp.concatenate(tcs, axis=1).astype(BF16)
    ts_ref[...] = jnp.concatenate(tss, axis=1).astype(BF16)
    o = A_WIDTH
    q = _rope(p[:, o:o + B_Q_WIDTH], cos, sin, B_HEAD_DIM // 4)
    q_ref[...] = (q * B_HEAD_DIM ** -0.5).astype(BF16)
    o += B_Q_WIDTH
    k = _rope(p[:, o:o + B_KV_WIDTH], cos, sin, B_HEAD_DIM // 4)
    o += B_KV_WIDTH
    _store_kv(k, p[:, o:o + B_KV_WIDTH], kb_ref, vb_ref, ks_ref, vs_ref, n_ptiles)


def _proj_odd_kernel(*refs, n_x, n_ptiles):
    x_refs = refs[:n_x]
    (g_ref, sh_ref, sc_ref, w_ref, cos_ref, sin_ref, qn_ref, kn_ref,
     q_ref, kb_ref, vb_ref, ks_ref, vs_ref) = refs[n_x:]
    h = _rms(_stream_tile(x_refs, n_ptiles), g_ref[...]) * (1.0 + sc_ref[...]) + sh_ref[...]
    p = _dot(h.astype(BF16), w_ref[...])
    cos = cos_ref[...]
    sin = sin_ref[...]
    q = _head_rms(p[:, :C_Q_WIDTH], qn_ref[...])
    k = _head_rms(p[:, C_Q_WIDTH:C_Q_WIDTH + C_KV_WIDTH], kn_ref[...])
    q_ref[...] = (_rope(q, cos, sin, C_HEAD_DIM // 4) * C_HEAD_DIM ** -0.5).astype(BF16)
    k = _rope(k, cos, sin, C_HEAD_DIM // 4)
    _store_kv(k, p[:, C_Q_WIDTH + C_KV_WIDTH:], kb_ref, vb_ref, ks_ref, vs_ref, n_ptiles)


def _in_projection(geom, layer, x, norm_g, mods, w, cos, sin, extras, kernel, out_widths, kv_width, name):
    t, d = geom.t, w.shape[0]
    tm = TOKEN_TILE
    n_out = w.shape[1]
    row = lambda i: (i, 0)
    const2 = lambda i: (0, 0)
    x_specs, x_args = _stream_specs(geom, x, d)
    in_specs = x_specs + [
        pl.BlockSpec((1, d), const2),
        _mod_spec(geom, layer, 0, d),
        _mod_spec(geom, layer, 1, d),
        pl.BlockSpec((d, n_out), const2),
        pl.BlockSpec((tm, LANES), lambda i: (geom.pos_block(i), 0)),
        pl.BlockSpec((tm, LANES), lambda i: (geom.pos_block(i), 0)),
    ] + [pl.BlockSpec(e.shape, const2) for e in extras]
    return pl.pallas_call(
        functools.partial(kernel, n_x=len(x_args), n_ptiles=geom.n_ptiles),
        out_shape=([jax.ShapeDtypeStruct((t, wd), BF16) for wd in out_widths]
                   + [jax.ShapeDtypeStruct((geom.tp, kv_width), F32)] * 2),
        grid=(geom.n_tiles,),
        in_specs=in_specs,
        out_specs=([pl.BlockSpec((tm, wd), row) for wd in out_widths]
                   + [pl.BlockSpec((tm, kv_width), lambda i: (jnp.minimum(i, geom.n_ptiles - 1), 0))] * 2),
        compiler_params=_cparams("arbitrary"),
        name=name,
    )(*x_args, norm_g, mods, mods, w, cos, sin, *extras)


def _fourier_kernel(cl_ref, sl_ref, tc_ref, ts_ref, *rest):
    o_ref = rest[-1]
    o_ref[...] = (_dot(cl_ref[...], tc_ref[...]) - _dot(sl_ref[...], ts_ref[...])).astype(o_ref.dtype)


def _fourier_tokens(tc, ts, cl, sl, n_seq, seq_len, row0, prev):
    t, width = tc.shape
    tr = min(seq_len, 512)
    n_r = seq_len // tr
    assert row0 % seq_len == 0
    seq0 = row0 // seq_len
    out0 = row0 // tr
    in_specs = [
        pl.BlockSpec((tr, seq_len), lambda s, r: (r, 0)),
        pl.BlockSpec((tr, seq_len), lambda s, r: (r, 0)),
        pl.BlockSpec((seq_len, width), lambda s, r: (seq0 + s, 0)),
        pl.BlockSpec((seq_len, width), lambda s, r: (seq0 + s, 0)),
    ]
    args = [cl, sl, tc, ts]
    aliases = {}
    if prev is not None:
        in_specs.append(pl.BlockSpec(memory_space=pl.ANY))
        args.append(prev)
        aliases = {4: 0}
    return pl.pallas_call(
        _fourier_kernel,
        out_shape=jax.ShapeDtypeStruct((t, width), BF16),
        grid=(n_seq, n_r),
        in_specs=in_specs,
        out_specs=pl.BlockSpec((tr, width), lambda s, r: (out0 + s * n_r + r, 0)),
        input_output_aliases=aliases,
        compiler_params=_cparams("arbitrary", "arbitrary"),
        name="fourier_tokens",
    )(*args)


def _attend(q, chunks, sink, o0, dh, den_col):
    m = sink
    acc = None
    den = None
    for k, v, mask in chunks:
        s = _dot_nt(q, k)
        if mask is not None:
            s = jnp.where(mask, s, MASKED)
        m_new = jnp.max(s, axis=-1, keepdims=True)
        if m is not None:
            m_new = jnp.maximum(m, m_new)
        p = jnp.exp(s - m_new)
        pv = _dot(p.astype(BF16), v)
        if acc is None:
            acc = pv
            if den_col is None:
                den = jnp.sum(p, axis=-1, keepdims=True)
        else:
            alpha = jnp.exp(m - m_new)
            acc = alpha * acc + pv
            if den_col is None:
                den = alpha * den + jnp.sum(p, axis=-1, keepdims=True)
        m = m_new
    if den_col is not None:
        den = acc[:, den_col:den_col + 1]
    if sink is not None:
        den = den + jnp.exp(sink - m)
    return acc[:, o0:o0 + dh] / den


def _attend_two_pass(q, chunks, sink):
    scores = []
    m = sink
    for k, _, mask in chunks:
        s = _dot_nt(q, k)
        if mask is not None:
            s = jnp.where(mask, s, MASKED)
        scores.append(s)
        mx = jnp.max(s, axis=-1, keepdims=True)
        m = mx if m is None else jnp.maximum(m, mx)
    den = None if sink is None else jnp.exp(sink - m)
    acc = None
    for (_, v, _), s in zip(chunks, scores):
        e = jnp.exp(s - m)
        es = jnp.sum(e, axis=-1, keepdims=True)
        den = es if den is None else den + es
        o = _dot(e.astype(BF16), v)
        acc = o if acc is None else acc + o
    return acc / den


def _stack_heads(q, kv, groups, dh):
    return jnp.concatenate([q[:, (kv * groups + g) * dh:(kv * groups + g + 1) * dh] for g in range(groups)], axis=0)


def _sink_column(sink_ref, kv, groups, rows):
    return jnp.concatenate([jnp.full((rows, 1), sink_ref[kv * groups + g], F32) for g in range(groups)], axis=0)


def _head_values(v, kv, dh, with_ones):
    if not with_ones:
        return v[:, kv * dh:(kv + 1) * dh], 0, None
    assert dh == LANES
    lane = lax.broadcasted_iota(I32, (v.shape[0], LANES), 1)
    ones = jnp.where(lane == 0, 1.0, 0.0).astype(BF16)
    return jnp.concatenate([v[:, kv * dh:(kv + 1) * dh], ones], axis=1), 0, dh


def _paired_heads_attention(q, sources, sink_ref, groups, dh):
    rows = q.shape[0]
    half_q = lax.broadcasted_iota(I32, (rows, LANES), 1) // dh
    per_head = []
    for kv in range(2):
        qh = jnp.concatenate(
            [jnp.where(half_q == kv, q[:, g * LANES:(g + 1) * LANES], jnp.zeros((), q.dtype)) for g in range(groups)],
            axis=0)
        chunks = []
        for k, v, msk in sources:
            half_v = lax.broadcasted_iota(I32, v.shape, 1) // dh
            chunks.append((k, jnp.where(half_v == kv, v, jnp.zeros((), v.dtype)), msk))
        sink = _sink_column(sink_ref, kv, groups, rows) if sink_ref is not None else None
        per_head.append(_attend_two_pass(qh, chunks, sink))
    return jnp.concatenate(
        [per_head[0][g * rows:(g + 1) * rows] + per_head[1][g * rows:(g + 1) * rows] for g in range(groups)], axis=1)


def _attn_kernel(*refs, kv_heads, groups, dh, has_sink, has_ctx, window, q_tile, seq_len, chunk):
    refs = list(refs)
    sink_ref = refs.pop(0) if has_sink else None
    q_ref, k_ref, v_ref = refs[:3]
    ck_ref, cv_ref = (refs[3], refs[4]) if has_ctx else (None, None)
    o_ref = refs[-1]
    q = q_ref[...]
    rows = q.shape[0]
    if window is None:
        spans = [(c * chunk, chunk) for c in range(seq_len // chunk)]
        mask = None
    else:
        n = pl.program_id(1)
        band = q_tile + 2 * window
        start = pl.multiple_of(jnp.clip(n * q_tile - window, 0, seq_len - band), LANES)
        spans = [(start, band)]
        qpos = n * q_tile + lax.broadcasted_iota(I32, (groups * rows, band), 0) % rows
        kpos = start + lax.broadcasted_iota(I32, (groups * rows, band), 1)
        mask = jnp.abs(kpos - qpos) <= window
    sources = [(k_ref[pl.ds(s0, n_s), :], v_ref[pl.ds(s0, n_s), :], mask) for s0, n_s in spans]
    if has_ctx:
        sources.append((ck_ref[...], cv_ref[...], None))
    if 2 * dh == LANES:
        o_ref[...] = _paired_heads_attention(q, sources, sink_ref, groups, dh).astype(o_ref.dtype)
        return
    outs = []
    for kv in range(kv_heads):
        online = dh == LANES and len(sources) > 1
        chunks = []
        for k, v, msk in sources:
            vh, o0, den_col = _head_values(v, kv, dh, with_ones=online)
            chunks.append((k[:, kv * dh:(kv + 1) * dh], vh, msk))
        sink = _sink_column(sink_ref, kv, groups, rows) if has_sink else None
        qh = _stack_heads(q, kv, groups, dh)
        o = _attend(qh, chunks, sink, o0, dh, den_col) if online else _attend_two_pass(qh, chunks, sink)
        outs.extend(o[g * rows:(g + 1) * rows] for g in range(groups))
    o_ref[...] = jnp.concatenate(outs, axis=1).astype(o_ref.dtype)


def _attention(q, k, v, ctx, sink, prev, *, n_seq, seq_len, row0, q_tile, kv_heads, groups, dh, window):
    t, qw = q.shape
    kw = k.shape[1]
    n_q = seq_len // q_tile
    assert row0 % seq_len == 0 and row0 % q_tile == 0
    seq0 = row0 // seq_len
    q0 = row0 // q_tile
    in_specs, args = [], []
    if sink is not None:
        in_specs.append(pl.BlockSpec(memory_space=pltpu.SMEM))
        args.append(sink)
    in_specs += [
        pl.BlockSpec((q_tile, qw), lambda s, n: (q0 + s * n_q + n, 0)),
        pl.BlockSpec((seq_len, kw), lambda s, n: (seq0 + s, 0)),
        pl.BlockSpec((seq_len, kw), lambda s, n: (seq0 + s, 0)),
    ]
    args += [q, k, v]
    if ctx is not None:
        p = ctx[0].shape[1]
        in_specs += [pl.BlockSpec((None, p, kw), lambda s, n: (s, 0, 0))] * 2
        args += list(ctx)
    aliases = {}
    if prev is not None:
        in_specs.append(pl.BlockSpec(memory_space=pl.ANY))
        aliases = {len(args): 0}
        args.append(prev)
    kern = functools.partial(
        _attn_kernel, kv_heads=kv_heads, groups=groups, dh=dh, has_sink=sink is not None,
        has_ctx=ctx is not None, window=window, q_tile=q_tile, seq_len=seq_len, chunk=min(seq_len, ATTN_KEY_CHUNK))
    return pl.pallas_call(
        kern,
        out_shape=jax.ShapeDtypeStruct((t, qw), BF16),
        grid=(n_seq, n_q),
        in_specs=in_specs,
        out_specs=pl.BlockSpec((q_tile, qw), lambda s, n: (q0 + s * n_q + n, 0)),
        input_output_aliases=aliases,
        compiler_params=_cparams("arbitrary", "arbitrary"),
        name="attention",
    )(*args)


def _out_route_kernel(*refs, n_mix, n_x, n_ptiles, tiles):
    mix_refs = refs[:n_mix]
    w_refs = refs[n_mix:2 * n_mix]
    x_refs = refs[2 * n_mix:2 * n_mix + n_x]
    x_all = _stream_tile(x_refs, n_ptiles)
    for j in range(tiles):
        rows = slice(j * TOKEN_TILE, (j + 1) * TOKEN_TILE)
        _out_route_tile(j, rows, x_all[rows], [m[rows, :] for m in mix_refs], w_refs, *refs[2 * n_mix + n_x:])


def _out_route_tile(j, rows, x, mixes, w_refs, gate_ref, g2_ref, sh2_ref, sc2_ref, rw_ref, rb_ref,
                    xo_ref, h_ref, pos_ref, post_ref, gates_ref, rows_ref, off_ref):
    acc = None
    for m, w_ref in zip(mixes, w_refs):
        part = _dot(m, w_ref[...])
        acc = part if acc is None else acc + part
    xn = x + gate_ref[...] * acc
    xo_ref[rows, :] = xn
    h = _rms(xn, g2_ref[...]) * (1.0 + sc2_ref[...]) + sh2_ref[...]
    hb = h.astype(BF16)
    h_ref[rows, :] = hb

    logits = _dot_nt(rw_ref[...], hb) + rb_ref[...]
    ne, tm = logits.shape
    expert = lax.broadcasted_iota(I32, (ne, tm), 0).astype(F32)
    work = logits
    sels, vals = [], []
    for _ in range(TOP_K):
        mx = jnp.max(work, axis=0, keepdims=True)
        first = jnp.min(jnp.where(work == mx, expert, float(ne)), axis=0, keepdims=True)
        sel = expert == first
        work = jnp.where(sel, -jnp.inf, work)
        sels.append(sel)
        vals.append(mx)
    exps = [jnp.exp(v - vals[0]) for v in vals]
    den = exps[0] + exps[1] + exps[2] + exps[3]

    onehot = jnp.zeros((ne, tm), F32)
    for sel in sels:
        onehot = onehot + sel.astype(F32)
    r_i = lax.broadcasted_iota(I32, (tm, tm), 0)
    c_i = lax.broadcasted_iota(I32, (tm, tm), 1)
    before = jnp.where(r_i < c_i, 1.0, 0.0).astype(BF16)
    earlier = _dot(onehot.astype(BF16), before)

    cnt = jnp.sum(onehot, axis=1, keepdims=True)
    seg8 = jnp.floor((cnt + (SEG_ALIGN - 1.0)) * (1.0 / SEG_ALIGN))
    e_r = lax.broadcasted_iota(I32, (ne, ne), 0)
    e_c = lax.broadcasted_iota(I32, (ne, ne), 1)
    lower = jnp.where(e_c < e_r, 1.0, 0.0).astype(BF16)
    off8 = _dot(lower, jnp.broadcast_to(seg8, (ne, LANES)).astype(BF16))[:, 0:1]
    seg_off = off8 * SEG_ALIGN
    base = seg_off + earlier

    sub = lax.broadcasted_iota(I32, (2 * TOP_K, tm), 0)
    token_rows = jnp.zeros((2 * TOP_K, tm), F32)
    for k in range(TOP_K):
        pos_k = jnp.sum(jnp.where(sels[k], base, 0.0), axis=0, keepdims=True)
        token_rows = jnp.where(sub == k, pos_k, token_rows)
        token_rows = jnp.where(sub == TOP_K + k, exps[k] / den, token_rows)
    first_lane = lax.broadcasted_iota(I32, (ne, tm), 1) == 0
    block = jnp.concatenate([
        token_rows,
        jnp.where(first_lane, seg8 * SEG_ALIGN, 0.0),
        jnp.where(first_lane, seg_off, 0.0),
        jnp.zeros((LANES - 2 * TOP_K - 2 * ne, tm), F32)], axis=0)
    by_token = block.T
    post_ref[j] = token_rows.astype(I32)
    pos_ref[rows, :] = by_token[:, 0:TOP_K].astype(I32)
    gates_ref[rows, :] = by_token[:, TOP_K:2 * TOP_K]
    rows_ref[j] = by_token[0:1, 2 * TOP_K:2 * TOP_K + ne].astype(I32)
    off_ref[j] = by_token[0:1, 2 * TOP_K + ne:2 * TOP_K + 2 * ne].astype(I32)


def _out_route(geom, layer, mixes, w_parts, x, mods, norm_g, router_w, router_b):
    t, d = geom.t, w_parts[0].shape[1]
    tiles = ROUTE_TILES_PER_STEP
    tm = TOKEN_TILE * tiles
    assert geom.n_ptiles % tiles == 0 and geom.tiles_per_lat % tiles == 0
    row = lambda i: (i, 0)
    const2 = lambda i: (0, 0)
    x_specs, x_args = _stream_specs(geom, x, d, tiles)
    in_specs = [pl.BlockSpec((tm, m.shape[1]), row) for m in mixes]
    in_specs += [pl.BlockSpec(w.shape, const2) for w in w_parts]
    in_specs += x_specs
    in_specs += [
        _mod_spec(geom, layer, 2, d, tiles),
        pl.BlockSpec((1, d), const2),
        _mod_spec(geom, layer, 3, d, tiles),
        _mod_spec(geom, layer, 4, d, tiles),
        pl.BlockSpec(router_w.shape, const2),
        pl.BlockSpec((N_EXPERTS, 1), const2),
    ]
    seg3 = lambda i: (i, 0, 0)
    out_shape = [
        jax.ShapeDtypeStruct((t, d), F32),
        jax.ShapeDtypeStruct((t, d), BF16),
        jax.ShapeDtypeStruct((t, TOP_K), I32),
        jax.ShapeDtypeStruct((geom.n_tiles, 2 * TOP_K, TOKEN_TILE), I32),
        jax.ShapeDtypeStruct((t, TOP_K), F32),
        jax.ShapeDtypeStruct((geom.n_tiles, 1, N_EXPERTS), I32),
        jax.ShapeDtypeStruct((geom.n_tiles, 1, N_EXPERTS), I32),
    ]
    out_specs = [
        pl.BlockSpec((tm, d), row),
        pl.BlockSpec((tm, d), row),
        pl.BlockSpec((tm, TOP_K), row),
        pl.BlockSpec((tiles, 2 * TOP_K, TOKEN_TILE), seg3),
        pl.BlockSpec((tm, TOP_K), row),
        pl.BlockSpec((tiles, 1, N_EXPERTS), seg3),
        pl.BlockSpec((tiles, 1, N_EXPERTS), seg3),
    ]
    return pl.pallas_call(
        functools.partial(_out_route_kernel, n_mix=len(mixes), n_x=len(x_args), n_ptiles=geom.n_ptiles // tiles,
                          tiles=tiles),
        out_shape=out_shape,
        grid=(geom.n_tiles // tiles,),
        in_specs=in_specs,
        out_specs=out_specs,
        compiler_params=_cparams("arbitrary"),
        name="out_route",
    )(*mixes, *w_parts, *x_args, mods, norm_g, mods, mods, router_w, router_b)


def _pack_pairs(v):
    n = v.shape[1] // 2
    bits = lax.bitcast_convert_type(v, U32)
    return (bits[:, :n] & jnp.uint32(0xFFFF0000)) | (bits[:, n:] >> 16)


def _unpack_pairs(p):
    hi = lax.bitcast_convert_type(p & jnp.uint32(0xFFFF0000), F32)
    lo = lax.bitcast_convert_type(p << 16, F32)
    return jnp.concatenate([hi, lo], axis=1).astype(BF16)


def _planned_copies(i, plan_refs, make_copy):
    for size, width, (local_ref, slot_ref, count_ref) in zip(COPY_SIZES, PLAN_WIDTHS, plan_refs):
        def one(c, carry, size=size, width=width, local_ref=local_ref, slot_ref=slot_ref):
            a = i * width + c
            make_copy(pl.multiple_of(local_ref[a], SEG_ALIGN), pl.multiple_of(slot_ref[a], SEG_ALIGN), size).start()
            return carry

        lax.fori_loop(0, count_ref[i], one, 0)


def _copy_plan(rows, seg_off, seg_dst):
    plan = []
    experts = jnp.arange(N_EXPERTS, dtype=I32)
    for n, (size, width) in enumerate(zip(COPY_SIZES, PLAN_WIDTHS)):
        if n == 0:
            count, done = rows // size, jnp.zeros_like(rows)
        else:
            count, done = (rows % (2 * size)) // size, rows - rows % (2 * size)
        cum = jnp.cumsum(count, axis=1)
        j = jnp.arange(width, dtype=I32)
        owner = jnp.sum((cum[:, None, :] <= j[None, :, None]).astype(I32), axis=2)
        pick = (jnp.minimum(owner, N_EXPERTS - 1)[:, :, None] == experts[None, None, :]).astype(I32)
        take = lambda v: jnp.sum(pick * v[:, None, :], axis=2)
        within = (j[None, :] - take(cum - count)) * size
        plan += [(take(seg_off + done) + within).reshape(-1).astype(I32),
                 (take(seg_dst + done) + within).reshape(-1).astype(I32),
                 cum[:, -1].astype(I32)]
    return plan


def _wait_copies(n_rows, make_copy):
    def wait_big(c, carry):
        make_copy(0, 0, WAIT_CHUNK).wait()
        return carry
    lax.fori_loop(0, n_rows // WAIT_CHUNK, wait_big, 0)
    size = WAIT_CHUNK // 2
    while size >= SEG_ALIGN:
        @pl.when(n_rows % (2 * size) >= size)
        def _(size=size):
            make_copy(0, 0, size).wait()

        size //= 2


def _dispatch_kernel(*refs):
    plan_refs, (tot_ref, h_ref, pos_ref, xs_ref, sorted_ref, sem) = _split_plan(refs)
    i = pl.program_id(0)
    buf = i % 2
    tm = h_ref.shape[0]
    n_sorted = sorted_ref.shape[1]
    pos = pos_ref[0]
    slot = lax.broadcasted_iota(I32, (n_sorted, tm), 0)
    hit = jnp.zeros((n_sorted, tm), F32)
    for k in range(TOP_K):
        hit = jnp.where(pos[k:k + 1, :] == slot, 1.0, hit)
    sorted_ref[buf] = _pack_pairs(_dot(hit.astype(BF16), h_ref[...]))

    def copies_from(b):
        def make_copy(local, slot, rows):
            return pltpu.make_async_copy(sorted_ref.at[b, pl.ds(local, rows)], xs_ref.at[pl.ds(slot, rows)],
                                         sem.at[b])
        return make_copy

    _planned_copies(i, plan_refs, copies_from(buf))

    @pl.when(i > 0)
    def _():
        _wait_copies(tot_ref[jnp.maximum(i - 1, 0)], copies_from(1 - buf))

    @pl.when(i == pl.num_programs(0) - 1)
    def _():
        _wait_copies(tot_ref[i], copies_from(buf))


def _split_plan(refs):
    n = 3 * len(COPY_SIZES)
    return [refs[k:k + 3] for k in range(0, n, 3)], refs[n:]


def _dispatch(h, pos, plan, tile_rows, n_slots):
    t, d = h.shape
    tm = TOKEN_TILE
    return pl.pallas_call(
        _dispatch_kernel,
        out_shape=jax.ShapeDtypeStruct((n_slots, d // 2), U32),
        grid_spec=pltpu.PrefetchScalarGridSpec(
            num_scalar_prefetch=len(plan) + 1,
            grid=(t // tm,),
            in_specs=[
                pl.BlockSpec((tm, d), lambda i, *_: (i, 0)),
                pl.BlockSpec((1, 2 * TOP_K, tm), lambda i, *_: (i, 0, 0)),
            ],
            out_specs=pl.BlockSpec(memory_space=pl.ANY),
            scratch_shapes=[pltpu.VMEM((2, SORTED_ROWS, d // 2), U32), pltpu.SemaphoreType.DMA((2,))],
        ),
        compiler_params=_cparams("arbitrary"),
        name="moe_dispatch",
    )(*plan, tile_rows, h, pos)


def _ffn_kernel(rows_ref, start_ref, wgu_ref, bgu_ref, wd_ref, bd_ref, xs_ref, ys_ref,
                wgu_bf, wd_bf, xbuf, ybuf, sem_in, sem_out):
    e = pl.program_id(0)
    d_ff = wd_ref.shape[1]
    tb = xbuf.shape[1]
    n_rows = rows_ref[e]
    n_tiles = (n_rows + tb - 1) // tb
    base = start_ref[e]

    chunk = 128
    def cast_gu(c, carry):
        r = pl.multiple_of(c * chunk, chunk)
        wgu_bf[pl.ds(r, chunk), :] = wgu_ref[0, pl.ds(r, chunk), :].astype(BF16)
        return carry
    lax.fori_loop(0, wgu_ref.shape[1] // chunk, cast_gu, 0)
    def cast_d(c, carry):
        r = pl.multiple_of(c * chunk, chunk)
        wd_bf[pl.ds(r, chunk), :] = wd_ref[0, pl.ds(r, chunk), :].astype(BF16)
        return carry
    lax.fori_loop(0, d_ff // chunk, cast_d, 0)

    half = tb // 2
    last_small = (n_rows - (n_tiles - 1) * tb) <= half

    def x_copy(s, slot):
        r = pl.multiple_of(base + s * tb, EXPERT_TILE)
        return pltpu.make_async_copy(xs_ref.at[pl.ds(r, tb)], xbuf.at[slot], sem_in.at[slot])

    def y_copy(s, slot, rows=tb):
        r = pl.multiple_of(base + s * tb, EXPERT_TILE)
        return pltpu.make_async_copy(ybuf.at[slot, pl.ds(0, rows)], ys_ref.at[pl.ds(r, rows)], sem_out.at[slot])

    def ffn_rows(s, slot, n):
        rows = s * tb + lax.broadcasted_iota(I32, (n, 1), 0)
        x = _unpack_pairs(jnp.where(rows < n_rows, xbuf[slot, pl.ds(0, n), :], jnp.uint32(0)))
        gu = _dot(x, wgu_bf[...]) + bgu_ref[0]
        gate = jnp.minimum(gu[:, :d_ff], SWIGLU_LIMIT)
        up = jnp.clip(gu[:, d_ff:], -SWIGLU_LIMIT, SWIGLU_LIMIT)
        act = (up + 1.0) * (gate * (1.0 / (1.0 + jnp.exp(-SWIGLU_ALPHA * gate))))
        y = _dot(act.astype(BF16), wd_bf[...]) + bd_ref[0]
        ybuf[slot, pl.ds(0, n), :] = _pack_pairs(y.astype(BF16).astype(F32))
        y_copy(s, slot, n).start()

    @pl.when(n_tiles > 0)
    def _():
        x_copy(0, 0).start()

    def tile(s, carry):
        slot = s % 2
        x_copy(s, slot).wait()

        @pl.when(s + 1 < n_tiles)
        def _():
            x_copy(s + 1, 1 - slot).start()

        @pl.when(s >= 2)
        def _():
            y_copy(s - 2, slot).wait()

        small = jnp.logical_and(s == n_tiles - 1, last_small)

        @pl.when(jnp.logical_not(small))
        def _():
            ffn_rows(s, slot, tb)

        @pl.when(small)
        def _():
            ffn_rows(s, slot, half)

        return carry

    lax.fori_loop(0, n_tiles, tile, 0)

    @pl.when(n_tiles >= 2)
    def _():
        y_copy(n_tiles - 2, n_tiles % 2).wait()

    @pl.when(jnp.logical_and(n_tiles >= 1, jnp.logical_not(last_small)))
    def _():
        y_copy(n_tiles - 1, (n_tiles - 1) % 2).wait()

    @pl.when(jnp.logical_and(n_tiles >= 1, last_small))
    def _():
        y_copy(n_tiles - 1, (n_tiles - 1) % 2, half).wait()


def _expert_ffn(layer, xs, expert_rows, expert_start, w_gu, b_gu, w_down, b_down):
    n_slots, packed_w = xs.shape
    tb = FFN_TILE
    depth, ne, d, two_f = w_gu.shape
    d_ff = two_f // 2
    exp4 = lambda e, *_: (layer, e, 0, 0)
    return pl.pallas_call(
        _ffn_kernel,
        out_shape=jax.ShapeDtypeStruct((n_slots, packed_w), U32),
        grid_spec=pltpu.PrefetchScalarGridSpec(
            num_scalar_prefetch=2,
            grid=(ne,),
            in_specs=[
                pl.BlockSpec((None, 1, d, two_f), exp4),
                pl.BlockSpec((None, 1, 1, two_f), exp4),
                pl.BlockSpec((None, 1, d_ff, d), exp4),
                pl.BlockSpec((None, 1, 1, d), exp4),
                pl.BlockSpec(memory_space=pl.ANY),
            ],
            out_specs=pl.BlockSpec(memory_space=pl.ANY),
            scratch_shapes=[
                pltpu.VMEM((d, two_f), BF16), pltpu.VMEM((d_ff, d), BF16),
                pltpu.VMEM((2, tb, packed_w), U32), pltpu.VMEM((2, tb, packed_w), U32),
                pltpu.SemaphoreType.DMA((2,)), pltpu.SemaphoreType.DMA((2,)),
            ],
        ),
        compiler_params=_cparams("arbitrary"),
        name="expert_ffn",
    )(expert_rows, expert_start, w_gu, b_gu.reshape(depth, ne, 1, two_f),
      w_down, b_down.reshape(depth, ne, 1, d), xs)


def _combine_kernel(*refs, final, n_ptiles):
    plan_refs, (tot_ref, x_ref, pos_ref, gates_ref, mg_ref, *rest) = _split_plan(refs)
    if final:
        fn_ref, ys_ref, op_ref, os_ref, buf, sem = rest
    else:
        ys_ref, o_ref, buf, sem = rest
    i = pl.program_id(0)
    cur = i % 2
    tm = x_ref.shape[0]
    n_sorted = buf.shape[1]

    def copies_into(b):
        def make_copy(local, slot, rows):
            return pltpu.make_async_copy(ys_ref.at[pl.ds(slot, rows)], buf.at[b, pl.ds(local, rows)], sem.at[b])
        return make_copy

    @pl.when(i == 0)
    def _():
        buf[...] = jnp.zeros_like(buf)
        _planned_copies(i, plan_refs, copies_into(cur))

    @pl.when(i + 1 < pl.num_programs(0))
    def _():
        _planned_copies(i + 1, plan_refs, copies_into(1 - cur))

    pos = pos_ref[...]
    g = gates_ref[...]
    lane = lax.broadcasted_iota(I32, (tm, n_sorted), 1)
    weight = jnp.zeros((tm, n_sorted), F32)
    for k in range(TOP_K):
        weight = jnp.where(pos[:, k:k + 1] == lane, g[:, k:k + 1], weight)
    _wait_copies(tot_ref[i], copies_into(cur))
    y = _dot(weight.astype(BF16), _unpack_pairs(buf[cur]))
    xn = x_ref[...] + mg_ref[...] * y
    if final:
        xn = _rms(xn, fn_ref[...])

        @pl.when(i < n_ptiles)
        def _():
            op_ref[...] = xn

        @pl.when(i >= n_ptiles)
        def _():
            os_ref[...] = xn
    else:
        o_ref[...] = xn


def _combine(geom, layer, ys, pos, plan, tile_rows, x, gates, mods, final_g):
    t, d = x.shape
    tm = TOKEN_TILE
    final = final_g is not None

    def mod_imap(i, *_):
        return ((layer * MOD_ROWS + geom.group(i)) * N_MOD + 5, 0, 0)

    row = lambda i, *_: (i, 0)
    in_specs = [
        pl.BlockSpec((tm, d), row),
        pl.BlockSpec((tm, TOP_K), row),
        pl.BlockSpec((tm, TOP_K), row),
        pl.BlockSpec((None, 1, d), mod_imap),
    ]
    args = [*plan, tile_rows, x, pos, gates, mods]
    if final:
        in_specs.append(pl.BlockSpec((1, d), lambda i, *_: (0, 0)))
        args.append(final_g)
    in_specs.append(pl.BlockSpec(memory_space=pl.ANY))
    args.append(ys)
    if final:
        n_pt = geom.n_ptiles
        out_shape = [jax.ShapeDtypeStruct((geom.tp, d), F32), jax.ShapeDtypeStruct((t - geom.tp, d), F32)]
        out_specs = [pl.BlockSpec((tm, d), lambda i, *_: (jnp.minimum(i, n_pt - 1), 0)),
                     pl.BlockSpec((tm, d), lambda i, *_: (jnp.maximum(i - n_pt, 0), 0))]
    else:
        out_shape = jax.ShapeDtypeStruct((t, d), F32)
        out_specs = pl.BlockSpec((tm, d), row)
    return pl.pallas_call(
        functools.partial(_combine_kernel, final=final, n_ptiles=geom.n_ptiles),
        out_shape=out_shape,
        grid_spec=pltpu.PrefetchScalarGridSpec(
            num_scalar_prefetch=len(plan) + 1,
            grid=(t // tm,),
            in_specs=in_specs,
            out_specs=out_specs,
            scratch_shapes=[pltpu.VMEM((2, SORTED_ROWS, d // 2), U32), pltpu.SemaphoreType.DMA((2,))],
        ),
        compiler_params=_cparams("arbitrary"),
        name="moe_combine",
    )(*args)


def _moe(geom, layer, h, pos, pos_t, gates, seg_rows, seg_off, x, mods, w_gu, b_gu, w_down, b_down, final_g):
    t = h.shape[0]
    tb = EXPERT_TILE
    n_tok_tiles = seg_rows.shape[0]
    max_rows = t * TOP_K + n_tok_tiles * N_EXPERTS * (SEG_ALIGN - 1)
    n_blocks = -(-max_rows // tb) + N_EXPERTS
    rows = seg_rows[:, 0, :]
    cnt = jnp.sum(rows, axis=0)
    n_tiles_e = (cnt + tb - 1) // tb
    tile_end = jnp.cumsum(n_tiles_e)
    tile_start = tile_end - n_tiles_e
    expert_start = (tile_start * tb).astype(I32)
    seg_dst = expert_start[None, :] + jnp.cumsum(rows, axis=0) - rows
    plan = _copy_plan(rows, seg_off[:, 0, :], seg_dst)
    tile_rows = jnp.sum(rows, axis=1).astype(I32)
    xs = _dispatch(h, pos_t, plan, tile_rows, n_blocks * tb + FFN_TILE - tb)
    ys = _expert_ffn(layer, xs, cnt.astype(I32), expert_start, w_gu, b_gu, w_down, b_down)
    return _combine(geom, layer, ys, pos, plan, tile_rows, x, gates, mods, final_g)


def kernel(x_prompt, x_sample, cache_b_k, cache_b_v, cache_c_k, cache_c_v, c, c_ctx,
           mod_w, mod_b, norm_mix, norm_ffn, even_w_in, even_w_out, even_sink,
           odd_w_in, odd_w_out, odd_q_norm, odd_k_norm, router_w, router_b,
           moe_w_gu, moe_b_gu, moe_w_down, moe_b_down, final_norm):
    bp, lp, d = x_prompt.shape
    bs, ls, _ = x_sample.shape
    past = cache_b_k.shape[2]
    depth = mod_w.shape[0]
    geom = _Geom(bp, lp, bs, ls)
    tp = geom.tp

    x = (x_prompt.reshape(tp, d), x_sample.reshape(bs * ls, d))
    cond = jnp.concatenate([c_ctx[None, :], c, jnp.zeros((MOD_ROWS - 1 - bs, d), F32)], axis=0)
    mods = _modulation(cond, mod_w, mod_b).reshape(depth * MOD_ROWS * N_MOD, 1, d)

    cn, sn = _dft_tables(A_GROUP_DIM)
    dft_chan = jnp.asarray(np.concatenate([cn, sn], axis=1), BF16)
    dft_p = [jnp.asarray(m, BF16) for m in _dft_tables(lp)]
    dft_s = [jnp.asarray(m, BF16) for m in _dft_tables(ls)]
    rope_b = [jnp.asarray(m) for m in _rope_tables(TOKEN_TILE, ls, B_HEAD_DIM)]
    rope_c = [jnp.asarray(m) for m in _rope_tables(TOKEN_TILE, ls, C_HEAD_DIM)]

    states = {"bk": [], "bv": [], "ck": [], "cv": []}
    for layer in range(depth):
        j = layer // 2
        g_mix = norm_mix[layer][None, :]
        g_ffn = norm_ffn[layer][None, :]
        if layer % 2 == 0:
            pair = _paired_head_order(B_KV_HEADS, B_HEADS // B_KV_HEADS, B_HEAD_DIM)
            w_in = even_w_in[j]
            w_in = jnp.concatenate([w_in[:, :A_WIDTH], w_in[:, A_WIDTH:A_WIDTH + B_Q_WIDTH][:, pair],
                                    w_in[:, A_WIDTH + B_Q_WIDTH:]], axis=1).astype(BF16)
            tc, ts, q, k, v, k_state, v_state = _in_projection(
                geom, layer, x, g_mix, mods, w_in, rope_b[0], rope_b[1], [dft_chan],
                _proj_even_kernel, (A_WIDTH, A_WIDTH, B_Q_WIDTH, B_KV_WIDTH, B_KV_WIDTH), B_KV_WIDTH, "proj_even")
            states["bk"].append(k_state.reshape(bp, lp, B_KV_HEADS, B_HEAD_DIM))
            states["bv"].append(v_state.reshape(bp, lp, B_KV_HEADS, B_HEAD_DIM))
            four = _fourier_tokens(tc, ts, dft_p[0], dft_p[1], bp, lp, 0, None)
            four = _fourier_tokens(tc, ts, dft_s[0], dft_s[1], bs, ls, tp, four)
            sink = even_sink[j]
            common = dict(kv_heads=B_KV_HEADS, groups=B_HEADS // B_KV_HEADS, dh=B_HEAD_DIM)
            att = _attention(q, k, v, None, sink, None, n_seq=bp, seq_len=lp, row0=0, q_tile=lp,
                             window=None, **common)
            ctx = (cache_b_k[:, j].reshape(bs, past, B_KV_WIDTH).astype(BF16),
                   cache_b_v[:, j].reshape(bs, past, B_KV_WIDTH).astype(BF16))
            att = _attention(q, k, v, ctx, sink, att, n_seq=bs, seq_len=ls, row0=tp, q_tile=ATTN_Q_TILE,
                             window=WINDOW, **common)
            w_out = even_w_out[j].astype(BF16)
            mixes = [four, att]
            w_parts = [w_out[:A_WIDTH], w_out[A_WIDTH:][pair]]
        else:
            q, k, v, k_state, v_state = _in_projection(
                geom, layer, x, g_mix, mods, odd_w_in[j].astype(BF16), rope_c[0], rope_c[1],
                [odd_q_norm[j][None, :], odd_k_norm[j][None, :]],
                _proj_odd_kernel, (C_Q_WIDTH, C_KV_WIDTH, C_KV_WIDTH), C_KV_WIDTH, "proj_odd")
            states["ck"].append(k_state.reshape(bp, lp, C_KV_HEADS, C_HEAD_DIM))
            states["cv"].append(v_state.reshape(bp, lp, C_KV_HEADS, C_HEAD_DIM))
            common = dict(kv_heads=C_KV_HEADS, groups=C_HEADS // C_KV_HEADS, dh=C_HEAD_DIM, window=None)
            att = _attention(q, k, v, None, None, None, n_seq=bp, seq_len=lp, row0=0, q_tile=lp, **common)
            ctx = (cache_c_k[:, j].reshape(bs, past, C_KV_WIDTH).astype(BF16),
                   cache_c_v[:, j].reshape(bs, past, C_KV_WIDTH).astype(BF16))
            att = _attention(q, k, v, ctx, None, att, n_seq=bs, seq_len=ls, row0=tp, q_tile=DENSE_Q_TILE, **common)
            mixes = [att]
            w_parts = [odd_w_out[j].astype(BF16)]
        x, h, pos, pos_t, gates, seg_rows, seg_off = _out_route(
            geom, layer, mixes, w_parts, x, mods, g_ffn, router_w[layer].T.astype(BF16), router_b[layer][:, None])
        final_g = final_norm[None, :] if layer == depth - 1 else None
        x = _moe(geom, layer, h, pos, pos_t, gates, seg_rows, seg_off, x, mods,
                 moe_w_gu, moe_b_gu, moe_w_down, moe_b_down, final_g)

    y_prompt = x[0].reshape(bp, lp, d)
    y_sample = x[1].reshape(bs, ls, d)
    return (y_prompt, y_sample,
            jnp.stack(states["bk"], axis=1), jnp.stack(states["bv"], axis=1),
            jnp.stack(states["ck"], axis=1), jnp.stack(states["cv"], axis=1))
```

```python
import functools

import numpy as np
import jax
import jax.numpy as jnp
from jax import lax
from jax.experimental import pallas as pl
from jax.experimental.pallas import tpu as pltpu

F32 = jnp.float32
BF16 = jnp.bfloat16
I32 = jnp.int32
U32 = jnp.uint32

GRID_W = 64
A_GROUPS = 4
A_GROUP_DIM = 128
A_WIDTH = A_GROUPS * A_GROUP_DIM
B_HEADS = 8
B_KV_HEADS = 2
B_HEAD_DIM = 64
B_Q_WIDTH = B_HEADS * B_HEAD_DIM
B_KV_WIDTH = B_KV_HEADS * B_HEAD_DIM
WINDOW = 128
C_HEADS = 8
C_KV_HEADS = 2
C_HEAD_DIM = 128
C_Q_WIDTH = C_HEADS * C_HEAD_DIM
C_KV_WIDTH = C_KV_HEADS * C_HEAD_DIM
ROPE_THETA = 10000.0
N_EXPERTS = 32
TOP_K = 4
SWIGLU_LIMIT = 7.0
SWIGLU_ALPHA = 1.702
EPS = 1e-6

LANES = 128
SUBLANES = 8
TOKEN_TILE = 256
ROUTE_TILES_PER_STEP = 4
EXPERT_TILE = 128
FFN_TILE = 256
ATTN_Q_TILE = 128
DENSE_Q_TILE = 256
ATTN_KEY_CHUNK = 1024
PROMPT_SEQS_PER_STEP = 2
SEG_ALIGN = SUBLANES
COPY_SIZES = (32, 16, 8)
WAIT_CHUNK = 256
SORTED_ROWS = -(-(TOKEN_TILE * TOP_K + N_EXPERTS * (SEG_ALIGN - 1)) // LANES) * LANES
PLAN_WIDTHS = (SORTED_ROWS // COPY_SIZES[0],) + (N_EXPERTS,) * (len(COPY_SIZES) - 1)
VMEM_LIMIT = 56 * 1024 * 1024
MASKED = -1e30
N_MOD = 6
MOD_ROWS = SUBLANES


def _cparams(*sem):
    return pltpu.CompilerParams(dimension_semantics=tuple(sem), vmem_limit_bytes=VMEM_LIMIT)


def _dot(a, b):
    return jnp.dot(a, b, preferred_element_type=F32)


def _dot_nt(a, b):
    return lax.dot_general(a, b, (((1,), (1,)), ((), ())), preferred_element_type=F32)


def _rms(x, g):
    return x * lax.rsqrt(jnp.mean(x * x, axis=-1, keepdims=True) + EPS) * g


def _dft_tables(n):
    j = np.arange(n, dtype=np.int64)
    ang = 2.0 * np.pi * ((j[:, None] * j[None, :]) % n).astype(np.float64) / n
    s = 1.0 / np.sqrt(n)
    return np.cos(ang) * s, np.sin(ang) * s


def _paired_head_order(kv_heads, groups, dh):
    assert kv_heads * dh == LANES
    return np.array([(kv * groups + g) * dh + d for g in range(groups) for kv in range(kv_heads) for d in range(dh)])


def _rope_tables(n_prompt_rows, n_latent, head_dim):
    quarter = head_dim // 4
    pos = np.arange(n_latent)
    row = (pos // GRID_W).astype(np.float32)
    col = (pos % GRID_W).astype(np.float32)
    inv = (np.float32(ROPE_THETA) ** (-np.arange(quarter, dtype=np.float32) / np.float32(quarter))).astype(np.float32)
    ang_row = (row[:, None] * inv[None, :]).astype(np.float32)
    ang_col = (col[:, None] * inv[None, :]).astype(np.float32)
    cos_h = np.concatenate([np.cos(ang_row)] * 2 + [np.cos(ang_col)] * 2, axis=1)
    sin_h = np.concatenate([-np.sin(ang_row), np.sin(ang_row), -np.sin(ang_col), np.sin(ang_col)], axis=1)
    reps = LANES // head_dim
    cos_l = np.tile(cos_h, (1, reps)).astype(np.float32)
    sin_l = np.tile(sin_h, (1, reps)).astype(np.float32)
    cos = np.concatenate([np.ones((n_prompt_rows, LANES), np.float32), cos_l], axis=0)
    sin = np.concatenate([np.zeros((n_prompt_rows, LANES), np.float32), sin_l], axis=0)
    return cos, sin


def _rope(x, cos, sin, quarter):
    lane = lax.broadcasted_iota(I32, (x.shape[0], LANES), 1)
    first = ((lane // quarter) % 2) == 0
    outs = []
    for c in range(x.shape[1] // LANES):
        xc = x[:, c * LANES:(c + 1) * LANES]
        partner = jnp.where(first, pltpu.roll(xc, LANES - quarter, 1), pltpu.roll(xc, quarter, 1))
        outs.append(xc * cos + partner * sin)
    return outs[0] if len(outs) == 1 else jnp.concatenate(outs, axis=1)


def _head_rms(x, g):
    outs = []
    for c in range(x.shape[1] // LANES):
        outs.append(_rms(x[:, c * LANES:(c + 1) * LANES], g))
    return outs[0] if len(outs) == 1 else jnp.concatenate(outs, axis=1)


def _mod_kernel(c_ref, w_ref, b_ref, o_ref):
    c = c_ref[...]
    s = c * (1.0 / (1.0 + jnp.exp(-c)))
    o_ref[0] = _dot(s.astype(BF16), w_ref[0].astype(BF16)) + b_ref[0]


def _modulation(cond, mod_w, mod_b):
    depth, d, n = mod_w.shape
    tn = 1536
    return pl.pallas_call(
        _mod_kernel,
        out_shape=jax.ShapeDtypeStruct((depth, MOD_ROWS, n), F32),
        grid=(depth, n // tn),
        in_specs=[
            pl.BlockSpec((MOD_ROWS, d), lambda l, j: (0, 0)),
            pl.BlockSpec((1, d, tn), lambda l, j: (l, 0, j)),
            pl.BlockSpec((1, 1, tn), lambda l, j: (l, 0, j)),
        ],
        out_specs=pl.BlockSpec((1, MOD_ROWS, tn), lambda l, j: (l, 0, j)),
        compiler_params=_cparams("arbitrary", "arbitrary"),
        name="modulation",
    )(cond, mod_w, mod_b.reshape(depth, 1, n))


class _Geom:
    def __init__(self, bp, lp, bs, ls):
        self.bp, self.lp, self.bs, self.ls = bp, lp, bs, ls
        self.tp = bp * lp
        self.t = bp * lp + bs * ls
        assert lp == TOKEN_TILE and ls % TOKEN_TILE == 0 and self.tp % ls == 0
        self.n_ptiles = self.tp // TOKEN_TILE
        self.tiles_per_lat = ls // TOKEN_TILE
        self.n_tiles = self.t // TOKEN_TILE

    def group(self, i):
        return jnp.where(i < self.n_ptiles, 0, 1 + (i - self.n_ptiles) // self.tiles_per_lat)

    def pos_block(self, i):
        return jnp.where(i < self.n_ptiles, 0, 1 + (i - self.n_ptiles) % self.tiles_per_lat)


def _mod_spec(geom, layer, which, d, tiles=1):
    def imap(i):
        return ((layer * MOD_ROWS + geom.group(i * tiles)) * N_MOD + which, 0, 0)
    return pl.BlockSpec((None, 1, d), imap)


def _stream_specs(geom, x, d, tiles=1):
    tm = TOKEN_TILE * tiles
    n_p = geom.n_ptiles // tiles
    if isinstance(x, tuple):
        return ([pl.BlockSpec((tm, d), lambda i, *_: (jnp.minimum(i, n_p - 1), 0)),
                 pl.BlockSpec((tm, d), lambda i, *_: (jnp.maximum(i - n_p, 0), 0))], list(x))
    return [pl.BlockSpec((tm, d), lambda i, *_: (i, 0))], [x]


def _stream_tile(x_refs, n_ptiles):
    if len(x_refs) == 1:
        return x_refs[0][...]
    return jnp.where(pl.program_id(0) < n_ptiles, x_refs[0][...], x_refs[1][...])


def _store_kv(k, v, kb_ref, vb_ref, ks_ref, vs_ref, n_ptiles):
    kb_ref[...] = k.astype(BF16)
    vb_ref[...] = v.astype(BF16)

    @pl.when(pl.program_id(0) < n_ptiles)
    def _():
        ks_ref[...] = k
        vs_ref[...] = v


def _proj_even_kernel(*refs, n_x, n_ptiles):
    x_refs = refs[:n_x]
    (g_ref, sh_ref, sc_ref, w_ref, cos_ref, sin_ref, dft_ref,
     tc_ref, ts_ref, q_ref, kb_ref, vb_ref, ks_ref, vs_ref) = refs[n_x:]
    h = _rms(_stream_tile(x_refs, n_ptiles), g_ref[...]) * (1.0 + sc_ref[...]) + sh_ref[...]
    p = _dot(h.astype(BF16), w_ref[...])
    cos = cos_ref[...]
    sin = sin_ref[...]
    dft = dft_ref[...]
    tcs, tss = [], []
    for g in range(A_GROUPS):
        t = _dot(p[:, g * A_GROUP_DIM:(g + 1) * A_GROUP_DIM].astype(BF16), dft)
        tcs.append(t[:, :A_GROUP_DIM])
        tss.append(t[:, A_GROUP_DIM:])
    tc_ref[...] = jnp.concatenate(tcs, axis=1).astype(BF16)
    ts_ref[...] = jnp.concatenate(tss, axis=1).astype(BF16)
    o = A_WIDTH
    q = _rope(p[:, o:o + B_Q_WIDTH], cos, sin, B_HEAD_DIM // 4)
    q_ref[...] = (q * B_HEAD_DIM ** -0.5).astype(BF16)
    o += B_Q_WIDTH
    k = _rope(p[:, o:o + B_KV_WIDTH], cos, sin, B_HEAD_DIM // 4)
    o += B_KV_WIDTH
    _store_kv(k, p[:, o:o + B_KV_WIDTH], kb_ref, vb_ref, ks_ref, vs_ref, n_ptiles)


def _proj_odd_kernel(*refs, n_x, n_ptiles):
    x_refs = refs[:n_x]
    (g_ref, sh_ref, sc_ref, w_ref, cos_ref, sin_ref, qn_ref, kn_ref,
     q_ref, kb_ref, vb_ref, ks_ref, vs_ref) = refs[n_x:]
    h = _rms(_stream_tile(x_refs, n_ptiles), g_ref[...]) * (1.0 + sc_ref[...]) + sh_ref[...]
    p = _dot(h.astype(BF16), w_ref[...])
    cos = cos_ref[...]
    sin = sin_ref[...]
    q = _head_rms(p[:, :C_Q_WIDTH], qn_ref[...])
    k = _head_rms(p[:, C_Q_WIDTH:C_Q_WIDTH + C_KV_WIDTH], kn_ref[...])
    q_ref[...] = (_rope(q, cos, sin, C_HEAD_DIM // 4) * C_HEAD_DIM ** -0.5).astype(BF16)
    k = _rope(k, cos, sin, C_HEAD_DIM // 4)
    _store_kv(k, p[:, C_Q_WIDTH + C_KV_WIDTH:], kb_ref, vb_ref, ks_ref, vs_ref, n_ptiles)


def _in_projection(geom, layer, x, norm_g, mods, w, cos, sin, extras, kernel, out_widths, kv_width, name):
    t, d = geom.t, w.shape[0]
    tm = TOKEN_TILE
    n_out = w.shape[1]
    row = lambda i: (i, 0)
    const2 = lambda i: (0, 0)
    x_specs, x_args = _stream_specs(geom, x, d)
    in_specs = x_specs + [
        pl.BlockSpec((1, d), const2),
        _mod_spec(geom, layer, 0, d),
        _mod_spec(geom, layer, 1, d),
        pl.BlockSpec((d, n_out), const2),
        pl.BlockSpec((tm, LANES), lambda i: (geom.pos_block(i), 0)),
        pl.BlockSpec((tm, LANES), lambda i: (geom.pos_block(i), 0)),
    ] + [pl.BlockSpec(e.shape, const2) for e in extras]
    return pl.pallas_call(
        functools.partial(kernel, n_x=len(x_args), n_ptiles=geom.n_ptiles),
        out_shape=([jax.ShapeDtypeStruct((t, wd), BF16) for wd in out_widths]
                   + [jax.ShapeDtypeStruct((geom.tp, kv_width), F32)] * 2),
        grid=(geom.n_tiles,),
        in_specs=in_specs,
        out_specs=([pl.BlockSpec((tm, wd), row) for wd in out_widths]
                   + [pl.BlockSpec((tm, kv_width), lambda i: (jnp.minimum(i, geom.n_ptiles - 1), 0))] * 2),
        compiler_params=_cparams("arbitrary"),
        name=name,
    )(*x_args, norm_g, mods, mods, w, cos, sin, *extras)


def _fourier_kernel(cl_ref, sl_ref, tc_ref, ts_ref, *rest):
    o_ref = rest[-1]
    o_ref[...] = (_dot(cl_ref[...], tc_ref[...]) - _dot(sl_ref[...], ts_ref[...])).astype(o_ref.dtype)


def _fourier_tokens(tc, ts, cl, sl, n_seq, seq_len, row0, prev):
    t, width = tc.shape
    tr = min(seq_len, 512)
    n_r = seq_len // tr
    assert row0 % seq_len == 0
    seq0 = row0 // seq_len
    out0 = row0 // tr
    in_specs = [
        pl.BlockSpec((tr, seq_len), lambda s, r: (r, 0)),
        pl.BlockSpec((tr, seq_len), lambda s, r: (r, 0)),
        pl.BlockSpec((seq_len, width), lambda s, r: (seq0 + s, 0)),
        pl.BlockSpec((seq_len, width), lambda s, r: (seq0 + s, 0)),
    ]
    args = [cl, sl, tc, ts]
    aliases = {}
    if prev is not None:
        in_specs.append(pl.BlockSpec(memory_space=pl.ANY))
        args.append(prev)
        aliases = {4: 0}
    return pl.pallas_call(
        _fourier_kernel,
        out_shape=jax.ShapeDtypeStruct((t, width), BF16),
        grid=(n_seq, n_r),
        in_specs=in_specs,
        out_specs=pl.BlockSpec((tr, width), lambda s, r: (out0 + s * n_r + r, 0)),
        input_output_aliases=aliases,
        compiler_params=_cparams("arbitrary", "arbitrary"),
        name="fourier_tokens",
    )(*args)


def _attend(q, chunks, sink, o0, dh, den_col):
    m = sink
    acc = None
    den = None
    for k, v, mask in chunks:
        s = _dot_nt(q, k)
        if mask is not None:
            s = jnp.where(mask, s, MASKED)
        m_new = jnp.max(s, axis=-1, keepdims=True)
        if m is not None:
            m_new = jnp.maximum(m, m_new)
        p = jnp.exp(s - m_new)
        pv = _dot(p.astype(BF16), v)
        if acc is None:
            acc = pv
            if den_col is None:
                den = jnp.sum(p, axis=-1, keepdims=True)
        else:
            alpha = jnp.exp(m - m_new)
            acc = alpha * acc + pv
            if den_col is None:
                den = alpha * den + jnp.sum(p, axis=-1, keepdims=True)
        m = m_new
    if den_col is not None:
        den = acc[:, den_col:den_col + 1]
    if sink is not None:
        den = den + jnp.exp(sink - m)
    return acc[:, o0:o0 + dh] / den


def _attend_two_pass(q, chunks, sink):
    scores = []
    m = sink
    for k, _, mask in chunks:
        s = _dot_nt(q, k)
        if mask is not None:
            s = jnp.where(mask, s, MASKED)
        scores.append(s)
        mx = jnp.max(s, axis=-1, keepdims=True)
        m = mx if m is None else jnp.maximum(m, mx)
    den = None if sink is None else jnp.exp(sink - m)
    acc = None
    for (_, v, _), s in zip(chunks, scores):
        e = jnp.exp(s - m)
        es = jnp.sum(e, axis=-1, keepdims=True)
        den = es if den is None else den + es
        o = _dot(e.astype(BF16), v)
        acc = o if acc is None else acc + o
    return acc / den


def _stack_heads(q, kv, groups, dh):
    return jnp.concatenate([q[:, (kv * groups + g) * dh:(kv * groups + g + 1) * dh] for g in range(groups)], axis=0)


def _sink_column(sink_ref, kv, groups, rows):
    return jnp.concatenate([jnp.full((rows, 1), sink_ref[kv * groups + g], F32) for g in range(groups)], axis=0)


def _head_values(v, kv, dh, with_ones):
    if not with_ones:
        return v[:, kv * dh:(kv + 1) * dh], 0, None
    assert dh == LANES
    lane = lax.broadcasted_iota(I32, (v.shape[0], LANES), 1)
    ones = jnp.where(lane == 0, 1.0, 0.0).astype(BF16)
    return jnp.concatenate([v[:, kv * dh:(kv + 1) * dh], ones], axis=1), 0, dh


def _paired_heads_attention(q, sources, sink_ref, groups, dh):
    rows = q.shape[0]
    half_q = lax.broadcasted_iota(I32, (rows, LANES), 1) // dh
    per_head = []
    for kv in range(2):
        qh = jnp.concatenate(
            [jnp.where(half_q == kv, q[:, g * LANES:(g + 1) * LANES], jnp.zeros((), q.dtype)) for g in range(groups)],
            axis=0)
        chunks = []
        for k, v, msk in sources:
            half_v = lax.broadcasted_iota(I32, v.shape, 1) // dh
            chunks.append((k, jnp.where(half_v == kv, v, jnp.zeros((), v.dtype)), msk))
        sink = _sink_column(sink_ref, kv, groups, rows) if sink_ref is not None else None
        per_head.append(_attend_two_pass(qh, chunks, sink))
    return jnp.concatenate(
        [per_head[0][g * rows:(g + 1) * rows] + per_head[1][g * rows:(g + 1) * rows] for g in range(groups)], axis=1)


def _attn_kernel(*refs, kv_heads, groups, dh, has_sink, has_ctx, window, q_tile, seq_len, chunk, seqs):
    refs = list(refs)
    sink_ref = refs.pop(0) if has_sink else None
    q_ref, k_ref, v_ref = refs[:3]
    ck_ref, cv_ref = (refs[3], refs[4]) if has_ctx else (None, None)
    o_ref = refs[-1]
    rows = q_ref.shape[0] // seqs
    assert seqs == 1 or (window is None and not has_ctx and q_tile == seq_len)
    for j in range(seqs):
        q_rows = slice(j * rows, (j + 1) * rows)
        q = q_ref[q_rows, :]
        if window is None:
            spans = [(j * seq_len + c * chunk, chunk) for c in range(seq_len // chunk)]
            mask = None
        else:
            n = pl.program_id(1)
            band = q_tile + 2 * window
            start = pl.multiple_of(jnp.clip(n * q_tile - window, 0, seq_len - band), LANES)
            spans = [(start, band)]
            qpos = n * q_tile + lax.broadcasted_iota(I32, (groups * rows, band), 0) % rows
            kpos = start + lax.broadcasted_iota(I32, (groups * rows, band), 1)
            mask = jnp.abs(kpos - qpos) <= window
        sources = [(k_ref[pl.ds(s0, n_s), :], v_ref[pl.ds(s0, n_s), :], mask) for s0, n_s in spans]
        if has_ctx:
            sources.append((ck_ref[...], cv_ref[...], None))
        if 2 * dh == LANES:
            o_ref[q_rows, :] = _paired_heads_attention(q, sources, sink_ref, groups, dh).astype(o_ref.dtype)
            continue
        outs = []
        for kv in range(kv_heads):
            online = dh == LANES and len(sources) > 1
            chunks = []
            for k, v, msk in sources:
                vh, o0, den_col = _head_values(v, kv, dh, with_ones=online)
                chunks.append((k[:, kv * dh:(kv + 1) * dh], vh, msk))
            sink = _sink_column(sink_ref, kv, groups, rows) if has_sink else None
            qh = _stack_heads(q, kv, groups, dh)
            o = _attend(qh, chunks, sink, o0, dh, den_col) if online else _attend_two_pass(qh, chunks, sink)
            outs.extend(o[g * rows:(g + 1) * rows] for g in range(groups))
        o_ref[q_rows, :] = jnp.concatenate(outs, axis=1).astype(o_ref.dtype)


def _attention(q, k, v, ctx, sink, prev, *, n_seq, seq_len, row0, q_tile, kv_heads, groups, dh, window, seqs=1):
    t, qw = q.shape
    kw = k.shape[1]
    n_q = seq_len // q_tile
    assert row0 % (seqs * seq_len) == 0 and row0 % (seqs * q_tile) == 0 and n_seq % seqs == 0
    seq0 = row0 // (seqs * seq_len)
    q0 = row0 // (seqs * q_tile)
    in_specs, args = [], []
    if sink is not None:
        in_specs.append(pl.BlockSpec(memory_space=pltpu.SMEM))
        args.append(sink)
    in_specs += [
        pl.BlockSpec((seqs * q_tile, qw), lambda s, n: (q0 + s * n_q + n, 0)),
        pl.BlockSpec((seqs * seq_len, kw), lambda s, n: (seq0 + s, 0)),
        pl.BlockSpec((seqs * seq_len, kw), lambda s, n: (seq0 + s, 0)),
    ]
    args += [q, k, v]
    if ctx is not None:
        p = ctx[0].shape[1]
        in_specs += [pl.BlockSpec((None, p, kw), lambda s, n: (s, 0, 0))] * 2
        args += list(ctx)
    aliases = {}
    if prev is not None:
        in_specs.append(pl.BlockSpec(memory_space=pl.ANY))
        aliases = {len(args): 0}
        args.append(prev)
    kern = functools.partial(
        _attn_kernel, kv_heads=kv_heads, groups=groups, dh=dh, has_sink=sink is not None,
        has_ctx=ctx is not None, window=window, q_tile=q_tile, seq_len=seq_len, chunk=min(seq_len, ATTN_KEY_CHUNK),
        seqs=seqs)
    return pl.pallas_call(
        kern,
        out_shape=jax.ShapeDtypeStruct((t, qw), BF16),
        grid=(n_seq // seqs, n_q),
        in_specs=in_specs,
        out_specs=pl.BlockSpec((seqs * q_tile, qw), lambda s, n: (q0 + s * n_q + n, 0)),
        input_output_aliases=aliases,
        compiler_params=_cparams("arbitrary", "arbitrary"),
        name="attention",
    )(*args)


def _out_route_kernel(*refs, n_mix, n_x, n_ptiles, tiles):
    mix_refs = refs[:n_mix]
    w_refs = refs[n_mix:2 * n_mix]
    x_refs = refs[2 * n_mix:2 * n_mix + n_x]
    x_all = _stream_tile(x_refs, n_ptiles)
    for j in range(tiles):
        rows = slice(j * TOKEN_TILE, (j + 1) * TOKEN_TILE)
        _out_route_tile(j, rows, x_all[rows], [m[rows, :] for m in mix_refs], w_refs, *refs[2 * n_mix + n_x:])


def _out_route_tile(j, rows, x, mixes, w_refs, gate_ref, g2_ref, sh2_ref, sc2_ref, rw_ref, rb_ref,
                    xo_ref, h_ref, pos_ref, post_ref, gates_ref, rows_ref, off_ref):
    acc = None
    for m, w_ref in zip(mixes, w_refs):
        part = _dot(m, w_ref[...])
        acc = part if acc is None else acc + part
    xn = x + gate_ref[...] * acc
    xo_ref[rows, :] = xn
    h = _rms(xn, g2_ref[...]) * (1.0 + sc2_ref[...]) + sh2_ref[...]
    hb = h.astype(BF16)
    h_ref[rows, :] = hb

    logits = _dot_nt(rw_ref[...], hb) + rb_ref[...]
    ne, tm = logits.shape
    expert = lax.broadcasted_iota(I32, (ne, tm), 0).astype(F32)
    work = logits
    sels, vals = [], []
    for _ in range(TOP_K):
        mx = jnp.max(work, axis=0, keepdims=True)
        first = jnp.min(jnp.where(work == mx, expert, float(ne)), axis=0, keepdims=True)
        sel = expert == first
        work = jnp.where(sel, -jnp.inf, work)
        sels.append(sel)
        vals.append(mx)
    exps = [jnp.exp(v - vals[0]) for v in vals]
    den = exps[0] + exps[1] + exps[2] + exps[3]

    onehot = jnp.zeros((ne, tm), F32)
    for sel in sels:
        onehot = onehot + sel.astype(F32)
    r_i = lax.broadcasted_iota(I32, (tm, tm), 0)
    c_i = lax.broadcasted_iota(I32, (tm, tm), 1)
    before = jnp.where(r_i < c_i, 1.0, 0.0).astype(BF16)
    earlier = _dot(onehot.astype(BF16), before)

    cnt = jnp.sum(onehot, axis=1, keepdims=True)
    seg8 = jnp.floor((cnt + (SEG_ALIGN - 1.0)) * (1.0 / SEG_ALIGN))
    e_r = lax.broadcasted_iota(I32, (ne, ne), 0)
    e_c = lax.broadcasted_iota(I32, (ne, ne), 1)
    lower = jnp.where(e_c < e_r, 1.0, 0.0).astype(BF16)
    off8 = _dot(lower, jnp.broadcast_to(seg8, (ne, LANES)).astype(BF16))[:, 0:1]
    seg_off = off8 * SEG_ALIGN
    base = seg_off + earlier

    sub = lax.broadcasted_iota(I32, (2 * TOP_K, tm), 0)
    token_rows = jnp.zeros((2 * TOP_K, tm), F32)
    for k in range(TOP_K):
        pos_k = jnp.sum(jnp.where(sels[k], base, 0.0), axis=0, keepdims=True)
        token_rows = jnp.where(sub == k, pos_k, token_rows)
        token_rows = jnp.where(sub == TOP_K + k, exps[k] / den, token_rows)
    first_lane = lax.broadcasted_iota(I32, (ne, tm), 1) == 0
    block = jnp.concatenate([
        token_rows,
        jnp.where(first_lane, seg8 * SEG_ALIGN, 0.0),
        jnp.where(first_lane, seg_off, 0.0),
        jnp.zeros((LANES - 2 * TOP_K - 2 * ne, tm), F32)], axis=0)
    by_token = block.T
    post_ref[j] = token_rows.astype(I32)
    pos_ref[rows, :] = by_token[:, 0:TOP_K].astype(I32)
    gates_ref[rows, :] = by_token[:, TOP_K:2 * TOP_K]
    rows_ref[j] = by_token[0:1, 2 * TOP_K:2 * TOP_K + ne].astype(I32)
    off_ref[j] = by_token[0:1, 2 * TOP_K + ne:2 * TOP_K + 2 * ne].astype(I32)


def _out_route(geom, layer, mixes, w_parts, x, mods, norm_g, router_w, router_b):
    t, d = geom.t, w_parts[0].shape[1]
    tiles = ROUTE_TILES_PER_STEP
    tm = TOKEN_TILE * tiles
    assert geom.n_ptiles % tiles == 0 and geom.tiles_per_lat % tiles == 0
    row = lambda i: (i, 0)
    const2 = lambda i: (0, 0)
    x_specs, x_args = _stream_specs(geom, x, d, tiles)
    in_specs = [pl.BlockSpec((tm, m.shape[1]), row) for m in mixes]
    in_specs += [pl.BlockSpec(w.shape, const2) for w in w_parts]
    in_specs += x_specs
    in_specs += [
        _mod_spec(geom, layer, 2, d, tiles),
        pl.BlockSpec((1, d), const2),
        _mod_spec(geom, layer, 3, d, tiles),
        _mod_spec(geom, layer, 4, d, tiles),
        pl.BlockSpec(router_w.shape, const2),
        pl.BlockSpec((N_EXPERTS, 1), const2),
    ]
    seg3 = lambda i: (i, 0, 0)
    out_shape = [
        jax.ShapeDtypeStruct((t, d), F32),
        jax.ShapeDtypeStruct((t, d), BF16),
        jax.ShapeDtypeStruct((t, TOP_K), I32),
        jax.ShapeDtypeStruct((geom.n_tiles, 2 * TOP_K, TOKEN_TILE), I32),
        jax.ShapeDtypeStruct((t, TOP_K), F32),
        jax.ShapeDtypeStruct((geom.n_tiles, 1, N_EXPERTS), I32),
        jax.ShapeDtypeStruct((geom.n_tiles, 1, N_EXPERTS), I32),
    ]
    out_specs = [
        pl.BlockSpec((tm, d), row),
        pl.BlockSpec((tm, d), row),
        pl.BlockSpec((tm, TOP_K), row),
        pl.BlockSpec((tiles, 2 * TOP_K, TOKEN_TILE), seg3),
        pl.BlockSpec((tm, TOP_K), row),
        pl.BlockSpec((tiles, 1, N_EXPERTS), seg3),
        pl.BlockSpec((tiles, 1, N_EXPERTS), seg3),
    ]
    return pl.pallas_call(
        functools.partial(_out_route_kernel, n_mix=len(mixes), n_x=len(x_args), n_ptiles=geom.n_ptiles // tiles,
                          tiles=tiles),
        out_shape=out_shape,
        grid=(geom.n_tiles // tiles,),
        in_specs=in_specs,
        out_specs=out_specs,
        compiler_params=_cparams("arbitrary"),
        name="out_route",
    )(*mixes, *w_parts, *x_args, mods, norm_g, mods, mods, router_w, router_b)


def _pack_pairs(v):
    n = v.shape[1] // 2
    bits = lax.bitcast_convert_type(v, U32)
    return (bits[:, :n] & jnp.uint32(0xFFFF0000)) | (bits[:, n:] >> 16)


def _unpack_pairs(p):
    hi = lax.bitcast_convert_type(p & jnp.uint32(0xFFFF0000), F32)
    lo = lax.bitcast_convert_type(p << 16, F32)
    return jnp.concatenate([hi, lo], axis=1).astype(BF16)


def _planned_copies(i, plan_refs, make_copy):
    for size, width, (local_ref, slot_ref, count_ref) in zip(COPY_SIZES, PLAN_WIDTHS, plan_refs):
        def one(c, carry, size=size, width=width, local_ref=local_ref, slot_ref=slot_ref):
            a = i * width + c
            make_copy(pl.multiple_of(local_ref[a], SEG_ALIGN), pl.multiple_of(slot_ref[a], SEG_ALIGN), size).start()
            return carry

        lax.fori_loop(0, count_ref[i], one, 0)


def _copy_plan(rows, seg_off, seg_dst):
    plan = []
    experts = jnp.arange(N_EXPERTS, dtype=I32)
    for n, (size, width) in enumerate(zip(COPY_SIZES, PLAN_WIDTHS)):
        if n == 0:
            count, done = rows // size, jnp.zeros_like(rows)
        else:
            count, done = (rows % (2 * size)) // size, rows - rows % (2 * size)
        cum = jnp.cumsum(count, axis=1)
        j = jnp.arange(width, dtype=I32)
        owner = jnp.sum((cum[:, None, :] <= j[None, :, None]).astype(I32), axis=2)
        pick = (jnp.minimum(owner, N_EXPERTS - 1)[:, :, None] == experts[None, None, :]).astype(I32)
        take = lambda v: jnp.sum(pick * v[:, None, :], axis=2)
        within = (j[None, :] - take(cum - count)) * size
        plan += [(take(seg_off + done) + within).reshape(-1).astype(I32),
                 (take(seg_dst + done) + within).reshape(-1).astype(I32),
                 cum[:, -1].astype(I32)]
    return plan


def _wait_copies(n_rows, make_copy):
    def wait_big(c, carry):
        make_copy(0, 0, WAIT_CHUNK).wait()
        return carry
    lax.fori_loop(0, n_rows // WAIT_CHUNK, wait_big, 0)
    size = WAIT_CHUNK // 2
    while size >= SEG_ALIGN:
        @pl.when(n_rows % (2 * size) >= size)
        def _(size=size):
            make_copy(0, 0, size).wait()

        size //= 2


def _dispatch_kernel(*refs):
    plan_refs, (tot_ref, h_ref, pos_ref, xs_ref, sorted_ref, sem) = _split_plan(refs)
    i = pl.program_id(0)
    buf = i % 2
    tm = h_ref.shape[0]
    n_sorted = sorted_ref.shape[1]
    pos = pos_ref[0]
    slot = lax.broadcasted_iota(I32, (n_sorted, tm), 0)
    hit = jnp.zeros((n_sorted, tm), F32)
    for k in range(TOP_K):
        hit = jnp.where(pos[k:k + 1, :] == slot, 1.0, hit)
    sorted_ref[buf] = _pack_pairs(_dot(hit.astype(BF16), h_ref[...]))

    def copies_from(b):
        def make_copy(local, slot, rows):
            return pltpu.make_async_copy(sorted_ref.at[b, pl.ds(local, rows)], xs_ref.at[pl.ds(slot, rows)],
                                         sem.at[b])
        return make_copy

    _planned_copies(i, plan_refs, copies_from(buf))

    @pl.when(i > 0)
    def _():
        _wait_copies(tot_ref[jnp.maximum(i - 1, 0)], copies_from(1 - buf))

    @pl.when(i == pl.num_programs(0) - 1)
    def _():
        _wait_copies(tot_ref[i], copies_from(buf))


def _split_plan(refs):
    n = 3 * len(COPY_SIZES)
    return [refs[k:k + 3] for k in range(0, n, 3)], refs[n:]


def _dispatch(h, pos, plan, tile_rows, n_slots):
    t, d = h.shape
    tm = TOKEN_TILE
    return pl.pallas_call(
        _dispatch_kernel,
        out_shape=jax.ShapeDtypeStruct((n_slots, d // 2), U32),
        grid_spec=pltpu.PrefetchScalarGridSpec(
            num_scalar_prefetch=len(plan) + 1,
            grid=(t // tm,),
            in_specs=[
                pl.BlockSpec((tm, d), lambda i, *_: (i, 0)),
                pl.BlockSpec((1, 2 * TOP_K, tm), lambda i, *_: (i, 0, 0)),
            ],
            out_specs=pl.BlockSpec(memory_space=pl.ANY),
            scratch_shapes=[pltpu.VMEM((2, SORTED_ROWS, d // 2), U32), pltpu.SemaphoreType.DMA((2,))],
        ),
        compiler_params=_cparams("arbitrary"),
        name="moe_dispatch",
    )(*plan, tile_rows, h, pos)


def _ffn_kernel(rows_ref, start_ref, wgu_ref, bgu_ref, wd_ref, bd_ref, xs_ref, ys_ref,
                wgu_bf, wd_bf, xbuf, ybuf, sem_in, sem_out):
    e = pl.program_id(0)
    d_ff = wd_ref.shape[1]
    tb = xbuf.shape[1]
    n_rows = rows_ref[e]
    n_tiles = (n_rows + tb - 1) // tb
    base = start_ref[e]

    chunk = 128
    def cast_gu(c, carry):
        r = pl.multiple_of(c * chunk, chunk)
        wgu_bf[pl.ds(r, chunk), :] = wgu_ref[0, pl.ds(r, chunk), :].astype(BF16)
        return carry
    lax.fori_loop(0, wgu_ref.shape[1] // chunk, cast_gu, 0)
    def cast_d(c, carry):
        r = pl.multiple_of(c * chunk, chunk)
        wd_bf[pl.ds(r, chunk), :] = wd_ref[0, pl.ds(r, chunk), :].astype(BF16)
        return carry
    lax.fori_loop(0, d_ff // chunk, cast_d, 0)

    half = tb // 2
    last_small = (n_rows - (n_tiles - 1) * tb) <= half

    def x_copy(s, slot):
        r = pl.multiple_of(base + s * tb, EXPERT_TILE)
        return pltpu.make_async_copy(xs_ref.at[pl.ds(r, tb)], xbuf.at[slot], sem_in.at[slot])

    def y_copy(s, slot, rows=tb):
        r = pl.multiple_of(base + s * tb, EXPERT_TILE)
        return pltpu.make_async_copy(ybuf.at[slot, pl.ds(0, rows)], ys_ref.at[pl.ds(r, rows)], sem_out.at[slot])

    def ffn_rows(s, slot, n):
        rows = s * tb + lax.broadcasted_iota(I32, (n, 1), 0)
        x = _unpack_pairs(jnp.where(rows < n_rows, xbuf[slot, pl.ds(0, n), :], jnp.uint32(0)))
        gu = _dot(x, wgu_bf[...]) + bgu_ref[0]
        gate = jnp.minimum(gu[:, :d_ff], SWIGLU_LIMIT)
        up = jnp.clip(gu[:, d_ff:], -SWIGLU_LIMIT, SWIGLU_LIMIT)
        act = (up + 1.0) * (gate * (1.0 / (1.0 + jnp.exp(-SWIGLU_ALPHA * gate))))
        y = _dot(act.astype(BF16), wd_bf[...]) + bd_ref[0]
        ybuf[slot, pl.ds(0, n), :] = _pack_pairs(y.astype(BF16).astype(F32))
        y_copy(s, slot, n).start()

    @pl.when(n_tiles > 0)
    def _():
        x_copy(0, 0).start()

    def tile(s, carry):
        slot = s % 2
        x_copy(s, slot).wait()

        @pl.when(s + 1 < n_tiles)
        def _():
            x_copy(s + 1, 1 - slot).start()

        @pl.when(s >= 2)
        def _():
            y_copy(s - 2, slot).wait()

        small = jnp.logical_and(s == n_tiles - 1, last_small)

        @pl.when(jnp.logical_not(small))
        def _():
            ffn_rows(s, slot, tb)

        @pl.when(small)
        def _():
            ffn_rows(s, slot, half)

        return carry

    lax.fori_loop(0, n_tiles, tile, 0)

    @pl.when(n_tiles >= 2)
    def _():
        y_copy(n_tiles - 2, n_tiles % 2).wait()

    @pl.when(jnp.logical_and(n_tiles >= 1, jnp.logical_not(last_small)))
    def _():
        y_copy(n_tiles - 1, (n_tiles - 1) % 2).wait()

    @pl.when(jnp.logical_and(n_tiles >= 1, last_small))
    def _():
        y_copy(n_tiles - 1, (n_tiles - 1) % 2, half).wait()


def _expert_ffn(layer, xs, expert_rows, expert_start, w_gu, b_gu, w_down, b_down):
    n_slots, packed_w = xs.shape
    tb = FFN_TILE
    depth, ne, d, two_f = w_gu.shape
    d_ff = two_f // 2
    exp4 = lambda e, *_: (layer, e, 0, 0)
    return pl.pallas_call(
        _ffn_kernel,
        out_shape=jax.ShapeDtypeStruct((n_slots, packed_w), U32),
        grid_spec=pltpu.PrefetchScalarGridSpec(
            num_scalar_prefetch=2,
            grid=(ne,),
            in_specs=[
                pl.BlockSpec((None, 1, d, two_f), exp4),
                pl.BlockSpec((None, 1, 1, two_f), exp4),
                pl.BlockSpec((None, 1, d_ff, d), exp4),
                pl.BlockSpec((None, 1, 1, d), exp4),
                pl.BlockSpec(memory_space=pl.ANY),
            ],
            out_specs=pl.BlockSpec(memory_space=pl.ANY),
            scratch_shapes=[
                pltpu.VMEM((d, two_f), BF16), pltpu.VMEM((d_ff, d), BF16),
                pltpu.VMEM((2, tb, packed_w), U32), pltpu.VMEM((2, tb, packed_w), U32),
                pltpu.SemaphoreType.DMA((2,)), pltpu.SemaphoreType.DMA((2,)),
            ],
        ),
        compiler_params=_cparams("arbitrary"),
        name="expert_ffn",
    )(expert_rows, expert_start, w_gu, b_gu.reshape(depth, ne, 1, two_f),
      w_down, b_down.reshape(depth, ne, 1, d), xs)


def _combine_kernel(*refs, final, n_ptiles):
    plan_refs, (tot_ref, x_ref, pos_ref, gates_ref, mg_ref, *rest) = _split_plan(refs)
    if final:
        fn_ref, ys_ref, op_ref, os_ref, buf, sem = rest
    else:
        ys_ref, o_ref, buf, sem = rest
    i = pl.program_id(0)
    cur = i % 2
    tm = x_ref.shape[0]
    n_sorted = buf.shape[1]

    def copies_into(b):
        def make_copy(local, slot, rows):
            return pltpu.make_async_copy(ys_ref.at[pl.ds(slot, rows)], buf.at[b, pl.ds(local, rows)], sem.at[b])
        return make_copy

    @pl.when(i == 0)
    def _():
        buf[...] = jnp.zeros_like(buf)
        _planned_copies(i, plan_refs, copies_into(cur))

    @pl.when(i + 1 < pl.num_programs(0))
    def _():
        _planned_copies(i + 1, plan_refs, copies_into(1 - cur))

    pos = pos_ref[...]
    g = gates_ref[...]
    lane = lax.broadcasted_iota(I32, (tm, n_sorted), 1)
    weight = jnp.zeros((tm, n_sorted), F32)
    for k in range(TOP_K):
        weight = jnp.where(pos[:, k:k + 1] == lane, g[:, k:k + 1], weight)
    _wait_copies(tot_ref[i], copies_into(cur))
    y = _dot(weight.astype(BF16), _unpack_pairs(buf[cur]))
    xn = x_ref[...] + mg_ref[...] * y
    if final:
        xn = _rms(xn, fn_ref[...])

        @pl.when(i < n_ptiles)
        def _():
            op_ref[...] = xn

        @pl.when(i >= n_ptiles)
        def _():
            os_ref[...] = xn
    else:
        o_ref[...] = xn


def _combine(geom, layer, ys, pos, plan, tile_rows, x, gates, mods, final_g):
    t, d = x.shape
    tm = TOKEN_TILE
    final = final_g is not None

    def mod_imap(i, *_):
        return ((layer * MOD_ROWS + geom.group(i)) * N_MOD + 5, 0, 0)

    row = lambda i, *_: (i, 0)
    in_specs = [
        pl.BlockSpec((tm, d), row),
        pl.BlockSpec((tm, TOP_K), row),
        pl.BlockSpec((tm, TOP_K), row),
        pl.BlockSpec((None, 1, d), mod_imap),
    ]
    args = [*plan, tile_rows, x, pos, gates, mods]
    if final:
        in_specs.append(pl.BlockSpec((1, d), lambda i, *_: (0, 0)))
        args.append(final_g)
    in_specs.append(pl.BlockSpec(memory_space=pl.ANY))
    args.append(ys)
    if final:
        n_pt = geom.n_ptiles
        out_shape = [jax.ShapeDtypeStruct((geom.tp, d), F32), jax.ShapeDtypeStruct((t - geom.tp, d), F32)]
        out_specs = [pl.BlockSpec((tm, d), lambda i, *_: (jnp.minimum(i, n_pt - 1), 0)),
                     pl.BlockSpec((tm, d), lambda i, *_: (jnp.maximum(i - n_pt, 0), 0))]
    else:
        out_shape = jax.ShapeDtypeStruct((t, d), F32)
        out_specs = pl.BlockSpec((tm, d), row)
    return pl.pallas_call(
        functools.partial(_combine_kernel, final=final, n_ptiles=geom.n_ptiles),
        out_shape=out_shape,
        grid_spec=pltpu.PrefetchScalarGridSpec(
            num_scalar_prefetch=len(plan) + 1,
            grid=(t // tm,),
            in_specs=in_specs,
            out_specs=out_specs,
            scratch_shapes=[pltpu.VMEM((2, SORTED_ROWS, d // 2), U32), pltpu.SemaphoreType.DMA((2,))],
        ),
        compiler_params=_cparams("arbitrary"),
        name="moe_combine",
    )(*args)


def _moe(geom, layer, h, pos, pos_t, gates, seg_rows, seg_off, x, mods, w_gu, b_gu, w_down, b_down, final_g):
    t = h.shape[0]
    tb = EXPERT_TILE
    n_tok_tiles = seg_rows.shape[0]
    max_rows = t * TOP_K + n_tok_tiles * N_EXPERTS * (SEG_ALIGN - 1)
    n_blocks = -(-max_rows // tb) + N_EXPERTS
    rows = seg_rows[:, 0, :]
    cnt = jnp.sum(rows, axis=0)
    n_tiles_e = (cnt + tb - 1) // tb
    tile_end = jnp.cumsum(n_tiles_e)
    tile_start = tile_end - n_tiles_e
    expert_start = (tile_start * tb).astype(I32)
    seg_dst = expert_start[None, :] + jnp.cumsum(rows, axis=0) - rows
    plan = _copy_plan(rows, seg_off[:, 0, :], seg_dst)
    tile_rows = jnp.sum(rows, axis=1).astype(I32)
    xs = _dispatch(h, pos_t, plan, tile_rows, n_blocks * tb + FFN_TILE - tb)
    ys = _expert_ffn(layer, xs, cnt.astype(I32), expert_start, w_gu, b_gu, w_down, b_down)
    return _combine(geom, layer, ys, pos, plan, tile_rows, x, gates, mods, final_g)


def kernel(x_prompt, x_sample, cache_b_k, cache_b_v, cache_c_k, cache_c_v, c, c_ctx,
           mod_w, mod_b, norm_mix, norm_ffn, even_w_in, even_w_out, even_sink,
           odd_w_in, odd_w_out, odd_q_norm, odd_k_norm, router_w, router_b,
           moe_w_gu, moe_b_gu, moe_w_down, moe_b_down, final_norm):
    bp, lp, d = x_prompt.shape
    bs, ls, _ = x_sample.shape
    past = cache_b_k.shape[2]
    depth = mod_w.shape[0]
    geom = _Geom(bp, lp, bs, ls)
    tp = geom.tp

    x = (x_prompt.reshape(tp, d), x_sample.reshape(bs * ls, d))
    cond = jnp.concatenate([c_ctx[None, :], c, jnp.zeros((MOD_ROWS - 1 - bs, d), F32)], axis=0)
    mods = _modulation(cond, mod_w, mod_b).reshape(depth * MOD_ROWS * N_MOD, 1, d)

    cn, sn = _dft_tables(A_GROUP_DIM)
    dft_chan = jnp.asarray(np.concatenate([cn, sn], axis=1), BF16)
    dft_p = [jnp.asarray(m, BF16) for m in _dft_tables(lp)]
    dft_s = [jnp.asarray(m, BF16) for m in _dft_tables(ls)]
    rope_b = [jnp.asarray(m) for m in _rope_tables(TOKEN_TILE, ls, B_HEAD_DIM)]
    rope_c = [jnp.asarray(m) for m in _rope_tables(TOKEN_TILE, ls, C_HEAD_DIM)]

    states = {"bk": [], "bv": [], "ck": [], "cv": []}
    for layer in range(depth):
        j = layer // 2
        g_mix = norm_mix[layer][None, :]
        g_ffn = norm_ffn[layer][None, :]
        if layer % 2 == 0:
            pair = _paired_head_order(B_KV_HEADS, B_HEADS // B_KV_HEADS, B_HEAD_DIM)
            w_in = even_w_in[j]
            w_in = jnp.concatenate([w_in[:, :A_WIDTH], w_in[:, A_WIDTH:A_WIDTH + B_Q_WIDTH][:, pair],
                                    w_in[:, A_WIDTH + B_Q_WIDTH:]], axis=1).astype(BF16)
            tc, ts, q, k, v, k_state, v_state = _in_projection(
                geom, layer, x, g_mix, mods, w_in, rope_b[0], rope_b[1], [dft_chan],
                _proj_even_kernel, (A_WIDTH, A_WIDTH, B_Q_WIDTH, B_KV_WIDTH, B_KV_WIDTH), B_KV_WIDTH, "proj_even")
            states["bk"].append(k_state.reshape(bp, lp, B_KV_HEADS, B_HEAD_DIM))
            states["bv"].append(v_state.reshape(bp, lp, B_KV_HEADS, B_HEAD_DIM))
            four = _fourier_tokens(tc, ts, dft_p[0], dft_p[1], bp, lp, 0, None)
            four = _fourier_tokens(tc, ts, dft_s[0], dft_s[1], bs, ls, tp, four)
            sink = even_sink[j]
            common = dict(kv_heads=B_KV_HEADS, groups=B_HEADS // B_KV_HEADS, dh=B_HEAD_DIM)
            att = _attention(q, k, v, None, sink, None, n_seq=bp, seq_len=lp, row0=0, q_tile=lp,
                             window=None, seqs=PROMPT_SEQS_PER_STEP, **common)
            ctx = (cache_b_k[:, j].reshape(bs, past, B_KV_WIDTH).astype(BF16),
                   cache_b_v[:, j].reshape(bs, past, B_KV_WIDTH).astype(BF16))
            att = _attention(q, k, v, ctx, sink, att, n_seq=bs, seq_len=ls, row0=tp, q_tile=ATTN_Q_TILE,
                             window=WINDOW, **common)
            w_out = even_w_out[j].astype(BF16)
            mixes = [four, att]
            w_parts = [w_out[:A_WIDTH], w_out[A_WIDTH:][pair]]
        else:
            q, k, v, k_state, v_state = _in_projection(
                geom, layer, x, g_mix, mods, odd_w_in[j].astype(BF16), rope_c[0], rope_c[1],
                [odd_q_norm[j][None, :], odd_k_norm[j][None, :]],
                _proj_odd_kernel, (C_Q_WIDTH, C_KV_WIDTH, C_KV_WIDTH), C_KV_WIDTH, "proj_odd")
            states["ck"].append(k_state.reshape(bp, lp, C_KV_HEADS, C_HEAD_DIM))
            states["cv"].append(v_state.reshape(bp, lp, C_KV_HEADS, C_HEAD_DIM))
            common = dict(kv_heads=C_KV_HEADS, groups=C_HEADS // C_KV_HEADS, dh=C_HEAD_DIM, window=None)
            att = _attention(q, k, v, None, None, None, n_seq=bp, seq_len=lp, row0=0, q_tile=lp,
                             seqs=PROMPT_SEQS_PER_STEP, **common)
            ctx = (cache_c_k[:, j].reshape(bs, past, C_KV_WIDTH).astype(BF16),
                   cache_c_v[:, j].reshape(bs, past, C_KV_WIDTH).astype(BF16))
            att = _attention(q, k, v, ctx, None, att, n_seq=bs, seq_len=ls, row0=tp, q_tile=DENSE_Q_TILE, **common)
            mixes = [att]
            w_parts = [odd_w_out[j].astype(BF16)]
        x, h, pos, pos_t, gates, seg_rows, seg_off = _out_route(
            geom, layer, mixes, w_parts, x, mods, g_ffn, router_w[layer].T.astype(BF16), router_b[layer][:, None])
        final_g = final_norm[None, :] if layer == depth - 1 else None
        x = _moe(geom, layer, h, pos, pos_t, gates, seg_rows, seg_off, x, mods,
                 moe_w_gu, moe_b_gu, moe_w_down, moe_b_down, final_g)

    y_prompt = x[0].reshape(bp, lp, d)
    y_sample = x[1].reshape(bs, ls, d)
    return (y_prompt, y_sample,
            jnp.stack(states["bk"], axis=1), jnp.stack(states["bv"], axis=1),
            jnp.stack(states["ck"], axis=1), jnp.stack(states["cv"], axis=1))
```

```python
import functools

import numpy as np
import jax
import jax.numpy as jnp
from jax import lax
from jax.experimental import pallas as pl
from jax.experimental.pallas import tpu as pltpu

F32 = jnp.float32
BF16 = jnp.bfloat16
I32 = jnp.int32
U32 = jnp.uint32

GRID_W = 64
A_GROUPS = 4
A_GROUP_DIM = 128
A_WIDTH = A_GROUPS * A_GROUP_DIM
B_HEADS = 8
B_KV_HEADS = 2
B_HEAD_DIM = 64
B_Q_WIDTH = B_HEADS * B_HEAD_DIM
B_KV_WIDTH = B_KV_HEADS * B_HEAD_DIM
WINDOW = 128
C_HEADS = 8
C_KV_HEADS = 2
C_HEAD_DIM = 128
C_Q_WIDTH = C_HEADS * C_HEAD_DIM
C_KV_WIDTH = C_KV_HEADS * C_HEAD_DIM
ROPE_THETA = 10000.0
N_EXPERTS = 32
TOP_K = 4
SWIGLU_LIMIT = 7.0
SWIGLU_ALPHA = 1.702
EPS = 1e-6

LANES = 128
SUBLANES = 8
TOKEN_TILE = 256
ROUTE_TILES_PER_STEP = 4
EXPERT_TILE = 128
FFN_TILE = 256
ATTN_Q_TILE = 128
DENSE_Q_TILE = 256
ATTN_KEY_CHUNK = 1024
PROMPT_SEQS_PER_STEP = 4
SEG_ALIGN = SUBLANES
COPY_SIZES = (32, 16, 8)
WAIT_CHUNK = 256
SORTED_ROWS = -(-(TOKEN_TILE * TOP_K + N_EXPERTS * (SEG_ALIGN - 1)) // LANES) * LANES
PLAN_WIDTHS = (SORTED_ROWS // COPY_SIZES[0],) + (N_EXPERTS,) * (len(COPY_SIZES) - 1)
VMEM_LIMIT = 56 * 1024 * 1024
MASKED = -1e30
N_MOD = 6
MOD_ROWS = SUBLANES


def _cparams(*sem):
    return pltpu.CompilerParams(dimension_semantics=tuple(sem), vmem_limit_bytes=VMEM_LIMIT)


def _dot(a, b):
    return jnp.dot(a, b, preferred_element_type=F32)


def _dot_nt(a, b):
    return lax.dot_general(a, b, (((1,), (1,)), ((), ())), preferred_element_type=F32)


def _rms(x, g):
    return x * lax.rsqrt(jnp.mean(x * x, axis=-1, keepdims=True) + EPS) * g


def _dft_tables(n):
    j = np.arange(n, dtype=np.int64)
    ang = 2.0 * np.pi * ((j[:, None] * j[None, :]) % n).astype(np.float64) / n
    s = 1.0 / np.sqrt(n)
    return np.cos(ang) * s, np.sin(ang) * s


def _paired_head_order(kv_heads, groups, dh):
    assert kv_heads * dh == LANES
    return np.array([(kv * groups + g) * dh + d for g in range(groups) for kv in range(kv_heads) for d in range(dh)])


def _rope_tables(n_prompt_rows, n_latent, head_dim):
    quarter = head_dim // 4
    pos = np.arange(n_latent)
    row = (pos // GRID_W).astype(np.float32)
    col = (pos % GRID_W).astype(np.float32)
    inv = (np.float32(ROPE_THETA) ** (-np.arange(quarter, dtype=np.float32) / np.float32(quarter))).astype(np.float32)
    ang_row = (row[:, None] * inv[None, :]).astype(np.float32)
    ang_col = (col[:, None] * inv[None, :]).astype(np.float32)
    cos_h = np.concatenate([np.cos(ang_row)] * 2 + [np.cos(ang_col)] * 2, axis=1)
    sin_h = np.concatenate([-np.sin(ang_row), np.sin(ang_row), -np.sin(ang_col), np.sin(ang_col)], axis=1)
    reps = LANES // head_dim
    cos_l = np.tile(cos_h, (1, reps)).astype(np.float32)
    sin_l = np.tile(sin_h, (1, reps)).astype(np.float32)
    cos = np.concatenate([np.ones((n_prompt_rows, LANES), np.float32), cos_l], axis=0)
    sin = np.concatenate([np.zeros((n_prompt_rows, LANES), np.float32), sin_l], axis=0)
    return cos, sin


def _rope(x, cos, sin, quarter):
    lane = lax.broadcasted_iota(I32, (x.shape[0], LANES), 1)
    first = ((lane // quarter) % 2) == 0
    outs = []
    for c in range(x.shape[1] // LANES):
        xc = x[:, c * LANES:(c + 1) * LANES]
        partner = jnp.where(first, pltpu.roll(xc, LANES - quarter, 1), pltpu.roll(xc, quarter, 1))
        outs.append(xc * cos + partner * sin)
    return outs[0] if len(outs) == 1 else jnp.concatenate(outs, axis=1)


def _head_rms(x, g):
    outs = []
    for c in range(x.shape[1] // LANES):
        outs.append(_rms(x[:, c * LANES:(c + 1) * LANES], g))
    return outs[0] if len(outs) == 1 else jnp.concatenate(outs, axis=1)


def _mod_kernel(c_ref, w_ref, b_ref, o_ref):
    c = c_ref[...]
    s = c * (1.0 / (1.0 + jnp.exp(-c)))
    o_ref[0] = _dot(s.astype(BF16), w_ref[0].astype(BF16)) + b_ref[0]


def _modulation(cond, mod_w, mod_b):
    depth, d, n = mod_w.shape
    tn = 1536
    return pl.pallas_call(
        _mod_kernel,
        out_shape=jax.ShapeDtypeStruct((depth, MOD_ROWS, n), F32),
        grid=(depth, n // tn),
        in_specs=[
            pl.BlockSpec((MOD_ROWS, d), lambda l, j: (0, 0)),
            pl.BlockSpec((1, d, tn), lambda l, j: (l, 0, j)),
            pl.BlockSpec((1, 1, tn), lambda l, j: (l, 0, j)),
        ],
        out_specs=pl.BlockSpec((1, MOD_ROWS, tn), lambda l, j: (l, 0, j)),
        compiler_params=_cparams("arbitrary", "arbitrary"),
        name="modulation",
    )(cond, mod_w, mod_b.reshape(depth, 1, n))


class _Geom:
    def __init__(self, bp, lp, bs, ls):
        self.bp, self.lp, self.bs, self.ls = bp, lp, bs, ls
        self.tp = bp * lp
        self.t = bp * lp + bs * ls
        assert lp == TOKEN_TILE and ls % TOKEN_TILE == 0 and self.tp % ls == 0
        self.n_ptiles = self.tp // TOKEN_TILE
        self.tiles_per_lat = ls // TOKEN_TILE
        self.n_tiles = self.t // TOKEN_TILE

    def group(self, i):
        return jnp.where(i < self.n_ptiles, 0, 1 + (i - self.n_ptiles) // self.tiles_per_lat)

    def pos_block(self, i):
        return jnp.where(i < self.n_ptiles, 0, 1 + (i - self.n_ptiles) % self.tiles_per_lat)


def _mod_spec(geom, layer, which, d, tiles=1):
    def imap(i):
        return ((layer * MOD_ROWS + geom.group(i * tiles)) * N_MOD + which, 0, 0)
    return pl.BlockSpec((None, 1, d), imap)


def _stream_specs(geom, x, d, tiles=1):
    tm = TOKEN_TILE * tiles
    n_p = geom.n_ptiles // tiles
    if isinstance(x, tuple):
        return ([pl.BlockSpec((tm, d), lambda i, *_: (jnp.minimum(i, n_p - 1), 0)),
                 pl.BlockSpec((tm, d), lambda i, *_: (jnp.maximum(i - n_p, 0), 0))], list(x))
    return [pl.BlockSpec((tm, d), lambda i, *_: (i, 0))], [x]


def _stream_tile(x_refs, n_ptiles):
    if len(x_refs) == 1:
        return x_refs[0][...]
    return jnp.where(pl.program_id(0) < n_ptiles, x_refs[0][...], x_refs[1][...])


def _store_kv(k, v, kb_ref, vb_ref, ks_ref, vs_ref, n_ptiles):
    kb_ref[...] = k.astype(BF16)
    vb_ref[...] = v.astype(BF16)

    @pl.when(pl.program_id(0) < n_ptiles)
    def _():
        ks_ref[...] = k
        vs_ref[...] = v


def _proj_even_kernel(*refs, n_x, n_ptiles):
    x_refs = refs[:n_x]
    (g_ref, sh_ref, sc_ref, w_ref, cos_ref, sin_ref, dft_ref,
     tc_ref, ts_ref, q_ref, kb_ref, vb_ref, ks_ref, vs_ref) = refs[n_x:]
    h = _rms(_stream_tile(x_refs, n_ptiles), g_ref[...]) * (1.0 + sc_ref[...]) + sh_ref[...]
    p = _dot(h.astype(BF16), w_ref[...])
    cos = cos_ref[...]
    sin = sin_ref[...]
    dft = dft_ref[...]
    tcs, tss = [], []
    for g in range(A_GROUPS):
        t = _dot(p[:, g * A_GROUP_DIM:(g + 1) * A_GROUP_DIM].astype(BF16), dft)
        tcs.append(t[:, :A_GROUP_DIM])
        tss.append(t[:, A_GROUP_DIM:])
    tc_ref[...] = jnp.concatenate(tcs, axis=1).astype(BF16)
    ts_ref[...] = jnp.concatenate(tss, axis=1).astype(BF16)
    o = A_WIDTH
    q = _rope(p[:, o:o + B_Q_WIDTH], cos, sin, B_HEAD_DIM // 4)
    q_ref[...] = (q * B_HEAD_DIM ** -0.5).astype(BF16)
    o += B_Q_WIDTH
    k = _rope(p[:, o:o + B_KV_WIDTH], cos, sin, B_HEAD_DIM // 4)
    o += B_KV_WIDTH
    _store_kv(k, p[:, o:o + B_KV_WIDTH], kb_ref, vb_ref, ks_ref, vs_ref, n_ptiles)


def _proj_odd_kernel(*refs, n_x, n_ptiles):
    x_refs = refs[:n_x]
    (g_ref, sh_ref, sc_ref, w_ref, cos_ref, sin_ref, qn_ref, kn_ref,
     q_ref, kb_ref, vb_ref, ks_ref, vs_ref) = refs[n_x:]
    h = _rms(_stream_tile(x_refs, n_ptiles), g_ref[...]) * (1.0 + sc_ref[...]) + sh_ref[...]
    p = _dot(h.astype(BF16), w_ref[...])
    cos = cos_ref[...]
    sin = sin_ref[...]
    q = _head_rms(p[:, :C_Q_WIDTH], qn_ref[...])
    k = _head_rms(p[:, C_Q_WIDTH:C_Q_WIDTH + C_KV_WIDTH], kn_ref[...])
    q_ref[...] = (_rope(q, cos, sin, C_HEAD_DIM // 4) * C_HEAD_DIM ** -0.5).astype(BF16)
    k = _rope(k, cos, sin, C_HEAD_DIM // 4)
    _store_kv(k, p[:, C_Q_WIDTH + C_KV_WIDTH:], kb_ref, vb_ref, ks_ref, vs_ref, n_ptiles)


def _in_projection(geom, layer, x, norm_g, mods, w, cos, sin, extras, kernel, out_widths, kv_width, name):
    t, d = geom.t, w.shape[0]
    tm = TOKEN_TILE
    n_out = w.shape[1]
    row = lambda i: (i, 0)
    const2 = lambda i: (0, 0)
    x_specs, x_args = _stream_specs(geom, x, d)
    in_specs = x_specs + [
        pl.BlockSpec((1, d), const2),
        _mod_spec(geom, layer, 0, d),
        _mod_spec(geom, layer, 1, d),
        pl.BlockSpec((d, n_out), const2),
        pl.BlockSpec((tm, LANES), lambda i: (geom.pos_block(i), 0)),
        pl.BlockSpec((tm, LANES), lambda i: (geom.pos_block(i), 0)),
    ] + [pl.BlockSpec(e.shape, const2) for e in extras]
    return pl.pallas_call(
        functools.partial(kernel, n_x=len(x_args), n_ptiles=geom.n_ptiles),
        out_shape=([jax.ShapeDtypeStruct((t, wd), BF16) for wd in out_widths]
                   + [jax.ShapeDtypeStruct((geom.tp, kv_width), F32)] * 2),
        grid=(geom.n_tiles,),
        in_specs=in_specs,
        out_specs=([pl.BlockSpec((tm, wd), row) for wd in out_widths]
                   + [pl.BlockSpec((tm, kv_width), lambda i: (jnp.minimum(i, geom.n_ptiles - 1), 0))] * 2),
        compiler_params=_cparams("arbitrary"),
        name=name,
    )(*x_args, norm_g, mods, mods, w, cos, sin, *extras)


def _fourier_kernel(cl_ref, sl_ref, tc_ref, ts_ref, *rest):
    o_ref = rest[-1]
    o_ref[...] = (_dot(cl_ref[...], tc_ref[...]) - _dot(sl_ref[...], ts_ref[...])).astype(o_ref.dtype)


def _fourier_tokens(tc, ts, cl, sl, n_seq, seq_len, row0, prev):
    t, width = tc.shape
    tr = min(seq_len, 512)
    n_r = seq_len // tr
    assert row0 % seq_len == 0
    seq0 = row0 // seq_len
    out0 = row0 // tr
    in_specs = [
        pl.BlockSpec((tr, seq_len), lambda s, r: (r, 0)),
        pl.BlockSpec((tr, seq_len), lambda s, r: (r, 0)),
        pl.BlockSpec((seq_len, width), lambda s, r: (seq0 + s, 0)),
        pl.BlockSpec((seq_len, width), lambda s, r: (seq0 + s, 0)),
    ]
    args = [cl, sl, tc, ts]
    aliases = {}
    if prev is not None:
        in_specs.append(pl.BlockSpec(memory_space=pl.ANY))
        args.append(prev)
        aliases = {4: 0}
    return pl.pallas_call(
        _fourier_kernel,
        out_shape=jax.ShapeDtypeStruct((t, width), BF16),
        grid=(n_seq, n_r),
        in_specs=in_specs,
        out_specs=pl.BlockSpec((tr, width), lambda s, r: (out0 + s * n_r + r, 0)),
        input_output_aliases=aliases,
        compiler_params=_cparams("arbitrary", "arbitrary"),
        name="fourier_tokens",
    )(*args)


def _attend(q, chunks, sink, o0, dh, den_col):
    m = sink
    acc = None
    den = None
    for k, v, mask in chunks:
        s = _dot_nt(q, k)
        if mask is not None:
            s = jnp.where(mask, s, MASKED)
        m_new = jnp.max(s, axis=-1, keepdims=True)
        if m is not None:
            m_new = jnp.maximum(m, m_new)
        p = jnp.exp(s - m_new)
        pv = _dot(p.astype(BF16), v)
        if acc is None:
            acc = pv
            if den_col is None:
                den = jnp.sum(p, axis=-1, keepdims=True)
        else:
            alpha = jnp.exp(m - m_new)
            acc = alpha * acc + pv
            if den_col is None:
                den = alpha * den + jnp.sum(p, axis=-1, keepdims=True)
        m = m_new
    if den_col is not None:
        den = acc[:, den_col:den_col + 1]
    if sink is not None:
        den = den + jnp.exp(sink - m)
    return acc[:, o0:o0 + dh] / den


def _attend_two_pass(q, chunks, sink):
    scores = []
    m = sink
    for k, _, mask in chunks:
        s = _dot_nt(q, k)
        if mask is not None:
            s = jnp.where(mask, s, MASKED)
        scores.append(s)
        mx = jnp.max(s, axis=-1, keepdims=True)
        m = mx if m is None else jnp.maximum(m, mx)
    den = None if sink is None else jnp.exp(sink - m)
    acc = None
    for (_, v, _), s in zip(chunks, scores):
        e = jnp.exp(s - m)
        es = jnp.sum(e, axis=-1, keepdims=True)
        den = es if den is None else den + es
        o = _dot(e.astype(BF16), v)
        acc = o if acc is None else acc + o
    return acc / den


def _stack_heads(q, kv, groups, dh):
    return jnp.concatenate([q[:, (kv * groups + g) * dh:(kv * groups + g + 1) * dh] for g in range(groups)], axis=0)


def _sink_column(sink_ref, kv, groups, rows):
    return jnp.concatenate([jnp.full((rows, 1), sink_ref[kv * groups + g], F32) for g in range(groups)], axis=0)


def _head_values(v, kv, dh, with_ones):
    if not with_ones:
        return v[:, kv * dh:(kv + 1) * dh], 0, None
    assert dh == LANES
    lane = lax.broadcasted_iota(I32, (v.shape[0], LANES), 1)
    ones = jnp.where(lane == 0, 1.0, 0.0).astype(BF16)
    return jnp.concatenate([v[:, kv * dh:(kv + 1) * dh], ones], axis=1), 0, dh


def _paired_heads_attention(q, sources, sink_ref, groups, dh):
    rows = q.shape[0]
    half_q = lax.broadcasted_iota(I32, (rows, LANES), 1) // dh
    per_head = []
    for kv in range(2):
        qh = jnp.concatenate(
            [jnp.where(half_q == kv, q[:, g * LANES:(g + 1) * LANES], jnp.zeros((), q.dtype)) for g in range(groups)],
            axis=0)
        chunks = []
        for k, v, msk in sources:
            half_v = lax.broadcasted_iota(I32, v.shape, 1) // dh
            chunks.append((k, jnp.where(half_v == kv, v, jnp.zeros((), v.dtype)), msk))
        sink = _sink_column(sink_ref, kv, groups, rows) if sink_ref is not None else None
        per_head.append(_attend_two_pass(qh, chunks, sink))
    return jnp.concatenate(
        [per_head[0][g * rows:(g + 1) * rows] + per_head[1][g * rows:(g + 1) * rows] for g in range(groups)], axis=1)


def _attn_kernel(*refs, kv_heads, groups, dh, has_sink, has_ctx, window, q_tile, seq_len, chunk, seqs):
    refs = list(refs)
    sink_ref = refs.pop(0) if has_sink else None
    q_ref, k_ref, v_ref = refs[:3]
    ck_ref, cv_ref = (refs[3], refs[4]) if has_ctx else (None, None)
    o_ref = refs[-1]
    rows = q_ref.shape[0] // seqs
    assert seqs == 1 or (window is None and not has_ctx and q_tile == seq_len)
    for j in range(seqs):
        q_rows = slice(j * rows, (j + 1) * rows)
        q = q_ref[q_rows, :]
        if window is None:
            spans = [(j * seq_len + c * chunk, chunk) for c in range(seq_len // chunk)]
            mask = None
        else:
            n = pl.program_id(1)
            band = q_tile + 2 * window
            start = pl.multiple_of(jnp.clip(n * q_tile - window, 0, seq_len - band), LANES)
            spans = [(start, band)]
            qpos = n * q_tile + lax.broadcasted_iota(I32, (groups * rows, band), 0) % rows
            kpos = start + lax.broadcasted_iota(I32, (groups * rows, band), 1)
            mask = jnp.abs(kpos - qpos) <= window
        sources = [(k_ref[pl.ds(s0, n_s), :], v_ref[pl.ds(s0, n_s), :], mask) for s0, n_s in spans]
        if has_ctx:
            sources.append((ck_ref[...], cv_ref[...], None))
        if 2 * dh == LANES:
            o_ref[q_rows, :] = _paired_heads_attention(q, sources, sink_ref, groups, dh).astype(o_ref.dtype)
            continue
        outs = []
        for kv in range(kv_heads):
            online = dh == LANES and len(sources) > 1
            chunks = []
            for k, v, msk in sources:
                vh, o0, den_col = _head_values(v, kv, dh, with_ones=online)
                chunks.append((k[:, kv * dh:(kv + 1) * dh], vh, msk))
            sink = _sink_column(sink_ref, kv, groups, rows) if has_sink else None
            qh = _stack_heads(q, kv, groups, dh)
            o = _attend(qh, chunks, sink, o0, dh, den_col) if online else _attend_two_pass(qh, chunks, sink)
            outs.extend(o[g * rows:(g + 1) * rows] for g in range(groups))
        o_ref[q_rows, :] = jnp.concatenate(outs, axis=1).astype(o_ref.dtype)


def _attention(q, k, v, ctx, sink, prev, *, n_seq, seq_len, row0, q_tile, kv_heads, groups, dh, window, seqs=1):
    t, qw = q.shape
    kw = k.shape[1]
    n_q = seq_len // q_tile
    assert row0 % (seqs * seq_len) == 0 and row0 % (seqs * q_tile) == 0 and n_seq % seqs == 0
    seq0 = row0 // (seqs * seq_len)
    q0 = row0 // (seqs * q_tile)
    in_specs, args = [], []
    if sink is not None:
        in_specs.append(pl.BlockSpec(memory_space=pltpu.SMEM))
        args.append(sink)
    in_specs += [
        pl.BlockSpec((seqs * q_tile, qw), lambda s, n: (q0 + s * n_q + n, 0)),
        pl.BlockSpec((seqs * seq_len, kw), lambda s, n: (seq0 + s, 0)),
        pl.BlockSpec((seqs * seq_len, kw), lambda s, n: (seq0 + s, 0)),
    ]
    args += [q, k, v]
    if ctx is not None:
        p = ctx[0].shape[1]
        in_specs += [pl.BlockSpec((None, p, kw), lambda s, n: (s, 0, 0))] * 2
        args += list(ctx)
    aliases = {}
    if prev is not None:
        in_specs.append(pl.BlockSpec(memory_space=pl.ANY))
        aliases = {len(args): 0}
        args.append(prev)
    kern = functools.partial(
        _attn_kernel, kv_heads=kv_heads, groups=groups, dh=dh, has_sink=sink is not None,
        has_ctx=ctx is not None, window=window, q_tile=q_tile, seq_len=seq_len, chunk=min(seq_len, ATTN_KEY_CHUNK),
        seqs=seqs)
    return pl.pallas_call(
        kern,
        out_shape=jax.ShapeDtypeStruct((t, qw), BF16),
        grid=(n_seq // seqs, n_q),
        in_specs=in_specs,
        out_specs=pl.BlockSpec((seqs * q_tile, qw), lambda s, n: (q0 + s * n_q + n, 0)),
        input_output_aliases=aliases,
        compiler_params=_cparams("arbitrary", "arbitrary"),
        name="attention",
    )(*args)


def _out_route_kernel(*refs, n_mix, n_x, n_ptiles, tiles):
    mix_refs = refs[:n_mix]
    w_refs = refs[n_mix:2 * n_mix]
    x_refs = refs[2 * n_mix:2 * n_mix + n_x]
    x_all = _stream_tile(x_refs, n_ptiles)
    for j in range(tiles):
        rows = slice(j * TOKEN_TILE, (j + 1) * TOKEN_TILE)
        _out_route_tile(j, rows, x_all[rows], [m[rows, :] for m in mix_refs], w_refs, *refs[2 * n_mix + n_x:])


def _out_route_tile(j, rows, x, mixes, w_refs, gate_ref, g2_ref, sh2_ref, sc2_ref, rw_ref, rb_ref,
                    xo_ref, h_ref, pos_ref, post_ref, gates_ref, rows_ref, off_ref):
    acc = None
    for m, w_ref in zip(mixes, w_refs):
        part = _dot(m, w_ref[...])
        acc = part if acc is None else acc + part
    xn = x + gate_ref[...] * acc
    xo_ref[rows, :] = xn
    h = _rms(xn, g2_ref[...]) * (1.0 + sc2_ref[...]) + sh2_ref[...]
    hb = h.astype(BF16)
    h_ref[rows, :] = hb

    logits = _dot_nt(rw_ref[...], hb) + rb_ref[...]
    ne, tm = logits.shape
    expert = lax.broadcasted_iota(I32, (ne, tm), 0).astype(F32)
    work = logits
    sels, vals = [], []
    for _ in range(TOP_K):
        mx = jnp.max(work, axis=0, keepdims=True)
        first = jnp.min(jnp.where(work == mx, expert, float(ne)), axis=0, keepdims=True)
        sel = expert == first
        work = jnp.where(sel, -jnp.inf, work)
        sels.append(sel)
        vals.append(mx)
    exps = [jnp.exp(v - vals[0]) for v in vals]
    den = exps[0] + exps[1] + exps[2] + exps[3]

    onehot = jnp.zeros((ne, tm), F32)
    for sel in sels:
        onehot = onehot + sel.astype(F32)
    r_i = lax.broadcasted_iota(I32, (tm, tm), 0)
    c_i = lax.broadcasted_iota(I32, (tm, tm), 1)
    before = jnp.where(r_i < c_i, 1.0, 0.0).astype(BF16)
    earlier = _dot(onehot.astype(BF16), before)

    cnt = jnp.sum(onehot, axis=1, keepdims=True)
    seg8 = jnp.floor((cnt + (SEG_ALIGN - 1.0)) * (1.0 / SEG_ALIGN))
    e_r = lax.broadcasted_iota(I32, (ne, ne), 0)
    e_c = lax.broadcasted_iota(I32, (ne, ne), 1)
    lower = jnp.where(e_c < e_r, 1.0, 0.0).astype(BF16)
    off8 = _dot(lower, jnp.broadcast_to(seg8, (ne, LANES)).astype(BF16))[:, 0:1]
    seg_off = off8 * SEG_ALIGN
    base = seg_off + earlier

    sub = lax.broadcasted_iota(I32, (2 * TOP_K, tm), 0)
    token_rows = jnp.zeros((2 * TOP_K, tm), F32)
    for k in range(TOP_K):
        pos_k = jnp.sum(jnp.where(sels[k], base, 0.0), axis=0, keepdims=True)
        token_rows = jnp.where(sub == k, pos_k, token_rows)
        token_rows = jnp.where(sub == TOP_K + k, exps[k] / den, token_rows)
    first_lane = lax.broadcasted_iota(I32, (ne, tm), 1) == 0
    block = jnp.concatenate([
        token_rows,
        jnp.where(first_lane, seg8 * SEG_ALIGN, 0.0),
        jnp.where(first_lane, seg_off, 0.0),
        jnp.zeros((LANES - 2 * TOP_K - 2 * ne, tm), F32)], axis=0)
    by_token = block.T
    post_ref[j] = token_rows.astype(I32)
    pos_ref[rows, :] = by_token[:, 0:TOP_K].astype(I32)
    gates_ref[rows, :] = by_token[:, TOP_K:2 * TOP_K]
    rows_ref[j] = by_token[0:1, 2 * TOP_K:2 * TOP_K + ne].astype(I32)
    off_ref[j] = by_token[0:1, 2 * TOP_K + ne:2 * TOP_K + 2 * ne].astype(I32)


def _out_route(geom, layer, mixes, w_parts, x, mods, norm_g, router_w, router_b):
    t, d = geom.t, w_parts[0].shape[1]
    tiles = ROUTE_TILES_PER_STEP
    tm = TOKEN_TILE * tiles
    assert geom.n_ptiles % tiles == 0 and geom.tiles_per_lat % tiles == 0
    row = lambda i: (i, 0)
    const2 = lambda i: (0, 0)
    x_specs, x_args = _stream_specs(geom, x, d, tiles)
    in_specs = [pl.BlockSpec((tm, m.shape[1]), row) for m in mixes]
    in_specs += [pl.BlockSpec(w.shape, const2) for w in w_parts]
    in_specs += x_specs
    in_specs += [
        _mod_spec(geom, layer, 2, d, tiles),
        pl.BlockSpec((1, d), const2),
        _mod_spec(geom, layer, 3, d, tiles),
        _mod_spec(geom, layer, 4, d, tiles),
        pl.BlockSpec(router_w.shape, const2),
        pl.BlockSpec((N_EXPERTS, 1), const2),
    ]
    seg3 = lambda i: (i, 0, 0)
    out_shape = [
        jax.ShapeDtypeStruct((t, d), F32),
        jax.ShapeDtypeStruct((t, d), BF16),
        jax.ShapeDtypeStruct((t, TOP_K), I32),
        jax.ShapeDtypeStruct((geom.n_tiles, 2 * TOP_K, TOKEN_TILE), I32),
        jax.ShapeDtypeStruct((t, TOP_K), F32),
        jax.ShapeDtypeStruct((geom.n_tiles, 1, N_EXPERTS), I32),
        jax.ShapeDtypeStruct((geom.n_tiles, 1, N_EXPERTS), I32),
    ]
    out_specs = [
        pl.BlockSpec((tm, d), row),
        pl.BlockSpec((tm, d), row),
        pl.BlockSpec((tm, TOP_K), row),
        pl.BlockSpec((tiles, 2 * TOP_K, TOKEN_TILE), seg3),
        pl.BlockSpec((tm, TOP_K), row),
        pl.BlockSpec((tiles, 1, N_EXPERTS), seg3),
        pl.BlockSpec((tiles, 1, N_EXPERTS), seg3),
    ]
    return pl.pallas_call(
        functools.partial(_out_route_kernel, n_mix=len(mixes), n_x=len(x_args), n_ptiles=geom.n_ptiles // tiles,
                          tiles=tiles),
        out_shape=out_shape,
        grid=(geom.n_tiles // tiles,),
        in_specs=in_specs,
        out_specs=out_specs,
        compiler_params=_cparams("arbitrary"),
        name="out_route",
    )(*mixes, *w_parts, *x_args, mods, norm_g, mods, mods, router_w, router_b)


def _pack_pairs(v):
    n = v.shape[1] // 2
    bits = lax.bitcast_convert_type(v, U32)
    return (bits[:, :n] & jnp.uint32(0xFFFF0000)) | (bits[:, n:] >> 16)


def _unpack_pairs(p):
    hi = lax.bitcast_convert_type(p & jnp.uint32(0xFFFF0000), F32)
    lo = lax.bitcast_convert_type(p << 16, F32)
    return jnp.concatenate([hi, lo], axis=1).astype(BF16)


def _planned_copies(i, plan_refs, make_copy):
    for size, width, (local_ref, slot_ref, count_ref) in zip(COPY_SIZES, PLAN_WIDTHS, plan_refs):
        def one(c, carry, size=size, width=width, local_ref=local_ref, slot_ref=slot_ref):
            a = i * width + c
            make_copy(pl.multiple_of(local_ref[a], SEG_ALIGN), pl.multiple_of(slot_ref[a], SEG_ALIGN), size).start()
            return carry

        lax.fori_loop(0, count_ref[i], one, 0)


def _copy_plan(rows, seg_off, seg_dst):
    plan = []
    experts = jnp.arange(N_EXPERTS, dtype=I32)
    for n, (size, width) in enumerate(zip(COPY_SIZES, PLAN_WIDTHS)):
        if n == 0:
            count, done = rows // size, jnp.zeros_like(rows)
        else:
            count, done = (rows % (2 * size)) // size, rows - rows % (2 * size)
        cum = jnp.cumsum(count, axis=1)
        j = jnp.arange(width, dtype=I32)
        owner = jnp.sum((cum[:, None, :] <= j[None, :, None]).astype(I32), axis=2)
        pick = (jnp.minimum(owner, N_EXPERTS - 1)[:, :, None] == experts[None, None, :]).astype(I32)
        take = lambda v: jnp.sum(pick * v[:, None, :], axis=2)
        within = (j[None, :] - take(cum - count)) * size
        plan += [(take(seg_off + done) + within).reshape(-1).astype(I32),
                 (take(seg_dst + done) + within).reshape(-1).astype(I32),
                 cum[:, -1].astype(I32)]
    return plan


def _wait_copies(n_rows, make_copy):
    def wait_big(c, carry):
        make_copy(0, 0, WAIT_CHUNK).wait()
        return carry
    lax.fori_loop(0, n_rows // WAIT_CHUNK, wait_big, 0)
    size = WAIT_CHUNK // 2
    while size >= SEG_ALIGN:
        @pl.when(n_rows % (2 * size) >= size)
        def _(size=size):
            make_copy(0, 0, size).wait()

        size //= 2


def _dispatch_kernel(*refs):
    plan_refs, (tot_ref, h_ref, pos_ref, xs_ref, sorted_ref, sem) = _split_plan(refs)
    i = pl.program_id(0)
    buf = i % 2
    tm = h_ref.shape[0]
    n_sorted = sorted_ref.shape[1]
    pos = pos_ref[0]
    slot = lax.broadcasted_iota(I32, (n_sorted, tm), 0)
    hit = jnp.zeros((n_sorted, tm), F32)
    for k in range(TOP_K):
        hit = jnp.where(pos[k:k + 1, :] == slot, 1.0, hit)
    sorted_ref[buf] = _pack_pairs(_dot(hit.astype(BF16), h_ref[...]))

    def copies_from(b):
        def make_copy(local, slot, rows):
            return pltpu.make_async_copy(sorted_ref.at[b, pl.ds(local, rows)], xs_ref.at[pl.ds(slot, rows)],
                                         sem.at[b])
        return make_copy

    _planned_copies(i, plan_refs, copies_from(buf))

    @pl.when(i > 0)
    def _():
        _wait_copies(tot_ref[jnp.maximum(i - 1, 0)], copies_from(1 - buf))

    @pl.when(i == pl.num_programs(0) - 1)
    def _():
        _wait_copies(tot_ref[i], copies_from(buf))


def _split_plan(refs):
    n = 3 * len(COPY_SIZES)
    return [refs[k:k + 3] for k in range(0, n, 3)], refs[n:]


def _dispatch(h, pos, plan, tile_rows, n_slots):
    t, d = h.shape
    tm = TOKEN_TILE
    return pl.pallas_call(
        _dispatch_kernel,
        out_shape=jax.ShapeDtypeStruct((n_slots, d // 2), U32),
        grid_spec=pltpu.PrefetchScalarGridSpec(
            num_scalar_prefetch=len(plan) + 1,
            grid=(t // tm,),
            in_specs=[
                pl.BlockSpec((tm, d), lambda i, *_: (i, 0)),
                pl.BlockSpec((1, 2 * TOP_K, tm), lambda i, *_: (i, 0, 0)),
            ],
            out_specs=pl.BlockSpec(memory_space=pl.ANY),
            scratch_shapes=[pltpu.VMEM((2, SORTED_ROWS, d // 2), U32), pltpu.SemaphoreType.DMA((2,))],
        ),
        compiler_params=_cparams("arbitrary"),
        name="moe_dispatch",
    )(*plan, tile_rows, h, pos)


def _ffn_kernel(rows_ref, start_ref, wgu_ref, bgu_ref, wd_ref, bd_ref, xs_ref, ys_ref,
                wgu_bf, wd_bf, xbuf, ybuf, sem_in, sem_out):
    e = pl.program_id(0)
    d_ff = wd_ref.shape[1]
    tb = xbuf.shape[1]
    n_rows = rows_ref[e]
    n_tiles = (n_rows + tb - 1) // tb
    base = start_ref[e]

    chunk = 128
    def cast_gu(c, carry):
        r = pl.multiple_of(c * chunk, chunk)
        wgu_bf[pl.ds(r, chunk), :] = wgu_ref[0, pl.ds(r, chunk), :].astype(BF16)
        return carry
    lax.fori_loop(0, wgu_ref.shape[1] // chunk, cast_gu, 0)
    def cast_d(c, carry):
        r = pl.multiple_of(c * chunk, chunk)
        wd_bf[pl.ds(r, chunk), :] = wd_ref[0, pl.ds(r, chunk), :].astype(BF16)
        return carry
    lax.fori_loop(0, d_ff // chunk, cast_d, 0)

    half = tb // 2
    last_small = (n_rows - (n_tiles - 1) * tb) <= half

    def x_copy(s, slot):
        r = pl.multiple_of(base + s * tb, EXPERT_TILE)
        return pltpu.make_async_copy(xs_ref.at[pl.ds(r, tb)], xbuf.at[slot], sem_in.at[slot])

    def y_copy(s, slot, rows=tb):
        r = pl.multiple_of(base + s * tb, EXPERT_TILE)
        return pltpu.make_async_copy(ybuf.at[slot, pl.ds(0, rows)], ys_ref.at[pl.ds(r, rows)], sem_out.at[slot])

    def ffn_rows(s, slot, n):
        rows = s * tb + lax.broadcasted_iota(I32, (n, 1), 0)
        x = _unpack_pairs(jnp.where(rows < n_rows, xbuf[slot, pl.ds(0, n), :], jnp.uint32(0)))
        gu = _dot(x, wgu_bf[...]) + bgu_ref[0]
        gate = jnp.minimum(gu[:, :d_ff], SWIGLU_LIMIT)
        up = jnp.clip(gu[:, d_ff:], -SWIGLU_LIMIT, SWIGLU_LIMIT)
        act = (up + 1.0) * (gate * (1.0 / (1.0 + jnp.exp(-SWIGLU_ALPHA * gate))))
        y = _dot(act.astype(BF16), wd_bf[...]) + bd_ref[0]
        ybuf[slot, pl.ds(0, n), :] = _pack_pairs(y.astype(BF16).astype(F32))
        y_copy(s, slot, n).start()

    @pl.when(n_tiles > 0)
    def _():
        x_copy(0, 0).start()

    def tile(s, carry):
        slot = s % 2
        x_copy(s, slot).wait()

        @pl.when(s + 1 < n_tiles)
        def _():
            x_copy(s + 1, 1 - slot).start()

        @pl.when(s >= 2)
        def _():
            y_copy(s - 2, slot).wait()

        small = jnp.logical_and(s == n_tiles - 1, last_small)

        @pl.when(jnp.logical_not(small))
        def _():
            ffn_rows(s, slot, tb)

        @pl.when(small)
        def _():
            ffn_rows(s, slot, half)

        return carry

    lax.fori_loop(0, n_tiles, tile, 0)

    @pl.when(n_tiles >= 2)
    def _():
        y_copy(n_tiles - 2, n_tiles % 2).wait()

    @pl.when(jnp.logical_and(n_tiles >= 1, jnp.logical_not(last_small)))
    def _():
        y_copy(n_tiles - 1, (n_tiles - 1) % 2).wait()

    @pl.when(jnp.logical_and(n_tiles >= 1, last_small))
    def _():
        y_copy(n_tiles - 1, (n_tiles - 1) % 2, half).wait()


def _expert_ffn(layer, xs, expert_rows, expert_start, w_gu, b_gu, w_down, b_down):
    n_slots, packed_w = xs.shape
    tb = FFN_TILE
    depth, ne, d, two_f = w_gu.shape
    d_ff = two_f // 2
    exp4 = lambda e, *_: (layer, e, 0, 0)
    return pl.pallas_call(
        _ffn_kernel,
        out_shape=jax.ShapeDtypeStruct((n_slots, packed_w), U32),
        grid_spec=pltpu.PrefetchScalarGridSpec(
            num_scalar_prefetch=2,
            grid=(ne,),
            in_specs=[
                pl.BlockSpec((None, 1, d, two_f), exp4),
                pl.BlockSpec((None, 1, 1, two_f), exp4),
                pl.BlockSpec((None, 1, d_ff, d), exp4),
                pl.BlockSpec((None, 1, 1, d), exp4),
                pl.BlockSpec(memory_space=pl.ANY),
            ],
            out_specs=pl.BlockSpec(memory_space=pl.ANY),
            scratch_shapes=[
                pltpu.VMEM((d, two_f), BF16), pltpu.VMEM((d_ff, d), BF16),
                pltpu.VMEM((2, tb, packed_w), U32), pltpu.VMEM((2, tb, packed_w), U32),
                pltpu.SemaphoreType.DMA((2,)), pltpu.SemaphoreType.DMA((2,)),
            ],
        ),
        compiler_params=_cparams("arbitrary"),
        name="expert_ffn",
    )(expert_rows, expert_start, w_gu, b_gu.reshape(depth, ne, 1, two_f),
      w_down, b_down.reshape(depth, ne, 1, d), xs)


def _combine_kernel(*refs, final, n_ptiles):
    plan_refs, (tot_ref, x_ref, pos_ref, gates_ref, mg_ref, *rest) = _split_plan(refs)
    if final:
        fn_ref, ys_ref, op_ref, os_ref, buf, sem = rest
    else:
        ys_ref, o_ref, buf, sem = rest
    i = pl.program_id(0)
    cur = i % 2
    tm = x_ref.shape[0]
    n_sorted = buf.shape[1]

    def copies_into(b):
        def make_copy(local, slot, rows):
            return pltpu.make_async_copy(ys_ref.at[pl.ds(slot, rows)], buf.at[b, pl.ds(local, rows)], sem.at[b])
        return make_copy

    @pl.when(i == 0)
    def _():
        buf[...] = jnp.zeros_like(buf)
        _planned_copies(i, plan_refs, copies_into(cur))

    @pl.when(i + 1 < pl.num_programs(0))
    def _():
        _planned_copies(i + 1, plan_refs, copies_into(1 - cur))

    pos = pos_ref[...]
    g = gates_ref[...]
    lane = lax.broadcasted_iota(I32, (tm, n_sorted), 1)
    weight = jnp.zeros((tm, n_sorted), F32)
    for k in range(TOP_K):
        weight = jnp.where(pos[:, k:k + 1] == lane, g[:, k:k + 1], weight)
    _wait_copies(tot_ref[i], copies_into(cur))
    y = _dot(weight.astype(BF16), _unpack_pairs(buf[cur]))
    xn = x_ref[...] + mg_ref[...] * y
    if final:
        xn = _rms(xn, fn_ref[...])

        @pl.when(i < n_ptiles)
        def _():
            op_ref[...] = xn

        @pl.when(i >= n_ptiles)
        def _():
            os_ref[...] = xn
    else:
        o_ref[...] = xn


def _combine(geom, layer, ys, pos, plan, tile_rows, x, gates, mods, final_g):
    t, d = x.shape
    tm = TOKEN_TILE
    final = final_g is not None

    def mod_imap(i, *_):
        return ((layer * MOD_ROWS + geom.group(i)) * N_MOD + 5, 0, 0)

    row = lambda i, *_: (i, 0)
    in_specs = [
        pl.BlockSpec((tm, d), row),
        pl.BlockSpec((tm, TOP_K), row),
        pl.BlockSpec((tm, TOP_K), row),
        pl.BlockSpec((None, 1, d), mod_imap),
    ]
    args = [*plan, tile_rows, x, pos, gates, mods]
    if final:
        in_specs.append(pl.BlockSpec((1, d), lambda i, *_: (0, 0)))
        args.append(final_g)
    in_specs.append(pl.BlockSpec(memory_space=pl.ANY))
    args.append(ys)
    if final:
        n_pt = geom.n_ptiles
        out_shape = [jax.ShapeDtypeStruct((geom.tp, d), F32), jax.ShapeDtypeStruct((t - geom.tp, d), F32)]
        out_specs = [pl.BlockSpec((tm, d), lambda i, *_: (jnp.minimum(i, n_pt - 1), 0)),
                     pl.BlockSpec((tm, d), lambda i, *_: (jnp.maximum(i - n_pt, 0), 0))]
    else:
        out_shape = jax.ShapeDtypeStruct((t, d), F32)
        out_specs = pl.BlockSpec((tm, d), row)
    return pl.pallas_call(
        functools.partial(_combine_kernel, final=final, n_ptiles=geom.n_ptiles),
        out_shape=out_shape,
        grid_spec=pltpu.PrefetchScalarGridSpec(
            num_scalar_prefetch=len(plan) + 1,
            grid=(t // tm,),
            in_specs=in_specs,
            out_specs=out_specs,
            scratch_shapes=[pltpu.VMEM((2, SORTED_ROWS, d // 2), U32), pltpu.SemaphoreType.DMA((2,))],
        ),
        compiler_params=_cparams("arbitrary"),
        name="moe_combine",
    )(*args)


def _moe(geom, layer, h, pos, pos_t, gates, seg_rows, seg_off, x, mods, w_gu, b_gu, w_down, b_down, final_g):
    t = h.shape[0]
    tb = EXPERT_TILE
    n_tok_tiles = seg_rows.shape[0]
    max_rows = t * TOP_K + n_tok_tiles * N_EXPERTS * (SEG_ALIGN - 1)
    n_blocks = -(-max_rows // tb) + N_EXPERTS
    rows = seg_rows[:, 0, :]
    cnt = jnp.sum(rows, axis=0)
    n_tiles_e = (cnt + tb - 1) // tb
    tile_end = jnp.cumsum(n_tiles_e)
    tile_start = tile_end - n_tiles_e
    expert_start = (tile_start * tb).astype(I32)
    seg_dst = expert_start[None, :] + jnp.cumsum(rows, axis=0) - rows
    plan = _copy_plan(rows, seg_off[:, 0, :], seg_dst)
    tile_rows = jnp.sum(rows, axis=1).astype(I32)
    xs = _dispatch(h, pos_t, plan, tile_rows, n_blocks * tb + FFN_TILE - tb)
    ys = _expert_ffn(layer, xs, cnt.astype(I32), expert_start, w_gu, b_gu, w_down, b_down)
    return _combine(geom, layer, ys, pos, plan, tile_rows, x, gates, mods, final_g)


def kernel(x_prompt, x_sample, cache_b_k, cache_b_v, cache_c_k, cache_c_v, c, c_ctx,
           mod_w, mod_b, norm_mix, norm_ffn, even_w_in, even_w_out, even_sink,
           odd_w_in, odd_w_out, odd_q_norm, odd_k_norm, router_w, router_b,
           moe_w_gu, moe_b_gu, moe_w_down, moe_b_down, final_norm):
    bp, lp, d = x_prompt.shape
    bs, ls, _ = x_sample.shape
    past = cache_b_k.shape[2]
    depth = mod_w.shape[0]
    geom = _Geom(bp, lp, bs, ls)
    tp = geom.tp

    x = (x_prompt.reshape(tp, d), x_sample.reshape(bs * ls, d))
    cond = jnp.concatenate([c_ctx[None, :], c, jnp.zeros((MOD_ROWS - 1 - bs, d), F32)], axis=0)
    mods = _modulation(cond, mod_w, mod_b).reshape(depth * MOD_ROWS * N_MOD, 1, d)

    cn, sn = _dft_tables(A_GROUP_DIM)
    dft_chan = jnp.asarray(np.concatenate([cn, sn], axis=1), BF16)
    dft_p = [jnp.asarray(m, BF16) for m in _dft_tables(lp)]
    dft_s = [jnp.asarray(m, BF16) for m in _dft_tables(ls)]
    rope_b = [jnp.asarray(m) for m in _rope_tables(TOKEN_TILE, ls, B_HEAD_DIM)]
    rope_c = [jnp.asarray(m) for m in _rope_tables(TOKEN_TILE, ls, C_HEAD_DIM)]

    states = {"bk": [], "bv": [], "ck": [], "cv": []}
    for layer in range(depth):
        j = layer // 2
        g_mix = norm_mix[layer][None, :]
        g_ffn = norm_ffn[layer][None, :]
        if layer % 2 == 0:
            pair = _paired_head_order(B_KV_HEADS, B_HEADS // B_KV_HEADS, B_HEAD_DIM)
            w_in = even_w_in[j]
            w_in = jnp.concatenate([w_in[:, :A_WIDTH], w_in[:, A_WIDTH:A_WIDTH + B_Q_WIDTH][:, pair],
                                    w_in[:, A_WIDTH + B_Q_WIDTH:]], axis=1).astype(BF16)
            tc, ts, q, k, v, k_state, v_state = _in_projection(
                geom, layer, x, g_mix, mods, w_in, rope_b[0], rope_b[1], [dft_chan],
                _proj_even_kernel, (A_WIDTH, A_WIDTH, B_Q_WIDTH, B_KV_WIDTH, B_KV_WIDTH), B_KV_WIDTH, "proj_even")
            states["bk"].append(k_state.reshape(bp, lp, B_KV_HEADS, B_HEAD_DIM))
            states["bv"].append(v_state.reshape(bp, lp, B_KV_HEADS, B_HEAD_DIM))
            four = _fourier_tokens(tc, ts, dft_p[0], dft_p[1], bp, lp, 0, None)
            four = _fourier_tokens(tc, ts, dft_s[0], dft_s[1], bs, ls, tp, four)
            sink = even_sink[j]
            common = dict(kv_heads=B_KV_HEADS, groups=B_HEADS // B_KV_HEADS, dh=B_HEAD_DIM)
            att = _attention(q, k, v, None, sink, None, n_seq=bp, seq_len=lp, row0=0, q_tile=lp,
                             window=None, seqs=PROMPT_SEQS_PER_STEP, **common)
            ctx = (cache_b_k[:, j].reshape(bs, past, B_KV_WIDTH).astype(BF16),
                   cache_b_v[:, j].reshape(bs, past, B_KV_WIDTH).astype(BF16))
            att = _attention(q, k, v, ctx, sink, att, n_seq=bs, seq_len=ls, row0=tp, q_tile=ATTN_Q_TILE,
                             window=WINDOW, **common)
            w_out = even_w_out[j].astype(BF16)
            mixes = [four, att]
            w_parts = [w_out[:A_WIDTH], w_out[A_WIDTH:][pair]]
        else:
            q, k, v, k_state, v_state = _in_projection(
                geom, layer, x, g_mix, mods, odd_w_in[j].astype(BF16), rope_c[0], rope_c[1],
                [odd_q_norm[j][None, :], odd_k_norm[j][None, :]],
                _proj_odd_kernel, (C_Q_WIDTH, C_KV_WIDTH, C_KV_WIDTH), C_KV_WIDTH, "proj_odd")
            states["ck"].append(k_state.reshape(bp, lp, C_KV_HEADS, C_HEAD_DIM))
            states["cv"].append(v_state.reshape(bp, lp, C_KV_HEADS, C_HEAD_DIM))
            common = dict(kv_heads=C_KV_HEADS, groups=C_HEADS // C_KV_HEADS, dh=C_HEAD_DIM, window=None)
            att = _attention(q, k, v, None, None, None, n_seq=bp, seq_len=lp, row0=0, q_tile=lp,
                             seqs=PROMPT_SEQS_PER_STEP, **common)
            ctx = (cache_c_k[:, j].reshape(bs, past, C_KV_WIDTH).astype(BF16),
                   cache_c_v[:, j].reshape(bs, past, C_KV_WIDTH).astype(BF16))
            att = _attention(q, k, v, ctx, None, att, n_seq=bs, seq_len=ls, row0=tp, q_tile=DENSE_Q_TILE, **common)
            mixes = [att]
            w_parts = [odd_w_out[j].astype(BF16)]
        x, h, pos, pos_t, gates, seg_rows, seg_off = _out_route(
            geom, layer, mixes, w_parts, x, mods, g_ffn, router_w[layer].T.astype(BF16), router_b[layer][:, None])
        final_g = final_norm[None, :] if layer == depth - 1 else None
        x = _moe(geom, layer, h, pos, pos_t, gates, seg_rows, seg_off, x, mods,
                 moe_w_gu, moe_b_gu, moe_w_down, moe_b_down, final_g)

    y_prompt = x[0].reshape(bp, lp, d)
    y_sample = x[1].reshape(bs, ls, d)
    return (y_prompt, y_sample,
            jnp.stack(states["bk"], axis=1), jnp.stack(states["bv"], axis=1),
            jnp.stack(states["ck"], axis=1), jnp.stack(states["cv"], axis=1))
```

```python
import functools

import numpy as np
import jax
import jax.numpy as jnp
from jax import lax
from jax.experimental import pallas as pl
from jax.experimental.pallas import tpu as pltpu

F32 = jnp.float32
BF16 = jnp.bfloat16
I32 = jnp.int32
U32 = jnp.uint32

GRID_W = 64
A_GROUPS = 4
A_GROUP_DIM = 128
A_WIDTH = A_GROUPS * A_GROUP_DIM
B_HEADS = 8
B_KV_HEADS = 2
B_HEAD_DIM = 64
B_Q_WIDTH = B_HEADS * B_HEAD_DIM
B_KV_WIDTH = B_KV_HEADS * B_HEAD_DIM
WINDOW = 128
C_HEADS = 8
C_KV_HEADS = 2
C_HEAD_DIM = 128
C_Q_WIDTH = C_HEADS * C_HEAD_DIM
C_KV_WIDTH = C_KV_HEADS * C_HEAD_DIM
ROPE_THETA = 10000.0
N_EXPERTS = 32
TOP_K = 4
SWIGLU_LIMIT = 7.0
SWIGLU_ALPHA = 1.702
EPS = 1e-6

LANES = 128
SUBLANES = 8
TOKEN_TILE = 256
ROUTE_TILES_PER_STEP = 4
EXPERT_TILE = 128
FFN_TILE = 256
ATTN_Q_TILE = 128
DENSE_Q_TILE = 256
ATTN_KEY_CHUNK = 1024
PROMPT_SEQS_PER_STEP = 4
SEG_ALIGN = SUBLANES
COPY_SIZES = (32, 16, 8)
WAIT_CHUNK = 256
SORTED_ROWS = -(-(TOKEN_TILE * TOP_K + N_EXPERTS * (SEG_ALIGN - 1)) // LANES) * LANES
PLAN_WIDTHS = (SORTED_ROWS // COPY_SIZES[0],) + (N_EXPERTS,) * (len(COPY_SIZES) - 1)
VMEM_LIMIT = 56 * 1024 * 1024
MASKED = -1e30
N_MOD = 6
MOD_ROWS = SUBLANES


def _cparams(*sem):
    return pltpu.CompilerParams(dimension_semantics=tuple(sem), vmem_limit_bytes=VMEM_LIMIT)


def _dot(a, b):
    return jnp.dot(a, b, preferred_element_type=F32)


def _dot_nt(a, b):
    return lax.dot_general(a, b, (((1,), (1,)), ((), ())), preferred_element_type=F32)


def _rms(x, g):
    return x * lax.rsqrt(jnp.mean(x * x, axis=-1, keepdims=True) + EPS) * g


def _dft_tables(n):
    j = np.arange(n, dtype=np.int64)
    ang = 2.0 * np.pi * ((j[:, None] * j[None, :]) % n).astype(np.float64) / n
    s = 1.0 / np.sqrt(n)
    return np.cos(ang) * s, np.sin(ang) * s


def _paired_head_order(kv_heads, groups, dh):
    assert kv_heads * dh == LANES
    return np.array([(kv * groups + g) * dh + d for g in range(groups) for kv in range(kv_heads) for d in range(dh)])


def _rope_tables(n_prompt_rows, n_latent, head_dim):
    quarter = head_dim // 4
    pos = np.arange(n_latent)
    row = (pos // GRID_W).astype(np.float32)
    col = (pos % GRID_W).astype(np.float32)
    inv = (np.float32(ROPE_THETA) ** (-np.arange(quarter, dtype=np.float32) / np.float32(quarter))).astype(np.float32)
    ang_row = (row[:, None] * inv[None, :]).astype(np.float32)
    ang_col = (col[:, None] * inv[None, :]).astype(np.float32)
    cos_h = np.concatenate([np.cos(ang_row)] * 2 + [np.cos(ang_col)] * 2, axis=1)
    sin_h = np.concatenate([-np.sin(ang_row), np.sin(ang_row), -np.sin(ang_col), np.sin(ang_col)], axis=1)
    reps = LANES // head_dim
    cos_l = np.tile(cos_h, (1, reps)).astype(np.float32)
    sin_l = np.tile(sin_h, (1, reps)).astype(np.float32)
    cos = np.concatenate([np.ones((n_prompt_rows, LANES), np.float32), cos_l], axis=0)
    sin = np.concatenate([np.zeros((n_prompt_rows, LANES), np.float32), sin_l], axis=0)
    return cos, sin


def _rope(x, cos, sin, quarter):
    lane = lax.broadcasted_iota(I32, (x.shape[0], LANES), 1)
    first = ((lane // quarter) % 2) == 0
    outs = []
    for c in range(x.shape[1] // LANES):
        xc = x[:, c * LANES:(c + 1) * LANES]
        partner = jnp.where(first, pltpu.roll(xc, LANES - quarter, 1), pltpu.roll(xc, quarter, 1))
        outs.append(xc * cos + partner * sin)
    return outs[0] if len(outs) == 1 else jnp.concatenate(outs, axis=1)


def _head_rms(x, g):
    outs = []
    for c in range(x.shape[1] // LANES):
        outs.append(_rms(x[:, c * LANES:(c + 1) * LANES], g))
    return outs[0] if len(outs) == 1 else jnp.concatenate(outs, axis=1)


def _mod_kernel(c_ref, w_ref, b_ref, o_ref):
    c = c_ref[...]
    s = c * (1.0 / (1.0 + jnp.exp(-c)))
    o_ref[0] = _dot(s.astype(BF16), w_ref[0].astype(BF16)) + b_ref[0]


def _modulation(cond, mod_w, mod_b):
    depth, d, n = mod_w.shape
    tn = 1536
    return pl.pallas_call(
        _mod_kernel,
        out_shape=jax.ShapeDtypeStruct((depth, MOD_ROWS, n), F32),
        grid=(depth, n // tn),
        in_specs=[
            pl.BlockSpec((MOD_ROWS, d), lambda l, j: (0, 0)),
            pl.BlockSpec((1, d, tn), lambda l, j: (l, 0, j)),
            pl.BlockSpec((1, 1, tn), lambda l, j: (l, 0, j)),
        ],
        out_specs=pl.BlockSpec((1, MOD_ROWS, tn), lambda l, j: (l, 0, j)),
        compiler_params=_cparams("arbitrary", "arbitrary"),
        name="modulation",
    )(cond, mod_w, mod_b.reshape(depth, 1, n))


class _Geom:
    def __init__(self, bp, lp, bs, ls):
        self.bp, self.lp, self.bs, self.ls = bp, lp, bs, ls
        self.tp = bp * lp
        self.t = bp * lp + bs * ls
        assert lp == TOKEN_TILE and ls % TOKEN_TILE == 0 and self.tp % ls == 0
        self.n_ptiles = self.tp // TOKEN_TILE
        self.tiles_per_lat = ls // TOKEN_TILE
        self.n_tiles = self.t // TOKEN_TILE

    def group(self, i):
        return jnp.where(i < self.n_ptiles, 0, 1 + (i - self.n_ptiles) // self.tiles_per_lat)

    def pos_block(self, i):
        return jnp.where(i < self.n_ptiles, 0, 1 + (i - self.n_ptiles) % self.tiles_per_lat)


def _mod_spec(geom, layer, which, d, tiles=1):
    def imap(i):
        return ((layer * MOD_ROWS + geom.group(i * tiles)) * N_MOD + which, 0, 0)
    return pl.BlockSpec((None, 1, d), imap)


def _stream_specs(geom, x, d, tiles=1):
    tm = TOKEN_TILE * tiles
    n_p = geom.n_ptiles // tiles
    if isinstance(x, tuple):
        return ([pl.BlockSpec((tm, d), lambda i, *_: (jnp.minimum(i, n_p - 1), 0)),
                 pl.BlockSpec((tm, d), lambda i, *_: (jnp.maximum(i - n_p, 0), 0))], list(x))
    return [pl.BlockSpec((tm, d), lambda i, *_: (i, 0))], [x]


def _stream_tile(x_refs, n_ptiles):
    if len(x_refs) == 1:
        return x_refs[0][...]
    return jnp.where(pl.program_id(0) < n_ptiles, x_refs[0][...], x_refs[1][...])


def _store_kv(k, v, kb_ref, vb_ref, ks_ref, vs_ref, n_ptiles):
    kb_ref[...] = k.astype(BF16)
    vb_ref[...] = v.astype(BF16)

    @pl.when(pl.program_id(0) < n_ptiles)
    def _():
        ks_ref[...] = k
        vs_ref[...] = v


def _proj_even_kernel(*refs, n_x, n_ptiles):
    x_refs = refs[:n_x]
    (g_ref, sh_ref, sc_ref, w_ref, cos_ref, sin_ref, dft_ref,
     tc_ref, ts_ref, q_ref, kb_ref, vb_ref, ks_ref, vs_ref) = refs[n_x:]
    h = _rms(_stream_tile(x_refs, n_ptiles), g_ref[...]) * (1.0 + sc_ref[...]) + sh_ref[...]
    p = _dot(h.astype(BF16), w_ref[...])
    cos = cos_ref[...]
    sin = sin_ref[...]
    dft = dft_ref[...]
    tcs, tss = [], []
    for g in range(A_GROUPS):
        t = _dot(p[:, g * A_GROUP_DIM:(g + 1) * A_GROUP_DIM].astype(BF16), dft)
        tcs.append(t[:, :A_GROUP_DIM])
        tss.append(t[:, A_GROUP_DIM:])
    tc_ref[...] = jnp.concatenate(tcs, axis=1).astype(BF16)
    ts_ref[...] = jnp.concatenate(tss, axis=1).astype(BF16)
    o = A_WIDTH
    q = _rope(p[:, o:o + B_Q_WIDTH], cos, sin, B_HEAD_DIM // 4)
    q_ref[...] = (q * B_HEAD_DIM ** -0.5).astype(BF16)
    o += B_Q_WIDTH
    k = _rope(p[:, o:o + B_KV_WIDTH], cos, sin, B_HEAD_DIM // 4)
    o += B_KV_WIDTH
    _store_kv(k, p[:, o:o + B_KV_WIDTH], kb_ref, vb_ref, ks_ref, vs_ref, n_ptiles)


def _proj_odd_kernel(*refs, n_x, n_ptiles):
    x_refs = refs[:n_x]
    (g_ref, sh_ref, sc_ref, w_ref, cos_ref, sin_ref, qn_ref, kn_ref,
     q_ref, kb_ref, vb_ref, ks_ref, vs_ref) = refs[n_x:]
    h = _rms(_stream_tile(x_refs, n_ptiles), g_ref[...]) * (1.0 + sc_ref[...]) + sh_ref[...]
    p = _dot(h.astype(BF16), w_ref[...])
    cos = cos_ref[...]
    sin = sin_ref[...]
    q = _head_rms(p[:, :C_Q_WIDTH], qn_ref[...])
    k = _head_rms(p[:, C_Q_WIDTH:C_Q_WIDTH + C_KV_WIDTH], kn_ref[...])
    q_ref[...] = (_rope(q, cos, sin, C_HEAD_DIM // 4) * C_HEAD_DIM ** -0.5).astype(BF16)
    k = _rope(k, cos, sin, C_HEAD_DIM // 4)
    _store_kv(k, p[:, C_Q_WIDTH + C_KV_WIDTH:], kb_ref, vb_ref, ks_ref, vs_ref, n_ptiles)


def _in_projection(geom, layer, x, norm_g, mods, w, cos, sin, extras, kernel, out_widths, kv_width, name):
    t, d = geom.t, w.shape[0]
    tm = TOKEN_TILE
    n_out = w.shape[1]
    row = lambda i: (i, 0)
    const2 = lambda i: (0, 0)
    x_specs, x_args = _stream_specs(geom, x, d)
    in_specs = x_specs + [
        pl.BlockSpec((1, d), const2),
        _mod_spec(geom, layer, 0, d),
        _mod_spec(geom, layer, 1, d),
        pl.BlockSpec((d, n_out), const2),
        pl.BlockSpec((tm, LANES), lambda i: (geom.pos_block(i), 0)),
        pl.BlockSpec((tm, LANES), lambda i: (geom.pos_block(i), 0)),
    ] + [pl.BlockSpec(e.shape, const2) for e in extras]
    return pl.pallas_call(
        functools.partial(kernel, n_x=len(x_args), n_ptiles=geom.n_ptiles),
        out_shape=([jax.ShapeDtypeStruct((t, wd), BF16) for wd in out_widths]
                   + [jax.ShapeDtypeStruct((geom.tp, kv_width), F32)] * 2),
        grid=(geom.n_tiles,),
        in_specs=in_specs,
        out_specs=([pl.BlockSpec((tm, wd), row) for wd in out_widths]
                   + [pl.BlockSpec((tm, kv_width), lambda i: (jnp.minimum(i, geom.n_ptiles - 1), 0))] * 2),
        compiler_params=_cparams("arbitrary"),
        name=name,
    )(*x_args, norm_g, mods, mods, w, cos, sin, *extras)


def _fourier_kernel(cl_ref, sl_ref, tc_ref, ts_ref, *rest):
    o_ref = rest[-1]
    o_ref[...] = (_dot(cl_ref[...], tc_ref[...]) - _dot(sl_ref[...], ts_ref[...])).astype(o_ref.dtype)


def _fourier_tokens(tc, ts, cl, sl, n_seq, seq_len, row0, prev):
    t, width = tc.shape
    tr = min(seq_len, 512)
    n_r = seq_len // tr
    assert row0 % seq_len == 0
    seq0 = row0 // seq_len
    out0 = row0 // tr
    in_specs = [
        pl.BlockSpec((tr, seq_len), lambda s, r: (r, 0)),
        pl.BlockSpec((tr, seq_len), lambda s, r: (r, 0)),
        pl.BlockSpec((seq_len, width), lambda s, r: (seq0 + s, 0)),
        pl.BlockSpec((seq_len, width), lambda s, r: (seq0 + s, 0)),
    ]
    args = [cl, sl, tc, ts]
    aliases = {}
    if prev is not None:
        in_specs.append(pl.BlockSpec(memory_space=pl.ANY))
        args.append(prev)
        aliases = {4: 0}
    return pl.pallas_call(
        _fourier_kernel,
        out_shape=jax.ShapeDtypeStruct((t, width), BF16),
        grid=(n_seq, n_r),
        in_specs=in_specs,
        out_specs=pl.BlockSpec((tr, width), lambda s, r: (out0 + s * n_r + r, 0)),
        input_output_aliases=aliases,
        compiler_params=_cparams("arbitrary", "arbitrary"),
        name="fourier_tokens",
    )(*args)


def _attend(q, chunks, sink, o0, dh, den_col):
    m = sink
    acc = None
    den = None
    for k, v, mask in chunks:
        s = _dot_nt(q, k)
        if mask is not None:
            s = jnp.where(mask, s, MASKED)
        m_new = jnp.max(s, axis=-1, keepdims=True)
        if m is not None:
            m_new = jnp.maximum(m, m_new)
        p = jnp.exp(s - m_new)
        pv = _dot(p.astype(BF16), v)
        if acc is None:
            acc = pv
            if den_col is None:
                den = jnp.sum(p, axis=-1, keepdims=True)
        else:
            alpha = jnp.exp(m - m_new)
            acc = alpha * acc + pv
            if den_col is None:
                den = alpha * den + jnp.sum(p, axis=-1, keepdims=True)
        m = m_new
    if den_col is not None:
        den = acc[:, den_col:den_col + 1]
    if sink is not None:
        den = den + jnp.exp(sink - m)
    return acc[:, o0:o0 + dh] / den


def _attend_two_pass(q, chunks, sink):
    scores = []
    m = sink
    for k, _, mask in chunks:
        s = _dot_nt(q, k)
        if mask is not None:
            s = jnp.where(mask, s, MASKED)
        scores.append(s)
        mx = jnp.max(s, axis=-1, keepdims=True)
        m = mx if m is None else jnp.maximum(m, mx)
    den = None if sink is None else jnp.exp(sink - m)
    acc = None
    for (_, v, _), s in zip(chunks, scores):
        e = jnp.exp(s - m)
        es = jnp.sum(e, axis=-1, keepdims=True)
        den = es if den is None else den + es
        o = _dot(e.astype(BF16), v)
        acc = o if acc is None else acc + o
    return acc / den


def _stack_heads(q, kv, groups, dh):
    return jnp.concatenate([q[:, (kv * groups + g) * dh:(kv * groups + g + 1) * dh] for g in range(groups)], axis=0)


def _sink_column(sink_ref, kv, groups, rows):
    return jnp.concatenate([jnp.full((rows, 1), sink_ref[kv * groups + g], F32) for g in range(groups)], axis=0)


def _head_values(v, kv, dh, with_ones):
    if not with_ones:
        return v[:, kv * dh:(kv + 1) * dh], 0, None
    assert dh == LANES
    lane = lax.broadcasted_iota(I32, (v.shape[0], LANES), 1)
    ones = jnp.where(lane == 0, 1.0, 0.0).astype(BF16)
    return jnp.concatenate([v[:, kv * dh:(kv + 1) * dh], ones], axis=1), 0, dh


def _paired_heads_attention(q, sources, sink_ref, groups, dh):
    rows = q.shape[0]
    half_q = lax.broadcasted_iota(I32, (rows, LANES), 1) // dh
    per_head = []
    for kv in range(2):
        qh = jnp.concatenate(
            [jnp.where(half_q == kv, q[:, g * LANES:(g + 1) * LANES], jnp.zeros((), q.dtype)) for g in range(groups)],
            axis=0)
        chunks = []
        for k, v, msk in sources:
            half_v = lax.broadcasted_iota(I32, v.shape, 1) // dh
            chunks.append((k, jnp.where(half_v == kv, v, jnp.zeros((), v.dtype)), msk))
        sink = _sink_column(sink_ref, kv, groups, rows) if sink_ref is not None else None
        per_head.append(_attend_two_pass(qh, chunks, sink))
    return jnp.concatenate(
        [per_head[0][g * rows:(g + 1) * rows] + per_head[1][g * rows:(g + 1) * rows] for g in range(groups)], axis=1)


def _attn_kernel(*refs, kv_heads, groups, dh, has_sink, has_ctx, window, q_tile, seq_len, chunk, seqs):
    refs = list(refs)
    sink_ref = refs.pop(0) if has_sink else None
    q_ref, k_ref, v_ref = refs[:3]
    ck_ref, cv_ref = (refs[3], refs[4]) if has_ctx else (None, None)
    o_ref = refs[-1]
    rows = q_ref.shape[0] // seqs
    assert seqs == 1 or (window is None and not has_ctx and q_tile == seq_len)
    for j in range(seqs):
        q_rows = slice(j * rows, (j + 1) * rows)
        q = q_ref[q_rows, :]
        if window is None:
            spans = [(j * seq_len + c * chunk, chunk) for c in range(seq_len // chunk)]
            mask = None
        else:
            n = pl.program_id(1)
            band = q_tile + 2 * window
            start = pl.multiple_of(jnp.clip(n * q_tile - window, 0, seq_len - band), LANES)
            spans = [(start, band)]
            qpos = n * q_tile + lax.broadcasted_iota(I32, (groups * rows, band), 0) % rows
            kpos = start + lax.broadcasted_iota(I32, (groups * rows, band), 1)
            mask = jnp.abs(kpos - qpos) <= window
        sources = [(k_ref[pl.ds(s0, n_s), :], v_ref[pl.ds(s0, n_s), :], mask) for s0, n_s in spans]
        if has_ctx:
            sources.append((ck_ref[...], cv_ref[...], None))
        if 2 * dh == LANES:
            o_ref[q_rows, :] = _paired_heads_attention(q, sources, sink_ref, groups, dh).astype(o_ref.dtype)
            continue
        outs = []
        for kv in range(kv_heads):
            online = dh == LANES and len(sources) > 1
            chunks = []
            for k, v, msk in sources:
                vh, o0, den_col = _head_values(v, kv, dh, with_ones=online)
                chunks.append((k[:, kv * dh:(kv + 1) * dh], vh, msk))
            sink = _sink_column(sink_ref, kv, groups, rows) if has_sink else None
            qh = _stack_heads(q, kv, groups, dh)
            o = _attend(qh, chunks, sink, o0, dh, den_col) if online else _attend_two_pass(qh, chunks, sink)
            outs.extend(o[g * rows:(g + 1) * rows] for g in range(groups))
        o_ref[q_rows, :] = jnp.concatenate(outs, axis=1).astype(o_ref.dtype)


def _attention(q, k, v, ctx, sink, prev, *, n_seq, seq_len, row0, q_tile, kv_heads, groups, dh, window, seqs=1):
    t, qw = q.shape
    kw = k.shape[1]
    n_q = seq_len // q_tile
    assert row0 % (seqs * seq_len) == 0 and row0 % (seqs * q_tile) == 0 and n_seq % seqs == 0
    seq0 = row0 // (seqs * seq_len)
    q0 = row0 // (seqs * q_tile)
    in_specs, args = [], []
    if sink is not None:
        in_specs.append(pl.BlockSpec(memory_space=pltpu.SMEM))
        args.append(sink)
    in_specs += [
        pl.BlockSpec((seqs * q_tile, qw), lambda s, n: (q0 + s * n_q + n, 0)),
        pl.BlockSpec((seqs * seq_len, kw), lambda s, n: (seq0 + s, 0)),
        pl.BlockSpec((seqs * seq_len, kw), lambda s, n: (seq0 + s, 0)),
    ]
    args += [q, k, v]
    if ctx is not None:
        p = ctx[0].shape[1]
        in_specs += [pl.BlockSpec((None, p, kw), lambda s, n: (s, 0, 0))] * 2
        args += list(ctx)
    aliases = {}
    if prev is not None:
        in_specs.append(pl.BlockSpec(memory_space=pl.ANY))
        aliases = {len(args): 0}
        args.append(prev)
    kern = functools.partial(
        _attn_kernel, kv_heads=kv_heads, groups=groups, dh=dh, has_sink=sink is not None,
        has_ctx=ctx is not None, window=window, q_tile=q_tile, seq_len=seq_len, chunk=min(seq_len, ATTN_KEY_CHUNK),
        seqs=seqs)
    return pl.pallas_call(
        kern,
        out_shape=jax.ShapeDtypeStruct((t, qw), BF16),
        grid=(n_seq // seqs, n_q),
        in_specs=in_specs,
        out_specs=pl.BlockSpec((seqs * q_tile, qw), lambda s, n: (q0 + s * n_q + n, 0)),
        input_output_aliases=aliases,
        compiler_params=_cparams("arbitrary", "arbitrary"),
        name="attention",
    )(*args)


def _out_route_kernel(*refs, n_mix, n_x, n_ptiles, tiles):
    mix_refs = refs[:n_mix]
    w_refs = refs[n_mix:2 * n_mix]
    x_refs = refs[2 * n_mix:2 * n_mix + n_x]
    x_all = _stream_tile(x_refs, n_ptiles)
    for j in range(tiles):
        rows = slice(j * TOKEN_TILE, (j + 1) * TOKEN_TILE)
        _out_route_tile(j, rows, x_all[rows], [m[rows, :] for m in mix_refs], w_refs, *refs[2 * n_mix + n_x:])


def _out_route_tile(j, rows, x, mixes, w_refs, gate_ref, g2_ref, sh2_ref, sc2_ref, rw_ref, rb_ref,
                    xo_ref, h_ref, pos_ref, post_ref, gates_ref, rows_ref, off_ref):
    acc = None
    for m, w_ref in zip(mixes, w_refs):
        part = _dot(m, w_ref[...])
        acc = part if acc is None else acc + part
    xn = x + gate_ref[...] * acc
    xo_ref[rows, :] = xn
    h = _rms(xn, g2_ref[...]) * (1.0 + sc2_ref[...]) + sh2_ref[...]
    hb = h.astype(BF16)
    h_ref[rows, :] = hb

    logits = _dot_nt(rw_ref[...], hb) + rb_ref[...]
    ne, tm = logits.shape
    expert = lax.broadcasted_iota(I32, (ne, tm), 0).astype(F32)
    work = logits
    sels, vals = [], []
    for _ in range(TOP_K):
        mx = jnp.max(work, axis=0, keepdims=True)
        first = jnp.min(jnp.where(work == mx, expert, float(ne)), axis=0, keepdims=True)
        sel = expert == first
        work = jnp.where(sel, -jnp.inf, work)
        sels.append(sel)
        vals.append(mx)
    exps = [jnp.exp(v - vals[0]) for v in vals]
    den = exps[0] + exps[1] + exps[2] + exps[3]

    onehot = jnp.zeros((ne, tm), F32)
    for sel in sels:
        onehot = onehot + sel.astype(F32)
    r_i = lax.broadcasted_iota(I32, (tm, tm), 0)
    c_i = lax.broadcasted_iota(I32, (tm, tm), 1)
    before = jnp.where(r_i < c_i, 1.0, 0.0).astype(BF16)
    earlier = _dot(onehot.astype(BF16), before)

    cnt = jnp.sum(onehot, axis=1, keepdims=True)
    seg8 = jnp.floor((cnt + (SEG_ALIGN - 1.0)) * (1.0 / SEG_ALIGN))
    e_r = lax.broadcasted_iota(I32, (ne, ne), 0)
    e_c = lax.broadcasted_iota(I32, (ne, ne), 1)
    lower = jnp.where(e_c < e_r, 1.0, 0.0).astype(BF16)
    off8 = _dot(lower, jnp.broadcast_to(seg8, (ne, LANES)).astype(BF16))[:, 0:1]
    seg_off = off8 * SEG_ALIGN
    base = seg_off + earlier

    sub = lax.broadcasted_iota(I32, (2 * TOP_K, tm), 0)
    token_rows = jnp.zeros((2 * TOP_K, tm), F32)
    for k in range(TOP_K):
        pos_k = jnp.sum(jnp.where(sels[k], base, 0.0), axis=0, keepdims=True)
        token_rows = jnp.where(sub == k, pos_k, token_rows)
        token_rows = jnp.where(sub == TOP_K + k, exps[k] / den, token_rows)
    first_lane = lax.broadcasted_iota(I32, (ne, tm), 1) == 0
    block = jnp.concatenate([
        token_rows,
        jnp.where(first_lane, seg8 * SEG_ALIGN, 0.0),
        jnp.where(first_lane, seg_off, 0.0),
        jnp.zeros((LANES - 2 * TOP_K - 2 * ne, tm), F32)], axis=0)
    by_token = block.T
    post_ref[j] = token_rows.astype(I32)
    pos_ref[rows, :] = by_token[:, 0:TOP_K].astype(I32)
    gates_ref[rows, :] = by_token[:, TOP_K:2 * TOP_K]
    rows_ref[j] = by_token[0:1, 2 * TOP_K:2 * TOP_K + ne].astype(I32)
    off_ref[j] = by_token[0:1, 2 * TOP_K + ne:2 * TOP_K + 2 * ne].astype(I32)


def _out_route(geom, layer, mixes, w_parts, x, mods, norm_g, router_w, router_b):
    t, d = geom.t, w_parts[0].shape[1]
    tiles = ROUTE_TILES_PER_STEP
    tm = TOKEN_TILE * tiles
    assert geom.n_ptiles % tiles == 0 and geom.tiles_per_lat % tiles == 0
    row = lambda i: (i, 0)
    const2 = lambda i: (0, 0)
    x_specs, x_args = _stream_specs(geom, x, d, tiles)
    in_specs = [pl.BlockSpec((tm, m.shape[1]), row) for m in mixes]
    in_specs += [pl.BlockSpec(w.shape, const2) for w in w_parts]
    in_specs += x_specs
    in_specs += [
        _mod_spec(geom, layer, 2, d, tiles),
        pl.BlockSpec((1, d), const2),
        _mod_spec(geom, layer, 3, d, tiles),
        _mod_spec(geom, layer, 4, d, tiles),
        pl.BlockSpec(router_w.shape, const2),
        pl.BlockSpec((N_EXPERTS, 1), const2),
    ]
    seg3 = lambda i: (i, 0, 0)
    out_shape = [
        jax.ShapeDtypeStruct((t, d), F32),
        jax.ShapeDtypeStruct((t, d), BF16),
        jax.ShapeDtypeStruct((t, TOP_K), I32),
        jax.ShapeDtypeStruct((geom.n_tiles, 2 * TOP_K, TOKEN_TILE), I32),
        jax.ShapeDtypeStruct((t, TOP_K), F32),
        jax.ShapeDtypeStruct((geom.n_tiles, 1, N_EXPERTS), I32),
        jax.ShapeDtypeStruct((geom.n_tiles, 1, N_EXPERTS), I32),
    ]
    out_specs = [
        pl.BlockSpec((tm, d), row),
        pl.BlockSpec((tm, d), row),
        pl.BlockSpec((tm, TOP_K), row),
        pl.BlockSpec((tiles, 2 * TOP_K, TOKEN_TILE), seg3),
        pl.BlockSpec((tm, TOP_K), row),
        pl.BlockSpec((tiles, 1, N_EXPERTS), seg3),
        pl.BlockSpec((tiles, 1, N_EXPERTS), seg3),
    ]
    return pl.pallas_call(
        functools.partial(_out_route_kernel, n_mix=len(mixes), n_x=len(x_args), n_ptiles=geom.n_ptiles // tiles,
                          tiles=tiles),
        out_shape=out_shape,
        grid=(geom.n_tiles // tiles,),
        in_specs=in_specs,
        out_specs=out_specs,
        compiler_params=_cparams("arbitrary"),
        name="out_route",
    )(*mixes, *w_parts, *x_args, mods, norm_g, mods, mods, router_w, router_b)


def _pack_pairs(v):
    n = v.shape[1] // 2
    bits = lax.bitcast_convert_type(v, U32)
    return (bits[:, :n] & jnp.uint32(0xFFFF0000)) | (bits[:, n:] >> 16)


def _unpack_pairs(p):
    hi = lax.bitcast_convert_type(p & jnp.uint32(0xFFFF0000), F32)
    lo = lax.bitcast_convert_type(p << 16, F32)
    return jnp.concatenate([hi, lo], axis=1).astype(BF16)


def _planned_copies(i, plan_refs, make_copy):
    for size, width, (local_ref, slot_ref, count_ref) in zip(COPY_SIZES, PLAN_WIDTHS, plan_refs):
        count = count_ref[i]

        def start(c, priority, size=size, width=width, local_ref=local_ref, slot_ref=slot_ref):
            a = i * width + c
            make_copy(pl.multiple_of(local_ref[a], SEG_ALIGN), pl.multiple_of(slot_ref[a], SEG_ALIGN),
                      size).start(priority=priority)

        def pair(p, carry, start=start, count=count):
            start(2 * p, 0)

            @pl.when(2 * p + 1 < count)
            def _():
                start(2 * p + 1, 1)

            return carry

        lax.fori_loop(0, (count + 1) // 2, pair, 0)


def _copy_plan(rows, seg_off, seg_dst):
    plan = []
    experts = jnp.arange(N_EXPERTS, dtype=I32)
    for n, (size, width) in enumerate(zip(COPY_SIZES, PLAN_WIDTHS)):
        if n == 0:
            count, done = rows // size, jnp.zeros_like(rows)
        else:
            count, done = (rows % (2 * size)) // size, rows - rows % (2 * size)
        cum = jnp.cumsum(count, axis=1)
        j = jnp.arange(width, dtype=I32)
        owner = jnp.sum((cum[:, None, :] <= j[None, :, None]).astype(I32), axis=2)
        pick = (jnp.minimum(owner, N_EXPERTS - 1)[:, :, None] == experts[None, None, :]).astype(I32)
        take = lambda v: jnp.sum(pick * v[:, None, :], axis=2)
        within = (j[None, :] - take(cum - count)) * size
        plan += [(take(seg_off + done) + within).reshape(-1).astype(I32),
                 (take(seg_dst + done) + within).reshape(-1).astype(I32),
                 cum[:, -1].astype(I32)]
    return plan


def _wait_copies(n_rows, make_copy):
    def wait_big(c, carry):
        make_copy(0, 0, WAIT_CHUNK).wait()
        return carry
    lax.fori_loop(0, n_rows // WAIT_CHUNK, wait_big, 0)
    size = WAIT_CHUNK // 2
    while size >= SEG_ALIGN:
        @pl.when(n_rows % (2 * size) >= size)
        def _(size=size):
            make_copy(0, 0, size).wait()

        size //= 2


def _dispatch_kernel(*refs):
    plan_refs, (tot_ref, h_ref, pos_ref, xs_ref, sorted_ref, sem) = _split_plan(refs)
    i = pl.program_id(0)
    buf = i % 2
    tm = h_ref.shape[0]
    n_sorted = sorted_ref.shape[1]
    pos = pos_ref[0]
    slot = lax.broadcasted_iota(I32, (n_sorted, tm), 0)
    hit = jnp.zeros((n_sorted, tm), F32)
    for k in range(TOP_K):
        hit = jnp.where(pos[k:k + 1, :] == slot, 1.0, hit)
    sorted_ref[buf] = _pack_pairs(_dot(hit.astype(BF16), h_ref[...]))

    def copies_from(b):
        def make_copy(local, slot, rows):
            return pltpu.make_async_copy(sorted_ref.at[b, pl.ds(local, rows)], xs_ref.at[pl.ds(slot, rows)],
                                         sem.at[b])
        return make_copy

    _planned_copies(i, plan_refs, copies_from(buf))

    @pl.when(i > 0)
    def _():
        _wait_copies(tot_ref[jnp.maximum(i - 1, 0)], copies_from(1 - buf))

    @pl.when(i == pl.num_programs(0) - 1)
    def _():
        _wait_copies(tot_ref[i], copies_from(buf))


def _split_plan(refs):
    n = 3 * len(COPY_SIZES)
    return [refs[k:k + 3] for k in range(0, n, 3)], refs[n:]


def _dispatch(h, pos, plan, tile_rows, n_slots):
    t, d = h.shape
    tm = TOKEN_TILE
    return pl.pallas_call(
        _dispatch_kernel,
        out_shape=jax.ShapeDtypeStruct((n_slots, d // 2), U32),
        grid_spec=pltpu.PrefetchScalarGridSpec(
            num_scalar_prefetch=len(plan) + 1,
            grid=(t // tm,),
            in_specs=[
                pl.BlockSpec((tm, d), lambda i, *_: (i, 0)),
                pl.BlockSpec((1, 2 * TOP_K, tm), lambda i, *_: (i, 0, 0)),
            ],
            out_specs=pl.BlockSpec(memory_space=pl.ANY),
            scratch_shapes=[pltpu.VMEM((2, SORTED_ROWS, d // 2), U32), pltpu.SemaphoreType.DMA((2,))],
        ),
        compiler_params=_cparams("arbitrary"),
        name="moe_dispatch",
    )(*plan, tile_rows, h, pos)


def _ffn_kernel(rows_ref, start_ref, wgu_ref, bgu_ref, wd_ref, bd_ref, xs_ref, ys_ref,
                wgu_bf, wd_bf, xbuf, ybuf, sem_in, sem_out):
    e = pl.program_id(0)
    d_ff = wd_ref.shape[1]
    tb = xbuf.shape[1]
    n_rows = rows_ref[e]
    n_tiles = (n_rows + tb - 1) // tb
    base = start_ref[e]

    chunk = 128
    def cast_gu(c, carry):
        r = pl.multiple_of(c * chunk, chunk)
        wgu_bf[pl.ds(r, chunk), :] = wgu_ref[0, pl.ds(r, chunk), :].astype(BF16)
        return carry
    lax.fori_loop(0, wgu_ref.shape[1] // chunk, cast_gu, 0)
    def cast_d(c, carry):
        r = pl.multiple_of(c * chunk, chunk)
        wd_bf[pl.ds(r, chunk), :] = wd_ref[0, pl.ds(r, chunk), :].astype(BF16)
        return carry
    lax.fori_loop(0, d_ff // chunk, cast_d, 0)

    half = tb // 2
    last_small = (n_rows - (n_tiles - 1) * tb) <= half

    def x_copy(s, slot):
        r = pl.multiple_of(base + s * tb, EXPERT_TILE)
        return pltpu.make_async_copy(xs_ref.at[pl.ds(r, tb)], xbuf.at[slot], sem_in.at[slot])

    def y_copy(s, slot, rows=tb):
        r = pl.multiple_of(base + s * tb, EXPERT_TILE)
        return pltpu.make_async_copy(ybuf.at[slot, pl.ds(0, rows)], ys_ref.at[pl.ds(r, rows)], sem_out.at[slot])

    def ffn_rows(s, slot, n):
        rows = s * tb + lax.broadcasted_iota(I32, (n, 1), 0)
        x = _unpack_pairs(jnp.where(rows < n_rows, xbuf[slot, pl.ds(0, n), :], jnp.uint32(0)))
        gu = _dot(x, wgu_bf[...]) + bgu_ref[0]
        gate = jnp.minimum(gu[:, :d_ff], SWIGLU_LIMIT)
        up = jnp.clip(gu[:, d_ff:], -SWIGLU_LIMIT, SWIGLU_LIMIT)
        act = (up + 1.0) * (gate * (1.0 / (1.0 + jnp.exp(-SWIGLU_ALPHA * gate))))
        y = _dot(act.astype(BF16), wd_bf[...]) + bd_ref[0]
        ybuf[slot, pl.ds(0, n), :] = _pack_pairs(y.astype(BF16).astype(F32))
        y_copy(s, slot, n).start()

    @pl.when(n_tiles > 0)
    def _():
        x_copy(0, 0).start()

    def tile(s, carry):
        slot = s % 2
        x_copy(s, slot).wait()

        @pl.when(s + 1 < n_tiles)
        def _():
            x_copy(s + 1, 1 - slot).start()

        @pl.when(s >= 2)
        def _():
            y_copy(s - 2, slot).wait()

        small = jnp.logical_and(s == n_tiles - 1, last_small)

        @pl.when(jnp.logical_not(small))
        def _():
            ffn_rows(s, slot, tb)

        @pl.when(small)
        def _():
            ffn_rows(s, slot, half)

        return carry

    lax.fori_loop(0, n_tiles, tile, 0)

    @pl.when(n_tiles >= 2)
    def _():
        y_copy(n_tiles - 2, n_tiles % 2).wait()

    @pl.when(jnp.logical_and(n_tiles >= 1, jnp.logical_not(last_small)))
    def _():
        y_copy(n_tiles - 1, (n_tiles - 1) % 2).wait()

    @pl.when(jnp.logical_and(n_tiles >= 1, last_small))
    def _():
        y_copy(n_tiles - 1, (n_tiles - 1) % 2, half).wait()


def _expert_ffn(layer, xs, expert_rows, expert_start, w_gu, b_gu, w_down, b_down):
    n_slots, packed_w = xs.shape
    tb = FFN_TILE
    depth, ne, d, two_f = w_gu.shape
    d_ff = two_f // 2
    exp4 = lambda e, *_: (layer, e, 0, 0)
    return pl.pallas_call(
        _ffn_kernel,
        out_shape=jax.ShapeDtypeStruct((n_slots, packed_w), U32),
        grid_spec=pltpu.PrefetchScalarGridSpec(
            num_scalar_prefetch=2,
            grid=(ne,),
            in_specs=[
                pl.BlockSpec((None, 1, d, two_f), exp4),
                pl.BlockSpec((None, 1, 1, two_f), exp4),
                pl.BlockSpec((None, 1, d_ff, d), exp4),
                pl.BlockSpec((None, 1, 1, d), exp4),
                pl.BlockSpec(memory_space=pl.ANY),
            ],
            out_specs=pl.BlockSpec(memory_space=pl.ANY),
            scratch_shapes=[
                pltpu.VMEM((d, two_f), BF16), pltpu.VMEM((d_ff, d), BF16),
                pltpu.VMEM((2, tb, packed_w), U32), pltpu.VMEM((2, tb, packed_w), U32),
                pltpu.SemaphoreType.DMA((2,)), pltpu.SemaphoreType.DMA((2,)),
            ],
        ),
        compiler_params=_cparams("arbitrary"),
        name="expert_ffn",
    )(expert_rows, expert_start, w_gu, b_gu.reshape(depth, ne, 1, two_f),
      w_down, b_down.reshape(depth, ne, 1, d), xs)


def _combine_kernel(*refs, final, n_ptiles):
    plan_refs, (tot_ref, x_ref, pos_ref, gates_ref, mg_ref, *rest) = _split_plan(refs)
    if final:
        fn_ref, ys_ref, op_ref, os_ref, buf, sem = rest
    else:
        ys_ref, o_ref, buf, sem = rest
    i = pl.program_id(0)
    cur = i % 2
    tm = x_ref.shape[0]
    n_sorted = buf.shape[1]

    def copies_into(b):
        def make_copy(local, slot, rows):
            return pltpu.make_async_copy(ys_ref.at[pl.ds(slot, rows)], buf.at[b, pl.ds(local, rows)], sem.at[b])
        return make_copy

    @pl.when(i == 0)
    def _():
        buf[...] = jnp.zeros_like(buf)
        _planned_copies(i, plan_refs, copies_into(cur))

    @pl.when(i + 1 < pl.num_programs(0))
    def _():
        _planned_copies(i + 1, plan_refs, copies_into(1 - cur))

    pos = pos_ref[...]
    g = gates_ref[...]
    lane = lax.broadcasted_iota(I32, (tm, n_sorted), 1)
    weight = jnp.zeros((tm, n_sorted), F32)
    for k in range(TOP_K):
        weight = jnp.where(pos[:, k:k + 1] == lane, g[:, k:k + 1], weight)
    _wait_copies(tot_ref[i], copies_into(cur))
    y = _dot(weight.astype(BF16), _unpack_pairs(buf[cur]))
    xn = x_ref[...] + mg_ref[...] * y
    if final:
        xn = _rms(xn, fn_ref[...])

        @pl.when(i < n_ptiles)
        def _():
            op_ref[...] = xn

        @pl.when(i >= n_ptiles)
        def _():
            os_ref[...] = xn
    else:
        o_ref[...] = xn


def _combine(geom, layer, ys, pos, plan, tile_rows, x, gates, mods, final_g):
    t, d = x.shape
    tm = TOKEN_TILE
    final = final_g is not None

    def mod_imap(i, *_):
        return ((layer * MOD_ROWS + geom.group(i)) * N_MOD + 5, 0, 0)

    row = lambda i, *_: (i, 0)
    in_specs = [
        pl.BlockSpec((tm, d), row),
        pl.BlockSpec((tm, TOP_K), row),
        pl.BlockSpec((tm, TOP_K), row),
        pl.BlockSpec((None, 1, d), mod_imap),
    ]
    args = [*plan, tile_rows, x, pos, gates, mods]
    if final:
        in_specs.append(pl.BlockSpec((1, d), lambda i, *_: (0, 0)))
        args.append(final_g)
    in_specs.append(pl.BlockSpec(memory_space=pl.ANY))
    args.append(ys)
    if final:
        n_pt = geom.n_ptiles
        out_shape = [jax.ShapeDtypeStruct((geom.tp, d), F32), jax.ShapeDtypeStruct((t - geom.tp, d), F32)]
        out_specs = [pl.BlockSpec((tm, d), lambda i, *_: (jnp.minimum(i, n_pt - 1), 0)),
                     pl.BlockSpec((tm, d), lambda i, *_: (jnp.maximum(i - n_pt, 0), 0))]
    else:
        out_shape = jax.ShapeDtypeStruct((t, d), F32)
        out_specs = pl.BlockSpec((tm, d), row)
    return pl.pallas_call(
        functools.partial(_combine_kernel, final=final, n_ptiles=geom.n_ptiles),
        out_shape=out_shape,
        grid_spec=pltpu.PrefetchScalarGridSpec(
            num_scalar_prefetch=len(plan) + 1,
            grid=(t // tm,),
            in_specs=in_specs,
            out_specs=out_specs,
            scratch_shapes=[pltpu.VMEM((2, SORTED_ROWS, d // 2), U32), pltpu.SemaphoreType.DMA((2,))],
        ),
        compiler_params=_cparams("arbitrary"),
        name="moe_combine",
    )(*args)


def _moe(geom, layer, h, pos, pos_t, gates, seg_rows, seg_off, x, mods, w_gu, b_gu, w_down, b_down, final_g):
    t = h.shape[0]
    tb = EXPERT_TILE
    n_tok_tiles = seg_rows.shape[0]
    max_rows = t * TOP_K + n_tok_tiles * N_EXPERTS * (SEG_ALIGN - 1)
    n_blocks = -(-max_rows // tb) + N_EXPERTS
    rows = seg_rows[:, 0, :]
    cnt = jnp.sum(rows, axis=0)
    n_tiles_e = (cnt + tb - 1) // tb
    tile_end = jnp.cumsum(n_tiles_e)
    tile_start = tile_end - n_tiles_e
    expert_start = (tile_start * tb).astype(I32)
    seg_dst = expert_start[None, :] + jnp.cumsum(rows, axis=0) - rows
    plan = _copy_plan(rows, seg_off[:, 0, :], seg_dst)
    tile_rows = jnp.sum(rows, axis=1).astype(I32)
    xs = _dispatch(h, pos_t, plan, tile_rows, n_blocks * tb + FFN_TILE - tb)
    ys = _expert_ffn(layer, xs, cnt.astype(I32), expert_start, w_gu, b_gu, w_down, b_down)
    return _combine(geom, layer, ys, pos, plan, tile_rows, x, gates, mods, final_g)


def kernel(x_prompt, x_sample, cache_b_k, cache_b_v, cache_c_k, cache_c_v, c, c_ctx,
           mod_w, mod_b, norm_mix, norm_ffn, even_w_in, even_w_out, even_sink,
           odd_w_in, odd_w_out, odd_q_norm, odd_k_norm, router_w, router_b,
           moe_w_gu, moe_b_gu, moe_w_down, moe_b_down, final_norm):
    bp, lp, d = x_prompt.shape
    bs, ls, _ = x_sample.shape
    past = cache_b_k.shape[2]
    depth = mod_w.shape[0]
    geom = _Geom(bp, lp, bs, ls)
    tp = geom.tp

    x = (x_prompt.reshape(tp, d), x_sample.reshape(bs * ls, d))
    cond = jnp.concatenate([c_ctx[None, :], c, jnp.zeros((MOD_ROWS - 1 - bs, d), F32)], axis=0)
    mods = _modulation(cond, mod_w, mod_b).reshape(depth * MOD_ROWS * N_MOD, 1, d)

    cn, sn = _dft_tables(A_GROUP_DIM)
    dft_chan = jnp.asarray(np.concatenate([cn, sn], axis=1), BF16)
    dft_p = [jnp.asarray(m, BF16) for m in _dft_tables(lp)]
    dft_s = [jnp.asarray(m, BF16) for m in _dft_tables(ls)]
    rope_b = [jnp.asarray(m) for m in _rope_tables(TOKEN_TILE, ls, B_HEAD_DIM)]
    rope_c = [jnp.asarray(m) for m in _rope_tables(TOKEN_TILE, ls, C_HEAD_DIM)]

    states = {"bk": [], "bv": [], "ck": [], "cv": []}
    for layer in range(depth):
        j = layer // 2
        g_mix = norm_mix[layer][None, :]
        g_ffn = norm_ffn[layer][None, :]
        if layer % 2 == 0:
            pair = _paired_head_order(B_KV_HEADS, B_HEADS // B_KV_HEADS, B_HEAD_DIM)
            w_in = even_w_in[j]
            w_in = jnp.concatenate([w_in[:, :A_WIDTH], w_in[:, A_WIDTH:A_WIDTH + B_Q_WIDTH][:, pair],
                                    w_in[:, A_WIDTH + B_Q_WIDTH:]], axis=1).astype(BF16)
            tc, ts, q, k, v, k_state, v_state = _in_projection(
                geom, layer, x, g_mix, mods, w_in, rope_b[0], rope_b[1], [dft_chan],
                _proj_even_kernel, (A_WIDTH, A_WIDTH, B_Q_WIDTH, B_KV_WIDTH, B_KV_WIDTH), B_KV_WIDTH, "proj_even")
            states["bk"].append(k_state.reshape(bp, lp, B_KV_HEADS, B_HEAD_DIM))
            states["bv"].append(v_state.reshape(bp, lp, B_KV_HEADS, B_HEAD_DIM))
            four = _fourier_tokens(tc, ts, dft_p[0], dft_p[1], bp, lp, 0, None)
            four = _fourier_tokens(tc, ts, dft_s[0], dft_s[1], bs, ls, tp, four)
            sink = even_sink[j]
            common = dict(kv_heads=B_KV_HEADS, groups=B_HEADS // B_KV_HEADS, dh=B_HEAD_DIM)
            att = _attention(q, k, v, None, sink, None, n_seq=bp, seq_len=lp, row0=0, q_tile=lp,
                             window=None, seqs=PROMPT_SEQS_PER_STEP, **common)
            ctx = (cache_b_k[:, j].reshape(bs, past, B_KV_WIDTH).astype(BF16),
                   cache_b_v[:, j].reshape(bs, past, B_KV_WIDTH).astype(BF16))
            att = _attention(q, k, v, ctx, sink, att, n_seq=bs, seq_len=ls, row0=tp, q_tile=ATTN_Q_TILE,
                             window=WINDOW, **common)
            w_out = even_w_out[j].astype(BF16)
            mixes = [four, att]
            w_parts = [w_out[:A_WIDTH], w_out[A_WIDTH:][pair]]
        else:
            q, k, v, k_state, v_state = _in_projection(
                geom, layer, x, g_mix, mods, odd_w_in[j].astype(BF16), rope_c[0], rope_c[1],
                [odd_q_norm[j][None, :], odd_k_norm[j][None, :]],
                _proj_odd_kernel, (C_Q_WIDTH, C_KV_WIDTH, C_KV_WIDTH), C_KV_WIDTH, "proj_odd")
            states["ck"].append(k_state.reshape(bp, lp, C_KV_HEADS, C_HEAD_DIM))
            states["cv"].append(v_state.reshape(bp, lp, C_KV_HEADS, C_HEAD_DIM))
            common = dict(kv_heads=C_KV_HEADS, groups=C_HEADS // C_KV_HEADS, dh=C_HEAD_DIM, window=None)
            att = _attention(q, k, v, None, None, None, n_seq=bp, seq_len=lp, row0=0, q_tile=lp,
                             seqs=PROMPT_SEQS_PER_STEP, **common)
            ctx = (cache_c_k[:, j].reshape(bs, past, C_KV_WIDTH).astype(BF16),
                   cache_c_v[:, j].reshape(bs, past, C_KV_WIDTH).astype(BF16))
            att = _attention(q, k, v, ctx, None, att, n_seq=bs, seq_len=ls, row0=tp, q_tile=DENSE_Q_TILE, **common)
            mixes = [att]
            w_parts = [odd_w_out[j].astype(BF16)]
        x, h, pos, pos_t, gates, seg_rows, seg_off = _out_route(
            geom, layer, mixes, w_parts, x, mods, g_ffn, router_w[layer].T.astype(BF16), router_b[layer][:, None])
        final_g = final_norm[None, :] if layer == depth - 1 else None
        x = _moe(geom, layer, h, pos, pos_t, gates, seg_rows, seg_off, x, mods,
                 moe_w_gu, moe_b_gu, moe_w_down, moe_b_down, final_g)

    y_prompt = x[0].reshape(bp, lp, d)
    y_sample = x[1].reshape(bs, ls, d)
    return (y_prompt, y_sample,
            jnp.stack(states["bk"], axis=1), jnp.stack(states["bv"], axis=1),
            jnp.stack(states["ck"], axis=1), jnp.stack(states["cv"], axis=1))
```
